```python
import math
import jax
import jax.numpy as jnp
from jax import lax
import numpy as np

D_MODEL = 1024
BATCH = 16
SEQ = 256
DEPTH = 2
DEC_BATCH = 4
DEC_SEQ = 1024
PAST_LEN = 512

GRID_W = 64
HEAD_DIM = 64
ATT_HEADS = 8
ATT_KV_HEADS = 2
ATT_WIDTH = ATT_HEADS * HEAD_DIM
KV_WIDTH = ATT_KV_HEADS * HEAD_DIM
RW_HEADS = 4
RW_WIDTH = RW_HEADS * HEAD_DIM
RW_DECAY_LORA = 64
RW_AAA_LORA = 64
RW_GATE_LORA = 128
RW_GN_EPS = 64e-5
HG_HEADS = 4
HG_WIDTH = HG_HEADS * HEAD_DIM
HG_CHUNK = 32
HG_F_MIN = 1e-6
MIX_WIDTH = ATT_WIDTH + RW_WIDTH + HG_WIDTH
IN_SIZES = (ATT_WIDTH, KV_WIDTH, KV_WIDTH, RW_WIDTH, RW_WIDTH, RW_WIDTH, 2 * RW_DECAY_LORA, 2 * RW_AAA_LORA, RW_GATE_LORA, HG_WIDTH, 2 * HG_WIDTH, HG_WIDTH, HG_WIDTH)
IN_WIDTH = ATT_WIDTH + 2 * KV_WIDTH + 3 * RW_WIDTH + 2 * RW_DECAY_LORA + 2 * RW_AAA_LORA + RW_GATE_LORA + 5 * HG_WIDTH
N_EXPERTS = 32
TOP_K = 4
EXPERT_FF = D_MODEL
SWIGLU_LIMIT = 7.0
SWIGLU_ALPHA = 1.702
MOE_BLOCK = 128
Q_BLOCK = 128
ROPE_THETA = 10000.0
NORM_EPS = 1e-6

kernel_name = 'hybrid_diffusion_attn_rwkv7_hgrn2_moe_step'


def rms_norm(x, g):
    xf = x.astype(jnp.float32)
    y = xf * lax.rsqrt(jnp.mean(xf * xf, axis=-1, keepdims=True) + NORM_EPS)
    return (y * g.astype(jnp.float32)).astype(x.dtype)


def axial_rope(n_tok):
    rows = n_tok // GRID_W
    row = jnp.repeat(jnp.arange(rows, dtype=jnp.float32), GRID_W)
    col = jnp.tile(jnp.arange(GRID_W, dtype=jnp.float32), rows)
    n_freq = HEAD_DIM // 4
    inv = ROPE_THETA ** (-jnp.arange(n_freq, dtype=jnp.float32) / n_freq)
    ang = jnp.concatenate([row[:, None] * inv, col[:, None] * inv], axis=-1)
    return jnp.cos(ang), jnp.sin(ang)


def apply_rope(x, cos, sin):
    xf = x.astype(jnp.float32).reshape(x.shape[:-1] + (HEAD_DIM // 2, 2))
    x1, x2 = xf[..., 0], xf[..., 1]
    out = jnp.stack([x1 * cos - x2 * sin, x1 * sin + x2 * cos], axis=-1)
    return out.reshape(x.shape).astype(x.dtype)


def block_attention(q, k, v):
    b, hq, tq, d = q.shape
    hkv = k.shape[1]
    rep = hq // hkv
    nblk = tq // Q_BLOCK
    qb = q.reshape(b, hkv, rep, nblk, Q_BLOCK, d).transpose(3, 0, 1, 2, 4, 5)
    scale = 1.0 / math.sqrt(d)

    def one_block(qblk):
        s = jnp.einsum('bgrqd,bgkd->bgrqk', qblk, k, preferred_element_type=jnp.float32) * scale
        pr = jax.nn.softmax(s, axis=-1)
        return jnp.einsum('bgrqk,bgkd->bgrqd', pr.astype(v.dtype), v)

    o = lax.map(one_block, qb)
    return o.transpose(1, 2, 3, 0, 4, 5).reshape(b, hq, tq, d)


def rwkv7_scan(r, w, k, v, a_, b_, s0):
    xs = tuple(jnp.moveaxis(t, 1, 0) for t in (r, w, k, v, a_, b_))

    def step(S, inp):
        rt, wt, kt, vt, at, bt = inp
        sa = jnp.einsum('bhvk,bhk->bhv', S, at)
        S = S * wt[:, :, None, :] + sa[..., None] * bt[:, :, None, :] + vt[..., None] * kt[:, :, None, :]
        return S, jnp.einsum('bhvk,bhk->bhv', S, rt)

    s_last, ys = lax.scan(step, s0, xs)
    return jnp.moveaxis(ys, 0, 1), s_last


def head_group_norm(y, g, b):
    mu = jnp.mean(y, axis=-1, keepdims=True)
    yc = y - mu
    var = jnp.mean(yc * yc, axis=-1, keepdims=True)
    yn = (yc * lax.rsqrt(var + RW_GN_EPS)).reshape(y.shape[0], y.shape[1], -1)
    return yn * g.astype(jnp.float32) + b.astype(jnp.float32)


def rwkv7_mixer(r, k, v, wd, ad, gd, p, s0):
    B, T, _ = r.shape
    f32 = jnp.float32
    heads = lambda t: t.astype(f32).reshape(B, T, RW_HEADS, HEAD_DIM)
    r, k, v = heads(r), heads(k), heads(v)
    wd = wd.astype(f32).reshape(B, T, 2, RW_DECAY_LORA)
    ad = ad.astype(f32).reshape(B, T, 2, RW_AAA_LORA)
    kk = k * p['rw_kk'].astype(f32).reshape(RW_HEADS, HEAD_DIM)
    kk = kk * lax.rsqrt(jnp.sum(kk * kk, axis=-1, keepdims=True) + 1e-12)
    g = jax.nn.sigmoid(gd.astype(f32)) @ p['rw_g2'].astype(f32)
    ka = p['rw_ka'].astype(f32).reshape(RW_HEADS, HEAD_DIM)
    rk = p['rw_rk'].astype(f32)
    ys, bonuses, states = [], [], []
    for d in range(2):
        w_raw = p['rw_w0'][d].astype(f32) + jnp.tanh(wd[:, :, d]) @ p['rw_w2'][d].astype(f32)
        decay = heads(jnp.exp(-jnp.exp(-jax.nn.softplus(-w_raw) - 0.5)))
        a = heads(jax.nn.sigmoid(p['rw_a0'][d].astype(f32) + ad[:, :, d] @ p['rw_a2'][d].astype(f32)))
        kd = k * (1.0 + (a - 1.0) * ka)
        seq = (r, decay, kd, v, -kk, kk * a)
        if d == 1:
            seq = tuple(jnp.flip(t, axis=1) for t in seq)
        y, s_last = rwkv7_scan(*seq, s0[:, d].astype(f32))
        if d == 1:
            y = jnp.flip(y, axis=1)
        ys.append(y)
        bonuses.append(jnp.sum(r * kd * rk, axis=-1, keepdims=True) * v)
        states.append(s_last)
    y = head_group_norm(ys[0] + ys[1], p['rw_gn_g'], p['rw_gn_b']) + (bonuses[0] + bonuses[1]).reshape(B, T, RW_WIDTH)
    return y * g, jnp.stack(states, axis=1)


def chunk_gla(q, k, v, logf, s0):
    B, H, T, dk = q.shape
    dv = v.shape[-1]
    n = T // HG_CHUNK
    rs = lambda t: t.reshape(B, H, n, HG_CHUNK, t.shape[-1])
    q, k, v, logf = rs(q), rs(k), rs(v), rs(logf)
    b = jnp.cumsum(logf, axis=3)
    causal = jnp.tril(jnp.ones((HG_CHUNK, HG_CHUNK), dtype=bool))
    diff = b[:, :, :, :, None, :] - b[:, :, :, None, :, :]
    dec = jnp.where(causal[:, :, None], jnp.exp(jnp.minimum(diff, 0.0)), 0.0)
    att = jnp.einsum('bhntd,bhnsd,bhntsd->bhnts', q, k, dec)
    o_intra = jnp.einsum('bhnts,bhnsv->bhntv', att, v)
    b_last = b[:, :, :, -1, :]
    kv = jnp.einsum('bhnsd,bhnsv->bhndv', k * jnp.exp(b_last[:, :, :, None, :] - b), v)

    def step(S, inp):
        dl, kvn = inp
        return dl[..., None] * S + kvn, S

    s_last, s_before = lax.scan(step, s0, (jnp.moveaxis(jnp.exp(b_last), 2, 0), jnp.moveaxis(kv, 2, 0)))
    s_before = jnp.moveaxis(s_before, 0, 2)
    o_inter = jnp.einsum('bhntd,bhndv->bhntv', q * jnp.exp(b), s_before)
    return (o_intra + o_inter).reshape(B, H, T, dv), s_last


def hgrn2_mixer(q, f_raw, i, g, lb, norm_g, s0):
    B, T, _ = q.shape
    f32 = jnp.float32
    heads = lambda t: t.astype(f32).reshape(B, T, HG_HEADS, HEAD_DIM).transpose(0, 2, 1, 3)
    qh = heads(jax.nn.silu(q.astype(f32)))
    vh = heads(i)
    f_raw = f_raw.astype(f32).reshape(B, T, 2, HG_WIDTH)
    outs, states = [], []
    for d in range(2):
        lbd = lb[d]
        f = lbd + (1.0 - lbd) * jax.nn.sigmoid(f_raw[:, :, d])
        logf = jnp.log(jnp.maximum(f, HG_F_MIN))
        kf = 1.0 - f
        seq = (qh, heads(kf), vh, heads(logf))
        if d == 1:
            seq = tuple(jnp.flip(t, axis=2) for t in seq)
        o, s_last = chunk_gla(*seq, s0[:, d].astype(f32))
        if d == 1:
            o = jnp.flip(o, axis=2)
        outs.append(o)
        states.append(s_last)
    o = rms_norm(outs[0] + outs[1], norm_g).transpose(0, 2, 1, 3).reshape(B, T, HG_WIDTH)
    return o * jax.nn.silu(g.astype(f32)), jnp.stack(states, axis=1)


def moe(h, p):
    T, D = h.shape
    logits = (h @ p['router_w']).astype(jnp.float32) + p['router_b'].astype(jnp.float32)
    top_logit, top_idx = lax.top_k(logits, TOP_K)
    gates = jax.nn.softmax(top_logit, axis=-1)
    A = T * TOP_K
    flat_e = top_idx.reshape(A)
    flat_tok = jnp.arange(A, dtype=jnp.int32) // TOP_K
    flat_g = gates.reshape(A)
    order = jnp.argsort(flat_e)
    se = flat_e[order]
    counts = jnp.bincount(flat_e, length=N_EXPERTS)
    starts = jnp.cumsum(counts) - counts
    padded = (counts + MOE_BLOCK - 1) // MOE_BLOCK * MOE_BLOCK
    pends = jnp.cumsum(padded)
    pstarts = pends - padded
    dest = pstarts[se] + jnp.arange(A, dtype=jnp.int32) - starts[se]
    n_blocks = -(-(A + N_EXPERTS * (MOE_BLOCK - 1)) // MOE_BLOCK)
    n_slots = n_blocks * MOE_BLOCK
    slot_tok = jnp.full((n_slots,), T, dtype=jnp.int32).at[dest].set(flat_tok[order])
    slot_g = jnp.zeros((n_slots,), dtype=jnp.float32).at[dest].set(flat_g[order])
    block_e = jnp.clip(jnp.searchsorted(pends, jnp.arange(n_blocks, dtype=jnp.int32) * MOE_BLOCK, side='right'), 0, N_EXPERTS - 1)
    h_pad = jnp.concatenate([h, jnp.zeros((1, D), h.dtype)], axis=0)
    xb = h_pad[slot_tok].reshape(n_blocks, MOE_BLOCK, D)

    def expert_block(args):
        xblk, e = args
        gu = xblk @ p['moe_w_gu'][e] + p['moe_b_gu'][e]
        glu, lin = gu[:, :EXPERT_FF], gu[:, EXPERT_FF:]
        glu = jnp.minimum(glu, SWIGLU_LIMIT)
        lin = jnp.clip(lin, -SWIGLU_LIMIT, SWIGLU_LIMIT)
        act = glu * jax.nn.sigmoid(SWIGLU_ALPHA * glu) * (lin + 1.0)
        return act @ p['moe_w_down'][e] + p['moe_b_down'][e]

    yb = lax.map(expert_block, (xb, block_e)).reshape(n_slots, D)
    y = jax.ops.segment_sum(yb * slot_g[:, None].astype(yb.dtype), slot_tok, num_segments=T + 1)[:T]
    return y.astype(h.dtype)


def token_mixers(h, p, lb, cache):
    B, T, _ = h.shape
    proj = h @ p['w_in']
    split_points = np.cumsum(IN_SIZES)[:-1].tolist()
    (aq, ak, av, rr, rk, rv, rwd, rad, rgd, hq, hf, hi, hg) = jnp.split(proj, split_points, axis=-1)
    qh = rms_norm(aq.reshape(B, T, ATT_HEADS, HEAD_DIM), p['att_qnorm_g']).transpose(0, 2, 1, 3)
    kh = rms_norm(ak.reshape(B, T, ATT_KV_HEADS, HEAD_DIM), p['att_knorm_g']).transpose(0, 2, 1, 3)
    vh = av.reshape(B, T, ATT_KV_HEADS, HEAD_DIM).transpose(0, 2, 1, 3)
    if cache is None:
        att = block_attention(qh, kh, vh)
        rw_s0 = jnp.zeros((B, 2, RW_HEADS, HEAD_DIM, HEAD_DIM), jnp.float32)
        hg_s0 = jnp.zeros((B, 2, HG_HEADS, HEAD_DIM, HEAD_DIM), jnp.float32)
    else:
        ck, cv, rw_s0, hg_s0 = cache
        cos, sin = axial_rope(T)
        qr = apply_rope(qh, cos, sin)
        kr = apply_rope(kh, cos, sin)
        att = block_attention(qr, jnp.concatenate([ck.astype(kr.dtype), kr], axis=2), jnp.concatenate([cv.astype(vh.dtype), vh], axis=2))
    att = att.transpose(0, 2, 1, 3).reshape(B, T, ATT_WIDTH)
    rw_out, rw_state = rwkv7_mixer(rr, rk, rv, rwd, rad, rgd, p, rw_s0)
    hg_out, hg_state = hgrn2_mixer(hq, hf, hi, hg, lb, p['hg_norm_g'], hg_s0)
    mix = jnp.concatenate([att.astype(h.dtype), rw_out.astype(h.dtype), hg_out.astype(h.dtype)], axis=-1)
    new_cache = (kh, vh, rw_state, hg_state) if cache is None else None
    return mix, new_cache


def layer(x, mod, p, lb, cache):
    shift1, scale1, gate1, shift2, scale2, gate2 = jnp.split(mod[:, None, :], 6, axis=-1)
    h = rms_norm(x, p['norm_mix_g']) * (1.0 + scale1) + shift1
    mix, new_cache = token_mixers(h, p, lb, cache)
    x = x + gate1 * (mix @ p['w_out'])
    h = rms_norm(x, p['norm_ffn_g']) * (1.0 + scale2) + shift2
    B, T, D = h.shape
    x = x + gate2 * moe(h.reshape(B * T, D), p).reshape(B, T, D)
    return x, new_cache


def hgrn_lower_bounds(hg_lb):
    sm = jax.nn.softmax(hg_lb.astype(jnp.float32), axis=0)
    return jnp.cumsum(sm, axis=0) - sm[0:1]


def setup_inputs(seed: int = 0) -> dict:
    key = jax.random.key(seed)
    kit = iter(jax.random.split(key, 40))
    nrm = lambda shape, scale: scale * jax.random.normal(next(kit), shape, jnp.float32)
    gain = lambda shape: 1.0 + nrm(shape, 0.02)
    D = D_MODEL
    return {
        'x_prompt': nrm((BATCH, SEQ, D), 1.0),
        'x_sample': nrm((DEC_BATCH, DEC_SEQ, D), 1.0),
        'cache_att_k': nrm((DEC_BATCH, DEPTH, ATT_KV_HEADS, PAST_LEN, HEAD_DIM), 1.0),
        'cache_att_v': nrm((DEC_BATCH, DEPTH, ATT_KV_HEADS, PAST_LEN, HEAD_DIM), 1.0),
        'state_rwkv': nrm((DEC_BATCH, DEPTH, 2, RW_HEADS, HEAD_DIM, HEAD_DIM), 0.5),
        'state_hgrn': nrm((DEC_BATCH, DEPTH, 2, HG_HEADS, HEAD_DIM, HEAD_DIM), 0.5),
        'c': nrm((DEC_BATCH, D), 1.0),
        'c_ctx': nrm((D,), 1.0),
        'w_mod': nrm((DEPTH, D, 6 * D), 0.5 * D ** -0.5),
        'b_mod': nrm((DEPTH, 6 * D), 0.02),
        'norm_mix_g': gain((DEPTH, D)),
        'norm_ffn_g': gain((DEPTH, D)),
        'w_in': nrm((DEPTH, D, IN_WIDTH), D ** -0.5),
        'w_out': nrm((DEPTH, MIX_WIDTH, D), MIX_WIDTH ** -0.5),
        'att_qnorm_g': gain((DEPTH, HEAD_DIM)),
        'att_knorm_g': gain((DEPTH, HEAD_DIM)),
        'rw_w0': nrm((DEPTH, 2, RW_WIDTH), 0.5),
        'rw_w2': nrm((DEPTH, 2, RW_DECAY_LORA, RW_WIDTH), 0.5 * RW_DECAY_LORA ** -0.5),
        'rw_a0': nrm((DEPTH, 2, RW_WIDTH), 0.5),
        'rw_a2': nrm((DEPTH, 2, RW_AAA_LORA, RW_WIDTH), 0.5 * RW_AAA_LORA ** -0.5),
        'rw_g2': nrm((DEPTH, RW_GATE_LORA, RW_WIDTH), RW_GATE_LORA ** -0.5),
        'rw_kk': 0.85 + nrm((DEPTH, RW_WIDTH), 0.05),
        'rw_ka': 1.0 + nrm((DEPTH, RW_WIDTH), 0.05),
        'rw_rk': nrm((DEPTH, RW_HEADS, HEAD_DIM), 0.1),
        'rw_gn_g': gain((DEPTH, RW_WIDTH)),
        'rw_gn_b': nrm((DEPTH, RW_WIDTH), 0.02),
        'hg_lb': nrm((DEPTH, 2, HG_WIDTH), 1.0),
        'hg_norm_g': gain((DEPTH, HEAD_DIM)),
        'router_w': nrm((DEPTH, D, N_EXPERTS), D ** -0.5),
        'router_b': nrm((DEPTH, N_EXPERTS), 0.01),
        'moe_w_gu': nrm((DEPTH, N_EXPERTS, D, 2 * EXPERT_FF), D ** -0.5),
        'moe_b_gu': nrm((DEPTH, N_EXPERTS, 2 * EXPERT_FF), 0.02),
        'moe_w_down': nrm((DEPTH, N_EXPERTS, EXPERT_FF, D), EXPERT_FF ** -0.5),
        'moe_b_down': nrm((DEPTH, N_EXPERTS, D), 0.02),
        'final_norm_g': gain((D,)),
    }


def reference(x_prompt, x_sample, cache_att_k, cache_att_v, state_rwkv, state_hgrn, c, c_ctx,
              w_mod, b_mod, norm_mix_g, norm_ffn_g, w_in, w_out, att_qnorm_g, att_knorm_g,
              rw_w0, rw_w2, rw_a0, rw_a2, rw_g2, rw_kk, rw_ka, rw_rk, rw_gn_g, rw_gn_b,
              hg_lb, hg_norm_g, router_w, router_b, moe_w_gu, moe_b_gu, moe_w_down, moe_b_down,
              final_norm_g):
    lb_all = hgrn_lower_bounds(hg_lb)
    silu_ctx = jax.nn.silu(c_ctx)[None, :]
    silu_lat = jax.nn.silu(c)
    xp, xs = x_prompt, x_sample
    ks, vs, srs, shs = [], [], [], []
    for l in range(DEPTH):
        p = dict(norm_mix_g=norm_mix_g[l], norm_ffn_g=norm_ffn_g[l], w_in=w_in[l], w_out=w_out[l],
                 att_qnorm_g=att_qnorm_g[l], att_knorm_g=att_knorm_g[l],
                 rw_w0=rw_w0[l], rw_w2=rw_w2[l], rw_a0=rw_a0[l], rw_a2=rw_a2[l], rw_g2=rw_g2[l],
                 rw_kk=rw_kk[l], rw_ka=rw_ka[l], rw_rk=rw_rk[l], rw_gn_g=rw_gn_g[l], rw_gn_b=rw_gn_b[l],
                 hg_norm_g=hg_norm_g[l], router_w=router_w[l], router_b=router_b[l],
                 moe_w_gu=moe_w_gu[l], moe_b_gu=moe_b_gu[l], moe_w_down=moe_w_down[l], moe_b_down=moe_b_down[l])
        xp, (k_l, v_l, sr_l, sh_l) = layer(xp, silu_ctx @ w_mod[l] + b_mod[l], p, lb_all[l], None)
        ks.append(k_l)
        vs.append(v_l)
        srs.append(sr_l)
        shs.append(sh_l)
        xs, _ = layer(xs, silu_lat @ w_mod[l] + b_mod[l], p, lb_all[l],
                      (cache_att_k[:, l], cache_att_v[:, l], state_rwkv[:, l], state_hgrn[:, l]))
    y_prompt = rms_norm(xp, final_norm_g)
    y_sample = rms_norm(xs, final_norm_g)
    new_att_k = jnp.stack(ks, axis=1)
    new_att_v = jnp.stack(vs, axis=1)
    new_state_rwkv = jnp.stack(srs, axis=1)
    new_state_hgrn = jnp.stack(shs, axis=1)
    return (y_prompt, y_sample, new_att_k, new_att_v, new_state_rwkv, new_state_hgrn)
```

```python
import math
from functools import partial

import jax
import jax.numpy as jnp
import numpy as np
from jax import lax
from jax.experimental import pallas as pl
from jax.experimental.pallas import tpu as pltpu

D_MODEL = 1024
DEPTH = 2
GRID_W = 64
HEAD_DIM = 64
ATT_HEADS = 8
ATT_KV_HEADS = 2
ATT_WIDTH = ATT_HEADS * HEAD_DIM
KV_WIDTH = ATT_KV_HEADS * HEAD_DIM
RW_HEADS = 4
RW_WIDTH = RW_HEADS * HEAD_DIM
RW_DECAY_LORA = 64
RW_AAA_LORA = 64
RW_GATE_LORA = 128
RW_GN_EPS = 64e-5
HG_HEADS = 4
HG_WIDTH = HG_HEADS * HEAD_DIM
HG_CHUNK = 32
HG_F_MIN = 1e-6
IN_SIZES = (ATT_WIDTH, KV_WIDTH, KV_WIDTH, RW_WIDTH, RW_WIDTH, RW_WIDTH, 2 * RW_DECAY_LORA, 2 * RW_AAA_LORA,
            RW_GATE_LORA, HG_WIDTH, 2 * HG_WIDTH, HG_WIDTH, HG_WIDTH)
N_EXPERTS = 32
TOP_K = 4
EXPERT_FF = D_MODEL
SWIGLU_LIMIT = 7.0
SWIGLU_ALPHA = 1.702
MOE_BLOCK = 128
Q_BLOCK = 128
ROPE_THETA = 10000.0
NORM_EPS = 1e-6


def _rms_norm_kernel(x_ref, g_ref, o_ref):
    x = x_ref[...]
    y = x * lax.rsqrt(jnp.mean(x * x, axis=-1, keepdims=True) + NORM_EPS)
    o_ref[...] = y * g_ref[...]


def rms_norm_rows(x, g, block_rows=512):
    R, D = x.shape
    return pl.pallas_call(
        _rms_norm_kernel,
        grid=(R // block_rows,),
        in_specs=[pl.BlockSpec((block_rows, D), lambda i: (i, 0)),
                  pl.BlockSpec((1, D), lambda i: (0, 0))],
        out_specs=pl.BlockSpec((block_rows, D), lambda i: (i, 0)),
        out_shape=jax.ShapeDtypeStruct((R, D), jnp.float32),
        name="final_rms_norm",
    )(x, g.reshape(1, D))


def rms_norm(x, g):
    xf = x.astype(jnp.float32)
    y = xf * lax.rsqrt(jnp.mean(xf * xf, axis=-1, keepdims=True) + NORM_EPS)
    return (y * g.astype(jnp.float32)).astype(x.dtype)


def axial_rope(n_tok):
    rows = n_tok // GRID_W
    row = jnp.repeat(jnp.arange(rows, dtype=jnp.float32), GRID_W)
    col = jnp.tile(jnp.arange(GRID_W, dtype=jnp.float32), rows)
    n_freq = HEAD_DIM // 4
    inv = ROPE_THETA ** (-jnp.arange(n_freq, dtype=jnp.float32) / n_freq)
    ang = jnp.concatenate([row[:, None] * inv, col[:, None] * inv], axis=-1)
    return jnp.cos(ang), jnp.sin(ang)


def apply_rope(x, cos, sin):
    xf = x.astype(jnp.float32).reshape(x.shape[:-1] + (HEAD_DIM // 2, 2))
    x1, x2 = xf[..., 0], xf[..., 1]
    out = jnp.stack([x1 * cos - x2 * sin, x1 * sin + x2 * cos], axis=-1)
    return out.reshape(x.shape).astype(x.dtype)


def block_attention(q, k, v):
    b, hq, tq, d = q.shape
    hkv = k.shape[1]
    rep = hq // hkv
    nblk = tq // Q_BLOCK
    qb = q.reshape(b, hkv, rep, nblk, Q_BLOCK, d).transpose(3, 0, 1, 2, 4, 5)
    scale = 1.0 / math.sqrt(d)

    def one_block(qblk):
        s = jnp.einsum('bgrqd,bgkd->bgrqk', qblk, k, preferred_element_type=jnp.float32) * scale
        pr = jax.nn.softmax(s, axis=-1)
        return jnp.einsum('bgrqk,bgkd->bgrqd', pr.astype(v.dtype), v)

    o = lax.map(one_block, qb)
    return o.transpose(1, 2, 3, 0, 4, 5).reshape(b, hq, tq, d)


def rwkv7_scan(r, w, k, v, a_, b_, s0):
    xs = tuple(jnp.moveaxis(t, 1, 0) for t in (r, w, k, v, a_, b_))

    def step(S, inp):
        rt, wt, kt, vt, at, bt = inp
        sa = jnp.einsum('bhvk,bhk->bhv', S, at)
        S = S * wt[:, :, None, :] + sa[..., None] * bt[:, :, None, :] + vt[..., None] * kt[:, :, None, :]
        return S, jnp.einsum('bhvk,bhk->bhv', S, rt)

    s_last, ys = lax.scan(step, s0, xs)
    return jnp.moveaxis(ys, 0, 1), s_last


def head_group_norm(y, g, b):
    mu = jnp.mean(y, axis=-1, keepdims=True)
    yc = y - mu
    var = jnp.mean(yc * yc, axis=-1, keepdims=True)
    yn = (yc * lax.rsqrt(var + RW_GN_EPS)).reshape(y.shape[0], y.shape[1], -1)
    return yn * g.astype(jnp.float32) + b.astype(jnp.float32)


def rwkv7_mixer(r, k, v, wd, ad, gd, p, s0):
    B, T, _ = r.shape
    f32 = jnp.float32
    heads = lambda t: t.astype(f32).reshape(B, T, RW_HEADS, HEAD_DIM)
    r, k, v = heads(r), heads(k), heads(v)
    wd = wd.astype(f32).reshape(B, T, 2, RW_DECAY_LORA)
    ad = ad.astype(f32).reshape(B, T, 2, RW_AAA_LORA)
    kk = k * p['rw_kk'].astype(f32).reshape(RW_HEADS, HEAD_DIM)
    kk = kk * lax.rsqrt(jnp.sum(kk * kk, axis=-1, keepdims=True) + 1e-12)
    g = jax.nn.sigmoid(gd.astype(f32)) @ p['rw_g2'].astype(f32)
    ka = p['rw_ka'].astype(f32).reshape(RW_HEADS, HEAD_DIM)
    rk = p['rw_rk'].astype(f32)
    ys, bonuses, states = [], [], []
    for d in range(2):
        w_raw = p['rw_w0'][d].astype(f32) + jnp.tanh(wd[:, :, d]) @ p['rw_w2'][d].astype(f32)
        decay = heads(jnp.exp(-jnp.exp(-jax.nn.softplus(-w_raw) - 0.5)))
        a = heads(jax.nn.sigmoid(p['rw_a0'][d].astype(f32) + ad[:, :, d] @ p['rw_a2'][d].astype(f32)))
        kd = k * (1.0 + (a - 1.0) * ka)
        seq = (r, decay, kd, v, -kk, kk * a)
        if d == 1:
            seq = tuple(jnp.flip(t, axis=1) for t in seq)
        y, s_last = rwkv7_scan(*seq, s0[:, d].astype(f32))
        if d == 1:
            y = jnp.flip(y, axis=1)
        ys.append(y)
        bonuses.append(jnp.sum(r * kd * rk, axis=-1, keepdims=True) * v)
        states.append(s_last)
    y = head_group_norm(ys[0] + ys[1], p['rw_gn_g'], p['rw_gn_b']) + (bonuses[0] + bonuses[1]).reshape(B, T, RW_WIDTH)
    return y * g, jnp.stack(states, axis=1)


def chunk_gla(q, k, v, logf, s0):
    B, H, T, dk = q.shape
    dv = v.shape[-1]
    n = T // HG_CHUNK
    rs = lambda t: t.reshape(B, H, n, HG_CHUNK, t.shape[-1])
    q, k, v, logf = rs(q), rs(k), rs(v), rs(logf)
    b = jnp.cumsum(logf, axis=3)
    causal = jnp.tril(jnp.ones((HG_CHUNK, HG_CHUNK), dtype=bool))
    diff = b[:, :, :, :, None, :] - b[:, :, :, None, :, :]
    dec = jnp.where(causal[:, :, None], jnp.exp(jnp.minimum(diff, 0.0)), 0.0)
    att = jnp.einsum('bhntd,bhnsd,bhntsd->bhnts', q, k, dec)
    o_intra = jnp.einsum('bhnts,bhnsv->bhntv', att, v)
    b_last = b[:, :, :, -1, :]
    kv = jnp.einsum('bhnsd,bhnsv->bhndv', k * jnp.exp(b_last[:, :, :, None, :] - b), v)

    def step(S, inp):
        dl, kvn = inp
        return dl[..., None] * S + kvn, S

    s_last, s_before = lax.scan(step, s0, (jnp.moveaxis(jnp.exp(b_last), 2, 0), jnp.moveaxis(kv, 2, 0)))
    s_before = jnp.moveaxis(s_before, 0, 2)
    o_inter = jnp.einsum('bhntd,bhndv->bhntv', q * jnp.exp(b), s_before)
    return (o_intra + o_inter).reshape(B, H, T, dv), s_last


def hgrn2_mixer(q, f_raw, i, g, lb, norm_g, s0):
    B, T, _ = q.shape
    f32 = jnp.float32
    heads = lambda t: t.astype(f32).reshape(B, T, HG_HEADS, HEAD_DIM).transpose(0, 2, 1, 3)
    qh = heads(jax.nn.silu(q.astype(f32)))
    vh = heads(i)
    f_raw = f_raw.astype(f32).reshape(B, T, 2, HG_WIDTH)
    outs, states = [], []
    for d in range(2):
        lbd = lb[d]
        f = lbd + (1.0 - lbd) * jax.nn.sigmoid(f_raw[:, :, d])
        logf = jnp.log(jnp.maximum(f, HG_F_MIN))
        kf = 1.0 - f
        seq = (qh, heads(kf), vh, heads(logf))
        if d == 1:
            seq = tuple(jnp.flip(t, axis=2) for t in seq)
        o, s_last = chunk_gla(*seq, s0[:, d].astype(f32))
        if d == 1:
            o = jnp.flip(o, axis=2)
        outs.append(o)
        states.append(s_last)
    o = rms_norm(outs[0] + outs[1], norm_g).transpose(0, 2, 1, 3).reshape(B, T, HG_WIDTH)
    return o * jax.nn.silu(g.astype(f32)), jnp.stack(states, axis=1)


def moe(h, p):
    T, D = h.shape
    logits = (h @ p['router_w']).astype(jnp.float32) + p['router_b'].astype(jnp.float32)
    top_logit, top_idx = lax.top_k(logits, TOP_K)
    gates = jax.nn.softmax(top_logit, axis=-1)
    A = T * TOP_K
    flat_e = top_idx.reshape(A)
    flat_tok = jnp.arange(A, dtype=jnp.int32) // TOP_K
    flat_g = gates.reshape(A)
    order = jnp.argsort(flat_e)
    se = flat_e[order]
    counts = jnp.bincount(flat_e, length=N_EXPERTS)
    starts = jnp.cumsum(counts) - counts
    padded = (counts + MOE_BLOCK - 1) // MOE_BLOCK * MOE_BLOCK
    pends = jnp.cumsum(padded)
    pstarts = pends - padded
    dest = pstarts[se] + jnp.arange(A, dtype=jnp.int32) - starts[se]
    n_blocks = -(-(A + N_EXPERTS * (MOE_BLOCK - 1)) // MOE_BLOCK)
    n_slots = n_blocks * MOE_BLOCK
    slot_tok = jnp.full((n_slots,), T, dtype=jnp.int32).at[dest].set(flat_tok[order])
    slot_g = jnp.zeros((n_slots,), dtype=jnp.float32).at[dest].set(flat_g[order])
    block_e = jnp.clip(jnp.searchsorted(pends, jnp.arange(n_blocks, dtype=jnp.int32) * MOE_BLOCK, side='right'), 0, N_EXPERTS - 1)
    h_pad = jnp.concatenate([h, jnp.zeros((1, D), h.dtype)], axis=0)
    xb = h_pad[slot_tok].reshape(n_blocks, MOE_BLOCK, D)

    def expert_block(args):
        xblk, e = args
        gu = xblk @ p['moe_w_gu'][e] + p['moe_b_gu'][e]
        glu, lin = gu[:, :EXPERT_FF], gu[:, EXPERT_FF:]
        glu = jnp.minimum(glu, SWIGLU_LIMIT)
        lin = jnp.clip(lin, -SWIGLU_LIMIT, SWIGLU_LIMIT)
        act = glu * jax.nn.sigmoid(SWIGLU_ALPHA * glu) * (lin + 1.0)
        return act @ p['moe_w_down'][e] + p['moe_b_down'][e]

    yb = lax.map(expert_block, (xb, block_e)).reshape(n_slots, D)
    y = jax.ops.segment_sum(yb * slot_g[:, None].astype(yb.dtype), slot_tok, num_segments=T + 1)[:T]
    return y.astype(h.dtype)


def token_mixers(h, p, lb, cache):
    B, T, _ = h.shape
    proj = h @ p['w_in']
    split_points = np.cumsum(IN_SIZES)[:-1].tolist()
    (aq, ak, av, rr, rk, rv, rwd, rad, rgd, hq, hf, hi, hg) = jnp.split(proj, split_points, axis=-1)
    qh = rms_norm(aq.reshape(B, T, ATT_HEADS, HEAD_DIM), p['att_qnorm_g']).transpose(0, 2, 1, 3)
    kh = rms_norm(ak.reshape(B, T, ATT_KV_HEADS, HEAD_DIM), p['att_knorm_g']).transpose(0, 2, 1, 3)
    vh = av.reshape(B, T, ATT_KV_HEADS, HEAD_DIM).transpose(0, 2, 1, 3)
    if cache is None:
        att = block_attention(qh, kh, vh)
        rw_s0 = jnp.zeros((B, 2, RW_HEADS, HEAD_DIM, HEAD_DIM), jnp.float32)
        hg_s0 = jnp.zeros((B, 2, HG_HEADS, HEAD_DIM, HEAD_DIM), jnp.float32)
    else:
        ck, cv, rw_s0, hg_s0 = cache
        cos, sin = axial_rope(T)
        qr = apply_rope(qh, cos, sin)
        kr = apply_rope(kh, cos, sin)
        att = block_attention(qr, jnp.concatenate([ck.astype(kr.dtype), kr], axis=2), jnp.concatenate([cv.astype(vh.dtype), vh], axis=2))
    att = att.transpose(0, 2, 1, 3).reshape(B, T, ATT_WIDTH)
    rw_out, rw_state = rwkv7_mixer(rr, rk, rv, rwd, rad, rgd, p, rw_s0)
    hg_out, hg_state = hgrn2_mixer(hq, hf, hi, hg, lb, p['hg_norm_g'], hg_s0)
    mix = jnp.concatenate([att.astype(h.dtype), rw_out.astype(h.dtype), hg_out.astype(h.dtype)], axis=-1)
    new_cache = (kh, vh, rw_state, hg_state) if cache is None else None
    return mix, new_cache


def layer(x, mod, p, lb, cache):
    shift1, scale1, gate1, shift2, scale2, gate2 = jnp.split(mod[:, None, :], 6, axis=-1)
    h = rms_norm(x, p['norm_mix_g']) * (1.0 + scale1) + shift1
    mix, new_cache = token_mixers(h, p, lb, cache)
    x = x + gate1 * (mix @ p['w_out'])
    h = rms_norm(x, p['norm_ffn_g']) * (1.0 + scale2) + shift2
    B, T, D = h.shape
    x = x + gate2 * moe(h.reshape(B * T, D), p).reshape(B, T, D)
    return x, new_cache


def hgrn_lower_bounds(hg_lb):
    sm = jax.nn.softmax(hg_lb.astype(jnp.float32), axis=0)
    return jnp.cumsum(sm, axis=0) - sm[0:1]


def kernel(x_prompt, x_sample, cache_att_k, cache_att_v, state_rwkv, state_hgrn, c, c_ctx, w_mod, b_mod, norm_mix_g, norm_ffn_g, w_in, w_out, att_qnorm_g, att_knorm_g, rw_w0, rw_w2, rw_a0, rw_a2, rw_g2, rw_kk, rw_ka, rw_rk, rw_gn_g, rw_gn_b, hg_lb, hg_norm_g, router_w, router_b, moe_w_gu, moe_b_gu, moe_w_down, moe_b_down, final_norm_g):
    lb_all = hgrn_lower_bounds(hg_lb)
    silu_ctx = jax.nn.silu(c_ctx)[None, :]
    silu_lat = jax.nn.silu(c)
    xp, xs = x_prompt, x_sample
    ks, vs, srs, shs = [], [], [], []
    for l in range(DEPTH):
        p = dict(norm_mix_g=norm_mix_g[l], norm_ffn_g=norm_ffn_g[l], w_in=w_in[l], w_out=w_out[l],
                 att_qnorm_g=att_qnorm_g[l], att_knorm_g=att_knorm_g[l],
                 rw_w0=rw_w0[l], rw_w2=rw_w2[l], rw_a0=rw_a0[l], rw_a2=rw_a2[l], rw_g2=rw_g2[l],
                 rw_kk=rw_kk[l], rw_ka=rw_ka[l], rw_rk=rw_rk[l], rw_gn_g=rw_gn_g[l], rw_gn_b=rw_gn_b[l],
                 hg_norm_g=hg_norm_g[l], router_w=router_w[l], router_b=router_b[l],
                 moe_w_gu=moe_w_gu[l], moe_b_gu=moe_b_gu[l], moe_w_down=moe_w_down[l], moe_b_down=moe_b_down[l])
        xp, (k_l, v_l, sr_l, sh_l) = layer(xp, silu_ctx @ w_mod[l] + b_mod[l], p, lb_all[l], None)
        ks.append(k_l)
        vs.append(v_l)
        srs.append(sr_l)
        shs.append(sh_l)
        xs, _ = layer(xs, silu_lat @ w_mod[l] + b_mod[l], p, lb_all[l],
                      (cache_att_k[:, l], cache_att_v[:, l], state_rwkv[:, l], state_hgrn[:, l]))
    y_prompt = rms_norm_rows(xp.reshape(-1, D_MODEL), final_norm_g).reshape(xp.shape)
    y_sample = rms_norm_rows(xs.reshape(-1, D_MODEL), final_norm_g).reshape(xs.shape)
    return (y_prompt, y_sample, jnp.stack(ks, axis=1), jnp.stack(vs, axis=1),
            jnp.stack(srs, axis=1), jnp.stack(shs, axis=1))
```

```python
import math
from functools import partial

import jax
import jax.numpy as jnp
import numpy as np
from jax import lax
from jax.experimental import pallas as pl
from jax.experimental.pallas import tpu as pltpu

D_MODEL = 1024
DEPTH = 2
GRID_W = 64
HEAD_DIM = 64
ATT_HEADS = 8
ATT_KV_HEADS = 2
ATT_WIDTH = ATT_HEADS * HEAD_DIM
KV_WIDTH = ATT_KV_HEADS * HEAD_DIM
RW_HEADS = 4
RW_WIDTH = RW_HEADS * HEAD_DIM
RW_DECAY_LORA = 64
RW_AAA_LORA = 64
RW_GATE_LORA = 128
RW_GN_EPS = 64e-5
HG_HEADS = 4
HG_WIDTH = HG_HEADS * HEAD_DIM
HG_CHUNK = 32
HG_F_MIN = 1e-6
IN_SIZES = (ATT_WIDTH, KV_WIDTH, KV_WIDTH, RW_WIDTH, RW_WIDTH, RW_WIDTH, 2 * RW_DECAY_LORA, 2 * RW_AAA_LORA,
            RW_GATE_LORA, HG_WIDTH, 2 * HG_WIDTH, HG_WIDTH, HG_WIDTH)
N_EXPERTS = 32
TOP_K = 4
EXPERT_FF = D_MODEL
SWIGLU_LIMIT = 7.0
SWIGLU_ALPHA = 1.702
MOE_BLOCK = 128
Q_BLOCK = 128
ROPE_THETA = 10000.0
NORM_EPS = 1e-6

RW_CHUNK = 64
BF16 = jnp.bfloat16
F32 = jnp.float32

_NN = (((1,), (0,)), ((), ()))
_NT = (((1,), (1,)), ((), ()))
_TN = (((0,), (0,)), ((), ()))


def _split(x, n):
    parts = []
    for _ in range(n - 1):
        hi = x.astype(BF16)
        parts.append(hi)
        x = x - hi.astype(F32)
    parts.append(x.astype(BF16))
    return parts


def _mm(a, b, dims=_NN, passes=1):
    d = lambda x, y: lax.dot_general(x, y, dims, preferred_element_type=F32)
    if passes == 1:
        return d(a.astype(BF16), b.astype(BF16))
    ah, al = _split(a, 2)
    bh, bl = _split(b, 2)
    return d(ah, bl) + d(al, bh) + d(ah, bh)


def _mm_exact_lhs(a01, b, n=3):
    a = a01.astype(BF16)
    out = None
    for t in reversed(_split(b, n)):
        y = lax.dot_general(a, t, _NN, preferred_element_type=F32)
        out = y if out is None else out + y
    return out


def _mm_exact_rhs(a, b01, n=3):
    b = b01.astype(BF16)
    out = None
    for t in reversed(_split(a, n)):
        y = lax.dot_general(t, b, _NN, preferred_element_type=F32)
        out = y if out is None else out + y
    return out


def _head_blockdiag(width):
    r = lax.broadcasted_iota(jnp.int32, (width, width), 0) // HEAD_DIM
    c = lax.broadcasted_iota(jnp.int32, (width, width), 1) // HEAD_DIM
    return (r == c).astype(F32)


def _sigmoid(x):
    return 1.0 / (1.0 + jnp.exp(-x))


def _softplus(x):
    return jnp.maximum(x, 0.0) + jnp.log(1.0 + jnp.exp(-jnp.abs(x)))


def _rwkv_kernel(p_ref, s0_ref, w0_ref, w2_ref, a0_ref, a2_ref, g2_ref, kk_ref, ka_ref, rk_ref, gng_ref, gnb_ref,
                 out_ref, st_ref,
                 lw_scr, kd_scr, bb_scr, y_scr, kk_scr, s_scr, *, T):
    C = RW_CHUNK
    n_chunks = T // C
    x = p_ref[0]
    r = x[:, 0:256]
    k = x[:, 256:512]
    v = x[:, 512:768]
    wd = x[:, 768:896]
    ad = x[:, 896:1024]
    gd = x[:, 1024:1152]
    bd = _head_blockdiag(RW_WIDTH)
    seg = lambda t: _mm_exact_rhs(t, bd)

    kk = k * kk_ref[...]
    kk = kk * lax.rsqrt(seg(kk * kk) + 1e-12)
    g = _mm(_sigmoid(gd), g2_ref[...], passes=3)
    ka = ka_ref[...]
    kk_scr[...] = kk
    bonus = jnp.zeros_like(r)
    for d in range(2):
        w_raw = w0_ref[d:d + 1, :] + _mm(jnp.tanh(wd[:, d * 64:(d + 1) * 64]), w2_ref[d], passes=3)
        lw_scr[d] = -jnp.exp(-_softplus(-w_raw) - 0.5)
        a = _sigmoid(a0_ref[d:d + 1, :] + _mm(ad[:, d * 64:(d + 1) * 64], a2_ref[d], passes=3))
        kd = k * (1.0 + (a - 1.0) * ka)
        kd_scr[d] = kd
        bb_scr[d] = kk * a
        bonus = bonus + seg(r * kd * rk_ref[...]) * v
    s_scr[...] = s0_ref[0]

    ti = lax.broadcasted_iota(jnp.int32, (C, C), 0)
    si = lax.broadcasted_iota(jnp.int32, (C, C), 1)
    ones_cc = jnp.ones((C, C), F32)

    def chunk_body(i, carry):
        ch = []
        for d in range(2):
            ci = i if d == 0 else n_chunks - 1 - i
            rows = pl.ds(pl.multiple_of(ci * C, C), C)
            strict = (ti > si) if d == 0 else (ti < si)
            incl = (ti >= si) if d == 0 else (ti <= si)
            lw = lw_scr[d, rows, :]
            cum = _mm_exact_lhs(incl.astype(F32), lw)
            total = _mm_exact_lhs(ones_cc, lw)
            cum_ex = cum - lw
            mid = 0.5 * total
            rr = p_ref[0, rows, 0:256]
            vv = p_ref[0, rows, 512:768]
            kdc = kd_scr[d, rows, :]
            bbc = bb_scr[d, rows, :]
            kkc = kk_scr[rows, :]
            e_inv = jnp.exp(mid - cum)
            At = -kkc * jnp.exp(cum_ex - mid)
            Rt = rr * jnp.exp(cum - mid)
            Bt = bbc * e_inv
            Kt = kdc * e_inv
            Ap = -kkc * jnp.exp(cum_ex)
            Rp = rr * jnp.exp(cum)
            e_out = jnp.exp(total - cum)
            Bh = bbc * e_out
            Kh = kdc * e_out
            e_tot = jnp.exp(total[0:1, :])
            for h in range(RW_HEADS):
                hs = slice(h * HEAD_DIM, (h + 1) * HEAD_DIM)
                ch.append(dict(d=d, h=h, rows=rows, hs=hs, strict=strict, incl=incl,
                               AR=jnp.concatenate([At[:, hs], Rt[:, hs]], axis=0), Bt=Bt[:, hs], Kt=Kt[:, hs],
                               V=vv[:, hs], X1=Ap[:, hs], Rp=Rp[:, hs], Bh=Bh[:, hs], Kh=Kh[:, hs],
                               e_tot=e_tot[:, hs]))
        for c in ch:
            c['AB'] = _mm(c['AR'], c['Bt'], _NT)
            c['AK'] = _mm(c['AR'], c['Kt'], _NT)
        for c in ch:
            c['P'] = jnp.where(c['strict'], c['AB'][:C], 0.0)
            c['A_ak'] = jnp.where(c['strict'], c['AK'][:C], 0.0)
            c['A_rb'] = jnp.where(c['incl'], c['AB'][C:], 0.0)
            c['A_rk'] = jnp.where(c['incl'], c['AK'][C:], 0.0)
        for c in ch:
            c['X2'] = _mm(c['A_ak'], c['V'])
        for lvl in range(6):
            for c in ch:
                if lvl < 5:
                    c['PZ'] = _mm(c['P'], jnp.concatenate([c['P'], c['X1'], c['X2']], axis=1))
                else:
                    c['PZ'] = _mm(c['P'], jnp.concatenate([c['X1'], c['X2']], axis=1))
            for c in ch:
                PZ = c['PZ']
                if lvl < 5:
                    c['P'] = PZ[:, :C]
                    c['X1'] = c['X1'] + PZ[:, C:2 * C]
                    c['X2'] = c['X2'] + PZ[:, 2 * C:]
                else:
                    c['X1'] = c['X1'] + PZ[:, :C]
                    c['X2'] = c['X2'] + PZ[:, C:]
        for c in ch:
            c['S0'] = s_scr[c['d'], c['h']]
            c['UY'] = _mm(jnp.concatenate([c['X1'], c['Rp']], axis=0), c['S0'], _NT)
        for c in ch:
            c['U'] = c['UY'][:C] + c['X2']
        for c in ch:
            c['Y'] = c['UY'][C:] + _mm(c['A_rb'], c['U']) + _mm(c['A_rk'], c['V'])
            c['S1'] = c['S0'] * c['e_tot'] + _mm(c['U'], c['Bh'], _TN) + _mm(c['V'], c['Kh'], _TN)
        for c in ch:
            s_scr[c['d'], c['h']] = c['S1']
            y_scr[c['d'], c['rows'], c['hs']] = c['Y']
        return carry

    lax.fori_loop(0, n_chunks, chunk_body, 0)

    y = y_scr[0] + y_scr[1]
    mu = seg(y) * (1.0 / HEAD_DIM)
    yc = y - mu
    var = seg(yc * yc) * (1.0 / HEAD_DIM)
    yn = yc * lax.rsqrt(var + RW_GN_EPS)
    out_ref[0] = (yn * gng_ref[...] + gnb_ref[...] + bonus) * g
    st_ref[0] = s_scr[...]


def rwkv7_mixer_pallas(p_rw, s0, prm):
    B, T, W = p_rw.shape
    row = lambda a: a.reshape(1, RW_WIDTH)
    full = lambda shape: pl.BlockSpec(shape, lambda b: (0,) * len(shape))
    return pl.pallas_call(
        partial(_rwkv_kernel, T=T),
        grid=(B,),
        in_specs=[pl.BlockSpec((1, T, W), lambda b: (b, 0, 0)),
                  pl.BlockSpec((1, 2, RW_HEADS, HEAD_DIM, HEAD_DIM), lambda b: (b, 0, 0, 0, 0)),
                  full((2, RW_WIDTH)), full((2, 64, RW_WIDTH)), full((2, RW_WIDTH)), full((2, 64, RW_WIDTH)),
                  full((128, RW_WIDTH)), full((1, RW_WIDTH)), full((1, RW_WIDTH)), full((1, RW_WIDTH)),
                  full((1, RW_WIDTH)), full((1, RW_WIDTH))],
        out_specs=[pl.BlockSpec((1, T, RW_WIDTH), lambda b: (b, 0, 0)),
                   pl.BlockSpec((1, 2, RW_HEADS, HEAD_DIM, HEAD_DIM), lambda b: (b, 0, 0, 0, 0))],
        out_shape=[jax.ShapeDtypeStruct((B, T, RW_WIDTH), F32),
                   jax.ShapeDtypeStruct((B, 2, RW_HEADS, HEAD_DIM, HEAD_DIM), F32)],
        scratch_shapes=[pltpu.VMEM((2, T, RW_WIDTH), F32)] * 4
        + [pltpu.VMEM((T, RW_WIDTH), F32), pltpu.VMEM((2, RW_HEADS, HEAD_DIM, HEAD_DIM), F32)],
        compiler_params=pltpu.CompilerParams(dimension_semantics=("arbitrary",), vmem_limit_bytes=48 * 1024 * 1024),
        name="rwkv7_mixer",
    )(p_rw, s0, prm['rw_w0'], prm['rw_w2'], prm['rw_a0'], prm['rw_a2'], prm['rw_g2'], row(prm['rw_kk']),
      row(prm['rw_ka']), row(prm['rw_rk']), row(prm['rw_gn_g']), row(prm['rw_gn_b']))


def _rms_norm_kernel(x_ref, g_ref, o_ref):
    x = x_ref[...]
    y = x * lax.rsqrt(jnp.mean(x * x, axis=-1, keepdims=True) + NORM_EPS)
    o_ref[...] = y * g_ref[...]


def rms_norm_rows(x, g, block_rows=512):
    R, D = x.shape
    return pl.pallas_call(
        _rms_norm_kernel,
        grid=(R // block_rows,),
        in_specs=[pl.BlockSpec((block_rows, D), lambda i: (i, 0)),
                  pl.BlockSpec((1, D), lambda i: (0, 0))],
        out_specs=pl.BlockSpec((block_rows, D), lambda i: (i, 0)),
        out_shape=jax.ShapeDtypeStruct((R, D), jnp.float32),
        name="final_rms_norm",
    )(x, g.reshape(1, D))


HG_SUB = 16
HG_ROWS = 64


def _hgrn_kernel(p_ref, s0_ref, lb_ref, ng_ref, out_ref, st_ref, lf_scr, kf_scr, o_scr, s_scr, *, T):
    R, c = HG_ROWS, HG_SUB
    n_it = T // R
    x = p_ref[0]
    bd = _head_blockdiag(HG_WIDTH)
    seg = lambda t: _mm_exact_rhs(t, bd)
    for d in range(2):
        lbd = lb_ref[d:d + 1, :]
        f = lbd + (1.0 - lbd) * _sigmoid(x[:, 256 + 256 * d:512 + 256 * d])
        lf_scr[d] = jnp.log(jnp.maximum(f, HG_F_MIN))
        kf_scr[d] = 1.0 - f
        for h in range(HG_HEADS):
            s_scr[d, h] = s0_ref[0, d, h].T

    ti = lax.broadcasted_iota(jnp.int32, (R, R), 0)
    si = lax.broadcasted_iota(jnp.int32, (R, R), 1)
    same_blk = (ti // c) == (si // c)
    t16 = lax.broadcasted_iota(jnp.int32, (c, 1), 0)

    def body(i, carry):
        for d in range(2):
            ci = i if d == 0 else n_it - 1 - i
            rows = pl.ds(pl.multiple_of(ci * R, R), R)
            incl = (ti >= si) if d == 0 else (ti <= si)
            lf = lf_scr[d, rows, :]
            cum = _mm_exact_lhs((incl & same_blk).astype(F32), lf)
            tot = _mm_exact_lhs(same_blk.astype(F32), lf)
            xq = p_ref[0, rows, 0:256]
            q = xq * _sigmoid(xq)
            v = p_ref[0, rows, 768:1024]
            kf = kf_scr[d, rows, :]
            Qp = q * jnp.exp(cum)
            Kh = kf * jnp.exp(tot - cum)
            e_tot = jnp.exp(tot)
            blocks = range(R // c) if d == 0 else range(R // c - 1, -1, -1)
            o_parts = [None] * (R // c)
            for j in blocks:
                rs = slice(j * c, (j + 1) * c)
                cb, qb, kb, vb = cum[rs], q[rs], kf[rs], v[rs]
                prods = []
                for s in range(c):
                    e = jnp.exp(jnp.minimum(cb - cb[s:s + 1, :], 0.0))
                    prods.append(qb * (kb[s:s + 1, :] * e))
                att = _mm_exact_rhs(jnp.concatenate(prods, axis=0), bd, n=2)
                o_blk = jnp.zeros((c, HG_WIDTH), F32)
                for s in range(c):
                    keep = (t16 >= s) if d == 0 else (t16 <= s)
                    o_blk = o_blk + jnp.where(keep, att[s * c:(s + 1) * c], 0.0) * vb[s:s + 1, :]
                o_heads = []
                for h in range(HG_HEADS):
                    hs = slice(h * HEAD_DIM, (h + 1) * HEAD_DIM)
                    ST = s_scr[d, h]
                    o_heads.append(_mm(Qp[rs, hs], ST, _NT))
                    s_scr[d, h] = ST * e_tot[j * c:j * c + 1, hs] + _mm(vb[:, hs], Kh[rs, hs], _TN)
                o_parts[j] = o_blk + jnp.concatenate(o_heads, axis=1)
            o_scr[d, rows, :] = jnp.concatenate(o_parts, axis=0)
        return carry

    lax.fori_loop(0, n_it, body, 0)

    o = o_scr[0] + o_scr[1]
    o = o * lax.rsqrt(seg(o * o) * (1.0 / HEAD_DIM) + NORM_EPS) * ng_ref[...]
    gg = x[:, 1024:1280]
    out_ref[0] = o * (gg * _sigmoid(gg))
    for d in range(2):
        for h in range(HG_HEADS):
            st_ref[0, d, h] = s_scr[d, h].T


def hgrn2_mixer_pallas(p_hg, s0, lb, norm_g):
    B, T, W = p_hg.shape
    full = lambda shape: pl.BlockSpec(shape, lambda b: (0,) * len(shape))
    st_spec = pl.BlockSpec((1, 2, HG_HEADS, HEAD_DIM, HEAD_DIM), lambda b: (b, 0, 0, 0, 0))
    return pl.pallas_call(
        partial(_hgrn_kernel, T=T),
        grid=(B,),
        in_specs=[pl.BlockSpec((1, T, W), lambda b: (b, 0, 0)), st_spec, full((2, HG_WIDTH)), full((1, HG_WIDTH))],
        out_specs=[pl.BlockSpec((1, T, HG_WIDTH), lambda b: (b, 0, 0)), st_spec],
        out_shape=[jax.ShapeDtypeStruct((B, T, HG_WIDTH), F32),
                   jax.ShapeDtypeStruct((B, 2, HG_HEADS, HEAD_DIM, HEAD_DIM), F32)],
        scratch_shapes=[pltpu.VMEM((2, T, HG_WIDTH), F32)] * 3
        + [pltpu.VMEM((2, HG_HEADS, HEAD_DIM, HEAD_DIM), F32)],
        compiler_params=pltpu.CompilerParams(dimension_semantics=("arbitrary",), vmem_limit_bytes=48 * 1024 * 1024),
        name="hgrn2_mixer",
    )(p_hg, s0, lb, jnp.tile(norm_g.reshape(1, HEAD_DIM), (1, HG_HEADS)))


ATT_REP = ATT_HEADS // ATT_KV_HEADS
ATT_QROWS = 128


def _swap_pairs(x):
    w = x.shape[-1]
    lane = lax.broadcasted_iota(jnp.int32, x.shape, x.ndim - 1)
    return jnp.where(lane % 2 == 0, pltpu.roll(x, w - 1, x.ndim - 1), pltpu.roll(x, 1, x.ndim - 1))


def _att_kernel(*refs, T, past, rope):
    if rope:
        p_ref, qg_ref, kg_ref, cos_ref, sin_ref, ck_ref, cv_ref, out_ref, k_scr, v_scr, q_scr = refs
    else:
        p_ref, qg_ref, kg_ref, out_ref, kh_ref, vh_ref, k_scr, v_scr, q_scr = refs
    x = p_ref[0]
    q = x[:, 0:ATT_WIDTH]
    k = x[:, ATT_WIDTH:ATT_WIDTH + KV_WIDTH]
    v = x[:, ATT_WIDTH + KV_WIDTH:ATT_WIDTH + 2 * KV_WIDTH]
    inv_d = 1.0 / HEAD_DIM
    q = q * lax.rsqrt(_mm_exact_rhs(q * q, _head_blockdiag(ATT_WIDTH)) * inv_d + NORM_EPS) * qg_ref[...]
    k = k * lax.rsqrt(_mm_exact_rhs(k * k, _head_blockdiag(KV_WIDTH)) * inv_d + NORM_EPS) * kg_ref[...]
    if rope:
        cos, sin = cos_ref[...], sin_ref[...]
        rep = ATT_WIDTH // KV_WIDTH
        q = q * jnp.concatenate([cos] * rep, axis=1) + _swap_pairs(q) * jnp.concatenate([sin] * rep, axis=1)
        k = k * cos + _swap_pairs(k) * sin
    q_scr[...] = (q * (1.0 / math.sqrt(HEAD_DIM))).astype(BF16)
    for g in range(ATT_KV_HEADS):
        gs = slice(g * HEAD_DIM, (g + 1) * HEAD_DIM)
        if rope:
            k_scr[g, 0:past, :] = ck_ref[0, g].astype(BF16)
            v_scr[g, 0:past, :] = cv_ref[0, g].astype(BF16)
        else:
            kh_ref[0, g] = k[:, gs]
            vh_ref[0, g] = v[:, gs]
        k_scr[g, past:past + T, :] = k[:, gs].astype(BF16)
        v_scr[g, past:past + T, :] = v[:, gs].astype(BF16)
    QR = ATT_QROWS

    def q_block(qb, carry):
        rows = pl.ds(pl.multiple_of(qb * QR, QR), QR)
        qblk = q_scr[rows, :]
        for g in range(ATT_KV_HEADS):
            qs = jnp.concatenate([qblk[:, (g * ATT_REP + r) * HEAD_DIM:(g * ATT_REP + r + 1) * HEAD_DIM]
                                  for r in range(ATT_REP)], axis=0)
            s = lax.dot_general(qs, k_scr[g], _NT, preferred_element_type=F32)
            e = jnp.exp(s - jnp.max(s, axis=-1, keepdims=True))
            l = jnp.sum(e, axis=-1, keepdims=True)
            o = lax.dot_general(e.astype(BF16), v_scr[g], _NN, preferred_element_type=F32) / l
            for r in range(ATT_REP):
                h = g * ATT_REP + r
                out_ref[0, rows, h * HEAD_DIM:(h + 1) * HEAD_DIM] = o[r * QR:(r + 1) * QR]
        return carry

    lax.fori_loop(0, T // QR, q_block, 0)


def rope_tables(T):
    rows = T // GRID_W
    row = jnp.repeat(jnp.arange(rows, dtype=F32), GRID_W)
    col = jnp.tile(jnp.arange(GRID_W, dtype=F32), rows)
    n_freq = HEAD_DIM // 4
    inv = ROPE_THETA ** (-jnp.arange(n_freq, dtype=F32) / n_freq)
    ang = jnp.concatenate([row[:, None] * inv, col[:, None] * inv], axis=-1)
    cos = jnp.repeat(jnp.cos(ang), 2, axis=-1)
    sin = jnp.stack([-jnp.sin(ang), jnp.sin(ang)], axis=-1).reshape(T, HEAD_DIM)
    return jnp.tile(cos, (1, ATT_KV_HEADS)), jnp.tile(sin, (1, ATT_KV_HEADS))


def attention_pallas(p_att, qnorm_g, knorm_g, cache=None):
    B, T, W = p_att.shape
    rope = cache is not None
    past = cache[0].shape[2] if rope else 0
    full = lambda shape: pl.BlockSpec(shape, lambda b: (0,) * len(shape))
    qg = jnp.tile(qnorm_g.reshape(1, HEAD_DIM), (1, ATT_HEADS))
    kg = jnp.tile(knorm_g.reshape(1, HEAD_DIM), (1, ATT_KV_HEADS))
    in_specs = [pl.BlockSpec((1, T, W), lambda b: (b, 0, 0)), full((1, ATT_WIDTH)), full((1, KV_WIDTH))]
    args = [p_att, qg, kg]
    out_specs = [pl.BlockSpec((1, T, ATT_WIDTH), lambda b: (b, 0, 0))]
    out_shape = [jax.ShapeDtypeStruct((B, T, ATT_WIDTH), F32)]
    if rope:
        cos, sin = rope_tables(T)
        kv_spec = pl.BlockSpec((1, ATT_KV_HEADS, past, HEAD_DIM), lambda b: (b, 0, 0, 0))
        in_specs += [full((T, KV_WIDTH)), full((T, KV_WIDTH)), kv_spec, kv_spec]
        args += [cos, sin, cache[0], cache[1]]
    else:
        kv_spec = pl.BlockSpec((1, ATT_KV_HEADS, T, HEAD_DIM), lambda b: (b, 0, 0, 0))
        out_specs += [kv_spec, kv_spec]
        out_shape += [jax.ShapeDtypeStruct((B, ATT_KV_HEADS, T, HEAD_DIM), F32)] * 2
    res = pl.pallas_call(
        partial(_att_kernel, T=T, past=past, rope=rope),
        grid=(B,),
        in_specs=in_specs,
        out_specs=out_specs,
        out_shape=out_shape,
        scratch_shapes=[pltpu.VMEM((ATT_KV_HEADS, past + T, HEAD_DIM), BF16)] * 2
        + [pltpu.VMEM((T, ATT_WIDTH), BF16)],
        compiler_params=pltpu.CompilerParams(dimension_semantics=("arbitrary",), vmem_limit_bytes=48 * 1024 * 1024),
        name="attention_rope" if rope else "attention_ctx",
    )(*args)
    return res[0] if rope else tuple(res)


ROW_TILE = 256
MOD_TILE = 1536
ROUTE_TILE = 256
MOE_SLOTS = 256
MOE_TOKENS = 512
P_ATT, P_RW, P_HG = ATT_WIDTH + 2 * KV_WIDTH, 3 * RW_WIDTH + 384, 5 * HG_WIDTH


def _mod_kernel(c_ref, w_ref, b_ref, o_ref):
    c = c_ref[...]
    o_ref[0] = _mm(c * _sigmoid(c), w_ref[0], passes=3) + b_ref[0]


def adaln_mod_pallas(cvec, w_mod, b_mod):
    n = 6 * D_MODEL
    return pl.pallas_call(
        _mod_kernel,
        grid=(DEPTH, n // MOD_TILE),
        in_specs=[pl.BlockSpec((8, D_MODEL), lambda l, j: (0, 0)),
                  pl.BlockSpec((1, D_MODEL, MOD_TILE), lambda l, j: (l, 0, j)),
                  pl.BlockSpec((1, 1, MOD_TILE), lambda l, j: (l, 0, j))],
        out_specs=pl.BlockSpec((1, 8, MOD_TILE), lambda l, j: (l, 0, j)),
        out_shape=jax.ShapeDtypeStruct((DEPTH, 8, n), F32),
        compiler_params=pltpu.CompilerParams(dimension_semantics=("arbitrary", "arbitrary"),
                                             vmem_limit_bytes=48 * 1024 * 1024),
        name="adaln_mod",
    )(cvec, w_mod, b_mod.reshape(DEPTH, 1, n))


def _rms(x):
    return x * lax.rsqrt(jnp.mean(x * x, axis=-1, keepdims=True) + NORM_EPS)


def _in_kernel(*refs, has_res):
    if has_res:
        x_ref, y_ref, pm_ref, m_ref, g_ref, w_ref, xo_ref, pa_ref, pr_ref, ph_ref = refs
        x = x_ref[...] + pm_ref[0, 5:6, :] * y_ref[...]
        xo_ref[...] = x
    else:
        x_ref, m_ref, g_ref, w_ref, pa_ref, pr_ref, ph_ref = refs
        x = x_ref[...]
    h = _rms(x) * g_ref[...] * (1.0 + m_ref[0, 1:2, :]) + m_ref[0, 0:1, :]
    proj = lax.dot_general(h.astype(BF16), w_ref[...], _NN, preferred_element_type=F32)
    pa_ref[...] = proj[:, 0:P_ATT]
    pr_ref[...] = proj[:, P_ATT:P_ATT + P_RW]
    ph_ref[...] = proj[:, P_ATT + P_RW:]


def in_proj_pallas(x, mod, norm_g, w_in_bf, rows_per_mod, res=None):
    R = x.shape[0]
    tpm = rows_per_mod // ROW_TILE
    rt = lambda w: pl.BlockSpec((ROW_TILE, w), lambda i: (i, 0))
    ms = pl.BlockSpec((1, 6, D_MODEL), lambda i: (i // tpm, 0, 0))
    full = lambda shape: pl.BlockSpec(shape, lambda i: (0,) * len(shape))
    in_specs = [rt(D_MODEL)] + ([rt(D_MODEL), ms] if res else []) + [ms, full((1, D_MODEL)),
                                                                     full((D_MODEL, w_in_bf.shape[1]))]
    args = [x] + ([res[0], res[1]] if res else []) + [mod, norm_g.reshape(1, D_MODEL), w_in_bf]
    widths = ([D_MODEL] if res else []) + [P_ATT, P_RW, P_HG]
    return pl.pallas_call(
        partial(_in_kernel, has_res=res is not None),
        grid=(R // ROW_TILE,),
        in_specs=in_specs,
        out_specs=[rt(w) for w in widths],
        out_shape=[jax.ShapeDtypeStruct((R, w), F32) for w in widths],
        compiler_params=pltpu.CompilerParams(dimension_semantics=("arbitrary",), vmem_limit_bytes=48 * 1024 * 1024),
        name="in_proj",
    )(*args)


def _out_kernel(att_ref, rw_ref, hg_ref, x_ref, m_ref, g_ref, w_ref, rw_w_ref, rb_ref, xo_ref, h_ref, lg_ref):
    d = lambda a, lo, hi: lax.dot_general(a.astype(BF16), w_ref[lo:hi, :], _NN, preferred_element_type=F32)
    mixo = (d(att_ref[...], 0, ATT_WIDTH) + d(rw_ref[...], ATT_WIDTH, ATT_WIDTH + RW_WIDTH)
            + d(hg_ref[...], ATT_WIDTH + RW_WIDTH, ATT_WIDTH + RW_WIDTH + HG_WIDTH))
    x = x_ref[...] + m_ref[0, 2:3, :] * mixo
    xo_ref[...] = x
    h = _rms(x) * g_ref[...] * (1.0 + m_ref[0, 4:5, :]) + m_ref[0, 3:4, :]
    h_ref[...] = h.astype(BF16)
    lg_ref[...] = _mm(h, rw_w_ref[...], passes=3) + rb_ref[...]


def out_proj_pallas(att, rw, hg, x, mod, norm_g, w_out_bf, router_w, router_b, rows_per_mod):
    R = x.shape[0]
    tpm = rows_per_mod // ROW_TILE
    rt = lambda w: pl.BlockSpec((ROW_TILE, w), lambda i: (i, 0))
    full = lambda shape: pl.BlockSpec(shape, lambda i: (0,) * len(shape))
    return pl.pallas_call(
        _out_kernel,
        grid=(R // ROW_TILE,),
        in_specs=[rt(ATT_WIDTH), rt(RW_WIDTH), rt(HG_WIDTH), rt(D_MODEL),
                  pl.BlockSpec((1, 6, D_MODEL), lambda i: (i // tpm, 0, 0)), full((1, D_MODEL)),
                  full((D_MODEL, D_MODEL)), full((D_MODEL, N_EXPERTS)), full((1, N_EXPERTS))],
        out_specs=[rt(D_MODEL), rt(D_MODEL), rt(N_EXPERTS)],
        out_shape=[jax.ShapeDtypeStruct((R, D_MODEL), F32), jax.ShapeDtypeStruct((R, D_MODEL), BF16),
                   jax.ShapeDtypeStruct((R, N_EXPERTS), F32)],
        compiler_params=pltpu.CompilerParams(dimension_semantics=("arbitrary",), vmem_limit_bytes=48 * 1024 * 1024),
        name="out_proj",
    )(att, rw, hg, x, mod, norm_g.reshape(1, D_MODEL), w_out_bf, router_w, router_b.reshape(1, N_EXPERTS))


def _final_kernel(x_ref, y_ref, m_ref, g_ref, o_ref):
    o_ref[...] = _rms(x_ref[...] + m_ref[0, 5:6, :] * y_ref[...]) * g_ref[...]


def final_norm_pallas(x, y, mod, norm_g, rows_per_mod):
    R = x.shape[0]
    tpm = rows_per_mod // ROW_TILE
    rt = pl.BlockSpec((ROW_TILE, D_MODEL), lambda i: (i, 0))
    return pl.pallas_call(
        _final_kernel,
        grid=(R // ROW_TILE,),
        in_specs=[rt, rt, pl.BlockSpec((1, 6, D_MODEL), lambda i: (i // tpm, 0, 0)),
                  pl.BlockSpec((1, D_MODEL), lambda i: (0, 0))],
        out_specs=rt,
        out_shape=jax.ShapeDtypeStruct((R, D_MODEL), F32),
        name="final_norm",
    )(x, y, mod, norm_g.reshape(1, D_MODEL))


def _route_kernel(lg_ref, posT_ref, gateT_ref, offs_ref, *, n_tok):
    Rt = ROUTE_TILE
    lane = lax.broadcasted_iota(jnp.int32, (Rt, N_EXPERTS), 1)
    ti = lax.broadcasted_iota(jnp.int32, (Rt, Rt), 0)
    si = lax.broadcasted_iota(jnp.int32, (Rt, Rt), 1)
    upper = (ti < si).astype(BF16)
    eye = (ti == si).astype(BF16)
    tn = lambda a, b: lax.dot_general(a, b, _TN, preferred_element_type=F32)
    off = jnp.zeros((N_EXPERTS, 1), F32)
    for it in range(n_tok // Rt):
        rows = slice(it * Rt, (it + 1) * Rt)
        lg = lg_ref[rows, :]
        work = lg
        member = jnp.zeros((Rt, N_EXPERTS), jnp.bool_)
        top = None
        for k in range(TOP_K):
            m = jnp.max(work, axis=-1, keepdims=True)
            if top is None:
                top = m
            first = jnp.min(jnp.where(work == m, lane, N_EXPERTS), axis=-1, keepdims=True)
            pick = lane == first
            member = member | pick
            work = jnp.where(pick, -jnp.inf, work)
        e = jnp.where(member, jnp.exp(lg - top), 0.0)
        gate = e / jnp.sum(e, axis=-1, keepdims=True)
        mem = member.astype(BF16)
        memT = tn(mem, eye)
        rankT = tn(mem, upper) + off
        g1, g2, g3 = _split(gate, 3)
        gateT = tn(g3, eye) + tn(g2, eye) + tn(g1, eye)
        posT_ref[:, rows] = jnp.where(memT > 0.5, rankT, -1.0).astype(jnp.int32)
        gateT_ref[:, rows] = gateT
        offs_ref[:, it:it + 1] = off.astype(jnp.int32)
        off = off + jnp.sum(memT, axis=1, keepdims=True)
    offs_ref[:, n_tok // Rt:n_tok // Rt + 1] = off.astype(jnp.int32)


def moe_route_pallas(logits):
    n_tok = logits.shape[0]
    nt = n_tok // ROUTE_TILE
    return pl.pallas_call(
        partial(_route_kernel, n_tok=n_tok),
        out_shape=[jax.ShapeDtypeStruct((N_EXPERTS, n_tok), jnp.int32),
                   jax.ShapeDtypeStruct((N_EXPERTS, n_tok), F32),
                   jax.ShapeDtypeStruct((N_EXPERTS, nt + 1), jnp.int32)],
        name="moe_route",
    )(logits)


def _moe_kernel(off_ref, h_ref, posT_ref, gateT_ref, wgu_ref, bgu_ref, wdn_ref, bdn_ref, y_ref, xb_scr, gs_scr,
                *, n_tok):
    e = pl.program_id(0)
    SB, TW = MOE_SLOTS, MOE_TOKENS
    n_chunks = n_tok // TW

    @pl.when(e == 0)
    def _():
        y_ref[...] = jnp.zeros_like(y_ref)

    count = off_ref[e, n_chunks]
    slot_col = lax.broadcasted_iota(jnp.int32, (SB, 1), 0)

    def block(sb, carry):
        s0 = sb * SB
        xb_scr[...] = jnp.zeros_like(xb_scr)
        gs_scr[...] = jnp.zeros_like(gs_scr)

        def select(c):
            return posT_ref[0, :, c * TW:(c + 1) * TW] == (slot_col + s0)

        def hit(c):
            return (off_ref[e, c] < s0 + SB) & (off_ref[e, c + 1] > s0)

        for c in range(n_chunks):
            @pl.when(hit(c))
            def _(c=c):
                sel = select(c)
                xb_scr[...] += lax.dot_general(sel.astype(BF16), h_ref[c * TW:(c + 1) * TW, :], _NN,
                                               preferred_element_type=F32)
                gs_scr[...] += jnp.sum(jnp.where(sel, gateT_ref[0, :, c * TW:(c + 1) * TW], 0.0), axis=1,
                                       keepdims=True)
        xb = xb_scr[...].astype(BF16)
        gu = lax.dot_general(xb, wgu_ref[0], _NN, preferred_element_type=F32) + bgu_ref[0]
        glu = jnp.minimum(gu[:, :EXPERT_FF], SWIGLU_LIMIT)
        lin = jnp.clip(gu[:, EXPERT_FF:], -SWIGLU_LIMIT, SWIGLU_LIMIT)
        act = glu * _sigmoid(SWIGLU_ALPHA * glu) * (lin + 1.0)
        yb = lax.dot_general(act.astype(BF16), wdn_ref[0], _NN, preferred_element_type=F32) + bdn_ref[0]
        ybg = (yb * gs_scr[...]).astype(BF16)
        for c in range(n_chunks):
            @pl.when(hit(c))
            def _(c=c):
                sel = select(c)
                y_ref[c * TW:(c + 1) * TW, :] += lax.dot_general(sel.astype(BF16), ybg, _TN,
                                                                 preferred_element_type=F32)
        return carry

    lax.fori_loop(0, (count + SB - 1) // SB, block, 0)


def moe_experts_pallas(h_bf16, posT, gateT, chunk_off, w_gu, b_gu, w_down, b_down):
    n_tok = h_bf16.shape[0]
    once = dict(pipeline_mode=pl.Buffered(1))
    grid_spec = pltpu.PrefetchScalarGridSpec(
        num_scalar_prefetch=1,
        grid=(N_EXPERTS,),
        in_specs=[pl.BlockSpec((n_tok, D_MODEL), lambda e, off: (0, 0), **once),
                  pl.BlockSpec((1, 1, n_tok), lambda e, off: (e, 0, 0)),
                  pl.BlockSpec((1, 1, n_tok), lambda e, off: (e, 0, 0)),
                  pl.BlockSpec((1, D_MODEL, 2 * EXPERT_FF), lambda e, off: (e, 0, 0)),
                  pl.BlockSpec((1, 1, 2 * EXPERT_FF), lambda e, off: (e, 0, 0)),
                  pl.BlockSpec((1, EXPERT_FF, D_MODEL), lambda e, off: (e, 0, 0)),
                  pl.BlockSpec((1, 1, D_MODEL), lambda e, off: (e, 0, 0))],
        out_specs=pl.BlockSpec((n_tok, D_MODEL), lambda e, off: (0, 0), **once),
        scratch_shapes=[pltpu.VMEM((MOE_SLOTS, D_MODEL), F32), pltpu.VMEM((MOE_SLOTS, 1), F32)],
    )
    return pl.pallas_call(
        partial(_moe_kernel, n_tok=n_tok),
        grid_spec=grid_spec,
        out_shape=jax.ShapeDtypeStruct((n_tok, D_MODEL), F32),
        compiler_params=pltpu.CompilerParams(dimension_semantics=("arbitrary",), vmem_limit_bytes=58 * 1024 * 1024),
        name="moe_experts",
    )(chunk_off, h_bf16, posT.reshape(N_EXPERTS, 1, n_tok), gateT.reshape(N_EXPERTS, 1, n_tok),
      w_gu, b_gu.reshape(N_EXPERTS, 1, 2 * EXPERT_FF), w_down, b_down.reshape(N_EXPERTS, 1, D_MODEL))


def moe_pallas(h_bf16, logits, w_gu, b_gu, w_down, b_down):
    posT, gateT, offs = moe_route_pallas(logits)
    chunk_off = offs[:, ::MOE_TOKENS // ROUTE_TILE]
    return moe_experts_pallas(h_bf16, posT, gateT, chunk_off, w_gu, b_gu, w_down, b_down)


def rms_norm(x, g):
    xf = x.astype(jnp.float32)
    y = xf * lax.rsqrt(jnp.mean(xf * xf, axis=-1, keepdims=True) + NORM_EPS)
    return (y * g.astype(jnp.float32)).astype(x.dtype)


def axial_rope(n_tok):
    rows = n_tok // GRID_W
    row = jnp.repeat(jnp.arange(rows, dtype=jnp.float32), GRID_W)
    col = jnp.tile(jnp.arange(GRID_W, dtype=jnp.float32), rows)
    n_freq = HEAD_DIM // 4
    inv = ROPE_THETA ** (-jnp.arange(n_freq, dtype=jnp.float32) / n_freq)
    ang = jnp.concatenate([row[:, None] * inv, col[:, None] * inv], axis=-1)
    return jnp.cos(ang), jnp.sin(ang)


def apply_rope(x, cos, sin):
    xf = x.astype(jnp.float32).reshape(x.shape[:-1] + (HEAD_DIM // 2, 2))
    x1, x2 = xf[..., 0], xf[..., 1]
    out = jnp.stack([x1 * cos - x2 * sin, x1 * sin + x2 * cos], axis=-1)
    return out.reshape(x.shape).astype(x.dtype)


def block_attention(q, k, v):
    b, hq, tq, d = q.shape
    hkv = k.shape[1]
    rep = hq // hkv
    nblk = tq // Q_BLOCK
    qb = q.reshape(b, hkv, rep, nblk, Q_BLOCK, d).transpose(3, 0, 1, 2, 4, 5)
    scale = 1.0 / math.sqrt(d)

    def one_block(qblk):
        s = jnp.einsum('bgrqd,bgkd->bgrqk', qblk, k, preferred_element_type=jnp.float32) * scale
        pr = jax.nn.softmax(s, axis=-1)
        return jnp.einsum('bgrqk,bgkd->bgrqd', pr.astype(v.dtype), v)

    o = lax.map(one_block, qb)
    return o.transpose(1, 2, 3, 0, 4, 5).reshape(b, hq, tq, d)


def rwkv7_scan(r, w, k, v, a_, b_, s0):
    xs = tuple(jnp.moveaxis(t, 1, 0) for t in (r, w, k, v, a_, b_))

    def step(S, inp):
        rt, wt, kt, vt, at, bt = inp
        sa = jnp.einsum('bhvk,bhk->bhv', S, at)
        S = S * wt[:, :, None, :] + sa[..., None] * bt[:, :, None, :] + vt[..., None] * kt[:, :, None, :]
        return S, jnp.einsum('bhvk,bhk->bhv', S, rt)

    s_last, ys = lax.scan(step, s0, xs)
    return jnp.moveaxis(ys, 0, 1), s_last


def head_group_norm(y, g, b):
    mu = jnp.mean(y, axis=-1, keepdims=True)
    yc = y - mu
    var = jnp.mean(yc * yc, axis=-1, keepdims=True)
    yn = (yc * lax.rsqrt(var + RW_GN_EPS)).reshape(y.shape[0], y.shape[1], -1)
    return yn * g.astype(jnp.float32) + b.astype(jnp.float32)


def rwkv7_mixer(r, k, v, wd, ad, gd, p, s0):
    B, T, _ = r.shape
    f32 = jnp.float32
    heads = lambda t: t.astype(f32).reshape(B, T, RW_HEADS, HEAD_DIM)
    r, k, v = heads(r), heads(k), heads(v)
    wd = wd.astype(f32).reshape(B, T, 2, RW_DECAY_LORA)
    ad = ad.astype(f32).reshape(B, T, 2, RW_AAA_LORA)
    kk = k * p['rw_kk'].astype(f32).reshape(RW_HEADS, HEAD_DIM)
    kk = kk * lax.rsqrt(jnp.sum(kk * kk, axis=-1, keepdims=True) + 1e-12)
    g = jax.nn.sigmoid(gd.astype(f32)) @ p['rw_g2'].astype(f32)
    ka = p['rw_ka'].astype(f32).reshape(RW_HEADS, HEAD_DIM)
    rk = p['rw_rk'].astype(f32)
    ys, bonuses, states = [], [], []
    for d in range(2):
        w_raw = p['rw_w0'][d].astype(f32) + jnp.tanh(wd[:, :, d]) @ p['rw_w2'][d].astype(f32)
        decay = heads(jnp.exp(-jnp.exp(-jax.nn.softplus(-w_raw) - 0.5)))
        a = heads(jax.nn.sigmoid(p['rw_a0'][d].astype(f32) + ad[:, :, d] @ p['rw_a2'][d].astype(f32)))
        kd = k * (1.0 + (a - 1.0) * ka)
        seq = (r, decay, kd, v, -kk, kk * a)
        if d == 1:
            seq = tuple(jnp.flip(t, axis=1) for t in seq)
        y, s_last = rwkv7_scan(*seq, s0[:, d].astype(f32))
        if d == 1:
            y = jnp.flip(y, axis=1)
        ys.append(y)
        bonuses.append(jnp.sum(r * kd * rk, axis=-1, keepdims=True) * v)
        states.append(s_last)
    y = head_group_norm(ys[0] + ys[1], p['rw_gn_g'], p['rw_gn_b']) + (bonuses[0] + bonuses[1]).reshape(B, T, RW_WIDTH)
    return y * g, jnp.stack(states, axis=1)


def chunk_gla(q, k, v, logf, s0):
    B, H, T, dk = q.shape
    dv = v.shape[-1]
    n = T // HG_CHUNK
    rs = lambda t: t.reshape(B, H, n, HG_CHUNK, t.shape[-1])
    q, k, v, logf = rs(q), rs(k), rs(v), rs(logf)
    b = jnp.cumsum(logf, axis=3)
    causal = jnp.tril(jnp.ones((HG_CHUNK, HG_CHUNK), dtype=bool))
    diff = b[:, :, :, :, None, :] - b[:, :, :, None, :, :]
    dec = jnp.where(causal[:, :, None], jnp.exp(jnp.minimum(diff, 0.0)), 0.0)
    att = jnp.einsum('bhntd,bhnsd,bhntsd->bhnts', q, k, dec)
    o_intra = jnp.einsum('bhnts,bhnsv->bhntv', att, v)
    b_last = b[:, :, :, -1, :]
    kv = jnp.einsum('bhnsd,bhnsv->bhndv', k * jnp.exp(b_last[:, :, :, None, :] - b), v)

    def step(S, inp):
        dl, kvn = inp
        return dl[..., None] * S + kvn, S

    s_last, s_before = lax.scan(step, s0, (jnp.moveaxis(jnp.exp(b_last), 2, 0), jnp.moveaxis(kv, 2, 0)))
    s_before = jnp.moveaxis(s_before, 0, 2)
    o_inter = jnp.einsum('bhntd,bhndv->bhntv', q * jnp.exp(b), s_before)
    return (o_intra + o_inter).reshape(B, H, T, dv), s_last


def hgrn2_mixer(q, f_raw, i, g, lb, norm_g, s0):
    B, T, _ = q.shape
    f32 = jnp.float32
    heads = lambda t: t.astype(f32).reshape(B, T, HG_HEADS, HEAD_DIM).transpose(0, 2, 1, 3)
    qh = heads(jax.nn.silu(q.astype(f32)))
    vh = heads(i)
    f_raw = f_raw.astype(f32).reshape(B, T, 2, HG_WIDTH)
    outs, states = [], []
    for d in range(2):
        lbd = lb[d]
        f = lbd + (1.0 - lbd) * jax.nn.sigmoid(f_raw[:, :, d])
        logf = jnp.log(jnp.maximum(f, HG_F_MIN))
        kf = 1.0 - f
        seq = (qh, heads(kf), vh, heads(logf))
        if d == 1:
            seq = tuple(jnp.flip(t, axis=2) for t in seq)
        o, s_last = chunk_gla(*seq, s0[:, d].astype(f32))
        if d == 1:
            o = jnp.flip(o, axis=2)
        outs.append(o)
        states.append(s_last)
    o = rms_norm(outs[0] + outs[1], norm_g).transpose(0, 2, 1, 3).reshape(B, T, HG_WIDTH)
    return o * jax.nn.silu(g.astype(f32)), jnp.stack(states, axis=1)


def moe(h, p):
    T, D = h.shape
    logits = (h @ p['router_w']).astype(jnp.float32) + p['router_b'].astype(jnp.float32)
    top_logit, top_idx = lax.top_k(logits, TOP_K)
    gates = jax.nn.softmax(top_logit, axis=-1)
    A = T * TOP_K
    flat_e = top_idx.reshape(A)
    flat_tok = jnp.arange(A, dtype=jnp.int32) // TOP_K
    flat_g = gates.reshape(A)
    order = jnp.argsort(flat_e)
    se = flat_e[order]
    counts = jnp.bincount(flat_e, length=N_EXPERTS)
    starts = jnp.cumsum(counts) - counts
    padded = (counts + MOE_BLOCK - 1) // MOE_BLOCK * MOE_BLOCK
    pends = jnp.cumsum(padded)
    pstarts = pends - padded
    dest = pstarts[se] + jnp.arange(A, dtype=jnp.int32) - starts[se]
    n_blocks = -(-(A + N_EXPERTS * (MOE_BLOCK - 1)) // MOE_BLOCK)
    n_slots = n_blocks * MOE_BLOCK
    slot_tok = jnp.full((n_slots,), T, dtype=jnp.int32).at[dest].set(flat_tok[order])
    slot_g = jnp.zeros((n_slots,), dtype=jnp.float32).at[dest].set(flat_g[order])
    block_e = jnp.clip(jnp.searchsorted(pends, jnp.arange(n_blocks, dtype=jnp.int32) * MOE_BLOCK, side='right'), 0, N_EXPERTS - 1)
    h_pad = jnp.concatenate([h, jnp.zeros((1, D), h.dtype)], axis=0)
    xb = h_pad[slot_tok].reshape(n_blocks, MOE_BLOCK, D)

    def expert_block(args):
        xblk, e = args
        gu = xblk @ p['moe_w_gu'][e] + p['moe_b_gu'][e]
        glu, lin = gu[:, :EXPERT_FF], gu[:, EXPERT_FF:]
        glu = jnp.minimum(glu, SWIGLU_LIMIT)
        lin = jnp.clip(lin, -SWIGLU_LIMIT, SWIGLU_LIMIT)
        act = glu * jax.nn.sigmoid(SWIGLU_ALPHA * glu) * (lin + 1.0)
        return act @ p['moe_w_down'][e] + p['moe_b_down'][e]

    yb = lax.map(expert_block, (xb, block_e)).reshape(n_slots, D)
    y = jax.ops.segment_sum(yb * slot_g[:, None].astype(yb.dtype), slot_tok, num_segments=T + 1)[:T]
    return y.astype(h.dtype)


def token_mixers(h, p, lb, cache):
    B, T, _ = h.shape
    proj = h @ p['w_in']
    split_points = np.cumsum(IN_SIZES)[:-1].tolist()
    (aq, ak, av, rr, rk, rv, rwd, rad, rgd, hq, hf, hi, hg) = jnp.split(proj, split_points, axis=-1)
    qh = rms_norm(aq.reshape(B, T, ATT_HEADS, HEAD_DIM), p['att_qnorm_g']).transpose(0, 2, 1, 3)
    kh = rms_norm(ak.reshape(B, T, ATT_KV_HEADS, HEAD_DIM), p['att_knorm_g']).transpose(0, 2, 1, 3)
    vh = av.reshape(B, T, ATT_KV_HEADS, HEAD_DIM).transpose(0, 2, 1, 3)
    if cache is None:
        att = block_attention(qh, kh, vh)
        rw_s0 = jnp.zeros((B, 2, RW_HEADS, HEAD_DIM, HEAD_DIM), jnp.float32)
        hg_s0 = jnp.zeros((B, 2, HG_HEADS, HEAD_DIM, HEAD_DIM), jnp.float32)
    else:
        ck, cv, rw_s0, hg_s0 = cache
        cos, sin = axial_rope(T)
        qr = apply_rope(qh, cos, sin)
        kr = apply_rope(kh, cos, sin)
        att = block_attention(qr, jnp.concatenate([ck.astype(kr.dtype), kr], axis=2), jnp.concatenate([cv.astype(vh.dtype), vh], axis=2))
    att = att.transpose(0, 2, 1, 3).reshape(B, T, ATT_WIDTH)
    rw_out, rw_state = rwkv7_mixer_pallas(proj[:, :, 768:1920], rw_s0.astype(F32), p)
    hg_out, hg_state = hgrn2_mixer_pallas(proj[:, :, 1920:3200], hg_s0.astype(F32), lb, p['hg_norm_g'])
    mix = jnp.concatenate([att.astype(h.dtype), rw_out.astype(h.dtype), hg_out.astype(h.dtype)], axis=-1)
    new_cache = (kh, vh, rw_state, hg_state) if cache is None else None
    return mix, new_cache


def layer(x, mod, p, lb, cache):
    shift1, scale1, gate1, shift2, scale2, gate2 = jnp.split(mod[:, None, :], 6, axis=-1)
    h = rms_norm(x, p['norm_mix_g']) * (1.0 + scale1) + shift1
    mix, new_cache = token_mixers(h, p, lb, cache)
    x = x + gate1 * (mix @ p['w_out'])
    h = rms_norm(x, p['norm_ffn_g']) * (1.0 + scale2) + shift2
    B, T, D = h.shape
    x = x + gate2 * moe(h.reshape(B * T, D), p).reshape(B, T, D)
    return x, new_cache


def hgrn_lower_bounds(hg_lb):
    sm = jax.nn.softmax(hg_lb.astype(jnp.float32), axis=0)
    return jnp.cumsum(sm, axis=0) - sm[0:1]


def kernel(x_prompt, x_sample, cache_att_k, cache_att_v, state_rwkv, state_hgrn, c, c_ctx, w_mod, b_mod, norm_mix_g, norm_ffn_g, w_in, w_out, att_qnorm_g, att_knorm_g, rw_w0, rw_w2, rw_a0, rw_a2, rw_g2, rw_kk, rw_ka, rw_rk, rw_gn_g, rw_gn_b, hg_lb, hg_norm_g, router_w, router_b, moe_w_gu, moe_b_gu, moe_w_down, moe_b_down, final_norm_g):
    BP, TP, _ = x_prompt.shape
    BS, TS, _ = x_sample.shape
    lb_all = hgrn_lower_bounds(hg_lb)
    cvec = jnp.concatenate([c_ctx[None, :], c, jnp.zeros((8 - 1 - BS, D_MODEL), F32)], axis=0)
    mod_all = adaln_mod_pallas(cvec, w_mod, b_mod).reshape(DEPTH, 8, 6, D_MODEL)
    zeros_state = jnp.zeros((BP, 2, RW_HEADS, HEAD_DIM, HEAD_DIM), F32)
    x = {'p': x_prompt.reshape(BP * TP, D_MODEL), 's': x_sample.reshape(BS * TS, D_MODEL)}
    dims = {'p': (TP, BP), 's': (TS, BS)}
    y_moe = {'p': None, 's': None}
    mod_prev = {'p': None, 's': None}
    ks, vs, srs, shs = [], [], [], []
    for l in range(DEPTH):
        prm = dict(rw_w0=rw_w0[l], rw_w2=rw_w2[l], rw_a0=rw_a0[l], rw_a2=rw_a2[l], rw_g2=rw_g2[l],
                   rw_kk=rw_kk[l], rw_ka=rw_ka[l], rw_rk=rw_rk[l], rw_gn_g=rw_gn_g[l], rw_gn_b=rw_gn_b[l])
        w_in_bf, w_out_bf = w_in[l].astype(BF16), w_out[l].astype(BF16)
        w_gu_bf, w_dn_bf = moe_w_gu[l].astype(BF16), moe_w_down[l].astype(BF16)
        mods = {'p': mod_all[l, 0:1], 's': mod_all[l, 1:1 + BS]}
        for s in ('p', 's'):
            T, B = dims[s]
            rpm = B * T if s == 'p' else T
            if l == 0:
                p_att, p_rw, p_hg = in_proj_pallas(x[s], mods[s], norm_mix_g[l], w_in_bf, rpm)
            else:
                x[s], p_att, p_rw, p_hg = in_proj_pallas(x[s], mods[s], norm_mix_g[l], w_in_bf, rpm,
                                                         res=(y_moe[s], mod_prev[s]))
            p_att, p_rw, p_hg = (t.reshape(B, T, -1) for t in (p_att, p_rw, p_hg))
            if s == 'p':
                att, k_l, v_l = attention_pallas(p_att, att_qnorm_g[l], att_knorm_g[l])
                rw_out, sr_l = rwkv7_mixer_pallas(p_rw, zeros_state, prm)
                hg_out, sh_l = hgrn2_mixer_pallas(p_hg, zeros_state, lb_all[l], hg_norm_g[l])
                ks.append(k_l)
                vs.append(v_l)
                srs.append(sr_l)
                shs.append(sh_l)
            else:
                att = attention_pallas(p_att, att_qnorm_g[l], att_knorm_g[l], (cache_att_k[:, l], cache_att_v[:, l]))
                rw_out, _ = rwkv7_mixer_pallas(p_rw, state_rwkv[:, l], prm)
                hg_out, _ = hgrn2_mixer_pallas(p_hg, state_hgrn[:, l], lb_all[l], hg_norm_g[l])
            x[s], h2, logits = out_proj_pallas(att.reshape(B * T, -1), rw_out.reshape(B * T, -1),
                                               hg_out.reshape(B * T, -1), x[s], mods[s], norm_ffn_g[l], w_out_bf,
                                               router_w[l], router_b[l], rpm)
            y_moe[s] = moe_pallas(h2, logits, w_gu_bf, moe_b_gu[l], w_dn_bf, moe_b_down[l])
            mod_prev[s] = mods[s]
    y_prompt = final_norm_pallas(x['p'], y_moe['p'], mod_prev['p'], final_norm_g, BP * TP).reshape(x_prompt.shape)
    y_sample = final_norm_pallas(x['s'], y_moe['s'], mod_prev['s'], final_norm_g, TS).reshape(x_sample.shape)
    return (y_prompt, y_sample, jnp.stack(ks, axis=1), jnp.stack(vs, axis=1),
            jnp.stack(srs, axis=1), jnp.stack(shs, axis=1))
```

```python
import math
from functools import partial

import jax
import jax.numpy as jnp
import numpy as np
from jax import lax
from jax.experimental import pallas as pl
from jax.experimental.pallas import tpu as pltpu

D_MODEL = 1024
DEPTH = 2
GRID_W = 64
HEAD_DIM = 64
ATT_HEADS = 8
ATT_KV_HEADS = 2
ATT_WIDTH = ATT_HEADS * HEAD_DIM
KV_WIDTH = ATT_KV_HEADS * HEAD_DIM
RW_HEADS = 4
RW_WIDTH = RW_HEADS * HEAD_DIM
RW_DECAY_LORA = 64
RW_AAA_LORA = 64
RW_GATE_LORA = 128
RW_GN_EPS = 64e-5
HG_HEADS = 4
HG_WIDTH = HG_HEADS * HEAD_DIM
HG_CHUNK = 32
HG_F_MIN = 1e-6
IN_SIZES = (ATT_WIDTH, KV_WIDTH, KV_WIDTH, RW_WIDTH, RW_WIDTH, RW_WIDTH, 2 * RW_DECAY_LORA, 2 * RW_AAA_LORA,
            RW_GATE_LORA, HG_WIDTH, 2 * HG_WIDTH, HG_WIDTH, HG_WIDTH)
N_EXPERTS = 32
TOP_K = 4
EXPERT_FF = D_MODEL
SWIGLU_LIMIT = 7.0
SWIGLU_ALPHA = 1.702
MOE_BLOCK = 128
Q_BLOCK = 128
ROPE_THETA = 10000.0
NORM_EPS = 1e-6

RW_CHUNK = 64
BF16 = jnp.bfloat16
F32 = jnp.float32

_NN = (((1,), (0,)), ((), ()))
_NT = (((1,), (1,)), ((), ()))
_TN = (((0,), (0,)), ((), ()))


def _split(x, n):
    parts = []
    for _ in range(n - 1):
        hi = x.astype(BF16)
        parts.append(hi)
        x = x - hi.astype(F32)
    parts.append(x.astype(BF16))
    return parts


def _mm(a, b, dims=_NN, passes=1):
    d = lambda x, y: lax.dot_general(x, y, dims, preferred_element_type=F32)
    if passes == 1:
        return d(a.astype(BF16), b.astype(BF16))
    ah, al = _split(a, 2)
    bh, bl = _split(b, 2)
    return d(ah, bl) + d(al, bh) + d(ah, bh)


def _mm_exact_lhs(a01, b, n=3):
    a = a01.astype(BF16)
    out = None
    for t in reversed(_split(b, n)):
        y = lax.dot_general(a, t, _NN, preferred_element_type=F32)
        out = y if out is None else out + y
    return out


def _mm_exact_rhs(a, b01, n=3):
    b = b01.astype(BF16)
    out = None
    for t in reversed(_split(a, n)):
        y = lax.dot_general(t, b, _NN, preferred_element_type=F32)
        out = y if out is None else out + y
    return out


def _head_blockdiag(width):
    r = lax.broadcasted_iota(jnp.int32, (width, width), 0) // HEAD_DIM
    c = lax.broadcasted_iota(jnp.int32, (width, width), 1) // HEAD_DIM
    return (r == c).astype(F32)


def _sigmoid(x):
    return 1.0 / (1.0 + jnp.exp(-x))


def _softplus(x):
    return jnp.maximum(x, 0.0) + jnp.log(1.0 + jnp.exp(-jnp.abs(x)))


def _rwkv_kernel(p_ref, s0_ref, w0_ref, w2_ref, a0_ref, a2_ref, g2_ref, kk_ref, ka_ref, rk_ref, gng_ref, gnb_ref,
                 out_ref, st_ref,
                 lw_scr, kd_scr, bb_scr, y_scr, kk_scr, s_scr, *, T, NB):
    C = RW_CHUNK
    n_chunks = T // C
    bd = _head_blockdiag(RW_WIDTH)
    seg = lambda t: _mm_exact_rhs(t, bd)
    ka = ka_ref[...]
    for nb in range(NB):
        k = p_ref[nb, :, 256:512]
        kk = k * kk_ref[...]
        kk = kk * lax.rsqrt(seg(kk * kk) + 1e-12)
        kk_scr[nb] = kk
        for d in range(2):
            wd = p_ref[nb, :, 768 + 64 * d:832 + 64 * d]
            ad = p_ref[nb, :, 896 + 64 * d:960 + 64 * d]
            w_raw = w0_ref[d:d + 1, :] + _mm(jnp.tanh(wd), w2_ref[d], passes=3)
            lw_scr[nb, d] = -jnp.exp(-_softplus(-w_raw) - 0.5)
            a = _sigmoid(a0_ref[d:d + 1, :] + _mm(ad, a2_ref[d], passes=3))
            kd_scr[nb, d] = k * (1.0 + (a - 1.0) * ka)
            bb_scr[nb, d] = kk * a
    s_scr[...] = s0_ref[...]

    ti = lax.broadcasted_iota(jnp.int32, (C, C), 0)
    si = lax.broadcasted_iota(jnp.int32, (C, C), 1)
    ones_cc = jnp.ones((C, C), F32)

    def chunk_body(i, carry):
        ch = []
        for nb, d in [(nb, d) for nb in range(NB) for d in range(2)]:
            ci = i if d == 0 else n_chunks - 1 - i
            rows = pl.ds(pl.multiple_of(ci * C, C), C)
            strict = (ti > si) if d == 0 else (ti < si)
            incl = (ti >= si) if d == 0 else (ti <= si)
            lw = lw_scr[nb, d, rows, :]
            cum = _mm_exact_lhs(incl.astype(F32), lw)
            total = _mm_exact_lhs(ones_cc, lw)
            cum_ex = cum - lw
            mid = 0.5 * total
            rr = p_ref[nb, rows, 0:256]
            vv = p_ref[nb, rows, 512:768]
            kdc = kd_scr[nb, d, rows, :]
            bbc = bb_scr[nb, d, rows, :]
            kkc = kk_scr[nb, rows, :]
            e_inv = jnp.exp(mid - cum)
            At = -kkc * jnp.exp(cum_ex - mid)
            Rt = rr * jnp.exp(cum - mid)
            Bt = bbc * e_inv
            Kt = kdc * e_inv
            Ap = -kkc * jnp.exp(cum_ex)
            Rp = rr * jnp.exp(cum)
            e_out = jnp.exp(total - cum)
            Bh = bbc * e_out
            Kh = kdc * e_out
            e_tot = jnp.exp(total[0:1, :])
            for h in range(RW_HEADS):
                hs = slice(h * HEAD_DIM, (h + 1) * HEAD_DIM)
                ch.append(dict(nb=nb, d=d, h=h, rows=rows, hs=hs, strict=strict, incl=incl,
                               AR=jnp.concatenate([At[:, hs], Rt[:, hs]], axis=0), Bt=Bt[:, hs], Kt=Kt[:, hs],
                               V=vv[:, hs], X1=Ap[:, hs], Rp=Rp[:, hs], Bh=Bh[:, hs], Kh=Kh[:, hs],
                               e_tot=e_tot[:, hs]))
        for c in ch:
            c['AB'] = _mm(c['AR'], c['Bt'], _NT)
            c['AK'] = _mm(c['AR'], c['Kt'], _NT)
        for c in ch:
            c['P'] = jnp.where(c['strict'], c['AB'][:C], 0.0)
            c['A_ak'] = jnp.where(c['strict'], c['AK'][:C], 0.0)
            c['A_rb'] = jnp.where(c['incl'], c['AB'][C:], 0.0)
            c['A_rk'] = jnp.where(c['incl'], c['AK'][C:], 0.0)
        for c in ch:
            c['X2'] = _mm(c['A_ak'], c['V'])
        for lvl in range(6):
            for c in ch:
                if lvl < 5:
                    c['PZ'] = _mm(c['P'], jnp.concatenate([c['P'], c['X1'], c['X2']], axis=1))
                else:
                    c['PZ'] = _mm(c['P'], jnp.concatenate([c['X1'], c['X2']], axis=1))
            for c in ch:
                PZ = c['PZ']
                if lvl < 5:
                    c['P'] = PZ[:, :C]
                    c['X1'] = c['X1'] + PZ[:, C:2 * C]
                    c['X2'] = c['X2'] + PZ[:, 2 * C:]
                else:
                    c['X1'] = c['X1'] + PZ[:, :C]
                    c['X2'] = c['X2'] + PZ[:, C:]
        for c in ch:
            c['S0'] = s_scr[c['nb'], c['d'], c['h']]
            c['UY'] = _mm(jnp.concatenate([c['X1'], c['Rp']], axis=0), c['S0'], _NT)
        for c in ch:
            c['U'] = c['UY'][:C] + c['X2']
        for c in ch:
            c['Y'] = c['UY'][C:] + _mm(c['A_rb'], c['U']) + _mm(c['A_rk'], c['V'])
            c['S1'] = c['S0'] * c['e_tot'] + _mm(c['U'], c['Bh'], _TN) + _mm(c['V'], c['Kh'], _TN)
        for c in ch:
            s_scr[c['nb'], c['d'], c['h']] = c['S1']
            y_scr[c['nb'], c['d'], c['rows'], c['hs']] = c['Y']
        return carry

    lax.fori_loop(0, n_chunks, chunk_body, 0)

    for nb in range(NB):
        r = p_ref[nb, :, 0:256]
        v = p_ref[nb, :, 512:768]
        bonus = seg(r * (kd_scr[nb, 0] + kd_scr[nb, 1]) * rk_ref[...]) * v
        g = _mm(_sigmoid(p_ref[nb, :, 1024:1152]), g2_ref[...], passes=3)
        y = y_scr[nb, 0] + y_scr[nb, 1]
        mu = seg(y) * (1.0 / HEAD_DIM)
        yc = y - mu
        var = seg(yc * yc) * (1.0 / HEAD_DIM)
        yn = yc * lax.rsqrt(var + RW_GN_EPS)
        out_ref[nb] = (yn * gng_ref[...] + gnb_ref[...] + bonus) * g
    st_ref[...] = s_scr[...]


RW_NB = 2


def rwkv7_mixer_pallas(p_rw, s0, prm):
    B, T, W = p_rw.shape
    NB = RW_NB
    row = lambda a: a.reshape(1, RW_WIDTH)
    full = lambda shape: pl.BlockSpec(shape, lambda b: (0,) * len(shape))
    st_spec = pl.BlockSpec((NB, 2, RW_HEADS, HEAD_DIM, HEAD_DIM), lambda b: (b, 0, 0, 0, 0))
    return pl.pallas_call(
        partial(_rwkv_kernel, T=T, NB=NB),
        grid=(B // NB,),
        in_specs=[pl.BlockSpec((NB, T, W), lambda b: (b, 0, 0)), st_spec,
                  full((2, RW_WIDTH)), full((2, 64, RW_WIDTH)), full((2, RW_WIDTH)), full((2, 64, RW_WIDTH)),
                  full((128, RW_WIDTH)), full((1, RW_WIDTH)), full((1, RW_WIDTH)), full((1, RW_WIDTH)),
                  full((1, RW_WIDTH)), full((1, RW_WIDTH))],
        out_specs=[pl.BlockSpec((NB, T, RW_WIDTH), lambda b: (b, 0, 0)), st_spec],
        out_shape=[jax.ShapeDtypeStruct((B, T, RW_WIDTH), F32),
                   jax.ShapeDtypeStruct((B, 2, RW_HEADS, HEAD_DIM, HEAD_DIM), F32)],
        scratch_shapes=[pltpu.VMEM((NB, 2, T, RW_WIDTH), F32)] * 4
        + [pltpu.VMEM((NB, T, RW_WIDTH), F32), pltpu.VMEM((NB, 2, RW_HEADS, HEAD_DIM, HEAD_DIM), F32)],
        compiler_params=pltpu.CompilerParams(dimension_semantics=("arbitrary",), vmem_limit_bytes=56 * 1024 * 1024),
        name="rwkv7_mixer",
    )(p_rw, s0, prm['rw_w0'], prm['rw_w2'], prm['rw_a0'], prm['rw_a2'], prm['rw_g2'], row(prm['rw_kk']),
      row(prm['rw_ka']), row(prm['rw_rk']), row(prm['rw_gn_g']), row(prm['rw_gn_b']))


def _rms_norm_kernel(x_ref, g_ref, o_ref):
    x = x_ref[...]
    y = x * lax.rsqrt(jnp.mean(x * x, axis=-1, keepdims=True) + NORM_EPS)
    o_ref[...] = y * g_ref[...]


def rms_norm_rows(x, g, block_rows=512):
    R, D = x.shape
    return pl.pallas_call(
        _rms_norm_kernel,
        grid=(R // block_rows,),
        in_specs=[pl.BlockSpec((block_rows, D), lambda i: (i, 0)),
                  pl.BlockSpec((1, D), lambda i: (0, 0))],
        out_specs=pl.BlockSpec((block_rows, D), lambda i: (i, 0)),
        out_shape=jax.ShapeDtypeStruct((R, D), jnp.float32),
        name="final_rms_norm",
    )(x, g.reshape(1, D))


HG_SUB = 16
HG_ROWS = 64


def _hgrn_kernel(p_ref, s0_ref, lb_ref, ng_ref, out_ref, st_ref, lf_scr, kf_scr, o_scr, s_scr, *, T):
    R, c = HG_ROWS, HG_SUB
    n_it = T // R
    x = p_ref[0]
    bd = _head_blockdiag(HG_WIDTH)
    seg = lambda t: _mm_exact_rhs(t, bd)
    for d in range(2):
        lbd = lb_ref[d:d + 1, :]
        f = lbd + (1.0 - lbd) * _sigmoid(x[:, 256 + 256 * d:512 + 256 * d])
        lf_scr[d] = jnp.log(jnp.maximum(f, HG_F_MIN))
        kf_scr[d] = 1.0 - f
        for h in range(HG_HEADS):
            s_scr[d, h] = s0_ref[0, d, h].T

    ti = lax.broadcasted_iota(jnp.int32, (R, R), 0)
    si = lax.broadcasted_iota(jnp.int32, (R, R), 1)
    same_blk = (ti // c) == (si // c)
    t16 = lax.broadcasted_iota(jnp.int32, (c, 1), 0)

    def body(i, carry):
        for d in range(2):
            ci = i if d == 0 else n_it - 1 - i
            rows = pl.ds(pl.multiple_of(ci * R, R), R)
            incl = (ti >= si) if d == 0 else (ti <= si)
            lf = lf_scr[d, rows, :]
            cum = _mm_exact_lhs((incl & same_blk).astype(F32), lf)
            tot = _mm_exact_lhs(same_blk.astype(F32), lf)
            xq = p_ref[0, rows, 0:256]
            q = xq * _sigmoid(xq)
            v = p_ref[0, rows, 768:1024]
            kf = kf_scr[d, rows, :]
            Qp = q * jnp.exp(cum)
            Kh = kf * jnp.exp(tot - cum)
            e_tot = jnp.exp(tot)
            blocks = range(R // c) if d == 0 else range(R // c - 1, -1, -1)
            o_parts = [None] * (R // c)
            for j in blocks:
                rs = slice(j * c, (j + 1) * c)
                cb, qb, kb, vb = cum[rs], q[rs], kf[rs], v[rs]
                prods = []
                for s in range(c):
                    e = jnp.exp(jnp.minimum(cb - cb[s:s + 1, :], 0.0))
                    prods.append(qb * (kb[s:s + 1, :] * e))
                att = _mm_exact_rhs(jnp.concatenate(prods, axis=0), bd, n=1)
                o_blk = jnp.zeros((c, HG_WIDTH), F32)
                for s in range(c):
                    keep = (t16 >= s) if d == 0 else (t16 <= s)
                    o_blk = o_blk + jnp.where(keep, att[s * c:(s + 1) * c], 0.0) * vb[s:s + 1, :]
                o_heads = []
                for h in range(HG_HEADS):
                    hs = slice(h * HEAD_DIM, (h + 1) * HEAD_DIM)
                    ST = s_scr[d, h]
                    o_heads.append(_mm(Qp[rs, hs], ST, _NT))
                    s_scr[d, h] = ST * e_tot[j * c:j * c + 1, hs] + _mm(vb[:, hs], Kh[rs, hs], _TN)
                o_parts[j] = o_blk + jnp.concatenate(o_heads, axis=1)
            o_scr[d, rows, :] = jnp.concatenate(o_parts, axis=0)
        return carry

    lax.fori_loop(0, n_it, body, 0)

    o = o_scr[0] + o_scr[1]
    o = o * lax.rsqrt(seg(o * o) * (1.0 / HEAD_DIM) + NORM_EPS) * ng_ref[...]
    gg = x[:, 1024:1280]
    out_ref[0] = o * (gg * _sigmoid(gg))
    for d in range(2):
        for h in range(HG_HEADS):
            st_ref[0, d, h] = s_scr[d, h].T


def hgrn2_mixer_pallas(p_hg, s0, lb, norm_g):
    B, T, W = p_hg.shape
    full = lambda shape: pl.BlockSpec(shape, lambda b: (0,) * len(shape))
    st_spec = pl.BlockSpec((1, 2, HG_HEADS, HEAD_DIM, HEAD_DIM), lambda b: (b, 0, 0, 0, 0))
    return pl.pallas_call(
        partial(_hgrn_kernel, T=T),
        grid=(B,),
        in_specs=[pl.BlockSpec((1, T, W), lambda b: (b, 0, 0)), st_spec, full((2, HG_WIDTH)), full((1, HG_WIDTH))],
        out_specs=[pl.BlockSpec((1, T, HG_WIDTH), lambda b: (b, 0, 0)), st_spec],
        out_shape=[jax.ShapeDtypeStruct((B, T, HG_WIDTH), F32),
                   jax.ShapeDtypeStruct((B, 2, HG_HEADS, HEAD_DIM, HEAD_DIM), F32)],
        scratch_shapes=[pltpu.VMEM((2, T, HG_WIDTH), F32)] * 3
        + [pltpu.VMEM((2, HG_HEADS, HEAD_DIM, HEAD_DIM), F32)],
        compiler_params=pltpu.CompilerParams(dimension_semantics=("arbitrary",), vmem_limit_bytes=48 * 1024 * 1024),
        name="hgrn2_mixer",
    )(p_hg, s0, lb, jnp.tile(norm_g.reshape(1, HEAD_DIM), (1, HG_HEADS)))


ATT_REP = ATT_HEADS // ATT_KV_HEADS
ATT_QROWS = 128


def _swap_pairs(x):
    w = x.shape[-1]
    lane = lax.broadcasted_iota(jnp.int32, x.shape, x.ndim - 1)
    return jnp.where(lane % 2 == 0, pltpu.roll(x, w - 1, x.ndim - 1), pltpu.roll(x, 1, x.ndim - 1))


def _att_kernel(*refs, T, past, rope):
    if rope:
        p_ref, qg_ref, kg_ref, cos_ref, sin_ref, ck_ref, cv_ref, out_ref, k_scr, v_scr, q_scr = refs
    else:
        p_ref, qg_ref, kg_ref, out_ref, kh_ref, vh_ref, k_scr, v_scr, q_scr = refs
    x = p_ref[0]
    q = x[:, 0:ATT_WIDTH]
    k = x[:, ATT_WIDTH:ATT_WIDTH + KV_WIDTH]
    v = x[:, ATT_WIDTH + KV_WIDTH:ATT_WIDTH + 2 * KV_WIDTH]
    inv_d = 1.0 / HEAD_DIM
    q = q * lax.rsqrt(_mm_exact_rhs(q * q, _head_blockdiag(ATT_WIDTH)) * inv_d + NORM_EPS) * qg_ref[...]
    k = k * lax.rsqrt(_mm_exact_rhs(k * k, _head_blockdiag(KV_WIDTH)) * inv_d + NORM_EPS) * kg_ref[...]
    if rope:
        cos, sin = cos_ref[...], sin_ref[...]
        rep = ATT_WIDTH // KV_WIDTH
        q = q * jnp.concatenate([cos] * rep, axis=1) + _swap_pairs(q) * jnp.concatenate([sin] * rep, axis=1)
        k = k * cos + _swap_pairs(k) * sin
    q_scr[...] = (q * (1.0 / math.sqrt(HEAD_DIM))).astype(BF16)
    for g in range(ATT_KV_HEADS):
        gs = slice(g * HEAD_DIM, (g + 1) * HEAD_DIM)
        if rope:
            k_scr[g, 0:past, :] = ck_ref[0, g].astype(BF16)
            v_scr[g, 0:past, :] = cv_ref[0, g].astype(BF16)
        else:
            kh_ref[0, g] = k[:, gs]
            vh_ref[0, g] = v[:, gs]
        k_scr[g, past:past + T, :] = k[:, gs].astype(BF16)
        v_scr[g, past:past + T, :] = v[:, gs].astype(BF16)
    QR = ATT_QROWS

    def q_block(qb, carry):
        rows = pl.ds(pl.multiple_of(qb * QR, QR), QR)
        qblk = q_scr[rows, :]
        for g in range(ATT_KV_HEADS):
            qs = jnp.concatenate([qblk[:, (g * ATT_REP + r) * HEAD_DIM:(g * ATT_REP + r + 1) * HEAD_DIM]
                                  for r in range(ATT_REP)], axis=0)
            s = lax.dot_general(qs, k_scr[g], _NT, preferred_element_type=F32)
            e = jnp.exp(s - jnp.max(s, axis=-1, keepdims=True))
            l = jnp.sum(e, axis=-1, keepdims=True)
            o = lax.dot_general(e.astype(BF16), v_scr[g], _NN, preferred_element_type=F32) / l
            for r in range(ATT_REP):
                h = g * ATT_REP + r
                out_ref[0, rows, h * HEAD_DIM:(h + 1) * HEAD_DIM] = o[r * QR:(r + 1) * QR]
        return carry

    lax.fori_loop(0, T // QR, q_block, 0)


def rope_tables(T):
    rows = T // GRID_W
    row = jnp.repeat(jnp.arange(rows, dtype=F32), GRID_W)
    col = jnp.tile(jnp.arange(GRID_W, dtype=F32), rows)
    n_freq = HEAD_DIM // 4
    inv = ROPE_THETA ** (-jnp.arange(n_freq, dtype=F32) / n_freq)
    ang = jnp.concatenate([row[:, None] * inv, col[:, None] * inv], axis=-1)
    cos = jnp.repeat(jnp.cos(ang), 2, axis=-1)
    sin = jnp.stack([-jnp.sin(ang), jnp.sin(ang)], axis=-1).reshape(T, HEAD_DIM)
    return jnp.tile(cos, (1, ATT_KV_HEADS)), jnp.tile(sin, (1, ATT_KV_HEADS))


def attention_pallas(p_att, qnorm_g, knorm_g, cache=None):
    B, T, W = p_att.shape
    rope = cache is not None
    past = cache[0].shape[2] if rope else 0
    full = lambda shape: pl.BlockSpec(shape, lambda b: (0,) * len(shape))
    qg = jnp.tile(qnorm_g.reshape(1, HEAD_DIM), (1, ATT_HEADS))
    kg = jnp.tile(knorm_g.reshape(1, HEAD_DIM), (1, ATT_KV_HEADS))
    in_specs = [pl.BlockSpec((1, T, W), lambda b: (b, 0, 0)), full((1, ATT_WIDTH)), full((1, KV_WIDTH))]
    args = [p_att, qg, kg]
    out_specs = [pl.BlockSpec((1, T, ATT_WIDTH), lambda b: (b, 0, 0))]
    out_shape = [jax.ShapeDtypeStruct((B, T, ATT_WIDTH), F32)]
    if rope:
        cos, sin = rope_tables(T)
        kv_spec = pl.BlockSpec((1, ATT_KV_HEADS, past, HEAD_DIM), lambda b: (b, 0, 0, 0))
        in_specs += [full((T, KV_WIDTH)), full((T, KV_WIDTH)), kv_spec, kv_spec]
        args += [cos, sin, cache[0], cache[1]]
    else:
        kv_spec = pl.BlockSpec((1, ATT_KV_HEADS, T, HEAD_DIM), lambda b: (b, 0, 0, 0))
        out_specs += [kv_spec, kv_spec]
        out_shape += [jax.ShapeDtypeStruct((B, ATT_KV_HEADS, T, HEAD_DIM), F32)] * 2
    res = pl.pallas_call(
        partial(_att_kernel, T=T, past=past, rope=rope),
        grid=(B,),
        in_specs=in_specs,
        out_specs=out_specs,
        out_shape=out_shape,
        scratch_shapes=[pltpu.VMEM((ATT_KV_HEADS, past + T, HEAD_DIM), BF16)] * 2
        + [pltpu.VMEM((T, ATT_WIDTH), BF16)],
        compiler_params=pltpu.CompilerParams(dimension_semantics=("arbitrary",), vmem_limit_bytes=48 * 1024 * 1024),
        name="attention_rope" if rope else "attention_ctx",
    )(*args)
    return res[0] if rope else tuple(res)


ROW_TILE = 256
MOD_TILE = 1536
ROUTE_TILE = 256
MOE_SLOTS = 256
MOE_TOKENS = 256
P_ATT, P_RW, P_HG = ATT_WIDTH + 2 * KV_WIDTH, 3 * RW_WIDTH + 384, 5 * HG_WIDTH


def _mod_kernel(c_ref, w_ref, b_ref, o_ref):
    c = c_ref[...]
    o_ref[0] = _mm(c * _sigmoid(c), w_ref[0], passes=3) + b_ref[0]


def adaln_mod_pallas(cvec, w_mod, b_mod):
    n = 6 * D_MODEL
    return pl.pallas_call(
        _mod_kernel,
        grid=(DEPTH, n // MOD_TILE),
        in_specs=[pl.BlockSpec((8, D_MODEL), lambda l, j: (0, 0)),
                  pl.BlockSpec((1, D_MODEL, MOD_TILE), lambda l, j: (l, 0, j)),
                  pl.BlockSpec((1, 1, MOD_TILE), lambda l, j: (l, 0, j))],
        out_specs=pl.BlockSpec((1, 8, MOD_TILE), lambda l, j: (l, 0, j)),
        out_shape=jax.ShapeDtypeStruct((DEPTH, 8, n), F32),
        compiler_params=pltpu.CompilerParams(dimension_semantics=("arbitrary", "arbitrary"),
                                             vmem_limit_bytes=48 * 1024 * 1024),
        name="adaln_mod",
    )(cvec, w_mod, b_mod.reshape(DEPTH, 1, n))


def _rms(x):
    return x * lax.rsqrt(jnp.mean(x * x, axis=-1, keepdims=True) + NORM_EPS)


def _in_kernel(*refs, has_res):
    if has_res:
        x_ref, y_ref, pm_ref, m_ref, g_ref, w_ref, xo_ref, pa_ref, pr_ref, ph_ref = refs
        x = x_ref[...] + pm_ref[0, 5:6, :] * y_ref[...]
        xo_ref[...] = x
    else:
        x_ref, m_ref, g_ref, w_ref, pa_ref, pr_ref, ph_ref = refs
        x = x_ref[...]
    h = _rms(x) * g_ref[...] * (1.0 + m_ref[0, 1:2, :]) + m_ref[0, 0:1, :]
    proj = lax.dot_general(h, w_ref[...], _NN, precision=lax.Precision.DEFAULT, preferred_element_type=F32)
    pa_ref[...] = proj[:, 0:P_ATT]
    pr_ref[...] = proj[:, P_ATT:P_ATT + P_RW]
    ph_ref[...] = proj[:, P_ATT + P_RW:]


def in_proj_pallas(x, mod, norm_g, w_in_bf, rows_per_mod, res=None):
    R = x.shape[0]
    tpm = rows_per_mod // ROW_TILE
    rt = lambda w: pl.BlockSpec((ROW_TILE, w), lambda i: (i, 0))
    ms = pl.BlockSpec((1, 6, D_MODEL), lambda i: (i // tpm, 0, 0))
    full = lambda shape: pl.BlockSpec(shape, lambda i: (0,) * len(shape))
    in_specs = [rt(D_MODEL)] + ([rt(D_MODEL), ms] if res else []) + [ms, full((1, D_MODEL)),
                                                                     full((D_MODEL, w_in_bf.shape[1]))]
    args = [x] + ([res[0], res[1]] if res else []) + [mod, norm_g.reshape(1, D_MODEL), w_in_bf]
    widths = ([D_MODEL] if res else []) + [P_ATT, P_RW, P_HG]
    return pl.pallas_call(
        partial(_in_kernel, has_res=res is not None),
        grid=(R // ROW_TILE,),
        in_specs=in_specs,
        out_specs=[rt(w) for w in widths],
        out_shape=[jax.ShapeDtypeStruct((R, w), F32) for w in widths],
        compiler_params=pltpu.CompilerParams(dimension_semantics=("arbitrary",), vmem_limit_bytes=48 * 1024 * 1024),
        name="in_proj",
    )(*args)


def _out_kernel(att_ref, rw_ref, hg_ref, x_ref, m_ref, g_ref, w_ref, rw_w_ref, rb_ref, xo_ref, h_ref, lg_ref):
    d = lambda a, lo, hi: lax.dot_general(a.astype(BF16), w_ref[lo:hi, :], _NN, preferred_element_type=F32)
    mixo = (d(att_ref[...], 0, ATT_WIDTH) + d(rw_ref[...], ATT_WIDTH, ATT_WIDTH + RW_WIDTH)
            + d(hg_ref[...], ATT_WIDTH + RW_WIDTH, ATT_WIDTH + RW_WIDTH + HG_WIDTH))
    x = x_ref[...] + m_ref[0, 2:3, :] * mixo
    xo_ref[...] = x
    h = _rms(x) * g_ref[...] * (1.0 + m_ref[0, 4:5, :]) + m_ref[0, 3:4, :]
    h_ref[...] = h.astype(BF16)
    lg_ref[...] = _mm(h, rw_w_ref[...], passes=3) + rb_ref[...]


def out_proj_pallas(att, rw, hg, x, mod, norm_g, w_out_bf, router_w, router_b, rows_per_mod):
    R = x.shape[0]
    tpm = rows_per_mod // ROW_TILE
    rt = lambda w: pl.BlockSpec((ROW_TILE, w), lambda i: (i, 0))
    full = lambda shape: pl.BlockSpec(shape, lambda i: (0,) * len(shape))
    return pl.pallas_call(
        _out_kernel,
        grid=(R // ROW_TILE,),
        in_specs=[rt(ATT_WIDTH), rt(RW_WIDTH), rt(HG_WIDTH), rt(D_MODEL),
                  pl.BlockSpec((1, 6, D_MODEL), lambda i: (i // tpm, 0, 0)), full((1, D_MODEL)),
                  full((D_MODEL, D_MODEL)), full((D_MODEL, N_EXPERTS)), full((1, N_EXPERTS))],
        out_specs=[rt(D_MODEL), rt(D_MODEL), rt(N_EXPERTS)],
        out_shape=[jax.ShapeDtypeStruct((R, D_MODEL), F32), jax.ShapeDtypeStruct((R, D_MODEL), BF16),
                   jax.ShapeDtypeStruct((R, N_EXPERTS), F32)],
        compiler_params=pltpu.CompilerParams(dimension_semantics=("arbitrary",), vmem_limit_bytes=48 * 1024 * 1024),
        name="out_proj",
    )(att, rw, hg, x, mod, norm_g.reshape(1, D_MODEL), w_out_bf, router_w, router_b.reshape(1, N_EXPERTS))


def _final_kernel(x_ref, y_ref, m_ref, g_ref, o_ref):
    o_ref[...] = _rms(x_ref[...] + m_ref[0, 5:6, :] * y_ref[...]) * g_ref[...]


def final_norm_pallas(x, y, mod, norm_g, rows_per_mod):
    R = x.shape[0]
    tpm = rows_per_mod // ROW_TILE
    rt = pl.BlockSpec((ROW_TILE, D_MODEL), lambda i: (i, 0))
    return pl.pallas_call(
        _final_kernel,
        grid=(R // ROW_TILE,),
        in_specs=[rt, rt, pl.BlockSpec((1, 6, D_MODEL), lambda i: (i // tpm, 0, 0)),
                  pl.BlockSpec((1, D_MODEL), lambda i: (0, 0))],
        out_specs=rt,
        out_shape=jax.ShapeDtypeStruct((R, D_MODEL), F32),
        name="final_norm",
    )(x, y, mod, norm_g.reshape(1, D_MODEL))


def _route_kernel(lg_ref, posT_ref, gateT_ref, offs_ref, *, n_tok):
    Rt = ROUTE_TILE
    lane = lax.broadcasted_iota(jnp.int32, (Rt, N_EXPERTS), 1)
    ti = lax.broadcasted_iota(jnp.int32, (Rt, Rt), 0)
    si = lax.broadcasted_iota(jnp.int32, (Rt, Rt), 1)
    upper = (ti < si).astype(BF16)
    eye = (ti == si).astype(BF16)
    tn = lambda a, b: lax.dot_general(a, b, _TN, preferred_element_type=F32)
    off = jnp.zeros((N_EXPERTS, 1), F32)
    for it in range(n_tok // Rt):
        rows = slice(it * Rt, (it + 1) * Rt)
        lg = lg_ref[rows, :]
        work = lg
        member = jnp.zeros((Rt, N_EXPERTS), jnp.bool_)
        top = None
        for k in range(TOP_K):
            m = jnp.max(work, axis=-1, keepdims=True)
            if top is None:
                top = m
            first = jnp.min(jnp.where(work == m, lane, N_EXPERTS), axis=-1, keepdims=True)
            pick = lane == first
            member = member | pick
            work = jnp.where(pick, -jnp.inf, work)
        e = jnp.where(member, jnp.exp(lg - top), 0.0)
        gate = e / jnp.sum(e, axis=-1, keepdims=True)
        mem = member.astype(BF16)
        memT = tn(mem, eye)
        rankT = tn(mem, upper) + off
        g1, g2, g3 = _split(gate, 3)
        gateT = tn(g3, eye) + tn(g2, eye) + tn(g1, eye)
        posT_ref[:, rows] = jnp.where(memT > 0.5, rankT, -1.0).astype(jnp.int32)
        gateT_ref[:, rows] = gateT
        offs_ref[:, it:it + 1] = off.astype(jnp.int32)
        off = off + jnp.sum(memT, axis=1, keepdims=True)
    offs_ref[:, n_tok // Rt:n_tok // Rt + 1] = off.astype(jnp.int32)


def moe_route_pallas(logits):
    n_tok = logits.shape[0]
    nt = n_tok // ROUTE_TILE
    return pl.pallas_call(
        partial(_route_kernel, n_tok=n_tok),
        out_shape=[jax.ShapeDtypeStruct((N_EXPERTS, n_tok), jnp.int32),
                   jax.ShapeDtypeStruct((N_EXPERTS, n_tok), F32),
                   jax.ShapeDtypeStruct((N_EXPERTS, nt + 1), jnp.int32)],
        name="moe_route",
    )(logits)


def _moe_kernel(off_ref, h_ref, posT_ref, gateT_ref, wgu_ref, bgu_ref, wdn_ref, bdn_ref, y_ref, xb_scr, gs_scr,
                *, n_tok):
    e = pl.program_id(0)
    SB, TW = MOE_SLOTS, MOE_TOKENS
    n_chunks = n_tok // TW

    @pl.when(e == 0)
    def _():
        y_ref[...] = jnp.zeros_like(y_ref)

    count = off_ref[e, n_chunks]
    slot_col = lax.broadcasted_iota(jnp.int32, (SB, 1), 0)

    def block(sb, carry):
        s0 = sb * SB
        xb_scr[...] = jnp.zeros_like(xb_scr)
        gs_scr[...] = jnp.zeros_like(gs_scr)

        def select(c):
            return posT_ref[0, :, c * TW:(c + 1) * TW] == (slot_col + s0)

        def hit(c):
            return (off_ref[e, c] < s0 + SB) & (off_ref[e, c + 1] > s0)

        for c in range(n_chunks):
            @pl.when(hit(c))
            def _(c=c):
                sel = select(c)
                xb_scr[...] += lax.dot_general(sel.astype(BF16), h_ref[c * TW:(c + 1) * TW, :], _NN,
                                               preferred_element_type=F32)
                gs_scr[...] += jnp.sum(jnp.where(sel, gateT_ref[0, :, c * TW:(c + 1) * TW], 0.0), axis=1,
                                       keepdims=True)
        dot = lambda a, w: lax.dot_general(a, w, _NN, precision=lax.Precision.DEFAULT, preferred_element_type=F32)
        gu = dot(xb_scr[...], wgu_ref[0, 0]) + bgu_ref[0, 0]
        glu = jnp.minimum(gu[:, :EXPERT_FF], SWIGLU_LIMIT)
        lin = jnp.clip(gu[:, EXPERT_FF:], -SWIGLU_LIMIT, SWIGLU_LIMIT)
        act = glu * _sigmoid(SWIGLU_ALPHA * glu) * (lin + 1.0)
        yb = dot(act, wdn_ref[0, 0]) + bdn_ref[0, 0]
        ybg = (yb * gs_scr[...]).astype(BF16)
        for c in range(n_chunks):
            @pl.when(hit(c))
            def _(c=c):
                sel = select(c)
                y_ref[c * TW:(c + 1) * TW, :] += lax.dot_general(sel.astype(BF16), ybg, _TN,
                                                                 preferred_element_type=F32)
        return carry

    lax.fori_loop(0, (count + SB - 1) // SB, block, 0)


def moe_experts_pallas(h_bf16, posT, gateT, chunk_off, l, w_gu, b_gu, w_down, b_down):
    n_tok = h_bf16.shape[0]
    once = dict(pipeline_mode=pl.Buffered(1))
    grid_spec = pltpu.PrefetchScalarGridSpec(
        num_scalar_prefetch=1,
        grid=(N_EXPERTS,),
        in_specs=[pl.BlockSpec((n_tok, D_MODEL), lambda e, off: (0, 0), **once),
                  pl.BlockSpec((1, 1, n_tok), lambda e, off: (e, 0, 0)),
                  pl.BlockSpec((1, 1, n_tok), lambda e, off: (e, 0, 0)),
                  pl.BlockSpec((1, 1, D_MODEL, 2 * EXPERT_FF), lambda e, off: (l, e, 0, 0)),
                  pl.BlockSpec((1, 1, 1, 2 * EXPERT_FF), lambda e, off: (l, e, 0, 0)),
                  pl.BlockSpec((1, 1, EXPERT_FF, D_MODEL), lambda e, off: (l, e, 0, 0)),
                  pl.BlockSpec((1, 1, 1, D_MODEL), lambda e, off: (l, e, 0, 0))],
        out_specs=pl.BlockSpec((n_tok, D_MODEL), lambda e, off: (0, 0), **once),
        scratch_shapes=[pltpu.VMEM((MOE_SLOTS, D_MODEL), F32), pltpu.VMEM((MOE_SLOTS, 1), F32)],
    )
    return pl.pallas_call(
        partial(_moe_kernel, n_tok=n_tok),
        grid_spec=grid_spec,
        out_shape=jax.ShapeDtypeStruct((n_tok, D_MODEL), F32),
        compiler_params=pltpu.CompilerParams(dimension_semantics=("arbitrary",), vmem_limit_bytes=60 * 1024 * 1024),
        name="moe_experts",
    )(chunk_off, h_bf16, posT.reshape(N_EXPERTS, 1, n_tok), gateT.reshape(N_EXPERTS, 1, n_tok),
      w_gu, b_gu.reshape(DEPTH, N_EXPERTS, 1, 2 * EXPERT_FF), w_down, b_down.reshape(DEPTH, N_EXPERTS, 1, D_MODEL))


def moe_pallas(h_bf16, logits, l, w_gu, b_gu, w_down, b_down):
    posT, gateT, offs = moe_route_pallas(logits)
    chunk_off = offs[:, ::MOE_TOKENS // ROUTE_TILE]
    return moe_experts_pallas(h_bf16, posT, gateT, chunk_off, l, w_gu, b_gu, w_down, b_down)


def rms_norm(x, g):
    xf = x.astype(jnp.float32)
    y = xf * lax.rsqrt(jnp.mean(xf * xf, axis=-1, keepdims=True) + NORM_EPS)
    return (y * g.astype(jnp.float32)).astype(x.dtype)


def axial_rope(n_tok):
    rows = n_tok // GRID_W
    row = jnp.repeat(jnp.arange(rows, dtype=jnp.float32), GRID_W)
    col = jnp.tile(jnp.arange(GRID_W, dtype=jnp.float32), rows)
    n_freq = HEAD_DIM // 4
    inv = ROPE_THETA ** (-jnp.arange(n_freq, dtype=jnp.float32) / n_freq)
    ang = jnp.concatenate([row[:, None] * inv, col[:, None] * inv], axis=-1)
    return jnp.cos(ang), jnp.sin(ang)


def apply_rope(x, cos, sin):
    xf = x.astype(jnp.float32).reshape(x.shape[:-1] + (HEAD_DIM // 2, 2))
    x1, x2 = xf[..., 0], xf[..., 1]
    out = jnp.stack([x1 * cos - x2 * sin, x1 * sin + x2 * cos], axis=-1)
    return out.reshape(x.shape).astype(x.dtype)


def block_attention(q, k, v):
    b, hq, tq, d = q.shape
    hkv = k.shape[1]
    rep = hq // hkv
    nblk = tq // Q_BLOCK
    qb = q.reshape(b, hkv, rep, nblk, Q_BLOCK, d).transpose(3, 0, 1, 2, 4, 5)
    scale = 1.0 / math.sqrt(d)

    def one_block(qblk):
        s = jnp.einsum('bgrqd,bgkd->bgrqk', qblk, k, preferred_element_type=jnp.float32) * scale
        pr = jax.nn.softmax(s, axis=-1)
        return jnp.einsum('bgrqk,bgkd->bgrqd', pr.astype(v.dtype), v)

    o = lax.map(one_block, qb)
    return o.transpose(1, 2, 3, 0, 4, 5).reshape(b, hq, tq, d)


def rwkv7_scan(r, w, k, v, a_, b_, s0):
    xs = tuple(jnp.moveaxis(t, 1, 0) for t in (r, w, k, v, a_, b_))

    def step(S, inp):
        rt, wt, kt, vt, at, bt = inp
        sa = jnp.einsum('bhvk,bhk->bhv', S, at)
        S = S * wt[:, :, None, :] + sa[..., None] * bt[:, :, None, :] + vt[..., None] * kt[:, :, None, :]
        return S, jnp.einsum('bhvk,bhk->bhv', S, rt)

    s_last, ys = lax.scan(step, s0, xs)
    return jnp.moveaxis(ys, 0, 1), s_last


def head_group_norm(y, g, b):
    mu = jnp.mean(y, axis=-1, keepdims=True)
    yc = y - mu
    var = jnp.mean(yc * yc, axis=-1, keepdims=True)
    yn = (yc * lax.rsqrt(var + RW_GN_EPS)).reshape(y.shape[0], y.shape[1], -1)
    return yn * g.astype(jnp.float32) + b.astype(jnp.float32)


def rwkv7_mixer(r, k, v, wd, ad, gd, p, s0):
    B, T, _ = r.shape
    f32 = jnp.float32
    heads = lambda t: t.astype(f32).reshape(B, T, RW_HEADS, HEAD_DIM)
    r, k, v = heads(r), heads(k), heads(v)
    wd = wd.astype(f32).reshape(B, T, 2, RW_DECAY_LORA)
    ad = ad.astype(f32).reshape(B, T, 2, RW_AAA_LORA)
    kk = k * p['rw_kk'].astype(f32).reshape(RW_HEADS, HEAD_DIM)
    kk = kk * lax.rsqrt(jnp.sum(kk * kk, axis=-1, keepdims=True) + 1e-12)
    g = jax.nn.sigmoid(gd.astype(f32)) @ p['rw_g2'].astype(f32)
    ka = p['rw_ka'].astype(f32).reshape(RW_HEADS, HEAD_DIM)
    rk = p['rw_rk'].astype(f32)
    ys, bonuses, states = [], [], []
    for d in range(2):
        w_raw = p['rw_w0'][d].astype(f32) + jnp.tanh(wd[:, :, d]) @ p['rw_w2'][d].astype(f32)
        decay = heads(jnp.exp(-jnp.exp(-jax.nn.softplus(-w_raw) - 0.5)))
        a = heads(jax.nn.sigmoid(p['rw_a0'][d].astype(f32) + ad[:, :, d] @ p['rw_a2'][d].astype(f32)))
        kd = k * (1.0 + (a - 1.0) * ka)
        seq = (r, decay, kd, v, -kk, kk * a)
        if d == 1:
            seq = tuple(jnp.flip(t, axis=1) for t in seq)
        y, s_last = rwkv7_scan(*seq, s0[:, d].astype(f32))
        if d == 1:
            y = jnp.flip(y, axis=1)
        ys.append(y)
        bonuses.append(jnp.sum(r * kd * rk, axis=-1, keepdims=True) * v)
        states.append(s_last)
    y = head_group_norm(ys[0] + ys[1], p['rw_gn_g'], p['rw_gn_b']) + (bonuses[0] + bonuses[1]).reshape(B, T, RW_WIDTH)
    return y * g, jnp.stack(states, axis=1)


def chunk_gla(q, k, v, logf, s0):
    B, H, T, dk = q.shape
    dv = v.shape[-1]
    n = T // HG_CHUNK
    rs = lambda t: t.reshape(B, H, n, HG_CHUNK, t.shape[-1])
    q, k, v, logf = rs(q), rs(k), rs(v), rs(logf)
    b = jnp.cumsum(logf, axis=3)
    causal = jnp.tril(jnp.ones((HG_CHUNK, HG_CHUNK), dtype=bool))
    diff = b[:, :, :, :, None, :] - b[:, :, :, None, :, :]
    dec = jnp.where(causal[:, :, None], jnp.exp(jnp.minimum(diff, 0.0)), 0.0)
    att = jnp.einsum('bhntd,bhnsd,bhntsd->bhnts', q, k, dec)
    o_intra = jnp.einsum('bhnts,bhnsv->bhntv', att, v)
    b_last = b[:, :, :, -1, :]
    kv = jnp.einsum('bhnsd,bhnsv->bhndv', k * jnp.exp(b_last[:, :, :, None, :] - b), v)

    def step(S, inp):
        dl, kvn = inp
        return dl[..., None] * S + kvn, S

    s_last, s_before = lax.scan(step, s0, (jnp.moveaxis(jnp.exp(b_last), 2, 0), jnp.moveaxis(kv, 2, 0)))
    s_before = jnp.moveaxis(s_before, 0, 2)
    o_inter = jnp.einsum('bhntd,bhndv->bhntv', q * jnp.exp(b), s_before)
    return (o_intra + o_inter).reshape(B, H, T, dv), s_last


def hgrn2_mixer(q, f_raw, i, g, lb, norm_g, s0):
    B, T, _ = q.shape
    f32 = jnp.float32
    heads = lambda t: t.astype(f32).reshape(B, T, HG_HEADS, HEAD_DIM).transpose(0, 2, 1, 3)
    qh = heads(jax.nn.silu(q.astype(f32)))
    vh = heads(i)
    f_raw = f_raw.astype(f32).reshape(B, T, 2, HG_WIDTH)
    outs, states = [], []
    for d in range(2):
        lbd = lb[d]
        f = lbd + (1.0 - lbd) * jax.nn.sigmoid(f_raw[:, :, d])
        logf = jnp.log(jnp.maximum(f, HG_F_MIN))
        kf = 1.0 - f
        seq = (qh, heads(kf), vh, heads(logf))
        if d == 1:
            seq = tuple(jnp.flip(t, axis=2) for t in seq)
        o, s_last = chunk_gla(*seq, s0[:, d].astype(f32))
        if d == 1:
            o = jnp.flip(o, axis=2)
        outs.append(o)
        states.append(s_last)
    o = rms_norm(outs[0] + outs[1], norm_g).transpose(0, 2, 1, 3).reshape(B, T, HG_WIDTH)
    return o * jax.nn.silu(g.astype(f32)), jnp.stack(states, axis=1)


def moe(h, p):
    T, D = h.shape
    logits = (h @ p['router_w']).astype(jnp.float32) + p['router_b'].astype(jnp.float32)
    top_logit, top_idx = lax.top_k(logits, TOP_K)
    gates = jax.nn.softmax(top_logit, axis=-1)
    A = T * TOP_K
    flat_e = top_idx.reshape(A)
    flat_tok = jnp.arange(A, dtype=jnp.int32) // TOP_K
    flat_g = gates.reshape(A)
    order = jnp.argsort(flat_e)
    se = flat_e[order]
    counts = jnp.bincount(flat_e, length=N_EXPERTS)
    starts = jnp.cumsum(counts) - counts
    padded = (counts + MOE_BLOCK - 1) // MOE_BLOCK * MOE_BLOCK
    pends = jnp.cumsum(padded)
    pstarts = pends - padded
    dest = pstarts[se] + jnp.arange(A, dtype=jnp.int32) - starts[se]
    n_blocks = -(-(A + N_EXPERTS * (MOE_BLOCK - 1)) // MOE_BLOCK)
    n_slots = n_blocks * MOE_BLOCK
    slot_tok = jnp.full((n_slots,), T, dtype=jnp.int32).at[dest].set(flat_tok[order])
    slot_g = jnp.zeros((n_slots,), dtype=jnp.float32).at[dest].set(flat_g[order])
    block_e = jnp.clip(jnp.searchsorted(pends, jnp.arange(n_blocks, dtype=jnp.int32) * MOE_BLOCK, side='right'), 0, N_EXPERTS - 1)
    h_pad = jnp.concatenate([h, jnp.zeros((1, D), h.dtype)], axis=0)
    xb = h_pad[slot_tok].reshape(n_blocks, MOE_BLOCK, D)

    def expert_block(args):
        xblk, e = args
        gu = xblk @ p['moe_w_gu'][e] + p['moe_b_gu'][e]
        glu, lin = gu[:, :EXPERT_FF], gu[:, EXPERT_FF:]
        glu = jnp.minimum(glu, SWIGLU_LIMIT)
        lin = jnp.clip(lin, -SWIGLU_LIMIT, SWIGLU_LIMIT)
        act = glu * jax.nn.sigmoid(SWIGLU_ALPHA * glu) * (lin + 1.0)
        return act @ p['moe_w_down'][e] + p['moe_b_down'][e]

    yb = lax.map(expert_block, (xb, block_e)).reshape(n_slots, D)
    y = jax.ops.segment_sum(yb * slot_g[:, None].astype(yb.dtype), slot_tok, num_segments=T + 1)[:T]
    return y.astype(h.dtype)


def token_mixers(h, p, lb, cache):
    B, T, _ = h.shape
    proj = h @ p['w_in']
    split_points = np.cumsum(IN_SIZES)[:-1].tolist()
    (aq, ak, av, rr, rk, rv, rwd, rad, rgd, hq, hf, hi, hg) = jnp.split(proj, split_points, axis=-1)
    qh = rms_norm(aq.reshape(B, T, ATT_HEADS, HEAD_DIM), p['att_qnorm_g']).transpose(0, 2, 1, 3)
    kh = rms_norm(ak.reshape(B, T, ATT_KV_HEADS, HEAD_DIM), p['att_knorm_g']).transpose(0, 2, 1, 3)
    vh = av.reshape(B, T, ATT_KV_HEADS, HEAD_DIM).transpose(0, 2, 1, 3)
    if cache is None:
        att = block_attention(qh, kh, vh)
        rw_s0 = jnp.zeros((B, 2, RW_HEADS, HEAD_DIM, HEAD_DIM), jnp.float32)
        hg_s0 = jnp.zeros((B, 2, HG_HEADS, HEAD_DIM, HEAD_DIM), jnp.float32)
    else:
        ck, cv, rw_s0, hg_s0 = cache
        cos, sin = axial_rope(T)
        qr = apply_rope(qh, cos, sin)
        kr = apply_rope(kh, cos, sin)
        att = block_attention(qr, jnp.concatenate([ck.astype(kr.dtype), kr], axis=2), jnp.concatenate([cv.astype(vh.dtype), vh], axis=2))
    att = att.transpose(0, 2, 1, 3).reshape(B, T, ATT_WIDTH)
    rw_out, rw_state = rwkv7_mixer_pallas(proj[:, :, 768:1920], rw_s0.astype(F32), p)
    hg_out, hg_state = hgrn2_mixer_pallas(proj[:, :, 1920:3200], hg_s0.astype(F32), lb, p['hg_norm_g'])
    mix = jnp.concatenate([att.astype(h.dtype), rw_out.astype(h.dtype), hg_out.astype(h.dtype)], axis=-1)
    new_cache = (kh, vh, rw_state, hg_state) if cache is None else None
    return mix, new_cache


def layer(x, mod, p, lb, cache):
    shift1, scale1, gate1, shift2, scale2, gate2 = jnp.split(mod[:, None, :], 6, axis=-1)
    h = rms_norm(x, p['norm_mix_g']) * (1.0 + scale1) + shift1
    mix, new_cache = token_mixers(h, p, lb, cache)
    x = x + gate1 * (mix @ p['w_out'])
    h = rms_norm(x, p['norm_ffn_g']) * (1.0 + scale2) + shift2
    B, T, D = h.shape
    x = x + gate2 * moe(h.reshape(B * T, D), p).reshape(B, T, D)
    return x, new_cache


def hgrn_lower_bounds(hg_lb):
    sm = jax.nn.softmax(hg_lb.astype(jnp.float32), axis=0)
    return jnp.cumsum(sm, axis=0) - sm[0:1]


def kernel(x_prompt, x_sample, cache_att_k, cache_att_v, state_rwkv, state_hgrn, c, c_ctx, w_mod, b_mod, norm_mix_g, norm_ffn_g, w_in, w_out, att_qnorm_g, att_knorm_g, rw_w0, rw_w2, rw_a0, rw_a2, rw_g2, rw_kk, rw_ka, rw_rk, rw_gn_g, rw_gn_b, hg_lb, hg_norm_g, router_w, router_b, moe_w_gu, moe_b_gu, moe_w_down, moe_b_down, final_norm_g):
    BP, TP, _ = x_prompt.shape
    BS, TS, _ = x_sample.shape
    lb_all = hgrn_lower_bounds(hg_lb)
    cvec = jnp.concatenate([c_ctx[None, :], c, jnp.zeros((8 - 1 - BS, D_MODEL), F32)], axis=0)
    mod_all = adaln_mod_pallas(cvec, w_mod, b_mod).reshape(DEPTH, 8, 6, D_MODEL)
    zeros_state = jnp.zeros((BP, 2, RW_HEADS, HEAD_DIM, HEAD_DIM), F32)
    x = {'p': x_prompt.reshape(BP * TP, D_MODEL), 's': x_sample.reshape(BS * TS, D_MODEL)}
    dims = {'p': (TP, BP), 's': (TS, BS)}
    y_moe = {'p': None, 's': None}
    mod_prev = {'p': None, 's': None}
    ks, vs, srs, shs = [], [], [], []
    for l in range(DEPTH):
        prm = dict(rw_w0=rw_w0[l], rw_w2=rw_w2[l], rw_a0=rw_a0[l], rw_a2=rw_a2[l], rw_g2=rw_g2[l],
                   rw_kk=rw_kk[l], rw_ka=rw_ka[l], rw_rk=rw_rk[l], rw_gn_g=rw_gn_g[l], rw_gn_b=rw_gn_b[l])
        w_in_bf, w_out_bf = w_in[l], w_out[l].astype(BF16)
        mods = {'p': mod_all[l, 0:1], 's': mod_all[l, 1:1 + BS]}
        for s in ('p', 's'):
            T, B = dims[s]
            rpm = B * T if s == 'p' else T
            if l == 0:
                p_att, p_rw, p_hg = in_proj_pallas(x[s], mods[s], norm_mix_g[l], w_in_bf, rpm)
            else:
                x[s], p_att, p_rw, p_hg = in_proj_pallas(x[s], mods[s], norm_mix_g[l], w_in_bf, rpm,
                                                         res=(y_moe[s], mod_prev[s]))
            p_att, p_rw, p_hg = (t.reshape(B, T, -1) for t in (p_att, p_rw, p_hg))
            if s == 'p':
                att, k_l, v_l = attention_pallas(p_att, att_qnorm_g[l], att_knorm_g[l])
                rw_out, sr_l = rwkv7_mixer_pallas(p_rw, zeros_state, prm)
                hg_out, sh_l = hgrn2_mixer_pallas(p_hg, zeros_state, lb_all[l], hg_norm_g[l])
                ks.append(k_l)
                vs.append(v_l)
                srs.append(sr_l)
                shs.append(sh_l)
            else:
                att = attention_pallas(p_att, att_qnorm_g[l], att_knorm_g[l], (cache_att_k[:, l], cache_att_v[:, l]))
                rw_out, _ = rwkv7_mixer_pallas(p_rw, state_rwkv[:, l], prm)
                hg_out, _ = hgrn2_mixer_pallas(p_hg, state_hgrn[:, l], lb_all[l], hg_norm_g[l])
            x[s], h2, logits = out_proj_pallas(att.reshape(B * T, -1), rw_out.reshape(B * T, -1),
                                               hg_out.reshape(B * T, -1), x[s], mods[s], norm_ffn_g[l], w_out_bf,
                                               router_w[l], router_b[l], rpm)
            y_moe[s] = moe_pallas(h2, logits, l, moe_w_gu, moe_b_gu, moe_w_down, moe_b_down)
            mod_prev[s] = mods[s]
    y_prompt = final_norm_pallas(x['p'], y_moe['p'], mod_prev['p'], final_norm_g, BP * TP).reshape(x_prompt.shape)
    y_sample = final_norm_pallas(x['s'], y_moe['s'], mod_prev['s'], final_norm_g, TS).reshape(x_sample.shape)
    return (y_prompt, y_sample, jnp.stack(ks, axis=1), jnp.stack(vs, axis=1),
            jnp.stack(srs, axis=1), jnp.stack(shs, axis=1))
```

```python
import math
from functools import partial

import jax
import jax.numpy as jnp
import numpy as np
from jax import lax
from jax.experimental import pallas as pl
from jax.experimental.pallas import tpu as pltpu

D_MODEL = 1024
DEPTH = 2
GRID_W = 64
HEAD_DIM = 64
ATT_HEADS = 8
ATT_KV_HEADS = 2
ATT_WIDTH = ATT_HEADS * HEAD_DIM
KV_WIDTH = ATT_KV_HEADS * HEAD_DIM
RW_HEADS = 4
RW_WIDTH = RW_HEADS * HEAD_DIM
RW_DECAY_LORA = 64
RW_AAA_LORA = 64
RW_GATE_LORA = 128
RW_GN_EPS = 64e-5
HG_HEADS = 4
HG_WIDTH = HG_HEADS * HEAD_DIM
HG_CHUNK = 32
HG_F_MIN = 1e-6
IN_SIZES = (ATT_WIDTH, KV_WIDTH, KV_WIDTH, RW_WIDTH, RW_WIDTH, RW_WIDTH, 2 * RW_DECAY_LORA, 2 * RW_AAA_LORA,
            RW_GATE_LORA, HG_WIDTH, 2 * HG_WIDTH, HG_WIDTH, HG_WIDTH)
N_EXPERTS = 32
TOP_K = 4
EXPERT_FF = D_MODEL
SWIGLU_LIMIT = 7.0
SWIGLU_ALPHA = 1.702
MOE_BLOCK = 128
Q_BLOCK = 128
ROPE_THETA = 10000.0
NORM_EPS = 1e-6

RW_CHUNK = 64
BF16 = jnp.bfloat16
F32 = jnp.float32

_NN = (((1,), (0,)), ((), ()))
_NT = (((1,), (1,)), ((), ()))
_TN = (((0,), (0,)), ((), ()))


def _split(x, n):
    parts = []
    for _ in range(n - 1):
        hi = x.astype(BF16)
        parts.append(hi)
        x = x - hi.astype(F32)
    parts.append(x.astype(BF16))
    return parts


def _mm(a, b, dims=_NN, passes=1):
    d = lambda x, y: lax.dot_general(x, y, dims, preferred_element_type=F32)
    if passes == 1:
        return d(a.astype(BF16), b.astype(BF16))
    ah, al = _split(a, 2)
    bh, bl = _split(b, 2)
    return d(ah, bl) + d(al, bh) + d(ah, bh)


def _mm_exact_lhs(a01, b, n=3):
    a = a01.astype(BF16)
    out = None
    for t in reversed(_split(b, n)):
        y = lax.dot_general(a, t, _NN, preferred_element_type=F32)
        out = y if out is None else out + y
    return out


def _mm_exact_rhs(a, b01, n=3):
    b = b01.astype(BF16)
    out = None
    for t in reversed(_split(a, n)):
        y = lax.dot_general(t, b, _NN, preferred_element_type=F32)
        out = y if out is None else out + y
    return out


def _head_blockdiag(width):
    r = lax.broadcasted_iota(jnp.int32, (width, width), 0) // HEAD_DIM
    c = lax.broadcasted_iota(jnp.int32, (width, width), 1) // HEAD_DIM
    return (r == c).astype(F32)


def _sigmoid(x):
    return 1.0 / (1.0 + jnp.exp(-x))


def _softplus(x):
    return jnp.maximum(x, 0.0) + jnp.log(1.0 + jnp.exp(-jnp.abs(x)))


def _rwkv_kernel(p_ref, s0_ref, w0_ref, w2_ref, a0_ref, a2_ref, g2_ref, kk_ref, ka_ref, rk_ref, gng_ref, gnb_ref,
                 out_ref, st_ref,
                 lw_scr, kd_scr, bb_scr, y_scr, kk_scr, s_scr, *, T, NB):
    C = RW_CHUNK
    n_chunks = T // C
    bd = _head_blockdiag(RW_WIDTH)
    seg = lambda t: _mm_exact_rhs(t, bd)
    ka = ka_ref[...]
    for nb in range(NB):
        k = p_ref[nb, :, 256:512]
        kk = k * kk_ref[...]
        kk = kk * lax.rsqrt(seg(kk * kk) + 1e-12)
        kk_scr[nb] = kk
        for d in range(2):
            wd = p_ref[nb, :, 768 + 64 * d:832 + 64 * d]
            ad = p_ref[nb, :, 896 + 64 * d:960 + 64 * d]
            w_raw = w0_ref[d:d + 1, :] + _mm(jnp.tanh(wd), w2_ref[d], passes=3)
            lw_scr[nb, d] = -jnp.exp(-_softplus(-w_raw) - 0.5)
            a = _sigmoid(a0_ref[d:d + 1, :] + _mm(ad, a2_ref[d], passes=3))
            kd_scr[nb, d] = k * (1.0 + (a - 1.0) * ka)
            bb_scr[nb, d] = kk * a
    s_scr[...] = s0_ref[...]

    ti = lax.broadcasted_iota(jnp.int32, (C, C), 0)
    si = lax.broadcasted_iota(jnp.int32, (C, C), 1)
    ones_cc = jnp.ones((C, C), F32)

    def chunk_body(i, carry):
        ch = []
        for nb, d in [(nb, d) for nb in range(NB) for d in range(2)]:
            ci = i if d == 0 else n_chunks - 1 - i
            rows = pl.ds(pl.multiple_of(ci * C, C), C)
            strict = (ti > si) if d == 0 else (ti < si)
            incl = (ti >= si) if d == 0 else (ti <= si)
            lw = lw_scr[nb, d, rows, :]
            cum = _mm_exact_lhs(incl.astype(F32), lw)
            total = _mm_exact_lhs(ones_cc, lw)
            cum_ex = cum - lw
            mid = 0.5 * total
            rr = p_ref[nb, rows, 0:256]
            vv = p_ref[nb, rows, 512:768]
            kdc = kd_scr[nb, d, rows, :]
            bbc = bb_scr[nb, d, rows, :]
            kkc = kk_scr[nb, rows, :]
            e_inv = jnp.exp(mid - cum)
            At = -kkc * jnp.exp(cum_ex - mid)
            Rt = rr * jnp.exp(cum - mid)
            Bt = bbc * e_inv
            Kt = kdc * e_inv
            Ap = -kkc * jnp.exp(cum_ex)
            Rp = rr * jnp.exp(cum)
            e_out = jnp.exp(total - cum)
            Bh = bbc * e_out
            Kh = kdc * e_out
            e_tot = jnp.exp(total[0:1, :])
            for h in range(RW_HEADS):
                hs = slice(h * HEAD_DIM, (h + 1) * HEAD_DIM)
                ch.append(dict(nb=nb, d=d, h=h, rows=rows, hs=hs, strict=strict, incl=incl,
                               AR=jnp.concatenate([At[:, hs], Rt[:, hs]], axis=0), Bt=Bt[:, hs], Kt=Kt[:, hs],
                               V=vv[:, hs], X1=Ap[:, hs], Rp=Rp[:, hs], Bh=Bh[:, hs], Kh=Kh[:, hs],
                               e_tot=e_tot[:, hs]))
        for c in ch:
            c['AB'] = _mm(c['AR'], c['Bt'], _NT)
            c['AK'] = _mm(c['AR'], c['Kt'], _NT)
        for c in ch:
            c['P'] = jnp.where(c['strict'], c['AB'][:C], 0.0)
            c['A_ak'] = jnp.where(c['strict'], c['AK'][:C], 0.0)
            c['A_rb'] = jnp.where(c['incl'], c['AB'][C:], 0.0)
            c['A_rk'] = jnp.where(c['incl'], c['AK'][C:], 0.0)
        for c in ch:
            c['X2'] = _mm(c['A_ak'], c['V'])
        for lvl in range(6):
            for c in ch:
                if lvl < 5:
                    c['PZ'] = _mm(c['P'], jnp.concatenate([c['P'], c['X1'], c['X2']], axis=1))
                else:
                    c['PZ'] = _mm(c['P'], jnp.concatenate([c['X1'], c['X2']], axis=1))
            for c in ch:
                PZ = c['PZ']
                if lvl < 5:
                    c['P'] = PZ[:, :C]
                    c['X1'] = c['X1'] + PZ[:, C:2 * C]
                    c['X2'] = c['X2'] + PZ[:, 2 * C:]
                else:
                    c['X1'] = c['X1'] + PZ[:, :C]
                    c['X2'] = c['X2'] + PZ[:, C:]
        for c in ch:
            c['S0'] = s_scr[c['nb'], c['d'], c['h']]
            c['UY'] = _mm(jnp.concatenate([c['X1'], c['Rp']], axis=0), c['S0'], _NT)
        for c in ch:
            c['U'] = c['UY'][:C] + c['X2']
        for c in ch:
            c['Y'] = c['UY'][C:] + _mm(c['A_rb'], c['U']) + _mm(c['A_rk'], c['V'])
            c['S1'] = c['S0'] * c['e_tot'] + _mm(c['U'], c['Bh'], _TN) + _mm(c['V'], c['Kh'], _TN)
        for c in ch:
            s_scr[c['nb'], c['d'], c['h']] = c['S1']
            y_scr[c['nb'], c['d'], c['rows'], c['hs']] = c['Y']
        return carry

    lax.fori_loop(0, n_chunks, chunk_body, 0)

    for nb in range(NB):
        r = p_ref[nb, :, 0:256]
        v = p_ref[nb, :, 512:768]
        bonus = seg(r * (kd_scr[nb, 0] + kd_scr[nb, 1]) * rk_ref[...]) * v
        g = _mm(_sigmoid(p_ref[nb, :, 1024:1152]), g2_ref[...], passes=3)
        y = y_scr[nb, 0] + y_scr[nb, 1]
        mu = seg(y) * (1.0 / HEAD_DIM)
        yc = y - mu
        var = seg(yc * yc) * (1.0 / HEAD_DIM)
        yn = yc * lax.rsqrt(var + RW_GN_EPS)
        out_ref[nb] = (yn * gng_ref[...] + gnb_ref[...] + bonus) * g
    st_ref[...] = s_scr[...]


RW_ROWS = 1024


def rwkv7_mixer_pallas(p_rw, s0, prm):
    B, T, W = p_rw.shape
    NB = max(2, RW_ROWS // T)
    row = lambda a: a.reshape(1, RW_WIDTH)
    full = lambda shape: pl.BlockSpec(shape, lambda b: (0,) * len(shape))
    st_spec = pl.BlockSpec((NB, 2, RW_HEADS, HEAD_DIM, HEAD_DIM), lambda b: (b, 0, 0, 0, 0))
    return pl.pallas_call(
        partial(_rwkv_kernel, T=T, NB=NB),
        grid=(B // NB,),
        in_specs=[pl.BlockSpec((NB, T, W), lambda b: (b, 0, 0)), st_spec,
                  full((2, RW_WIDTH)), full((2, 64, RW_WIDTH)), full((2, RW_WIDTH)), full((2, 64, RW_WIDTH)),
                  full((128, RW_WIDTH)), full((1, RW_WIDTH)), full((1, RW_WIDTH)), full((1, RW_WIDTH)),
                  full((1, RW_WIDTH)), full((1, RW_WIDTH))],
        out_specs=[pl.BlockSpec((NB, T, RW_WIDTH), lambda b: (b, 0, 0)), st_spec],
        out_shape=[jax.ShapeDtypeStruct((B, T, RW_WIDTH), F32),
                   jax.ShapeDtypeStruct((B, 2, RW_HEADS, HEAD_DIM, HEAD_DIM), F32)],
        scratch_shapes=[pltpu.VMEM((NB, 2, T, RW_WIDTH), F32)] * 4
        + [pltpu.VMEM((NB, T, RW_WIDTH), F32), pltpu.VMEM((NB, 2, RW_HEADS, HEAD_DIM, HEAD_DIM), F32)],
        compiler_params=pltpu.CompilerParams(dimension_semantics=("arbitrary",), vmem_limit_bytes=56 * 1024 * 1024),
        name="rwkv7_mixer",
    )(p_rw, s0, prm['rw_w0'], prm['rw_w2'], prm['rw_a0'], prm['rw_a2'], prm['rw_g2'], row(prm['rw_kk']),
      row(prm['rw_ka']), row(prm['rw_rk']), row(prm['rw_gn_g']), row(prm['rw_gn_b']))


def _rms_norm_kernel(x_ref, g_ref, o_ref):
    x = x_ref[...]
    y = x * lax.rsqrt(jnp.mean(x * x, axis=-1, keepdims=True) + NORM_EPS)
    o_ref[...] = y * g_ref[...]


def rms_norm_rows(x, g, block_rows=512):
    R, D = x.shape
    return pl.pallas_call(
        _rms_norm_kernel,
        grid=(R // block_rows,),
        in_specs=[pl.BlockSpec((block_rows, D), lambda i: (i, 0)),
                  pl.BlockSpec((1, D), lambda i: (0, 0))],
        out_specs=pl.BlockSpec((block_rows, D), lambda i: (i, 0)),
        out_shape=jax.ShapeDtypeStruct((R, D), jnp.float32),
        name="final_rms_norm",
    )(x, g.reshape(1, D))


HG_SUB = 16
HG_ROWS = 64


def _hgrn_kernel(p_ref, s0_ref, lb_ref, ng_ref, out_ref, st_ref, lf_scr, kf_scr, o_scr, s_scr, *, T):
    R, c = HG_ROWS, HG_SUB
    n_it = T // R
    x = p_ref[0]
    bd = _head_blockdiag(HG_WIDTH)
    seg = lambda t: _mm_exact_rhs(t, bd)
    for d in range(2):
        lbd = lb_ref[d:d + 1, :]
        f = lbd + (1.0 - lbd) * _sigmoid(x[:, 256 + 256 * d:512 + 256 * d])
        lf_scr[d] = jnp.log(jnp.maximum(f, HG_F_MIN))
        kf_scr[d] = 1.0 - f
        for h in range(HG_HEADS):
            s_scr[d, h] = s0_ref[0, d, h].T

    ti = lax.broadcasted_iota(jnp.int32, (R, R), 0)
    si = lax.broadcasted_iota(jnp.int32, (R, R), 1)
    same_blk = (ti // c) == (si // c)
    t16 = lax.broadcasted_iota(jnp.int32, (c, 1), 0)

    def body(i, carry):
        for d in range(2):
            ci = i if d == 0 else n_it - 1 - i
            rows = pl.ds(pl.multiple_of(ci * R, R), R)
            incl = (ti >= si) if d == 0 else (ti <= si)
            lf = lf_scr[d, rows, :]
            cum = _mm_exact_lhs((incl & same_blk).astype(F32), lf)
            tot = _mm_exact_lhs(same_blk.astype(F32), lf)
            xq = p_ref[0, rows, 0:256]
            q = xq * _sigmoid(xq)
            v = p_ref[0, rows, 768:1024]
            kf = kf_scr[d, rows, :]
            Qp = q * jnp.exp(cum)
            Kh = kf * jnp.exp(tot - cum)
            e_tot = jnp.exp(tot)
            blocks = range(R // c) if d == 0 else range(R // c - 1, -1, -1)
            o_parts = [None] * (R // c)
            for j in blocks:
                rs = slice(j * c, (j + 1) * c)
                cb, qb, kb, vb = cum[rs], q[rs], kf[rs], v[rs]
                prods = []
                for s in range(c):
                    e = jnp.exp(jnp.minimum(cb - cb[s:s + 1, :], 0.0))
                    prods.append(qb * (kb[s:s + 1, :] * e))
                att = _mm_exact_rhs(jnp.concatenate(prods, axis=0), bd, n=1)
                o_blk = jnp.zeros((c, HG_WIDTH), F32)
                for s in range(c):
                    keep = (t16 >= s) if d == 0 else (t16 <= s)
                    o_blk = o_blk + jnp.where(keep, att[s * c:(s + 1) * c], 0.0) * vb[s:s + 1, :]
                o_heads = []
                for h in range(HG_HEADS):
                    hs = slice(h * HEAD_DIM, (h + 1) * HEAD_DIM)
                    ST = s_scr[d, h]
                    o_heads.append(_mm(Qp[rs, hs], ST, _NT))
                    s_scr[d, h] = ST * e_tot[j * c:j * c + 1, hs] + _mm(vb[:, hs], Kh[rs, hs], _TN)
                o_parts[j] = o_blk + jnp.concatenate(o_heads, axis=1)
            o_scr[d, rows, :] = jnp.concatenate(o_parts, axis=0)
        return carry

    lax.fori_loop(0, n_it, body, 0)

    o = o_scr[0] + o_scr[1]
    o = o * lax.rsqrt(seg(o * o) * (1.0 / HEAD_DIM) + NORM_EPS) * ng_ref[...]
    gg = x[:, 1024:1280]
    out_ref[0] = o * (gg * _sigmoid(gg))
    for d in range(2):
        for h in range(HG_HEADS):
            st_ref[0, d, h] = s_scr[d, h].T


def hgrn2_mixer_pallas(p_hg, s0, lb, norm_g):
    B, T, W = p_hg.shape
    full = lambda shape: pl.BlockSpec(shape, lambda b: (0,) * len(shape))
    st_spec = pl.BlockSpec((1, 2, HG_HEADS, HEAD_DIM, HEAD_DIM), lambda b: (b, 0, 0, 0, 0))
    return pl.pallas_call(
        partial(_hgrn_kernel, T=T),
        grid=(B,),
        in_specs=[pl.BlockSpec((1, T, W), lambda b: (b, 0, 0)), st_spec, full((2, HG_WIDTH)), full((1, HG_WIDTH))],
        out_specs=[pl.BlockSpec((1, T, HG_WIDTH), lambda b: (b, 0, 0)), st_spec],
        out_shape=[jax.ShapeDtypeStruct((B, T, HG_WIDTH), F32),
                   jax.ShapeDtypeStruct((B, 2, HG_HEADS, HEAD_DIM, HEAD_DIM), F32)],
        scratch_shapes=[pltpu.VMEM((2, T, HG_WIDTH), F32)] * 3
        + [pltpu.VMEM((2, HG_HEADS, HEAD_DIM, HEAD_DIM), F32)],
        compiler_params=pltpu.CompilerParams(dimension_semantics=("arbitrary",), vmem_limit_bytes=48 * 1024 * 1024),
        name="hgrn2_mixer",
    )(p_hg, s0, lb, jnp.tile(norm_g.reshape(1, HEAD_DIM), (1, HG_HEADS)))


ATT_REP = ATT_HEADS // ATT_KV_HEADS
ATT_QROWS = 128


def _swap_pairs(x):
    w = x.shape[-1]
    lane = lax.broadcasted_iota(jnp.int32, x.shape, x.ndim - 1)
    return jnp.where(lane % 2 == 0, pltpu.roll(x, w - 1, x.ndim - 1), pltpu.roll(x, 1, x.ndim - 1))


def _att_kernel(*refs, T, past, rope):
    if rope:
        p_ref, qg_ref, kg_ref, cos_ref, sin_ref, ck_ref, cv_ref, out_ref, k_scr, v_scr, q_scr = refs
    else:
        p_ref, qg_ref, kg_ref, out_ref, kh_ref, vh_ref, k_scr, v_scr, q_scr = refs
    x = p_ref[0]
    q = x[:, 0:ATT_WIDTH]
    k = x[:, ATT_WIDTH:ATT_WIDTH + KV_WIDTH]
    v = x[:, ATT_WIDTH + KV_WIDTH:ATT_WIDTH + 2 * KV_WIDTH]
    inv_d = 1.0 / HEAD_DIM
    q = q * lax.rsqrt(_mm_exact_rhs(q * q, _head_blockdiag(ATT_WIDTH)) * inv_d + NORM_EPS) * qg_ref[...]
    k = k * lax.rsqrt(_mm_exact_rhs(k * k, _head_blockdiag(KV_WIDTH)) * inv_d + NORM_EPS) * kg_ref[...]
    if rope:
        cos, sin = cos_ref[...], sin_ref[...]
        rep = ATT_WIDTH // KV_WIDTH
        q = q * jnp.concatenate([cos] * rep, axis=1) + _swap_pairs(q) * jnp.concatenate([sin] * rep, axis=1)
        k = k * cos + _swap_pairs(k) * sin
    q_scr[...] = (q * (1.0 / math.sqrt(HEAD_DIM))).astype(BF16)
    for g in range(ATT_KV_HEADS):
        gs = slice(g * HEAD_DIM, (g + 1) * HEAD_DIM)
        if rope:
            k_scr[g, 0:past, :] = ck_ref[0, g].astype(BF16)
            v_scr[g, 0:past, :] = cv_ref[0, g].astype(BF16)
        else:
            kh_ref[0, g] = k[:, gs]
            vh_ref[0, g] = v[:, gs]
        k_scr[g, past:past + T, :] = k[:, gs].astype(BF16)
        v_scr[g, past:past + T, :] = v[:, gs].astype(BF16)
    QR = ATT_QROWS

    def q_block(qb, carry):
        rows = pl.ds(pl.multiple_of(qb * QR, QR), QR)
        qblk = q_scr[rows, :]
        for g in range(ATT_KV_HEADS):
            qs = jnp.concatenate([qblk[:, (g * ATT_REP + r) * HEAD_DIM:(g * ATT_REP + r + 1) * HEAD_DIM]
                                  for r in range(ATT_REP)], axis=0)
            s = lax.dot_general(qs, k_scr[g], _NT, preferred_element_type=F32)
            e = jnp.exp(s - jnp.max(s, axis=-1, keepdims=True))
            l = jnp.sum(e, axis=-1, keepdims=True)
            o = lax.dot_general(e.astype(BF16), v_scr[g], _NN, preferred_element_type=F32) / l
            for r in range(ATT_REP):
                h = g * ATT_REP + r
                out_ref[0, rows, h * HEAD_DIM:(h + 1) * HEAD_DIM] = o[r * QR:(r + 1) * QR]
        return carry

    lax.fori_loop(0, T // QR, q_block, 0)


def rope_tables(T):
    rows = T // GRID_W
    row = jnp.repeat(jnp.arange(rows, dtype=F32), GRID_W)
    col = jnp.tile(jnp.arange(GRID_W, dtype=F32), rows)
    n_freq = HEAD_DIM // 4
    inv = ROPE_THETA ** (-jnp.arange(n_freq, dtype=F32) / n_freq)
    ang = jnp.concatenate([row[:, None] * inv, col[:, None] * inv], axis=-1)
    cos = jnp.repeat(jnp.cos(ang), 2, axis=-1)
    sin = jnp.stack([-jnp.sin(ang), jnp.sin(ang)], axis=-1).reshape(T, HEAD_DIM)
    return jnp.tile(cos, (1, ATT_KV_HEADS)), jnp.tile(sin, (1, ATT_KV_HEADS))


def attention_pallas(p_att, qnorm_g, knorm_g, cache=None):
    B, T, W = p_att.shape
    rope = cache is not None
    past = cache[0].shape[2] if rope else 0
    full = lambda shape: pl.BlockSpec(shape, lambda b: (0,) * len(shape))
    qg = jnp.tile(qnorm_g.reshape(1, HEAD_DIM), (1, ATT_HEADS))
    kg = jnp.tile(knorm_g.reshape(1, HEAD_DIM), (1, ATT_KV_HEADS))
    in_specs = [pl.BlockSpec((1, T, W), lambda b: (b, 0, 0)), full((1, ATT_WIDTH)), full((1, KV_WIDTH))]
    args = [p_att, qg, kg]
    out_specs = [pl.BlockSpec((1, T, ATT_WIDTH), lambda b: (b, 0, 0))]
    out_shape = [jax.ShapeDtypeStruct((B, T, ATT_WIDTH), F32)]
    if rope:
        cos, sin = rope_tables(T)
        kv_spec = pl.BlockSpec((1, ATT_KV_HEADS, past, HEAD_DIM), lambda b: (b, 0, 0, 0))
        in_specs += [full((T, KV_WIDTH)), full((T, KV_WIDTH)), kv_spec, kv_spec]
        args += [cos, sin, cache[0], cache[1]]
    else:
        kv_spec = pl.BlockSpec((1, ATT_KV_HEADS, T, HEAD_DIM), lambda b: (b, 0, 0, 0))
        out_specs += [kv_spec, kv_spec]
        out_shape += [jax.ShapeDtypeStruct((B, ATT_KV_HEADS, T, HEAD_DIM), F32)] * 2
    res = pl.pallas_call(
        partial(_att_kernel, T=T, past=past, rope=rope),
        grid=(B,),
        in_specs=in_specs,
        out_specs=out_specs,
        out_shape=out_shape,
        scratch_shapes=[pltpu.VMEM((ATT_KV_HEADS, past + T, HEAD_DIM), BF16)] * 2
        + [pltpu.VMEM((T, ATT_WIDTH), BF16)],
        compiler_params=pltpu.CompilerParams(dimension_semantics=("arbitrary",), vmem_limit_bytes=48 * 1024 * 1024),
        name="attention_rope" if rope else "attention_ctx",
    )(*args)
    return res[0] if rope else tuple(res)


ROW_TILE = 512
MOD_TILE = 1536
ROUTE_TILE = 256
MOE_SLOTS = 256
MOE_TOKENS = 256
MOE_WINDOW = 768
P_ATT, P_RW, P_HG = ATT_WIDTH + 2 * KV_WIDTH, 3 * RW_WIDTH + 384, 5 * HG_WIDTH


def _mod_kernel(c_ref, w_ref, b_ref, o_ref):
    c = c_ref[...]
    o_ref[0] = _mm(c * _sigmoid(c), w_ref[0], passes=3) + b_ref[0]


def adaln_mod_pallas(cvec, w_mod, b_mod):
    n = 6 * D_MODEL
    return pl.pallas_call(
        _mod_kernel,
        grid=(DEPTH, n // MOD_TILE),
        in_specs=[pl.BlockSpec((8, D_MODEL), lambda l, j: (0, 0)),
                  pl.BlockSpec((1, D_MODEL, MOD_TILE), lambda l, j: (l, 0, j)),
                  pl.BlockSpec((1, 1, MOD_TILE), lambda l, j: (l, 0, j))],
        out_specs=pl.BlockSpec((1, 8, MOD_TILE), lambda l, j: (l, 0, j)),
        out_shape=jax.ShapeDtypeStruct((DEPTH, 8, n), F32),
        compiler_params=pltpu.CompilerParams(dimension_semantics=("arbitrary", "arbitrary"),
                                             vmem_limit_bytes=48 * 1024 * 1024),
        name="adaln_mod",
    )(cvec, w_mod, b_mod.reshape(DEPTH, 1, n))


def _rms(x):
    return x * lax.rsqrt(jnp.mean(x * x, axis=-1, keepdims=True) + NORM_EPS)


def _in_kernel(*refs, has_res):
    if has_res:
        x_ref, y_ref, pm_ref, m_ref, g_ref, w_ref, xo_ref, pa_ref, pr_ref, ph_ref = refs
        x = x_ref[...] + pm_ref[0, 5:6, :] * y_ref[...]
        xo_ref[...] = x
    else:
        x_ref, m_ref, g_ref, w_ref, pa_ref, pr_ref, ph_ref = refs
        x = x_ref[...]
    h = _rms(x) * g_ref[...] * (1.0 + m_ref[0, 1:2, :]) + m_ref[0, 0:1, :]
    proj = lax.dot_general(h, w_ref[...], _NN, precision=lax.Precision.DEFAULT, preferred_element_type=F32)
    pa_ref[...] = proj[:, 0:P_ATT]
    pr_ref[...] = proj[:, P_ATT:P_ATT + P_RW]
    ph_ref[...] = proj[:, P_ATT + P_RW:]


def in_proj_pallas(x, mod, norm_g, w_in_bf, rows_per_mod, res=None):
    R = x.shape[0]
    tpm = rows_per_mod // ROW_TILE
    rt = lambda w: pl.BlockSpec((ROW_TILE, w), lambda i: (i, 0))
    ms = pl.BlockSpec((1, 6, D_MODEL), lambda i: (i // tpm, 0, 0))
    full = lambda shape: pl.BlockSpec(shape, lambda i: (0,) * len(shape))
    in_specs = [rt(D_MODEL)] + ([rt(D_MODEL), ms] if res else []) + [ms, full((1, D_MODEL)),
                                                                     full((D_MODEL, w_in_bf.shape[1]))]
    args = [x] + ([res[0], res[1]] if res else []) + [mod, norm_g.reshape(1, D_MODEL), w_in_bf]
    widths = ([D_MODEL] if res else []) + [P_ATT, P_RW, P_HG]
    return pl.pallas_call(
        partial(_in_kernel, has_res=res is not None),
        grid=(R // ROW_TILE,),
        in_specs=in_specs,
        out_specs=[rt(w) for w in widths],
        out_shape=[jax.ShapeDtypeStruct((R, w), F32) for w in widths],
        compiler_params=pltpu.CompilerParams(dimension_semantics=("arbitrary",), vmem_limit_bytes=48 * 1024 * 1024),
        name="in_proj",
    )(*args)


def _out_kernel(att_ref, rw_ref, hg_ref, x_ref, m_ref, g_ref, w_ref, rw_w_ref, rb_ref, xo_ref, h_ref, lg_ref):
    d = lambda a, lo, hi: lax.dot_general(a.astype(BF16), w_ref[lo:hi, :], _NN, preferred_element_type=F32)
    mixo = (d(att_ref[...], 0, ATT_WIDTH) + d(rw_ref[...], ATT_WIDTH, ATT_WIDTH + RW_WIDTH)
            + d(hg_ref[...], ATT_WIDTH + RW_WIDTH, ATT_WIDTH + RW_WIDTH + HG_WIDTH))
    x = x_ref[...] + m_ref[0, 2:3, :] * mixo
    xo_ref[...] = x
    h = _rms(x) * g_ref[...] * (1.0 + m_ref[0, 4:5, :]) + m_ref[0, 3:4, :]
    h_ref[...] = h.astype(BF16)
    lg_ref[...] = _mm(h, rw_w_ref[...], passes=3) + rb_ref[...]


def out_proj_pallas(att, rw, hg, x, mod, norm_g, w_out_bf, router_w, router_b, rows_per_mod):
    R = x.shape[0]
    tpm = rows_per_mod // ROW_TILE
    rt = lambda w: pl.BlockSpec((ROW_TILE, w), lambda i: (i, 0))
    full = lambda shape: pl.BlockSpec(shape, lambda i: (0,) * len(shape))
    return pl.pallas_call(
        _out_kernel,
        grid=(R // ROW_TILE,),
        in_specs=[rt(ATT_WIDTH), rt(RW_WIDTH), rt(HG_WIDTH), rt(D_MODEL),
                  pl.BlockSpec((1, 6, D_MODEL), lambda i: (i // tpm, 0, 0)), full((1, D_MODEL)),
                  full((D_MODEL, D_MODEL)), full((D_MODEL, N_EXPERTS)), full((1, N_EXPERTS))],
        out_specs=[rt(D_MODEL), rt(D_MODEL), rt(N_EXPERTS)],
        out_shape=[jax.ShapeDtypeStruct((R, D_MODEL), F32), jax.ShapeDtypeStruct((R, D_MODEL), BF16),
                   jax.ShapeDtypeStruct((R, N_EXPERTS), F32)],
        compiler_params=pltpu.CompilerParams(dimension_semantics=("arbitrary",), vmem_limit_bytes=48 * 1024 * 1024),
        name="out_proj",
    )(att, rw, hg, x, mod, norm_g.reshape(1, D_MODEL), w_out_bf, router_w, router_b.reshape(1, N_EXPERTS))


def _final_kernel(x_ref, y_ref, m_ref, g_ref, o_ref):
    o_ref[...] = _rms(x_ref[...] + m_ref[0, 5:6, :] * y_ref[...]) * g_ref[...]


def final_norm_pallas(x, y, mod, norm_g, rows_per_mod):
    R = x.shape[0]
    tpm = rows_per_mod // ROW_TILE
    rt = pl.BlockSpec((ROW_TILE, D_MODEL), lambda i: (i, 0))
    return pl.pallas_call(
        _final_kernel,
        grid=(R // ROW_TILE,),
        in_specs=[rt, rt, pl.BlockSpec((1, 6, D_MODEL), lambda i: (i // tpm, 0, 0)),
                  pl.BlockSpec((1, D_MODEL), lambda i: (0, 0))],
        out_specs=rt,
        out_shape=jax.ShapeDtypeStruct((R, D_MODEL), F32),
        name="final_norm",
    )(x, y, mod, norm_g.reshape(1, D_MODEL))


def _route_kernel(lg_ref, posT_ref, gateT_ref, offs_ref, *, n_tok):
    Rt = ROUTE_TILE
    lane = lax.broadcasted_iota(jnp.int32, (Rt, N_EXPERTS), 1)
    ti = lax.broadcasted_iota(jnp.int32, (Rt, Rt), 0)
    si = lax.broadcasted_iota(jnp.int32, (Rt, Rt), 1)
    upper = (ti < si).astype(BF16)
    eye = (ti == si).astype(BF16)
    tn = lambda a, b: lax.dot_general(a, b, _TN, preferred_element_type=F32)
    off = jnp.zeros((N_EXPERTS, 1), F32)
    for it in range(n_tok // Rt):
        rows = slice(it * Rt, (it + 1) * Rt)
        lg = lg_ref[rows, :]
        work = lg
        member = jnp.zeros((Rt, N_EXPERTS), jnp.bool_)
        top = None
        for k in range(TOP_K):
            m = jnp.max(work, axis=-1, keepdims=True)
            if top is None:
                top = m
            first = jnp.min(jnp.where(work == m, lane, N_EXPERTS), axis=-1, keepdims=True)
            pick = lane == first
            member = member | pick
            work = jnp.where(pick, -jnp.inf, work)
        e = jnp.where(member, jnp.exp(lg - top), 0.0)
        gate = e / jnp.sum(e, axis=-1, keepdims=True)
        mem = member.astype(BF16)
        memT = tn(mem, eye)
        rankT = tn(mem, upper) + off
        g1, g2, g3 = _split(gate, 3)
        gateT = tn(g3, eye) + tn(g2, eye) + tn(g1, eye)
        posT_ref[:, rows] = jnp.where(memT > 0.5, rankT, -1.0).astype(jnp.int32)
        gateT_ref[:, rows] = gateT
        offs_ref[:, it:it + 1] = off.astype(jnp.int32)
        off = off + jnp.sum(memT, axis=1, keepdims=True)
    offs_ref[:, n_tok // Rt:n_tok // Rt + 1] = off.astype(jnp.int32)


def moe_route_pallas(logits):
    n_tok = logits.shape[0]
    nt = n_tok // ROUTE_TILE
    return pl.pallas_call(
        partial(_route_kernel, n_tok=n_tok),
        out_shape=[jax.ShapeDtypeStruct((N_EXPERTS, n_tok), jnp.int32),
                   jax.ShapeDtypeStruct((N_EXPERTS, n_tok), F32),
                   jax.ShapeDtypeStruct((N_EXPERTS, nt + 1), jnp.int32)],
        name="moe_route",
    )(logits)


def _moe_kernel(off_ref, h_ref, posT_ref, gateT_ref, wgu_ref, bgu_ref, wdn_ref, bdn_ref, y_ref, xb_scr, gs_scr,
                *, n_tok):
    e = pl.program_id(0)
    SB, TW = MOE_SLOTS, MOE_TOKENS
    n_chunks = n_tok // TW

    @pl.when(e == 0)
    def _():
        y_ref[...] = jnp.zeros_like(y_ref)

    count = off_ref[e, n_chunks]
    slot_col = lax.broadcasted_iota(jnp.int32, (SB, 1), 0)
    WIN = MOE_WINDOW
    lane_tok = lax.broadcasted_iota(jnp.int32, (1, WIN), 1)

    def block(sb, carry):
        s0 = sb * SB
        xb_scr[...] = jnp.zeros_like(xb_scr)
        gs_scr[...] = jnp.zeros_like(gs_scr)
        c_lo = jnp.int32(0)
        c_end = jnp.int32(0)
        for c in range(n_chunks):
            c_lo += (off_ref[e, c + 1] <= s0).astype(jnp.int32)
            c_end += (off_ref[e, c] < s0 + SB).astype(jnp.int32)
        t_lo = c_lo * TW
        n_win = ((c_end - c_lo) * TW + WIN - 1) // WIN

        def window(w):
            t0 = t_lo + w * WIN
            tc = pl.multiple_of(jnp.minimum(t0, n_tok - WIN), TW)
            pos = posT_ref[0, :, pl.ds(tc, WIN)]
            sel = (pos == slot_col + s0) & (lane_tok + tc >= t0)
            return tc, sel

        def gather(w, carry):
            tc, sel = window(w)
            xb_scr[...] += lax.dot_general(sel.astype(BF16), h_ref[pl.ds(tc, WIN), :], _NN,
                                           preferred_element_type=F32)
            gs_scr[...] += jnp.sum(jnp.where(sel, gateT_ref[0, :, pl.ds(tc, WIN)], 0.0), axis=1, keepdims=True)
            return carry

        lax.fori_loop(0, n_win, gather, 0)
        dot = lambda a, w: lax.dot_general(a, w, _NN, precision=lax.Precision.DEFAULT, preferred_element_type=F32)
        gu = dot(xb_scr[...], wgu_ref[0, 0]) + bgu_ref[0, 0]
        glu = jnp.minimum(gu[:, :EXPERT_FF], SWIGLU_LIMIT)
        lin = jnp.clip(gu[:, EXPERT_FF:], -SWIGLU_LIMIT, SWIGLU_LIMIT)
        act = glu * _sigmoid(SWIGLU_ALPHA * glu) * (lin + 1.0)
        yb = dot(act, wdn_ref[0, 0]) + bdn_ref[0, 0]
        ybg = (yb * gs_scr[...]).astype(BF16)
        def scatter(w, carry):
            tc, sel = window(w)
            y_ref[pl.ds(tc, WIN), :] += lax.dot_general(sel.astype(BF16), ybg, _TN, preferred_element_type=F32)
            return carry

        lax.fori_loop(0, n_win, scatter, 0)
        return carry

    lax.fori_loop(0, (count + SB - 1) // SB, block, 0)


def moe_experts_pallas(h_bf16, posT, gateT, chunk_off, l, w_gu, b_gu, w_down, b_down):
    n_tok = h_bf16.shape[0]
    once = dict(pipeline_mode=pl.Buffered(1))
    grid_spec = pltpu.PrefetchScalarGridSpec(
        num_scalar_prefetch=1,
        grid=(N_EXPERTS,),
        in_specs=[pl.BlockSpec((n_tok, D_MODEL), lambda e, off: (0, 0), **once),
                  pl.BlockSpec((1, 1, n_tok), lambda e, off: (e, 0, 0)),
                  pl.BlockSpec((1, 1, n_tok), lambda e, off: (e, 0, 0)),
                  pl.BlockSpec((1, 1, D_MODEL, 2 * EXPERT_FF), lambda e, off: (l, e, 0, 0)),
                  pl.BlockSpec((1, 1, 1, 2 * EXPERT_FF), lambda e, off: (l, e, 0, 0)),
                  pl.BlockSpec((1, 1, EXPERT_FF, D_MODEL), lambda e, off: (l, e, 0, 0)),
                  pl.BlockSpec((1, 1, 1, D_MODEL), lambda e, off: (l, e, 0, 0))],
        out_specs=pl.BlockSpec((n_tok, D_MODEL), lambda e, off: (0, 0), **once),
        scratch_shapes=[pltpu.VMEM((MOE_SLOTS, D_MODEL), F32), pltpu.VMEM((MOE_SLOTS, 1), F32)],
    )
    return pl.pallas_call(
        partial(_moe_kernel, n_tok=n_tok),
        grid_spec=grid_spec,
        out_shape=jax.ShapeDtypeStruct((n_tok, D_MODEL), F32),
        compiler_params=pltpu.CompilerParams(dimension_semantics=("arbitrary",), vmem_limit_bytes=60 * 1024 * 1024),
        name="moe_experts",
    )(chunk_off, h_bf16, posT.reshape(N_EXPERTS, 1, n_tok), gateT.reshape(N_EXPERTS, 1, n_tok),
      w_gu, b_gu.reshape(DEPTH, N_EXPERTS, 1, 2 * EXPERT_FF), w_down, b_down.reshape(DEPTH, N_EXPERTS, 1, D_MODEL))


def moe_pallas(h_bf16, logits, l, w_gu, b_gu, w_down, b_down):
    posT, gateT, offs = moe_route_pallas(logits)
    chunk_off = offs[:, ::MOE_TOKENS // ROUTE_TILE]
    return moe_experts_pallas(h_bf16, posT, gateT, chunk_off, l, w_gu, b_gu, w_down, b_down)


def rms_norm(x, g):
    xf = x.astype(jnp.float32)
    y = xf * lax.rsqrt(jnp.mean(xf * xf, axis=-1, keepdims=True) + NORM_EPS)
    return (y * g.astype(jnp.float32)).astype(x.dtype)


def axial_rope(n_tok):
    rows = n_tok // GRID_W
    row = jnp.repeat(jnp.arange(rows, dtype=jnp.float32), GRID_W)
    col = jnp.tile(jnp.arange(GRID_W, dtype=jnp.float32), rows)
    n_freq = HEAD_DIM // 4
    inv = ROPE_THETA ** (-jnp.arange(n_freq, dtype=jnp.float32) / n_freq)
    ang = jnp.concatenate([row[:, None] * inv, col[:, None] * inv], axis=-1)
    return jnp.cos(ang), jnp.sin(ang)


def apply_rope(x, cos, sin):
    xf = x.astype(jnp.float32).reshape(x.shape[:-1] + (HEAD_DIM // 2, 2))
    x1, x2 = xf[..., 0], xf[..., 1]
    out = jnp.stack([x1 * cos - x2 * sin, x1 * sin + x2 * cos], axis=-1)
    return out.reshape(x.shape).astype(x.dtype)


def block_attention(q, k, v):
    b, hq, tq, d = q.shape
    hkv = k.shape[1]
    rep = hq // hkv
    nblk = tq // Q_BLOCK
    qb = q.reshape(b, hkv, rep, nblk, Q_BLOCK, d).transpose(3, 0, 1, 2, 4, 5)
    scale = 1.0 / math.sqrt(d)

    def one_block(qblk):
        s = jnp.einsum('bgrqd,bgkd->bgrqk', qblk, k, preferred_element_type=jnp.float32) * scale
        pr = jax.nn.softmax(s, axis=-1)
        return jnp.einsum('bgrqk,bgkd->bgrqd', pr.astype(v.dtype), v)

    o = lax.map(one_block, qb)
    return o.transpose(1, 2, 3, 0, 4, 5).reshape(b, hq, tq, d)


def rwkv7_scan(r, w, k, v, a_, b_, s0):
    xs = tuple(jnp.moveaxis(t, 1, 0) for t in (r, w, k, v, a_, b_))

    def step(S, inp):
        rt, wt, kt, vt, at, bt = inp
        sa = jnp.einsum('bhvk,bhk->bhv', S, at)
        S = S * wt[:, :, None, :] + sa[..., None] * bt[:, :, None, :] + vt[..., None] * kt[:, :, None, :]
        return S, jnp.einsum('bhvk,bhk->bhv', S, rt)

    s_last, ys = lax.scan(step, s0, xs)
    return jnp.moveaxis(ys, 0, 1), s_last


def head_group_norm(y, g, b):
    mu = jnp.mean(y, axis=-1, keepdims=True)
    yc = y - mu
    var = jnp.mean(yc * yc, axis=-1, keepdims=True)
    yn = (yc * lax.rsqrt(var + RW_GN_EPS)).reshape(y.shape[0], y.shape[1], -1)
    return yn * g.astype(jnp.float32) + b.astype(jnp.float32)


def rwkv7_mixer(r, k, v, wd, ad, gd, p, s0):
    B, T, _ = r.shape
    f32 = jnp.float32
    heads = lambda t: t.astype(f32).reshape(B, T, RW_HEADS, HEAD_DIM)
    r, k, v = heads(r), heads(k), heads(v)
    wd = wd.astype(f32).reshape(B, T, 2, RW_DECAY_LORA)
    ad = ad.astype(f32).reshape(B, T, 2, RW_AAA_LORA)
    kk = k * p['rw_kk'].astype(f32).reshape(RW_HEADS, HEAD_DIM)
    kk = kk * lax.rsqrt(jnp.sum(kk * kk, axis=-1, keepdims=True) + 1e-12)
    g = jax.nn.sigmoid(gd.astype(f32)) @ p['rw_g2'].astype(f32)
    ka = p['rw_ka'].astype(f32).reshape(RW_HEADS, HEAD_DIM)
    rk = p['rw_rk'].astype(f32)
    ys, bonuses, states = [], [], []
    for d in range(2):
        w_raw = p['rw_w0'][d].astype(f32) + jnp.tanh(wd[:, :, d]) @ p['rw_w2'][d].astype(f32)
        decay = heads(jnp.exp(-jnp.exp(-jax.nn.softplus(-w_raw) - 0.5)))
        a = heads(jax.nn.sigmoid(p['rw_a0'][d].astype(f32) + ad[:, :, d] @ p['rw_a2'][d].astype(f32)))
        kd = k * (1.0 + (a - 1.0) * ka)
        seq = (r, decay, kd, v, -kk, kk * a)
        if d == 1:
            seq = tuple(jnp.flip(t, axis=1) for t in seq)
        y, s_last = rwkv7_scan(*seq, s0[:, d].astype(f32))
        if d == 1:
            y = jnp.flip(y, axis=1)
        ys.append(y)
        bonuses.append(jnp.sum(r * kd * rk, axis=-1, keepdims=True) * v)
        states.append(s_last)
    y = head_group_norm(ys[0] + ys[1], p['rw_gn_g'], p['rw_gn_b']) + (bonuses[0] + bonuses[1]).reshape(B, T, RW_WIDTH)
    return y * g, jnp.stack(states, axis=1)


def chunk_gla(q, k, v, logf, s0):
    B, H, T, dk = q.shape
    dv = v.shape[-1]
    n = T // HG_CHUNK
    rs = lambda t: t.reshape(B, H, n, HG_CHUNK, t.shape[-1])
    q, k, v, logf = rs(q), rs(k), rs(v), rs(logf)
    b = jnp.cumsum(logf, axis=3)
    causal = jnp.tril(jnp.ones((HG_CHUNK, HG_CHUNK), dtype=bool))
    diff = b[:, :, :, :, None, :] - b[:, :, :, None, :, :]
    dec = jnp.where(causal[:, :, None], jnp.exp(jnp.minimum(diff, 0.0)), 0.0)
    att = jnp.einsum('bhntd,bhnsd,bhntsd->bhnts', q, k, dec)
    o_intra = jnp.einsum('bhnts,bhnsv->bhntv', att, v)
    b_last = b[:, :, :, -1, :]
    kv = jnp.einsum('bhnsd,bhnsv->bhndv', k * jnp.exp(b_last[:, :, :, None, :] - b), v)

    def step(S, inp):
        dl, kvn = inp
        return dl[..., None] * S + kvn, S

    s_last, s_before = lax.scan(step, s0, (jnp.moveaxis(jnp.exp(b_last), 2, 0), jnp.moveaxis(kv, 2, 0)))
    s_before = jnp.moveaxis(s_before, 0, 2)
    o_inter = jnp.einsum('bhntd,bhndv->bhntv', q * jnp.exp(b), s_before)
    return (o_intra + o_inter).reshape(B, H, T, dv), s_last


def hgrn2_mixer(q, f_raw, i, g, lb, norm_g, s0):
    B, T, _ = q.shape
    f32 = jnp.float32
    heads = lambda t: t.astype(f32).reshape(B, T, HG_HEADS, HEAD_DIM).transpose(0, 2, 1, 3)
    qh = heads(jax.nn.silu(q.astype(f32)))
    vh = heads(i)
    f_raw = f_raw.astype(f32).reshape(B, T, 2, HG_WIDTH)
    outs, states = [], []
    for d in range(2):
        lbd = lb[d]
        f = lbd + (1.0 - lbd) * jax.nn.sigmoid(f_raw[:, :, d])
        logf = jnp.log(jnp.maximum(f, HG_F_MIN))
        kf = 1.0 - f
        seq = (qh, heads(kf), vh, heads(logf))
        if d == 1:
            seq = tuple(jnp.flip(t, axis=2) for t in seq)
        o, s_last = chunk_gla(*seq, s0[:, d].astype(f32))
        if d == 1:
            o = jnp.flip(o, axis=2)
        outs.append(o)
        states.append(s_last)
    o = rms_norm(outs[0] + outs[1], norm_g).transpose(0, 2, 1, 3).reshape(B, T, HG_WIDTH)
    return o * jax.nn.silu(g.astype(f32)), jnp.stack(states, axis=1)


def moe(h, p):
    T, D = h.shape
    logits = (h @ p['router_w']).astype(jnp.float32) + p['router_b'].astype(jnp.float32)
    top_logit, top_idx = lax.top_k(logits, TOP_K)
    gates = jax.nn.softmax(top_logit, axis=-1)
    A = T * TOP_K
    flat_e = top_idx.reshape(A)
    flat_tok = jnp.arange(A, dtype=jnp.int32) // TOP_K
    flat_g = gates.reshape(A)
    order = jnp.argsort(flat_e)
    se = flat_e[order]
    counts = jnp.bincount(flat_e, length=N_EXPERTS)
    starts = jnp.cumsum(counts) - counts
    padded = (counts + MOE_BLOCK - 1) // MOE_BLOCK * MOE_BLOCK
    pends = jnp.cumsum(padded)
    pstarts = pends - padded
    dest = pstarts[se] + jnp.arange(A, dtype=jnp.int32) - starts[se]
    n_blocks = -(-(A + N_EXPERTS * (MOE_BLOCK - 1)) // MOE_BLOCK)
    n_slots = n_blocks * MOE_BLOCK
    slot_tok = jnp.full((n_slots,), T, dtype=jnp.int32).at[dest].set(flat_tok[order])
    slot_g = jnp.zeros((n_slots,), dtype=jnp.float32).at[dest].set(flat_g[order])
    block_e = jnp.clip(jnp.searchsorted(pends, jnp.arange(n_blocks, dtype=jnp.int32) * MOE_BLOCK, side='right'), 0, N_EXPERTS - 1)
    h_pad = jnp.concatenate([h, jnp.zeros((1, D), h.dtype)], axis=0)
    xb = h_pad[slot_tok].reshape(n_blocks, MOE_BLOCK, D)

    def expert_block(args):
        xblk, e = args
        gu = xblk @ p['moe_w_gu'][e] + p['moe_b_gu'][e]
        glu, lin = gu[:, :EXPERT_FF], gu[:, EXPERT_FF:]
        glu = jnp.minimum(glu, SWIGLU_LIMIT)
        lin = jnp.clip(lin, -SWIGLU_LIMIT, SWIGLU_LIMIT)
        act = glu * jax.nn.sigmoid(SWIGLU_ALPHA * glu) * (lin + 1.0)
        return act @ p['moe_w_down'][e] + p['moe_b_down'][e]

    yb = lax.map(expert_block, (xb, block_e)).reshape(n_slots, D)
    y = jax.ops.segment_sum(yb * slot_g[:, None].astype(yb.dtype), slot_tok, num_segments=T + 1)[:T]
    return y.astype(h.dtype)


def token_mixers(h, p, lb, cache):
    B, T, _ = h.shape
    proj = h @ p['w_in']
    split_points = np.cumsum(IN_SIZES)[:-1].tolist()
    (aq, ak, av, rr, rk, rv, rwd, rad, rgd, hq, hf, hi, hg) = jnp.split(proj, split_points, axis=-1)
    qh = rms_norm(aq.reshape(B, T, ATT_HEADS, HEAD_DIM), p['att_qnorm_g']).transpose(0, 2, 1, 3)
    kh = rms_norm(ak.reshape(B, T, ATT_KV_HEADS, HEAD_DIM), p['att_knorm_g']).transpose(0, 2, 1, 3)
    vh = av.reshape(B, T, ATT_KV_HEADS, HEAD_DIM).transpose(0, 2, 1, 3)
    if cache is None:
        att = block_attention(qh, kh, vh)
        rw_s0 = jnp.zeros((B, 2, RW_HEADS, HEAD_DIM, HEAD_DIM), jnp.float32)
        hg_s0 = jnp.zeros((B, 2, HG_HEADS, HEAD_DIM, HEAD_DIM), jnp.float32)
    else:
        ck, cv, rw_s0, hg_s0 = cache
        cos, sin = axial_rope(T)
        qr = apply_rope(qh, cos, sin)
        kr = apply_rope(kh, cos, sin)
        att = block_attention(qr, jnp.concatenate([ck.astype(kr.dtype), kr], axis=2), jnp.concatenate([cv.astype(vh.dtype), vh], axis=2))
    att = att.transpose(0, 2, 1, 3).reshape(B, T, ATT_WIDTH)
    rw_out, rw_state = rwkv7_mixer_pallas(proj[:, :, 768:1920], rw_s0.astype(F32), p)
    hg_out, hg_state = hgrn2_mixer_pallas(proj[:, :, 1920:3200], hg_s0.astype(F32), lb, p['hg_norm_g'])
    mix = jnp.concatenate([att.astype(h.dtype), rw_out.astype(h.dtype), hg_out.astype(h.dtype)], axis=-1)
    new_cache = (kh, vh, rw_state, hg_state) if cache is None else None
    return mix, new_cache


def layer(x, mod, p, lb, cache):
    shift1, scale1, gate1, shift2, scale2, gate2 = jnp.split(mod[:, None, :], 6, axis=-1)
    h = rms_norm(x, p['norm_mix_g']) * (1.0 + scale1) + shift1
    mix, new_cache = token_mixers(h, p, lb, cache)
    x = x + gate1 * (mix @ p['w_out'])
    h = rms_norm(x, p['norm_ffn_g']) * (1.0 + scale2) + shift2
    B, T, D = h.shape
    x = x + gate2 * moe(h.reshape(B * T, D), p).reshape(B, T, D)
    return x, new_cache


def hgrn_lower_bounds(hg_lb):
    sm = jax.nn.softmax(hg_lb.astype(jnp.float32), axis=0)
    return jnp.cumsum(sm, axis=0) - sm[0:1]


def kernel(x_prompt, x_sample, cache_att_k, cache_att_v, state_rwkv, state_hgrn, c, c_ctx, w_mod, b_mod, norm_mix_g, norm_ffn_g, w_in, w_out, att_qnorm_g, att_knorm_g, rw_w0, rw_w2, rw_a0, rw_a2, rw_g2, rw_kk, rw_ka, rw_rk, rw_gn_g, rw_gn_b, hg_lb, hg_norm_g, router_w, router_b, moe_w_gu, moe_b_gu, moe_w_down, moe_b_down, final_norm_g):
    BP, TP, _ = x_prompt.shape
    BS, TS, _ = x_sample.shape
    lb_all = hgrn_lower_bounds(hg_lb)
    cvec = jnp.concatenate([c_ctx[None, :], c, jnp.zeros((8 - 1 - BS, D_MODEL), F32)], axis=0)
    mod_all = adaln_mod_pallas(cvec, w_mod, b_mod).reshape(DEPTH, 8, 6, D_MODEL)
    zeros_state = jnp.zeros((BP, 2, RW_HEADS, HEAD_DIM, HEAD_DIM), F32)
    x = {'p': x_prompt.reshape(BP * TP, D_MODEL), 's': x_sample.reshape(BS * TS, D_MODEL)}
    dims = {'p': (TP, BP), 's': (TS, BS)}
    y_moe = {'p': None, 's': None}
    mod_prev = {'p': None, 's': None}
    ks, vs, srs, shs = [], [], [], []
    for l in range(DEPTH):
        prm = dict(rw_w0=rw_w0[l], rw_w2=rw_w2[l], rw_a0=rw_a0[l], rw_a2=rw_a2[l], rw_g2=rw_g2[l],
                   rw_kk=rw_kk[l], rw_ka=rw_ka[l], rw_rk=rw_rk[l], rw_gn_g=rw_gn_g[l], rw_gn_b=rw_gn_b[l])
        w_in_bf, w_out_bf = w_in[l], w_out[l].astype(BF16)
        mods = {'p': mod_all[l, 0:1], 's': mod_all[l, 1:1 + BS]}
        for s in ('p', 's'):
            T, B = dims[s]
            rpm = B * T if s == 'p' else T
            if l == 0:
                p_att, p_rw, p_hg = in_proj_pallas(x[s], mods[s], norm_mix_g[l], w_in_bf, rpm)
            else:
                x[s], p_att, p_rw, p_hg = in_proj_pallas(x[s], mods[s], norm_mix_g[l], w_in_bf, rpm,
                                                         res=(y_moe[s], mod_prev[s]))
            p_att, p_rw, p_hg = (t.reshape(B, T, -1) for t in (p_att, p_rw, p_hg))
            if s == 'p':
                att, k_l, v_l = attention_pallas(p_att, att_qnorm_g[l], att_knorm_g[l])
                rw_out, sr_l = rwkv7_mixer_pallas(p_rw, zeros_state, prm)
                hg_out, sh_l = hgrn2_mixer_pallas(p_hg, zeros_state, lb_all[l], hg_norm_g[l])
                ks.append(k_l)
                vs.append(v_l)
                srs.append(sr_l)
                shs.append(sh_l)
            else:
                att = attention_pallas(p_att, att_qnorm_g[l], att_knorm_g[l], (cache_att_k[:, l], cache_att_v[:, l]))
                rw_out, _ = rwkv7_mixer_pallas(p_rw, state_rwkv[:, l], prm)
                hg_out, _ = hgrn2_mixer_pallas(p_hg, state_hgrn[:, l], lb_all[l], hg_norm_g[l])
            x[s], h2, logits = out_proj_pallas(att.reshape(B * T, -1), rw_out.reshape(B * T, -1),
                                               hg_out.reshape(B * T, -1), x[s], mods[s], norm_ffn_g[l], w_out_bf,
                                               router_w[l], router_b[l], rpm)
            y_moe[s] = moe_pallas(h2, logits, l, moe_w_gu, moe_b_gu, moe_w_down, moe_b_down)
            mod_prev[s] = mods[s]
    y_prompt = final_norm_pallas(x['p'], y_moe['p'], mod_prev['p'], final_norm_g, BP * TP).reshape(x_prompt.shape)
    y_sample = final_norm_pallas(x['s'], y_moe['s'], mod_prev['s'], final_norm_g, TS).reshape(x_sample.shape)
    return (y_prompt, y_sample, jnp.stack(ks, axis=1), jnp.stack(vs, axis=1),
            jnp.stack(srs, axis=1), jnp.stack(shs, axis=1))
```

```python
import math
from functools import partial

import jax
import jax.numpy as jnp
from jax import lax
from jax.experimental import pallas as pl
from jax.experimental.pallas import tpu as pltpu
from jax.experimental.pallas import tpu_sc as plsc

D_MODEL = 1024
DEPTH = 2
GRID_W = 64
HEAD_DIM = 64
ATT_HEADS = 8
ATT_KV_HEADS = 2
ATT_WIDTH = ATT_HEADS * HEAD_DIM
KV_WIDTH = ATT_KV_HEADS * HEAD_DIM
RW_HEADS = 4
RW_WIDTH = RW_HEADS * HEAD_DIM
RW_GN_EPS = 64e-5
HG_HEADS = 4
HG_WIDTH = HG_HEADS * HEAD_DIM
HG_F_MIN = 1e-6
N_EXPERTS = 32
TOP_K = 4
EXPERT_FF = D_MODEL
SWIGLU_LIMIT = 7.0
SWIGLU_ALPHA = 1.702
ROPE_THETA = 10000.0
NORM_EPS = 1e-6

RW_CHUNK = 64
BF16 = jnp.bfloat16
F32 = jnp.float32

_NN = (((1,), (0,)), ((), ()))
_NT = (((1,), (1,)), ((), ()))
_TN = (((0,), (0,)), ((), ()))


def _split(x, n):
    parts = []
    for _ in range(n - 1):
        hi = x.astype(BF16)
        parts.append(hi)
        x = x - hi.astype(F32)
    parts.append(x.astype(BF16))
    return parts


def _mm(a, b, dims=_NN, passes=1):
    d = lambda x, y: lax.dot_general(x, y, dims, preferred_element_type=F32)
    if passes == 1:
        return d(a.astype(BF16), b.astype(BF16))
    ah, al = _split(a, 2)
    bh, bl = _split(b, 2)
    return d(ah, bl) + d(al, bh) + d(ah, bh)


def _mm_exact_lhs(a01, b, n=3):
    a = a01.astype(BF16)
    out = None
    for t in reversed(_split(b, n)):
        y = lax.dot_general(a, t, _NN, preferred_element_type=F32)
        out = y if out is None else out + y
    return out


def _mm_exact_rhs(a, b01, n=3):
    b = b01.astype(BF16)
    out = None
    for t in reversed(_split(a, n)):
        y = lax.dot_general(t, b, _NN, preferred_element_type=F32)
        out = y if out is None else out + y
    return out


def _head_blockdiag(width):
    r = lax.broadcasted_iota(jnp.int32, (width, width), 0) // HEAD_DIM
    c = lax.broadcasted_iota(jnp.int32, (width, width), 1) // HEAD_DIM
    return (r == c).astype(F32)


def _sigmoid(x):
    return 1.0 / (1.0 + jnp.exp(-x))


def _softplus(x):
    return jnp.maximum(x, 0.0) + jnp.log(1.0 + jnp.exp(-jnp.abs(x)))


def _rwkv_kernel(p_ref, s0_ref, w0_ref, w2_ref, a0_ref, a2_ref, g2_ref, kk_ref, ka_ref, rk_ref, gng_ref, gnb_ref,
                 out_ref, st_ref,
                 lw_scr, kd_scr, bb_scr, y_scr, kk_scr, s_scr, *, T, NB):
    C = RW_CHUNK
    n_chunks = T // C
    bd = _head_blockdiag(RW_WIDTH)
    seg = lambda t: _mm_exact_rhs(t, bd)
    ka = ka_ref[...]
    for nb in range(NB):
        k = p_ref[nb, :, 256:512]
        kk = k * kk_ref[...]
        kk = kk * lax.rsqrt(seg(kk * kk) + 1e-12)
        kk_scr[nb] = kk
        for d in range(2):
            wd = p_ref[nb, :, 768 + 64 * d:832 + 64 * d]
            ad = p_ref[nb, :, 896 + 64 * d:960 + 64 * d]
            w_raw = w0_ref[d:d + 1, :] + _mm(jnp.tanh(wd), w2_ref[d], passes=3)
            lw_scr[nb, d] = -jnp.exp(-_softplus(-w_raw) - 0.5)
            a = _sigmoid(a0_ref[d:d + 1, :] + _mm(ad, a2_ref[d], passes=3))
            kd_scr[nb, d] = k * (1.0 + (a - 1.0) * ka)
            bb_scr[nb, d] = kk * a
    s_scr[...] = s0_ref[...]

    ti = lax.broadcasted_iota(jnp.int32, (C, C), 0)
    si = lax.broadcasted_iota(jnp.int32, (C, C), 1)
    ones_cc = jnp.ones((C, C), F32)

    def chunk_body(i, carry):
        ch = []
        for nb, d in [(nb, d) for nb in range(NB) for d in range(2)]:
            ci = i if d == 0 else n_chunks - 1 - i
            rows = pl.ds(pl.multiple_of(ci * C, C), C)
            strict = (ti > si) if d == 0 else (ti < si)
            incl = (ti >= si) if d == 0 else (ti <= si)
            lw = lw_scr[nb, d, rows, :]
            cum = _mm_exact_lhs(incl.astype(F32), lw)
            total = _mm_exact_lhs(ones_cc, lw)
            cum_ex = cum - lw
            mid = 0.5 * total
            rr = p_ref[nb, rows, 0:256]
            vv = p_ref[nb, rows, 512:768]
            kdc = kd_scr[nb, d, rows, :]
            bbc = bb_scr[nb, d, rows, :]
            kkc = kk_scr[nb, rows, :]
            e_inv = jnp.exp(mid - cum)
            At = -kkc * jnp.exp(cum_ex - mid)
            Rt = rr * jnp.exp(cum - mid)
            Bt = bbc * e_inv
            Kt = kdc * e_inv
            Ap = -kkc * jnp.exp(cum_ex)
            Rp = rr * jnp.exp(cum)
            e_out = jnp.exp(total - cum)
            Bh = bbc * e_out
            Kh = kdc * e_out
            e_tot = jnp.exp(total[0:1, :])
            for h in range(RW_HEADS):
                hs = slice(h * HEAD_DIM, (h + 1) * HEAD_DIM)
                ch.append(dict(nb=nb, d=d, h=h, rows=rows, hs=hs, strict=strict, incl=incl,
                               AR=jnp.concatenate([At[:, hs], Rt[:, hs]], axis=0), Bt=Bt[:, hs], Kt=Kt[:, hs],
                               V=vv[:, hs], X1=Ap[:, hs], Rp=Rp[:, hs], Bh=Bh[:, hs], Kh=Kh[:, hs],
                               e_tot=e_tot[:, hs]))
        for c in ch:
            c['AB'] = _mm(c['AR'], c['Bt'], _NT)
            c['AK'] = _mm(c['AR'], c['Kt'], _NT)
        for c in ch:
            c['P'] = jnp.where(c['strict'], c['AB'][:C], 0.0)
            c['A_ak'] = jnp.where(c['strict'], c['AK'][:C], 0.0)
            c['A_rb'] = jnp.where(c['incl'], c['AB'][C:], 0.0)
            c['A_rk'] = jnp.where(c['incl'], c['AK'][C:], 0.0)
        for c in ch:
            c['X2'] = _mm(c['A_ak'], c['V'])
        for lvl in range(6):
            for c in ch:
                if lvl < 5:
                    c['PZ'] = _mm(c['P'], jnp.concatenate([c['P'], c['X1'], c['X2']], axis=1))
                else:
                    c['PZ'] = _mm(c['P'], jnp.concatenate([c['X1'], c['X2']], axis=1))
            for c in ch:
                PZ = c['PZ']
                if lvl < 5:
                    c['P'] = PZ[:, :C]
                    c['X1'] = c['X1'] + PZ[:, C:2 * C]
                    c['X2'] = c['X2'] + PZ[:, 2 * C:]
                else:
                    c['X1'] = c['X1'] + PZ[:, :C]
                    c['X2'] = c['X2'] + PZ[:, C:]
        for c in ch:
            c['S0'] = s_scr[c['nb'], c['d'], c['h']]
            c['UY'] = _mm(jnp.concatenate([c['X1'], c['Rp']], axis=0), c['S0'], _NT)
        for c in ch:
            c['U'] = c['UY'][:C] + c['X2']
        for c in ch:
            c['Y'] = c['UY'][C:] + _mm(c['A_rb'], c['U']) + _mm(c['A_rk'], c['V'])
            c['S1'] = c['S0'] * c['e_tot'] + _mm(c['U'], c['Bh'], _TN) + _mm(c['V'], c['Kh'], _TN)
        for c in ch:
            s_scr[c['nb'], c['d'], c['h']] = c['S1']
            y_scr[c['nb'], c['d'], c['rows'], c['hs']] = c['Y']
        return carry

    lax.fori_loop(0, n_chunks, chunk_body, 0)

    for nb in range(NB):
        r = p_ref[nb, :, 0:256]
        v = p_ref[nb, :, 512:768]
        bonus = seg(r * (kd_scr[nb, 0] + kd_scr[nb, 1]) * rk_ref[...]) * v
        g = _mm(_sigmoid(p_ref[nb, :, 1024:1152]), g2_ref[...], passes=3)
        y = y_scr[nb, 0] + y_scr[nb, 1]
        mu = seg(y) * (1.0 / HEAD_DIM)
        yc = y - mu
        var = seg(yc * yc) * (1.0 / HEAD_DIM)
        yn = yc * lax.rsqrt(var + RW_GN_EPS)
        out_ref[nb] = (yn * gng_ref[...] + gnb_ref[...] + bonus) * g
    st_ref[...] = s_scr[...]


RW_ROWS = 1024


def rwkv7_mixer_pallas(p_rw, s0, prm):
    B, T, W = p_rw.shape
    NB = max(2, RW_ROWS // T)
    row = lambda a: a.reshape(1, RW_WIDTH)
    full = lambda shape: pl.BlockSpec(shape, lambda b: (0,) * len(shape))
    st_spec = pl.BlockSpec((NB, 2, RW_HEADS, HEAD_DIM, HEAD_DIM), lambda b: (b, 0, 0, 0, 0))
    return pl.pallas_call(
        partial(_rwkv_kernel, T=T, NB=NB),
        grid=(B // NB,),
        in_specs=[pl.BlockSpec((NB, T, W), lambda b: (b, 0, 0)), st_spec,
                  full((2, RW_WIDTH)), full((2, 64, RW_WIDTH)), full((2, RW_WIDTH)), full((2, 64, RW_WIDTH)),
                  full((128, RW_WIDTH)), full((1, RW_WIDTH)), full((1, RW_WIDTH)), full((1, RW_WIDTH)),
                  full((1, RW_WIDTH)), full((1, RW_WIDTH))],
        out_specs=[pl.BlockSpec((NB, T, RW_WIDTH), lambda b: (b, 0, 0)), st_spec],
        out_shape=[jax.ShapeDtypeStruct((B, T, RW_WIDTH), F32),
                   jax.ShapeDtypeStruct((B, 2, RW_HEADS, HEAD_DIM, HEAD_DIM), F32)],
        scratch_shapes=[pltpu.VMEM((NB, 2, T, RW_WIDTH), F32)] * 4
        + [pltpu.VMEM((NB, T, RW_WIDTH), F32), pltpu.VMEM((NB, 2, RW_HEADS, HEAD_DIM, HEAD_DIM), F32)],
        compiler_params=pltpu.CompilerParams(dimension_semantics=("arbitrary",), vmem_limit_bytes=56 * 1024 * 1024),
        name="rwkv7_mixer",
    )(p_rw, s0, prm['rw_w0'], prm['rw_w2'], prm['rw_a0'], prm['rw_a2'], prm['rw_g2'], row(prm['rw_kk']),
      row(prm['rw_ka']), row(prm['rw_rk']), row(prm['rw_gn_g']), row(prm['rw_gn_b']))


HG_SUB = 16
HG_ROWS = 64


def _hgrn_kernel(p_ref, s0_ref, lb_ref, ng_ref, out_ref, st_ref, lf_scr, kf_scr, o_scr, s_scr, *, T):
    R, c = HG_ROWS, HG_SUB
    n_it = T // R
    x = p_ref[0]
    bd = _head_blockdiag(HG_WIDTH)
    seg = lambda t: _mm_exact_rhs(t, bd)
    for d in range(2):
        lbd = lb_ref[d:d + 1, :]
        f = lbd + (1.0 - lbd) * _sigmoid(x[:, 256 + 256 * d:512 + 256 * d])
        lf_scr[d] = jnp.log(jnp.maximum(f, HG_F_MIN))
        kf_scr[d] = 1.0 - f
        for h in range(HG_HEADS):
            s_scr[d, h] = s0_ref[0, d, h].T

    ti = lax.broadcasted_iota(jnp.int32, (R, R), 0)
    si = lax.broadcasted_iota(jnp.int32, (R, R), 1)
    same_blk = (ti // c) == (si // c)
    t16 = lax.broadcasted_iota(jnp.int32, (c, 1), 0)

    def body(i, carry):
        for d in range(2):
            ci = i if d == 0 else n_it - 1 - i
            rows = pl.ds(pl.multiple_of(ci * R, R), R)
            incl = (ti >= si) if d == 0 else (ti <= si)
            lf = lf_scr[d, rows, :]
            cum = _mm_exact_lhs((incl & same_blk).astype(F32), lf)
            tot = _mm_exact_lhs(same_blk.astype(F32), lf)
            xq = p_ref[0, rows, 0:256]
            q = xq * _sigmoid(xq)
            v = p_ref[0, rows, 768:1024]
            kf = kf_scr[d, rows, :]
            Qp = q * jnp.exp(cum)
            Kh = kf * jnp.exp(tot - cum)
            e_tot = jnp.exp(tot)
            blocks = range(R // c) if d == 0 else range(R // c - 1, -1, -1)
            o_parts = [None] * (R // c)
            for j in blocks:
                rs = slice(j * c, (j + 1) * c)
                cb, qb, kb, vb = cum[rs], q[rs], kf[rs], v[rs]
                prods = []
                for s in range(c):
                    e = jnp.exp(jnp.minimum(cb - cb[s:s + 1, :], 0.0))
                    prods.append(qb * (kb[s:s + 1, :] * e))
                att = _mm_exact_rhs(jnp.concatenate(prods, axis=0), bd, n=1)
                o_blk = jnp.zeros((c, HG_WIDTH), F32)
                for s in range(c):
                    keep = (t16 >= s) if d == 0 else (t16 <= s)
                    o_blk = o_blk + jnp.where(keep, att[s * c:(s + 1) * c], 0.0) * vb[s:s + 1, :]
                o_heads = []
                for h in range(HG_HEADS):
                    hs = slice(h * HEAD_DIM, (h + 1) * HEAD_DIM)
                    ST = s_scr[d, h]
                    o_heads.append(_mm(Qp[rs, hs], ST, _NT))
                    s_scr[d, h] = ST * e_tot[j * c:j * c + 1, hs] + _mm(vb[:, hs], Kh[rs, hs], _TN)
                o_parts[j] = o_blk + jnp.concatenate(o_heads, axis=1)
            o_scr[d, rows, :] = jnp.concatenate(o_parts, axis=0)
        return carry

    lax.fori_loop(0, n_it, body, 0)

    o = o_scr[0] + o_scr[1]
    o = o * lax.rsqrt(seg(o * o) * (1.0 / HEAD_DIM) + NORM_EPS) * ng_ref[...]
    gg = x[:, 1024:1280]
    out_ref[0] = o * (gg * _sigmoid(gg))
    for d in range(2):
        for h in range(HG_HEADS):
            st_ref[0, d, h] = s_scr[d, h].T


def hgrn2_mixer_pallas(p_hg, s0, lb, norm_g):
    B, T, W = p_hg.shape
    full = lambda shape: pl.BlockSpec(shape, lambda b: (0,) * len(shape))
    st_spec = pl.BlockSpec((1, 2, HG_HEADS, HEAD_DIM, HEAD_DIM), lambda b: (b, 0, 0, 0, 0))
    return pl.pallas_call(
        partial(_hgrn_kernel, T=T),
        grid=(B,),
        in_specs=[pl.BlockSpec((1, T, W), lambda b: (b, 0, 0)), st_spec, full((2, HG_WIDTH)), full((1, HG_WIDTH))],
        out_specs=[pl.BlockSpec((1, T, HG_WIDTH), lambda b: (b, 0, 0)), st_spec],
        out_shape=[jax.ShapeDtypeStruct((B, T, HG_WIDTH), F32),
                   jax.ShapeDtypeStruct((B, 2, HG_HEADS, HEAD_DIM, HEAD_DIM), F32)],
        scratch_shapes=[pltpu.VMEM((2, T, HG_WIDTH), F32)] * 3
        + [pltpu.VMEM((2, HG_HEADS, HEAD_DIM, HEAD_DIM), F32)],
        compiler_params=pltpu.CompilerParams(dimension_semantics=("arbitrary",), vmem_limit_bytes=48 * 1024 * 1024),
        name="hgrn2_mixer",
    )(p_hg, s0, lb, jnp.tile(norm_g.reshape(1, HEAD_DIM), (1, HG_HEADS)))


ATT_REP = ATT_HEADS // ATT_KV_HEADS
ATT_QROWS = 128


def _swap_pairs(x):
    w = x.shape[-1]
    lane = lax.broadcasted_iota(jnp.int32, x.shape, x.ndim - 1)
    return jnp.where(lane % 2 == 0, pltpu.roll(x, w - 1, x.ndim - 1), pltpu.roll(x, 1, x.ndim - 1))


def _att_kernel(*refs, T, past, rope):
    if rope:
        p_ref, qg_ref, kg_ref, cos_ref, sin_ref, ck_ref, cv_ref, out_ref, k_scr, v_scr, q_scr = refs
    else:
        p_ref, qg_ref, kg_ref, out_ref, kh_ref, vh_ref, k_scr, v_scr, q_scr = refs
    x = p_ref[0]
    q = x[:, 0:ATT_WIDTH]
    k = x[:, ATT_WIDTH:ATT_WIDTH + KV_WIDTH]
    v = x[:, ATT_WIDTH + KV_WIDTH:ATT_WIDTH + 2 * KV_WIDTH]
    inv_d = 1.0 / HEAD_DIM
    q = q * lax.rsqrt(_mm_exact_rhs(q * q, _head_blockdiag(ATT_WIDTH)) * inv_d + NORM_EPS) * qg_ref[...]
    k = k * lax.rsqrt(_mm_exact_rhs(k * k, _head_blockdiag(KV_WIDTH)) * inv_d + NORM_EPS) * kg_ref[...]
    if rope:
        cos, sin = cos_ref[...], sin_ref[...]
        rep = ATT_WIDTH // KV_WIDTH
        q = q * jnp.concatenate([cos] * rep, axis=1) + _swap_pairs(q) * jnp.concatenate([sin] * rep, axis=1)
        k = k * cos + _swap_pairs(k) * sin
    q_scr[...] = (q * (1.0 / math.sqrt(HEAD_DIM))).astype(BF16)
    for g in range(ATT_KV_HEADS):
        gs = slice(g * HEAD_DIM, (g + 1) * HEAD_DIM)
        if rope:
            k_scr[g, 0:past, :] = ck_ref[0, g].astype(BF16)
            v_scr[g, 0:past, :] = cv_ref[0, g].astype(BF16)
        else:
            kh_ref[0, g] = k[:, gs]
            vh_ref[0, g] = v[:, gs]
        k_scr[g, past:past + T, :] = k[:, gs].astype(BF16)
        v_scr[g, past:past + T, :] = v[:, gs].astype(BF16)
    QR = ATT_QROWS

    def q_block(qb, carry):
        rows = pl.ds(pl.multiple_of(qb * QR, QR), QR)
        qblk = q_scr[rows, :]
        for g in range(ATT_KV_HEADS):
            qs = jnp.concatenate([qblk[:, (g * ATT_REP + r) * HEAD_DIM:(g * ATT_REP + r + 1) * HEAD_DIM]
                                  for r in range(ATT_REP)], axis=0)
            s = lax.dot_general(qs, k_scr[g], _NT, preferred_element_type=F32)
            e = jnp.exp(s - jnp.max(s, axis=-1, keepdims=True))
            l = jnp.sum(e, axis=-1, keepdims=True)
            o = lax.dot_general(e.astype(BF16), v_scr[g], _NN, preferred_element_type=F32) / l
            for r in range(ATT_REP):
                h = g * ATT_REP + r
                out_ref[0, rows, h * HEAD_DIM:(h + 1) * HEAD_DIM] = o[r * QR:(r + 1) * QR]
        return carry

    lax.fori_loop(0, T // QR, q_block, 0)


def rope_tables(T):
    rows = T // GRID_W
    row = jnp.repeat(jnp.arange(rows, dtype=F32), GRID_W)
    col = jnp.tile(jnp.arange(GRID_W, dtype=F32), rows)
    n_freq = HEAD_DIM // 4
    inv = ROPE_THETA ** (-jnp.arange(n_freq, dtype=F32) / n_freq)
    ang = jnp.concatenate([row[:, None] * inv, col[:, None] * inv], axis=-1)
    cos = jnp.repeat(jnp.cos(ang), 2, axis=-1)
    sin = jnp.stack([-jnp.sin(ang), jnp.sin(ang)], axis=-1).reshape(T, HEAD_DIM)
    return jnp.tile(cos, (1, ATT_KV_HEADS)), jnp.tile(sin, (1, ATT_KV_HEADS))


def attention_pallas(p_att, qnorm_g, knorm_g, cache=None):
    B, T, W = p_att.shape
    rope = cache is not None
    past = cache[0].shape[2] if rope else 0
    full = lambda shape: pl.BlockSpec(shape, lambda b: (0,) * len(shape))
    qg = jnp.tile(qnorm_g.reshape(1, HEAD_DIM), (1, ATT_HEADS))
    kg = jnp.tile(knorm_g.reshape(1, HEAD_DIM), (1, ATT_KV_HEADS))
    in_specs = [pl.BlockSpec((1, T, W), lambda b: (b, 0, 0)), full((1, ATT_WIDTH)), full((1, KV_WIDTH))]
    args = [p_att, qg, kg]
    out_specs = [pl.BlockSpec((1, T, ATT_WIDTH), lambda b: (b, 0, 0))]
    out_shape = [jax.ShapeDtypeStruct((B, T, ATT_WIDTH), F32)]
    if rope:
        cos, sin = rope_tables(T)
        kv_spec = pl.BlockSpec((1, ATT_KV_HEADS, past, HEAD_DIM), lambda b: (b, 0, 0, 0))
        in_specs += [full((T, KV_WIDTH)), full((T, KV_WIDTH)), kv_spec, kv_spec]
        args += [cos, sin, cache[0], cache[1]]
    else:
        kv_spec = pl.BlockSpec((1, ATT_KV_HEADS, T, HEAD_DIM), lambda b: (b, 0, 0, 0))
        out_specs += [kv_spec, kv_spec]
        out_shape += [jax.ShapeDtypeStruct((B, ATT_KV_HEADS, T, HEAD_DIM), F32)] * 2
    res = pl.pallas_call(
        partial(_att_kernel, T=T, past=past, rope=rope),
        grid=(B,),
        in_specs=in_specs,
        out_specs=out_specs,
        out_shape=out_shape,
        scratch_shapes=[pltpu.VMEM((ATT_KV_HEADS, past + T, HEAD_DIM), BF16)] * 2
        + [pltpu.VMEM((T, ATT_WIDTH), BF16)],
        compiler_params=pltpu.CompilerParams(dimension_semantics=("arbitrary",), vmem_limit_bytes=48 * 1024 * 1024),
        name="attention_rope" if rope else "attention_ctx",
    )(*args)
    return res[0] if rope else tuple(res)


ROW_TILE = 256
MOD_TILE = 1536
ROUTE_TILE = 256
MOE_SLOTS = 256
SC_ROWS = 32
SC_LANES = 128
P_ATT, P_RW, P_HG = ATT_WIDTH + 2 * KV_WIDTH, 3 * RW_WIDTH + 384, 5 * HG_WIDTH


def _mod_kernel(c_ref, w_ref, b_ref, o_ref):
    c = c_ref[...]
    o_ref[0] = _mm(c * _sigmoid(c), w_ref[0], passes=3) + b_ref[0]


def adaln_mod_pallas(cvec, w_mod, b_mod):
    n = 6 * D_MODEL
    return pl.pallas_call(
        _mod_kernel,
        grid=(DEPTH, n // MOD_TILE),
        in_specs=[pl.BlockSpec((8, D_MODEL), lambda l, j: (0, 0)),
                  pl.BlockSpec((1, D_MODEL, MOD_TILE), lambda l, j: (l, 0, j)),
                  pl.BlockSpec((1, 1, MOD_TILE), lambda l, j: (l, 0, j))],
        out_specs=pl.BlockSpec((1, 8, MOD_TILE), lambda l, j: (l, 0, j)),
        out_shape=jax.ShapeDtypeStruct((DEPTH, 8, n), F32),
        compiler_params=pltpu.CompilerParams(dimension_semantics=("arbitrary", "arbitrary"),
                                             vmem_limit_bytes=48 * 1024 * 1024),
        name="adaln_mod",
    )(cvec, w_mod, b_mod.reshape(DEPTH, 1, n))


def _rms(x):
    return x * lax.rsqrt(jnp.mean(x * x, axis=-1, keepdims=True) + NORM_EPS)


def _moe_residual(x_ref, g_ref, g4_ref, pm_ref):
    y = g4_ref[:, 0:1] * g_ref[0]
    for k in range(1, TOP_K):
        y = y + g4_ref[:, k:k + 1] * g_ref[k]
    return x_ref[...] + pm_ref[0, 5:6, :] * y


def _in_kernel(*refs, has_res):
    if has_res:
        x_ref, gth_ref, g4_ref, pm_ref, m_ref, g_ref, w_ref, xo_ref, pa_ref, pr_ref, ph_ref = refs
        x = _moe_residual(x_ref, gth_ref, g4_ref, pm_ref)
        xo_ref[...] = x
    else:
        x_ref, m_ref, g_ref, w_ref, pa_ref, pr_ref, ph_ref = refs
        x = x_ref[...]
    h = _rms(x) * g_ref[...] * (1.0 + m_ref[0, 1:2, :]) + m_ref[0, 0:1, :]
    proj = lax.dot_general(h, w_ref[...], _NN, precision=lax.Precision.DEFAULT, preferred_element_type=F32)
    pa_ref[...] = proj[:, 0:P_ATT]
    pr_ref[...] = proj[:, P_ATT:P_ATT + P_RW]
    ph_ref[...] = proj[:, P_ATT + P_RW:]


def _res_specs(row0):
    t0 = row0 // ROW_TILE
    return [pl.BlockSpec((TOP_K, ROW_TILE, D_MODEL), lambda i: (0, i + t0, 0)),
            pl.BlockSpec((ROW_TILE, TOP_K), lambda i: (i + t0, 0))]


def in_proj_pallas(x, mod, norm_g, w_in, rows_per_mod, res=None):
    R = x.shape[0]
    tpm = rows_per_mod // ROW_TILE
    rt = lambda w: pl.BlockSpec((ROW_TILE, w), lambda i: (i, 0))
    ms = pl.BlockSpec((1, 6, D_MODEL), lambda i: (i // tpm, 0, 0))
    full = lambda shape, **kw: pl.BlockSpec(shape, lambda i: (0,) * len(shape), **kw)
    in_specs = [rt(D_MODEL)] + (_res_specs(res[2]) + [ms] if res else []) + [
        ms, full((1, D_MODEL)), full((D_MODEL, w_in.shape[1]), pipeline_mode=pl.Buffered(1))]
    args = [x] + ([res[0], res[1], res[3]] if res else []) + [mod, norm_g.reshape(1, D_MODEL), w_in]
    widths = ([D_MODEL] if res else []) + [P_ATT, P_RW, P_HG]
    return pl.pallas_call(
        partial(_in_kernel, has_res=res is not None),
        grid=(R // ROW_TILE,),
        in_specs=in_specs,
        out_specs=[rt(w) for w in widths],
        out_shape=[jax.ShapeDtypeStruct((R, w), F32) for w in widths],
        compiler_params=pltpu.CompilerParams(dimension_semantics=("arbitrary",), vmem_limit_bytes=48 * 1024 * 1024),
        name="in_proj",
    )(*args)


def _out_kernel(att_ref, rw_ref, hg_ref, x_ref, m_ref, g_ref, w_ref, rw_w_ref, rb_ref, xo_ref, h_ref, lg_ref):
    d = lambda a, lo, hi: lax.dot_general(a, w_ref[lo:hi, :], _NN, precision=lax.Precision.DEFAULT,
                                          preferred_element_type=F32)
    mixo = (d(att_ref[...], 0, ATT_WIDTH) + d(rw_ref[...], ATT_WIDTH, ATT_WIDTH + RW_WIDTH)
            + d(hg_ref[...], ATT_WIDTH + RW_WIDTH, ATT_WIDTH + RW_WIDTH + HG_WIDTH))
    x = x_ref[...] + m_ref[0, 2:3, :] * mixo
    xo_ref[...] = x
    h = _rms(x) * g_ref[...] * (1.0 + m_ref[0, 4:5, :]) + m_ref[0, 3:4, :]
    h_ref[...] = h
    lg_ref[...] = _mm(h, rw_w_ref[...], passes=3) + rb_ref[...]


def out_proj_pallas(att, rw, hg, x, mod, norm_g, w_out, router_w, router_b, rows_per_mod):
    R = x.shape[0]
    tpm = rows_per_mod // ROW_TILE
    rt = lambda w: pl.BlockSpec((ROW_TILE, w), lambda i: (i, 0))
    full = lambda shape: pl.BlockSpec(shape, lambda i: (0,) * len(shape))
    return pl.pallas_call(
        _out_kernel,
        grid=(R // ROW_TILE,),
        in_specs=[rt(ATT_WIDTH), rt(RW_WIDTH), rt(HG_WIDTH), rt(D_MODEL),
                  pl.BlockSpec((1, 6, D_MODEL), lambda i: (i // tpm, 0, 0)), full((1, D_MODEL)),
                  full((D_MODEL, D_MODEL)), full((D_MODEL, N_EXPERTS)), full((1, N_EXPERTS))],
        out_specs=[rt(D_MODEL), rt(D_MODEL), rt(N_EXPERTS)],
        out_shape=[jax.ShapeDtypeStruct((R, D_MODEL), F32), jax.ShapeDtypeStruct((R, D_MODEL), F32),
                   jax.ShapeDtypeStruct((R, N_EXPERTS), F32)],
        compiler_params=pltpu.CompilerParams(dimension_semantics=("arbitrary",), vmem_limit_bytes=48 * 1024 * 1024),
        name="out_proj",
    )(att, rw, hg, x, mod, norm_g.reshape(1, D_MODEL), w_out, router_w, router_b.reshape(1, N_EXPERTS))


def _final_kernel(x_ref, gth_ref, g4_ref, m_ref, g_ref, o_ref):
    o_ref[...] = _rms(_moe_residual(x_ref, gth_ref, g4_ref, m_ref)) * g_ref[...]


def final_norm_pallas(x, gathered, gate4, row0, mod, norm_g, rows_per_mod):
    R = x.shape[0]
    tpm = rows_per_mod // ROW_TILE
    rt = pl.BlockSpec((ROW_TILE, D_MODEL), lambda i: (i, 0))
    return pl.pallas_call(
        _final_kernel,
        grid=(R // ROW_TILE,),
        in_specs=[rt] + _res_specs(row0) + [pl.BlockSpec((1, 6, D_MODEL), lambda i: (i // tpm, 0, 0)),
                                            pl.BlockSpec((1, D_MODEL), lambda i: (0, 0))],
        out_specs=rt,
        out_shape=jax.ShapeDtypeStruct((R, D_MODEL), F32),
        name="final_norm",
    )(x, gathered, gate4, mod, norm_g.reshape(1, D_MODEL))


def _moe_max_blocks(n_tok):
    return (n_tok * TOP_K + N_EXPERTS * (MOE_SLOTS - 1)) // MOE_SLOTS


def _exact_nt_ones(a):
    ones = jnp.ones((8, a.shape[1]), BF16)
    out = None
    for t in reversed(_split(a, 3)):
        y = lax.dot_general(ones, t, _NT, preferred_element_type=F32)
        out = y if out is None else out + y
    return out


def _route_kernel(lg_ref, dest_ref, gate4_ref, blk_ref, rank_scr, gate_scr, *, n_tok):
    Rt, E = ROUTE_TILE, N_EXPERTS
    n_tiles = n_tok // Rt
    lane = lax.broadcasted_iota(jnp.int32, (Rt, E), 1)
    ti = lax.broadcasted_iota(jnp.int32, (Rt, Rt), 0)
    si = lax.broadcasted_iota(jnp.int32, (Rt, Rt), 1)
    lower = (ti > si).astype(BF16)
    ei = lax.broadcasted_iota(jnp.int32, (E, E), 0)
    ej = lax.broadcasted_iota(jnp.int32, (E, E), 1)
    upper_e = (ei < ej).astype(BF16)

    def tile_members(it, off):
        rows = pl.ds(pl.multiple_of(it * Rt, Rt), Rt)
        lg = lg_ref[rows, :]
        work = lg
        member = jnp.zeros((Rt, E), jnp.bool_)
        top = None
        for k in range(TOP_K):
            m = jnp.max(work, axis=-1, keepdims=True)
            if top is None:
                top = m
            first = jnp.min(jnp.where(work == m, lane, E), axis=-1, keepdims=True)
            pick = lane == first
            member = member | pick
            work = jnp.where(pick, -jnp.inf, work)
        ex = jnp.where(member, jnp.exp(lg - top), 0.0)
        gate_scr[rows, :] = ex / jnp.sum(ex, axis=-1, keepdims=True)
        mem = member.astype(BF16)
        rank = lax.dot_general(lower, mem, _NN, preferred_element_type=F32) + off
        rank_scr[rows, :] = jnp.where(member, rank, -1.0)
        return off + jnp.sum(mem.astype(F32), axis=0, keepdims=True)

    count = lax.fori_loop(0, n_tiles, tile_members, jnp.zeros((1, E), F32))
    nblk = jnp.floor((count + (MOE_SLOTS - 1)) * (1.0 / MOE_SLOTS))
    bstart = lax.dot_general(nblk.astype(BF16), upper_e, _NN, preferred_element_type=F32)
    bend = bstart + nblk
    pstart = bstart * MOE_SLOTS

    def tile_slots(it, carry):
        rows = pl.ds(pl.multiple_of(it * Rt, Rt), Rt)
        rank = rank_scr[rows, :]
        gate = gate_scr[rows, :]
        member = rank >= 0.0
        kidx = lax.dot_general(member.astype(BF16), upper_e, _NN, preferred_element_type=F32)
        slot = pstart + rank
        cols = []
        for k in range(TOP_K):
            sel = member & (kidx == k)
            dest_ref[k:k + 1, rows] = _exact_nt_ones(jnp.where(sel, slot, 0.0))[0:1].astype(jnp.int32)
            cols.append(jnp.sum(jnp.where(sel, gate, 0.0), axis=-1, keepdims=True))
        gate4_ref[rows, :] = jnp.concatenate(cols, axis=1)
        return carry

    lax.fori_loop(0, n_tiles, tile_slots, 0)
    nb = blk_ref.shape[1]
    bi = lax.broadcasted_iota(jnp.int32, (E, nb), 1).astype(F32)
    bend_col = jnp.sum(jnp.where(ei == ej, jnp.broadcast_to(bend, (E, E)), 0.0), axis=1, keepdims=True)
    owner = jnp.sum((bend_col <= bi).astype(F32), axis=0, keepdims=True)
    blk_ref[0:1, :] = jnp.minimum(owner, E - 1.0).astype(jnp.int32)
    blk_ref[1:2, :] = jnp.broadcast_to(jnp.sum(nblk, axis=-1, keepdims=True), (1, nb)).astype(jnp.int32)


def moe_route_pallas(logits):
    n_tok = logits.shape[0]
    nb = -(-_moe_max_blocks(n_tok) // 128) * 128
    return pl.pallas_call(
        partial(_route_kernel, n_tok=n_tok),
        out_shape=[jax.ShapeDtypeStruct((TOP_K, n_tok), jnp.int32),
                   jax.ShapeDtypeStruct((n_tok, TOP_K), F32),
                   jax.ShapeDtypeStruct((8, nb), jnp.int32)],
        scratch_shapes=[pltpu.VMEM((n_tok, N_EXPERTS), F32)] * 2,
        name="moe_route",
    )(logits)


def _moe_block_kernel(be_ref, nu_ref, xb_ref, wgu_ref, bgu_ref, wdn_ref, bdn_ref, yb_ref):
    @pl.when(pl.program_id(0) < nu_ref[0])
    def _():
        dot = lambda a, w: lax.dot_general(a, w, _NN, precision=lax.Precision.DEFAULT, preferred_element_type=F32)
        gu = dot(xb_ref[...], wgu_ref[0, 0]) + bgu_ref[0, 0]
        glu = jnp.minimum(gu[:, :EXPERT_FF], SWIGLU_LIMIT)
        lin = jnp.clip(gu[:, EXPERT_FF:], -SWIGLU_LIMIT, SWIGLU_LIMIT)
        act = glu * _sigmoid(SWIGLU_ALPHA * glu) * (lin + 1.0)
        yb_ref[...] = dot(act, wdn_ref[0, 0]) + bdn_ref[0, 0]


def moe_blocks_pallas(xb, block_e, n_used, l, w_gu, b_gu, w_down, b_down):
    n_blocks = xb.shape[0] // MOE_SLOTS
    blk = lambda i, be, nu: (jnp.minimum(i, nu[0] - 1), 0)
    wsel = lambda i, be, nu: (l, be[jnp.minimum(i, nu[0] - 1)], 0, 0)
    grid_spec = pltpu.PrefetchScalarGridSpec(
        num_scalar_prefetch=2,
        grid=(n_blocks,),
        in_specs=[pl.BlockSpec((MOE_SLOTS, D_MODEL), blk),
                  pl.BlockSpec((1, 1, D_MODEL, 2 * EXPERT_FF), wsel),
                  pl.BlockSpec((1, 1, 1, 2 * EXPERT_FF), wsel),
                  pl.BlockSpec((1, 1, EXPERT_FF, D_MODEL), wsel),
                  pl.BlockSpec((1, 1, 1, D_MODEL), wsel)],
        out_specs=pl.BlockSpec((MOE_SLOTS, D_MODEL), blk),
    )
    return pl.pallas_call(
        _moe_block_kernel,
        grid_spec=grid_spec,
        out_shape=jax.ShapeDtypeStruct(xb.shape, F32),
        compiler_params=pltpu.CompilerParams(dimension_semantics=("arbitrary",), vmem_limit_bytes=48 * 1024 * 1024),
        name="moe_blocks",
    )(block_e, n_used, xb, w_gu, b_gu.reshape(DEPTH, N_EXPERTS, 1, 2 * EXPERT_FF), w_down,
      b_down.reshape(DEPTH, N_EXPERTS, 1, D_MODEL))


def _sc_mesh():
    return plsc.VectorSubcoreMesh(core_axis_name="c", subcore_axis_name="s")


def _sc_index_rows(idx):
    return jnp.pad(idx.reshape(-1, SC_ROWS), ((0, 0), (0, SC_LANES - SC_ROWS)))


def sc_dispatch(h, dest, n_rows):
    n_tok, d = h.shape
    idx = [_sc_index_rows(dest[k]) for k in range(TOP_K)]

    @pl.kernel(out_type=jax.ShapeDtypeStruct((n_rows, d), h.dtype), mesh=_sc_mesh(), scratch_types=[])
    def kern(h_hbm, i0, i1, i2, i3, o_hbm):
        def body(x_vmem, *i_vmem):
            for iv in i_vmem:
                pltpu.sync_copy(x_vmem, o_hbm.at[iv.at[0, pl.ds(0, SC_ROWS)]])

        pltpu.emit_pipeline(
            body,
            grid=(n_tok // SC_ROWS,),
            in_specs=[pl.BlockSpec((SC_ROWS, d), lambda i: (i, 0))]
            + [pl.BlockSpec((1, SC_LANES), lambda i: (i, 0))] * TOP_K,
            out_specs=[],
            core_axis_name=("c", "s"),
            dimension_semantics=(pltpu.PARALLEL,),
        )(h_hbm, i0, i1, i2, i3)

    return kern(h, *idx)


def sc_combine_gather(yb, dest):
    n_tok = dest.shape[1]
    d = yb.shape[1]
    idx = _sc_index_rows(dest.reshape(TOP_K * n_tok))

    @pl.kernel(out_type=jax.ShapeDtypeStruct((TOP_K * n_tok, d), yb.dtype), mesh=_sc_mesh(), scratch_types=[])
    def kern(y_hbm, i_hbm, o_hbm):
        def body(i_vmem, o_vmem):
            pltpu.sync_copy(y_hbm.at[i_vmem.at[0, pl.ds(0, SC_ROWS)]], o_vmem)

        pltpu.emit_pipeline(
            body,
            grid=(TOP_K * n_tok // SC_ROWS,),
            in_specs=[pl.BlockSpec((1, SC_LANES), lambda i: (i, 0))],
            out_specs=[pl.BlockSpec((SC_ROWS, d), lambda i: (i, 0))],
            core_axis_name=("c", "s"),
            dimension_semantics=(pltpu.PARALLEL,),
        )(i_hbm, o_hbm)

    return kern(yb, idx).reshape(TOP_K, n_tok, d)


def hgrn_lower_bounds(hg_lb):
    sm = jax.nn.softmax(hg_lb.astype(jnp.float32), axis=0)
    return jnp.cumsum(sm, axis=0) - sm[0:1]


def kernel(x_prompt, x_sample, cache_att_k, cache_att_v, state_rwkv, state_hgrn, c, c_ctx, w_mod, b_mod, norm_mix_g, norm_ffn_g, w_in, w_out, att_qnorm_g, att_knorm_g, rw_w0, rw_w2, rw_a0, rw_a2, rw_g2, rw_kk, rw_ka, rw_rk, rw_gn_g, rw_gn_b, hg_lb, hg_norm_g, router_w, router_b, moe_w_gu, moe_b_gu, moe_w_down, moe_b_down, final_norm_g):
    BP, TP, _ = x_prompt.shape
    BS, TS, _ = x_sample.shape
    n_p, n_s = BP * TP, BS * TS
    lb_all = hgrn_lower_bounds(hg_lb)
    cvec = jnp.concatenate([c_ctx[None, :], c, jnp.zeros((8 - 1 - BS, D_MODEL), F32)], axis=0)
    mod_all = adaln_mod_pallas(cvec, w_mod, b_mod).reshape(DEPTH, 8, 6, D_MODEL)
    zeros_state = jnp.zeros((BP, 2, RW_HEADS, HEAD_DIM, HEAD_DIM), F32)
    x = {'p': x_prompt.reshape(n_p, D_MODEL), 's': x_sample.reshape(n_s, D_MODEL)}
    dims = {'p': (TP, BP, n_p, 0), 's': (TS, BS, TS, n_p)}
    moe_out, mod_prev = None, None
    ks, vs, srs, shs = [], [], [], []
    for l in range(DEPTH):
        prm = dict(rw_w0=rw_w0[l], rw_w2=rw_w2[l], rw_a0=rw_a0[l], rw_a2=rw_a2[l], rw_g2=rw_g2[l],
                   rw_kk=rw_kk[l], rw_ka=rw_ka[l], rw_rk=rw_rk[l], rw_gn_g=rw_gn_g[l], rw_gn_b=rw_gn_b[l])
        mods = {'p': mod_all[l, 0:1], 's': mod_all[l, 1:1 + BS]}
        h2, logits = {}, {}
        for s in ('p', 's'):
            T, B, rpm, row0 = dims[s]
            if l == 0:
                p_att, p_rw, p_hg = in_proj_pallas(x[s], mods[s], norm_mix_g[l], w_in[l], rpm)
            else:
                x[s], p_att, p_rw, p_hg = in_proj_pallas(x[s], mods[s], norm_mix_g[l], w_in[l], rpm,
                                                         res=(moe_out[0], moe_out[1], row0, mod_prev[s]))
            p_att, p_rw, p_hg = (t.reshape(B, T, -1) for t in (p_att, p_rw, p_hg))
            if s == 'p':
                att, k_l, v_l = attention_pallas(p_att, att_qnorm_g[l], att_knorm_g[l])
                rw_out, sr_l = rwkv7_mixer_pallas(p_rw, zeros_state, prm)
                hg_out, sh_l = hgrn2_mixer_pallas(p_hg, zeros_state, lb_all[l], hg_norm_g[l])
                ks.append(k_l)
                vs.append(v_l)
                srs.append(sr_l)
                shs.append(sh_l)
            else:
                att = attention_pallas(p_att, att_qnorm_g[l], att_knorm_g[l], (cache_att_k[:, l], cache_att_v[:, l]))
                rw_out, _ = rwkv7_mixer_pallas(p_rw, state_rwkv[:, l], prm)
                hg_out, _ = hgrn2_mixer_pallas(p_hg, state_hgrn[:, l], lb_all[l], hg_norm_g[l])
            x[s], h2[s], logits[s] = out_proj_pallas(att.reshape(B * T, -1), rw_out.reshape(B * T, -1),
                                                     hg_out.reshape(B * T, -1), x[s], mods[s], norm_ffn_g[l],
                                                     w_out[l], router_w[l], router_b[l], rpm)
        h_all = jnp.concatenate([h2['p'], h2['s']], axis=0)
        dest, gate4, blk = moe_route_pallas(jnp.concatenate([logits['p'], logits['s']], axis=0))
        xb = sc_dispatch(h_all, dest, _moe_max_blocks(n_p + n_s) * MOE_SLOTS)
        yb = moe_blocks_pallas(xb, blk[0], blk[1, :1], l, moe_w_gu, moe_b_gu, moe_w_down, moe_b_down)
        moe_out = (sc_combine_gather(yb, dest), gate4)
        mod_prev = mods
    y_prompt = final_norm_pallas(x['p'], moe_out[0], moe_out[1], 0, mod_prev['p'], final_norm_g, n_p)
    y_sample = final_norm_pallas(x['s'], moe_out[0], moe_out[1], n_p, mod_prev['s'], final_norm_g, TS)
    return (y_prompt.reshape(x_prompt.shape), y_sample.reshape(x_sample.shape), jnp.stack(ks, axis=1),
            jnp.stack(vs, axis=1), jnp.stack(srs, axis=1), jnp.stack(shs, axis=1))
```

```python
import math
from functools import partial

import jax
import jax.numpy as jnp
from jax import lax
from jax.experimental import pallas as pl
from jax.experimental.pallas import tpu as pltpu
from jax.experimental.pallas import tpu_sc as plsc

D_MODEL = 1024
DEPTH = 2
GRID_W = 64
HEAD_DIM = 64
ATT_HEADS = 8
ATT_KV_HEADS = 2
ATT_WIDTH = ATT_HEADS * HEAD_DIM
KV_WIDTH = ATT_KV_HEADS * HEAD_DIM
RW_HEADS = 4
RW_WIDTH = RW_HEADS * HEAD_DIM
RW_GN_EPS = 64e-5
HG_HEADS = 4
HG_WIDTH = HG_HEADS * HEAD_DIM
HG_F_MIN = 1e-6
N_EXPERTS = 32
TOP_K = 4
EXPERT_FF = D_MODEL
SWIGLU_LIMIT = 7.0
SWIGLU_ALPHA = 1.702
ROPE_THETA = 10000.0
NORM_EPS = 1e-6

RW_CHUNK = 64
BF16 = jnp.bfloat16
F32 = jnp.float32

_NN = (((1,), (0,)), ((), ()))
_NT = (((1,), (1,)), ((), ()))
_TN = (((0,), (0,)), ((), ()))


def _split(x, n):
    parts = []
    for _ in range(n - 1):
        hi = x.astype(BF16)
        parts.append(hi)
        x = x - hi.astype(F32)
    parts.append(x.astype(BF16))
    return parts


def _mm(a, b, dims=_NN, passes=1):
    d = lambda x, y: lax.dot_general(x, y, dims, preferred_element_type=F32)
    if passes == 1:
        return d(a.astype(BF16), b.astype(BF16))
    ah, al = _split(a, 2)
    bh, bl = _split(b, 2)
    return d(ah, bl) + d(al, bh) + d(ah, bh)


def _mm_exact_lhs(a01, b, n=3):
    a = a01.astype(BF16)
    out = None
    for t in reversed(_split(b, n)):
        y = lax.dot_general(a, t, _NN, preferred_element_type=F32)
        out = y if out is None else out + y
    return out


def _mm_exact_rhs(a, b01, n=3):
    b = b01.astype(BF16)
    out = None
    for t in reversed(_split(a, n)):
        y = lax.dot_general(t, b, _NN, preferred_element_type=F32)
        out = y if out is None else out + y
    return out


def _head_blockdiag(width):
    r = lax.broadcasted_iota(jnp.int32, (width, width), 0) // HEAD_DIM
    c = lax.broadcasted_iota(jnp.int32, (width, width), 1) // HEAD_DIM
    return (r == c).astype(F32)


def _sigmoid(x):
    return 1.0 / (1.0 + jnp.exp(-x))


def _softplus(x):
    return jnp.maximum(x, 0.0) + jnp.log(1.0 + jnp.exp(-jnp.abs(x)))


def _rwkv_kernel(p_ref, s0_ref, w0_ref, w2_ref, a0_ref, a2_ref, g2_ref, kk_ref, ka_ref, rk_ref, gng_ref, gnb_ref,
                 out_ref, st_ref,
                 lw_scr, kd_scr, bb_scr, y_scr, kk_scr, s_scr, *, T, NB):
    C = RW_CHUNK
    n_chunks = T // C
    bd = _head_blockdiag(RW_WIDTH)
    seg = lambda t: _mm_exact_rhs(t, bd)
    ka = ka_ref[...]
    for nb in range(NB):
        k = p_ref[nb, :, 256:512]
        kk = k * kk_ref[...]
        kk = kk * lax.rsqrt(seg(kk * kk) + 1e-12)
        kk_scr[nb] = kk
        for d in range(2):
            wd = p_ref[nb, :, 768 + 64 * d:832 + 64 * d]
            ad = p_ref[nb, :, 896 + 64 * d:960 + 64 * d]
            w_raw = w0_ref[d:d + 1, :] + _mm(jnp.tanh(wd), w2_ref[d], passes=3)
            lw_scr[nb, d] = -jnp.exp(-_softplus(-w_raw) - 0.5)
            a = _sigmoid(a0_ref[d:d + 1, :] + _mm(ad, a2_ref[d], passes=3))
            kd_scr[nb, d] = k * (1.0 + (a - 1.0) * ka)
            bb_scr[nb, d] = kk * a
    s_scr[...] = s0_ref[...]

    ti = lax.broadcasted_iota(jnp.int32, (C, C), 0)
    si = lax.broadcasted_iota(jnp.int32, (C, C), 1)
    ones_cc = jnp.ones((C, C), F32)

    def chunk_body(i, carry):
        ch = []
        for nb, d in [(nb, d) for nb in range(NB) for d in range(2)]:
            ci = i if d == 0 else n_chunks - 1 - i
            rows = pl.ds(pl.multiple_of(ci * C, C), C)
            strict = (ti > si) if d == 0 else (ti < si)
            incl = (ti >= si) if d == 0 else (ti <= si)
            lw = lw_scr[nb, d, rows, :]
            cum = _mm_exact_lhs(incl.astype(F32), lw)
            total = _mm_exact_lhs(ones_cc, lw)
            cum_ex = cum - lw
            mid = 0.5 * total
            rr = p_ref[nb, rows, 0:256]
            vv = p_ref[nb, rows, 512:768]
            kdc = kd_scr[nb, d, rows, :]
            bbc = bb_scr[nb, d, rows, :]
            kkc = kk_scr[nb, rows, :]
            e_inv = jnp.exp(mid - cum)
            At = -kkc * jnp.exp(cum_ex - mid)
            Rt = rr * jnp.exp(cum - mid)
            Bt = bbc * e_inv
            Kt = kdc * e_inv
            Ap = -kkc * jnp.exp(cum_ex)
            Rp = rr * jnp.exp(cum)
            e_out = jnp.exp(total - cum)
            Bh = bbc * e_out
            Kh = kdc * e_out
            e_tot = jnp.exp(total[0:1, :])
            for h in range(RW_HEADS):
                hs = slice(h * HEAD_DIM, (h + 1) * HEAD_DIM)
                ch.append(dict(nb=nb, d=d, h=h, rows=rows, hs=hs, strict=strict, incl=incl,
                               AR=jnp.concatenate([At[:, hs], Rt[:, hs]], axis=0), Bt=Bt[:, hs], Kt=Kt[:, hs],
                               V=vv[:, hs], X1=Ap[:, hs], Rp=Rp[:, hs], Bh=Bh[:, hs], Kh=Kh[:, hs],
                               e_tot=e_tot[:, hs]))
        for c in ch:
            c['AB'] = _mm(c['AR'], c['Bt'], _NT)
            c['AK'] = _mm(c['AR'], c['Kt'], _NT)
        for c in ch:
            c['P'] = jnp.where(c['strict'], c['AB'][:C], 0.0)
            c['A_ak'] = jnp.where(c['strict'], c['AK'][:C], 0.0)
            c['A_rb'] = jnp.where(c['incl'], c['AB'][C:], 0.0)
            c['A_rk'] = jnp.where(c['incl'], c['AK'][C:], 0.0)
        for c in ch:
            c['X2'] = _mm(c['A_ak'], c['V'])
        for lvl in range(6):
            for c in ch:
                if lvl < 5:
                    c['PZ'] = _mm(c['P'], jnp.concatenate([c['P'], c['X1'], c['X2']], axis=1))
                else:
                    c['PZ'] = _mm(c['P'], jnp.concatenate([c['X1'], c['X2']], axis=1))
            for c in ch:
                PZ = c['PZ']
                if lvl < 5:
                    c['P'] = PZ[:, :C]
                    c['X1'] = c['X1'] + PZ[:, C:2 * C]
                    c['X2'] = c['X2'] + PZ[:, 2 * C:]
                else:
                    c['X1'] = c['X1'] + PZ[:, :C]
                    c['X2'] = c['X2'] + PZ[:, C:]
        for c in ch:
            c['S0'] = s_scr[c['nb'], c['d'], c['h']]
            c['UY'] = _mm(jnp.concatenate([c['X1'], c['Rp']], axis=0), c['S0'], _NT)
        for c in ch:
            c['U'] = c['UY'][:C] + c['X2']
        for c in ch:
            c['Y'] = c['UY'][C:] + _mm(c['A_rb'], c['U']) + _mm(c['A_rk'], c['V'])
            c['S1'] = c['S0'] * c['e_tot'] + _mm(c['U'], c['Bh'], _TN) + _mm(c['V'], c['Kh'], _TN)
        for c in ch:
            s_scr[c['nb'], c['d'], c['h']] = c['S1']
            y_scr[c['nb'], c['d'], c['rows'], c['hs']] = c['Y']
        return carry

    lax.fori_loop(0, n_chunks, chunk_body, 0)

    for nb in range(NB):
        r = p_ref[nb, :, 0:256]
        v = p_ref[nb, :, 512:768]
        bonus = seg(r * (kd_scr[nb, 0] + kd_scr[nb, 1]) * rk_ref[...]) * v
        g = _mm(_sigmoid(p_ref[nb, :, 1024:1152]), g2_ref[...], passes=3)
        y = y_scr[nb, 0] + y_scr[nb, 1]
        mu = seg(y) * (1.0 / HEAD_DIM)
        yc = y - mu
        var = seg(yc * yc) * (1.0 / HEAD_DIM)
        yn = yc * lax.rsqrt(var + RW_GN_EPS)
        out_ref[nb] = (yn * gng_ref[...] + gnb_ref[...] + bonus) * g
    st_ref[...] = s_scr[...]


RW_ROWS = 1024


def rwkv7_mixer_pallas(p_rw, s0, prm):
    B, T, W = p_rw.shape
    NB = max(2, RW_ROWS // T)
    row = lambda a: a.reshape(1, RW_WIDTH)
    full = lambda shape: pl.BlockSpec(shape, lambda b: (0,) * len(shape))
    st_spec = pl.BlockSpec((NB, 2, RW_HEADS, HEAD_DIM, HEAD_DIM), lambda b: (b, 0, 0, 0, 0))
    return pl.pallas_call(
        partial(_rwkv_kernel, T=T, NB=NB),
        grid=(B // NB,),
        in_specs=[pl.BlockSpec((NB, T, W), lambda b: (b, 0, 0)), st_spec,
                  full((2, RW_WIDTH)), full((2, 64, RW_WIDTH)), full((2, RW_WIDTH)), full((2, 64, RW_WIDTH)),
                  full((128, RW_WIDTH)), full((1, RW_WIDTH)), full((1, RW_WIDTH)), full((1, RW_WIDTH)),
                  full((1, RW_WIDTH)), full((1, RW_WIDTH))],
        out_specs=[pl.BlockSpec((NB, T, RW_WIDTH), lambda b: (b, 0, 0)), st_spec],
        out_shape=[jax.ShapeDtypeStruct((B, T, RW_WIDTH), F32),
                   jax.ShapeDtypeStruct((B, 2, RW_HEADS, HEAD_DIM, HEAD_DIM), F32)],
        scratch_shapes=[pltpu.VMEM((NB, 2, T, RW_WIDTH), F32)] * 4
        + [pltpu.VMEM((NB, T, RW_WIDTH), F32), pltpu.VMEM((NB, 2, RW_HEADS, HEAD_DIM, HEAD_DIM), F32)],
        compiler_params=pltpu.CompilerParams(dimension_semantics=("arbitrary",), vmem_limit_bytes=56 * 1024 * 1024),
        name="rwkv7_mixer",
    )(p_rw, s0, prm['rw_w0'], prm['rw_w2'], prm['rw_a0'], prm['rw_a2'], prm['rw_g2'], row(prm['rw_kk']),
      row(prm['rw_ka']), row(prm['rw_rk']), row(prm['rw_gn_g']), row(prm['rw_gn_b']))


HG_SUB = 16
HG_ROWS = 64


def _hgrn_kernel(p_ref, s0_ref, lb_ref, ng_ref, out_ref, st_ref, lf_scr, kf_scr, o_scr, s_scr, *, T):
    R, c = HG_ROWS, HG_SUB
    n_it = T // R
    x = p_ref[0]
    bd = _head_blockdiag(HG_WIDTH)
    seg = lambda t: _mm_exact_rhs(t, bd)
    for d in range(2):
        lbd = lb_ref[d:d + 1, :]
        f = lbd + (1.0 - lbd) * _sigmoid(x[:, 256 + 256 * d:512 + 256 * d])
        lf_scr[d] = jnp.log(jnp.maximum(f, HG_F_MIN))
        kf_scr[d] = 1.0 - f
        for h in range(HG_HEADS):
            s_scr[d, h] = s0_ref[0, d, h].T

    ti = lax.broadcasted_iota(jnp.int32, (R, R), 0)
    si = lax.broadcasted_iota(jnp.int32, (R, R), 1)
    same_blk = (ti // c) == (si // c)
    t16 = lax.broadcasted_iota(jnp.int32, (c, 1), 0)

    def body(i, carry):
        for d in range(2):
            ci = i if d == 0 else n_it - 1 - i
            rows = pl.ds(pl.multiple_of(ci * R, R), R)
            incl = (ti >= si) if d == 0 else (ti <= si)
            lf = lf_scr[d, rows, :]
            cum = _mm_exact_lhs((incl & same_blk).astype(F32), lf)
            tot = _mm_exact_lhs(same_blk.astype(F32), lf)
            xq = p_ref[0, rows, 0:256]
            q = xq * _sigmoid(xq)
            v = p_ref[0, rows, 768:1024]
            kf = kf_scr[d, rows, :]
            Qp = q * jnp.exp(cum)
            Kh = kf * jnp.exp(tot - cum)
            e_tot = jnp.exp(tot)
            blocks = range(R // c) if d == 0 else range(R // c - 1, -1, -1)
            o_parts = [None] * (R // c)
            for j in blocks:
                rs = slice(j * c, (j + 1) * c)
                cb, qb, kb, vb = cum[rs], q[rs], kf[rs], v[rs]
                prods = []
                for s in range(c):
                    e = jnp.exp(jnp.minimum(cb - cb[s:s + 1, :], 0.0))
                    prods.append(qb * (kb[s:s + 1, :] * e))
                att = _mm_exact_rhs(jnp.concatenate(prods, axis=0), bd, n=1)
                o_blk = jnp.zeros((c, HG_WIDTH), F32)
                for s in range(c):
                    keep = (t16 >= s) if d == 0 else (t16 <= s)
                    o_blk = o_blk + jnp.where(keep, att[s * c:(s + 1) * c], 0.0) * vb[s:s + 1, :]
                o_heads = []
                for h in range(HG_HEADS):
                    hs = slice(h * HEAD_DIM, (h + 1) * HEAD_DIM)
                    ST = s_scr[d, h]
                    o_heads.append(_mm(Qp[rs, hs], ST, _NT))
                    s_scr[d, h] = ST * e_tot[j * c:j * c + 1, hs] + _mm(vb[:, hs], Kh[rs, hs], _TN)
                o_parts[j] = o_blk + jnp.concatenate(o_heads, axis=1)
            o_scr[d, rows, :] = jnp.concatenate(o_parts, axis=0)
        return carry

    lax.fori_loop(0, n_it, body, 0)

    o = o_scr[0] + o_scr[1]
    o = o * lax.rsqrt(seg(o * o) * (1.0 / HEAD_DIM) + NORM_EPS) * ng_ref[...]
    gg = x[:, 1024:1280]
    out_ref[0] = o * (gg * _sigmoid(gg))
    for d in range(2):
        for h in range(HG_HEADS):
            st_ref[0, d, h] = s_scr[d, h].T


def hgrn2_mixer_pallas(p_hg, s0, lb, norm_g):
    B, T, W = p_hg.shape
    full = lambda shape: pl.BlockSpec(shape, lambda b: (0,) * len(shape))
    st_spec = pl.BlockSpec((1, 2, HG_HEADS, HEAD_DIM, HEAD_DIM), lambda b: (b, 0, 0, 0, 0))
    return pl.pallas_call(
        partial(_hgrn_kernel, T=T),
        grid=(B,),
        in_specs=[pl.BlockSpec((1, T, W), lambda b: (b, 0, 0)), st_spec, full((2, HG_WIDTH)), full((1, HG_WIDTH))],
        out_specs=[pl.BlockSpec((1, T, HG_WIDTH), lambda b: (b, 0, 0)), st_spec],
        out_shape=[jax.ShapeDtypeStruct((B, T, HG_WIDTH), F32),
                   jax.ShapeDtypeStruct((B, 2, HG_HEADS, HEAD_DIM, HEAD_DIM), F32)],
        scratch_shapes=[pltpu.VMEM((2, T, HG_WIDTH), F32)] * 3
        + [pltpu.VMEM((2, HG_HEADS, HEAD_DIM, HEAD_DIM), F32)],
        compiler_params=pltpu.CompilerParams(dimension_semantics=("arbitrary",), vmem_limit_bytes=48 * 1024 * 1024),
        name="hgrn2_mixer",
    )(p_hg, s0, lb, jnp.tile(norm_g.reshape(1, HEAD_DIM), (1, HG_HEADS)))


ATT_REP = ATT_HEADS // ATT_KV_HEADS
ATT_QROWS = 128


def _swap_pairs(x):
    w = x.shape[-1]
    lane = lax.broadcasted_iota(jnp.int32, x.shape, x.ndim - 1)
    return jnp.where(lane % 2 == 0, pltpu.roll(x, w - 1, x.ndim - 1), pltpu.roll(x, 1, x.ndim - 1))


def _att_kernel(*refs, T, past, rope):
    if rope:
        p_ref, qg_ref, kg_ref, cos_ref, sin_ref, ck_ref, cv_ref, out_ref, k_scr, v_scr, q_scr = refs
    else:
        p_ref, qg_ref, kg_ref, out_ref, kh_ref, vh_ref, k_scr, v_scr, q_scr = refs
    x = p_ref[0]
    q = x[:, 0:ATT_WIDTH]
    k = x[:, ATT_WIDTH:ATT_WIDTH + KV_WIDTH]
    v = x[:, ATT_WIDTH + KV_WIDTH:ATT_WIDTH + 2 * KV_WIDTH]
    inv_d = 1.0 / HEAD_DIM
    q = q * lax.rsqrt(_mm_exact_rhs(q * q, _head_blockdiag(ATT_WIDTH)) * inv_d + NORM_EPS) * qg_ref[...]
    k = k * lax.rsqrt(_mm_exact_rhs(k * k, _head_blockdiag(KV_WIDTH)) * inv_d + NORM_EPS) * kg_ref[...]
    if rope:
        cos, sin = cos_ref[...], sin_ref[...]
        rep = ATT_WIDTH // KV_WIDTH
        q = q * jnp.concatenate([cos] * rep, axis=1) + _swap_pairs(q) * jnp.concatenate([sin] * rep, axis=1)
        k = k * cos + _swap_pairs(k) * sin
    q_scr[...] = (q * (1.0 / math.sqrt(HEAD_DIM))).astype(BF16)
    for g in range(ATT_KV_HEADS):
        gs = slice(g * HEAD_DIM, (g + 1) * HEAD_DIM)
        if rope:
            k_scr[g, 0:past, :] = ck_ref[0, g].astype(BF16)
            v_scr[g, 0:past, :] = cv_ref[0, g].astype(BF16)
        else:
            kh_ref[0, g] = k[:, gs]
            vh_ref[0, g] = v[:, gs]
        k_scr[g, past:past + T, :] = k[:, gs].astype(BF16)
        v_scr[g, past:past + T, :] = v[:, gs].astype(BF16)
    QR = ATT_QROWS

    def q_block(qb, carry):
        rows = pl.ds(pl.multiple_of(qb * QR, QR), QR)
        qblk = q_scr[rows, :]
        for g in range(ATT_KV_HEADS):
            qs = jnp.concatenate([qblk[:, (g * ATT_REP + r) * HEAD_DIM:(g * ATT_REP + r + 1) * HEAD_DIM]
                                  for r in range(ATT_REP)], axis=0)
            s = lax.dot_general(qs, k_scr[g], _NT, preferred_element_type=F32)
            e = jnp.exp(s - jnp.max(s, axis=-1, keepdims=True))
            l = jnp.sum(e, axis=-1, keepdims=True)
            o = lax.dot_general(e.astype(BF16), v_scr[g], _NN, preferred_element_type=F32) / l
            for r in range(ATT_REP):
                h = g * ATT_REP + r
                out_ref[0, rows, h * HEAD_DIM:(h + 1) * HEAD_DIM] = o[r * QR:(r + 1) * QR]
        return carry

    lax.fori_loop(0, T // QR, q_block, 0)


def rope_tables(T):
    rows = T // GRID_W
    row = jnp.repeat(jnp.arange(rows, dtype=F32), GRID_W)
    col = jnp.tile(jnp.arange(GRID_W, dtype=F32), rows)
    n_freq = HEAD_DIM // 4
    inv = ROPE_THETA ** (-jnp.arange(n_freq, dtype=F32) / n_freq)
    ang = jnp.concatenate([row[:, None] * inv, col[:, None] * inv], axis=-1)
    cos = jnp.repeat(jnp.cos(ang), 2, axis=-1)
    sin = jnp.stack([-jnp.sin(ang), jnp.sin(ang)], axis=-1).reshape(T, HEAD_DIM)
    return jnp.tile(cos, (1, ATT_KV_HEADS)), jnp.tile(sin, (1, ATT_KV_HEADS))


def attention_pallas(p_att, qnorm_g, knorm_g, cache=None):
    B, T, W = p_att.shape
    rope = cache is not None
    past = cache[0].shape[2] if rope else 0
    full = lambda shape: pl.BlockSpec(shape, lambda b: (0,) * len(shape))
    qg = jnp.tile(qnorm_g.reshape(1, HEAD_DIM), (1, ATT_HEADS))
    kg = jnp.tile(knorm_g.reshape(1, HEAD_DIM), (1, ATT_KV_HEADS))
    in_specs = [pl.BlockSpec((1, T, W), lambda b: (b, 0, 0)), full((1, ATT_WIDTH)), full((1, KV_WIDTH))]
    args = [p_att, qg, kg]
    out_specs = [pl.BlockSpec((1, T, ATT_WIDTH), lambda b: (b, 0, 0))]
    out_shape = [jax.ShapeDtypeStruct((B, T, ATT_WIDTH), F32)]
    if rope:
        cos, sin = rope_tables(T)
        kv_spec = pl.BlockSpec((1, ATT_KV_HEADS, past, HEAD_DIM), lambda b: (b, 0, 0, 0))
        in_specs += [full((T, KV_WIDTH)), full((T, KV_WIDTH)), kv_spec, kv_spec]
        args += [cos, sin, cache[0], cache[1]]
    else:
        kv_spec = pl.BlockSpec((1, ATT_KV_HEADS, T, HEAD_DIM), lambda b: (b, 0, 0, 0))
        out_specs += [kv_spec, kv_spec]
        out_shape += [jax.ShapeDtypeStruct((B, ATT_KV_HEADS, T, HEAD_DIM), F32)] * 2
    res = pl.pallas_call(
        partial(_att_kernel, T=T, past=past, rope=rope),
        grid=(B,),
        in_specs=in_specs,
        out_specs=out_specs,
        out_shape=out_shape,
        scratch_shapes=[pltpu.VMEM((ATT_KV_HEADS, past + T, HEAD_DIM), BF16)] * 2
        + [pltpu.VMEM((T, ATT_WIDTH), BF16)],
        compiler_params=pltpu.CompilerParams(dimension_semantics=("arbitrary",), vmem_limit_bytes=48 * 1024 * 1024),
        name="attention_rope" if rope else "attention_ctx",
    )(*args)
    return res[0] if rope else tuple(res)


ROW_TILE = 256
MOD_TILE = 1536
ROUTE_TILE = 256
MOE_SLOTS = 256
SC_ROWS = 32
SC_LANES = 128
P_ATT, P_RW, P_HG = ATT_WIDTH + 2 * KV_WIDTH, 3 * RW_WIDTH + 384, 5 * HG_WIDTH


def _mod_kernel(c_ref, w_ref, b_ref, o_ref):
    c = c_ref[...]
    o_ref[0] = _mm(c * _sigmoid(c), w_ref[0], passes=3) + b_ref[0]


def adaln_mod_pallas(cvec, w_mod, b_mod):
    n = 6 * D_MODEL
    return pl.pallas_call(
        _mod_kernel,
        grid=(DEPTH, n // MOD_TILE),
        in_specs=[pl.BlockSpec((8, D_MODEL), lambda l, j: (0, 0)),
                  pl.BlockSpec((1, D_MODEL, MOD_TILE), lambda l, j: (l, 0, j)),
                  pl.BlockSpec((1, 1, MOD_TILE), lambda l, j: (l, 0, j))],
        out_specs=pl.BlockSpec((1, 8, MOD_TILE), lambda l, j: (l, 0, j)),
        out_shape=jax.ShapeDtypeStruct((DEPTH, 8, n), F32),
        compiler_params=pltpu.CompilerParams(dimension_semantics=("arbitrary", "arbitrary"),
                                             vmem_limit_bytes=48 * 1024 * 1024),
        name="adaln_mod",
    )(cvec, w_mod, b_mod.reshape(DEPTH, 1, n))


def _rms(x):
    return x * lax.rsqrt(jnp.mean(x * x, axis=-1, keepdims=True) + NORM_EPS)


def _moe_residual(x_ref, g_ref, g4_ref, pm_ref):
    y = g4_ref[:, 0:1] * g_ref[0]
    for k in range(1, TOP_K):
        y = y + g4_ref[:, k:k + 1] * g_ref[k]
    return x_ref[...] + pm_ref[0, 5:6, :] * y


def _in_kernel(*refs, has_res):
    if has_res:
        x_ref, gth_ref, g4_ref, pm_ref, m_ref, g_ref, w_ref, xo_ref, pa_ref, pr_ref, ph_ref = refs
        x = _moe_residual(x_ref, gth_ref, g4_ref, pm_ref)
        xo_ref[...] = x
    else:
        x_ref, m_ref, g_ref, w_ref, pa_ref, pr_ref, ph_ref = refs
        x = x_ref[...]
    h = _rms(x) * g_ref[...] * (1.0 + m_ref[0, 1:2, :]) + m_ref[0, 0:1, :]
    proj = lax.dot_general(h, w_ref[...], _NN, precision=lax.Precision.DEFAULT, preferred_element_type=F32)
    pa_ref[...] = proj[:, 0:P_ATT]
    pr_ref[...] = proj[:, P_ATT:P_ATT + P_RW]
    ph_ref[...] = proj[:, P_ATT + P_RW:]


def _res_specs(row0):
    t0 = row0 // ROW_TILE
    return [pl.BlockSpec((TOP_K, ROW_TILE, D_MODEL), lambda i: (0, i + t0, 0)),
            pl.BlockSpec((ROW_TILE, TOP_K), lambda i: (i + t0, 0))]


def in_proj_pallas(x, mod, norm_g, w_in, rows_per_mod, res=None):
    R = x.shape[0]
    tpm = rows_per_mod // ROW_TILE
    rt = lambda w: pl.BlockSpec((ROW_TILE, w), lambda i: (i, 0))
    ms = pl.BlockSpec((1, 6, D_MODEL), lambda i: (i // tpm, 0, 0))
    full = lambda shape, **kw: pl.BlockSpec(shape, lambda i: (0,) * len(shape), **kw)
    in_specs = [rt(D_MODEL)] + (_res_specs(res[2]) + [ms] if res else []) + [
        ms, full((1, D_MODEL)), full((D_MODEL, w_in.shape[1]), pipeline_mode=pl.Buffered(1))]
    args = [x] + ([res[0], res[1], res[3]] if res else []) + [mod, norm_g.reshape(1, D_MODEL), w_in]
    widths = ([D_MODEL] if res else []) + [P_ATT, P_RW, P_HG]
    return pl.pallas_call(
        partial(_in_kernel, has_res=res is not None),
        grid=(R // ROW_TILE,),
        in_specs=in_specs,
        out_specs=[rt(w) for w in widths],
        out_shape=[jax.ShapeDtypeStruct((R, w), F32) for w in widths],
        compiler_params=pltpu.CompilerParams(dimension_semantics=("arbitrary",), vmem_limit_bytes=48 * 1024 * 1024),
        name="in_proj",
    )(*args)


def _out_kernel(att_ref, rw_ref, hg_ref, x_ref, m_ref, g_ref, w_ref, rw_w_ref, rb_ref, xo_ref, h_ref, lg_ref):
    d = lambda a, lo, hi: lax.dot_general(a, w_ref[lo:hi, :], _NN, precision=lax.Precision.DEFAULT,
                                          preferred_element_type=F32)
    mixo = (d(att_ref[...], 0, ATT_WIDTH) + d(rw_ref[...], ATT_WIDTH, ATT_WIDTH + RW_WIDTH)
            + d(hg_ref[...], ATT_WIDTH + RW_WIDTH, ATT_WIDTH + RW_WIDTH + HG_WIDTH))
    x = x_ref[...] + m_ref[0, 2:3, :] * mixo
    xo_ref[...] = x
    h = _rms(x) * g_ref[...] * (1.0 + m_ref[0, 4:5, :]) + m_ref[0, 3:4, :]
    h_ref[...] = h
    lg_ref[...] = _mm(h, rw_w_ref[...], passes=3) + rb_ref[...]


def out_proj_pallas(att, rw, hg, x, mod, norm_g, w_out, router_w, router_b, rows_per_mod):
    R = x.shape[0]
    tpm = rows_per_mod // ROW_TILE
    rt = lambda w: pl.BlockSpec((ROW_TILE, w), lambda i: (i, 0))
    full = lambda shape: pl.BlockSpec(shape, lambda i: (0,) * len(shape))
    return pl.pallas_call(
        _out_kernel,
        grid=(R // ROW_TILE,),
        in_specs=[rt(ATT_WIDTH), rt(RW_WIDTH), rt(HG_WIDTH), rt(D_MODEL),
                  pl.BlockSpec((1, 6, D_MODEL), lambda i: (i // tpm, 0, 0)), full((1, D_MODEL)),
                  full((D_MODEL, D_MODEL)), full((D_MODEL, N_EXPERTS)), full((1, N_EXPERTS))],
        out_specs=[rt(D_MODEL), rt(D_MODEL), rt(N_EXPERTS)],
        out_shape=[jax.ShapeDtypeStruct((R, D_MODEL), F32), jax.ShapeDtypeStruct((R, D_MODEL), F32),
                   jax.ShapeDtypeStruct((R, N_EXPERTS), F32)],
        compiler_params=pltpu.CompilerParams(dimension_semantics=("arbitrary",), vmem_limit_bytes=48 * 1024 * 1024),
        name="out_proj",
    )(att, rw, hg, x, mod, norm_g.reshape(1, D_MODEL), w_out, router_w, router_b.reshape(1, N_EXPERTS))


def _final_kernel(x_ref, gth_ref, g4_ref, m_ref, g_ref, o_ref):
    o_ref[...] = _rms(_moe_residual(x_ref, gth_ref, g4_ref, m_ref)) * g_ref[...]


def final_norm_pallas(x, gathered, gate4, row0, mod, norm_g, rows_per_mod):
    R = x.shape[0]
    tpm = rows_per_mod // ROW_TILE
    rt = pl.BlockSpec((ROW_TILE, D_MODEL), lambda i: (i, 0))
    return pl.pallas_call(
        _final_kernel,
        grid=(R // ROW_TILE,),
        in_specs=[rt] + _res_specs(row0) + [pl.BlockSpec((1, 6, D_MODEL), lambda i: (i // tpm, 0, 0)),
                                            pl.BlockSpec((1, D_MODEL), lambda i: (0, 0))],
        out_specs=rt,
        out_shape=jax.ShapeDtypeStruct((R, D_MODEL), F32),
        name="final_norm",
    )(x, gathered, gate4, mod, norm_g.reshape(1, D_MODEL))


def _moe_max_blocks(n_tok):
    return (n_tok * TOP_K + N_EXPERTS * (MOE_SLOTS - 1)) // MOE_SLOTS


def _exact_nt_ones(a):
    ones = jnp.ones((8, a.shape[1]), BF16)
    out = None
    for t in reversed(_split(a, 3)):
        y = lax.dot_general(ones, t, _NT, preferred_element_type=F32)
        out = y if out is None else out + y
    return out


def _route_kernel(lg_ref, dest_ref, gate4_ref, blk_ref, rank_scr, gate_scr, *, n_tok):
    Rt, E = ROUTE_TILE, N_EXPERTS
    n_tiles = n_tok // Rt
    lane = lax.broadcasted_iota(jnp.int32, (Rt, E), 1)
    ti = lax.broadcasted_iota(jnp.int32, (Rt, Rt), 0)
    si = lax.broadcasted_iota(jnp.int32, (Rt, Rt), 1)
    lower = (ti > si).astype(BF16)
    ei = lax.broadcasted_iota(jnp.int32, (E, E), 0)
    ej = lax.broadcasted_iota(jnp.int32, (E, E), 1)
    upper_e = (ei < ej).astype(BF16)

    def tile_members(it, off):
        rows = pl.ds(pl.multiple_of(it * Rt, Rt), Rt)
        lg = lg_ref[rows, :]
        work = lg
        member = jnp.zeros((Rt, E), jnp.bool_)
        top = None
        for k in range(TOP_K):
            m = jnp.max(work, axis=-1, keepdims=True)
            if top is None:
                top = m
            first = jnp.min(jnp.where(work == m, lane, E), axis=-1, keepdims=True)
            pick = lane == first
            member = member | pick
            work = jnp.where(pick, -jnp.inf, work)
        ex = jnp.where(member, jnp.exp(lg - top), 0.0)
        gate_scr[rows, :] = ex / jnp.sum(ex, axis=-1, keepdims=True)
        mem = member.astype(BF16)
        rank = lax.dot_general(lower, mem, _NN, preferred_element_type=F32) + off
        rank_scr[rows, :] = jnp.where(member, rank, -1.0)
        return off + jnp.sum(mem.astype(F32), axis=0, keepdims=True)

    count = lax.fori_loop(0, n_tiles, tile_members, jnp.zeros((1, E), F32))
    nblk = jnp.floor((count + (MOE_SLOTS - 1)) * (1.0 / MOE_SLOTS))
    bstart = lax.dot_general(nblk.astype(BF16), upper_e, _NN, preferred_element_type=F32)
    bend = bstart + nblk
    pstart = bstart * MOE_SLOTS

    def tile_slots(it, carry):
        rows = pl.ds(pl.multiple_of(it * Rt, Rt), Rt)
        rank = rank_scr[rows, :]
        gate = gate_scr[rows, :]
        member = rank >= 0.0
        kidx = lax.dot_general(member.astype(BF16), upper_e, _NN, preferred_element_type=F32)
        slot = pstart + rank
        cols = []
        for k in range(TOP_K):
            sel = member & (kidx == k)
            dest_ref[k:k + 1, rows] = _exact_nt_ones(jnp.where(sel, slot, 0.0))[0:1].astype(jnp.int32)
            cols.append(jnp.sum(jnp.where(sel, gate, 0.0), axis=-1, keepdims=True))
        gate4_ref[rows, :] = jnp.concatenate(cols, axis=1)
        return carry

    lax.fori_loop(0, n_tiles, tile_slots, 0)
    nb = blk_ref.shape[1]
    bi = lax.broadcasted_iota(jnp.int32, (E, nb), 1).astype(F32)
    bend_col = jnp.sum(jnp.where(ei == ej, jnp.broadcast_to(bend, (E, E)), 0.0), axis=1, keepdims=True)
    owner = jnp.sum((bend_col <= bi).astype(F32), axis=0, keepdims=True)
    blk_ref[0:1, :] = jnp.minimum(owner, E - 1.0).astype(jnp.int32)
    blk_ref[1:2, :] = jnp.broadcast_to(jnp.sum(nblk, axis=-1, keepdims=True), (1, nb)).astype(jnp.int32)
    blk_ref[2:8, :] = jnp.zeros((6, nb), jnp.int32)


def moe_route_pallas(logits):
    n_tok = logits.shape[0]
    nb = -(-_moe_max_blocks(n_tok) // 128) * 128
    return pl.pallas_call(
        partial(_route_kernel, n_tok=n_tok),
        out_shape=[jax.ShapeDtypeStruct((TOP_K, n_tok), jnp.int32),
                   jax.ShapeDtypeStruct((n_tok, TOP_K), F32),
                   jax.ShapeDtypeStruct((8, nb), jnp.int32)],
        scratch_shapes=[pltpu.VMEM((n_tok, N_EXPERTS), F32)] * 2,
        name="moe_route",
    )(logits)


def _moe_block_kernel(be_ref, nu_ref, first_ref, par_ref, nxt_ref, xb_ref, wgu_hbm, bgu_ref, wdn_hbm, bdn_ref, yb_ref,
                      wgu_buf, wdn_buf, sem, *, l):
    i = pl.program_id(0)

    def weight_copies(e, slot):
        return (pltpu.make_async_copy(wgu_hbm.at[l, e], wgu_buf.at[slot], sem.at[0, slot]),
                pltpu.make_async_copy(wdn_hbm.at[l, e], wdn_buf.at[slot], sem.at[1, slot]))

    @pl.when(i < nu_ref[0])
    def _():
        slot = par_ref[i]

        @pl.when(first_ref[i] == 1)
        def _():
            @pl.when(i == 0)
            def _():
                for cp in weight_copies(be_ref[0], slot):
                    cp.start()

            for cp in weight_copies(be_ref[i], slot):
                cp.wait()

            @pl.when(nxt_ref[i] >= 0)
            def _():
                for cp in weight_copies(nxt_ref[i], 1 - slot):
                    cp.start()

        dot = lambda a, w: lax.dot_general(a, w, _NN, precision=lax.Precision.DEFAULT, preferred_element_type=F32)
        gu = dot(xb_ref[...], wgu_buf[slot]) + bgu_ref[0, 0]
        glu = jnp.minimum(gu[:, :EXPERT_FF], SWIGLU_LIMIT)
        lin = jnp.clip(gu[:, EXPERT_FF:], -SWIGLU_LIMIT, SWIGLU_LIMIT)
        act = glu * _sigmoid(SWIGLU_ALPHA * glu) * (lin + 1.0)
        yb_ref[...] = dot(act, wdn_buf[slot]) + bdn_ref[0, 0]


def _expert_runs(block_e, n_used):
    n = block_e.shape[0]
    idx = jnp.arange(n, dtype=jnp.int32)
    valid = idx < n_used[0]
    first = valid & ((idx == 0) | (block_e != jnp.roll(block_e, 1)))
    par = (jnp.cumsum(first.astype(jnp.int32)) - 1) % 2
    start = jnp.where(first, idx, n)
    nxt_start = lax.cummin(jnp.concatenate([start[1:], jnp.full((1,), n, jnp.int32)]), reverse=True)
    nxt = jnp.where(nxt_start < n, block_e[jnp.minimum(nxt_start, n - 1)], -1)
    return first.astype(jnp.int32), par.astype(jnp.int32), nxt.astype(jnp.int32)


def moe_blocks_pallas(xb, block_e, n_used, l, w_gu, b_gu, w_down, b_down):
    n_blocks = xb.shape[0] // MOE_SLOTS
    first, par, nxt = _expert_runs(block_e, n_used)
    blk = lambda i, be, nu, *_: (jnp.minimum(i, nu[0] - 1), 0)
    bsel = lambda i, be, nu, *_: (l, be[jnp.minimum(i, nu[0] - 1)], 0, 0)
    grid_spec = pltpu.PrefetchScalarGridSpec(
        num_scalar_prefetch=5,
        grid=(n_blocks,),
        in_specs=[pl.BlockSpec((MOE_SLOTS, D_MODEL), blk),
                  pl.BlockSpec(memory_space=pl.ANY),
                  pl.BlockSpec((1, 1, 1, 2 * EXPERT_FF), bsel),
                  pl.BlockSpec(memory_space=pl.ANY),
                  pl.BlockSpec((1, 1, 1, D_MODEL), bsel)],
        out_specs=pl.BlockSpec((MOE_SLOTS, D_MODEL), blk),
        scratch_shapes=[pltpu.VMEM((2, D_MODEL, 2 * EXPERT_FF), F32), pltpu.VMEM((2, EXPERT_FF, D_MODEL), F32),
                        pltpu.SemaphoreType.DMA((2, 2))],
    )
    return pl.pallas_call(
        partial(_moe_block_kernel, l=l),
        grid_spec=grid_spec,
        out_shape=jax.ShapeDtypeStruct(xb.shape, F32),
        compiler_params=pltpu.CompilerParams(dimension_semantics=("arbitrary",), vmem_limit_bytes=48 * 1024 * 1024),
        name="moe_blocks",
    )(block_e, n_used, first, par, nxt, xb, w_gu, b_gu.reshape(DEPTH, N_EXPERTS, 1, 2 * EXPERT_FF), w_down,
      b_down.reshape(DEPTH, N_EXPERTS, 1, D_MODEL))


def _sc_mesh():
    return plsc.VectorSubcoreMesh(core_axis_name="c", subcore_axis_name="s")


def _sc_index_rows(idx):
    return jnp.pad(idx.reshape(-1, SC_ROWS), ((0, 0), (0, SC_LANES - SC_ROWS)))


def sc_dispatch(h, dest, n_rows):
    n_tok, d = h.shape
    idx = [_sc_index_rows(dest[k]) for k in range(TOP_K)]

    @pl.kernel(out_type=jax.ShapeDtypeStruct((n_rows, d), h.dtype), mesh=_sc_mesh(), scratch_types=[])
    def kern(h_hbm, i0, i1, i2, i3, o_hbm):
        def body(x_vmem, *i_vmem):
            for iv in i_vmem:
                pltpu.sync_copy(x_vmem, o_hbm.at[iv.at[0, pl.ds(0, SC_ROWS)]])

        pltpu.emit_pipeline(
            body,
            grid=(n_tok // SC_ROWS,),
            in_specs=[pl.BlockSpec((SC_ROWS, d), lambda i: (i, 0))]
            + [pl.BlockSpec((1, SC_LANES), lambda i: (i, 0))] * TOP_K,
            out_specs=[],
            core_axis_name=("c", "s"),
            dimension_semantics=(pltpu.PARALLEL,),
        )(h_hbm, i0, i1, i2, i3)

    return kern(h, *idx)


def sc_combine_gather(yb, dest):
    n_tok = dest.shape[1]
    d = yb.shape[1]
    idx = _sc_index_rows(dest.reshape(TOP_K * n_tok))

    @pl.kernel(out_type=jax.ShapeDtypeStruct((TOP_K * n_tok, d), yb.dtype), mesh=_sc_mesh(), scratch_types=[])
    def kern(y_hbm, i_hbm, o_hbm):
        def body(i_vmem, o_vmem):
            pltpu.sync_copy(y_hbm.at[i_vmem.at[0, pl.ds(0, SC_ROWS)]], o_vmem)

        pltpu.emit_pipeline(
            body,
            grid=(TOP_K * n_tok // SC_ROWS,),
            in_specs=[pl.BlockSpec((1, SC_LANES), lambda i: (i, 0))],
            out_specs=[pl.BlockSpec((SC_ROWS, d), lambda i: (i, 0))],
            core_axis_name=("c", "s"),
            dimension_semantics=(pltpu.PARALLEL,),
        )(i_hbm, o_hbm)

    return kern(yb, idx).reshape(TOP_K, n_tok, d)


def hgrn_lower_bounds(hg_lb):
    sm = jax.nn.softmax(hg_lb.astype(jnp.float32), axis=0)
    return jnp.cumsum(sm, axis=0) - sm[0:1]


def kernel(x_prompt, x_sample, cache_att_k, cache_att_v, state_rwkv, state_hgrn, c, c_ctx, w_mod, b_mod, norm_mix_g, norm_ffn_g, w_in, w_out, att_qnorm_g, att_knorm_g, rw_w0, rw_w2, rw_a0, rw_a2, rw_g2, rw_kk, rw_ka, rw_rk, rw_gn_g, rw_gn_b, hg_lb, hg_norm_g, router_w, router_b, moe_w_gu, moe_b_gu, moe_w_down, moe_b_down, final_norm_g):
    BP, TP, _ = x_prompt.shape
    BS, TS, _ = x_sample.shape
    n_p, n_s = BP * TP, BS * TS
    lb_all = hgrn_lower_bounds(hg_lb)
    cvec = jnp.concatenate([c_ctx[None, :], c, jnp.zeros((8 - 1 - BS, D_MODEL), F32)], axis=0)
    mod_all = adaln_mod_pallas(cvec, w_mod, b_mod).reshape(DEPTH, 8, 6, D_MODEL)
    zeros_state = jnp.zeros((BP, 2, RW_HEADS, HEAD_DIM, HEAD_DIM), F32)
    x = {'p': x_prompt.reshape(n_p, D_MODEL), 's': x_sample.reshape(n_s, D_MODEL)}
    dims = {'p': (TP, BP, n_p, 0), 's': (TS, BS, TS, n_p)}
    moe_out, mod_prev = None, None
    ks, vs, srs, shs = [], [], [], []
    for l in range(DEPTH):
        prm = dict(rw_w0=rw_w0[l], rw_w2=rw_w2[l], rw_a0=rw_a0[l], rw_a2=rw_a2[l], rw_g2=rw_g2[l],
                   rw_kk=rw_kk[l], rw_ka=rw_ka[l], rw_rk=rw_rk[l], rw_gn_g=rw_gn_g[l], rw_gn_b=rw_gn_b[l])
        mods = {'p': mod_all[l, 0:1], 's': mod_all[l, 1:1 + BS]}
        h2, logits = {}, {}
        for s in ('p', 's'):
            T, B, rpm, row0 = dims[s]
            if l == 0:
                p_att, p_rw, p_hg = in_proj_pallas(x[s], mods[s], norm_mix_g[l], w_in[l], rpm)
            else:
                x[s], p_att, p_rw, p_hg = in_proj_pallas(x[s], mods[s], norm_mix_g[l], w_in[l], rpm,
                                                         res=(moe_out[0], moe_out[1], row0, mod_prev[s]))
            p_att, p_rw, p_hg = (t.reshape(B, T, -1) for t in (p_att, p_rw, p_hg))
            if s == 'p':
                att, k_l, v_l = attention_pallas(p_att, att_qnorm_g[l], att_knorm_g[l])
                rw_out, sr_l = rwkv7_mixer_pallas(p_rw, zeros_state, prm)
                hg_out, sh_l = hgrn2_mixer_pallas(p_hg, zeros_state, lb_all[l], hg_norm_g[l])
                ks.append(k_l)
                vs.append(v_l)
                srs.append(sr_l)
                shs.append(sh_l)
            else:
                att = attention_pallas(p_att, att_qnorm_g[l], att_knorm_g[l], (cache_att_k[:, l], cache_att_v[:, l]))
                rw_out, _ = rwkv7_mixer_pallas(p_rw, state_rwkv[:, l], prm)
                hg_out, _ = hgrn2_mixer_pallas(p_hg, state_hgrn[:, l], lb_all[l], hg_norm_g[l])
            x[s], h2[s], logits[s] = out_proj_pallas(att.reshape(B * T, -1), rw_out.reshape(B * T, -1),
                                                     hg_out.reshape(B * T, -1), x[s], mods[s], norm_ffn_g[l],
                                                     w_out[l], router_w[l], router_b[l], rpm)
        h_all = jnp.concatenate([h2['p'], h2['s']], axis=0)
        dest, gate4, blk = moe_route_pallas(jnp.concatenate([logits['p'], logits['s']], axis=0))
        xb = sc_dispatch(h_all, dest, _moe_max_blocks(n_p + n_s) * MOE_SLOTS)
        yb = moe_blocks_pallas(xb, blk[0], blk[1, :1], l, moe_w_gu, moe_b_gu, moe_w_down, moe_b_down)
        moe_out = (sc_combine_gather(yb, dest), gate4)
        mod_prev = mods
    y_prompt = final_norm_pallas(x['p'], moe_out[0], moe_out[1], 0, mod_prev['p'], final_norm_g, n_p)
    y_sample = final_norm_pallas(x['s'], moe_out[0], moe_out[1], n_p, mod_prev['s'], final_norm_g, TS)
    return (y_prompt.reshape(x_prompt.shape), y_sample.reshape(x_sample.shape), jnp.stack(ks, axis=1),
            jnp.stack(vs, axis=1), jnp.stack(srs, axis=1), jnp.stack(shs, axis=1))
```

```python
import math
from functools import partial

import jax
import jax.numpy as jnp
from jax import lax
from jax.experimental import pallas as pl
from jax.experimental.pallas import tpu as pltpu
from jax.experimental.pallas import tpu_sc as plsc

D_MODEL = 1024
DEPTH = 2
GRID_W = 64
HEAD_DIM = 64
ATT_HEADS = 8
ATT_KV_HEADS = 2
ATT_WIDTH = ATT_HEADS * HEAD_DIM
KV_WIDTH = ATT_KV_HEADS * HEAD_DIM
RW_HEADS = 4
RW_WIDTH = RW_HEADS * HEAD_DIM
RW_GN_EPS = 64e-5
HG_HEADS = 4
HG_WIDTH = HG_HEADS * HEAD_DIM
HG_F_MIN = 1e-6
N_EXPERTS = 32
TOP_K = 4
EXPERT_FF = D_MODEL
SWIGLU_LIMIT = 7.0
SWIGLU_ALPHA = 1.702
ROPE_THETA = 10000.0
NORM_EPS = 1e-6

RW_CHUNK = 64
BF16 = jnp.bfloat16
F32 = jnp.float32

_NN = (((1,), (0,)), ((), ()))
_NT = (((1,), (1,)), ((), ()))
_TN = (((0,), (0,)), ((), ()))


def _split(x, n):
    parts = []
    for _ in range(n - 1):
        hi = x.astype(BF16)
        parts.append(hi)
        x = x - hi.astype(F32)
    parts.append(x.astype(BF16))
    return parts


def _mm(a, b, dims=_NN, passes=1):
    d = lambda x, y: lax.dot_general(x, y, dims, preferred_element_type=F32)
    if passes == 1:
        return d(a.astype(BF16), b.astype(BF16))
    ah, al = _split(a, 2)
    bh, bl = _split(b, 2)
    return d(ah, bl) + d(al, bh) + d(ah, bh)


def _mm_exact_lhs(a01, b, n=3):
    a = a01.astype(BF16)
    out = None
    for t in reversed(_split(b, n)):
        y = lax.dot_general(a, t, _NN, preferred_element_type=F32)
        out = y if out is None else out + y
    return out


def _mm_exact_rhs(a, b01, n=3):
    b = b01.astype(BF16)
    out = None
    for t in reversed(_split(a, n)):
        y = lax.dot_general(t, b, _NN, preferred_element_type=F32)
        out = y if out is None else out + y
    return out


def _head_blockdiag(width):
    r = lax.broadcasted_iota(jnp.int32, (width, width), 0) // HEAD_DIM
    c = lax.broadcasted_iota(jnp.int32, (width, width), 1) // HEAD_DIM
    return (r == c).astype(F32)


def _sigmoid(x):
    return 1.0 / (1.0 + jnp.exp(-x))


def _softplus(x):
    return jnp.maximum(x, 0.0) + jnp.log(1.0 + jnp.exp(-jnp.abs(x)))


def _rwkv_kernel(p_ref, s0_ref, w0_ref, w2_ref, a0_ref, a2_ref, g2_ref, kk_ref, ka_ref, rk_ref, gng_ref, gnb_ref,
                 out_ref, st_ref,
                 lw_scr, kd_scr, bb_scr, y_scr, kk_scr, s_scr, *, T, NB):
    C = RW_CHUNK
    n_chunks = T // C
    bd = _head_blockdiag(RW_WIDTH)
    seg = lambda t: _mm_exact_rhs(t, bd)
    ka = ka_ref[...]
    for nb in range(NB):
        k = p_ref[nb, :, 256:512]
        kk = k * kk_ref[...]
        kk = kk * lax.rsqrt(seg(kk * kk) + 1e-12)
        kk_scr[nb] = kk
        for d in range(2):
            wd = p_ref[nb, :, 768 + 64 * d:832 + 64 * d]
            ad = p_ref[nb, :, 896 + 64 * d:960 + 64 * d]
            w_raw = w0_ref[d:d + 1, :] + _mm(jnp.tanh(wd), w2_ref[d], passes=3)
            lw_scr[nb, d] = -jnp.exp(-_softplus(-w_raw) - 0.5)
            a = _sigmoid(a0_ref[d:d + 1, :] + _mm(ad, a2_ref[d], passes=3))
            kd_scr[nb, d] = k * (1.0 + (a - 1.0) * ka)
            bb_scr[nb, d] = kk * a
    s_scr[...] = s0_ref[...]

    ti = lax.broadcasted_iota(jnp.int32, (C, C), 0)
    si = lax.broadcasted_iota(jnp.int32, (C, C), 1)
    ones_cc = jnp.ones((C, C), F32)

    def chunk_body(i, carry):
        ch = []
        for nb, d in [(nb, d) for nb in range(NB) for d in range(2)]:
            ci = i if d == 0 else n_chunks - 1 - i
            rows = pl.ds(pl.multiple_of(ci * C, C), C)
            strict = (ti > si) if d == 0 else (ti < si)
            incl = (ti >= si) if d == 0 else (ti <= si)
            lw = lw_scr[nb, d, rows, :]
            cum = _mm_exact_lhs(incl.astype(F32), lw)
            total = _mm_exact_lhs(ones_cc, lw)
            cum_ex = cum - lw
            mid = 0.5 * total
            rr = p_ref[nb, rows, 0:256]
            vv = p_ref[nb, rows, 512:768]
            kdc = kd_scr[nb, d, rows, :]
            bbc = bb_scr[nb, d, rows, :]
            kkc = kk_scr[nb, rows, :]
            e_inv = jnp.exp(mid - cum)
            At = -kkc * jnp.exp(cum_ex - mid)
            Rt = rr * jnp.exp(cum - mid)
            Bt = bbc * e_inv
            Kt = kdc * e_inv
            Ap = -kkc * jnp.exp(cum_ex)
            Rp = rr * jnp.exp(cum)
            e_out = jnp.exp(total - cum)
            Bh = bbc * e_out
            Kh = kdc * e_out
            e_tot = jnp.exp(total[0:1, :])
            for h in range(RW_HEADS):
                hs = slice(h * HEAD_DIM, (h + 1) * HEAD_DIM)
                ch.append(dict(nb=nb, d=d, h=h, rows=rows, hs=hs, strict=strict, incl=incl,
                               AR=jnp.concatenate([At[:, hs], Rt[:, hs]], axis=0), Bt=Bt[:, hs], Kt=Kt[:, hs],
                               V=vv[:, hs], X1=Ap[:, hs], Rp=Rp[:, hs], Bh=Bh[:, hs], Kh=Kh[:, hs],
                               e_tot=e_tot[:, hs]))
        for c in ch:
            c['AB'] = _mm(c['AR'], c['Bt'], _NT)
            c['AK'] = _mm(c['AR'], c['Kt'], _NT)
        for c in ch:
            c['P'] = jnp.where(c['strict'], c['AB'][:C], 0.0)
            c['A_ak'] = jnp.where(c['strict'], c['AK'][:C], 0.0)
            c['A_rb'] = jnp.where(c['incl'], c['AB'][C:], 0.0)
            c['A_rk'] = jnp.where(c['incl'], c['AK'][C:], 0.0)
        for c in ch:
            c['X2'] = _mm(c['A_ak'], c['V'])
        for lvl in range(6):
            for c in ch:
                if lvl < 5:
                    c['PZ'] = _mm(c['P'], jnp.concatenate([c['P'], c['X1'], c['X2']], axis=1))
                else:
                    c['PZ'] = _mm(c['P'], jnp.concatenate([c['X1'], c['X2']], axis=1))
            for c in ch:
                PZ = c['PZ']
                if lvl < 5:
                    c['P'] = PZ[:, :C]
                    c['X1'] = c['X1'] + PZ[:, C:2 * C]
                    c['X2'] = c['X2'] + PZ[:, 2 * C:]
                else:
                    c['X1'] = c['X1'] + PZ[:, :C]
                    c['X2'] = c['X2'] + PZ[:, C:]
        for c in ch:
            c['S0'] = s_scr[c['nb'], c['d'], c['h']]
            c['UY'] = _mm(jnp.concatenate([c['X1'], c['Rp']], axis=0), c['S0'], _NT)
        for c in ch:
            c['U'] = c['UY'][:C] + c['X2']
        for c in ch:
            c['Y'] = c['UY'][C:] + _mm(c['A_rb'], c['U']) + _mm(c['A_rk'], c['V'])
            c['S1'] = c['S0'] * c['e_tot'] + _mm(c['U'], c['Bh'], _TN) + _mm(c['V'], c['Kh'], _TN)
        for c in ch:
            s_scr[c['nb'], c['d'], c['h']] = c['S1']
            y_scr[c['nb'], c['d'], c['rows'], c['hs']] = c['Y']
        return carry

    lax.fori_loop(0, n_chunks, chunk_body, 0)

    for nb in range(NB):
        r = p_ref[nb, :, 0:256]
        v = p_ref[nb, :, 512:768]
        bonus = seg(r * (kd_scr[nb, 0] + kd_scr[nb, 1]) * rk_ref[...]) * v
        g = _mm(_sigmoid(p_ref[nb, :, 1024:1152]), g2_ref[...], passes=3)
        y = y_scr[nb, 0] + y_scr[nb, 1]
        mu = seg(y) * (1.0 / HEAD_DIM)
        yc = y - mu
        var = seg(yc * yc) * (1.0 / HEAD_DIM)
        yn = yc * lax.rsqrt(var + RW_GN_EPS)
        out_ref[nb] = (yn * gng_ref[...] + gnb_ref[...] + bonus) * g
    st_ref[...] = s_scr[...]


RW_ROWS = 1024


def rwkv7_mixer_pallas(p_rw, s0, prm):
    B, T, W = p_rw.shape
    NB = max(2, RW_ROWS // T)
    row = lambda a: a.reshape(1, RW_WIDTH)
    full = lambda shape: pl.BlockSpec(shape, lambda b: (0,) * len(shape))
    st_spec = pl.BlockSpec((NB, 2, RW_HEADS, HEAD_DIM, HEAD_DIM), lambda b: (b, 0, 0, 0, 0))
    return pl.pallas_call(
        partial(_rwkv_kernel, T=T, NB=NB),
        grid=(B // NB,),
        in_specs=[pl.BlockSpec((NB, T, W), lambda b: (b, 0, 0)), st_spec,
                  full((2, RW_WIDTH)), full((2, 64, RW_WIDTH)), full((2, RW_WIDTH)), full((2, 64, RW_WIDTH)),
                  full((128, RW_WIDTH)), full((1, RW_WIDTH)), full((1, RW_WIDTH)), full((1, RW_WIDTH)),
                  full((1, RW_WIDTH)), full((1, RW_WIDTH))],
        out_specs=[pl.BlockSpec((NB, T, RW_WIDTH), lambda b: (b, 0, 0)), st_spec],
        out_shape=[jax.ShapeDtypeStruct((B, T, RW_WIDTH), F32),
                   jax.ShapeDtypeStruct((B, 2, RW_HEADS, HEAD_DIM, HEAD_DIM), F32)],
        scratch_shapes=[pltpu.VMEM((NB, 2, T, RW_WIDTH), F32)] * 4
        + [pltpu.VMEM((NB, T, RW_WIDTH), F32), pltpu.VMEM((NB, 2, RW_HEADS, HEAD_DIM, HEAD_DIM), F32)],
        compiler_params=pltpu.CompilerParams(dimension_semantics=("arbitrary",), vmem_limit_bytes=56 * 1024 * 1024),
        name="rwkv7_mixer",
    )(p_rw, s0, prm['rw_w0'], prm['rw_w2'], prm['rw_a0'], prm['rw_a2'], prm['rw_g2'], row(prm['rw_kk']),
      row(prm['rw_ka']), row(prm['rw_rk']), row(prm['rw_gn_g']), row(prm['rw_gn_b']))


HG_SUB = 16
HG_ROWS = 64


def _hgrn_kernel(p_ref, s0_ref, lb_ref, ng_ref, out_ref, st_ref, lf_scr, kf_scr, o_scr, s_scr, *, T):
    R, c = HG_ROWS, HG_SUB
    n_it = T // R
    x = p_ref[0]
    bd = _head_blockdiag(HG_WIDTH)
    seg = lambda t: _mm_exact_rhs(t, bd)
    for d in range(2):
        lbd = lb_ref[d:d + 1, :]
        f = lbd + (1.0 - lbd) * _sigmoid(x[:, 256 + 256 * d:512 + 256 * d])
        lf_scr[d] = jnp.log(jnp.maximum(f, HG_F_MIN))
        kf_scr[d] = 1.0 - f
        for h in range(HG_HEADS):
            s_scr[d, h] = s0_ref[0, d, h].T

    ti = lax.broadcasted_iota(jnp.int32, (R, R), 0)
    si = lax.broadcasted_iota(jnp.int32, (R, R), 1)
    same_blk = (ti // c) == (si // c)
    t16 = lax.broadcasted_iota(jnp.int32, (c, 1), 0)

    def body(i, carry):
        for d in range(2):
            ci = i if d == 0 else n_it - 1 - i
            rows = pl.ds(pl.multiple_of(ci * R, R), R)
            incl = (ti >= si) if d == 0 else (ti <= si)
            lf = lf_scr[d, rows, :]
            cum = _mm_exact_lhs((incl & same_blk).astype(F32), lf)
            tot = _mm_exact_lhs(same_blk.astype(F32), lf)
            xq = p_ref[0, rows, 0:256]
            q = xq * _sigmoid(xq)
            v = p_ref[0, rows, 768:1024]
            kf = kf_scr[d, rows, :]
            Qp = q * jnp.exp(cum)
            Kh = kf * jnp.exp(tot - cum)
            e_tot = jnp.exp(tot)
            blocks = range(R // c) if d == 0 else range(R // c - 1, -1, -1)
            o_parts = [None] * (R // c)
            for j in blocks:
                rs = slice(j * c, (j + 1) * c)
                cb, qb, kb, vb = cum[rs], q[rs], kf[rs], v[rs]
                prods = []
                for s in range(c):
                    e = jnp.exp(jnp.minimum(cb - cb[s:s + 1, :], 0.0))
                    prods.append(qb * (kb[s:s + 1, :] * e))
                att = _mm_exact_rhs(jnp.concatenate(prods, axis=0), bd, n=1)
                o_blk = jnp.zeros((c, HG_WIDTH), F32)
                for s in range(c):
                    keep = (t16 >= s) if d == 0 else (t16 <= s)
                    o_blk = o_blk + jnp.where(keep, att[s * c:(s + 1) * c], 0.0) * vb[s:s + 1, :]
                o_heads = []
                for h in range(HG_HEADS):
                    hs = slice(h * HEAD_DIM, (h + 1) * HEAD_DIM)
                    ST = s_scr[d, h]
                    o_heads.append(_mm(Qp[rs, hs], ST, _NT))
                    s_scr[d, h] = ST * e_tot[j * c:j * c + 1, hs] + _mm(vb[:, hs], Kh[rs, hs], _TN)
                o_parts[j] = o_blk + jnp.concatenate(o_heads, axis=1)
            o_scr[d, rows, :] = jnp.concatenate(o_parts, axis=0)
        return carry

    lax.fori_loop(0, n_it, body, 0)

    o = o_scr[0] + o_scr[1]
    o = o * lax.rsqrt(seg(o * o) * (1.0 / HEAD_DIM) + NORM_EPS) * ng_ref[...]
    gg = x[:, 1024:1280]
    out_ref[0] = o * (gg * _sigmoid(gg))
    for d in range(2):
        for h in range(HG_HEADS):
            st_ref[0, d, h] = s_scr[d, h].T


def hgrn2_mixer_pallas(p_hg, s0, lb, norm_g):
    B, T, W = p_hg.shape
    full = lambda shape: pl.BlockSpec(shape, lambda b: (0,) * len(shape))
    st_spec = pl.BlockSpec((1, 2, HG_HEADS, HEAD_DIM, HEAD_DIM), lambda b: (b, 0, 0, 0, 0))
    return pl.pallas_call(
        partial(_hgrn_kernel, T=T),
        grid=(B,),
        in_specs=[pl.BlockSpec((1, T, W), lambda b: (b, 0, 0)), st_spec, full((2, HG_WIDTH)), full((1, HG_WIDTH))],
        out_specs=[pl.BlockSpec((1, T, HG_WIDTH), lambda b: (b, 0, 0)), st_spec],
        out_shape=[jax.ShapeDtypeStruct((B, T, HG_WIDTH), F32),
                   jax.ShapeDtypeStruct((B, 2, HG_HEADS, HEAD_DIM, HEAD_DIM), F32)],
        scratch_shapes=[pltpu.VMEM((2, T, HG_WIDTH), F32)] * 3
        + [pltpu.VMEM((2, HG_HEADS, HEAD_DIM, HEAD_DIM), F32)],
        compiler_params=pltpu.CompilerParams(dimension_semantics=("arbitrary",), vmem_limit_bytes=48 * 1024 * 1024),
        name="hgrn2_mixer",
    )(p_hg, s0, lb, jnp.tile(norm_g.reshape(1, HEAD_DIM), (1, HG_HEADS)))


ATT_REP = ATT_HEADS // ATT_KV_HEADS
ATT_QROWS = 128


def _swap_pairs(x):
    w = x.shape[-1]
    lane = lax.broadcasted_iota(jnp.int32, x.shape, x.ndim - 1)
    return jnp.where(lane % 2 == 0, pltpu.roll(x, w - 1, x.ndim - 1), pltpu.roll(x, 1, x.ndim - 1))


def _att_kernel(*refs, T, past, rope):
    if rope:
        p_ref, qg_ref, kg_ref, cos_ref, sin_ref, ck_ref, cv_ref, out_ref, k_scr, v_scr, q_scr = refs
    else:
        p_ref, qg_ref, kg_ref, out_ref, kh_ref, vh_ref, k_scr, v_scr, q_scr = refs
    x = p_ref[0]
    q = x[:, 0:ATT_WIDTH]
    k = x[:, ATT_WIDTH:ATT_WIDTH + KV_WIDTH]
    v = x[:, ATT_WIDTH + KV_WIDTH:ATT_WIDTH + 2 * KV_WIDTH]
    inv_d = 1.0 / HEAD_DIM
    q = q * lax.rsqrt(_mm_exact_rhs(q * q, _head_blockdiag(ATT_WIDTH)) * inv_d + NORM_EPS) * qg_ref[...]
    k = k * lax.rsqrt(_mm_exact_rhs(k * k, _head_blockdiag(KV_WIDTH)) * inv_d + NORM_EPS) * kg_ref[...]
    if rope:
        cos, sin = cos_ref[...], sin_ref[...]
        rep = ATT_WIDTH // KV_WIDTH
        q = q * jnp.concatenate([cos] * rep, axis=1) + _swap_pairs(q) * jnp.concatenate([sin] * rep, axis=1)
        k = k * cos + _swap_pairs(k) * sin
    q_scr[...] = (q * (1.0 / math.sqrt(HEAD_DIM))).astype(BF16)
    for g in range(ATT_KV_HEADS):
        gs = slice(g * HEAD_DIM, (g + 1) * HEAD_DIM)
        if rope:
            k_scr[g, 0:past, :] = ck_ref[0, g].astype(BF16)
            v_scr[g, 0:past, :] = cv_ref[0, g].astype(BF16)
        else:
            kh_ref[0, g] = k[:, gs]
            vh_ref[0, g] = v[:, gs]
        k_scr[g, past:past + T, :] = k[:, gs].astype(BF16)
        v_scr[g, past:past + T, :] = v[:, gs].astype(BF16)
    QR = ATT_QROWS

    def q_block(qb, carry):
        rows = pl.ds(pl.multiple_of(qb * QR, QR), QR)
        qblk = q_scr[rows, :]
        for g in range(ATT_KV_HEADS):
            qs = jnp.concatenate([qblk[:, (g * ATT_REP + r) * HEAD_DIM:(g * ATT_REP + r + 1) * HEAD_DIM]
                                  for r in range(ATT_REP)], axis=0)
            s = lax.dot_general(qs, k_scr[g], _NT, preferred_element_type=F32)
            e = jnp.exp(s - jnp.max(s, axis=-1, keepdims=True))
            l = jnp.sum(e, axis=-1, keepdims=True)
            o = lax.dot_general(e.astype(BF16), v_scr[g], _NN, preferred_element_type=F32) / l
            for r in range(ATT_REP):
                h = g * ATT_REP + r
                out_ref[0, rows, h * HEAD_DIM:(h + 1) * HEAD_DIM] = o[r * QR:(r + 1) * QR]
        return carry

    lax.fori_loop(0, T // QR, q_block, 0)


def rope_tables(T):
    rows = T // GRID_W
    row = jnp.repeat(jnp.arange(rows, dtype=F32), GRID_W)
    col = jnp.tile(jnp.arange(GRID_W, dtype=F32), rows)
    n_freq = HEAD_DIM // 4
    inv = ROPE_THETA ** (-jnp.arange(n_freq, dtype=F32) / n_freq)
    ang = jnp.concatenate([row[:, None] * inv, col[:, None] * inv], axis=-1)
    cos = jnp.repeat(jnp.cos(ang), 2, axis=-1)
    sin = jnp.stack([-jnp.sin(ang), jnp.sin(ang)], axis=-1).reshape(T, HEAD_DIM)
    return jnp.tile(cos, (1, ATT_KV_HEADS)), jnp.tile(sin, (1, ATT_KV_HEADS))


def attention_pallas(p_att, qnorm_g, knorm_g, cache=None):
    B, T, W = p_att.shape
    rope = cache is not None
    past = cache[0].shape[2] if rope else 0
    full = lambda shape: pl.BlockSpec(shape, lambda b: (0,) * len(shape))
    qg = jnp.tile(qnorm_g.reshape(1, HEAD_DIM), (1, ATT_HEADS))
    kg = jnp.tile(knorm_g.reshape(1, HEAD_DIM), (1, ATT_KV_HEADS))
    in_specs = [pl.BlockSpec((1, T, W), lambda b: (b, 0, 0)), full((1, ATT_WIDTH)), full((1, KV_WIDTH))]
    args = [p_att, qg, kg]
    out_specs = [pl.BlockSpec((1, T, ATT_WIDTH), lambda b: (b, 0, 0))]
    out_shape = [jax.ShapeDtypeStruct((B, T, ATT_WIDTH), F32)]
    if rope:
        cos, sin = rope_tables(T)
        kv_spec = pl.BlockSpec((1, ATT_KV_HEADS, past, HEAD_DIM), lambda b: (b, 0, 0, 0))
        in_specs += [full((T, KV_WIDTH)), full((T, KV_WIDTH)), kv_spec, kv_spec]
        args += [cos, sin, cache[0], cache[1]]
    else:
        kv_spec = pl.BlockSpec((1, ATT_KV_HEADS, T, HEAD_DIM), lambda b: (b, 0, 0, 0))
        out_specs += [kv_spec, kv_spec]
        out_shape += [jax.ShapeDtypeStruct((B, ATT_KV_HEADS, T, HEAD_DIM), F32)] * 2
    res = pl.pallas_call(
        partial(_att_kernel, T=T, past=past, rope=rope),
        grid=(B,),
        in_specs=in_specs,
        out_specs=out_specs,
        out_shape=out_shape,
        scratch_shapes=[pltpu.VMEM((ATT_KV_HEADS, past + T, HEAD_DIM), BF16)] * 2
        + [pltpu.VMEM((T, ATT_WIDTH), BF16)],
        compiler_params=pltpu.CompilerParams(dimension_semantics=("arbitrary",), vmem_limit_bytes=48 * 1024 * 1024),
        name="attention_rope" if rope else "attention_ctx",
    )(*args)
    return res[0] if rope else tuple(res)


ROW_TILE = 256
MOD_TILE = 1536
ROUTE_TILE = 256
MOE_SLOTS = 256
MOE_WEIGHT_SPLIT = 4
SC_ROWS = 32
SC_LANES = 128
P_ATT, P_RW, P_HG = ATT_WIDTH + 2 * KV_WIDTH, 3 * RW_WIDTH + 384, 5 * HG_WIDTH


def _mod_kernel(c_ref, w_ref, b_ref, o_ref):
    c = c_ref[...]
    o_ref[0] = _mm(c * _sigmoid(c), w_ref[0], passes=3) + b_ref[0]


def adaln_mod_pallas(cvec, w_mod, b_mod):
    n = 6 * D_MODEL
    return pl.pallas_call(
        _mod_kernel,
        grid=(DEPTH, n // MOD_TILE),
        in_specs=[pl.BlockSpec((8, D_MODEL), lambda l, j: (0, 0)),
                  pl.BlockSpec((1, D_MODEL, MOD_TILE), lambda l, j: (l, 0, j)),
                  pl.BlockSpec((1, 1, MOD_TILE), lambda l, j: (l, 0, j))],
        out_specs=pl.BlockSpec((1, 8, MOD_TILE), lambda l, j: (l, 0, j)),
        out_shape=jax.ShapeDtypeStruct((DEPTH, 8, n), F32),
        compiler_params=pltpu.CompilerParams(dimension_semantics=("arbitrary", "arbitrary"),
                                             vmem_limit_bytes=48 * 1024 * 1024),
        name="adaln_mod",
    )(cvec, w_mod, b_mod.reshape(DEPTH, 1, n))


def _rms(x):
    return x * lax.rsqrt(jnp.mean(x * x, axis=-1, keepdims=True) + NORM_EPS)


def _moe_residual(x_ref, g_ref, g4_ref, pm_ref):
    y = g4_ref[:, 0:1] * g_ref[0]
    for k in range(1, TOP_K):
        y = y + g4_ref[:, k:k + 1] * g_ref[k]
    return x_ref[...] + pm_ref[0, 5:6, :] * y


def _in_kernel(*refs, has_res):
    if has_res:
        x_ref, gth_ref, g4_ref, pm_ref, m_ref, g_ref, w_ref, xo_ref, pa_ref, pr_ref, ph_ref = refs
        x = _moe_residual(x_ref, gth_ref, g4_ref, pm_ref)
        xo_ref[...] = x
    else:
        x_ref, m_ref, g_ref, w_ref, pa_ref, pr_ref, ph_ref = refs
        x = x_ref[...]
    h = _rms(x) * g_ref[...] * (1.0 + m_ref[0, 1:2, :]) + m_ref[0, 0:1, :]
    proj = lax.dot_general(h, w_ref[...], _NN, precision=lax.Precision.DEFAULT, preferred_element_type=F32)
    pa_ref[...] = proj[:, 0:P_ATT]
    pr_ref[...] = proj[:, P_ATT:P_ATT + P_RW]
    ph_ref[...] = proj[:, P_ATT + P_RW:]


def _res_specs(row0):
    t0 = row0 // ROW_TILE
    return [pl.BlockSpec((TOP_K, ROW_TILE, D_MODEL), lambda i: (0, i, 0)),
            pl.BlockSpec((ROW_TILE, TOP_K), lambda i: (i + t0, 0))]


def in_proj_pallas(x, mod, norm_g, w_in, rows_per_mod, res=None):
    R = x.shape[0]
    tpm = rows_per_mod // ROW_TILE
    rt = lambda w: pl.BlockSpec((ROW_TILE, w), lambda i: (i, 0))
    ms = pl.BlockSpec((1, 6, D_MODEL), lambda i: (i // tpm, 0, 0))
    full = lambda shape, **kw: pl.BlockSpec(shape, lambda i: (0,) * len(shape), **kw)
    in_specs = [rt(D_MODEL)] + (_res_specs(res[2]) + [ms] if res else []) + [
        ms, full((1, D_MODEL)), full((D_MODEL, w_in.shape[1]), pipeline_mode=pl.Buffered(1))]
    args = [x] + ([res[0], res[1], res[3]] if res else []) + [mod, norm_g.reshape(1, D_MODEL), w_in]
    widths = ([D_MODEL] if res else []) + [P_ATT, P_RW, P_HG]
    return pl.pallas_call(
        partial(_in_kernel, has_res=res is not None),
        grid=(R // ROW_TILE,),
        in_specs=in_specs,
        out_specs=[rt(w) for w in widths],
        out_shape=[jax.ShapeDtypeStruct((R, w), F32) for w in widths],
        compiler_params=pltpu.CompilerParams(dimension_semantics=("arbitrary",), vmem_limit_bytes=48 * 1024 * 1024),
        name="in_proj",
    )(*args)


def _out_kernel(att_ref, rw_ref, hg_ref, x_ref, m_ref, g_ref, w_ref, rw_w_ref, rb_ref, xo_ref, h_ref, lg_ref):
    d = lambda a, lo, hi: lax.dot_general(a, w_ref[lo:hi, :], _NN, precision=lax.Precision.DEFAULT,
                                          preferred_element_type=F32)
    mixo = (d(att_ref[...], 0, ATT_WIDTH) + d(rw_ref[...], ATT_WIDTH, ATT_WIDTH + RW_WIDTH)
            + d(hg_ref[...], ATT_WIDTH + RW_WIDTH, ATT_WIDTH + RW_WIDTH + HG_WIDTH))
    x = x_ref[...] + m_ref[0, 2:3, :] * mixo
    xo_ref[...] = x
    h = _rms(x) * g_ref[...] * (1.0 + m_ref[0, 4:5, :]) + m_ref[0, 3:4, :]
    h_ref[...] = h
    lg_ref[...] = _mm(h, rw_w_ref[...], passes=3) + rb_ref[...]


def out_proj_pallas(att, rw, hg, x, mod, norm_g, w_out, router_w, router_b, rows_per_mod):
    R = x.shape[0]
    tpm = rows_per_mod // ROW_TILE
    rt = lambda w: pl.BlockSpec((ROW_TILE, w), lambda i: (i, 0))
    full = lambda shape: pl.BlockSpec(shape, lambda i: (0,) * len(shape))
    return pl.pallas_call(
        _out_kernel,
        grid=(R // ROW_TILE,),
        in_specs=[rt(ATT_WIDTH), rt(RW_WIDTH), rt(HG_WIDTH), rt(D_MODEL),
                  pl.BlockSpec((1, 6, D_MODEL), lambda i: (i // tpm, 0, 0)), full((1, D_MODEL)),
                  full((D_MODEL, D_MODEL)), full((D_MODEL, N_EXPERTS)), full((1, N_EXPERTS))],
        out_specs=[rt(D_MODEL), rt(D_MODEL), rt(N_EXPERTS)],
        out_shape=[jax.ShapeDtypeStruct((R, D_MODEL), F32), jax.ShapeDtypeStruct((R, D_MODEL), F32),
                   jax.ShapeDtypeStruct((R, N_EXPERTS), F32)],
        compiler_params=pltpu.CompilerParams(dimension_semantics=("arbitrary",), vmem_limit_bytes=48 * 1024 * 1024),
        name="out_proj",
    )(att, rw, hg, x, mod, norm_g.reshape(1, D_MODEL), w_out, router_w, router_b.reshape(1, N_EXPERTS))


def _final_kernel(x_ref, gth_ref, g4_ref, m_ref, g_ref, o_ref):
    o_ref[...] = _rms(_moe_residual(x_ref, gth_ref, g4_ref, m_ref)) * g_ref[...]


def final_norm_pallas(x, gathered, gate4, row0, mod, norm_g, rows_per_mod):
    R = x.shape[0]
    tpm = rows_per_mod // ROW_TILE
    rt = pl.BlockSpec((ROW_TILE, D_MODEL), lambda i: (i, 0))
    return pl.pallas_call(
        _final_kernel,
        grid=(R // ROW_TILE,),
        in_specs=[rt] + _res_specs(row0) + [pl.BlockSpec((1, 6, D_MODEL), lambda i: (i // tpm, 0, 0)),
                                            pl.BlockSpec((1, D_MODEL), lambda i: (0, 0))],
        out_specs=rt,
        out_shape=jax.ShapeDtypeStruct((R, D_MODEL), F32),
        name="final_norm",
    )(x, gathered, gate4, mod, norm_g.reshape(1, D_MODEL))


def _moe_max_blocks(n_tok):
    return (n_tok * TOP_K + N_EXPERTS * (MOE_SLOTS - 1)) // MOE_SLOTS


def _exact_nt_ones(a):
    ones = jnp.ones((8, a.shape[1]), BF16)
    out = None
    for t in reversed(_split(a, 3)):
        y = lax.dot_general(ones, t, _NT, preferred_element_type=F32)
        out = y if out is None else out + y
    return out


def _route_kernel(lg_ref, dest_ref, gate4_ref, blk_ref, rank_scr, gate_scr, *, n_tok):
    Rt, E = ROUTE_TILE, N_EXPERTS
    n_tiles = n_tok // Rt
    lane = lax.broadcasted_iota(jnp.int32, (Rt, E), 1)
    ti = lax.broadcasted_iota(jnp.int32, (Rt, Rt), 0)
    si = lax.broadcasted_iota(jnp.int32, (Rt, Rt), 1)
    lower = (ti > si).astype(BF16)
    ei = lax.broadcasted_iota(jnp.int32, (E, E), 0)
    ej = lax.broadcasted_iota(jnp.int32, (E, E), 1)
    upper_e = (ei < ej).astype(BF16)

    def tile_members(it, off):
        rows = pl.ds(pl.multiple_of(it * Rt, Rt), Rt)
        lg = lg_ref[rows, :]
        work = lg
        member = jnp.zeros((Rt, E), jnp.bool_)
        top = None
        for k in range(TOP_K):
            m = jnp.max(work, axis=-1, keepdims=True)
            if top is None:
                top = m
            first = jnp.min(jnp.where(work == m, lane, E), axis=-1, keepdims=True)
            pick = lane == first
            member = member | pick
            work = jnp.where(pick, -jnp.inf, work)
        ex = jnp.where(member, jnp.exp(lg - top), 0.0)
        gate_scr[rows, :] = ex / jnp.sum(ex, axis=-1, keepdims=True)
        mem = member.astype(BF16)
        rank = lax.dot_general(lower, mem, _NN, preferred_element_type=F32) + off
        rank_scr[rows, :] = jnp.where(member, rank, -1.0)
        return off + jnp.sum(mem.astype(F32), axis=0, keepdims=True)

    count = lax.fori_loop(0, n_tiles, tile_members, jnp.zeros((1, E), F32))
    nblk = jnp.floor((count + (MOE_SLOTS - 1)) * (1.0 / MOE_SLOTS))
    bstart = lax.dot_general(nblk.astype(BF16), upper_e, _NN, preferred_element_type=F32)
    bend = bstart + nblk
    pstart = bstart * MOE_SLOTS

    def tile_slots(it, carry):
        rows = pl.ds(pl.multiple_of(it * Rt, Rt), Rt)
        rank = rank_scr[rows, :]
        gate = gate_scr[rows, :]
        member = rank >= 0.0
        kidx = lax.dot_general(member.astype(BF16), upper_e, _NN, preferred_element_type=F32)
        slot = pstart + rank
        cols = []
        for k in range(TOP_K):
            sel = member & (kidx == k)
            dest_ref[k:k + 1, rows] = _exact_nt_ones(jnp.where(sel, slot, 0.0))[0:1].astype(jnp.int32)
            cols.append(jnp.sum(jnp.where(sel, gate, 0.0), axis=-1, keepdims=True))
        gate4_ref[rows, :] = jnp.concatenate(cols, axis=1)
        return carry

    lax.fori_loop(0, n_tiles, tile_slots, 0)
    nb = blk_ref.shape[1]
    bi = lax.broadcasted_iota(jnp.int32, (E, nb), 1).astype(F32)
    bend_col = jnp.sum(jnp.where(ei == ej, jnp.broadcast_to(bend, (E, E)), 0.0), axis=1, keepdims=True)
    owner = jnp.sum((bend_col <= bi).astype(F32), axis=0, keepdims=True)
    blk_ref[0:1, :] = jnp.minimum(owner, E - 1.0).astype(jnp.int32)
    blk_ref[1:2, :] = jnp.broadcast_to(jnp.sum(nblk, axis=-1, keepdims=True), (1, nb)).astype(jnp.int32)
    blk_ref[2:8, :] = jnp.zeros((6, nb), jnp.int32)


def moe_route_pallas(logits):
    n_tok = logits.shape[0]
    nb = -(-_moe_max_blocks(n_tok) // 128) * 128
    return pl.pallas_call(
        partial(_route_kernel, n_tok=n_tok),
        out_shape=[jax.ShapeDtypeStruct((TOP_K, n_tok), jnp.int32),
                   jax.ShapeDtypeStruct((n_tok, TOP_K), F32),
                   jax.ShapeDtypeStruct((8, nb), jnp.int32)],
        scratch_shapes=[pltpu.VMEM((n_tok, N_EXPERTS), F32)] * 2,
        name="moe_route",
    )(logits)


def _moe_block_kernel(be_ref, nu_ref, first_ref, par_ref, nxt_ref, xb_ref, wgu_hbm, bgu_ref, wdn_hbm, bdn_ref, yb_ref,
                      wgu_buf, wdn_buf, sem, *, l):
    i = pl.program_id(0)

    def weight_copies(e, slot):
        cps = []
        for hbm, buf, s in ((wgu_hbm, wgu_buf, 0), (wdn_hbm, wdn_buf, 1)):
            rows = buf.shape[1] // MOE_WEIGHT_SPLIT
            for j in range(MOE_WEIGHT_SPLIT):
                rs = pl.ds(j * rows, rows)
                cps.append(pltpu.make_async_copy(hbm.at[l, e, rs], buf.at[slot, rs], sem.at[s, slot]))
        return cps

    @pl.when(i < nu_ref[0])
    def _():
        slot = par_ref[i]

        @pl.when(first_ref[i] == 1)
        def _():
            @pl.when(i == 0)
            def _():
                for cp in weight_copies(be_ref[0], slot):
                    cp.start()

            for cp in weight_copies(be_ref[i], slot):
                cp.wait()

            @pl.when(nxt_ref[i] >= 0)
            def _():
                for cp in weight_copies(nxt_ref[i], 1 - slot):
                    cp.start()

        dot = lambda a, w: lax.dot_general(a, w, _NN, precision=lax.Precision.DEFAULT, preferred_element_type=F32)
        gu = dot(xb_ref[...], wgu_buf[slot]) + bgu_ref[0, 0]
        glu = jnp.minimum(gu[:, :EXPERT_FF], SWIGLU_LIMIT)
        lin = jnp.clip(gu[:, EXPERT_FF:], -SWIGLU_LIMIT, SWIGLU_LIMIT)
        act = glu * _sigmoid(SWIGLU_ALPHA * glu) * (lin + 1.0)
        yb_ref[...] = dot(act, wdn_buf[slot]) + bdn_ref[0, 0]


def _expert_runs(block_e, n_used):
    n = block_e.shape[0]
    idx = jnp.arange(n, dtype=jnp.int32)
    valid = idx < n_used[0]
    first = valid & ((idx == 0) | (block_e != jnp.roll(block_e, 1)))
    par = (jnp.cumsum(first.astype(jnp.int32)) - 1) % 2
    start = jnp.where(first, idx, n)
    nxt_start = lax.cummin(jnp.concatenate([start[1:], jnp.full((1,), n, jnp.int32)]), reverse=True)
    nxt = jnp.where(nxt_start < n, block_e[jnp.minimum(nxt_start, n - 1)], -1)
    return first.astype(jnp.int32), par.astype(jnp.int32), nxt.astype(jnp.int32)


def moe_blocks_pallas(xb, block_e, n_used, l, w_gu, b_gu, w_down, b_down):
    n_blocks = xb.shape[0] // MOE_SLOTS
    first, par, nxt = _expert_runs(block_e, n_used)
    blk = lambda i, be, nu, *_: (jnp.minimum(i, nu[0] - 1), 0)
    bsel = lambda i, be, nu, *_: (l, be[jnp.minimum(i, nu[0] - 1)], 0, 0)
    grid_spec = pltpu.PrefetchScalarGridSpec(
        num_scalar_prefetch=5,
        grid=(n_blocks,),
        in_specs=[pl.BlockSpec((MOE_SLOTS, D_MODEL), blk),
                  pl.BlockSpec(memory_space=pl.ANY),
                  pl.BlockSpec((1, 1, 1, 2 * EXPERT_FF), bsel),
                  pl.BlockSpec(memory_space=pl.ANY),
                  pl.BlockSpec((1, 1, 1, D_MODEL), bsel)],
        out_specs=pl.BlockSpec((MOE_SLOTS, D_MODEL), blk),
        scratch_shapes=[pltpu.VMEM((2, D_MODEL, 2 * EXPERT_FF), F32), pltpu.VMEM((2, EXPERT_FF, D_MODEL), F32),
                        pltpu.SemaphoreType.DMA((2, 2))],
    )
    return pl.pallas_call(
        partial(_moe_block_kernel, l=l),
        grid_spec=grid_spec,
        out_shape=jax.ShapeDtypeStruct(xb.shape, F32),
        compiler_params=pltpu.CompilerParams(dimension_semantics=("arbitrary",), vmem_limit_bytes=48 * 1024 * 1024),
        name="moe_blocks",
    )(block_e, n_used, first, par, nxt, xb, w_gu, b_gu.reshape(DEPTH, N_EXPERTS, 1, 2 * EXPERT_FF), w_down,
      b_down.reshape(DEPTH, N_EXPERTS, 1, D_MODEL))


def _sc_mesh():
    return plsc.VectorSubcoreMesh(core_axis_name="c", subcore_axis_name="s")


def _sc_index_rows(idx):
    return jnp.pad(idx.reshape(-1, SC_ROWS), ((0, 0), (0, SC_LANES - SC_ROWS)))


def sc_dispatch(h, dest, n_rows):
    n_tok, d = h.shape
    idx = [_sc_index_rows(dest[k]) for k in range(TOP_K)]

    @pl.kernel(out_type=jax.ShapeDtypeStruct((n_rows, d), h.dtype), mesh=_sc_mesh(), scratch_types=[])
    def kern(h_hbm, i0, i1, i2, i3, o_hbm):
        def body(x_vmem, *i_vmem):
            for iv in i_vmem:
                pltpu.sync_copy(x_vmem, o_hbm.at[iv.at[0, pl.ds(0, SC_ROWS)]])

        pltpu.emit_pipeline(
            body,
            grid=(n_tok // SC_ROWS,),
            in_specs=[pl.BlockSpec((SC_ROWS, d), lambda i: (i, 0))]
            + [pl.BlockSpec((1, SC_LANES), lambda i: (i, 0))] * TOP_K,
            out_specs=[],
            core_axis_name=("c", "s"),
            dimension_semantics=(pltpu.PARALLEL,),
        )(h_hbm, i0, i1, i2, i3)

    return kern(h, *idx)


def sc_combine_gather(yb, dest):
    n_tok = dest.shape[1]
    d = yb.shape[1]
    idx = _sc_index_rows(dest.reshape(TOP_K * n_tok))

    @pl.kernel(out_type=jax.ShapeDtypeStruct((TOP_K * n_tok, d), yb.dtype), mesh=_sc_mesh(), scratch_types=[])
    def kern(y_hbm, i_hbm, o_hbm):
        def body(i_vmem, o_vmem):
            pltpu.sync_copy(y_hbm.at[i_vmem.at[0, pl.ds(0, SC_ROWS)]], o_vmem)

        pltpu.emit_pipeline(
            body,
            grid=(TOP_K * n_tok // SC_ROWS,),
            in_specs=[pl.BlockSpec((1, SC_LANES), lambda i: (i, 0))],
            out_specs=[pl.BlockSpec((SC_ROWS, d), lambda i: (i, 0))],
            core_axis_name=("c", "s"),
            dimension_semantics=(pltpu.PARALLEL,),
        )(i_hbm, o_hbm)

    return kern(yb, idx).reshape(TOP_K, n_tok, d)


def hgrn_lower_bounds(hg_lb):
    sm = jax.nn.softmax(hg_lb.astype(jnp.float32), axis=0)
    return jnp.cumsum(sm, axis=0) - sm[0:1]


def kernel(x_prompt, x_sample, cache_att_k, cache_att_v, state_rwkv, state_hgrn, c, c_ctx, w_mod, b_mod, norm_mix_g, norm_ffn_g, w_in, w_out, att_qnorm_g, att_knorm_g, rw_w0, rw_w2, rw_a0, rw_a2, rw_g2, rw_kk, rw_ka, rw_rk, rw_gn_g, rw_gn_b, hg_lb, hg_norm_g, router_w, router_b, moe_w_gu, moe_b_gu, moe_w_down, moe_b_down, final_norm_g):
    BP, TP, _ = x_prompt.shape
    BS, TS, _ = x_sample.shape
    n_p, n_s = BP * TP, BS * TS
    lb_all = hgrn_lower_bounds(hg_lb)
    cvec = jnp.concatenate([c_ctx[None, :], c, jnp.zeros((8 - 1 - BS, D_MODEL), F32)], axis=0)
    mod_all = adaln_mod_pallas(cvec, w_mod, b_mod).reshape(DEPTH, 8, 6, D_MODEL)
    zeros_state = jnp.zeros((BP, 2, RW_HEADS, HEAD_DIM, HEAD_DIM), F32)
    x = {'p': x_prompt.reshape(n_p, D_MODEL), 's': x_sample.reshape(n_s, D_MODEL)}
    dims = {'p': (TP, BP, n_p, 0), 's': (TS, BS, TS, n_p)}
    moe_out, mod_prev = None, None
    ks, vs, srs, shs = [], [], [], []
    for l in range(DEPTH):
        prm = dict(rw_w0=rw_w0[l], rw_w2=rw_w2[l], rw_a0=rw_a0[l], rw_a2=rw_a2[l], rw_g2=rw_g2[l],
                   rw_kk=rw_kk[l], rw_ka=rw_ka[l], rw_rk=rw_rk[l], rw_gn_g=rw_gn_g[l], rw_gn_b=rw_gn_b[l])
        mods = {'p': mod_all[l, 0:1], 's': mod_all[l, 1:1 + BS]}
        h2, logits = {}, {}
        for s in ('p', 's'):
            T, B, rpm, row0 = dims[s]
            if l == 0:
                p_att, p_rw, p_hg = in_proj_pallas(x[s], mods[s], norm_mix_g[l], w_in[l], rpm)
            else:
                x[s], p_att, p_rw, p_hg = in_proj_pallas(x[s], mods[s], norm_mix_g[l], w_in[l], rpm,
                                                         res=(*moe_out[s], row0, mod_prev[s]))
            p_att, p_rw, p_hg = (t.reshape(B, T, -1) for t in (p_att, p_rw, p_hg))
            if s == 'p':
                att, k_l, v_l = attention_pallas(p_att, att_qnorm_g[l], att_knorm_g[l])
                rw_out, sr_l = rwkv7_mixer_pallas(p_rw, zeros_state, prm)
                hg_out, sh_l = hgrn2_mixer_pallas(p_hg, zeros_state, lb_all[l], hg_norm_g[l])
                ks.append(k_l)
                vs.append(v_l)
                srs.append(sr_l)
                shs.append(sh_l)
            else:
                att = attention_pallas(p_att, att_qnorm_g[l], att_knorm_g[l], (cache_att_k[:, l], cache_att_v[:, l]))
                rw_out, _ = rwkv7_mixer_pallas(p_rw, state_rwkv[:, l], prm)
                hg_out, _ = hgrn2_mixer_pallas(p_hg, state_hgrn[:, l], lb_all[l], hg_norm_g[l])
            x[s], h2[s], logits[s] = out_proj_pallas(att.reshape(B * T, -1), rw_out.reshape(B * T, -1),
                                                     hg_out.reshape(B * T, -1), x[s], mods[s], norm_ffn_g[l],
                                                     w_out[l], router_w[l], router_b[l], rpm)
        h_all = jnp.concatenate([h2['p'], h2['s']], axis=0)
        dest, gate4, blk = moe_route_pallas(jnp.concatenate([logits['p'], logits['s']], axis=0))
        xb = sc_dispatch(h_all, dest, _moe_max_blocks(n_p + n_s) * MOE_SLOTS)
        yb = moe_blocks_pallas(xb, blk[0], blk[1, :1], l, moe_w_gu, moe_b_gu, moe_w_down, moe_b_down)
        moe_out = {'p': (sc_combine_gather(yb, dest[:, :n_p]), gate4), 's': (sc_combine_gather(yb, dest[:, n_p:]), gate4)}
        mod_prev = mods
    y_prompt = final_norm_pallas(x['p'], *moe_out['p'], 0, mod_prev['p'], final_norm_g, n_p)
    y_sample = final_norm_pallas(x['s'], *moe_out['s'], n_p, mod_prev['s'], final_norm_g, TS)
    return (y_prompt.reshape(x_prompt.shape), y_sample.reshape(x_sample.shape), jnp.stack(ks, axis=1),
            jnp.stack(vs, axis=1), jnp.stack(srs, axis=1), jnp.stack(shs, axis=1))
```

```python
import math
from functools import partial

import jax
import jax.numpy as jnp
from jax import lax
from jax.experimental import pallas as pl
from jax.experimental.pallas import tpu as pltpu
from jax.experimental.pallas import tpu_sc as plsc

D_MODEL = 1024
DEPTH = 2
GRID_W = 64
HEAD_DIM = 64
ATT_HEADS = 8
ATT_KV_HEADS = 2
ATT_WIDTH = ATT_HEADS * HEAD_DIM
KV_WIDTH = ATT_KV_HEADS * HEAD_DIM
RW_HEADS = 4
RW_WIDTH = RW_HEADS * HEAD_DIM
RW_GN_EPS = 64e-5
HG_HEADS = 4
HG_WIDTH = HG_HEADS * HEAD_DIM
HG_F_MIN = 1e-6
N_EXPERTS = 32
TOP_K = 4
EXPERT_FF = D_MODEL
SWIGLU_LIMIT = 7.0
SWIGLU_ALPHA = 1.702
ROPE_THETA = 10000.0
NORM_EPS = 1e-6

RW_CHUNK = 64
BF16 = jnp.bfloat16
F32 = jnp.float32

_NN = (((1,), (0,)), ((), ()))
_NT = (((1,), (1,)), ((), ()))
_TN = (((0,), (0,)), ((), ()))


def _split(x, n):
    parts = []
    for _ in range(n - 1):
        hi = x.astype(BF16)
        parts.append(hi)
        x = x - hi.astype(F32)
    parts.append(x.astype(BF16))
    return parts


def _mm(a, b, dims=_NN, passes=1):
    d = lambda x, y: lax.dot_general(x, y, dims, preferred_element_type=F32)
    if passes == 1:
        return d(a.astype(BF16), b.astype(BF16))
    ah, al = _split(a, 2)
    bh, bl = _split(b, 2)
    return d(ah, bl) + d(al, bh) + d(ah, bh)


def _mm_exact_lhs(a01, b, n=3):
    a = a01.astype(BF16)
    out = None
    for t in reversed(_split(b, n)):
        y = lax.dot_general(a, t, _NN, preferred_element_type=F32)
        out = y if out is None else out + y
    return out


def _mm_exact_rhs(a, b01, n=3):
    b = b01.astype(BF16)
    out = None
    for t in reversed(_split(a, n)):
        y = lax.dot_general(t, b, _NN, preferred_element_type=F32)
        out = y if out is None else out + y
    return out


def _head_blockdiag(width):
    r = lax.broadcasted_iota(jnp.int32, (width, width), 0) // HEAD_DIM
    c = lax.broadcasted_iota(jnp.int32, (width, width), 1) // HEAD_DIM
    return (r == c).astype(F32)


def _sigmoid(x):
    return 1.0 / (1.0 + jnp.exp(-x))


def _softplus(x):
    return jnp.maximum(x, 0.0) + jnp.log(1.0 + jnp.exp(-jnp.abs(x)))


def _rwkv_kernel(p_ref, s0_ref, w0_ref, w2_ref, a0_ref, a2_ref, g2_ref, kk_ref, ka_ref, rk_ref, gng_ref, gnb_ref,
                 out_ref, st_ref,
                 lw_scr, kd_scr, bb_scr, y_scr, kk_scr, s_scr, *, T, NB):
    C = RW_CHUNK
    n_chunks = T // C
    bd = _head_blockdiag(RW_WIDTH)
    seg = lambda t: _mm_exact_rhs(t, bd)
    ka = ka_ref[...]
    for nb in range(NB):
        k = p_ref[nb, :, 256:512]
        kk = k * kk_ref[...]
        kk = kk * lax.rsqrt(seg(kk * kk) + 1e-12)
        kk_scr[nb] = kk
        for d in range(2):
            wd = p_ref[nb, :, 768 + 64 * d:832 + 64 * d]
            ad = p_ref[nb, :, 896 + 64 * d:960 + 64 * d]
            w_raw = w0_ref[d:d + 1, :] + _mm(jnp.tanh(wd), w2_ref[d], passes=3)
            lw_scr[nb, d] = -jnp.exp(-_softplus(-w_raw) - 0.5)
            a = _sigmoid(a0_ref[d:d + 1, :] + _mm(ad, a2_ref[d], passes=3))
            kd_scr[nb, d] = k * (1.0 + (a - 1.0) * ka)
            bb_scr[nb, d] = kk * a
    s_scr[...] = s0_ref[...]

    ti = lax.broadcasted_iota(jnp.int32, (C, C), 0)
    si = lax.broadcasted_iota(jnp.int32, (C, C), 1)
    ones_cc = jnp.ones((C, C), F32)

    def chunk_body(i, carry):
        ch = []
        for nb, d in [(nb, d) for nb in range(NB) for d in range(2)]:
            ci = i if d == 0 else n_chunks - 1 - i
            rows = pl.ds(pl.multiple_of(ci * C, C), C)
            strict = (ti > si) if d == 0 else (ti < si)
            incl = (ti >= si) if d == 0 else (ti <= si)
            lw = lw_scr[nb, d, rows, :]
            cum = _mm_exact_lhs(incl.astype(F32), lw)
            total = _mm_exact_lhs(ones_cc, lw)
            cum_ex = cum - lw
            mid = 0.5 * total
            rr = p_ref[nb, rows, 0:256]
            vv = p_ref[nb, rows, 512:768]
            kdc = kd_scr[nb, d, rows, :]
            bbc = bb_scr[nb, d, rows, :]
            kkc = kk_scr[nb, rows, :]
            e_inv = jnp.exp(mid - cum)
            At = -kkc * jnp.exp(cum_ex - mid)
            Rt = rr * jnp.exp(cum - mid)
            Bt = bbc * e_inv
            Kt = kdc * e_inv
            Ap = -kkc * jnp.exp(cum_ex)
            Rp = rr * jnp.exp(cum)
            e_out = jnp.exp(total - cum)
            Bh = bbc * e_out
            Kh = kdc * e_out
            e_tot = jnp.exp(total[0:1, :])
            for h in range(RW_HEADS):
                hs = slice(h * HEAD_DIM, (h + 1) * HEAD_DIM)
                ch.append(dict(nb=nb, d=d, h=h, rows=rows, hs=hs, strict=strict, incl=incl,
                               AR=jnp.concatenate([At[:, hs], Rt[:, hs]], axis=0), Bt=Bt[:, hs], Kt=Kt[:, hs],
                               V=vv[:, hs], X1=Ap[:, hs], Rp=Rp[:, hs], Bh=Bh[:, hs], Kh=Kh[:, hs],
                               e_tot=e_tot[:, hs]))
        for c in ch:
            c['AB'] = _mm(c['AR'], c['Bt'], _NT)
            c['AK'] = _mm(c['AR'], c['Kt'], _NT)
        for c in ch:
            c['P'] = jnp.where(c['strict'], c['AB'][:C], 0.0)
            c['A_ak'] = jnp.where(c['strict'], c['AK'][:C], 0.0)
            c['A_rb'] = jnp.where(c['incl'], c['AB'][C:], 0.0)
            c['A_rk'] = jnp.where(c['incl'], c['AK'][C:], 0.0)
        for c in ch:
            c['X2'] = _mm(c['A_ak'], c['V'])
        for lvl in range(6):
            for c in ch:
                if lvl < 5:
                    c['PZ'] = _mm(c['P'], jnp.concatenate([c['P'], c['X1'], c['X2']], axis=1))
                else:
                    c['PZ'] = _mm(c['P'], jnp.concatenate([c['X1'], c['X2']], axis=1))
            for c in ch:
                PZ = c['PZ']
                if lvl < 5:
                    c['P'] = PZ[:, :C]
                    c['X1'] = c['X1'] + PZ[:, C:2 * C]
                    c['X2'] = c['X2'] + PZ[:, 2 * C:]
                else:
                    c['X1'] = c['X1'] + PZ[:, :C]
                    c['X2'] = c['X2'] + PZ[:, C:]
        for c in ch:
            c['S0'] = s_scr[c['nb'], c['d'], c['h']]
            c['UY'] = _mm(jnp.concatenate([c['X1'], c['Rp']], axis=0), c['S0'], _NT)
        for c in ch:
            c['U'] = c['UY'][:C] + c['X2']
        for c in ch:
            c['Y'] = c['UY'][C:] + _mm(c['A_rb'], c['U']) + _mm(c['A_rk'], c['V'])
            c['S1'] = c['S0'] * c['e_tot'] + _mm(c['U'], c['Bh'], _TN) + _mm(c['V'], c['Kh'], _TN)
        for c in ch:
            s_scr[c['nb'], c['d'], c['h']] = c['S1']
            y_scr[c['nb'], c['d'], c['rows'], c['hs']] = c['Y']
        return carry

    lax.fori_loop(0, n_chunks, chunk_body, 0)

    for nb in range(NB):
        r = p_ref[nb, :, 0:256]
        v = p_ref[nb, :, 512:768]
        bonus = seg(r * (kd_scr[nb, 0] + kd_scr[nb, 1]) * rk_ref[...]) * v
        g = _mm(_sigmoid(p_ref[nb, :, 1024:1152]), g2_ref[...], passes=3)
        y = y_scr[nb, 0] + y_scr[nb, 1]
        mu = seg(y) * (1.0 / HEAD_DIM)
        yc = y - mu
        var = seg(yc * yc) * (1.0 / HEAD_DIM)
        yn = yc * lax.rsqrt(var + RW_GN_EPS)
        out_ref[nb] = (yn * gng_ref[...] + gnb_ref[...] + bonus) * g
    st_ref[...] = s_scr[...]


RW_ROWS = 1024


def rwkv7_mixer_pallas(p_rw, s0, prm):
    B, T, W = p_rw.shape
    NB = max(2, RW_ROWS // T)
    row = lambda a: a.reshape(1, RW_WIDTH)
    full = lambda shape: pl.BlockSpec(shape, lambda b: (0,) * len(shape))
    st_spec = pl.BlockSpec((NB, 2, RW_HEADS, HEAD_DIM, HEAD_DIM), lambda b: (b, 0, 0, 0, 0))
    return pl.pallas_call(
        partial(_rwkv_kernel, T=T, NB=NB),
        grid=(B // NB,),
        in_specs=[pl.BlockSpec((NB, T, W), lambda b: (b, 0, 0)), st_spec,
                  full((2, RW_WIDTH)), full((2, 64, RW_WIDTH)), full((2, RW_WIDTH)), full((2, 64, RW_WIDTH)),
                  full((128, RW_WIDTH)), full((1, RW_WIDTH)), full((1, RW_WIDTH)), full((1, RW_WIDTH)),
                  full((1, RW_WIDTH)), full((1, RW_WIDTH))],
        out_specs=[pl.BlockSpec((NB, T, RW_WIDTH), lambda b: (b, 0, 0)), st_spec],
        out_shape=[jax.ShapeDtypeStruct((B, T, RW_WIDTH), F32),
                   jax.ShapeDtypeStruct((B, 2, RW_HEADS, HEAD_DIM, HEAD_DIM), F32)],
        scratch_shapes=[pltpu.VMEM((NB, 2, T, RW_WIDTH), F32)] * 4
        + [pltpu.VMEM((NB, T, RW_WIDTH), F32), pltpu.VMEM((NB, 2, RW_HEADS, HEAD_DIM, HEAD_DIM), F32)],
        compiler_params=pltpu.CompilerParams(dimension_semantics=("arbitrary",), vmem_limit_bytes=56 * 1024 * 1024),
        name="rwkv7_mixer",
    )(p_rw, s0, prm['rw_w0'], prm['rw_w2'], prm['rw_a0'], prm['rw_a2'], prm['rw_g2'], row(prm['rw_kk']),
      row(prm['rw_ka']), row(prm['rw_rk']), row(prm['rw_gn_g']), row(prm['rw_gn_b']))


HG_SUB = 16
HG_ROWS = 64


def _hgrn_kernel(p_ref, s0_ref, lb_ref, ng_ref, out_ref, st_ref, lf_scr, kf_scr, o_scr, s_scr, *, T):
    R, c = HG_ROWS, HG_SUB
    n_it = T // R
    x = p_ref[0]
    bd = _head_blockdiag(HG_WIDTH)
    seg = lambda t: _mm_exact_rhs(t, bd)
    for d in range(2):
        lbd = lb_ref[d:d + 1, :]
        f = lbd + (1.0 - lbd) * _sigmoid(x[:, 256 + 256 * d:512 + 256 * d])
        lf_scr[d] = jnp.log(jnp.maximum(f, HG_F_MIN))
        kf_scr[d] = 1.0 - f
        for h in range(HG_HEADS):
            s_scr[d, h] = s0_ref[0, d, h].T

    ti = lax.broadcasted_iota(jnp.int32, (R, R), 0)
    si = lax.broadcasted_iota(jnp.int32, (R, R), 1)
    same_blk = (ti // c) == (si // c)
    t16 = lax.broadcasted_iota(jnp.int32, (c, 1), 0)

    def body(i, carry):
        for d in range(2):
            ci = i if d == 0 else n_it - 1 - i
            rows = pl.ds(pl.multiple_of(ci * R, R), R)
            incl = (ti >= si) if d == 0 else (ti <= si)
            lf = lf_scr[d, rows, :]
            cum = _mm_exact_lhs((incl & same_blk).astype(F32), lf)
            tot = _mm_exact_lhs(same_blk.astype(F32), lf)
            xq = p_ref[0, rows, 0:256]
            q = xq * _sigmoid(xq)
            v = p_ref[0, rows, 768:1024]
            kf = kf_scr[d, rows, :]
            Qp = q * jnp.exp(cum)
            Kh = kf * jnp.exp(tot - cum)
            e_tot = jnp.exp(tot)
            blocks = range(R // c) if d == 0 else range(R // c - 1, -1, -1)
            o_parts = [None] * (R // c)
            for j in blocks:
                rs = slice(j * c, (j + 1) * c)
                cb, qb, kb, vb = cum[rs], q[rs], kf[rs], v[rs]
                prods = []
                for s in range(c):
                    e = jnp.exp(jnp.minimum(cb - cb[s:s + 1, :], 0.0))
                    prods.append(qb * (kb[s:s + 1, :] * e))
                att = _mm_exact_rhs(jnp.concatenate(prods, axis=0), bd, n=1)
                o_blk = jnp.zeros((c, HG_WIDTH), F32)
                for s in range(c):
                    keep = (t16 >= s) if d == 0 else (t16 <= s)
                    o_blk = o_blk + jnp.where(keep, att[s * c:(s + 1) * c], 0.0) * vb[s:s + 1, :]
                o_heads = []
                for h in range(HG_HEADS):
                    hs = slice(h * HEAD_DIM, (h + 1) * HEAD_DIM)
                    ST = s_scr[d, h]
                    o_heads.append(_mm(Qp[rs, hs], ST, _NT))
                    s_scr[d, h] = ST * e_tot[j * c:j * c + 1, hs] + _mm(vb[:, hs], Kh[rs, hs], _TN)
                o_parts[j] = o_blk + jnp.concatenate(o_heads, axis=1)
            o_scr[d, rows, :] = jnp.concatenate(o_parts, axis=0)
        return carry

    lax.fori_loop(0, n_it, body, 0)

    o = o_scr[0] + o_scr[1]
    o = o * lax.rsqrt(seg(o * o) * (1.0 / HEAD_DIM) + NORM_EPS) * ng_ref[...]
    gg = x[:, 1024:1280]
    out_ref[0] = o * (gg * _sigmoid(gg))
    for d in range(2):
        for h in range(HG_HEADS):
            st_ref[0, d, h] = s_scr[d, h].T


def hgrn2_mixer_pallas(p_hg, s0, lb, norm_g):
    B, T, W = p_hg.shape
    full = lambda shape: pl.BlockSpec(shape, lambda b: (0,) * len(shape))
    st_spec = pl.BlockSpec((1, 2, HG_HEADS, HEAD_DIM, HEAD_DIM), lambda b: (b, 0, 0, 0, 0))
    return pl.pallas_call(
        partial(_hgrn_kernel, T=T),
        grid=(B,),
        in_specs=[pl.BlockSpec((1, T, W), lambda b: (b, 0, 0)), st_spec, full((2, HG_WIDTH)), full((1, HG_WIDTH))],
        out_specs=[pl.BlockSpec((1, T, HG_WIDTH), lambda b: (b, 0, 0)), st_spec],
        out_shape=[jax.ShapeDtypeStruct((B, T, HG_WIDTH), F32),
                   jax.ShapeDtypeStruct((B, 2, HG_HEADS, HEAD_DIM, HEAD_DIM), F32)],
        scratch_shapes=[pltpu.VMEM((2, T, HG_WIDTH), F32)] * 3
        + [pltpu.VMEM((2, HG_HEADS, HEAD_DIM, HEAD_DIM), F32)],
        compiler_params=pltpu.CompilerParams(dimension_semantics=("arbitrary",), vmem_limit_bytes=48 * 1024 * 1024),
        name="hgrn2_mixer",
    )(p_hg, s0, lb, jnp.tile(norm_g.reshape(1, HEAD_DIM), (1, HG_HEADS)))


ATT_REP = ATT_HEADS // ATT_KV_HEADS
ATT_QROWS = 128


def _swap_pairs(x):
    w = x.shape[-1]
    lane = lax.broadcasted_iota(jnp.int32, x.shape, x.ndim - 1)
    return jnp.where(lane % 2 == 0, pltpu.roll(x, w - 1, x.ndim - 1), pltpu.roll(x, 1, x.ndim - 1))


def _att_kernel(*refs, T, past, rope):
    if rope:
        p_ref, qg_ref, kg_ref, cos_ref, sin_ref, ck_ref, cv_ref, out_ref, k_scr, v_scr, q_scr = refs
    else:
        p_ref, qg_ref, kg_ref, out_ref, kh_ref, vh_ref, k_scr, v_scr, q_scr = refs
    x = p_ref[0]
    q = x[:, 0:ATT_WIDTH]
    k = x[:, ATT_WIDTH:ATT_WIDTH + KV_WIDTH]
    v = x[:, ATT_WIDTH + KV_WIDTH:ATT_WIDTH + 2 * KV_WIDTH]
    inv_d = 1.0 / HEAD_DIM
    q = q * lax.rsqrt(_mm_exact_rhs(q * q, _head_blockdiag(ATT_WIDTH)) * inv_d + NORM_EPS) * qg_ref[...]
    k = k * lax.rsqrt(_mm_exact_rhs(k * k, _head_blockdiag(KV_WIDTH)) * inv_d + NORM_EPS) * kg_ref[...]
    if rope:
        cos, sin = cos_ref[...], sin_ref[...]
        rep = ATT_WIDTH // KV_WIDTH
        q = q * jnp.concatenate([cos] * rep, axis=1) + _swap_pairs(q) * jnp.concatenate([sin] * rep, axis=1)
        k = k * cos + _swap_pairs(k) * sin
    q_scr[...] = (q * (1.0 / math.sqrt(HEAD_DIM))).astype(BF16)
    for g in range(ATT_KV_HEADS):
        gs = slice(g * HEAD_DIM, (g + 1) * HEAD_DIM)
        if rope:
            k_scr[g, 0:past, :] = ck_ref[0, g].astype(BF16)
            v_scr[g, 0:past, :] = cv_ref[0, g].astype(BF16)
        else:
            kh_ref[0, g] = k[:, gs]
            vh_ref[0, g] = v[:, gs]
        k_scr[g, past:past + T, :] = k[:, gs].astype(BF16)
        v_scr[g, past:past + T, :] = v[:, gs].astype(BF16)
    QR = ATT_QROWS

    def q_block(qb, carry):
        rows = pl.ds(pl.multiple_of(qb * QR, QR), QR)
        qblk = q_scr[rows, :]
        for g in range(ATT_KV_HEADS):
            qs = jnp.concatenate([qblk[:, (g * ATT_REP + r) * HEAD_DIM:(g * ATT_REP + r + 1) * HEAD_DIM]
                                  for r in range(ATT_REP)], axis=0)
            s = lax.dot_general(qs, k_scr[g], _NT, preferred_element_type=F32)
            e = jnp.exp(s - jnp.max(s, axis=-1, keepdims=True))
            l = jnp.sum(e, axis=-1, keepdims=True)
            o = lax.dot_general(e.astype(BF16), v_scr[g], _NN, preferred_element_type=F32) / l
            for r in range(ATT_REP):
                h = g * ATT_REP + r
                out_ref[0, rows, h * HEAD_DIM:(h + 1) * HEAD_DIM] = o[r * QR:(r + 1) * QR]
        return carry

    lax.fori_loop(0, T // QR, q_block, 0)


def rope_tables(T):
    rows = T // GRID_W
    row = jnp.repeat(jnp.arange(rows, dtype=F32), GRID_W)
    col = jnp.tile(jnp.arange(GRID_W, dtype=F32), rows)
    n_freq = HEAD_DIM // 4
    inv = ROPE_THETA ** (-jnp.arange(n_freq, dtype=F32) / n_freq)
    ang = jnp.concatenate([row[:, None] * inv, col[:, None] * inv], axis=-1)
    cos = jnp.repeat(jnp.cos(ang), 2, axis=-1)
    sin = jnp.stack([-jnp.sin(ang), jnp.sin(ang)], axis=-1).reshape(T, HEAD_DIM)
    return jnp.tile(cos, (1, ATT_KV_HEADS)), jnp.tile(sin, (1, ATT_KV_HEADS))


def attention_pallas(p_att, qnorm_g, knorm_g, cache=None):
    B, T, W = p_att.shape
    rope = cache is not None
    past = cache[0].shape[2] if rope else 0
    full = lambda shape: pl.BlockSpec(shape, lambda b: (0,) * len(shape))
    qg = jnp.tile(qnorm_g.reshape(1, HEAD_DIM), (1, ATT_HEADS))
    kg = jnp.tile(knorm_g.reshape(1, HEAD_DIM), (1, ATT_KV_HEADS))
    in_specs = [pl.BlockSpec((1, T, W), lambda b: (b, 0, 0)), full((1, ATT_WIDTH)), full((1, KV_WIDTH))]
    args = [p_att, qg, kg]
    out_specs = [pl.BlockSpec((1, T, ATT_WIDTH), lambda b: (b, 0, 0))]
    out_shape = [jax.ShapeDtypeStruct((B, T, ATT_WIDTH), F32)]
    if rope:
        cos, sin = rope_tables(T)
        kv_spec = pl.BlockSpec((1, ATT_KV_HEADS, past, HEAD_DIM), lambda b: (b, 0, 0, 0))
        in_specs += [full((T, KV_WIDTH)), full((T, KV_WIDTH)), kv_spec, kv_spec]
        args += [cos, sin, cache[0], cache[1]]
    else:
        kv_spec = pl.BlockSpec((1, ATT_KV_HEADS, T, HEAD_DIM), lambda b: (b, 0, 0, 0))
        out_specs += [kv_spec, kv_spec]
        out_shape += [jax.ShapeDtypeStruct((B, ATT_KV_HEADS, T, HEAD_DIM), F32)] * 2
    res = pl.pallas_call(
        partial(_att_kernel, T=T, past=past, rope=rope),
        grid=(B,),
        in_specs=in_specs,
        out_specs=out_specs,
        out_shape=out_shape,
        scratch_shapes=[pltpu.VMEM((ATT_KV_HEADS, past + T, HEAD_DIM), BF16)] * 2
        + [pltpu.VMEM((T, ATT_WIDTH), BF16)],
        compiler_params=pltpu.CompilerParams(dimension_semantics=("arbitrary",), vmem_limit_bytes=48 * 1024 * 1024),
        name="attention_rope" if rope else "attention_ctx",
    )(*args)
    return res[0] if rope else tuple(res)


ROW_TILE = 256
MOD_TILE = 1536
ROUTE_TILE = 256
MOE_SLOTS = 256
MOE_WEIGHT_SPLIT = 4
SC_ROWS = 32
SC_LANES = 128
P_ATT, P_RW, P_HG = ATT_WIDTH + 2 * KV_WIDTH, 3 * RW_WIDTH + 384, 5 * HG_WIDTH


def _mod_kernel(c_ref, w_ref, b_ref, o_ref):
    c = c_ref[...]
    o_ref[0] = _mm(c * _sigmoid(c), w_ref[0], passes=3) + b_ref[0]


def adaln_mod_pallas(cvec, w_mod, b_mod):
    n = 6 * D_MODEL
    return pl.pallas_call(
        _mod_kernel,
        grid=(DEPTH, n // MOD_TILE),
        in_specs=[pl.BlockSpec((8, D_MODEL), lambda l, j: (0, 0)),
                  pl.BlockSpec((1, D_MODEL, MOD_TILE), lambda l, j: (l, 0, j)),
                  pl.BlockSpec((1, 1, MOD_TILE), lambda l, j: (l, 0, j))],
        out_specs=pl.BlockSpec((1, 8, MOD_TILE), lambda l, j: (l, 0, j)),
        out_shape=jax.ShapeDtypeStruct((DEPTH, 8, n), F32),
        compiler_params=pltpu.CompilerParams(dimension_semantics=("arbitrary", "arbitrary"),
                                             vmem_limit_bytes=48 * 1024 * 1024),
        name="adaln_mod",
    )(cvec, w_mod, b_mod.reshape(DEPTH, 1, n))


def _rms(x):
    return x * lax.rsqrt(jnp.mean(x * x, axis=-1, keepdims=True) + NORM_EPS)


HALF = D_MODEL // 2
U32 = jnp.uint32


def _pack_rows(x):
    bits = lax.bitcast_convert_type(x.astype(BF16).astype(F32), U32)
    return (bits[:, :HALF] >> 16) | bits[:, HALF:]


def _unpack_rows(w):
    lo = lax.bitcast_convert_type(w << 16, F32)
    hi = lax.bitcast_convert_type(w & jnp.uint32(0xFFFF0000), F32)
    return lo, hi


def _moe_residual(x_ref, g_ref, g4_ref, pm_ref):
    y = None
    for k in range(TOP_K):
        t = g4_ref[:, k:k + 1] * jnp.concatenate(_unpack_rows(g_ref[k]), axis=1)
        y = t if y is None else y + t
    return x_ref[...] + pm_ref[0, 5:6, :] * y


def _in_kernel(*refs, has_res):
    if has_res:
        x_ref, gth_ref, g4_ref, pm_ref, m_ref, g_ref, w_ref, xo_ref, pa_ref, pr_ref, ph_ref = refs
        x = _moe_residual(x_ref, gth_ref, g4_ref, pm_ref)
        xo_ref[...] = x
    else:
        x_ref, m_ref, g_ref, w_ref, pa_ref, pr_ref, ph_ref = refs
        x = x_ref[...]
    h = _rms(x) * g_ref[...] * (1.0 + m_ref[0, 1:2, :]) + m_ref[0, 0:1, :]
    proj = lax.dot_general(h, w_ref[...], _NN, precision=lax.Precision.DEFAULT, preferred_element_type=F32)
    pa_ref[...] = proj[:, 0:P_ATT]
    pr_ref[...] = proj[:, P_ATT:P_ATT + P_RW]
    ph_ref[...] = proj[:, P_ATT + P_RW:]


def _res_specs(row0):
    t0 = row0 // ROW_TILE
    return [pl.BlockSpec((TOP_K, ROW_TILE, HALF), lambda i: (0, i, 0)),
            pl.BlockSpec((ROW_TILE, TOP_K), lambda i: (i + t0, 0))]


def in_proj_pallas(x, mod, norm_g, w_in, rows_per_mod, res=None):
    R = x.shape[0]
    tpm = rows_per_mod // ROW_TILE
    rt = lambda w: pl.BlockSpec((ROW_TILE, w), lambda i: (i, 0))
    ms = pl.BlockSpec((1, 6, D_MODEL), lambda i: (i // tpm, 0, 0))
    full = lambda shape, **kw: pl.BlockSpec(shape, lambda i: (0,) * len(shape), **kw)
    in_specs = [rt(D_MODEL)] + (_res_specs(res[2]) + [ms] if res else []) + [
        ms, full((1, D_MODEL)), full((D_MODEL, w_in.shape[1]), pipeline_mode=pl.Buffered(1))]
    args = [x] + ([res[0], res[1], res[3]] if res else []) + [mod, norm_g.reshape(1, D_MODEL), w_in]
    widths = ([D_MODEL] if res else []) + [P_ATT, P_RW, P_HG]
    return pl.pallas_call(
        partial(_in_kernel, has_res=res is not None),
        grid=(R // ROW_TILE,),
        in_specs=in_specs,
        out_specs=[rt(w) for w in widths],
        out_shape=[jax.ShapeDtypeStruct((R, w), F32) for w in widths],
        compiler_params=pltpu.CompilerParams(dimension_semantics=("arbitrary",), vmem_limit_bytes=48 * 1024 * 1024),
        name="in_proj",
    )(*args)


def _out_kernel(att_ref, rw_ref, hg_ref, x_ref, m_ref, g_ref, w_ref, rw_w_ref, rb_ref, *rest):
    xo_ref, h_ref, lg_ref = rest[-3:]
    d = lambda a, lo, hi: lax.dot_general(a, w_ref[lo:hi, :], _NN, precision=lax.Precision.DEFAULT,
                                          preferred_element_type=F32)
    mixo = (d(att_ref[...], 0, ATT_WIDTH) + d(rw_ref[...], ATT_WIDTH, ATT_WIDTH + RW_WIDTH)
            + d(hg_ref[...], ATT_WIDTH + RW_WIDTH, ATT_WIDTH + RW_WIDTH + HG_WIDTH))
    x = x_ref[...] + m_ref[0, 2:3, :] * mixo
    xo_ref[...] = x
    h = _rms(x) * g_ref[...] * (1.0 + m_ref[0, 4:5, :]) + m_ref[0, 3:4, :]
    h_ref[...] = _pack_rows(h)
    lg_ref[...] = _mm(h, rw_w_ref[...], passes=3) + rb_ref[...]


def out_proj_pallas(att, rw, hg, x, mod, norm_g, w_out, router_w, router_b, rows_per_mod, n_all, row0, joint=None):
    R = x.shape[0]
    tpm = rows_per_mod // ROW_TILE
    t0 = row0 // ROW_TILE
    rt = lambda w: pl.BlockSpec((ROW_TILE, w), lambda i: (i, 0))
    jt = lambda w: pl.BlockSpec((ROW_TILE, w), lambda i: (i + t0, 0))
    full = lambda shape: pl.BlockSpec(shape, lambda i: (0,) * len(shape))
    in_specs = [rt(ATT_WIDTH), rt(RW_WIDTH), rt(HG_WIDTH), rt(D_MODEL),
                pl.BlockSpec((1, 6, D_MODEL), lambda i: (i // tpm, 0, 0)), full((1, D_MODEL)),
                full((D_MODEL, D_MODEL)), full((D_MODEL, N_EXPERTS)), full((1, N_EXPERTS))]
    args = [att, rw, hg, x, mod, norm_g.reshape(1, D_MODEL), w_out, router_w, router_b.reshape(1, N_EXPERTS)]
    aliases = {}
    if joint is not None:
        in_specs += [pl.BlockSpec(memory_space=pl.ANY)] * 2
        aliases = {len(args): 1, len(args) + 1: 2}
        args += list(joint)
    return pl.pallas_call(
        _out_kernel,
        grid=(R // ROW_TILE,),
        in_specs=in_specs,
        out_specs=[rt(D_MODEL), jt(HALF), jt(N_EXPERTS)],
        out_shape=[jax.ShapeDtypeStruct((R, D_MODEL), F32), jax.ShapeDtypeStruct((n_all, HALF), U32),
                   jax.ShapeDtypeStruct((n_all, N_EXPERTS), F32)],
        input_output_aliases=aliases,
        compiler_params=pltpu.CompilerParams(dimension_semantics=("arbitrary",), vmem_limit_bytes=48 * 1024 * 1024),
        name="out_proj",
    )(*args)


def _final_kernel(x_ref, gth_ref, g4_ref, m_ref, g_ref, o_ref):
    o_ref[...] = _rms(_moe_residual(x_ref, gth_ref, g4_ref, m_ref)) * g_ref[...]


def final_norm_pallas(x, gathered, gate4, row0, mod, norm_g, rows_per_mod):
    R = x.shape[0]
    tpm = rows_per_mod // ROW_TILE
    rt = pl.BlockSpec((ROW_TILE, D_MODEL), lambda i: (i, 0))
    return pl.pallas_call(
        _final_kernel,
        grid=(R // ROW_TILE,),
        in_specs=[rt] + _res_specs(row0) + [pl.BlockSpec((1, 6, D_MODEL), lambda i: (i // tpm, 0, 0)),
                                            pl.BlockSpec((1, D_MODEL), lambda i: (0, 0))],
        out_specs=rt,
        out_shape=jax.ShapeDtypeStruct((R, D_MODEL), F32),
        name="final_norm",
    )(x, gathered, gate4, mod, norm_g.reshape(1, D_MODEL))


def _moe_max_blocks(n_tok):
    return (n_tok * TOP_K + N_EXPERTS * (MOE_SLOTS - 1)) // MOE_SLOTS


def _exact_nt_ones(a):
    ones = jnp.ones((8, a.shape[1]), BF16)
    out = None
    for t in reversed(_split(a, 3)):
        y = lax.dot_general(ones, t, _NT, preferred_element_type=F32)
        out = y if out is None else out + y
    return out


def _route_kernel(lg_ref, dest_ref, gate4_ref, blk_ref, rank_scr, gate_scr, *, n_tok):
    Rt, E = ROUTE_TILE, N_EXPERTS
    n_tiles = n_tok // Rt
    lane = lax.broadcasted_iota(jnp.int32, (Rt, E), 1)
    ti = lax.broadcasted_iota(jnp.int32, (Rt, Rt), 0)
    si = lax.broadcasted_iota(jnp.int32, (Rt, Rt), 1)
    lower = (ti > si).astype(BF16)
    ei = lax.broadcasted_iota(jnp.int32, (E, E), 0)
    ej = lax.broadcasted_iota(jnp.int32, (E, E), 1)
    upper_e = (ei < ej).astype(BF16)

    def tile_members(it, off):
        rows = pl.ds(pl.multiple_of(it * Rt, Rt), Rt)
        lg = lg_ref[rows, :]
        work = lg
        member = jnp.zeros((Rt, E), jnp.bool_)
        top = None
        for k in range(TOP_K):
            m = jnp.max(work, axis=-1, keepdims=True)
            if top is None:
                top = m
            first = jnp.min(jnp.where(work == m, lane, E), axis=-1, keepdims=True)
            pick = lane == first
            member = member | pick
            work = jnp.where(pick, -jnp.inf, work)
        ex = jnp.where(member, jnp.exp(lg - top), 0.0)
        gate_scr[rows, :] = ex / jnp.sum(ex, axis=-1, keepdims=True)
        mem = member.astype(BF16)
        rank = lax.dot_general(lower, mem, _NN, preferred_element_type=F32) + off
        rank_scr[rows, :] = jnp.where(member, rank, -1.0)
        return off + jnp.sum(mem.astype(F32), axis=0, keepdims=True)

    count = lax.fori_loop(0, n_tiles, tile_members, jnp.zeros((1, E), F32))
    nblk = jnp.floor((count + (MOE_SLOTS - 1)) * (1.0 / MOE_SLOTS))
    bstart = lax.dot_general(nblk.astype(BF16), upper_e, _NN, preferred_element_type=F32)
    bend = bstart + nblk
    pstart = bstart * MOE_SLOTS

    def tile_slots(it, carry):
        rows = pl.ds(pl.multiple_of(it * Rt, Rt), Rt)
        rank = rank_scr[rows, :]
        gate = gate_scr[rows, :]
        member = rank >= 0.0
        kidx = lax.dot_general(member.astype(BF16), upper_e, _NN, preferred_element_type=F32)
        slot = pstart + rank
        cols = []
        for k in range(TOP_K):
            sel = member & (kidx == k)
            dest_ref[k:k + 1, rows] = _exact_nt_ones(jnp.where(sel, slot, 0.0))[0:1].astype(jnp.int32)
            cols.append(jnp.sum(jnp.where(sel, gate, 0.0), axis=-1, keepdims=True))
        gate4_ref[rows, :] = jnp.concatenate(cols, axis=1)
        return carry

    lax.fori_loop(0, n_tiles, tile_slots, 0)
    nb = blk_ref.shape[1]
    bi = lax.broadcasted_iota(jnp.int32, (E, nb), 1).astype(F32)
    bend_col = jnp.sum(jnp.where(ei == ej, jnp.broadcast_to(bend, (E, E)), 0.0), axis=1, keepdims=True)
    owner = jnp.sum((bend_col <= bi).astype(F32), axis=0, keepdims=True)
    blk_ref[0:1, :] = jnp.minimum(owner, E - 1.0).astype(jnp.int32)
    blk_ref[1:2, :] = jnp.broadcast_to(jnp.sum(nblk, axis=-1, keepdims=True), (1, nb)).astype(jnp.int32)
    blk_ref[2:8, :] = jnp.zeros((6, nb), jnp.int32)


def moe_route_pallas(logits):
    n_tok = logits.shape[0]
    nb = -(-_moe_max_blocks(n_tok) // 128) * 128
    return pl.pallas_call(
        partial(_route_kernel, n_tok=n_tok),
        out_shape=[jax.ShapeDtypeStruct((TOP_K, n_tok), jnp.int32),
                   jax.ShapeDtypeStruct((n_tok, TOP_K), F32),
                   jax.ShapeDtypeStruct((8, nb), jnp.int32)],
        scratch_shapes=[pltpu.VMEM((n_tok, N_EXPERTS), F32)] * 2,
        name="moe_route",
    )(logits)


def _moe_block_kernel(be_ref, nu_ref, first_ref, par_ref, nxt_ref, xb_ref, wgu_hbm, bgu_ref, wdn_hbm, bdn_ref, yb_ref,
                      wgu_buf, wdn_buf, sem, *, l):
    i = pl.program_id(0)

    def weight_copies(e, slot):
        cps = []
        for hbm, buf, s in ((wgu_hbm, wgu_buf, 0), (wdn_hbm, wdn_buf, 1)):
            rows = buf.shape[1] // MOE_WEIGHT_SPLIT
            for j in range(MOE_WEIGHT_SPLIT):
                rs = pl.ds(j * rows, rows)
                cps.append(pltpu.make_async_copy(hbm.at[l, e, rs], buf.at[slot, rs], sem.at[s, slot]))
        return cps

    @pl.when(i < nu_ref[0])
    def _():
        slot = par_ref[i]

        @pl.when(first_ref[i] == 1)
        def _():
            @pl.when(i == 0)
            def _():
                for cp in weight_copies(be_ref[0], slot):
                    cp.start()

            for cp in weight_copies(be_ref[i], slot):
                cp.wait()

            @pl.when(nxt_ref[i] >= 0)
            def _():
                for cp in weight_copies(nxt_ref[i], 1 - slot):
                    cp.start()

        dot = lambda a, w: lax.dot_general(a, w, _NN, precision=lax.Precision.DEFAULT, preferred_element_type=F32)
        x_lo, x_hi = _unpack_rows(xb_ref[...])
        gu = dot(x_lo, wgu_buf[slot, 0:HALF]) + dot(x_hi, wgu_buf[slot, HALF:D_MODEL]) + bgu_ref[0, 0]
        glu = jnp.minimum(gu[:, :EXPERT_FF], SWIGLU_LIMIT)
        lin = jnp.clip(gu[:, EXPERT_FF:], -SWIGLU_LIMIT, SWIGLU_LIMIT)
        act = glu * _sigmoid(SWIGLU_ALPHA * glu) * (lin + 1.0)
        yb_ref[...] = _pack_rows(dot(act, wdn_buf[slot]) + bdn_ref[0, 0])


def _expert_runs(block_e, n_used):
    n = block_e.shape[0]
    idx = jnp.arange(n, dtype=jnp.int32)
    valid = idx < n_used[0]
    first = valid & ((idx == 0) | (block_e != jnp.roll(block_e, 1)))
    par = (jnp.cumsum(first.astype(jnp.int32)) - 1) % 2
    start = jnp.where(first, idx, n)
    nxt_start = lax.cummin(jnp.concatenate([start[1:], jnp.full((1,), n, jnp.int32)]), reverse=True)
    nxt = jnp.where(nxt_start < n, block_e[jnp.minimum(nxt_start, n - 1)], -1)
    return first.astype(jnp.int32), par.astype(jnp.int32), nxt.astype(jnp.int32)


def moe_blocks_pallas(xb, block_e, n_used, l, w_gu, b_gu, w_down, b_down):
    n_blocks = xb.shape[0] // MOE_SLOTS
    first, par, nxt = _expert_runs(block_e, n_used)
    blk = lambda i, be, nu, *_: (jnp.minimum(i, nu[0] - 1), 0)
    bsel = lambda i, be, nu, *_: (l, be[jnp.minimum(i, nu[0] - 1)], 0, 0)
    grid_spec = pltpu.PrefetchScalarGridSpec(
        num_scalar_prefetch=5,
        grid=(n_blocks,),
        in_specs=[pl.BlockSpec((MOE_SLOTS, HALF), blk),
                  pl.BlockSpec(memory_space=pl.ANY),
                  pl.BlockSpec((1, 1, 1, 2 * EXPERT_FF), bsel),
                  pl.BlockSpec(memory_space=pl.ANY),
                  pl.BlockSpec((1, 1, 1, D_MODEL), bsel)],
        out_specs=pl.BlockSpec((MOE_SLOTS, HALF), blk),
        scratch_shapes=[pltpu.VMEM((2, D_MODEL, 2 * EXPERT_FF), F32), pltpu.VMEM((2, EXPERT_FF, D_MODEL), F32),
                        pltpu.SemaphoreType.DMA((2, 2))],
    )
    return pl.pallas_call(
        partial(_moe_block_kernel, l=l),
        grid_spec=grid_spec,
        out_shape=jax.ShapeDtypeStruct(xb.shape, U32),
        compiler_params=pltpu.CompilerParams(dimension_semantics=("arbitrary",), vmem_limit_bytes=48 * 1024 * 1024),
        name="moe_blocks",
    )(block_e, n_used, first, par, nxt, xb, w_gu, b_gu.reshape(DEPTH, N_EXPERTS, 1, 2 * EXPERT_FF), w_down,
      b_down.reshape(DEPTH, N_EXPERTS, 1, D_MODEL))


def _sc_mesh():
    return plsc.VectorSubcoreMesh(core_axis_name="c", subcore_axis_name="s")


def _sc_index_rows(idx):
    return jnp.pad(idx.reshape(-1, SC_ROWS), ((0, 0), (0, SC_LANES - SC_ROWS)))


def sc_dispatch(h, dest, n_rows):
    n_tok, d = h.shape
    idx = [_sc_index_rows(dest[k]) for k in range(TOP_K)]

    @pl.kernel(out_type=jax.ShapeDtypeStruct((n_rows, d), h.dtype), mesh=_sc_mesh(), scratch_types=[])
    def kern(h_hbm, i0, i1, i2, i3, o_hbm):
        def body(x_vmem, *i_vmem):
            for iv in i_vmem:
                pltpu.sync_copy(x_vmem, o_hbm.at[iv.at[0, pl.ds(0, SC_ROWS)]])

        pltpu.emit_pipeline(
            body,
            grid=(n_tok // SC_ROWS,),
            in_specs=[pl.BlockSpec((SC_ROWS, d), lambda i: (i, 0))]
            + [pl.BlockSpec((1, SC_LANES), lambda i: (i, 0))] * TOP_K,
            out_specs=[],
            core_axis_name=("c", "s"),
            dimension_semantics=(pltpu.PARALLEL,),
        )(h_hbm, i0, i1, i2, i3)

    return kern(h, *idx)


def sc_combine_gather(yb, dest):
    n_tok = dest.shape[1]
    d = yb.shape[1]
    idx = _sc_index_rows(dest.reshape(TOP_K * n_tok))

    @pl.kernel(out_type=jax.ShapeDtypeStruct((TOP_K * n_tok, d), yb.dtype), mesh=_sc_mesh(), scratch_types=[])
    def kern(y_hbm, i_hbm, o_hbm):
        def body(i_vmem, o_vmem):
            pltpu.sync_copy(y_hbm.at[i_vmem.at[0, pl.ds(0, SC_ROWS)]], o_vmem)

        pltpu.emit_pipeline(
            body,
            grid=(TOP_K * n_tok // SC_ROWS,),
            in_specs=[pl.BlockSpec((1, SC_LANES), lambda i: (i, 0))],
            out_specs=[pl.BlockSpec((SC_ROWS, d), lambda i: (i, 0))],
            core_axis_name=("c", "s"),
            dimension_semantics=(pltpu.PARALLEL,),
        )(i_hbm, o_hbm)

    return kern(yb, idx).reshape(TOP_K, n_tok, d)


def hgrn_lower_bounds(hg_lb):
    sm = jax.nn.softmax(hg_lb.astype(jnp.float32), axis=0)
    return jnp.cumsum(sm, axis=0) - sm[0:1]


def kernel(x_prompt, x_sample, cache_att_k, cache_att_v, state_rwkv, state_hgrn, c, c_ctx, w_mod, b_mod, norm_mix_g, norm_ffn_g, w_in, w_out, att_qnorm_g, att_knorm_g, rw_w0, rw_w2, rw_a0, rw_a2, rw_g2, rw_kk, rw_ka, rw_rk, rw_gn_g, rw_gn_b, hg_lb, hg_norm_g, router_w, router_b, moe_w_gu, moe_b_gu, moe_w_down, moe_b_down, final_norm_g):
    BP, TP, _ = x_prompt.shape
    BS, TS, _ = x_sample.shape
    n_p, n_s = BP * TP, BS * TS
    lb_all = hgrn_lower_bounds(hg_lb)
    cvec = jnp.concatenate([c_ctx[None, :], c, jnp.zeros((8 - 1 - BS, D_MODEL), F32)], axis=0)
    mod_all = adaln_mod_pallas(cvec, w_mod, b_mod).reshape(DEPTH, 8, 6, D_MODEL)
    zeros_state = jnp.zeros((BP, 2, RW_HEADS, HEAD_DIM, HEAD_DIM), F32)
    x = {'p': x_prompt.reshape(n_p, D_MODEL), 's': x_sample.reshape(n_s, D_MODEL)}
    dims = {'p': (TP, BP, n_p, 0), 's': (TS, BS, TS, n_p)}
    moe_out, mod_prev = None, None
    ks, vs, srs, shs = [], [], [], []
    for l in range(DEPTH):
        prm = dict(rw_w0=rw_w0[l], rw_w2=rw_w2[l], rw_a0=rw_a0[l], rw_a2=rw_a2[l], rw_g2=rw_g2[l],
                   rw_kk=rw_kk[l], rw_ka=rw_ka[l], rw_rk=rw_rk[l], rw_gn_g=rw_gn_g[l], rw_gn_b=rw_gn_b[l])
        mods = {'p': mod_all[l, 0:1], 's': mod_all[l, 1:1 + BS]}
        joint = (jnp.zeros((n_p + n_s, HALF), U32), jnp.zeros((n_p + n_s, N_EXPERTS), F32))
        for s in ('p', 's'):
            T, B, rpm, row0 = dims[s]
            if l == 0:
                p_att, p_rw, p_hg = in_proj_pallas(x[s], mods[s], norm_mix_g[l], w_in[l], rpm)
            else:
                x[s], p_att, p_rw, p_hg = in_proj_pallas(x[s], mods[s], norm_mix_g[l], w_in[l], rpm,
                                                         res=(*moe_out[s], row0, mod_prev[s]))
            p_att, p_rw, p_hg = (t.reshape(B, T, -1) for t in (p_att, p_rw, p_hg))
            if s == 'p':
                att, k_l, v_l = attention_pallas(p_att, att_qnorm_g[l], att_knorm_g[l])
                rw_out, sr_l = rwkv7_mixer_pallas(p_rw, zeros_state, prm)
                hg_out, sh_l = hgrn2_mixer_pallas(p_hg, zeros_state, lb_all[l], hg_norm_g[l])
                ks.append(k_l)
                vs.append(v_l)
                srs.append(sr_l)
                shs.append(sh_l)
            else:
                att = attention_pallas(p_att, att_qnorm_g[l], att_knorm_g[l], (cache_att_k[:, l], cache_att_v[:, l]))
                rw_out, _ = rwkv7_mixer_pallas(p_rw, state_rwkv[:, l], prm)
                hg_out, _ = hgrn2_mixer_pallas(p_hg, state_hgrn[:, l], lb_all[l], hg_norm_g[l])
            x[s], *joint = out_proj_pallas(att.reshape(B * T, -1), rw_out.reshape(B * T, -1),
                                           hg_out.reshape(B * T, -1), x[s], mods[s], norm_ffn_g[l], w_out[l],
                                           router_w[l], router_b[l], rpm, n_p + n_s, row0, joint)
        h_all, logits_all = joint
        dest, gate4, blk = moe_route_pallas(logits_all)
        xb = sc_dispatch(h_all, dest, _moe_max_blocks(n_p + n_s) * MOE_SLOTS)
        yb = moe_blocks_pallas(xb, blk[0], blk[1, :1], l, moe_w_gu, moe_b_gu, moe_w_down, moe_b_down)
        moe_out = {'p': (sc_combine_gather(yb, dest[:, :n_p]), gate4), 's': (sc_combine_gather(yb, dest[:, n_p:]), gate4)}
        mod_prev = mods
    y_prompt = final_norm_pallas(x['p'], *moe_out['p'], 0, mod_prev['p'], final_norm_g, n_p)
    y_sample = final_norm_pallas(x['s'], *moe_out['s'], n_p, mod_prev['s'], final_norm_g, TS)
    return (y_prompt.reshape(x_prompt.shape), y_sample.reshape(x_sample.shape), jnp.stack(ks, axis=1),
            jnp.stack(vs, axis=1), jnp.stack(srs, axis=1), jnp.stack(shs, axis=1))
```

```python
import math
from functools import partial

import jax
import jax.numpy as jnp
from jax import lax
from jax.experimental import pallas as pl
from jax.experimental.pallas import tpu as pltpu
from jax.experimental.pallas import tpu_sc as plsc

D_MODEL = 1024
DEPTH = 2
GRID_W = 64
HEAD_DIM = 64
ATT_HEADS = 8
ATT_KV_HEADS = 2
ATT_WIDTH = ATT_HEADS * HEAD_DIM
KV_WIDTH = ATT_KV_HEADS * HEAD_DIM
RW_HEADS = 4
RW_WIDTH = RW_HEADS * HEAD_DIM
RW_GN_EPS = 64e-5
HG_HEADS = 4
HG_WIDTH = HG_HEADS * HEAD_DIM
HG_F_MIN = 1e-6
N_EXPERTS = 32
TOP_K = 4
EXPERT_FF = D_MODEL
SWIGLU_LIMIT = 7.0
SWIGLU_ALPHA = 1.702
ROPE_THETA = 10000.0
NORM_EPS = 1e-6

RW_CHUNK = 64
BF16 = jnp.bfloat16
F32 = jnp.float32

_NN = (((1,), (0,)), ((), ()))
_NT = (((1,), (1,)), ((), ()))
_TN = (((0,), (0,)), ((), ()))


def _split(x, n):
    parts = []
    for _ in range(n - 1):
        hi = x.astype(BF16)
        parts.append(hi)
        x = x - hi.astype(F32)
    parts.append(x.astype(BF16))
    return parts


def _mm(a, b, dims=_NN, passes=1):
    d = lambda x, y: lax.dot_general(x, y, dims, preferred_element_type=F32)
    if passes == 1:
        return d(a.astype(BF16), b.astype(BF16))
    ah, al = _split(a, 2)
    bh, bl = _split(b, 2)
    return d(ah, bl) + d(al, bh) + d(ah, bh)


def _mm_exact_lhs(a01, b, n=3):
    a = a01.astype(BF16)
    out = None
    for t in reversed(_split(b, n)):
        y = lax.dot_general(a, t, _NN, preferred_element_type=F32)
        out = y if out is None else out + y
    return out


def _mm_exact_rhs(a, b01, n=3):
    b = b01.astype(BF16)
    out = None
    for t in reversed(_split(a, n)):
        y = lax.dot_general(t, b, _NN, preferred_element_type=F32)
        out = y if out is None else out + y
    return out


def _head_blockdiag(width):
    r = lax.broadcasted_iota(jnp.int32, (width, width), 0) // HEAD_DIM
    c = lax.broadcasted_iota(jnp.int32, (width, width), 1) // HEAD_DIM
    return (r == c).astype(F32)


def _sigmoid(x):
    return 1.0 / (1.0 + jnp.exp(-x))


def _softplus(x):
    return jnp.maximum(x, 0.0) + jnp.log(1.0 + jnp.exp(-jnp.abs(x)))


def _rwkv_kernel(p_ref, s0_ref, w0_ref, w2_ref, a0_ref, a2_ref, g2_ref, kk_ref, ka_ref, rk_ref, gng_ref, gnb_ref,
                 *rest, T, NB):
    out_ref, st_ref, lw_scr, kd_scr, bb_scr, y_scr, kk_scr, s_scr = rest[-8:]
    C = RW_CHUNK
    n_chunks = T // C
    bd = _head_blockdiag(RW_WIDTH)
    seg = lambda t: _mm_exact_rhs(t, bd)
    ka = ka_ref[...]
    for nb in range(NB):
        k = p_ref[nb, :, 256:512]
        kk = k * kk_ref[...]
        kk = kk * lax.rsqrt(seg(kk * kk) + 1e-12)
        kk_scr[nb] = kk
        for d in range(2):
            wd = p_ref[nb, :, 768 + 64 * d:832 + 64 * d]
            ad = p_ref[nb, :, 896 + 64 * d:960 + 64 * d]
            w_raw = w0_ref[d:d + 1, :] + _mm(jnp.tanh(wd), w2_ref[d], passes=3)
            lw_scr[nb, d] = -jnp.exp(-_softplus(-w_raw) - 0.5)
            a = _sigmoid(a0_ref[d:d + 1, :] + _mm(ad, a2_ref[d], passes=3))
            kd_scr[nb, d] = k * (1.0 + (a - 1.0) * ka)
            bb_scr[nb, d] = kk * a
    s_scr[...] = s0_ref[:, 0]

    ti = lax.broadcasted_iota(jnp.int32, (C, C), 0)
    si = lax.broadcasted_iota(jnp.int32, (C, C), 1)
    ones_cc = jnp.ones((C, C), F32)

    def chunk_body(i, carry):
        ch = []
        for nb, d in [(nb, d) for nb in range(NB) for d in range(2)]:
            ci = i if d == 0 else n_chunks - 1 - i
            rows = pl.ds(pl.multiple_of(ci * C, C), C)
            strict = (ti > si) if d == 0 else (ti < si)
            incl = (ti >= si) if d == 0 else (ti <= si)
            lw = lw_scr[nb, d, rows, :]
            both = _mm_exact_lhs(jnp.concatenate([incl.astype(F32), ones_cc], axis=0), lw)
            cum, total = both[:C], both[C:]
            cum_ex = cum - lw
            mid = 0.5 * total
            rr = p_ref[nb, rows, 0:256]
            vv = p_ref[nb, rows, 512:768]
            kdc = kd_scr[nb, d, rows, :]
            bbc = bb_scr[nb, d, rows, :]
            kkc = kk_scr[nb, rows, :]
            e_inv = jnp.exp(mid - cum)
            At = -kkc * jnp.exp(cum_ex - mid)
            Rt = rr * jnp.exp(cum - mid)
            Bt = bbc * e_inv
            Kt = kdc * e_inv
            Ap = -kkc * jnp.exp(cum_ex)
            Rp = rr * jnp.exp(cum)
            e_out = jnp.exp(total - cum)
            Bh = bbc * e_out
            Kh = kdc * e_out
            e_tot = jnp.exp(total[0:1, :])
            for h in range(RW_HEADS):
                hs = slice(h * HEAD_DIM, (h + 1) * HEAD_DIM)
                ch.append(dict(nb=nb, d=d, h=h, rows=rows, hs=hs, strict=strict, incl=incl,
                               AR=jnp.concatenate([At[:, hs], Rt[:, hs]], axis=0), Bt=Bt[:, hs], Kt=Kt[:, hs],
                               V=vv[:, hs], X1=Ap[:, hs], Rp=Rp[:, hs], Bh=Bh[:, hs], Kh=Kh[:, hs],
                               e_tot=e_tot[:, hs]))
        for c in ch:
            c['AB'] = _mm(c['AR'], c['Bt'], _NT)
            c['AK'] = _mm(c['AR'], c['Kt'], _NT)
        for c in ch:
            c['P'] = jnp.where(c['strict'], c['AB'][:C], 0.0)
            c['A_ak'] = jnp.where(c['strict'], c['AK'][:C], 0.0)
            c['A_rb'] = jnp.where(c['incl'], c['AB'][C:], 0.0)
            c['A_rk'] = jnp.where(c['incl'], c['AK'][C:], 0.0)
        for c in ch:
            c['X2'] = _mm(c['A_ak'], c['V'])
        for lvl in range(6):
            for c in ch:
                if lvl < 5:
                    c['PZ'] = _mm(c['P'], jnp.concatenate([c['P'], c['X1'], c['X2']], axis=1))
                else:
                    c['PZ'] = _mm(c['P'], jnp.concatenate([c['X1'], c['X2']], axis=1))
            for c in ch:
                PZ = c['PZ']
                if lvl < 5:
                    c['P'] = PZ[:, :C]
                    c['X1'] = c['X1'] + PZ[:, C:2 * C]
                    c['X2'] = c['X2'] + PZ[:, 2 * C:]
                else:
                    c['X1'] = c['X1'] + PZ[:, :C]
                    c['X2'] = c['X2'] + PZ[:, C:]
        for c in ch:
            c['S0'] = s_scr[c['nb'], c['d'], c['h']]
            c['UY'] = _mm(jnp.concatenate([c['X1'], c['Rp']], axis=0), c['S0'], _NT)
        for c in ch:
            c['U'] = c['UY'][:C] + c['X2']
        for c in ch:
            c['Y'] = c['UY'][C:] + _mm(c['A_rb'], c['U']) + _mm(c['A_rk'], c['V'])
            c['S1'] = c['S0'] * c['e_tot'] + _mm(c['U'], c['Bh'], _TN) + _mm(c['V'], c['Kh'], _TN)
        for c in ch:
            s_scr[c['nb'], c['d'], c['h']] = c['S1']
            y_scr[c['nb'], c['d'], c['rows'], c['hs']] = c['Y']
        return carry

    lax.fori_loop(0, n_chunks, chunk_body, 0)

    for nb in range(NB):
        r = p_ref[nb, :, 0:256]
        v = p_ref[nb, :, 512:768]
        bonus = seg(r * (kd_scr[nb, 0] + kd_scr[nb, 1]) * rk_ref[...]) * v
        g = _mm(_sigmoid(p_ref[nb, :, 1024:1152]), g2_ref[...], passes=3)
        y = y_scr[nb, 0] + y_scr[nb, 1]
        mu = seg(y) * (1.0 / HEAD_DIM)
        yc = y - mu
        var = seg(yc * yc) * (1.0 / HEAD_DIM)
        yn = yc * lax.rsqrt(var + RW_GN_EPS)
        out_ref[nb] = (yn * gng_ref[...] + gnb_ref[...] + bonus) * g
    st_ref[:, 0] = s_scr[...]


RW_ROWS = 1024


def _state_spec(nb, layer):
    return pl.BlockSpec((nb, 1, 2, RW_HEADS, HEAD_DIM, HEAD_DIM), lambda b: (b, layer, 0, 0, 0, 0))


def rwkv7_mixer_pallas(p_rw, s0, l_in, prm, st_all=None, l_out=0):
    B, T, W = p_rw.shape
    NB = max(2, RW_ROWS // T)
    row = lambda a: a.reshape(1, RW_WIDTH)
    full = lambda shape: pl.BlockSpec(shape, lambda b: (0,) * len(shape))
    st_shape = (B, 1, 2, RW_HEADS, HEAD_DIM, HEAD_DIM) if st_all is None else st_all.shape
    return pl.pallas_call(
        partial(_rwkv_kernel, T=T, NB=NB),
        grid=(B // NB,),
        in_specs=[pl.BlockSpec((NB, T, W), lambda b: (b, 0, 0)), _state_spec(NB, l_in),
                  full((2, RW_WIDTH)), full((2, 64, RW_WIDTH)), full((2, RW_WIDTH)), full((2, 64, RW_WIDTH)),
                  full((128, RW_WIDTH)), full((1, RW_WIDTH)), full((1, RW_WIDTH)), full((1, RW_WIDTH)),
                  full((1, RW_WIDTH)), full((1, RW_WIDTH))]
        + ([] if st_all is None else [pl.BlockSpec(memory_space=pl.ANY)]),
        out_specs=[pl.BlockSpec((NB, T, RW_WIDTH), lambda b: (b, 0, 0)), _state_spec(NB, l_out)],
        out_shape=[jax.ShapeDtypeStruct((B, T, RW_WIDTH), F32), jax.ShapeDtypeStruct(st_shape, F32)],
        input_output_aliases={} if st_all is None else {12: 1},
        scratch_shapes=[pltpu.VMEM((NB, 2, T, RW_WIDTH), F32)] * 4
        + [pltpu.VMEM((NB, T, RW_WIDTH), F32), pltpu.VMEM((NB, 2, RW_HEADS, HEAD_DIM, HEAD_DIM), F32)],
        compiler_params=pltpu.CompilerParams(dimension_semantics=("arbitrary",), vmem_limit_bytes=56 * 1024 * 1024),
        name="rwkv7_mixer",
    )(p_rw, s0, prm['rw_w0'], prm['rw_w2'], prm['rw_a0'], prm['rw_a2'], prm['rw_g2'], row(prm['rw_kk']),
      row(prm['rw_ka']), row(prm['rw_rk']), row(prm['rw_gn_g']), row(prm['rw_gn_b']),
      *([] if st_all is None else [st_all]))


HG_SUB = 16
HG_ROWS = 64


def _hgrn_kernel(p_ref, s0_ref, lb_ref, ng_ref, *rest, T):
    out_ref, st_ref, lf_scr, kf_scr, o_scr, s_scr = rest[-6:]
    R, c = HG_ROWS, HG_SUB
    n_it = T // R
    x = p_ref[0]
    bd = _head_blockdiag(HG_WIDTH)
    seg = lambda t: _mm_exact_rhs(t, bd)
    for d in range(2):
        lbd = lb_ref[d:d + 1, :]
        f = lbd + (1.0 - lbd) * _sigmoid(x[:, 256 + 256 * d:512 + 256 * d])
        lf_scr[d] = jnp.log(jnp.maximum(f, HG_F_MIN))
        kf_scr[d] = 1.0 - f
        for h in range(HG_HEADS):
            s_scr[d, h] = s0_ref[0, 0, d, h].T

    ti = lax.broadcasted_iota(jnp.int32, (R, R), 0)
    si = lax.broadcasted_iota(jnp.int32, (R, R), 1)
    same_blk = (ti // c) == (si // c)
    t16 = lax.broadcasted_iota(jnp.int32, (c, 1), 0)

    def body(i, carry):
        for d in range(2):
            ci = i if d == 0 else n_it - 1 - i
            rows = pl.ds(pl.multiple_of(ci * R, R), R)
            incl = (ti >= si) if d == 0 else (ti <= si)
            lf = lf_scr[d, rows, :]
            both = _mm_exact_lhs(jnp.concatenate([(incl & same_blk).astype(F32), same_blk.astype(F32)], axis=0), lf)
            cum, tot = both[:R], both[R:]
            xq = p_ref[0, rows, 0:256]
            q = xq * _sigmoid(xq)
            v = p_ref[0, rows, 768:1024]
            kf = kf_scr[d, rows, :]
            Qp = q * jnp.exp(cum)
            Kh = kf * jnp.exp(tot - cum)
            e_tot = jnp.exp(tot)
            blocks = range(R // c) if d == 0 else range(R // c - 1, -1, -1)
            o_parts = [None] * (R // c)
            for j in blocks:
                rs = slice(j * c, (j + 1) * c)
                cb, qb, kb, vb = cum[rs], q[rs], kf[rs], v[rs]
                prods = []
                for s in range(c):
                    e = jnp.exp(jnp.minimum(cb - cb[s:s + 1, :], 0.0))
                    prods.append(qb * (kb[s:s + 1, :] * e))
                att = _mm_exact_rhs(jnp.concatenate(prods, axis=0), bd, n=1)
                o_blk = jnp.zeros((c, HG_WIDTH), F32)
                for s in range(c):
                    keep = (t16 >= s) if d == 0 else (t16 <= s)
                    o_blk = o_blk + jnp.where(keep, att[s * c:(s + 1) * c], 0.0) * vb[s:s + 1, :]
                o_heads = []
                for h in range(HG_HEADS):
                    hs = slice(h * HEAD_DIM, (h + 1) * HEAD_DIM)
                    ST = s_scr[d, h]
                    o_heads.append(_mm(Qp[rs, hs], ST, _NT))
                    s_scr[d, h] = ST * e_tot[j * c:j * c + 1, hs] + _mm(vb[:, hs], Kh[rs, hs], _TN)
                o_parts[j] = o_blk + jnp.concatenate(o_heads, axis=1)
            o_scr[d, rows, :] = jnp.concatenate(o_parts, axis=0)
        return carry

    lax.fori_loop(0, n_it, body, 0)

    o = o_scr[0] + o_scr[1]
    o = o * lax.rsqrt(seg(o * o) * (1.0 / HEAD_DIM) + NORM_EPS) * ng_ref[...]
    gg = x[:, 1024:1280]
    out_ref[0] = o * (gg * _sigmoid(gg))
    for d in range(2):
        for h in range(HG_HEADS):
            st_ref[0, 0, d, h] = s_scr[d, h].T


def hgrn2_mixer_pallas(p_hg, s0, l_in, lb, norm_g, st_all=None, l_out=0):
    B, T, W = p_hg.shape
    full = lambda shape: pl.BlockSpec(shape, lambda b: (0,) * len(shape))
    st_shape = (B, 1, 2, HG_HEADS, HEAD_DIM, HEAD_DIM) if st_all is None else st_all.shape
    return pl.pallas_call(
        partial(_hgrn_kernel, T=T),
        grid=(B,),
        in_specs=[pl.BlockSpec((1, T, W), lambda b: (b, 0, 0)), _state_spec(1, l_in), full((2, HG_WIDTH)),
                  full((1, HG_WIDTH))] + ([] if st_all is None else [pl.BlockSpec(memory_space=pl.ANY)]),
        out_specs=[pl.BlockSpec((1, T, HG_WIDTH), lambda b: (b, 0, 0)), _state_spec(1, l_out)],
        out_shape=[jax.ShapeDtypeStruct((B, T, HG_WIDTH), F32), jax.ShapeDtypeStruct(st_shape, F32)],
        input_output_aliases={} if st_all is None else {4: 1},
        scratch_shapes=[pltpu.VMEM((2, T, HG_WIDTH), F32)] * 3
        + [pltpu.VMEM((2, HG_HEADS, HEAD_DIM, HEAD_DIM), F32)],
        compiler_params=pltpu.CompilerParams(dimension_semantics=("arbitrary",), vmem_limit_bytes=48 * 1024 * 1024),
        name="hgrn2_mixer",
    )(p_hg, s0, lb, jnp.tile(norm_g.reshape(1, HEAD_DIM), (1, HG_HEADS)), *([] if st_all is None else [st_all]))


ATT_REP = ATT_HEADS // ATT_KV_HEADS
ATT_QROWS = 128


def _swap_pairs(x):
    w = x.shape[-1]
    lane = lax.broadcasted_iota(jnp.int32, x.shape, x.ndim - 1)
    return jnp.where(lane % 2 == 0, pltpu.roll(x, w - 1, x.ndim - 1), pltpu.roll(x, 1, x.ndim - 1))


def _att_kernel(*refs, T, past, rope):
    if rope:
        p_ref, qg_ref, kg_ref, cos_ref, sin_ref, ck_ref, cv_ref, out_ref, k_scr, v_scr, q_scr = refs
    else:
        p_ref, qg_ref, kg_ref, _, _, out_ref, kh_ref, vh_ref, k_scr, v_scr, q_scr = refs
    x = p_ref[0]
    q = x[:, 0:ATT_WIDTH]
    k = x[:, ATT_WIDTH:ATT_WIDTH + KV_WIDTH]
    v = x[:, ATT_WIDTH + KV_WIDTH:ATT_WIDTH + 2 * KV_WIDTH]
    inv_d = 1.0 / HEAD_DIM
    q = q * lax.rsqrt(_mm_exact_rhs(q * q, _head_blockdiag(ATT_WIDTH)) * inv_d + NORM_EPS) * qg_ref[...]
    k = k * lax.rsqrt(_mm_exact_rhs(k * k, _head_blockdiag(KV_WIDTH)) * inv_d + NORM_EPS) * kg_ref[...]
    if rope:
        cos, sin = cos_ref[...], sin_ref[...]
        rep = ATT_WIDTH // KV_WIDTH
        q = q * jnp.concatenate([cos] * rep, axis=1) + _swap_pairs(q) * jnp.concatenate([sin] * rep, axis=1)
        k = k * cos + _swap_pairs(k) * sin
    q_scr[...] = (q * (1.0 / math.sqrt(HEAD_DIM))).astype(BF16)
    for g in range(ATT_KV_HEADS):
        gs = slice(g * HEAD_DIM, (g + 1) * HEAD_DIM)
        if rope:
            k_scr[g, 0:past, :] = ck_ref[0, 0, g].astype(BF16)
            v_scr[g, 0:past, :] = cv_ref[0, 0, g].astype(BF16)
        else:
            kh_ref[0, 0, g] = k[:, gs]
            vh_ref[0, 0, g] = v[:, gs]
        k_scr[g, past:past + T, :] = k[:, gs].astype(BF16)
        v_scr[g, past:past + T, :] = v[:, gs].astype(BF16)
    QR = ATT_QROWS

    def q_block(qb, carry):
        rows = pl.ds(pl.multiple_of(qb * QR, QR), QR)
        qblk = q_scr[rows, :]
        for g in range(ATT_KV_HEADS):
            qs = jnp.concatenate([qblk[:, (g * ATT_REP + r) * HEAD_DIM:(g * ATT_REP + r + 1) * HEAD_DIM]
                                  for r in range(ATT_REP)], axis=0)
            s = lax.dot_general(qs, k_scr[g], _NT, preferred_element_type=F32)
            e = jnp.exp(s - jnp.max(s, axis=-1, keepdims=True))
            l = jnp.sum(e, axis=-1, keepdims=True)
            o = lax.dot_general(e.astype(BF16), v_scr[g], _NN, preferred_element_type=F32) / l
            for r in range(ATT_REP):
                h = g * ATT_REP + r
                out_ref[0, rows, h * HEAD_DIM:(h + 1) * HEAD_DIM] = o[r * QR:(r + 1) * QR]
        return carry

    lax.fori_loop(0, T // QR, q_block, 0)


def rope_tables(T):
    rows = T // GRID_W
    row = jnp.repeat(jnp.arange(rows, dtype=F32), GRID_W)
    col = jnp.tile(jnp.arange(GRID_W, dtype=F32), rows)
    n_freq = HEAD_DIM // 4
    inv = ROPE_THETA ** (-jnp.arange(n_freq, dtype=F32) / n_freq)
    ang = jnp.concatenate([row[:, None] * inv, col[:, None] * inv], axis=-1)
    cos = jnp.repeat(jnp.cos(ang), 2, axis=-1)
    sin = jnp.stack([-jnp.sin(ang), jnp.sin(ang)], axis=-1).reshape(T, HEAD_DIM)
    return jnp.tile(cos, (1, ATT_KV_HEADS)), jnp.tile(sin, (1, ATT_KV_HEADS))


def attention_pallas(p_att, qnorm_g, knorm_g, l, cache=None, kv_all=None):
    B, T, W = p_att.shape
    rope = cache is not None
    past = cache[0].shape[3] if rope else 0
    full = lambda shape: pl.BlockSpec(shape, lambda b: (0,) * len(shape))
    qg = jnp.tile(qnorm_g.reshape(1, HEAD_DIM), (1, ATT_HEADS))
    kg = jnp.tile(knorm_g.reshape(1, HEAD_DIM), (1, ATT_KV_HEADS))
    in_specs = [pl.BlockSpec((1, T, W), lambda b: (b, 0, 0)), full((1, ATT_WIDTH)), full((1, KV_WIDTH))]
    args = [p_att, qg, kg]
    out_specs = [pl.BlockSpec((1, T, ATT_WIDTH), lambda b: (b, 0, 0))]
    out_shape = [jax.ShapeDtypeStruct((B, T, ATT_WIDTH), F32)]
    if rope:
        cos, sin = rope_tables(T)
        kv_spec = pl.BlockSpec((1, 1, ATT_KV_HEADS, past, HEAD_DIM), lambda b: (b, l, 0, 0, 0))
        in_specs += [full((T, KV_WIDTH)), full((T, KV_WIDTH)), kv_spec, kv_spec]
        args += [cos, sin, cache[0], cache[1]]
        aliases = {}
    else:
        kv_spec = pl.BlockSpec((1, 1, ATT_KV_HEADS, T, HEAD_DIM), lambda b: (b, l, 0, 0, 0))
        in_specs += [pl.BlockSpec(memory_space=pl.ANY)] * 2
        args += list(kv_all)
        aliases = {3: 1, 4: 2}
        out_specs += [kv_spec, kv_spec]
        out_shape += [jax.ShapeDtypeStruct(kv_all[0].shape, F32)] * 2
    res = pl.pallas_call(
        partial(_att_kernel, T=T, past=past, rope=rope),
        grid=(B,),
        in_specs=in_specs,
        out_specs=out_specs,
        out_shape=out_shape,
        input_output_aliases=aliases,
        scratch_shapes=[pltpu.VMEM((ATT_KV_HEADS, past + T, HEAD_DIM), BF16)] * 2
        + [pltpu.VMEM((T, ATT_WIDTH), BF16)],
        compiler_params=pltpu.CompilerParams(dimension_semantics=("arbitrary",), vmem_limit_bytes=48 * 1024 * 1024),
        name="attention_rope" if rope else "attention_ctx",
    )(*args)
    return res[0] if rope else tuple(res)


ROW_TILE = 256
MOD_TILE = 1536
ROUTE_TILE = 256
MOE_SLOTS = 256
MOE_WEIGHT_SPLIT = 4
SC_ROWS = 32
SC_LANES = 128
P_ATT, P_RW, P_HG = ATT_WIDTH + 2 * KV_WIDTH, 3 * RW_WIDTH + 384, 5 * HG_WIDTH


def _mod_kernel(c_ref, w_ref, b_ref, o_ref):
    c = c_ref[...]
    o_ref[0] = _mm(c * _sigmoid(c), w_ref[0], passes=3) + b_ref[0]


def adaln_mod_pallas(cvec, w_mod, b_mod):
    n = 6 * D_MODEL
    return pl.pallas_call(
        _mod_kernel,
        grid=(DEPTH, n // MOD_TILE),
        in_specs=[pl.BlockSpec((8, D_MODEL), lambda l, j: (0, 0)),
                  pl.BlockSpec((1, D_MODEL, MOD_TILE), lambda l, j: (l, 0, j)),
                  pl.BlockSpec((1, 1, MOD_TILE), lambda l, j: (l, 0, j))],
        out_specs=pl.BlockSpec((1, 8, MOD_TILE), lambda l, j: (l, 0, j)),
        out_shape=jax.ShapeDtypeStruct((DEPTH, 8, n), F32),
        compiler_params=pltpu.CompilerParams(dimension_semantics=("arbitrary", "arbitrary"),
                                             vmem_limit_bytes=48 * 1024 * 1024),
        name="adaln_mod",
    )(cvec, w_mod, b_mod.reshape(DEPTH, 1, n))


def _rms(x):
    return x * lax.rsqrt(jnp.mean(x * x, axis=-1, keepdims=True) + NORM_EPS)


HALF = D_MODEL // 2
U32 = jnp.uint32


def _pack_rows(x):
    bits = lax.bitcast_convert_type(x.astype(BF16).astype(F32), U32)
    return (bits[:, :HALF] >> 16) | bits[:, HALF:]


def _unpack_rows(w):
    lo = lax.bitcast_convert_type(w << 16, F32)
    hi = lax.bitcast_convert_type(w & jnp.uint32(0xFFFF0000), F32)
    return lo, hi


def _moe_residual(x_ref, g_ref, g4_ref, pm_ref):
    y = None
    for k in range(TOP_K):
        t = g4_ref[:, k:k + 1] * jnp.concatenate(_unpack_rows(g_ref[k]), axis=1)
        y = t if y is None else y + t
    return x_ref[...] + pm_ref[0, 5:6, :] * y


def _in_kernel(*refs, has_res):
    if has_res:
        x_ref, gth_ref, g4_ref, pm_ref, m_ref, g_ref, w_ref, xo_ref, pa_ref, pr_ref, ph_ref = refs
        x = _moe_residual(x_ref, gth_ref, g4_ref, pm_ref)
        xo_ref[...] = x
    else:
        x_ref, m_ref, g_ref, w_ref, pa_ref, pr_ref, ph_ref = refs
        x = x_ref[...]
    h = _rms(x) * g_ref[...] * (1.0 + m_ref[0, 1:2, :]) + m_ref[0, 0:1, :]
    proj = lax.dot_general(h, w_ref[0], _NN, precision=lax.Precision.DEFAULT, preferred_element_type=F32)
    pa_ref[...] = proj[:, 0:P_ATT]
    pr_ref[...] = proj[:, P_ATT:P_ATT + P_RW]
    ph_ref[...] = proj[:, P_ATT + P_RW:]


def _res_specs(row0):
    t0 = row0 // ROW_TILE
    return [pl.BlockSpec((TOP_K, ROW_TILE, HALF), lambda i: (0, i, 0)),
            pl.BlockSpec((ROW_TILE, TOP_K), lambda i: (i + t0, 0))]


def in_proj_pallas(x, mod, norm_g, w_in, l, rows_per_mod, res=None):
    R = x.shape[0]
    tpm = rows_per_mod // ROW_TILE
    rt = lambda w: pl.BlockSpec((ROW_TILE, w), lambda i: (i, 0))
    ms = pl.BlockSpec((1, 6, D_MODEL), lambda i: (i // tpm, 0, 0))
    full = lambda shape, **kw: pl.BlockSpec(shape, lambda i: (0,) * len(shape), **kw)
    in_specs = [rt(D_MODEL)] + (_res_specs(res[2]) + [ms] if res else []) + [
        ms, full((1, D_MODEL)),
        pl.BlockSpec((1, D_MODEL, w_in.shape[2]), lambda i: (l, 0, 0), pipeline_mode=pl.Buffered(1))]
    args = [x] + ([res[0], res[1], res[3]] if res else []) + [mod, norm_g.reshape(1, D_MODEL), w_in]
    widths = ([D_MODEL] if res else []) + [P_ATT, P_RW, P_HG]
    return pl.pallas_call(
        partial(_in_kernel, has_res=res is not None),
        grid=(R // ROW_TILE,),
        in_specs=in_specs,
        out_specs=[rt(w) for w in widths],
        out_shape=[jax.ShapeDtypeStruct((R, w), F32) for w in widths],
        compiler_params=pltpu.CompilerParams(dimension_semantics=("arbitrary",), vmem_limit_bytes=48 * 1024 * 1024),
        name="in_proj",
    )(*args)


def _out_kernel(att_ref, rw_ref, hg_ref, x_ref, m_ref, g_ref, w_ref, rw_w_ref, rb_ref, *rest):
    xo_ref, h_ref, lg_ref = rest[-3:]
    d = lambda a, lo, hi: lax.dot_general(a, w_ref[0, lo:hi, :], _NN, precision=lax.Precision.DEFAULT,
                                          preferred_element_type=F32)
    mixo = (d(att_ref[...], 0, ATT_WIDTH) + d(rw_ref[...], ATT_WIDTH, ATT_WIDTH + RW_WIDTH)
            + d(hg_ref[...], ATT_WIDTH + RW_WIDTH, ATT_WIDTH + RW_WIDTH + HG_WIDTH))
    x = x_ref[...] + m_ref[0, 2:3, :] * mixo
    xo_ref[...] = x
    h = _rms(x) * g_ref[...] * (1.0 + m_ref[0, 4:5, :]) + m_ref[0, 3:4, :]
    h_ref[...] = _pack_rows(h)
    lg_ref[...] = _mm(h, rw_w_ref[...], passes=3) + rb_ref[...]


def out_proj_pallas(att, rw, hg, x, mod, norm_g, w_out, l, router_w, router_b, rows_per_mod, n_all, row0, joint=None):
    R = x.shape[0]
    tpm = rows_per_mod // ROW_TILE
    t0 = row0 // ROW_TILE
    rt = lambda w: pl.BlockSpec((ROW_TILE, w), lambda i: (i, 0))
    jt = lambda w: pl.BlockSpec((ROW_TILE, w), lambda i: (i + t0, 0))
    full = lambda shape: pl.BlockSpec(shape, lambda i: (0,) * len(shape))
    in_specs = [rt(ATT_WIDTH), rt(RW_WIDTH), rt(HG_WIDTH), rt(D_MODEL),
                pl.BlockSpec((1, 6, D_MODEL), lambda i: (i // tpm, 0, 0)), full((1, D_MODEL)),
                pl.BlockSpec((1, D_MODEL, D_MODEL), lambda i: (l, 0, 0)), full((D_MODEL, N_EXPERTS)),
                full((1, N_EXPERTS))]
    args = [att, rw, hg, x, mod, norm_g.reshape(1, D_MODEL), w_out, router_w, router_b.reshape(1, N_EXPERTS)]
    aliases = {}
    if joint is not None:
        in_specs += [pl.BlockSpec(memory_space=pl.ANY)] * 2
        aliases = {len(args): 1, len(args) + 1: 2}
        args += list(joint)
    return pl.pallas_call(
        _out_kernel,
        grid=(R // ROW_TILE,),
        in_specs=in_specs,
        out_specs=[rt(D_MODEL), jt(HALF), jt(N_EXPERTS)],
        out_shape=[jax.ShapeDtypeStruct((R, D_MODEL), F32), jax.ShapeDtypeStruct((n_all, HALF), U32),
                   jax.ShapeDtypeStruct((n_all, N_EXPERTS), F32)],
        input_output_aliases=aliases,
        compiler_params=pltpu.CompilerParams(dimension_semantics=("arbitrary",), vmem_limit_bytes=48 * 1024 * 1024),
        name="out_proj",
    )(*args)


def _final_kernel(x_ref, gth_ref, g4_ref, m_ref, g_ref, o_ref):
    o_ref[...] = _rms(_moe_residual(x_ref, gth_ref, g4_ref, m_ref)) * g_ref[...]


def final_norm_pallas(x, gathered, gate4, row0, mod, norm_g, rows_per_mod):
    R = x.shape[0]
    tpm = rows_per_mod // ROW_TILE
    rt = pl.BlockSpec((ROW_TILE, D_MODEL), lambda i: (i, 0))
    return pl.pallas_call(
        _final_kernel,
        grid=(R // ROW_TILE,),
        in_specs=[rt] + _res_specs(row0) + [pl.BlockSpec((1, 6, D_MODEL), lambda i: (i // tpm, 0, 0)),
                                            pl.BlockSpec((1, D_MODEL), lambda i: (0, 0))],
        out_specs=rt,
        out_shape=jax.ShapeDtypeStruct((R, D_MODEL), F32),
        name="final_norm",
    )(x, gathered, gate4, mod, norm_g.reshape(1, D_MODEL))


def _moe_max_blocks(n_tok):
    return (n_tok * TOP_K + N_EXPERTS * (MOE_SLOTS - 1)) // MOE_SLOTS


def _exact_nt_ones(a):
    ones = jnp.ones((8, a.shape[1]), BF16)
    out = None
    for t in reversed(_split(a, 3)):
        y = lax.dot_general(ones, t, _NT, preferred_element_type=F32)
        out = y if out is None else out + y
    return out


def _route_kernel(lg_ref, dest_ref, gate4_ref, blk_ref, rank_scr, gate_scr, *, n_tok):
    Rt, E = ROUTE_TILE, N_EXPERTS
    n_tiles = n_tok // Rt
    lane = lax.broadcasted_iota(jnp.int32, (Rt, E), 1)
    ti = lax.broadcasted_iota(jnp.int32, (Rt, Rt), 0)
    si = lax.broadcasted_iota(jnp.int32, (Rt, Rt), 1)
    lower = (ti > si).astype(BF16)
    ei = lax.broadcasted_iota(jnp.int32, (E, E), 0)
    ej = lax.broadcasted_iota(jnp.int32, (E, E), 1)
    upper_e = (ei < ej).astype(BF16)

    def tile_members(it, off):
        rows = pl.ds(pl.multiple_of(it * Rt, Rt), Rt)
        lg = lg_ref[rows, :]
        work = lg
        member = jnp.zeros((Rt, E), jnp.bool_)
        top = None
        for k in range(TOP_K):
            m = jnp.max(work, axis=-1, keepdims=True)
            if top is None:
                top = m
            first = jnp.min(jnp.where(work == m, lane, E), axis=-1, keepdims=True)
            pick = lane == first
            member = member | pick
            work = jnp.where(pick, -jnp.inf, work)
        ex = jnp.where(member, jnp.exp(lg - top), 0.0)
        gate_scr[rows, :] = ex / jnp.sum(ex, axis=-1, keepdims=True)
        mem = member.astype(BF16)
        rank = lax.dot_general(lower, mem, _NN, preferred_element_type=F32) + off
        rank_scr[rows, :] = jnp.where(member, rank, -1.0)
        return off + jnp.sum(mem.astype(F32), axis=0, keepdims=True)

    count = lax.fori_loop(0, n_tiles, tile_members, jnp.zeros((1, E), F32))
    nblk = jnp.floor((count + (MOE_SLOTS - 1)) * (1.0 / MOE_SLOTS))
    bstart = lax.dot_general(nblk.astype(BF16), upper_e, _NN, preferred_element_type=F32)
    bend = bstart + nblk
    pstart = bstart * MOE_SLOTS

    def tile_slots(it, carry):
        rows = pl.ds(pl.multiple_of(it * Rt, Rt), Rt)
        rank = rank_scr[rows, :]
        gate = gate_scr[rows, :]
        member = rank >= 0.0
        kidx = lax.dot_general(member.astype(BF16), upper_e, _NN, preferred_element_type=F32)
        slot = pstart + rank
        cols = []
        for k in range(TOP_K):
            sel = member & (kidx == k)
            dest_ref[k:k + 1, rows] = _exact_nt_ones(jnp.where(sel, slot, 0.0))[0:1].astype(jnp.int32)
            cols.append(jnp.sum(jnp.where(sel, gate, 0.0), axis=-1, keepdims=True))
        gate4_ref[rows, :] = jnp.concatenate(cols, axis=1)
        return carry

    lax.fori_loop(0, n_tiles, tile_slots, 0)
    nb = blk_ref.shape[1]
    bi = lax.broadcasted_iota(jnp.int32, (E, nb), 1).astype(F32)
    bend_col = jnp.sum(jnp.where(ei == ej, jnp.broadcast_to(bend, (E, E)), 0.0), axis=1, keepdims=True)
    owner = jnp.sum((bend_col <= bi).astype(F32), axis=0, keepdims=True)
    blk_ref[0:1, :] = jnp.minimum(owner, E - 1.0).astype(jnp.int32)
    blk_ref[1:2, :] = jnp.broadcast_to(jnp.sum(nblk, axis=-1, keepdims=True), (1, nb)).astype(jnp.int32)
    blk_ref[2:8, :] = jnp.zeros((6, nb), jnp.int32)


def moe_route_pallas(logits):
    n_tok = logits.shape[0]
    nb = -(-_moe_max_blocks(n_tok) // 128) * 128
    return pl.pallas_call(
        partial(_route_kernel, n_tok=n_tok),
        out_shape=[jax.ShapeDtypeStruct((TOP_K, n_tok), jnp.int32),
                   jax.ShapeDtypeStruct((n_tok, TOP_K), F32),
                   jax.ShapeDtypeStruct((8, nb), jnp.int32)],
        scratch_shapes=[pltpu.VMEM((n_tok, N_EXPERTS), F32)] * 2,
        name="moe_route",
    )(logits)


def _moe_block_kernel(be_ref, nu_ref, first_ref, par_ref, nxt_ref, xb_ref, wgu_hbm, bgu_ref, wdn_hbm, bdn_ref, yb_ref,
                      wgu_buf, wdn_buf, sem, *, l):
    i = pl.program_id(0)

    def weight_copies(e, slot):
        cps = []
        for hbm, buf, s in ((wgu_hbm, wgu_buf, 0), (wdn_hbm, wdn_buf, 1)):
            rows = buf.shape[1] // MOE_WEIGHT_SPLIT
            for j in range(MOE_WEIGHT_SPLIT):
                rs = pl.ds(j * rows, rows)
                cps.append(pltpu.make_async_copy(hbm.at[l, e, rs], buf.at[slot, rs], sem.at[s, slot]))
        return cps

    @pl.when(i < nu_ref[0])
    def _():
        slot = par_ref[i]

        @pl.when(first_ref[i] == 1)
        def _():
            @pl.when(i == 0)
            def _():
                for cp in weight_copies(be_ref[0], slot):
                    cp.start()

            for cp in weight_copies(be_ref[i], slot):
                cp.wait()

            @pl.when(nxt_ref[i] >= 0)
            def _():
                for cp in weight_copies(nxt_ref[i], 1 - slot):
                    cp.start()

        dot = lambda a, w: lax.dot_general(a, w, _NN, precision=lax.Precision.DEFAULT, preferred_element_type=F32)
        x_lo, x_hi = _unpack_rows(xb_ref[...])
        gu = dot(x_lo, wgu_buf[slot, 0:HALF]) + dot(x_hi, wgu_buf[slot, HALF:D_MODEL]) + bgu_ref[0, 0]
        glu = jnp.minimum(gu[:, :EXPERT_FF], SWIGLU_LIMIT)
        lin = jnp.clip(gu[:, EXPERT_FF:], -SWIGLU_LIMIT, SWIGLU_LIMIT)
        act = glu * _sigmoid(SWIGLU_ALPHA * glu) * (lin + 1.0)
        yb_ref[...] = _pack_rows(dot(act, wdn_buf[slot]) + bdn_ref[0, 0])


def _expert_runs(block_e, n_used):
    n = block_e.shape[0]
    idx = jnp.arange(n, dtype=jnp.int32)
    valid = idx < n_used[0]
    first = valid & ((idx == 0) | (block_e != jnp.roll(block_e, 1)))
    par = (jnp.cumsum(first.astype(jnp.int32)) - 1) % 2
    start = jnp.where(first, idx, n)
    nxt_start = lax.cummin(jnp.concatenate([start[1:], jnp.full((1,), n, jnp.int32)]), reverse=True)
    nxt = jnp.where(nxt_start < n, block_e[jnp.minimum(nxt_start, n - 1)], -1)
    return first.astype(jnp.int32), par.astype(jnp.int32), nxt.astype(jnp.int32)


def moe_blocks_pallas(xb, block_e, n_used, l, w_gu, b_gu, w_down, b_down):
    n_blocks = xb.shape[0] // MOE_SLOTS
    first, par, nxt = _expert_runs(block_e, n_used)
    blk = lambda i, be, nu, *_: (jnp.minimum(i, nu[0] - 1), 0)
    bsel = lambda i, be, nu, *_: (l, be[jnp.minimum(i, nu[0] - 1)], 0, 0)
    grid_spec = pltpu.PrefetchScalarGridSpec(
        num_scalar_prefetch=5,
        grid=(n_blocks,),
        in_specs=[pl.BlockSpec((MOE_SLOTS, HALF), blk),
                  pl.BlockSpec(memory_space=pl.ANY),
                  pl.BlockSpec((1, 1, 1, 2 * EXPERT_FF), bsel),
                  pl.BlockSpec(memory_space=pl.ANY),
                  pl.BlockSpec((1, 1, 1, D_MODEL), bsel)],
        out_specs=pl.BlockSpec((MOE_SLOTS, HALF), blk),
        scratch_shapes=[pltpu.VMEM((2, D_MODEL, 2 * EXPERT_FF), F32), pltpu.VMEM((2, EXPERT_FF, D_MODEL), F32),
                        pltpu.SemaphoreType.DMA((2, 2))],
    )
    return pl.pallas_call(
        partial(_moe_block_kernel, l=l),
        grid_spec=grid_spec,
        out_shape=jax.ShapeDtypeStruct(xb.shape, U32),
        compiler_params=pltpu.CompilerParams(dimension_semantics=("arbitrary",), vmem_limit_bytes=48 * 1024 * 1024),
        name="moe_blocks",
    )(block_e, n_used, first, par, nxt, xb, w_gu, b_gu.reshape(DEPTH, N_EXPERTS, 1, 2 * EXPERT_FF), w_down,
      b_down.reshape(DEPTH, N_EXPERTS, 1, D_MODEL))


def _sc_mesh():
    return plsc.VectorSubcoreMesh(core_axis_name="c", subcore_axis_name="s")


def _sc_index_rows(idx):
    return jnp.pad(idx.reshape(-1, SC_ROWS), ((0, 0), (0, SC_LANES - SC_ROWS)))


def sc_dispatch(h, dest, n_rows):
    n_tok, d = h.shape
    idx = [_sc_index_rows(dest[k]) for k in range(TOP_K)]

    @pl.kernel(out_type=jax.ShapeDtypeStruct((n_rows, d), h.dtype), mesh=_sc_mesh(), scratch_types=[])
    def kern(h_hbm, i0, i1, i2, i3, o_hbm):
        def body(x_vmem, *i_vmem):
            for iv in i_vmem:
                pltpu.sync_copy(x_vmem, o_hbm.at[iv.at[0, pl.ds(0, SC_ROWS)]])

        pltpu.emit_pipeline(
            body,
            grid=(n_tok // SC_ROWS,),
            in_specs=[pl.BlockSpec((SC_ROWS, d), lambda i: (i, 0))]
            + [pl.BlockSpec((1, SC_LANES), lambda i: (i, 0))] * TOP_K,
            out_specs=[],
            core_axis_name=("c", "s"),
            dimension_semantics=(pltpu.PARALLEL,),
        )(h_hbm, i0, i1, i2, i3)

    return kern(h, *idx)


def sc_combine_gather(yb, dest):
    n_tok = dest.shape[1]
    d = yb.shape[1]
    idx = _sc_index_rows(dest.reshape(TOP_K * n_tok))

    @pl.kernel(out_type=jax.ShapeDtypeStruct((TOP_K * n_tok, d), yb.dtype), mesh=_sc_mesh(), scratch_types=[])
    def kern(y_hbm, i_hbm, o_hbm):
        def body(i_vmem, o_vmem):
            pltpu.sync_copy(y_hbm.at[i_vmem.at[0, pl.ds(0, SC_ROWS)]], o_vmem)

        pltpu.emit_pipeline(
            body,
            grid=(TOP_K * n_tok // SC_ROWS,),
            in_specs=[pl.BlockSpec((1, SC_LANES), lambda i: (i, 0))],
            out_specs=[pl.BlockSpec((SC_ROWS, d), lambda i: (i, 0))],
            core_axis_name=("c", "s"),
            dimension_semantics=(pltpu.PARALLEL,),
        )(i_hbm, o_hbm)

    return kern(yb, idx).reshape(TOP_K, n_tok, d)


def hgrn_lower_bounds(hg_lb):
    sm = jax.nn.softmax(hg_lb.astype(jnp.float32), axis=0)
    return jnp.cumsum(sm, axis=0) - sm[0:1]


def kernel(x_prompt, x_sample, cache_att_k, cache_att_v, state_rwkv, state_hgrn, c, c_ctx, w_mod, b_mod, norm_mix_g, norm_ffn_g, w_in, w_out, att_qnorm_g, att_knorm_g, rw_w0, rw_w2, rw_a0, rw_a2, rw_g2, rw_kk, rw_ka, rw_rk, rw_gn_g, rw_gn_b, hg_lb, hg_norm_g, router_w, router_b, moe_w_gu, moe_b_gu, moe_w_down, moe_b_down, final_norm_g):
    BP, TP, _ = x_prompt.shape
    BS, TS, _ = x_sample.shape
    n_p, n_s = BP * TP, BS * TS
    lb_all = hgrn_lower_bounds(hg_lb)
    cvec = jnp.concatenate([c_ctx[None, :], c, jnp.zeros((8 - 1 - BS, D_MODEL), F32)], axis=0)
    mod_all = adaln_mod_pallas(cvec, w_mod, b_mod).reshape(DEPTH, 8, 6, D_MODEL)
    zeros_state = jnp.zeros((BP, 1, 2, RW_HEADS, HEAD_DIM, HEAD_DIM), F32)
    kv_all = (jnp.zeros((BP, DEPTH, ATT_KV_HEADS, TP, HEAD_DIM), F32),) * 2
    rw_states = jnp.zeros((BP, DEPTH, 2, RW_HEADS, HEAD_DIM, HEAD_DIM), F32)
    hg_states = jnp.zeros((BP, DEPTH, 2, HG_HEADS, HEAD_DIM, HEAD_DIM), F32)
    x = {'p': x_prompt.reshape(n_p, D_MODEL), 's': x_sample.reshape(n_s, D_MODEL)}
    dims = {'p': (TP, BP, n_p, 0), 's': (TS, BS, TS, n_p)}
    moe_out, mod_prev = None, None
    for l in range(DEPTH):
        prm = dict(rw_w0=rw_w0[l], rw_w2=rw_w2[l], rw_a0=rw_a0[l], rw_a2=rw_a2[l], rw_g2=rw_g2[l],
                   rw_kk=rw_kk[l], rw_ka=rw_ka[l], rw_rk=rw_rk[l], rw_gn_g=rw_gn_g[l], rw_gn_b=rw_gn_b[l])
        mods = {'p': mod_all[l, 0:1], 's': mod_all[l, 1:1 + BS]}
        joint = (jnp.zeros((n_p + n_s, HALF), U32), jnp.zeros((n_p + n_s, N_EXPERTS), F32))
        for s in ('p', 's'):
            T, B, rpm, row0 = dims[s]
            if l == 0:
                p_att, p_rw, p_hg = in_proj_pallas(x[s], mods[s], norm_mix_g[l], w_in, l, rpm)
            else:
                x[s], p_att, p_rw, p_hg = in_proj_pallas(x[s], mods[s], norm_mix_g[l], w_in, l, rpm,
                                                         res=(*moe_out[s], row0, mod_prev[s]))
            p_att, p_rw, p_hg = (t.reshape(B, T, -1) for t in (p_att, p_rw, p_hg))
            if s == 'p':
                att, *kv_all = attention_pallas(p_att, att_qnorm_g[l], att_knorm_g[l], l, kv_all=kv_all)
                rw_out, rw_states = rwkv7_mixer_pallas(p_rw, zeros_state, 0, prm, rw_states, l)
                hg_out, hg_states = hgrn2_mixer_pallas(p_hg, zeros_state, 0, lb_all[l], hg_norm_g[l], hg_states, l)
            else:
                att = attention_pallas(p_att, att_qnorm_g[l], att_knorm_g[l], l, cache=(cache_att_k, cache_att_v))
                rw_out, _ = rwkv7_mixer_pallas(p_rw, state_rwkv, l, prm)
                hg_out, _ = hgrn2_mixer_pallas(p_hg, state_hgrn, l, lb_all[l], hg_norm_g[l])
            x[s], *joint = out_proj_pallas(att.reshape(B * T, -1), rw_out.reshape(B * T, -1),
                                           hg_out.reshape(B * T, -1), x[s], mods[s], norm_ffn_g[l], w_out, l,
                                           router_w[l], router_b[l], rpm, n_p + n_s, row0, joint)
        h_all, logits_all = joint
        dest, gate4, blk = moe_route_pallas(logits_all)
        xb = sc_dispatch(h_all, dest, _moe_max_blocks(n_p + n_s) * MOE_SLOTS)
        yb = moe_blocks_pallas(xb, blk[0], blk[1, :1], l, moe_w_gu, moe_b_gu, moe_w_down, moe_b_down)
        moe_out = {'p': (sc_combine_gather(yb, dest[:, :n_p]), gate4), 's': (sc_combine_gather(yb, dest[:, n_p:]), gate4)}
        mod_prev = mods
    y_prompt = final_norm_pallas(x['p'], *moe_out['p'], 0, mod_prev['p'], final_norm_g, n_p)
    y_sample = final_norm_pallas(x['s'], *moe_out['s'], n_p, mod_prev['s'], final_norm_g, TS)
    return (y_prompt.reshape(x_prompt.shape), y_sample.reshape(x_sample.shape), kv_all[0], kv_all[1],
            rw_states, hg_states)
```

```python
import math
from functools import partial

import jax
import jax.numpy as jnp
from jax import lax
from jax.experimental import pallas as pl
from jax.experimental.pallas import tpu as pltpu
from jax.experimental.pallas import tpu_sc as plsc

D_MODEL = 1024
DEPTH = 2
GRID_W = 64
HEAD_DIM = 64
ATT_HEADS = 8
ATT_KV_HEADS = 2
ATT_WIDTH = ATT_HEADS * HEAD_DIM
KV_WIDTH = ATT_KV_HEADS * HEAD_DIM
RW_HEADS = 4
RW_WIDTH = RW_HEADS * HEAD_DIM
RW_GN_EPS = 64e-5
HG_HEADS = 4
HG_WIDTH = HG_HEADS * HEAD_DIM
HG_F_MIN = 1e-6
N_EXPERTS = 32
TOP_K = 4
EXPERT_FF = D_MODEL
SWIGLU_LIMIT = 7.0
SWIGLU_ALPHA = 1.702
ROPE_THETA = 10000.0
NORM_EPS = 1e-6

RW_CHUNK = 64
BF16 = jnp.bfloat16
F32 = jnp.float32

_NN = (((1,), (0,)), ((), ()))
_NT = (((1,), (1,)), ((), ()))
_TN = (((0,), (0,)), ((), ()))


def _split(x, n):
    parts = []
    for _ in range(n - 1):
        hi = x.astype(BF16)
        parts.append(hi)
        x = x - hi.astype(F32)
    parts.append(x.astype(BF16))
    return parts


def _mm(a, b, dims=_NN, passes=1):
    d = lambda x, y: lax.dot_general(x, y, dims, preferred_element_type=F32)
    if passes == 1:
        return d(a.astype(BF16), b.astype(BF16))
    ah, al = _split(a, 2)
    bh, bl = _split(b, 2)
    return d(ah, bl) + d(al, bh) + d(ah, bh)


def _mm_exact_lhs(a01, b, n=3):
    a = a01.astype(BF16)
    out = None
    for t in reversed(_split(b, n)):
        y = lax.dot_general(a, t, _NN, preferred_element_type=F32)
        out = y if out is None else out + y
    return out


def _mm_exact_rhs(a, b01, n=3):
    b = b01.astype(BF16)
    out = None
    for t in reversed(_split(a, n)):
        y = lax.dot_general(t, b, _NN, preferred_element_type=F32)
        out = y if out is None else out + y
    return out


def _head_blockdiag(width):
    r = lax.broadcasted_iota(jnp.int32, (width, width), 0) // HEAD_DIM
    c = lax.broadcasted_iota(jnp.int32, (width, width), 1) // HEAD_DIM
    return (r == c).astype(F32)


def _sigmoid(x):
    return 1.0 / (1.0 + jnp.exp(-x))


def _softplus(x):
    return jnp.maximum(x, 0.0) + jnp.log(1.0 + jnp.exp(-jnp.abs(x)))


def _rwkv_kernel(p_ref, s0_ref, w0_ref, w2_ref, a0_ref, a2_ref, g2_ref, kk_ref, ka_ref, rk_ref, gng_ref, gnb_ref,
                 *rest, T, NB):
    out_ref, st_ref, lw_scr, kd_scr, bb_scr, y_scr, kk_scr, s_scr = rest[-8:]
    C = RW_CHUNK
    n_chunks = T // C
    bd = _head_blockdiag(RW_WIDTH)
    seg = lambda t: _mm_exact_rhs(t, bd)
    ka = ka_ref[...]
    for nb in range(NB):
        k = p_ref[nb, :, 256:512]
        kk = k * kk_ref[...]
        kk = kk * lax.rsqrt(seg(kk * kk) + 1e-12)
        kk_scr[nb] = kk
        for d in range(2):
            wd = p_ref[nb, :, 768 + 64 * d:832 + 64 * d]
            ad = p_ref[nb, :, 896 + 64 * d:960 + 64 * d]
            w_raw = w0_ref[d:d + 1, :] + _mm(jnp.tanh(wd), w2_ref[d], passes=3)
            lw_scr[nb, d] = -jnp.exp(-_softplus(-w_raw) - 0.5)
            a = _sigmoid(a0_ref[d:d + 1, :] + _mm(ad, a2_ref[d], passes=3))
            kd_scr[nb, d] = k * (1.0 + (a - 1.0) * ka)
            bb_scr[nb, d] = kk * a
    s_scr[...] = s0_ref[:, 0]

    ti = lax.broadcasted_iota(jnp.int32, (C, C), 0)
    si = lax.broadcasted_iota(jnp.int32, (C, C), 1)
    ones_cc = jnp.ones((C, C), F32)

    def chunk_body(i, carry):
        ch = []
        for nb, d in [(nb, d) for nb in range(NB) for d in range(2)]:
            ci = i if d == 0 else n_chunks - 1 - i
            rows = pl.ds(pl.multiple_of(ci * C, C), C)
            strict = (ti > si) if d == 0 else (ti < si)
            incl = (ti >= si) if d == 0 else (ti <= si)
            lw = lw_scr[nb, d, rows, :]
            cum = _mm_exact_lhs(incl.astype(F32), lw)
            total = _mm_exact_lhs(ones_cc, lw)
            cum_ex = cum - lw
            mid = 0.5 * total
            rr = p_ref[nb, rows, 0:256]
            vv = p_ref[nb, rows, 512:768]
            kdc = kd_scr[nb, d, rows, :]
            bbc = bb_scr[nb, d, rows, :]
            kkc = kk_scr[nb, rows, :]
            e_inv = jnp.exp(mid - cum)
            At = -kkc * jnp.exp(cum_ex - mid)
            Rt = rr * jnp.exp(cum - mid)
            Bt = bbc * e_inv
            Kt = kdc * e_inv
            Ap = -kkc * jnp.exp(cum_ex)
            Rp = rr * jnp.exp(cum)
            e_out = jnp.exp(total - cum)
            Bh = bbc * e_out
            Kh = kdc * e_out
            e_tot = jnp.exp(total[0:1, :])
            for h in range(RW_HEADS):
                hs = slice(h * HEAD_DIM, (h + 1) * HEAD_DIM)
                ch.append(dict(nb=nb, d=d, h=h, rows=rows, hs=hs, strict=strict, incl=incl,
                               AR=jnp.concatenate([At[:, hs], Rt[:, hs]], axis=0), Bt=Bt[:, hs], Kt=Kt[:, hs],
                               V=vv[:, hs], X1=Ap[:, hs], Rp=Rp[:, hs], Bh=Bh[:, hs], Kh=Kh[:, hs],
                               e_tot=e_tot[:, hs]))
        for c in ch:
            c['AB'] = _mm(c['AR'], c['Bt'], _NT)
            c['AK'] = _mm(c['AR'], c['Kt'], _NT)
        for c in ch:
            c['P'] = jnp.where(c['strict'], c['AB'][:C], 0.0)
            c['A_ak'] = jnp.where(c['strict'], c['AK'][:C], 0.0)
            c['A_rb'] = jnp.where(c['incl'], c['AB'][C:], 0.0)
            c['A_rk'] = jnp.where(c['incl'], c['AK'][C:], 0.0)
        for c in ch:
            c['X2'] = _mm(c['A_ak'], c['V'])
        for lvl in range(6):
            for c in ch:
                if lvl < 5:
                    c['PZ'] = _mm(c['P'], jnp.concatenate([c['P'], c['X1'], c['X2']], axis=1))
                else:
                    c['PZ'] = _mm(c['P'], jnp.concatenate([c['X1'], c['X2']], axis=1))
            for c in ch:
                PZ = c['PZ']
                if lvl < 5:
                    c['P'] = PZ[:, :C]
                    c['X1'] = c['X1'] + PZ[:, C:2 * C]
                    c['X2'] = c['X2'] + PZ[:, 2 * C:]
                else:
                    c['X1'] = c['X1'] + PZ[:, :C]
                    c['X2'] = c['X2'] + PZ[:, C:]
        for c in ch:
            c['S0'] = s_scr[c['nb'], c['d'], c['h']]
            c['UY'] = _mm(jnp.concatenate([c['X1'], c['Rp']], axis=0), c['S0'], _NT)
        for c in ch:
            c['U'] = c['UY'][:C] + c['X2']
        for c in ch:
            c['Y'] = c['UY'][C:] + _mm(c['A_rb'], c['U']) + _mm(c['A_rk'], c['V'])
            c['S1'] = c['S0'] * c['e_tot'] + _mm(c['U'], c['Bh'], _TN) + _mm(c['V'], c['Kh'], _TN)
        for c in ch:
            s_scr[c['nb'], c['d'], c['h']] = c['S1']
            y_scr[c['nb'], c['d'], c['rows'], c['hs']] = c['Y']
        return carry

    lax.fori_loop(0, n_chunks, chunk_body, 0)

    for nb in range(NB):
        r = p_ref[nb, :, 0:256]
        v = p_ref[nb, :, 512:768]
        bonus = seg(r * (kd_scr[nb, 0] + kd_scr[nb, 1]) * rk_ref[...]) * v
        g = _mm(_sigmoid(p_ref[nb, :, 1024:1152]), g2_ref[...], passes=3)
        y = y_scr[nb, 0] + y_scr[nb, 1]
        mu = seg(y) * (1.0 / HEAD_DIM)
        yc = y - mu
        var = seg(yc * yc) * (1.0 / HEAD_DIM)
        yn = yc * lax.rsqrt(var + RW_GN_EPS)
        out_ref[nb] = (yn * gng_ref[...] + gnb_ref[...] + bonus) * g
    st_ref[:, 0] = s_scr[...]


RW_ROWS = 1024


def _state_spec(nb, layer):
    return pl.BlockSpec((nb, 1, 2, RW_HEADS, HEAD_DIM, HEAD_DIM), lambda b: (b, layer, 0, 0, 0, 0))


def rwkv7_mixer_pallas(p_rw, s0, l_in, prm, st_all=None, l_out=0):
    B, T, W = p_rw.shape
    NB = max(2, RW_ROWS // T)
    row = lambda a: a.reshape(1, RW_WIDTH)
    full = lambda shape: pl.BlockSpec(shape, lambda b: (0,) * len(shape))
    st_shape = (B, 1, 2, RW_HEADS, HEAD_DIM, HEAD_DIM) if st_all is None else st_all.shape
    return pl.pallas_call(
        partial(_rwkv_kernel, T=T, NB=NB),
        grid=(B // NB,),
        in_specs=[pl.BlockSpec((NB, T, W), lambda b: (b, 0, 0)), _state_spec(NB, l_in),
                  full((2, RW_WIDTH)), full((2, 64, RW_WIDTH)), full((2, RW_WIDTH)), full((2, 64, RW_WIDTH)),
                  full((128, RW_WIDTH)), full((1, RW_WIDTH)), full((1, RW_WIDTH)), full((1, RW_WIDTH)),
                  full((1, RW_WIDTH)), full((1, RW_WIDTH))]
        + ([] if st_all is None else [pl.BlockSpec(memory_space=pl.ANY)]),
        out_specs=[pl.BlockSpec((NB, T, RW_WIDTH), lambda b: (b, 0, 0)), _state_spec(NB, l_out)],
        out_shape=[jax.ShapeDtypeStruct((B, T, RW_WIDTH), F32), jax.ShapeDtypeStruct(st_shape, F32)],
        input_output_aliases={} if st_all is None else {12: 1},
        scratch_shapes=[pltpu.VMEM((NB, 2, T, RW_WIDTH), F32)] * 4
        + [pltpu.VMEM((NB, T, RW_WIDTH), F32), pltpu.VMEM((NB, 2, RW_HEADS, HEAD_DIM, HEAD_DIM), F32)],
        compiler_params=pltpu.CompilerParams(dimension_semantics=("arbitrary",), vmem_limit_bytes=56 * 1024 * 1024),
        name="rwkv7_mixer",
    )(p_rw, s0, prm['rw_w0'], prm['rw_w2'], prm['rw_a0'], prm['rw_a2'], prm['rw_g2'], row(prm['rw_kk']),
      row(prm['rw_ka']), row(prm['rw_rk']), row(prm['rw_gn_g']), row(prm['rw_gn_b']),
      *([] if st_all is None else [st_all]))


HG_SUB = 16
HG_ROWS = 64


def _hgrn_kernel(p_ref, s0_ref, lb_ref, ng_ref, *rest, T):
    out_ref, st_ref, lf_scr, kf_scr, o_scr, s_scr = rest[-6:]
    R, c = HG_ROWS, HG_SUB
    n_it = T // R
    x = p_ref[0]
    bd = _head_blockdiag(HG_WIDTH)
    seg = lambda t: _mm_exact_rhs(t, bd)
    for d in range(2):
        lbd = lb_ref[d:d + 1, :]
        f = lbd + (1.0 - lbd) * _sigmoid(x[:, 256 + 256 * d:512 + 256 * d])
        lf_scr[d] = jnp.log(jnp.maximum(f, HG_F_MIN))
        kf_scr[d] = 1.0 - f
        for h in range(HG_HEADS):
            s_scr[d, h] = s0_ref[0, 0, d, h].T

    ti = lax.broadcasted_iota(jnp.int32, (R, R), 0)
    si = lax.broadcasted_iota(jnp.int32, (R, R), 1)
    same_blk = (ti // c) == (si // c)
    t16 = lax.broadcasted_iota(jnp.int32, (c, 1), 0)

    def body(i, carry):
        for d in range(2):
            ci = i if d == 0 else n_it - 1 - i
            rows = pl.ds(pl.multiple_of(ci * R, R), R)
            incl = (ti >= si) if d == 0 else (ti <= si)
            lf = lf_scr[d, rows, :]
            cum = _mm_exact_lhs((incl & same_blk).astype(F32), lf)
            tot = _mm_exact_lhs(same_blk.astype(F32), lf)
            xq = p_ref[0, rows, 0:256]
            q = xq * _sigmoid(xq)
            v = p_ref[0, rows, 768:1024]
            kf = kf_scr[d, rows, :]
            Qp = q * jnp.exp(cum)
            Kh = kf * jnp.exp(tot - cum)
            e_tot = jnp.exp(tot)
            blocks = range(R // c) if d == 0 else range(R // c - 1, -1, -1)
            o_parts = [None] * (R // c)
            for j in blocks:
                rs = slice(j * c, (j + 1) * c)
                cb, qb, kb, vb = cum[rs], q[rs], kf[rs], v[rs]
                prods = []
                for s in range(c):
                    e = jnp.exp(jnp.minimum(cb - cb[s:s + 1, :], 0.0))
                    prods.append(qb * (kb[s:s + 1, :] * e))
                att = _mm_exact_rhs(jnp.concatenate(prods, axis=0), bd, n=1)
                o_blk = jnp.zeros((c, HG_WIDTH), F32)
                for s in range(c):
                    keep = (t16 >= s) if d == 0 else (t16 <= s)
                    o_blk = o_blk + jnp.where(keep, att[s * c:(s + 1) * c], 0.0) * vb[s:s + 1, :]
                o_heads = []
                for h in range(HG_HEADS):
                    hs = slice(h * HEAD_DIM, (h + 1) * HEAD_DIM)
                    ST = s_scr[d, h]
                    o_heads.append(_mm(Qp[rs, hs], ST, _NT))
                    s_scr[d, h] = ST * e_tot[j * c:j * c + 1, hs] + _mm(vb[:, hs], Kh[rs, hs], _TN)
                o_parts[j] = o_blk + jnp.concatenate(o_heads, axis=1)
            o_scr[d, rows, :] = jnp.concatenate(o_parts, axis=0)
        return carry

    lax.fori_loop(0, n_it, body, 0)

    o = o_scr[0] + o_scr[1]
    o = o * lax.rsqrt(seg(o * o) * (1.0 / HEAD_DIM) + NORM_EPS) * ng_ref[...]
    gg = x[:, 1024:1280]
    out_ref[0] = o * (gg * _sigmoid(gg))
    for d in range(2):
        for h in range(HG_HEADS):
            st_ref[0, 0, d, h] = s_scr[d, h].T


def hgrn2_mixer_pallas(p_hg, s0, l_in, lb, norm_g, st_all=None, l_out=0):
    B, T, W = p_hg.shape
    full = lambda shape: pl.BlockSpec(shape, lambda b: (0,) * len(shape))
    st_shape = (B, 1, 2, HG_HEADS, HEAD_DIM, HEAD_DIM) if st_all is None else st_all.shape
    return pl.pallas_call(
        partial(_hgrn_kernel, T=T),
        grid=(B,),
        in_specs=[pl.BlockSpec((1, T, W), lambda b: (b, 0, 0)), _state_spec(1, l_in), full((2, HG_WIDTH)),
                  full((1, HG_WIDTH))] + ([] if st_all is None else [pl.BlockSpec(memory_space=pl.ANY)]),
        out_specs=[pl.BlockSpec((1, T, HG_WIDTH), lambda b: (b, 0, 0)), _state_spec(1, l_out)],
        out_shape=[jax.ShapeDtypeStruct((B, T, HG_WIDTH), F32), jax.ShapeDtypeStruct(st_shape, F32)],
        input_output_aliases={} if st_all is None else {4: 1},
        scratch_shapes=[pltpu.VMEM((2, T, HG_WIDTH), F32)] * 3
        + [pltpu.VMEM((2, HG_HEADS, HEAD_DIM, HEAD_DIM), F32)],
        compiler_params=pltpu.CompilerParams(dimension_semantics=("arbitrary",), vmem_limit_bytes=48 * 1024 * 1024),
        name="hgrn2_mixer",
    )(p_hg, s0, lb, jnp.tile(norm_g.reshape(1, HEAD_DIM), (1, HG_HEADS)), *([] if st_all is None else [st_all]))


ATT_REP = ATT_HEADS // ATT_KV_HEADS
ATT_QROWS = 128


def _swap_pairs(x):
    w = x.shape[-1]
    lane = lax.broadcasted_iota(jnp.int32, x.shape, x.ndim - 1)
    return jnp.where(lane % 2 == 0, pltpu.roll(x, w - 1, x.ndim - 1), pltpu.roll(x, 1, x.ndim - 1))


def _att_kernel(*refs, T, past, rope):
    if rope:
        p_ref, qg_ref, kg_ref, cos_ref, sin_ref, ck_ref, cv_ref, out_ref, k_scr, v_scr, q_scr = refs
    else:
        p_ref, qg_ref, kg_ref, _, _, out_ref, kh_ref, vh_ref, k_scr, v_scr, q_scr = refs
    x = p_ref[0]
    q = x[:, 0:ATT_WIDTH]
    k = x[:, ATT_WIDTH:ATT_WIDTH + KV_WIDTH]
    v = x[:, ATT_WIDTH + KV_WIDTH:ATT_WIDTH + 2 * KV_WIDTH]
    inv_d = 1.0 / HEAD_DIM
    q = q * lax.rsqrt(_mm_exact_rhs(q * q, _head_blockdiag(ATT_WIDTH)) * inv_d + NORM_EPS) * qg_ref[...]
    k = k * lax.rsqrt(_mm_exact_rhs(k * k, _head_blockdiag(KV_WIDTH)) * inv_d + NORM_EPS) * kg_ref[...]
    if rope:
        cos, sin = cos_ref[...], sin_ref[...]
        rep = ATT_WIDTH // KV_WIDTH
        q = q * jnp.concatenate([cos] * rep, axis=1) + _swap_pairs(q) * jnp.concatenate([sin] * rep, axis=1)
        k = k * cos + _swap_pairs(k) * sin
    q_scr[...] = (q * (1.0 / math.sqrt(HEAD_DIM))).astype(BF16)
    for g in range(ATT_KV_HEADS):
        gs = slice(g * HEAD_DIM, (g + 1) * HEAD_DIM)
        if rope:
            k_scr[g, 0:past, :] = ck_ref[0, 0, g].astype(BF16)
            v_scr[g, 0:past, :] = cv_ref[0, 0, g].astype(BF16)
        else:
            kh_ref[0, 0, g] = k[:, gs].T
            vh_ref[0, 0, g] = v[:, gs].T
        k_scr[g, past:past + T, :] = k[:, gs].astype(BF16)
        v_scr[g, past:past + T, :] = v[:, gs].astype(BF16)
    QR = ATT_QROWS

    def q_block(qb, carry):
        rows = pl.ds(pl.multiple_of(qb * QR, QR), QR)
        qblk = q_scr[rows, :]
        for g in range(ATT_KV_HEADS):
            qs = jnp.concatenate([qblk[:, (g * ATT_REP + r) * HEAD_DIM:(g * ATT_REP + r + 1) * HEAD_DIM]
                                  for r in range(ATT_REP)], axis=0)
            s = lax.dot_general(qs, k_scr[g], _NT, preferred_element_type=F32)
            e = jnp.exp(s - jnp.max(s, axis=-1, keepdims=True))
            l = jnp.sum(e, axis=-1, keepdims=True)
            o = lax.dot_general(e.astype(BF16), v_scr[g], _NN, preferred_element_type=F32) / l
            for r in range(ATT_REP):
                h = g * ATT_REP + r
                out_ref[0, rows, h * HEAD_DIM:(h + 1) * HEAD_DIM] = o[r * QR:(r + 1) * QR]
        return carry

    lax.fori_loop(0, T // QR, q_block, 0)


def rope_tables(T):
    rows = T // GRID_W
    row = jnp.repeat(jnp.arange(rows, dtype=F32), GRID_W)
    col = jnp.tile(jnp.arange(GRID_W, dtype=F32), rows)
    n_freq = HEAD_DIM // 4
    inv = ROPE_THETA ** (-jnp.arange(n_freq, dtype=F32) / n_freq)
    ang = jnp.concatenate([row[:, None] * inv, col[:, None] * inv], axis=-1)
    cos = jnp.repeat(jnp.cos(ang), 2, axis=-1)
    sin = jnp.stack([-jnp.sin(ang), jnp.sin(ang)], axis=-1).reshape(T, HEAD_DIM)
    return jnp.tile(cos, (1, ATT_KV_HEADS)), jnp.tile(sin, (1, ATT_KV_HEADS))


def attention_pallas(p_att, qnorm_g, knorm_g, l, cache=None, kv_all=None):
    B, T, W = p_att.shape
    rope = cache is not None
    past = cache[0].shape[3] if rope else 0
    full = lambda shape: pl.BlockSpec(shape, lambda b: (0,) * len(shape))
    qg = jnp.tile(qnorm_g.reshape(1, HEAD_DIM), (1, ATT_HEADS))
    kg = jnp.tile(knorm_g.reshape(1, HEAD_DIM), (1, ATT_KV_HEADS))
    in_specs = [pl.BlockSpec((1, T, W), lambda b: (b, 0, 0)), full((1, ATT_WIDTH)), full((1, KV_WIDTH))]
    args = [p_att, qg, kg]
    out_specs = [pl.BlockSpec((1, T, ATT_WIDTH), lambda b: (b, 0, 0))]
    out_shape = [jax.ShapeDtypeStruct((B, T, ATT_WIDTH), F32)]
    if rope:
        cos, sin = rope_tables(T)
        kv_spec = pl.BlockSpec((1, 1, ATT_KV_HEADS, past, HEAD_DIM), lambda b: (b, l, 0, 0, 0))
        in_specs += [full((T, KV_WIDTH)), full((T, KV_WIDTH)), kv_spec, kv_spec]
        args += [cos, sin, cache[0], cache[1]]
        aliases = {}
    else:
        kv_spec = pl.BlockSpec((1, 1, ATT_KV_HEADS, HEAD_DIM, T), lambda b: (b, l, 0, 0, 0))
        in_specs += [pl.BlockSpec(memory_space=pl.ANY)] * 2
        args += list(kv_all)
        aliases = {3: 1, 4: 2}
        out_specs += [kv_spec, kv_spec]
        out_shape += [jax.ShapeDtypeStruct(kv_all[0].shape, F32)] * 2
    res = pl.pallas_call(
        partial(_att_kernel, T=T, past=past, rope=rope),
        grid=(B,),
        in_specs=in_specs,
        out_specs=out_specs,
        out_shape=out_shape,
        input_output_aliases=aliases,
        scratch_shapes=[pltpu.VMEM((ATT_KV_HEADS, past + T, HEAD_DIM), BF16)] * 2
        + [pltpu.VMEM((T, ATT_WIDTH), BF16)],
        compiler_params=pltpu.CompilerParams(dimension_semantics=("arbitrary",), vmem_limit_bytes=48 * 1024 * 1024),
        name="attention_rope" if rope else "attention_ctx",
    )(*args)
    return res[0] if rope else tuple(res)


ROW_TILE = 256
MOD_TILE = 1536
ROUTE_TILE = 256
MOE_SLOTS = 256
MOE_WEIGHT_SPLIT = 4
SC_ROWS = 32
SC_LANES = 128
P_ATT, P_RW, P_HG = ATT_WIDTH + 2 * KV_WIDTH, 3 * RW_WIDTH + 384, 5 * HG_WIDTH


def _mod_kernel(c_ref, w_ref, b_ref, o_ref):
    c = c_ref[...]
    o_ref[0] = _mm(c * _sigmoid(c), w_ref[0], passes=3) + b_ref[0]


def adaln_mod_pallas(cvec, w_mod, b_mod):
    n = 6 * D_MODEL
    return pl.pallas_call(
        _mod_kernel,
        grid=(DEPTH, n // MOD_TILE),
        in_specs=[pl.BlockSpec((8, D_MODEL), lambda l, j: (0, 0)),
                  pl.BlockSpec((1, D_MODEL, MOD_TILE), lambda l, j: (l, 0, j)),
                  pl.BlockSpec((1, 1, MOD_TILE), lambda l, j: (l, 0, j))],
        out_specs=pl.BlockSpec((1, 8, MOD_TILE), lambda l, j: (l, 0, j)),
        out_shape=jax.ShapeDtypeStruct((DEPTH, 8, n), F32),
        compiler_params=pltpu.CompilerParams(dimension_semantics=("arbitrary", "arbitrary"),
                                             vmem_limit_bytes=48 * 1024 * 1024),
        name="adaln_mod",
    )(cvec, w_mod, b_mod.reshape(DEPTH, 1, n))


def _rms(x):
    return x * lax.rsqrt(jnp.mean(x * x, axis=-1, keepdims=True) + NORM_EPS)


HALF = D_MODEL // 2
U32 = jnp.uint32


def _pack_rows(x):
    bits = lax.bitcast_convert_type(x.astype(BF16).astype(F32), U32)
    return (bits[:, :HALF] >> 16) | bits[:, HALF:]


def _unpack_rows(w):
    lo = lax.bitcast_convert_type(w << 16, F32)
    hi = lax.bitcast_convert_type(w & jnp.uint32(0xFFFF0000), F32)
    return lo, hi


def _moe_residual(x_ref, g_ref, g4_ref, pm_ref):
    y = None
    for k in range(TOP_K):
        t = g4_ref[:, k:k + 1] * jnp.concatenate(_unpack_rows(g_ref[k]), axis=1)
        y = t if y is None else y + t
    return x_ref[...] + pm_ref[0, 5:6, :] * y


def _in_kernel(*refs, has_res):
    if has_res:
        x_ref, gth_ref, g4_ref, pm_ref, m_ref, g_ref, w_ref, xo_ref, pa_ref, pr_ref, ph_ref = refs
        x = _moe_residual(x_ref, gth_ref, g4_ref, pm_ref)
        xo_ref[...] = x
    else:
        x_ref, m_ref, g_ref, w_ref, pa_ref, pr_ref, ph_ref = refs
        x = x_ref[...]
    h = _rms(x) * g_ref[...] * (1.0 + m_ref[0, 1:2, :]) + m_ref[0, 0:1, :]
    proj = lax.dot_general(h, w_ref[0], _NN, precision=lax.Precision.DEFAULT, preferred_element_type=F32)
    pa_ref[...] = proj[:, 0:P_ATT]
    pr_ref[...] = proj[:, P_ATT:P_ATT + P_RW]
    ph_ref[...] = proj[:, P_ATT + P_RW:]


def _res_specs(row0):
    t0 = row0 // ROW_TILE
    return [pl.BlockSpec((TOP_K, ROW_TILE, HALF), lambda i: (0, i, 0)),
            pl.BlockSpec((ROW_TILE, TOP_K), lambda i: (i + t0, 0))]


def in_proj_pallas(x, mod, norm_g, w_in, l, rows_per_mod, res=None):
    R = x.shape[0]
    tpm = rows_per_mod // ROW_TILE
    rt = lambda w: pl.BlockSpec((ROW_TILE, w), lambda i: (i, 0))
    ms = pl.BlockSpec((1, 6, D_MODEL), lambda i: (i // tpm, 0, 0))
    full = lambda shape, **kw: pl.BlockSpec(shape, lambda i: (0,) * len(shape), **kw)
    in_specs = [rt(D_MODEL)] + (_res_specs(res[2]) + [ms] if res else []) + [
        ms, full((1, D_MODEL)),
        pl.BlockSpec((1, D_MODEL, w_in.shape[2]), lambda i: (l, 0, 0), pipeline_mode=pl.Buffered(1))]
    args = [x] + ([res[0], res[1], res[3]] if res else []) + [mod, norm_g.reshape(1, D_MODEL), w_in]
    widths = ([D_MODEL] if res else []) + [P_ATT, P_RW, P_HG]
    return pl.pallas_call(
        partial(_in_kernel, has_res=res is not None),
        grid=(R // ROW_TILE,),
        in_specs=in_specs,
        out_specs=[rt(w) for w in widths],
        out_shape=[jax.ShapeDtypeStruct((R, w), F32) for w in widths],
        compiler_params=pltpu.CompilerParams(dimension_semantics=("arbitrary",), vmem_limit_bytes=48 * 1024 * 1024),
        name="in_proj",
    )(*args)


def _out_kernel(att_ref, rw_ref, hg_ref, x_ref, m_ref, g_ref, w_ref, rw_w_ref, rb_ref, *rest):
    xo_ref, h_ref, lg_ref = rest[-3:]
    d = lambda a, lo, hi: lax.dot_general(a, w_ref[0, lo:hi, :], _NN, precision=lax.Precision.DEFAULT,
                                          preferred_element_type=F32)
    mixo = (d(att_ref[...], 0, ATT_WIDTH) + d(rw_ref[...], ATT_WIDTH, ATT_WIDTH + RW_WIDTH)
            + d(hg_ref[...], ATT_WIDTH + RW_WIDTH, ATT_WIDTH + RW_WIDTH + HG_WIDTH))
    x = x_ref[...] + m_ref[0, 2:3, :] * mixo
    xo_ref[...] = x
    h = _rms(x) * g_ref[...] * (1.0 + m_ref[0, 4:5, :]) + m_ref[0, 3:4, :]
    h_ref[...] = _pack_rows(h)
    lg_ref[...] = _mm(h, rw_w_ref[...], passes=3) + rb_ref[...]


def out_proj_pallas(att, rw, hg, x, mod, norm_g, w_out, l, router_w, router_b, rows_per_mod, n_all, row0, joint=None):
    R = x.shape[0]
    tpm = rows_per_mod // ROW_TILE
    t0 = row0 // ROW_TILE
    rt = lambda w: pl.BlockSpec((ROW_TILE, w), lambda i: (i, 0))
    jt = lambda w: pl.BlockSpec((ROW_TILE, w), lambda i: (i + t0, 0))
    full = lambda shape: pl.BlockSpec(shape, lambda i: (0,) * len(shape))
    in_specs = [rt(ATT_WIDTH), rt(RW_WIDTH), rt(HG_WIDTH), rt(D_MODEL),
                pl.BlockSpec((1, 6, D_MODEL), lambda i: (i // tpm, 0, 0)), full((1, D_MODEL)),
                pl.BlockSpec((1, D_MODEL, D_MODEL), lambda i: (l, 0, 0)), full((D_MODEL, N_EXPERTS)),
                full((1, N_EXPERTS))]
    args = [att, rw, hg, x, mod, norm_g.reshape(1, D_MODEL), w_out, router_w, router_b.reshape(1, N_EXPERTS)]
    aliases = {}
    if joint is not None:
        in_specs += [pl.BlockSpec(memory_space=pl.ANY)] * 2
        aliases = {len(args): 1, len(args) + 1: 2}
        args += list(joint)
    return pl.pallas_call(
        _out_kernel,
        grid=(R // ROW_TILE,),
        in_specs=in_specs,
        out_specs=[rt(D_MODEL), jt(HALF), jt(N_EXPERTS)],
        out_shape=[jax.ShapeDtypeStruct((R, D_MODEL), F32), jax.ShapeDtypeStruct((n_all, HALF), U32),
                   jax.ShapeDtypeStruct((n_all, N_EXPERTS), F32)],
        input_output_aliases=aliases,
        compiler_params=pltpu.CompilerParams(dimension_semantics=("arbitrary",), vmem_limit_bytes=48 * 1024 * 1024),
        name="out_proj",
    )(*args)


def _final_kernel(x_ref, gth_ref, g4_ref, m_ref, g_ref, o_ref):
    o_ref[...] = _rms(_moe_residual(x_ref, gth_ref, g4_ref, m_ref)) * g_ref[...]


def final_norm_pallas(x, gathered, gate4, row0, mod, norm_g, rows_per_mod):
    R = x.shape[0]
    tpm = rows_per_mod // ROW_TILE
    rt = pl.BlockSpec((ROW_TILE, D_MODEL), lambda i: (i, 0))
    return pl.pallas_call(
        _final_kernel,
        grid=(R // ROW_TILE,),
        in_specs=[rt] + _res_specs(row0) + [pl.BlockSpec((1, 6, D_MODEL), lambda i: (i // tpm, 0, 0)),
                                            pl.BlockSpec((1, D_MODEL), lambda i: (0, 0))],
        out_specs=rt,
        out_shape=jax.ShapeDtypeStruct((R, D_MODEL), F32),
        name="final_norm",
    )(x, gathered, gate4, mod, norm_g.reshape(1, D_MODEL))


def _moe_max_blocks(n_tok):
    return (n_tok * TOP_K + N_EXPERTS * (MOE_SLOTS - 1)) // MOE_SLOTS


def _exact_nt_ones(a):
    ones = jnp.ones((8, a.shape[1]), BF16)
    out = None
    for t in reversed(_split(a, 3)):
        y = lax.dot_general(ones, t, _NT, preferred_element_type=F32)
        out = y if out is None else out + y
    return out


def _route_kernel(lg_ref, dest_ref, gate4_ref, blk_ref, rank_scr, gate_scr, *, n_tok):
    Rt, E = ROUTE_TILE, N_EXPERTS
    n_tiles = n_tok // Rt
    ti = lax.broadcasted_iota(jnp.int32, (Rt, Rt), 0)
    si = lax.broadcasted_iota(jnp.int32, (Rt, Rt), 1)
    before_t = (ti < si).astype(BF16)
    eye_t = (ti == si).astype(BF16)
    ei = lax.broadcasted_iota(jnp.int32, (E, E), 0)
    ej = lax.broadcasted_iota(jnp.int32, (E, E), 1)
    before_e = (ei > ej).astype(BF16)
    sub = lax.broadcasted_iota(jnp.int32, (E, Rt), 0)
    d = lambda a, b, dims: lax.dot_general(a, b, dims, preferred_element_type=F32)

    def tile_members(it, off):
        rows = pl.ds(pl.multiple_of(it * Rt, Rt), Rt)
        l3, l2, l1 = reversed(_split(lg_ref[rows, :], 3))
        lgT = d(l3, eye_t, _TN) + d(l2, eye_t, _TN) + d(l1, eye_t, _TN)
        work = lgT
        member = jnp.zeros((E, Rt), jnp.bool_)
        top = None
        for k in range(TOP_K):
            m = jnp.max(work, axis=0, keepdims=True)
            if top is None:
                top = m
            first = jnp.min(jnp.where(work == m, sub, E), axis=0, keepdims=True)
            pick = sub == first
            member = member | pick
            work = jnp.where(pick, -jnp.inf, work)
        ex = jnp.where(member, jnp.exp(lgT - top), 0.0)
        gate_scr[:, rows] = ex / jnp.sum(ex, axis=0, keepdims=True)
        mem = member.astype(BF16)
        rank = d(mem, before_t, _NN) + off
        rank_scr[:, rows] = jnp.where(member, rank, -1.0)
        return off + jnp.sum(mem.astype(F32), axis=1, keepdims=True)

    count = lax.fori_loop(0, n_tiles, tile_members, jnp.zeros((E, 1), F32))
    nblk = jnp.floor((count + (MOE_SLOTS - 1)) * (1.0 / MOE_SLOTS))
    bstart = d(before_e, jnp.broadcast_to(nblk, (E, 128)).astype(BF16), _NN)[:, 0:1]
    bend = bstart + nblk
    pstart = bstart * MOE_SLOTS

    def tile_slots(it, carry):
        rows = pl.ds(pl.multiple_of(it * Rt, Rt), Rt)
        rank = rank_scr[:, rows]
        gate = gate_scr[:, rows]
        member = rank >= 0.0
        kidx = d(before_e, member.astype(BF16), _NN)
        slot = pstart + rank
        grows = []
        for k in range(TOP_K):
            sel = member & (kidx == k)
            dest_ref[k:k + 1, rows] = jnp.sum(jnp.where(sel, slot, 0.0), axis=0, keepdims=True).astype(jnp.int32)
            grows.append(jnp.sum(jnp.where(sel, gate, 0.0), axis=0, keepdims=True))
        g4t = jnp.concatenate(grows + [jnp.zeros((128 - TOP_K, Rt), F32)], axis=0)
        g3, g2, g1 = reversed(_split(g4t, 3))
        g4 = d(eye_t, g3, _NT) + d(eye_t, g2, _NT) + d(eye_t, g1, _NT)
        gate4_ref[rows, :] = g4[:, 0:TOP_K]
        return carry

    lax.fori_loop(0, n_tiles, tile_slots, 0)
    nb = blk_ref.shape[1]
    bi = lax.broadcasted_iota(jnp.int32, (E, nb), 1).astype(F32)
    owner = jnp.sum((bend <= bi).astype(F32), axis=0, keepdims=True)
    blk_ref[0:1, :] = jnp.minimum(owner, E - 1.0).astype(jnp.int32)
    blk_ref[1:2, :] = jnp.broadcast_to(jnp.sum(nblk, axis=0, keepdims=True), (1, nb)).astype(jnp.int32)
    blk_ref[2:8, :] = jnp.zeros((6, nb), jnp.int32)


def moe_route_pallas(logits):
    n_tok = logits.shape[0]
    nb = -(-_moe_max_blocks(n_tok) // 128) * 128
    return pl.pallas_call(
        partial(_route_kernel, n_tok=n_tok),
        out_shape=[jax.ShapeDtypeStruct((TOP_K, n_tok), jnp.int32),
                   jax.ShapeDtypeStruct((n_tok, TOP_K), F32),
                   jax.ShapeDtypeStruct((8, nb), jnp.int32)],
        scratch_shapes=[pltpu.VMEM((N_EXPERTS, n_tok), F32)] * 2,
        name="moe_route",
    )(logits)


def _moe_block_kernel(be_ref, nu_ref, first_ref, par_ref, nxt_ref, xb_ref, wgu_hbm, bgu_ref, wdn_hbm, bdn_ref, yb_ref,
                      wgu_buf, wdn_buf, sem, *, l):
    i = pl.program_id(0)

    def weight_copies(e, slot):
        cps = []
        for hbm, buf, s in ((wgu_hbm, wgu_buf, 0), (wdn_hbm, wdn_buf, 1)):
            rows = buf.shape[1] // MOE_WEIGHT_SPLIT
            for j in range(MOE_WEIGHT_SPLIT):
                rs = pl.ds(j * rows, rows)
                cps.append(pltpu.make_async_copy(hbm.at[l, e, rs], buf.at[slot, rs], sem.at[s, slot]))
        return cps

    @pl.when(i < nu_ref[0])
    def _():
        slot = par_ref[i]

        @pl.when(first_ref[i] == 1)
        def _():
            @pl.when(i == 0)
            def _():
                for cp in weight_copies(be_ref[0], slot):
                    cp.start()

            for cp in weight_copies(be_ref[i], slot):
                cp.wait()

            @pl.when(nxt_ref[i] >= 0)
            def _():
                for cp in weight_copies(nxt_ref[i], 1 - slot):
                    cp.start()

        dot = lambda a, w: lax.dot_general(a, w, _NN, precision=lax.Precision.DEFAULT, preferred_element_type=F32)
        x_lo, x_hi = _unpack_rows(xb_ref[...])
        gu = dot(x_lo, wgu_buf[slot, 0:HALF]) + dot(x_hi, wgu_buf[slot, HALF:D_MODEL]) + bgu_ref[0, 0]
        glu = jnp.minimum(gu[:, :EXPERT_FF], SWIGLU_LIMIT)
        lin = jnp.clip(gu[:, EXPERT_FF:], -SWIGLU_LIMIT, SWIGLU_LIMIT)
        act = glu * _sigmoid(SWIGLU_ALPHA * glu) * (lin + 1.0)
        yb_ref[...] = _pack_rows(dot(act, wdn_buf[slot]) + bdn_ref[0, 0])


def _expert_runs(block_e, n_used):
    n = block_e.shape[0]
    idx = jnp.arange(n, dtype=jnp.int32)
    valid = idx < n_used[0]
    first = valid & ((idx == 0) | (block_e != jnp.roll(block_e, 1)))
    par = (jnp.cumsum(first.astype(jnp.int32)) - 1) % 2
    start = jnp.where(first, idx, n)
    nxt_start = lax.cummin(jnp.concatenate([start[1:], jnp.full((1,), n, jnp.int32)]), reverse=True)
    nxt = jnp.where(nxt_start < n, block_e[jnp.minimum(nxt_start, n - 1)], -1)
    return first.astype(jnp.int32), par.astype(jnp.int32), nxt.astype(jnp.int32)


def moe_blocks_pallas(xb, block_e, n_used, l, w_gu, b_gu, w_down, b_down):
    n_blocks = xb.shape[0] // MOE_SLOTS
    first, par, nxt = _expert_runs(block_e, n_used)
    blk = lambda i, be, nu, *_: (jnp.minimum(i, nu[0] - 1), 0)
    bsel = lambda i, be, nu, *_: (l, be[jnp.minimum(i, nu[0] - 1)], 0, 0)
    grid_spec = pltpu.PrefetchScalarGridSpec(
        num_scalar_prefetch=5,
        grid=(n_blocks,),
        in_specs=[pl.BlockSpec((MOE_SLOTS, HALF), blk),
                  pl.BlockSpec(memory_space=pl.ANY),
                  pl.BlockSpec((1, 1, 1, 2 * EXPERT_FF), bsel),
                  pl.BlockSpec(memory_space=pl.ANY),
                  pl.BlockSpec((1, 1, 1, D_MODEL), bsel)],
        out_specs=pl.BlockSpec((MOE_SLOTS, HALF), blk),
        scratch_shapes=[pltpu.VMEM((2, D_MODEL, 2 * EXPERT_FF), F32), pltpu.VMEM((2, EXPERT_FF, D_MODEL), F32),
                        pltpu.SemaphoreType.DMA((2, 2))],
    )
    return pl.pallas_call(
        partial(_moe_block_kernel, l=l),
        grid_spec=grid_spec,
        out_shape=jax.ShapeDtypeStruct(xb.shape, U32),
        compiler_params=pltpu.CompilerParams(dimension_semantics=("arbitrary",), vmem_limit_bytes=48 * 1024 * 1024),
        name="moe_blocks",
    )(block_e, n_used, first, par, nxt, xb, w_gu, b_gu.reshape(DEPTH, N_EXPERTS, 1, 2 * EXPERT_FF), w_down,
      b_down.reshape(DEPTH, N_EXPERTS, 1, D_MODEL))


def _sc_mesh():
    return plsc.VectorSubcoreMesh(core_axis_name="c", subcore_axis_name="s")


def _sc_index_rows(idx):
    return jnp.pad(idx.reshape(-1, SC_ROWS), ((0, 0), (0, SC_LANES - SC_ROWS)))


def sc_dispatch(h, dest, n_rows):
    n_tok, d = h.shape
    idx = [_sc_index_rows(dest[k]) for k in range(TOP_K)]

    @pl.kernel(out_type=jax.ShapeDtypeStruct((n_rows, d), h.dtype), mesh=_sc_mesh(), scratch_types=[])
    def kern(h_hbm, i0, i1, i2, i3, o_hbm):
        def body(x_vmem, *i_vmem):
            for iv in i_vmem:
                pltpu.sync_copy(x_vmem, o_hbm.at[iv.at[0, pl.ds(0, SC_ROWS)]])

        pltpu.emit_pipeline(
            body,
            grid=(n_tok // SC_ROWS,),
            in_specs=[pl.BlockSpec((SC_ROWS, d), lambda i: (i, 0))]
            + [pl.BlockSpec((1, SC_LANES), lambda i: (i, 0))] * TOP_K,
            out_specs=[],
            core_axis_name=("c", "s"),
            dimension_semantics=(pltpu.PARALLEL,),
        )(h_hbm, i0, i1, i2, i3)

    return kern(h, *idx)


def sc_combine_gather(yb, dest):
    n_tok = dest.shape[1]
    d = yb.shape[1]
    idx = _sc_index_rows(dest.reshape(TOP_K * n_tok))

    @pl.kernel(out_type=jax.ShapeDtypeStruct((TOP_K * n_tok, d), yb.dtype), mesh=_sc_mesh(), scratch_types=[])
    def kern(y_hbm, i_hbm, o_hbm):
        def body(i_vmem, o_vmem):
            pltpu.sync_copy(y_hbm.at[i_vmem.at[0, pl.ds(0, SC_ROWS)]], o_vmem)

        pltpu.emit_pipeline(
            body,
            grid=(TOP_K * n_tok // SC_ROWS,),
            in_specs=[pl.BlockSpec((1, SC_LANES), lambda i: (i, 0))],
            out_specs=[pl.BlockSpec((SC_ROWS, d), lambda i: (i, 0))],
            core_axis_name=("c", "s"),
            dimension_semantics=(pltpu.PARALLEL,),
        )(i_hbm, o_hbm)

    return kern(yb, idx).reshape(TOP_K, n_tok, d)


def hgrn_lower_bounds(hg_lb):
    sm = jax.nn.softmax(hg_lb.astype(jnp.float32), axis=0)
    return jnp.cumsum(sm, axis=0) - sm[0:1]


def kernel(x_prompt, x_sample, cache_att_k, cache_att_v, state_rwkv, state_hgrn, c, c_ctx, w_mod, b_mod, norm_mix_g, norm_ffn_g, w_in, w_out, att_qnorm_g, att_knorm_g, rw_w0, rw_w2, rw_a0, rw_a2, rw_g2, rw_kk, rw_ka, rw_rk, rw_gn_g, rw_gn_b, hg_lb, hg_norm_g, router_w, router_b, moe_w_gu, moe_b_gu, moe_w_down, moe_b_down, final_norm_g):
    BP, TP, _ = x_prompt.shape
    BS, TS, _ = x_sample.shape
    n_p, n_s = BP * TP, BS * TS
    lb_all = hgrn_lower_bounds(hg_lb)
    cvec = jnp.concatenate([c_ctx[None, :], c, jnp.zeros((8 - 1 - BS, D_MODEL), F32)], axis=0)
    mod_all = adaln_mod_pallas(cvec, w_mod, b_mod).reshape(DEPTH, 8, 6, D_MODEL)
    zeros_state = jnp.zeros((BP, 1, 2, RW_HEADS, HEAD_DIM, HEAD_DIM), F32)
    kv_all = (jnp.zeros((BP, DEPTH, ATT_KV_HEADS, HEAD_DIM, TP), F32),) * 2
    rw_states = jnp.zeros((BP, DEPTH, 2, RW_HEADS, HEAD_DIM, HEAD_DIM), F32)
    hg_states = jnp.zeros((BP, DEPTH, 2, HG_HEADS, HEAD_DIM, HEAD_DIM), F32)
    x = {'p': x_prompt.reshape(n_p, D_MODEL), 's': x_sample.reshape(n_s, D_MODEL)}
    dims = {'p': (TP, BP, n_p, 0), 's': (TS, BS, TS, n_p)}
    moe_out, mod_prev = None, None
    for l in range(DEPTH):
        prm = dict(rw_w0=rw_w0[l], rw_w2=rw_w2[l], rw_a0=rw_a0[l], rw_a2=rw_a2[l], rw_g2=rw_g2[l],
                   rw_kk=rw_kk[l], rw_ka=rw_ka[l], rw_rk=rw_rk[l], rw_gn_g=rw_gn_g[l], rw_gn_b=rw_gn_b[l])
        mods = {'p': mod_all[l, 0:1], 's': mod_all[l, 1:1 + BS]}
        joint = (jnp.zeros((n_p + n_s, HALF), U32), jnp.zeros((n_p + n_s, N_EXPERTS), F32))
        for s in ('p', 's'):
            T, B, rpm, row0 = dims[s]
            if l == 0:
                p_att, p_rw, p_hg = in_proj_pallas(x[s], mods[s], norm_mix_g[l], w_in, l, rpm)
            else:
                x[s], p_att, p_rw, p_hg = in_proj_pallas(x[s], mods[s], norm_mix_g[l], w_in, l, rpm,
                                                         res=(*moe_out[s], row0, mod_prev[s]))
            p_att, p_rw, p_hg = (t.reshape(B, T, -1) for t in (p_att, p_rw, p_hg))
            if s == 'p':
                att, *kv_all = attention_pallas(p_att, att_qnorm_g[l], att_knorm_g[l], l, kv_all=kv_all)
                rw_out, rw_states = rwkv7_mixer_pallas(p_rw, zeros_state, 0, prm, rw_states, l)
                hg_out, hg_states = hgrn2_mixer_pallas(p_hg, zeros_state, 0, lb_all[l], hg_norm_g[l], hg_states, l)
            else:
                att = attention_pallas(p_att, att_qnorm_g[l], att_knorm_g[l], l, cache=(cache_att_k, cache_att_v))
                rw_out, _ = rwkv7_mixer_pallas(p_rw, state_rwkv, l, prm)
                hg_out, _ = hgrn2_mixer_pallas(p_hg, state_hgrn, l, lb_all[l], hg_norm_g[l])
            x[s], *joint = out_proj_pallas(att.reshape(B * T, -1), rw_out.reshape(B * T, -1),
                                           hg_out.reshape(B * T, -1), x[s], mods[s], norm_ffn_g[l], w_out, l,
                                           router_w[l], router_b[l], rpm, n_p + n_s, row0, joint)
        h_all, logits_all = joint
        dest, gate4, blk = moe_route_pallas(logits_all)
        xb = sc_dispatch(h_all, dest, _moe_max_blocks(n_p + n_s) * MOE_SLOTS)
        yb = moe_blocks_pallas(xb, blk[0], blk[1, :1], l, moe_w_gu, moe_b_gu, moe_w_down, moe_b_down)
        moe_out = {'p': (sc_combine_gather(yb, dest[:, :n_p]), gate4), 's': (sc_combine_gather(yb, dest[:, n_p:]), gate4)}
        mod_prev = mods
    y_prompt = final_norm_pallas(x['p'], *moe_out['p'], 0, mod_prev['p'], final_norm_g, n_p)
    y_sample = final_norm_pallas(x['s'], *moe_out['s'], n_p, mod_prev['s'], final_norm_g, TS)
    return (y_prompt.reshape(x_prompt.shape), y_sample.reshape(x_sample.shape),
            jnp.swapaxes(kv_all[0], 3, 4), jnp.swapaxes(kv_all[1], 3, 4),
            rw_states, hg_states)
```

```python
import math
from functools import partial

import jax
import jax.numpy as jnp
from jax import lax
from jax.experimental import pallas as pl
from jax.experimental.pallas import tpu as pltpu
from jax.experimental.pallas import tpu_sc as plsc

D_MODEL = 1024
DEPTH = 2
GRID_W = 64
HEAD_DIM = 64
ATT_HEADS = 8
ATT_KV_HEADS = 2
ATT_WIDTH = ATT_HEADS * HEAD_DIM
KV_WIDTH = ATT_KV_HEADS * HEAD_DIM
RW_HEADS = 4
RW_WIDTH = RW_HEADS * HEAD_DIM
RW_GN_EPS = 64e-5
HG_HEADS = 4
HG_WIDTH = HG_HEADS * HEAD_DIM
HG_F_MIN = 1e-6
N_EXPERTS = 32
TOP_K = 4
EXPERT_FF = D_MODEL
SWIGLU_LIMIT = 7.0
SWIGLU_ALPHA = 1.702
ROPE_THETA = 10000.0
NORM_EPS = 1e-6

RW_CHUNK = 64
BF16 = jnp.bfloat16
F32 = jnp.float32

_NN = (((1,), (0,)), ((), ()))
_NT = (((1,), (1,)), ((), ()))
_TN = (((0,), (0,)), ((), ()))


def _split(x, n):
    parts = []
    for _ in range(n - 1):
        hi = x.astype(BF16)
        parts.append(hi)
        x = x - hi.astype(F32)
    parts.append(x.astype(BF16))
    return parts


def _mm(a, b, dims=_NN, passes=1):
    d = lambda x, y: lax.dot_general(x, y, dims, preferred_element_type=F32)
    if passes == 1:
        return d(a.astype(BF16), b.astype(BF16))
    ah, al = _split(a, 2)
    bh, bl = _split(b, 2)
    return d(ah, bl) + d(al, bh) + d(ah, bh)


def _mm_exact_lhs(a01, b, n=3):
    a = a01.astype(BF16)
    out = None
    for t in reversed(_split(b, n)):
        y = lax.dot_general(a, t, _NN, preferred_element_type=F32)
        out = y if out is None else out + y
    return out


def _mm_exact_rhs(a, b01, n=3):
    b = b01.astype(BF16)
    out = None
    for t in reversed(_split(a, n)):
        y = lax.dot_general(t, b, _NN, preferred_element_type=F32)
        out = y if out is None else out + y
    return out


def _head_blockdiag(width):
    r = lax.broadcasted_iota(jnp.int32, (width, width), 0) // HEAD_DIM
    c = lax.broadcasted_iota(jnp.int32, (width, width), 1) // HEAD_DIM
    return (r == c).astype(F32)


def _sigmoid(x):
    return 1.0 / (1.0 + jnp.exp(-x))


def _softplus(x):
    return jnp.maximum(x, 0.0) + jnp.log(1.0 + jnp.exp(-jnp.abs(x)))


def _rwkv_kernel(p_ref, s0_ref, w0_ref, w2_ref, a0_ref, a2_ref, g2_ref, kk_ref, ka_ref, rk_ref, gng_ref, gnb_ref,
                 *rest, T, NB):
    out_ref, st_ref, lw_scr, kd_scr, bb_scr, y_scr, kk_scr, s_scr = rest[-8:]
    C = RW_CHUNK
    n_chunks = T // C
    bd = _head_blockdiag(RW_WIDTH)
    seg = lambda t: _mm_exact_rhs(t, bd, n=2)
    ka = ka_ref[...]
    for nb in range(NB):
        k = p_ref[nb, :, 256:512]
        kk = k * kk_ref[...]
        kk = kk * lax.rsqrt(seg(kk * kk) + 1e-12)
        kk_scr[nb] = kk
        for d in range(2):
            wd = p_ref[nb, :, 768 + 64 * d:832 + 64 * d]
            ad = p_ref[nb, :, 896 + 64 * d:960 + 64 * d]
            w_raw = w0_ref[d:d + 1, :] + _mm(jnp.tanh(wd), w2_ref[d], passes=3)
            lw_scr[nb, d] = -jnp.exp(-_softplus(-w_raw) - 0.5)
            a = _sigmoid(a0_ref[d:d + 1, :] + _mm(ad, a2_ref[d], passes=3))
            kd_scr[nb, d] = k * (1.0 + (a - 1.0) * ka)
            bb_scr[nb, d] = kk * a
    s_scr[...] = s0_ref[:, 0]

    ti = lax.broadcasted_iota(jnp.int32, (C, C), 0)
    si = lax.broadcasted_iota(jnp.int32, (C, C), 1)
    ones_cc = jnp.ones((C, C), F32)

    def chunk_body(i, carry):
        ch = []
        for nb, d in [(nb, d) for nb in range(NB) for d in range(2)]:
            ci = i if d == 0 else n_chunks - 1 - i
            rows = pl.ds(pl.multiple_of(ci * C, C), C)
            strict = (ti > si) if d == 0 else (ti < si)
            incl = (ti >= si) if d == 0 else (ti <= si)
            lw = lw_scr[nb, d, rows, :]
            cum = _mm_exact_lhs(incl.astype(F32), lw)
            total = _mm_exact_lhs(ones_cc, lw)
            cum_ex = cum - lw
            mid = 0.5 * total
            rr = p_ref[nb, rows, 0:256]
            vv = p_ref[nb, rows, 512:768]
            kdc = kd_scr[nb, d, rows, :]
            bbc = bb_scr[nb, d, rows, :]
            kkc = kk_scr[nb, rows, :]
            e_inv = jnp.exp(mid - cum)
            At = -kkc * jnp.exp(cum_ex - mid)
            Rt = rr * jnp.exp(cum - mid)
            Bt = bbc * e_inv
            Kt = kdc * e_inv
            Ap = -kkc * jnp.exp(cum_ex)
            Rp = rr * jnp.exp(cum)
            e_out = jnp.exp(total - cum)
            Bh = bbc * e_out
            Kh = kdc * e_out
            e_tot = jnp.exp(total[0:1, :])
            for h in range(RW_HEADS):
                hs = slice(h * HEAD_DIM, (h + 1) * HEAD_DIM)
                ch.append(dict(nb=nb, d=d, h=h, rows=rows, hs=hs, strict=strict, incl=incl,
                               AR=jnp.concatenate([At[:, hs], Rt[:, hs]], axis=0), Bt=Bt[:, hs], Kt=Kt[:, hs],
                               V=vv[:, hs], X1=Ap[:, hs], Rp=Rp[:, hs], Bh=Bh[:, hs], Kh=Kh[:, hs],
                               e_tot=e_tot[:, hs]))
        for c in ch:
            c['AB'] = _mm(c['AR'], c['Bt'], _NT)
            c['AK'] = _mm(c['AR'], c['Kt'], _NT)
        for c in ch:
            c['P'] = jnp.where(c['strict'], c['AB'][:C], 0.0)
            c['A_ak'] = jnp.where(c['strict'], c['AK'][:C], 0.0)
            c['A_rb'] = jnp.where(c['incl'], c['AB'][C:], 0.0)
            c['A_rk'] = jnp.where(c['incl'], c['AK'][C:], 0.0)
        for c in ch:
            c['X2'] = _mm(c['A_ak'], c['V'])
        for lvl in range(6):
            for c in ch:
                if lvl < 5:
                    c['PZ'] = _mm(c['P'], jnp.concatenate([c['P'], c['X1'], c['X2']], axis=1))
                else:
                    c['PZ'] = _mm(c['P'], jnp.concatenate([c['X1'], c['X2']], axis=1))
            for c in ch:
                PZ = c['PZ']
                if lvl < 5:
                    c['P'] = PZ[:, :C]
                    c['X1'] = c['X1'] + PZ[:, C:2 * C]
                    c['X2'] = c['X2'] + PZ[:, 2 * C:]
                else:
                    c['X1'] = c['X1'] + PZ[:, :C]
                    c['X2'] = c['X2'] + PZ[:, C:]
        for c in ch:
            c['S0'] = s_scr[c['nb'], c['d'], c['h']]
            c['UY'] = _mm(jnp.concatenate([c['X1'], c['Rp']], axis=0), c['S0'], _NT)
        for c in ch:
            c['U'] = c['UY'][:C] + c['X2']
        for c in ch:
            c['Y'] = c['UY'][C:] + _mm(c['A_rb'], c['U']) + _mm(c['A_rk'], c['V'])
            c['S1'] = c['S0'] * c['e_tot'] + _mm(c['U'], c['Bh'], _TN) + _mm(c['V'], c['Kh'], _TN)
        for c in ch:
            s_scr[c['nb'], c['d'], c['h']] = c['S1']
            y_scr[c['nb'], c['d'], c['rows'], c['hs']] = c['Y']
        return carry

    lax.fori_loop(0, n_chunks, chunk_body, 0)

    for nb in range(NB):
        r = p_ref[nb, :, 0:256]
        v = p_ref[nb, :, 512:768]
        bonus = seg(r * (kd_scr[nb, 0] + kd_scr[nb, 1]) * rk_ref[...]) * v
        g = _mm(_sigmoid(p_ref[nb, :, 1024:1152]), g2_ref[...], passes=3)
        y = y_scr[nb, 0] + y_scr[nb, 1]
        mu = seg(y) * (1.0 / HEAD_DIM)
        yc = y - mu
        var = seg(yc * yc) * (1.0 / HEAD_DIM)
        yn = yc * lax.rsqrt(var + RW_GN_EPS)
        out_ref[nb] = (yn * gng_ref[...] + gnb_ref[...] + bonus) * g
    st_ref[:, 0] = s_scr[...]


RW_ROWS = 1024


def _state_spec(nb, layer):
    return pl.BlockSpec((nb, 1, 2, RW_HEADS, HEAD_DIM, HEAD_DIM), lambda b: (b, layer, 0, 0, 0, 0))


def rwkv7_mixer_pallas(p_rw, s0, l_in, prm, st_all=None, l_out=0):
    B, T, W = p_rw.shape
    NB = max(2, RW_ROWS // T)
    row = lambda a: a.reshape(1, RW_WIDTH)
    full = lambda shape: pl.BlockSpec(shape, lambda b: (0,) * len(shape))
    st_shape = (B, 1, 2, RW_HEADS, HEAD_DIM, HEAD_DIM) if st_all is None else st_all.shape
    return pl.pallas_call(
        partial(_rwkv_kernel, T=T, NB=NB),
        grid=(B // NB,),
        in_specs=[pl.BlockSpec((NB, T, W), lambda b: (b, 0, 0)), _state_spec(NB, l_in),
                  full((2, RW_WIDTH)), full((2, 64, RW_WIDTH)), full((2, RW_WIDTH)), full((2, 64, RW_WIDTH)),
                  full((128, RW_WIDTH)), full((1, RW_WIDTH)), full((1, RW_WIDTH)), full((1, RW_WIDTH)),
                  full((1, RW_WIDTH)), full((1, RW_WIDTH))]
        + ([] if st_all is None else [pl.BlockSpec(memory_space=pl.ANY)]),
        out_specs=[pl.BlockSpec((NB, T, RW_WIDTH), lambda b: (b, 0, 0)), _state_spec(NB, l_out)],
        out_shape=[jax.ShapeDtypeStruct((B, T, RW_WIDTH), F32), jax.ShapeDtypeStruct(st_shape, F32)],
        input_output_aliases={} if st_all is None else {12: 1},
        scratch_shapes=[pltpu.VMEM((NB, 2, T, RW_WIDTH), F32)] * 4
        + [pltpu.VMEM((NB, T, RW_WIDTH), F32), pltpu.VMEM((NB, 2, RW_HEADS, HEAD_DIM, HEAD_DIM), F32)],
        compiler_params=pltpu.CompilerParams(dimension_semantics=("arbitrary",), vmem_limit_bytes=56 * 1024 * 1024),
        name="rwkv7_mixer",
    )(p_rw, s0, prm['rw_w0'], prm['rw_w2'], prm['rw_a0'], prm['rw_a2'], prm['rw_g2'], row(prm['rw_kk']),
      row(prm['rw_ka']), row(prm['rw_rk']), row(prm['rw_gn_g']), row(prm['rw_gn_b']),
      *([] if st_all is None else [st_all]))


HG_SUB = 16
HG_ROWS = 64


def _hgrn_kernel(p_ref, s0_ref, lb_ref, ng_ref, *rest, T):
    out_ref, st_ref, lf_scr, kf_scr, o_scr, s_scr = rest[-6:]
    R, c = HG_ROWS, HG_SUB
    n_it = T // R
    x = p_ref[0]
    bd = _head_blockdiag(HG_WIDTH)
    seg = lambda t: _mm_exact_rhs(t, bd, n=2)
    for d in range(2):
        lbd = lb_ref[d:d + 1, :]
        f = lbd + (1.0 - lbd) * _sigmoid(x[:, 256 + 256 * d:512 + 256 * d])
        lf_scr[d] = jnp.log(jnp.maximum(f, HG_F_MIN))
        kf_scr[d] = 1.0 - f
        for h in range(HG_HEADS):
            s_scr[d, h] = s0_ref[0, 0, d, h].T

    ti = lax.broadcasted_iota(jnp.int32, (R, R), 0)
    si = lax.broadcasted_iota(jnp.int32, (R, R), 1)
    same_blk = (ti // c) == (si // c)
    t16 = lax.broadcasted_iota(jnp.int32, (c, 1), 0)

    def body(i, carry):
        for d in range(2):
            ci = i if d == 0 else n_it - 1 - i
            rows = pl.ds(pl.multiple_of(ci * R, R), R)
            incl = (ti >= si) if d == 0 else (ti <= si)
            lf = lf_scr[d, rows, :]
            cum = _mm_exact_lhs((incl & same_blk).astype(F32), lf)
            tot = _mm_exact_lhs(same_blk.astype(F32), lf)
            xq = p_ref[0, rows, 0:256]
            q = xq * _sigmoid(xq)
            v = p_ref[0, rows, 768:1024]
            kf = kf_scr[d, rows, :]
            Qp = q * jnp.exp(cum)
            Kh = kf * jnp.exp(tot - cum)
            e_tot = jnp.exp(tot)
            blocks = range(R // c) if d == 0 else range(R // c - 1, -1, -1)
            o_parts = [None] * (R // c)
            for j in blocks:
                rs = slice(j * c, (j + 1) * c)
                cb, qb, kb, vb = cum[rs], q[rs], kf[rs], v[rs]
                prods = []
                for s in range(c):
                    e = jnp.exp(jnp.minimum(cb - cb[s:s + 1, :], 0.0))
                    prods.append(qb * (kb[s:s + 1, :] * e))
                att = _mm_exact_rhs(jnp.concatenate(prods, axis=0), bd, n=1)
                o_blk = jnp.zeros((c, HG_WIDTH), F32)
                for s in range(c):
                    keep = (t16 >= s) if d == 0 else (t16 <= s)
                    o_blk = o_blk + jnp.where(keep, att[s * c:(s + 1) * c], 0.0) * vb[s:s + 1, :]
                o_heads = []
                for h in range(HG_HEADS):
                    hs = slice(h * HEAD_DIM, (h + 1) * HEAD_DIM)
                    ST = s_scr[d, h]
                    o_heads.append(_mm(Qp[rs, hs], ST, _NT))
                    s_scr[d, h] = ST * e_tot[j * c:j * c + 1, hs] + _mm(vb[:, hs], Kh[rs, hs], _TN)
                o_parts[j] = o_blk + jnp.concatenate(o_heads, axis=1)
            o_scr[d, rows, :] = jnp.concatenate(o_parts, axis=0)
        return carry

    lax.fori_loop(0, n_it, body, 0)

    o = o_scr[0] + o_scr[1]
    o = o * lax.rsqrt(seg(o * o) * (1.0 / HEAD_DIM) + NORM_EPS) * ng_ref[...]
    gg = x[:, 1024:1280]
    out_ref[0] = o * (gg * _sigmoid(gg))
    for d in range(2):
        for h in range(HG_HEADS):
            st_ref[0, 0, d, h] = s_scr[d, h].T


def hgrn2_mixer_pallas(p_hg, s0, l_in, lb, norm_g, st_all=None, l_out=0):
    B, T, W = p_hg.shape
    full = lambda shape: pl.BlockSpec(shape, lambda b: (0,) * len(shape))
    st_shape = (B, 1, 2, HG_HEADS, HEAD_DIM, HEAD_DIM) if st_all is None else st_all.shape
    return pl.pallas_call(
        partial(_hgrn_kernel, T=T),
        grid=(B,),
        in_specs=[pl.BlockSpec((1, T, W), lambda b: (b, 0, 0)), _state_spec(1, l_in), full((2, HG_WIDTH)),
                  full((1, HG_WIDTH))] + ([] if st_all is None else [pl.BlockSpec(memory_space=pl.ANY)]),
        out_specs=[pl.BlockSpec((1, T, HG_WIDTH), lambda b: (b, 0, 0)), _state_spec(1, l_out)],
        out_shape=[jax.ShapeDtypeStruct((B, T, HG_WIDTH), F32), jax.ShapeDtypeStruct(st_shape, F32)],
        input_output_aliases={} if st_all is None else {4: 1},
        scratch_shapes=[pltpu.VMEM((2, T, HG_WIDTH), F32)] * 3
        + [pltpu.VMEM((2, HG_HEADS, HEAD_DIM, HEAD_DIM), F32)],
        compiler_params=pltpu.CompilerParams(dimension_semantics=("arbitrary",), vmem_limit_bytes=48 * 1024 * 1024),
        name="hgrn2_mixer",
    )(p_hg, s0, lb, jnp.tile(norm_g.reshape(1, HEAD_DIM), (1, HG_HEADS)), *([] if st_all is None else [st_all]))


ATT_REP = ATT_HEADS // ATT_KV_HEADS
ATT_QROWS = 128


def _swap_pairs(x):
    w = x.shape[-1]
    lane = lax.broadcasted_iota(jnp.int32, x.shape, x.ndim - 1)
    return jnp.where(lane % 2 == 0, pltpu.roll(x, w - 1, x.ndim - 1), pltpu.roll(x, 1, x.ndim - 1))


def _att_kernel(*refs, T, past, rope):
    if rope:
        p_ref, qg_ref, kg_ref, cos_ref, sin_ref, ck_ref, cv_ref, out_ref, k_scr, v_scr, q_scr = refs
    else:
        p_ref, qg_ref, kg_ref, _, _, out_ref, kh_ref, vh_ref, k_scr, v_scr, q_scr = refs
    x = p_ref[0]
    q = x[:, 0:ATT_WIDTH]
    k = x[:, ATT_WIDTH:ATT_WIDTH + KV_WIDTH]
    v = x[:, ATT_WIDTH + KV_WIDTH:ATT_WIDTH + 2 * KV_WIDTH]
    inv_d = 1.0 / HEAD_DIM
    q = q * lax.rsqrt(_mm_exact_rhs(q * q, _head_blockdiag(ATT_WIDTH), n=2) * inv_d + NORM_EPS) * qg_ref[...]
    k = k * lax.rsqrt(_mm_exact_rhs(k * k, _head_blockdiag(KV_WIDTH), n=2) * inv_d + NORM_EPS) * kg_ref[...]
    if rope:
        cos, sin = cos_ref[...], sin_ref[...]
        rep = ATT_WIDTH // KV_WIDTH
        q = q * jnp.concatenate([cos] * rep, axis=1) + _swap_pairs(q) * jnp.concatenate([sin] * rep, axis=1)
        k = k * cos + _swap_pairs(k) * sin
    q_scr[...] = (q * (1.0 / math.sqrt(HEAD_DIM))).astype(BF16)
    for g in range(ATT_KV_HEADS):
        gs = slice(g * HEAD_DIM, (g + 1) * HEAD_DIM)
        if rope:
            k_scr[g, 0:past, :] = ck_ref[0, 0, g].astype(BF16)
            v_scr[g, 0:past, :] = cv_ref[0, 0, g].astype(BF16)
        else:
            kh_ref[0, 0, g] = k[:, gs].T
            vh_ref[0, 0, g] = v[:, gs].T
        k_scr[g, past:past + T, :] = k[:, gs].astype(BF16)
        v_scr[g, past:past + T, :] = v[:, gs].astype(BF16)
    QR = ATT_QROWS

    def q_block(qb, carry):
        rows = pl.ds(pl.multiple_of(qb * QR, QR), QR)
        qblk = q_scr[rows, :]
        for g in range(ATT_KV_HEADS):
            qs = jnp.concatenate([qblk[:, (g * ATT_REP + r) * HEAD_DIM:(g * ATT_REP + r + 1) * HEAD_DIM]
                                  for r in range(ATT_REP)], axis=0)
            s = lax.dot_general(qs, k_scr[g], _NT, preferred_element_type=F32)
            e = jnp.exp(s - jnp.max(s, axis=-1, keepdims=True))
            l = jnp.sum(e, axis=-1, keepdims=True)
            o = lax.dot_general(e.astype(BF16), v_scr[g], _NN, preferred_element_type=F32) / l
            for r in range(ATT_REP):
                h = g * ATT_REP + r
                out_ref[0, rows, h * HEAD_DIM:(h + 1) * HEAD_DIM] = o[r * QR:(r + 1) * QR]
        return carry

    lax.fori_loop(0, T // QR, q_block, 0, unroll=4 if T // QR >= 4 else 2)


def rope_tables(T):
    rows = T // GRID_W
    row = jnp.repeat(jnp.arange(rows, dtype=F32), GRID_W)
    col = jnp.tile(jnp.arange(GRID_W, dtype=F32), rows)
    n_freq = HEAD_DIM // 4
    inv = ROPE_THETA ** (-jnp.arange(n_freq, dtype=F32) / n_freq)
    ang = jnp.concatenate([row[:, None] * inv, col[:, None] * inv], axis=-1)
    cos = jnp.repeat(jnp.cos(ang), 2, axis=-1)
    sin = jnp.stack([-jnp.sin(ang), jnp.sin(ang)], axis=-1).reshape(T, HEAD_DIM)
    return jnp.tile(cos, (1, ATT_KV_HEADS)), jnp.tile(sin, (1, ATT_KV_HEADS))


def attention_pallas(p_att, qnorm_g, knorm_g, l, cache=None, kv_all=None):
    B, T, W = p_att.shape
    rope = cache is not None
    past = cache[0].shape[3] if rope else 0
    full = lambda shape: pl.BlockSpec(shape, lambda b: (0,) * len(shape))
    qg = jnp.tile(qnorm_g.reshape(1, HEAD_DIM), (1, ATT_HEADS))
    kg = jnp.tile(knorm_g.reshape(1, HEAD_DIM), (1, ATT_KV_HEADS))
    in_specs = [pl.BlockSpec((1, T, W), lambda b: (b, 0, 0)), full((1, ATT_WIDTH)), full((1, KV_WIDTH))]
    args = [p_att, qg, kg]
    out_specs = [pl.BlockSpec((1, T, ATT_WIDTH), lambda b: (b, 0, 0))]
    out_shape = [jax.ShapeDtypeStruct((B, T, ATT_WIDTH), F32)]
    if rope:
        cos, sin = rope_tables(T)
        kv_spec = pl.BlockSpec((1, 1, ATT_KV_HEADS, past, HEAD_DIM), lambda b: (b, l, 0, 0, 0))
        in_specs += [full((T, KV_WIDTH)), full((T, KV_WIDTH)), kv_spec, kv_spec]
        args += [cos, sin, cache[0], cache[1]]
        aliases = {}
    else:
        kv_spec = pl.BlockSpec((1, 1, ATT_KV_HEADS, HEAD_DIM, T), lambda b: (b, l, 0, 0, 0))
        in_specs += [pl.BlockSpec(memory_space=pl.ANY)] * 2
        args += list(kv_all)
        aliases = {3: 1, 4: 2}
        out_specs += [kv_spec, kv_spec]
        out_shape += [jax.ShapeDtypeStruct(kv_all[0].shape, F32)] * 2
    res = pl.pallas_call(
        partial(_att_kernel, T=T, past=past, rope=rope),
        grid=(B,),
        in_specs=in_specs,
        out_specs=out_specs,
        out_shape=out_shape,
        input_output_aliases=aliases,
        scratch_shapes=[pltpu.VMEM((ATT_KV_HEADS, past + T, HEAD_DIM), BF16)] * 2
        + [pltpu.VMEM((T, ATT_WIDTH), BF16)],
        compiler_params=pltpu.CompilerParams(dimension_semantics=("arbitrary",), vmem_limit_bytes=48 * 1024 * 1024),
        name="attention_rope" if rope else "attention_ctx",
    )(*args)
    return res[0] if rope else tuple(res)


ROW_TILE = 256
MOD_TILE = 1536
ROUTE_TILE = 256
MOE_SLOTS = 256
MOE_WEIGHT_SPLIT = 4
SC_ROWS = 32
SC_LANES = 128
P_ATT, P_RW, P_HG = ATT_WIDTH + 2 * KV_WIDTH, 3 * RW_WIDTH + 384, 5 * HG_WIDTH


def _mod_kernel(c_ref, w_ref, b_ref, o_ref):
    c = c_ref[...]
    o_ref[0] = _mm(c * _sigmoid(c), w_ref[0], passes=3) + b_ref[0]


def adaln_mod_pallas(cvec, w_mod, b_mod):
    n = 6 * D_MODEL
    return pl.pallas_call(
        _mod_kernel,
        grid=(DEPTH, n // MOD_TILE),
        in_specs=[pl.BlockSpec((8, D_MODEL), lambda l, j: (0, 0)),
                  pl.BlockSpec((1, D_MODEL, MOD_TILE), lambda l, j: (l, 0, j)),
                  pl.BlockSpec((1, 1, MOD_TILE), lambda l, j: (l, 0, j))],
        out_specs=pl.BlockSpec((1, 8, MOD_TILE), lambda l, j: (l, 0, j)),
        out_shape=jax.ShapeDtypeStruct((DEPTH, 8, n), F32),
        compiler_params=pltpu.CompilerParams(dimension_semantics=("arbitrary", "arbitrary"),
                                             vmem_limit_bytes=48 * 1024 * 1024),
        name="adaln_mod",
    )(cvec, w_mod, b_mod.reshape(DEPTH, 1, n))


def _rms(x):
    return x * lax.rsqrt(jnp.mean(x * x, axis=-1, keepdims=True) + NORM_EPS)


HALF = D_MODEL // 2
U32 = jnp.uint32


def _pack_rows(x):
    bits = lax.bitcast_convert_type(x.astype(BF16).astype(F32), U32)
    return (bits[:, :HALF] >> 16) | bits[:, HALF:]


def _unpack_rows(w):
    lo = lax.bitcast_convert_type(w << 16, F32)
    hi = lax.bitcast_convert_type(w & jnp.uint32(0xFFFF0000), F32)
    return lo, hi


def _moe_residual(x_ref, g_ref, g4_ref, pm_ref):
    y = None
    for k in range(TOP_K):
        t = g4_ref[:, k:k + 1] * jnp.concatenate(_unpack_rows(g_ref[k]), axis=1)
        y = t if y is None else y + t
    return x_ref[...] + pm_ref[0, 5:6, :] * y


def _in_kernel(*refs, has_res):
    if has_res:
        x_ref, gth_ref, g4_ref, pm_ref, m_ref, g_ref, w_ref, xo_ref, pa_ref, pr_ref, ph_ref = refs
        x = _moe_residual(x_ref, gth_ref, g4_ref, pm_ref)
        xo_ref[...] = x
    else:
        x_ref, m_ref, g_ref, w_ref, pa_ref, pr_ref, ph_ref = refs
        x = x_ref[...]
    h = _rms(x) * g_ref[...] * (1.0 + m_ref[0, 1:2, :]) + m_ref[0, 0:1, :]
    proj = lax.dot_general(h, w_ref[0], _NN, precision=lax.Precision.DEFAULT, preferred_element_type=F32)
    pa_ref[...] = proj[:, 0:P_ATT]
    pr_ref[...] = proj[:, P_ATT:P_ATT + P_RW]
    ph_ref[...] = proj[:, P_ATT + P_RW:]


def _res_specs(row0):
    t0 = row0 // ROW_TILE
    return [pl.BlockSpec((TOP_K, ROW_TILE, HALF), lambda i: (0, i, 0)),
            pl.BlockSpec((ROW_TILE, TOP_K), lambda i: (i + t0, 0))]


def in_proj_pallas(x, mod, norm_g, w_in, l, rows_per_mod, res=None):
    R = x.shape[0]
    tpm = rows_per_mod // ROW_TILE
    rt = lambda w: pl.BlockSpec((ROW_TILE, w), lambda i: (i, 0))
    ms = pl.BlockSpec((1, 6, D_MODEL), lambda i: (i // tpm, 0, 0))
    full = lambda shape, **kw: pl.BlockSpec(shape, lambda i: (0,) * len(shape), **kw)
    in_specs = [rt(D_MODEL)] + (_res_specs(res[2]) + [ms] if res else []) + [
        ms, full((1, D_MODEL)),
        pl.BlockSpec((1, D_MODEL, w_in.shape[2]), lambda i: (l, 0, 0), pipeline_mode=pl.Buffered(1))]
    args = [x] + ([res[0], res[1], res[3]] if res else []) + [mod, norm_g.reshape(1, D_MODEL), w_in]
    widths = ([D_MODEL] if res else []) + [P_ATT, P_RW, P_HG]
    return pl.pallas_call(
        partial(_in_kernel, has_res=res is not None),
        grid=(R // ROW_TILE,),
        in_specs=in_specs,
        out_specs=[rt(w) for w in widths],
        out_shape=[jax.ShapeDtypeStruct((R, w), F32) for w in widths],
        compiler_params=pltpu.CompilerParams(dimension_semantics=("arbitrary",), vmem_limit_bytes=48 * 1024 * 1024),
        name="in_proj",
    )(*args)


def _out_kernel(att_ref, rw_ref, hg_ref, x_ref, m_ref, g_ref, w_ref, rw_w_ref, rb_ref, *rest):
    xo_ref, h_ref, lg_ref = rest[-3:]
    d = lambda a, lo, hi: lax.dot_general(a, w_ref[0, lo:hi, :], _NN, precision=lax.Precision.DEFAULT,
                                          preferred_element_type=F32)
    mixo = (d(att_ref[...], 0, ATT_WIDTH) + d(rw_ref[...], ATT_WIDTH, ATT_WIDTH + RW_WIDTH)
            + d(hg_ref[...], ATT_WIDTH + RW_WIDTH, ATT_WIDTH + RW_WIDTH + HG_WIDTH))
    x = x_ref[...] + m_ref[0, 2:3, :] * mixo
    xo_ref[...] = x
    h = _rms(x) * g_ref[...] * (1.0 + m_ref[0, 4:5, :]) + m_ref[0, 3:4, :]
    h_ref[...] = _pack_rows(h)
    lg_ref[...] = _mm(h, rw_w_ref[...], passes=3) + rb_ref[...]


def out_proj_pallas(att, rw, hg, x, mod, norm_g, w_out, l, router_w, router_b, rows_per_mod, n_all, row0, joint=None):
    R = x.shape[0]
    tpm = rows_per_mod // ROW_TILE
    t0 = row0 // ROW_TILE
    rt = lambda w: pl.BlockSpec((ROW_TILE, w), lambda i: (i, 0))
    jt = lambda w: pl.BlockSpec((ROW_TILE, w), lambda i: (i + t0, 0))
    full = lambda shape: pl.BlockSpec(shape, lambda i: (0,) * len(shape))
    in_specs = [rt(ATT_WIDTH), rt(RW_WIDTH), rt(HG_WIDTH), rt(D_MODEL),
                pl.BlockSpec((1, 6, D_MODEL), lambda i: (i // tpm, 0, 0)), full((1, D_MODEL)),
                pl.BlockSpec((1, D_MODEL, D_MODEL), lambda i: (l, 0, 0)), full((D_MODEL, N_EXPERTS)),
                full((1, N_EXPERTS))]
    args = [att, rw, hg, x, mod, norm_g.reshape(1, D_MODEL), w_out, router_w, router_b.reshape(1, N_EXPERTS)]
    aliases = {}
    if joint is not None:
        in_specs += [pl.BlockSpec(memory_space=pl.ANY)] * 2
        aliases = {len(args): 1, len(args) + 1: 2}
        args += list(joint)
    return pl.pallas_call(
        _out_kernel,
        grid=(R // ROW_TILE,),
        in_specs=in_specs,
        out_specs=[rt(D_MODEL), jt(HALF), jt(N_EXPERTS)],
        out_shape=[jax.ShapeDtypeStruct((R, D_MODEL), F32), jax.ShapeDtypeStruct((n_all, HALF), U32),
                   jax.ShapeDtypeStruct((n_all, N_EXPERTS), F32)],
        input_output_aliases=aliases,
        compiler_params=pltpu.CompilerParams(dimension_semantics=("arbitrary",), vmem_limit_bytes=48 * 1024 * 1024),
        name="out_proj",
    )(*args)


def _final_kernel(x_ref, gth_ref, g4_ref, m_ref, g_ref, o_ref):
    o_ref[...] = _rms(_moe_residual(x_ref, gth_ref, g4_ref, m_ref)) * g_ref[...]


def final_norm_pallas(x, gathered, gate4, row0, mod, norm_g, rows_per_mod):
    R = x.shape[0]
    tpm = rows_per_mod // ROW_TILE
    rt = pl.BlockSpec((ROW_TILE, D_MODEL), lambda i: (i, 0))
    return pl.pallas_call(
        _final_kernel,
        grid=(R // ROW_TILE,),
        in_specs=[rt] + _res_specs(row0) + [pl.BlockSpec((1, 6, D_MODEL), lambda i: (i // tpm, 0, 0)),
                                            pl.BlockSpec((1, D_MODEL), lambda i: (0, 0))],
        out_specs=rt,
        out_shape=jax.ShapeDtypeStruct((R, D_MODEL), F32),
        name="final_norm",
    )(x, gathered, gate4, mod, norm_g.reshape(1, D_MODEL))


def _moe_max_blocks(n_tok):
    return (n_tok * TOP_K + N_EXPERTS * (MOE_SLOTS - 1)) // MOE_SLOTS


def _exact_nt_ones(a):
    ones = jnp.ones((8, a.shape[1]), BF16)
    out = None
    for t in reversed(_split(a, 3)):
        y = lax.dot_general(ones, t, _NT, preferred_element_type=F32)
        out = y if out is None else out + y
    return out


def _route_kernel(lg_ref, dest_ref, gate4_ref, blk_ref, rank_scr, gate_scr, *, n_tok):
    Rt, E = ROUTE_TILE, N_EXPERTS
    n_tiles = n_tok // Rt
    ti = lax.broadcasted_iota(jnp.int32, (Rt, Rt), 0)
    si = lax.broadcasted_iota(jnp.int32, (Rt, Rt), 1)
    before_t = (ti < si).astype(BF16)
    eye_t = (ti == si).astype(BF16)
    ei = lax.broadcasted_iota(jnp.int32, (E, E), 0)
    ej = lax.broadcasted_iota(jnp.int32, (E, E), 1)
    before_e = (ei > ej).astype(BF16)
    sub = lax.broadcasted_iota(jnp.int32, (E, Rt), 0)
    d = lambda a, b, dims: lax.dot_general(a, b, dims, preferred_element_type=F32)

    def tile_members(it, off):
        rows = pl.ds(pl.multiple_of(it * Rt, Rt), Rt)
        l3, l2, l1 = reversed(_split(lg_ref[rows, :], 3))
        lgT = d(l3, eye_t, _TN) + d(l2, eye_t, _TN) + d(l1, eye_t, _TN)
        work = lgT
        member = jnp.zeros((E, Rt), jnp.bool_)
        top = None
        for k in range(TOP_K):
            m = jnp.max(work, axis=0, keepdims=True)
            if top is None:
                top = m
            first = jnp.min(jnp.where(work == m, sub, E), axis=0, keepdims=True)
            pick = sub == first
            member = member | pick
            work = jnp.where(pick, -jnp.inf, work)
        ex = jnp.where(member, jnp.exp(lgT - top), 0.0)
        gate_scr[:, rows] = ex / jnp.sum(ex, axis=0, keepdims=True)
        mem = member.astype(BF16)
        rank = d(mem, before_t, _NN) + off
        rank_scr[:, rows] = jnp.where(member, rank, -1.0)
        return off + jnp.sum(mem.astype(F32), axis=1, keepdims=True)

    count = lax.fori_loop(0, n_tiles, tile_members, jnp.zeros((E, 1), F32))
    nblk = jnp.floor((count + (MOE_SLOTS - 1)) * (1.0 / MOE_SLOTS))
    bstart = d(before_e, jnp.broadcast_to(nblk, (E, 128)).astype(BF16), _NN)[:, 0:1]
    bend = bstart + nblk
    pstart = bstart * MOE_SLOTS

    def tile_slots(it, carry):
        rows = pl.ds(pl.multiple_of(it * Rt, Rt), Rt)
        rank = rank_scr[:, rows]
        gate = gate_scr[:, rows]
        member = rank >= 0.0
        kidx = d(before_e, member.astype(BF16), _NN)
        slot = pstart + rank
        grows = []
        for k in range(TOP_K):
            sel = member & (kidx == k)
            dest_ref[k:k + 1, rows] = jnp.sum(jnp.where(sel, slot, 0.0), axis=0, keepdims=True).astype(jnp.int32)
            grows.append(jnp.sum(jnp.where(sel, gate, 0.0), axis=0, keepdims=True))
        g4t = jnp.concatenate(grows + [jnp.zeros((128 - TOP_K, Rt), F32)], axis=0)
        g3, g2, g1 = reversed(_split(g4t, 3))
        g4 = d(eye_t, g3, _NT) + d(eye_t, g2, _NT) + d(eye_t, g1, _NT)
        gate4_ref[rows, :] = g4[:, 0:TOP_K]
        return carry

    lax.fori_loop(0, n_tiles, tile_slots, 0)
    nb = blk_ref.shape[1]
    bi = lax.broadcasted_iota(jnp.int32, (E, nb), 1).astype(F32)
    owner = jnp.sum((bend <= bi).astype(F32), axis=0, keepdims=True)
    blk_ref[0:1, :] = jnp.minimum(owner, E - 1.0).astype(jnp.int32)
    blk_ref[1:2, :] = jnp.broadcast_to(jnp.sum(nblk, axis=0, keepdims=True), (1, nb)).astype(jnp.int32)
    blk_ref[2:8, :] = jnp.zeros((6, nb), jnp.int32)


def moe_route_pallas(logits):
    n_tok = logits.shape[0]
    nb = -(-_moe_max_blocks(n_tok) // 128) * 128
    return pl.pallas_call(
        partial(_route_kernel, n_tok=n_tok),
        out_shape=[jax.ShapeDtypeStruct((TOP_K, n_tok), jnp.int32),
                   jax.ShapeDtypeStruct((n_tok, TOP_K), F32),
                   jax.ShapeDtypeStruct((8, nb), jnp.int32)],
        scratch_shapes=[pltpu.VMEM((N_EXPERTS, n_tok), F32)] * 2,
        name="moe_route",
    )(logits)


def _moe_block_kernel(be_ref, nu_ref, first_ref, par_ref, nxt_ref, xb_ref, wgu_hbm, bgu_ref, wdn_hbm, bdn_ref, yb_ref,
                      wgu_buf, wdn_buf, sem, *, l):
    i = pl.program_id(0)

    def weight_copies(e, slot):
        cps = []
        for hbm, buf, s in ((wgu_hbm, wgu_buf, 0), (wdn_hbm, wdn_buf, 1)):
            rows = buf.shape[1] // MOE_WEIGHT_SPLIT
            for j in range(MOE_WEIGHT_SPLIT):
                rs = pl.ds(j * rows, rows)
                cps.append(pltpu.make_async_copy(hbm.at[l, e, rs], buf.at[slot, rs], sem.at[s, slot]))
        return cps

    @pl.when(i < nu_ref[0])
    def _():
        slot = par_ref[i]

        @pl.when(first_ref[i] == 1)
        def _():
            @pl.when(i == 0)
            def _():
                for cp in weight_copies(be_ref[0], slot):
                    cp.start()

            for cp in weight_copies(be_ref[i], slot):
                cp.wait()

            @pl.when(nxt_ref[i] >= 0)
            def _():
                for cp in weight_copies(nxt_ref[i], 1 - slot):
                    cp.start()

        dot = lambda a, w: lax.dot_general(a, w, _NN, precision=lax.Precision.DEFAULT, preferred_element_type=F32)
        x_lo, x_hi = _unpack_rows(xb_ref[...])
        gu = dot(x_lo, wgu_buf[slot, 0:HALF]) + dot(x_hi, wgu_buf[slot, HALF:D_MODEL]) + bgu_ref[0, 0]
        glu = jnp.minimum(gu[:, :EXPERT_FF], SWIGLU_LIMIT)
        lin = jnp.clip(gu[:, EXPERT_FF:], -SWIGLU_LIMIT, SWIGLU_LIMIT)
        act = glu * _sigmoid(SWIGLU_ALPHA * glu) * (lin + 1.0)
        yb_ref[...] = _pack_rows(dot(act, wdn_buf[slot]) + bdn_ref[0, 0])


def _expert_runs(block_e, n_used):
    n = block_e.shape[0]
    idx = jnp.arange(n, dtype=jnp.int32)
    valid = idx < n_used[0]
    first = valid & ((idx == 0) | (block_e != jnp.roll(block_e, 1)))
    par = (jnp.cumsum(first.astype(jnp.int32)) - 1) % 2
    start = jnp.where(first, idx, n)
    nxt_start = lax.cummin(jnp.concatenate([start[1:], jnp.full((1,), n, jnp.int32)]), reverse=True)
    nxt = jnp.where(nxt_start < n, block_e[jnp.minimum(nxt_start, n - 1)], -1)
    return first.astype(jnp.int32), par.astype(jnp.int32), nxt.astype(jnp.int32)


def moe_blocks_pallas(xb, block_e, n_used, l, w_gu, b_gu, w_down, b_down):
    n_blocks = xb.shape[0] // MOE_SLOTS
    first, par, nxt = _expert_runs(block_e, n_used)
    blk = lambda i, be, nu, *_: (jnp.minimum(i, nu[0] - 1), 0)
    bsel = lambda i, be, nu, *_: (l, be[jnp.minimum(i, nu[0] - 1)], 0, 0)
    grid_spec = pltpu.PrefetchScalarGridSpec(
        num_scalar_prefetch=5,
        grid=(n_blocks,),
        in_specs=[pl.BlockSpec((MOE_SLOTS, HALF), blk),
                  pl.BlockSpec(memory_space=pl.ANY),
                  pl.BlockSpec((1, 1, 1, 2 * EXPERT_FF), bsel),
                  pl.BlockSpec(memory_space=pl.ANY),
                  pl.BlockSpec((1, 1, 1, D_MODEL), bsel)],
        out_specs=pl.BlockSpec((MOE_SLOTS, HALF), blk),
        scratch_shapes=[pltpu.VMEM((2, D_MODEL, 2 * EXPERT_FF), F32), pltpu.VMEM((2, EXPERT_FF, D_MODEL), F32),
                        pltpu.SemaphoreType.DMA((2, 2))],
    )
    return pl.pallas_call(
        partial(_moe_block_kernel, l=l),
        grid_spec=grid_spec,
        out_shape=jax.ShapeDtypeStruct(xb.shape, U32),
        compiler_params=pltpu.CompilerParams(dimension_semantics=("arbitrary",), vmem_limit_bytes=48 * 1024 * 1024),
        name="moe_blocks",
    )(block_e, n_used, first, par, nxt, xb, w_gu, b_gu.reshape(DEPTH, N_EXPERTS, 1, 2 * EXPERT_FF), w_down,
      b_down.reshape(DEPTH, N_EXPERTS, 1, D_MODEL))


def _sc_mesh():
    return plsc.VectorSubcoreMesh(core_axis_name="c", subcore_axis_name="s")


def _sc_index_rows(idx):
    return jnp.pad(idx.reshape(-1, SC_ROWS), ((0, 0), (0, SC_LANES - SC_ROWS)))


def sc_dispatch(h, dest, n_rows):
    n_tok, d = h.shape
    idx = [_sc_index_rows(dest[k]) for k in range(TOP_K)]

    @pl.kernel(out_type=jax.ShapeDtypeStruct((n_rows, d), h.dtype), mesh=_sc_mesh(), scratch_types=[])
    def kern(h_hbm, i0, i1, i2, i3, o_hbm):
        def body(x_vmem, *i_vmem):
            for iv in i_vmem:
                pltpu.sync_copy(x_vmem, o_hbm.at[iv.at[0, pl.ds(0, SC_ROWS)]])

        pltpu.emit_pipeline(
            body,
            grid=(n_tok // SC_ROWS,),
            in_specs=[pl.BlockSpec((SC_ROWS, d), lambda i: (i, 0))]
            + [pl.BlockSpec((1, SC_LANES), lambda i: (i, 0))] * TOP_K,
            out_specs=[],
            core_axis_name=("c", "s"),
            dimension_semantics=(pltpu.PARALLEL,),
        )(h_hbm, i0, i1, i2, i3)

    return kern(h, *idx)


def sc_combine_gather(yb, dest):
    n_tok = dest.shape[1]
    d = yb.shape[1]
    idx = _sc_index_rows(dest.reshape(TOP_K * n_tok))

    @pl.kernel(out_type=jax.ShapeDtypeStruct((TOP_K * n_tok, d), yb.dtype), mesh=_sc_mesh(), scratch_types=[])
    def kern(y_hbm, i_hbm, o_hbm):
        def body(i_vmem, o_vmem):
            pltpu.sync_copy(y_hbm.at[i_vmem.at[0, pl.ds(0, SC_ROWS)]], o_vmem)

        pltpu.emit_pipeline(
            body,
            grid=(TOP_K * n_tok // SC_ROWS,),
            in_specs=[pl.BlockSpec((1, SC_LANES), lambda i: (i, 0))],
            out_specs=[pl.BlockSpec((SC_ROWS, d), lambda i: (i, 0))],
            core_axis_name=("c", "s"),
            dimension_semantics=(pltpu.PARALLEL,),
        )(i_hbm, o_hbm)

    return kern(yb, idx).reshape(TOP_K, n_tok, d)


def hgrn_lower_bounds(hg_lb):
    sm = jax.nn.softmax(hg_lb.astype(jnp.float32), axis=0)
    return jnp.cumsum(sm, axis=0) - sm[0:1]


def kernel(x_prompt, x_sample, cache_att_k, cache_att_v, state_rwkv, state_hgrn, c, c_ctx, w_mod, b_mod, norm_mix_g, norm_ffn_g, w_in, w_out, att_qnorm_g, att_knorm_g, rw_w0, rw_w2, rw_a0, rw_a2, rw_g2, rw_kk, rw_ka, rw_rk, rw_gn_g, rw_gn_b, hg_lb, hg_norm_g, router_w, router_b, moe_w_gu, moe_b_gu, moe_w_down, moe_b_down, final_norm_g):
    BP, TP, _ = x_prompt.shape
    BS, TS, _ = x_sample.shape
    n_p, n_s = BP * TP, BS * TS
    lb_all = hgrn_lower_bounds(hg_lb)
    cvec = jnp.concatenate([c_ctx[None, :], c, jnp.zeros((8 - 1 - BS, D_MODEL), F32)], axis=0)
    mod_all = adaln_mod_pallas(cvec, w_mod, b_mod).reshape(DEPTH, 8, 6, D_MODEL)
    zeros_state = jnp.zeros((BP, 1, 2, RW_HEADS, HEAD_DIM, HEAD_DIM), F32)
    kv_all = (jnp.zeros((BP, DEPTH, ATT_KV_HEADS, HEAD_DIM, TP), F32),) * 2
    rw_states = jnp.zeros((BP, DEPTH, 2, RW_HEADS, HEAD_DIM, HEAD_DIM), F32)
    hg_states = jnp.zeros((BP, DEPTH, 2, HG_HEADS, HEAD_DIM, HEAD_DIM), F32)
    x = {'p': x_prompt.reshape(n_p, D_MODEL), 's': x_sample.reshape(n_s, D_MODEL)}
    dims = {'p': (TP, BP, n_p, 0), 's': (TS, BS, TS, n_p)}
    moe_out, mod_prev = None, None
    for l in range(DEPTH):
        prm = dict(rw_w0=rw_w0[l], rw_w2=rw_w2[l], rw_a0=rw_a0[l], rw_a2=rw_a2[l], rw_g2=rw_g2[l],
                   rw_kk=rw_kk[l], rw_ka=rw_ka[l], rw_rk=rw_rk[l], rw_gn_g=rw_gn_g[l], rw_gn_b=rw_gn_b[l])
        mods = {'p': mod_all[l, 0:1], 's': mod_all[l, 1:1 + BS]}
        joint = (jnp.zeros((n_p + n_s, HALF), U32), jnp.zeros((n_p + n_s, N_EXPERTS), F32))
        for s in ('p', 's'):
            T, B, rpm, row0 = dims[s]
            if l == 0:
                p_att, p_rw, p_hg = in_proj_pallas(x[s], mods[s], norm_mix_g[l], w_in, l, rpm)
            else:
                x[s], p_att, p_rw, p_hg = in_proj_pallas(x[s], mods[s], norm_mix_g[l], w_in, l, rpm,
                                                         res=(*moe_out[s], row0, mod_prev[s]))
            p_att, p_rw, p_hg = (t.reshape(B, T, -1) for t in (p_att, p_rw, p_hg))
            if s == 'p':
                att, *kv_all = attention_pallas(p_att, att_qnorm_g[l], att_knorm_g[l], l, kv_all=kv_all)
                rw_out, rw_states = rwkv7_mixer_pallas(p_rw, zeros_state, 0, prm, rw_states, l)
                hg_out, hg_states = hgrn2_mixer_pallas(p_hg, zeros_state, 0, lb_all[l], hg_norm_g[l], hg_states, l)
            else:
                att = attention_pallas(p_att, att_qnorm_g[l], att_knorm_g[l], l, cache=(cache_att_k, cache_att_v))
                rw_out, _ = rwkv7_mixer_pallas(p_rw, state_rwkv, l, prm)
                hg_out, _ = hgrn2_mixer_pallas(p_hg, state_hgrn, l, lb_all[l], hg_norm_g[l])
            x[s], *joint = out_proj_pallas(att.reshape(B * T, -1), rw_out.reshape(B * T, -1),
                                           hg_out.reshape(B * T, -1), x[s], mods[s], norm_ffn_g[l], w_out, l,
                                           router_w[l], router_b[l], rpm, n_p + n_s, row0, joint)
        h_all, logits_all = joint
        dest, gate4, blk = moe_route_pallas(logits_all)
        xb = sc_dispatch(h_all, dest, _moe_max_blocks(n_p + n_s) * MOE_SLOTS)
        yb = moe_blocks_pallas(xb, blk[0], blk[1, :1], l, moe_w_gu, moe_b_gu, moe_w_down, moe_b_down)
        moe_out = {'p': (sc_combine_gather(yb, dest[:, :n_p]), gate4), 's': (sc_combine_gather(yb, dest[:, n_p:]), gate4)}
        mod_prev = mods
    y_prompt = final_norm_pallas(x['p'], *moe_out['p'], 0, mod_prev['p'], final_norm_g, n_p)
    y_sample = final_norm_pallas(x['s'], *moe_out['s'], n_p, mod_prev['s'], final_norm_g, TS)
    return (y_prompt.reshape(x_prompt.shape), y_sample.reshape(x_sample.shape),
            jnp.swapaxes(kv_all[0], 3, 4), jnp.swapaxes(kv_all[1], 3, 4),
            rw_states, hg_states)
```

```python
import math
from functools import partial

import jax
import jax.numpy as jnp
from jax import lax
from jax.experimental import pallas as pl
from jax.experimental.pallas import tpu as pltpu
from jax.experimental.pallas import tpu_sc as plsc

D_MODEL = 1024
DEPTH = 2
GRID_W = 64
HEAD_DIM = 64
ATT_HEADS = 8
ATT_KV_HEADS = 2
ATT_WIDTH = ATT_HEADS * HEAD_DIM
KV_WIDTH = ATT_KV_HEADS * HEAD_DIM
RW_HEADS = 4
RW_WIDTH = RW_HEADS * HEAD_DIM
RW_GN_EPS = 64e-5
HG_HEADS = 4
HG_WIDTH = HG_HEADS * HEAD_DIM
HG_F_MIN = 1e-6
N_EXPERTS = 32
TOP_K = 4
EXPERT_FF = D_MODEL
SWIGLU_LIMIT = 7.0
SWIGLU_ALPHA = 1.702
ROPE_THETA = 10000.0
NORM_EPS = 1e-6

RW_CHUNK = 64
BF16 = jnp.bfloat16
F32 = jnp.float32

_NN = (((1,), (0,)), ((), ()))
_NT = (((1,), (1,)), ((), ()))
_TN = (((0,), (0,)), ((), ()))


def _split(x, n):
    parts = []
    for _ in range(n - 1):
        hi = x.astype(BF16)
        parts.append(hi)
        x = x - hi.astype(F32)
    parts.append(x.astype(BF16))
    return parts


def _mm(a, b, dims=_NN, passes=1):
    d = lambda x, y: lax.dot_general(x, y, dims, preferred_element_type=F32)
    if passes == 1:
        return d(a.astype(BF16), b.astype(BF16))
    ah, al = _split(a, 2)
    bh, bl = _split(b, 2)
    return d(ah, bl) + d(al, bh) + d(ah, bh)


def _mm_exact_lhs(a01, b, n=3):
    a = a01.astype(BF16)
    out = None
    for t in reversed(_split(b, n)):
        y = lax.dot_general(a, t, _NN, preferred_element_type=F32)
        out = y if out is None else out + y
    return out


def _mm_exact_rhs(a, b01, n=3):
    b = b01.astype(BF16)
    out = None
    for t in reversed(_split(a, n)):
        y = lax.dot_general(t, b, _NN, preferred_element_type=F32)
        out = y if out is None else out + y
    return out


def _head_blockdiag(width):
    r = lax.broadcasted_iota(jnp.int32, (width, width), 0) // HEAD_DIM
    c = lax.broadcasted_iota(jnp.int32, (width, width), 1) // HEAD_DIM
    return (r == c).astype(F32)


def _sigmoid(x):
    return 1.0 / (1.0 + jnp.exp(-x))


def _softplus(x):
    return jnp.maximum(x, 0.0) + jnp.log(1.0 + jnp.exp(-jnp.abs(x)))


def _rwkv_kernel(p_ref, s0_ref, w0_ref, w2_ref, a0_ref, a2_ref, g2_ref, kk_ref, ka_ref, rk_ref, gng_ref, gnb_ref,
                 *rest, T, NB):
    out_ref, st_ref, lw_scr, kd_scr, bb_scr, y_scr, kk_scr, s_scr = rest[-8:]
    C = RW_CHUNK
    n_chunks = T // C
    bd = _head_blockdiag(RW_WIDTH)
    seg = lambda t: _mm_exact_rhs(t, bd, n=2)
    ka = ka_ref[...]
    for nb in range(NB):
        k = p_ref[nb, :, 256:512]
        kk = k * kk_ref[...]
        kk = kk * lax.rsqrt(seg(kk * kk) + 1e-12)
        kk_scr[nb] = kk
        for d in range(2):
            wd = p_ref[nb, :, 768 + 64 * d:832 + 64 * d]
            ad = p_ref[nb, :, 896 + 64 * d:960 + 64 * d]
            w_raw = w0_ref[d:d + 1, :] + _mm(jnp.tanh(wd), w2_ref[d], passes=3)
            lw_scr[nb, d] = -jnp.exp(-_softplus(-w_raw) - 0.5)
            a = _sigmoid(a0_ref[d:d + 1, :] + _mm(ad, a2_ref[d], passes=3))
            kd_scr[nb, d] = k * (1.0 + (a - 1.0) * ka)
            bb_scr[nb, d] = kk * a
    s_scr[...] = s0_ref[:, 0]

    ti = lax.broadcasted_iota(jnp.int32, (C, C), 0)
    si = lax.broadcasted_iota(jnp.int32, (C, C), 1)
    ones_cc = jnp.ones((C, C), F32)

    def chunk_body(i, carry):
        ch = []
        for nb, d in [(nb, d) for nb in range(NB) for d in range(2)]:
            ci = i if d == 0 else n_chunks - 1 - i
            rows = pl.ds(pl.multiple_of(ci * C, C), C)
            strict = (ti > si) if d == 0 else (ti < si)
            incl = (ti >= si) if d == 0 else (ti <= si)
            lw = lw_scr[nb, d, rows, :]
            cum = _mm_exact_lhs(incl.astype(F32), lw)
            total = _mm_exact_lhs(ones_cc, lw)
            cum_ex = cum - lw
            mid = 0.5 * total
            rr = p_ref[nb, rows, 0:256]
            vv = p_ref[nb, rows, 512:768]
            kdc = kd_scr[nb, d, rows, :]
            bbc = bb_scr[nb, d, rows, :]
            kkc = kk_scr[nb, rows, :]
            e_inv = jnp.exp(mid - cum)
            At = -kkc * jnp.exp(cum_ex - mid)
            Rt = rr * jnp.exp(cum - mid)
            Bt = bbc * e_inv
            Kt = kdc * e_inv
            Ap = -kkc * jnp.exp(cum_ex)
            Rp = rr * jnp.exp(cum)
            e_out = jnp.exp(total - cum)
            Bh = bbc * e_out
            Kh = kdc * e_out
            e_tot = jnp.exp(total[0:1, :])
            for h in range(RW_HEADS):
                hs = slice(h * HEAD_DIM, (h + 1) * HEAD_DIM)
                ch.append(dict(nb=nb, d=d, h=h, rows=rows, hs=hs, strict=strict, incl=incl,
                               AR=jnp.concatenate([At[:, hs], Rt[:, hs]], axis=0), Bt=Bt[:, hs], Kt=Kt[:, hs],
                               V=vv[:, hs], X1=Ap[:, hs], Rp=Rp[:, hs], Bh=Bh[:, hs], Kh=Kh[:, hs],
                               e_tot=e_tot[:, hs]))
        for c in ch:
            c['AB'] = _mm(c['AR'], c['Bt'], _NT)
            c['AK'] = _mm(c['AR'], c['Kt'], _NT)
        for c in ch:
            c['P'] = jnp.where(c['strict'], c['AB'][:C], 0.0)
            c['A_ak'] = jnp.where(c['strict'], c['AK'][:C], 0.0)
            c['A_rb'] = jnp.where(c['incl'], c['AB'][C:], 0.0)
            c['A_rk'] = jnp.where(c['incl'], c['AK'][C:], 0.0)
        for c in ch:
            c['X2'] = _mm(c['A_ak'], c['V'])
        for lvl in range(6):
            for c in ch:
                if lvl < 5:
                    c['PZ'] = _mm(c['P'], jnp.concatenate([c['P'], c['X1'], c['X2']], axis=1))
                else:
                    c['PZ'] = _mm(c['P'], jnp.concatenate([c['X1'], c['X2']], axis=1))
            for c in ch:
                PZ = c['PZ']
                if lvl < 5:
                    c['P'] = PZ[:, :C]
                    c['X1'] = c['X1'] + PZ[:, C:2 * C]
                    c['X2'] = c['X2'] + PZ[:, 2 * C:]
                else:
                    c['X1'] = c['X1'] + PZ[:, :C]
                    c['X2'] = c['X2'] + PZ[:, C:]
        for c in ch:
            c['S0'] = s_scr[c['nb'], c['d'], c['h']]
            c['UY'] = _mm(jnp.concatenate([c['X1'], c['Rp']], axis=0), c['S0'], _NT)
        for c in ch:
            c['U'] = c['UY'][:C] + c['X2']
        for c in ch:
            c['Y'] = c['UY'][C:] + _mm(c['A_rb'], c['U']) + _mm(c['A_rk'], c['V'])
            c['S1'] = c['S0'] * c['e_tot'] + _mm(c['U'], c['Bh'], _TN) + _mm(c['V'], c['Kh'], _TN)
        for c in ch:
            s_scr[c['nb'], c['d'], c['h']] = c['S1']
            y_scr[c['nb'], c['d'], c['rows'], c['hs']] = c['Y']
        return carry

    lax.fori_loop(0, n_chunks, chunk_body, 0)

    for nb in range(NB):
        r = p_ref[nb, :, 0:256]
        v = p_ref[nb, :, 512:768]
        bonus = seg(r * (kd_scr[nb, 0] + kd_scr[nb, 1]) * rk_ref[...]) * v
        g = _mm(_sigmoid(p_ref[nb, :, 1024:1152]), g2_ref[...], passes=3)
        y = y_scr[nb, 0] + y_scr[nb, 1]
        mu = seg(y) * (1.0 / HEAD_DIM)
        yc = y - mu
        var = seg(yc * yc) * (1.0 / HEAD_DIM)
        yn = yc * lax.rsqrt(var + RW_GN_EPS)
        out_ref[nb] = (yn * gng_ref[...] + gnb_ref[...] + bonus) * g
    st_ref[:, 0] = s_scr[...]


RW_ROWS = 1024


def _state_spec(nb, layer):
    return pl.BlockSpec((nb, 1, 2, RW_HEADS, HEAD_DIM, HEAD_DIM), lambda b: (b, layer, 0, 0, 0, 0))


def rwkv7_mixer_pallas(p_rw, s0, l_in, prm, st_all=None, l_out=0):
    B, T, W = p_rw.shape
    NB = max(2, RW_ROWS // T)
    row = lambda a: a.reshape(1, RW_WIDTH)
    full = lambda shape: pl.BlockSpec(shape, lambda b: (0,) * len(shape))
    st_shape = (B, 1, 2, RW_HEADS, HEAD_DIM, HEAD_DIM) if st_all is None else st_all.shape
    return pl.pallas_call(
        partial(_rwkv_kernel, T=T, NB=NB),
        grid=(B // NB,),
        in_specs=[pl.BlockSpec((NB, T, W), lambda b: (b, 0, 0)), _state_spec(NB, l_in),
                  full((2, RW_WIDTH)), full((2, 64, RW_WIDTH)), full((2, RW_WIDTH)), full((2, 64, RW_WIDTH)),
                  full((128, RW_WIDTH)), full((1, RW_WIDTH)), full((1, RW_WIDTH)), full((1, RW_WIDTH)),
                  full((1, RW_WIDTH)), full((1, RW_WIDTH))]
        + ([] if st_all is None else [pl.BlockSpec(memory_space=pl.ANY)]),
        out_specs=[pl.BlockSpec((NB, T, RW_WIDTH), lambda b: (b, 0, 0)), _state_spec(NB, l_out)],
        out_shape=[jax.ShapeDtypeStruct((B, T, RW_WIDTH), F32), jax.ShapeDtypeStruct(st_shape, F32)],
        input_output_aliases={} if st_all is None else {12: 1},
        scratch_shapes=[pltpu.VMEM((NB, 2, T, RW_WIDTH), F32)] * 4
        + [pltpu.VMEM((NB, T, RW_WIDTH), F32), pltpu.VMEM((NB, 2, RW_HEADS, HEAD_DIM, HEAD_DIM), F32)],
        compiler_params=pltpu.CompilerParams(dimension_semantics=("arbitrary",), vmem_limit_bytes=56 * 1024 * 1024),
        name="rwkv7_mixer",
    )(p_rw, s0, prm['rw_w0'], prm['rw_w2'], prm['rw_a0'], prm['rw_a2'], prm['rw_g2'], row(prm['rw_kk']),
      row(prm['rw_ka']), row(prm['rw_rk']), row(prm['rw_gn_g']), row(prm['rw_gn_b']),
      *([] if st_all is None else [st_all]))


HG_SUB = 16
HG_ROWS = 256


def _hgrn_kernel(p_ref, s0_ref, lb_ref, ng_ref, *rest, T):
    out_ref, st_ref, lf_scr, kf_scr, o_scr, s_scr = rest[-6:]
    R, c = HG_ROWS, HG_SUB
    n_it = T // R
    x = p_ref[0]
    bd = _head_blockdiag(HG_WIDTH)
    seg = lambda t: _mm_exact_rhs(t, bd, n=2)
    for d in range(2):
        lbd = lb_ref[d:d + 1, :]
        f = lbd + (1.0 - lbd) * _sigmoid(x[:, 256 + 256 * d:512 + 256 * d])
        lf_scr[d] = jnp.log(jnp.maximum(f, HG_F_MIN))
        kf_scr[d] = 1.0 - f
        for h in range(HG_HEADS):
            s_scr[d, h] = s0_ref[0, 0, d, h].T

    ti = lax.broadcasted_iota(jnp.int32, (R, R), 0)
    si = lax.broadcasted_iota(jnp.int32, (R, R), 1)
    same_blk = (ti // c) == (si // c)
    t16 = lax.broadcasted_iota(jnp.int32, (c, 1), 0)

    def body(i, carry):
        for d in range(2):
            ci = i if d == 0 else n_it - 1 - i
            rows = pl.ds(pl.multiple_of(ci * R, R), R)
            incl = (ti >= si) if d == 0 else (ti <= si)
            lf = lf_scr[d, rows, :]
            cum = _mm_exact_lhs((incl & same_blk).astype(F32), lf)
            tot = _mm_exact_lhs(same_blk.astype(F32), lf)
            xq = p_ref[0, rows, 0:256]
            q = xq * _sigmoid(xq)
            v = p_ref[0, rows, 768:1024]
            kf = kf_scr[d, rows, :]
            Qp = q * jnp.exp(cum)
            Kh = kf * jnp.exp(tot - cum)
            e_tot = jnp.exp(tot)
            blocks = range(R // c) if d == 0 else range(R // c - 1, -1, -1)
            ST = [s_scr[d, h] for h in range(HG_HEADS)]
            o_parts = [None] * (R // c)
            for j in blocks:
                rs = slice(j * c, (j + 1) * c)
                cb, qb, kb, vb = cum[rs], q[rs], kf[rs], v[rs]
                half = c // 2
                spans = []
                for s in range(c):
                    if d == 0:
                        spans.append((half, c) if s >= half else (0, c))
                    else:
                        spans.append((0, half) if s < half else (0, c))
                prods = []
                for s, (lo, hi) in enumerate(spans):
                    e = jnp.exp(jnp.minimum(cb[lo:hi] - cb[s:s + 1, :], 0.0))
                    prods.append(qb[lo:hi] * (kb[s:s + 1, :] * e))
                att = _mm_exact_rhs(jnp.concatenate(prods, axis=0), bd, n=1)
                o_half = [jnp.zeros((half, HG_WIDTH), F32), jnp.zeros((half, HG_WIDTH), F32)]
                off = 0
                for s, (lo, hi) in enumerate(spans):
                    keep = (t16[lo:hi] >= s) if d == 0 else (t16[lo:hi] <= s)
                    term = jnp.where(keep, att[off:off + hi - lo], 0.0) * vb[s:s + 1, :]
                    off += hi - lo
                    for p in range(2):
                        a, b = max(lo, p * half), min(hi, (p + 1) * half)
                        if a < b:
                            o_half[p] = o_half[p] + term[a - lo:b - lo]
                o_blk = jnp.concatenate(o_half, axis=0)
                o_heads = []
                for h in range(HG_HEADS):
                    hs = slice(h * HEAD_DIM, (h + 1) * HEAD_DIM)
                    o_heads.append(_mm(Qp[rs, hs], ST[h], _NT))
                    ST[h] = ST[h] * e_tot[j * c:j * c + 1, hs] + _mm(vb[:, hs], Kh[rs, hs], _TN)
                o_parts[j] = o_blk + jnp.concatenate(o_heads, axis=1)
            for h in range(HG_HEADS):
                s_scr[d, h] = ST[h]
            o_scr[d, rows, :] = jnp.concatenate(o_parts, axis=0)
        return carry

    lax.fori_loop(0, n_it, body, 0)

    o = o_scr[0] + o_scr[1]
    o = o * lax.rsqrt(seg(o * o) * (1.0 / HEAD_DIM) + NORM_EPS) * ng_ref[...]
    gg = x[:, 1024:1280]
    out_ref[0] = o * (gg * _sigmoid(gg))
    for d in range(2):
        for h in range(HG_HEADS):
            st_ref[0, 0, d, h] = s_scr[d, h].T


def hgrn2_mixer_pallas(p_hg, s0, l_in, lb, norm_g, st_all=None, l_out=0):
    B, T, W = p_hg.shape
    full = lambda shape: pl.BlockSpec(shape, lambda b: (0,) * len(shape))
    st_shape = (B, 1, 2, HG_HEADS, HEAD_DIM, HEAD_DIM) if st_all is None else st_all.shape
    return pl.pallas_call(
        partial(_hgrn_kernel, T=T),
        grid=(B,),
        in_specs=[pl.BlockSpec((1, T, W), lambda b: (b, 0, 0)), _state_spec(1, l_in), full((2, HG_WIDTH)),
                  full((1, HG_WIDTH))] + ([] if st_all is None else [pl.BlockSpec(memory_space=pl.ANY)]),
        out_specs=[pl.BlockSpec((1, T, HG_WIDTH), lambda b: (b, 0, 0)), _state_spec(1, l_out)],
        out_shape=[jax.ShapeDtypeStruct((B, T, HG_WIDTH), F32), jax.ShapeDtypeStruct(st_shape, F32)],
        input_output_aliases={} if st_all is None else {4: 1},
        scratch_shapes=[pltpu.VMEM((2, T, HG_WIDTH), F32)] * 3
        + [pltpu.VMEM((2, HG_HEADS, HEAD_DIM, HEAD_DIM), F32)],
        compiler_params=pltpu.CompilerParams(dimension_semantics=("arbitrary",), vmem_limit_bytes=48 * 1024 * 1024),
        name="hgrn2_mixer",
    )(p_hg, s0, lb, jnp.tile(norm_g.reshape(1, HEAD_DIM), (1, HG_HEADS)), *([] if st_all is None else [st_all]))


ATT_REP = ATT_HEADS // ATT_KV_HEADS
ATT_QROWS = 128


def _swap_pairs(x):
    w = x.shape[-1]
    lane = lax.broadcasted_iota(jnp.int32, x.shape, x.ndim - 1)
    return jnp.where(lane % 2 == 0, pltpu.roll(x, w - 1, x.ndim - 1), pltpu.roll(x, 1, x.ndim - 1))


def _att_kernel(*refs, T, past, rope):
    if rope:
        p_ref, qg_ref, kg_ref, cos_ref, sin_ref, ck_ref, cv_ref, out_ref, k_scr, v_scr, q_scr = refs
    else:
        p_ref, qg_ref, kg_ref, _, _, out_ref, kh_ref, vh_ref, k_scr, v_scr, q_scr = refs
    x = p_ref[0]
    q = x[:, 0:ATT_WIDTH]
    k = x[:, ATT_WIDTH:ATT_WIDTH + KV_WIDTH]
    v = x[:, ATT_WIDTH + KV_WIDTH:ATT_WIDTH + 2 * KV_WIDTH]
    inv_d = 1.0 / HEAD_DIM
    q = q * lax.rsqrt(_mm_exact_rhs(q * q, _head_blockdiag(ATT_WIDTH), n=2) * inv_d + NORM_EPS) * qg_ref[...]
    k = k * lax.rsqrt(_mm_exact_rhs(k * k, _head_blockdiag(KV_WIDTH), n=2) * inv_d + NORM_EPS) * kg_ref[...]
    if rope:
        cos, sin = cos_ref[...], sin_ref[...]
        rep = ATT_WIDTH // KV_WIDTH
        q = q * jnp.concatenate([cos] * rep, axis=1) + _swap_pairs(q) * jnp.concatenate([sin] * rep, axis=1)
        k = k * cos + _swap_pairs(k) * sin
    q_scr[...] = (q * (1.0 / math.sqrt(HEAD_DIM))).astype(BF16)
    for g in range(ATT_KV_HEADS):
        gs = slice(g * HEAD_DIM, (g + 1) * HEAD_DIM)
        if rope:
            k_scr[g, 0:past, :] = ck_ref[0, 0, g].astype(BF16)
            v_scr[g, 0:past, :] = cv_ref[0, 0, g].astype(BF16)
        else:
            kh_ref[0, 0, g] = k[:, gs].T
            vh_ref[0, 0, g] = v[:, gs].T
        k_scr[g, past:past + T, :] = k[:, gs].astype(BF16)
        v_scr[g, past:past + T, :] = v[:, gs].astype(BF16)
    QR = ATT_QROWS

    def q_block(qb, carry):
        rows = pl.ds(pl.multiple_of(qb * QR, QR), QR)
        qblk = q_scr[rows, :]
        for g in range(ATT_KV_HEADS):
            qs = jnp.concatenate([qblk[:, (g * ATT_REP + r) * HEAD_DIM:(g * ATT_REP + r + 1) * HEAD_DIM]
                                  for r in range(ATT_REP)], axis=0)
            s = lax.dot_general(qs, k_scr[g], _NT, preferred_element_type=F32)
            e = jnp.exp(s - jnp.max(s, axis=-1, keepdims=True))
            l = jnp.sum(e, axis=-1, keepdims=True)
            o = lax.dot_general(e.astype(BF16), v_scr[g], _NN, preferred_element_type=F32) / l
            for r in range(ATT_REP):
                h = g * ATT_REP + r
                out_ref[0, rows, h * HEAD_DIM:(h + 1) * HEAD_DIM] = o[r * QR:(r + 1) * QR]
        return carry

    lax.fori_loop(0, T // QR, q_block, 0, unroll=4 if T // QR >= 4 else 2)


def rope_tables(T):
    rows = T // GRID_W
    row = jnp.repeat(jnp.arange(rows, dtype=F32), GRID_W)
    col = jnp.tile(jnp.arange(GRID_W, dtype=F32), rows)
    n_freq = HEAD_DIM // 4
    inv = ROPE_THETA ** (-jnp.arange(n_freq, dtype=F32) / n_freq)
    ang = jnp.concatenate([row[:, None] * inv, col[:, None] * inv], axis=-1)
    cos = jnp.repeat(jnp.cos(ang), 2, axis=-1)
    sin = jnp.stack([-jnp.sin(ang), jnp.sin(ang)], axis=-1).reshape(T, HEAD_DIM)
    return jnp.tile(cos, (1, ATT_KV_HEADS)), jnp.tile(sin, (1, ATT_KV_HEADS))


def attention_pallas(p_att, qnorm_g, knorm_g, l, cache=None, kv_all=None):
    B, T, W = p_att.shape
    rope = cache is not None
    past = cache[0].shape[3] if rope else 0
    full = lambda shape: pl.BlockSpec(shape, lambda b: (0,) * len(shape))
    qg = jnp.tile(qnorm_g.reshape(1, HEAD_DIM), (1, ATT_HEADS))
    kg = jnp.tile(knorm_g.reshape(1, HEAD_DIM), (1, ATT_KV_HEADS))
    in_specs = [pl.BlockSpec((1, T, W), lambda b: (b, 0, 0)), full((1, ATT_WIDTH)), full((1, KV_WIDTH))]
    args = [p_att, qg, kg]
    out_specs = [pl.BlockSpec((1, T, ATT_WIDTH), lambda b: (b, 0, 0))]
    out_shape = [jax.ShapeDtypeStruct((B, T, ATT_WIDTH), F32)]
    if rope:
        cos, sin = rope_tables(T)
        kv_spec = pl.BlockSpec((1, 1, ATT_KV_HEADS, past, HEAD_DIM), lambda b: (b, l, 0, 0, 0))
        in_specs += [full((T, KV_WIDTH)), full((T, KV_WIDTH)), kv_spec, kv_spec]
        args += [cos, sin, cache[0], cache[1]]
        aliases = {}
    else:
        kv_spec = pl.BlockSpec((1, 1, ATT_KV_HEADS, HEAD_DIM, T), lambda b: (b, l, 0, 0, 0))
        in_specs += [pl.BlockSpec(memory_space=pl.ANY)] * 2
        args += list(kv_all)
        aliases = {3: 1, 4: 2}
        out_specs += [kv_spec, kv_spec]
        out_shape += [jax.ShapeDtypeStruct(kv_all[0].shape, F32)] * 2
    res = pl.pallas_call(
        partial(_att_kernel, T=T, past=past, rope=rope),
        grid=(B,),
        in_specs=in_specs,
        out_specs=out_specs,
        out_shape=out_shape,
        input_output_aliases=aliases,
        scratch_shapes=[pltpu.VMEM((ATT_KV_HEADS, past + T, HEAD_DIM), BF16)] * 2
        + [pltpu.VMEM((T, ATT_WIDTH), BF16)],
        compiler_params=pltpu.CompilerParams(dimension_semantics=("arbitrary",), vmem_limit_bytes=48 * 1024 * 1024),
        name="attention_rope" if rope else "attention_ctx",
    )(*args)
    return res[0] if rope else tuple(res)


ROW_TILE = 256
MOD_TILE = 1536
ROUTE_TILE = 256
MOE_SLOTS = 256
MOE_WEIGHT_SPLIT = 4
SC_ROWS = 32
SC_LANES = 128
P_ATT, P_RW, P_HG = ATT_WIDTH + 2 * KV_WIDTH, 3 * RW_WIDTH + 384, 5 * HG_WIDTH


def _mod_kernel(c_ref, w_ref, b_ref, o_ref):
    c = c_ref[...]
    o_ref[0] = _mm(c * _sigmoid(c), w_ref[0], passes=3) + b_ref[0]


def adaln_mod_pallas(cvec, w_mod, b_mod):
    n = 6 * D_MODEL
    return pl.pallas_call(
        _mod_kernel,
        grid=(DEPTH, n // MOD_TILE),
        in_specs=[pl.BlockSpec((8, D_MODEL), lambda l, j: (0, 0)),
                  pl.BlockSpec((1, D_MODEL, MOD_TILE), lambda l, j: (l, 0, j)),
                  pl.BlockSpec((1, 1, MOD_TILE), lambda l, j: (l, 0, j))],
        out_specs=pl.BlockSpec((1, 8, MOD_TILE), lambda l, j: (l, 0, j)),
        out_shape=jax.ShapeDtypeStruct((DEPTH, 8, n), F32),
        compiler_params=pltpu.CompilerParams(dimension_semantics=("arbitrary", "arbitrary"),
                                             vmem_limit_bytes=48 * 1024 * 1024),
        name="adaln_mod",
    )(cvec, w_mod, b_mod.reshape(DEPTH, 1, n))


def _rms(x):
    return x * lax.rsqrt(jnp.mean(x * x, axis=-1, keepdims=True) + NORM_EPS)


HALF = D_MODEL // 2
U32 = jnp.uint32


def _pack_rows(x):
    bits = lax.bitcast_convert_type(x.astype(BF16).astype(F32), U32)
    return (bits[:, :HALF] >> 16) | bits[:, HALF:]


def _unpack_rows(w):
    lo = lax.bitcast_convert_type(w << 16, F32)
    hi = lax.bitcast_convert_type(w & jnp.uint32(0xFFFF0000), F32)
    return lo, hi


def _moe_residual(x_ref, g_ref, g4_ref, pm_ref):
    y = None
    for k in range(TOP_K):
        t = g4_ref[:, k:k + 1] * jnp.concatenate(_unpack_rows(g_ref[k]), axis=1)
        y = t if y is None else y + t
    return x_ref[...] + pm_ref[0, 5:6, :] * y


def _in_kernel(*refs, has_res):
    if has_res:
        x_ref, gth_ref, g4_ref, pm_ref, m_ref, g_ref, w_ref, xo_ref, pa_ref, pr_ref, ph_ref = refs
        x = _moe_residual(x_ref, gth_ref, g4_ref, pm_ref)
        xo_ref[...] = x
    else:
        x_ref, m_ref, g_ref, w_ref, pa_ref, pr_ref, ph_ref = refs
        x = x_ref[...]
    h = _rms(x) * g_ref[...] * (1.0 + m_ref[0, 1:2, :]) + m_ref[0, 0:1, :]
    proj = lax.dot_general(h, w_ref[0], _NN, precision=lax.Precision.DEFAULT, preferred_element_type=F32)
    pa_ref[...] = proj[:, 0:P_ATT]
    pr_ref[...] = proj[:, P_ATT:P_ATT + P_RW]
    ph_ref[...] = proj[:, P_ATT + P_RW:]


def _res_specs(row0):
    t0 = row0 // ROW_TILE
    return [pl.BlockSpec((TOP_K, ROW_TILE, HALF), lambda i: (0, i, 0)),
            pl.BlockSpec((ROW_TILE, TOP_K), lambda i: (i + t0, 0))]


def in_proj_pallas(x, mod, norm_g, w_in, l, rows_per_mod, res=None):
    R = x.shape[0]
    tpm = rows_per_mod // ROW_TILE
    rt = lambda w: pl.BlockSpec((ROW_TILE, w), lambda i: (i, 0))
    ms = pl.BlockSpec((1, 6, D_MODEL), lambda i: (i // tpm, 0, 0))
    full = lambda shape, **kw: pl.BlockSpec(shape, lambda i: (0,) * len(shape), **kw)
    in_specs = [rt(D_MODEL)] + (_res_specs(res[2]) + [ms] if res else []) + [
        ms, full((1, D_MODEL)),
        pl.BlockSpec((1, D_MODEL, w_in.shape[2]), lambda i: (l, 0, 0), pipeline_mode=pl.Buffered(1))]
    args = [x] + ([res[0], res[1], res[3]] if res else []) + [mod, norm_g.reshape(1, D_MODEL), w_in]
    widths = ([D_MODEL] if res else []) + [P_ATT, P_RW, P_HG]
    return pl.pallas_call(
        partial(_in_kernel, has_res=res is not None),
        grid=(R // ROW_TILE,),
        in_specs=in_specs,
        out_specs=[rt(w) for w in widths],
        out_shape=[jax.ShapeDtypeStruct((R, w), F32) for w in widths],
        compiler_params=pltpu.CompilerParams(dimension_semantics=("arbitrary",), vmem_limit_bytes=48 * 1024 * 1024),
        name="in_proj",
    )(*args)


def _out_kernel(att_ref, rw_ref, hg_ref, x_ref, m_ref, g_ref, w_ref, rw_w_ref, rb_ref, *rest):
    xo_ref, h_ref, lg_ref = rest[-3:]
    d = lambda a, lo, hi: lax.dot_general(a, w_ref[0, lo:hi, :], _NN, precision=lax.Precision.DEFAULT,
                                          preferred_element_type=F32)
    mixo = (d(att_ref[...], 0, ATT_WIDTH) + d(rw_ref[...], ATT_WIDTH, ATT_WIDTH + RW_WIDTH)
            + d(hg_ref[...], ATT_WIDTH + RW_WIDTH, ATT_WIDTH + RW_WIDTH + HG_WIDTH))
    x = x_ref[...] + m_ref[0, 2:3, :] * mixo
    xo_ref[...] = x
    h = _rms(x) * g_ref[...] * (1.0 + m_ref[0, 4:5, :]) + m_ref[0, 3:4, :]
    h_ref[...] = _pack_rows(h)
    lg_ref[...] = _mm(h, rw_w_ref[...], passes=3) + rb_ref[...]


def out_proj_pallas(att, rw, hg, x, mod, norm_g, w_out, l, router_w, router_b, rows_per_mod, n_all, row0, joint=None):
    R = x.shape[0]
    tpm = rows_per_mod // ROW_TILE
    t0 = row0 // ROW_TILE
    rt = lambda w: pl.BlockSpec((ROW_TILE, w), lambda i: (i, 0))
    jt = lambda w: pl.BlockSpec((ROW_TILE, w), lambda i: (i + t0, 0))
    full = lambda shape: pl.BlockSpec(shape, lambda i: (0,) * len(shape))
    in_specs = [rt(ATT_WIDTH), rt(RW_WIDTH), rt(HG_WIDTH), rt(D_MODEL),
                pl.BlockSpec((1, 6, D_MODEL), lambda i: (i // tpm, 0, 0)), full((1, D_MODEL)),
                pl.BlockSpec((1, D_MODEL, D_MODEL), lambda i: (l, 0, 0)), full((D_MODEL, N_EXPERTS)),
                full((1, N_EXPERTS))]
    args = [att, rw, hg, x, mod, norm_g.reshape(1, D_MODEL), w_out, router_w, router_b.reshape(1, N_EXPERTS)]
    aliases = {}
    if joint is not None:
        in_specs += [pl.BlockSpec(memory_space=pl.ANY)] * 2
        aliases = {len(args): 1, len(args) + 1: 2}
        args += list(joint)
    return pl.pallas_call(
        _out_kernel,
        grid=(R // ROW_TILE,),
        in_specs=in_specs,
        out_specs=[rt(D_MODEL), jt(HALF), jt(N_EXPERTS)],
        out_shape=[jax.ShapeDtypeStruct((R, D_MODEL), F32), jax.ShapeDtypeStruct((n_all, HALF), U32),
                   jax.ShapeDtypeStruct((n_all, N_EXPERTS), F32)],
        input_output_aliases=aliases,
        compiler_params=pltpu.CompilerParams(dimension_semantics=("arbitrary",), vmem_limit_bytes=48 * 1024 * 1024),
        name="out_proj",
    )(*args)


def _final_kernel(x_ref, gth_ref, g4_ref, m_ref, g_ref, o_ref):
    o_ref[...] = _rms(_moe_residual(x_ref, gth_ref, g4_ref, m_ref)) * g_ref[...]


def final_norm_pallas(x, gathered, gate4, row0, mod, norm_g, rows_per_mod):
    R = x.shape[0]
    tpm = rows_per_mod // ROW_TILE
    rt = pl.BlockSpec((ROW_TILE, D_MODEL), lambda i: (i, 0))
    return pl.pallas_call(
        _final_kernel,
        grid=(R // ROW_TILE,),
        in_specs=[rt] + _res_specs(row0) + [pl.BlockSpec((1, 6, D_MODEL), lambda i: (i // tpm, 0, 0)),
                                            pl.BlockSpec((1, D_MODEL), lambda i: (0, 0))],
        out_specs=rt,
        out_shape=jax.ShapeDtypeStruct((R, D_MODEL), F32),
        name="final_norm",
    )(x, gathered, gate4, mod, norm_g.reshape(1, D_MODEL))


def _moe_max_blocks(n_tok):
    return (n_tok * TOP_K + N_EXPERTS * (MOE_SLOTS - 1)) // MOE_SLOTS


def _route_kernel(lg_ref, dest_ref, gate4_ref, blk_ref, rank_scr, gate_scr, *, n_tok):
    Rt, E = ROUTE_TILE, N_EXPERTS
    n_tiles = n_tok // Rt
    ti = lax.broadcasted_iota(jnp.int32, (Rt, Rt), 0)
    si = lax.broadcasted_iota(jnp.int32, (Rt, Rt), 1)
    before_t = (ti < si).astype(BF16)
    eye_t = (ti == si).astype(BF16)
    ei = lax.broadcasted_iota(jnp.int32, (E, E), 0)
    ej = lax.broadcasted_iota(jnp.int32, (E, E), 1)
    before_e = (ei > ej).astype(BF16)
    sub = lax.broadcasted_iota(jnp.int32, (E, Rt), 0)
    d = lambda a, b, dims: lax.dot_general(a, b, dims, preferred_element_type=F32)

    def tile_members(it, off):
        rows = pl.ds(pl.multiple_of(it * Rt, Rt), Rt)
        l3, l2, l1 = reversed(_split(lg_ref[rows, :], 3))
        lgT = d(l3, eye_t, _TN) + d(l2, eye_t, _TN) + d(l1, eye_t, _TN)
        work = lgT
        member = jnp.zeros((E, Rt), jnp.bool_)
        top = None
        for k in range(TOP_K):
            m = jnp.max(work, axis=0, keepdims=True)
            if top is None:
                top = m
            first = jnp.min(jnp.where(work == m, sub, E), axis=0, keepdims=True)
            pick = sub == first
            member = member | pick
            work = jnp.where(pick, -jnp.inf, work)
        ex = jnp.where(member, jnp.exp(lgT - top), 0.0)
        gate_scr[:, rows] = ex / jnp.sum(ex, axis=0, keepdims=True)
        mem = member.astype(BF16)
        rank = d(mem, before_t, _NN) + off
        rank_scr[:, rows] = jnp.where(member, rank, -1.0)
        return off + jnp.sum(mem.astype(F32), axis=1, keepdims=True)

    count = lax.fori_loop(0, n_tiles, tile_members, jnp.zeros((E, 1), F32))
    nblk = jnp.floor((count + (MOE_SLOTS - 1)) * (1.0 / MOE_SLOTS))
    bstart = d(before_e, jnp.broadcast_to(nblk, (E, 128)).astype(BF16), _NN)[:, 0:1]
    bend = bstart + nblk
    pstart = bstart * MOE_SLOTS

    def tile_slots(it, carry):
        rows = pl.ds(pl.multiple_of(it * Rt, Rt), Rt)
        rank = rank_scr[:, rows]
        gate = gate_scr[:, rows]
        member = rank >= 0.0
        kidx = d(before_e, member.astype(BF16), _NN)
        slot = pstart + rank
        grows = []
        for k in range(TOP_K):
            sel = member & (kidx == k)
            dest_ref[k:k + 1, rows] = jnp.sum(jnp.where(sel, slot, 0.0), axis=0, keepdims=True).astype(jnp.int32)
            grows.append(jnp.sum(jnp.where(sel, gate, 0.0), axis=0, keepdims=True))
        g4t = jnp.concatenate(grows + [jnp.zeros((128 - TOP_K, Rt), F32)], axis=0)
        g3, g2, g1 = reversed(_split(g4t, 3))
        g4 = d(eye_t, g3, _NT) + d(eye_t, g2, _NT) + d(eye_t, g1, _NT)
        gate4_ref[rows, :] = g4[:, 0:TOP_K]
        return carry

    lax.fori_loop(0, n_tiles, tile_slots, 0)
    nb = blk_ref.shape[1]
    bi = lax.broadcasted_iota(jnp.int32, (E, nb), 1).astype(F32)
    owner = jnp.sum((bend <= bi).astype(F32), axis=0, keepdims=True)
    blk_ref[0:1, :] = jnp.minimum(owner, E - 1.0).astype(jnp.int32)
    blk_ref[1:2, :] = jnp.broadcast_to(jnp.sum(nblk, axis=0, keepdims=True), (1, nb)).astype(jnp.int32)
    blk_ref[2:8, :] = jnp.zeros((6, nb), jnp.int32)


def moe_route_pallas(logits):
    n_tok = logits.shape[0]
    nb = -(-_moe_max_blocks(n_tok) // 128) * 128
    return pl.pallas_call(
        partial(_route_kernel, n_tok=n_tok),
        out_shape=[jax.ShapeDtypeStruct((TOP_K, n_tok), jnp.int32),
                   jax.ShapeDtypeStruct((n_tok, TOP_K), F32),
                   jax.ShapeDtypeStruct((8, nb), jnp.int32)],
        scratch_shapes=[pltpu.VMEM((N_EXPERTS, n_tok), F32)] * 2,
        name="moe_route",
    )(logits)


def _moe_block_kernel(be_ref, nu_ref, first_ref, par_ref, nxt_ref, xb_ref, wgu_hbm, bgu_ref, wdn_hbm, bdn_ref, yb_ref,
                      wgu_buf, wdn_buf, sem, *, l):
    i = pl.program_id(0)

    def weight_copies(e, slot):
        cps = []
        for hbm, buf, s in ((wgu_hbm, wgu_buf, 0), (wdn_hbm, wdn_buf, 1)):
            rows = buf.shape[1] // MOE_WEIGHT_SPLIT
            for j in range(MOE_WEIGHT_SPLIT):
                rs = pl.ds(j * rows, rows)
                cps.append(pltpu.make_async_copy(hbm.at[l, e, rs], buf.at[slot, rs], sem.at[s, slot]))
        return cps

    @pl.when(i < nu_ref[0])
    def _():
        slot = par_ref[i]

        @pl.when(first_ref[i] == 1)
        def _():
            @pl.when(i == 0)
            def _():
                for cp in weight_copies(be_ref[0], slot):
                    cp.start()

            for cp in weight_copies(be_ref[i], slot):
                cp.wait()

            @pl.when(nxt_ref[i] >= 0)
            def _():
                for cp in weight_copies(nxt_ref[i], 1 - slot):
                    cp.start()

        dot = lambda a, w: lax.dot_general(a, w, _NN, precision=lax.Precision.DEFAULT, preferred_element_type=F32)
        x_lo, x_hi = _unpack_rows(xb_ref[...])
        gu = dot(x_lo, wgu_buf[slot, 0:HALF]) + dot(x_hi, wgu_buf[slot, HALF:D_MODEL]) + bgu_ref[0, 0]
        glu = jnp.minimum(gu[:, :EXPERT_FF], SWIGLU_LIMIT)
        lin = jnp.clip(gu[:, EXPERT_FF:], -SWIGLU_LIMIT, SWIGLU_LIMIT)
        act = glu * _sigmoid(SWIGLU_ALPHA * glu) * (lin + 1.0)
        yb_ref[...] = _pack_rows(dot(act, wdn_buf[slot]) + bdn_ref[0, 0])


def _expert_runs(block_e, n_used):
    n = block_e.shape[0]
    idx = jnp.arange(n, dtype=jnp.int32)
    valid = idx < n_used[0]
    first = valid & ((idx == 0) | (block_e != jnp.roll(block_e, 1)))
    par = (jnp.cumsum(first.astype(jnp.int32)) - 1) % 2
    start = jnp.where(first, idx, n)
    nxt_start = lax.cummin(jnp.concatenate([start[1:], jnp.full((1,), n, jnp.int32)]), reverse=True)
    nxt = jnp.where(nxt_start < n, block_e[jnp.minimum(nxt_start, n - 1)], -1)
    return first.astype(jnp.int32), par.astype(jnp.int32), nxt.astype(jnp.int32)


def moe_blocks_pallas(xb, block_e, n_used, l, w_gu, b_gu, w_down, b_down):
    n_blocks = xb.shape[0] // MOE_SLOTS
    first, par, nxt = _expert_runs(block_e, n_used)
    blk = lambda i, be, nu, *_: (jnp.minimum(i, nu[0] - 1), 0)
    bsel = lambda i, be, nu, *_: (l, be[jnp.minimum(i, nu[0] - 1)], 0, 0)
    grid_spec = pltpu.PrefetchScalarGridSpec(
        num_scalar_prefetch=5,
        grid=(n_blocks,),
        in_specs=[pl.BlockSpec((MOE_SLOTS, HALF), blk),
                  pl.BlockSpec(memory_space=pl.ANY),
                  pl.BlockSpec((1, 1, 1, 2 * EXPERT_FF), bsel),
                  pl.BlockSpec(memory_space=pl.ANY),
                  pl.BlockSpec((1, 1, 1, D_MODEL), bsel)],
        out_specs=pl.BlockSpec((MOE_SLOTS, HALF), blk),
        scratch_shapes=[pltpu.VMEM((2, D_MODEL, 2 * EXPERT_FF), F32), pltpu.VMEM((2, EXPERT_FF, D_MODEL), F32),
                        pltpu.SemaphoreType.DMA((2, 2))],
    )
    return pl.pallas_call(
        partial(_moe_block_kernel, l=l),
        grid_spec=grid_spec,
        out_shape=jax.ShapeDtypeStruct(xb.shape, U32),
        compiler_params=pltpu.CompilerParams(dimension_semantics=("arbitrary",), vmem_limit_bytes=48 * 1024 * 1024),
        name="moe_blocks",
    )(block_e, n_used, first, par, nxt, xb, w_gu, b_gu.reshape(DEPTH, N_EXPERTS, 1, 2 * EXPERT_FF), w_down,
      b_down.reshape(DEPTH, N_EXPERTS, 1, D_MODEL))


def _sc_mesh():
    return plsc.VectorSubcoreMesh(core_axis_name="c", subcore_axis_name="s")


def _sc_index_rows(idx):
    return jnp.pad(idx.reshape(-1, SC_ROWS), ((0, 0), (0, SC_LANES - SC_ROWS)))


def sc_dispatch(h, dest, n_rows):
    n_tok, d = h.shape
    idx = [_sc_index_rows(dest[k]) for k in range(TOP_K)]

    @pl.kernel(out_type=jax.ShapeDtypeStruct((n_rows, d), h.dtype), mesh=_sc_mesh(), scratch_types=[])
    def kern(h_hbm, i0, i1, i2, i3, o_hbm):
        def body(x_vmem, *i_vmem):
            for iv in i_vmem:
                pltpu.sync_copy(x_vmem, o_hbm.at[iv.at[0, pl.ds(0, SC_ROWS)]])

        pltpu.emit_pipeline(
            body,
            grid=(n_tok // SC_ROWS,),
            in_specs=[pl.BlockSpec((SC_ROWS, d), lambda i: (i, 0))]
            + [pl.BlockSpec((1, SC_LANES), lambda i: (i, 0))] * TOP_K,
            out_specs=[],
            core_axis_name=("c", "s"),
            dimension_semantics=(pltpu.PARALLEL,),
        )(h_hbm, i0, i1, i2, i3)

    return kern(h, *idx)


def sc_combine_gather(yb, dest):
    n_tok = dest.shape[1]
    d = yb.shape[1]
    idx = _sc_index_rows(dest.reshape(TOP_K * n_tok))

    @pl.kernel(out_type=jax.ShapeDtypeStruct((TOP_K * n_tok, d), yb.dtype), mesh=_sc_mesh(), scratch_types=[])
    def kern(y_hbm, i_hbm, o_hbm):
        def body(i_vmem, o_vmem):
            pltpu.sync_copy(y_hbm.at[i_vmem.at[0, pl.ds(0, SC_ROWS)]], o_vmem)

        pltpu.emit_pipeline(
            body,
            grid=(TOP_K * n_tok // SC_ROWS,),
            in_specs=[pl.BlockSpec((1, SC_LANES), lambda i: (i, 0))],
            out_specs=[pl.BlockSpec((SC_ROWS, d), lambda i: (i, 0))],
            core_axis_name=("c", "s"),
            dimension_semantics=(pltpu.PARALLEL,),
        )(i_hbm, o_hbm)

    return kern(yb, idx).reshape(TOP_K, n_tok, d)


def hgrn_lower_bounds(hg_lb):
    sm = jax.nn.softmax(hg_lb.astype(jnp.float32), axis=0)
    return jnp.cumsum(sm, axis=0) - sm[0:1]


def kernel(x_prompt, x_sample, cache_att_k, cache_att_v, state_rwkv, state_hgrn, c, c_ctx, w_mod, b_mod, norm_mix_g, norm_ffn_g, w_in, w_out, att_qnorm_g, att_knorm_g, rw_w0, rw_w2, rw_a0, rw_a2, rw_g2, rw_kk, rw_ka, rw_rk, rw_gn_g, rw_gn_b, hg_lb, hg_norm_g, router_w, router_b, moe_w_gu, moe_b_gu, moe_w_down, moe_b_down, final_norm_g):
    BP, TP, _ = x_prompt.shape
    BS, TS, _ = x_sample.shape
    n_p, n_s = BP * TP, BS * TS
    lb_all = hgrn_lower_bounds(hg_lb)
    cvec = jnp.concatenate([c_ctx[None, :], c, jnp.zeros((8 - 1 - BS, D_MODEL), F32)], axis=0)
    mod_all = adaln_mod_pallas(cvec, w_mod, b_mod).reshape(DEPTH, 8, 6, D_MODEL)
    zeros_state = jnp.zeros((BP, 1, 2, RW_HEADS, HEAD_DIM, HEAD_DIM), F32)
    kv_all = (jnp.zeros((BP, DEPTH, ATT_KV_HEADS, HEAD_DIM, TP), F32),) * 2
    rw_states = jnp.zeros((BP, DEPTH, 2, RW_HEADS, HEAD_DIM, HEAD_DIM), F32)
    hg_states = jnp.zeros((BP, DEPTH, 2, HG_HEADS, HEAD_DIM, HEAD_DIM), F32)
    x = {'p': x_prompt.reshape(n_p, D_MODEL), 's': x_sample.reshape(n_s, D_MODEL)}
    dims = {'p': (TP, BP, n_p, 0), 's': (TS, BS, TS, n_p)}
    moe_out, mod_prev = None, None
    for l in range(DEPTH):
        prm = dict(rw_w0=rw_w0[l], rw_w2=rw_w2[l], rw_a0=rw_a0[l], rw_a2=rw_a2[l], rw_g2=rw_g2[l],
                   rw_kk=rw_kk[l], rw_ka=rw_ka[l], rw_rk=rw_rk[l], rw_gn_g=rw_gn_g[l], rw_gn_b=rw_gn_b[l])
        mods = {'p': mod_all[l, 0:1], 's': mod_all[l, 1:1 + BS]}
        joint = (jnp.zeros((n_p + n_s, HALF), U32), jnp.zeros((n_p + n_s, N_EXPERTS), F32))
        for s in ('p', 's'):
            T, B, rpm, row0 = dims[s]
            if l == 0:
                p_att, p_rw, p_hg = in_proj_pallas(x[s], mods[s], norm_mix_g[l], w_in, l, rpm)
            else:
                x[s], p_att, p_rw, p_hg = in_proj_pallas(x[s], mods[s], norm_mix_g[l], w_in, l, rpm,
                                                         res=(*moe_out[s], row0, mod_prev[s]))
            p_att, p_rw, p_hg = (t.reshape(B, T, -1) for t in (p_att, p_rw, p_hg))
            if s == 'p':
                att, *kv_all = attention_pallas(p_att, att_qnorm_g[l], att_knorm_g[l], l, kv_all=kv_all)
                rw_out, rw_states = rwkv7_mixer_pallas(p_rw, zeros_state, 0, prm, rw_states, l)
                hg_out, hg_states = hgrn2_mixer_pallas(p_hg, zeros_state, 0, lb_all[l], hg_norm_g[l], hg_states, l)
            else:
                att = attention_pallas(p_att, att_qnorm_g[l], att_knorm_g[l], l, cache=(cache_att_k, cache_att_v))
                rw_out, _ = rwkv7_mixer_pallas(p_rw, state_rwkv, l, prm)
                hg_out, _ = hgrn2_mixer_pallas(p_hg, state_hgrn, l, lb_all[l], hg_norm_g[l])
            x[s], *joint = out_proj_pallas(att.reshape(B * T, -1), rw_out.reshape(B * T, -1),
                                           hg_out.reshape(B * T, -1), x[s], mods[s], norm_ffn_g[l], w_out, l,
                                           router_w[l], router_b[l], rpm, n_p + n_s, row0, joint)
        h_all, logits_all = joint
        dest, gate4, blk = moe_route_pallas(logits_all)
        xb = sc_dispatch(h_all, dest, _moe_max_blocks(n_p + n_s) * MOE_SLOTS)
        yb = moe_blocks_pallas(xb, blk[0], blk[1, :1], l, moe_w_gu, moe_b_gu, moe_w_down, moe_b_down)
        moe_out = {'p': (sc_combine_gather(yb, dest[:, :n_p]), gate4), 's': (sc_combine_gather(yb, dest[:, n_p:]), gate4)}
        mod_prev = mods
    y_prompt = final_norm_pallas(x['p'], *moe_out['p'], 0, mod_prev['p'], final_norm_g, n_p)
    y_sample = final_norm_pallas(x['s'], *moe_out['s'], n_p, mod_prev['s'], final_norm_g, TS)
    return (y_prompt.reshape(x_prompt.shape), y_sample.reshape(x_sample.shape),
            jnp.swapaxes(kv_all[0], 3, 4), jnp.swapaxes(kv_all[1], 3, 4),
            rw_states, hg_states)
```

```python
import math
from functools import partial

import jax
import jax.numpy as jnp
from jax import lax
from jax.experimental import pallas as pl
from jax.experimental.pallas import tpu as pltpu
from jax.experimental.pallas import tpu_sc as plsc

D_MODEL = 1024
DEPTH = 2
GRID_W = 64
HEAD_DIM = 64
ATT_HEADS = 8
ATT_KV_HEADS = 2
ATT_WIDTH = ATT_HEADS * HEAD_DIM
KV_WIDTH = ATT_KV_HEADS * HEAD_DIM
RW_HEADS = 4
RW_WIDTH = RW_HEADS * HEAD_DIM
RW_GN_EPS = 64e-5
HG_HEADS = 4
HG_WIDTH = HG_HEADS * HEAD_DIM
HG_F_MIN = 1e-6
N_EXPERTS = 32
TOP_K = 4
EXPERT_FF = D_MODEL
SWIGLU_LIMIT = 7.0
SWIGLU_ALPHA = 1.702
ROPE_THETA = 10000.0
NORM_EPS = 1e-6

RW_CHUNK = 64
BF16 = jnp.bfloat16
F32 = jnp.float32

_NN = (((1,), (0,)), ((), ()))
_NT = (((1,), (1,)), ((), ()))
_TN = (((0,), (0,)), ((), ()))


def _split(x, n):
    parts = []
    for _ in range(n - 1):
        hi = x.astype(BF16)
        parts.append(hi)
        x = x - hi.astype(F32)
    parts.append(x.astype(BF16))
    return parts


def _mm(a, b, dims=_NN, passes=1):
    d = lambda x, y: lax.dot_general(x, y, dims, preferred_element_type=F32)
    if passes == 1:
        return d(a.astype(BF16), b.astype(BF16))
    ah, al = _split(a, 2)
    bh, bl = _split(b, 2)
    return d(ah, bl) + d(al, bh) + d(ah, bh)


def _mm_exact_lhs(a01, b, n=3):
    a = a01.astype(BF16)
    out = None
    for t in reversed(_split(b, n)):
        y = lax.dot_general(a, t, _NN, preferred_element_type=F32)
        out = y if out is None else out + y
    return out


def _mm_exact_rhs(a, b01, n=3):
    b = b01.astype(BF16)
    out = None
    for t in reversed(_split(a, n)):
        y = lax.dot_general(t, b, _NN, preferred_element_type=F32)
        out = y if out is None else out + y
    return out


def _head_blockdiag(width):
    r = lax.broadcasted_iota(jnp.int32, (width, width), 0) // HEAD_DIM
    c = lax.broadcasted_iota(jnp.int32, (width, width), 1) // HEAD_DIM
    return (r == c).astype(F32)


def _sigmoid(x):
    return 1.0 / (1.0 + jnp.exp(-x))


def _softplus(x):
    return jnp.maximum(x, 0.0) + jnp.log(1.0 + jnp.exp(-jnp.abs(x)))


def _rwkv_kernel(p_ref, s0_ref, w0_ref, w2_ref, a0_ref, a2_ref, g2_ref, kk_ref, ka_ref, rk_ref, gng_ref, gnb_ref,
                 *rest, T, NB):
    out_ref, st_ref, lw_scr, kd_scr, bb_scr, y_scr, kk_scr, s_scr = rest[-8:]
    C = RW_CHUNK
    n_chunks = T // C
    bd = _head_blockdiag(RW_WIDTH)
    seg = lambda t: _mm_exact_rhs(t, bd, n=2)
    ka = ka_ref[...]
    for nb in range(NB):
        k = p_ref[nb, :, 256:512]
        kk = k * kk_ref[...]
        kk = kk * lax.rsqrt(seg(kk * kk) + 1e-12)
        kk_scr[nb] = kk
        for d in range(2):
            wd = p_ref[nb, :, 768 + 64 * d:832 + 64 * d]
            ad = p_ref[nb, :, 896 + 64 * d:960 + 64 * d]
            w_raw = w0_ref[d:d + 1, :] + _mm(jnp.tanh(wd), w2_ref[d], passes=3)
            lw_scr[nb, d] = -jnp.exp(-_softplus(-w_raw) - 0.5)
            a = _sigmoid(a0_ref[d:d + 1, :] + _mm(ad, a2_ref[d], passes=3))
            kd_scr[nb, d] = k * (1.0 + (a - 1.0) * ka)
            bb_scr[nb, d] = kk * a
    s_scr[...] = s0_ref[:, 0]

    ti = lax.broadcasted_iota(jnp.int32, (C, C), 0)
    si = lax.broadcasted_iota(jnp.int32, (C, C), 1)
    ones_cc = jnp.ones((C, C), F32)
    t2 = lax.broadcasted_iota(jnp.int32, (C, 2 * C), 0)
    col2 = lax.broadcasted_iota(jnp.int32, (C, 2 * C), 1)
    right = col2 >= C
    s2 = jnp.where(right, col2 - C, col2)

    def chunk_body(i, carry):
        ch = []
        for nb, d in [(nb, d) for nb in range(NB) for d in range(2)]:
            ci = i if d == 0 else n_chunks - 1 - i
            rows = pl.ds(pl.multiple_of(ci * C, C), C)
            strict = (ti > si) if d == 0 else (ti < si)
            incl = (ti >= si) if d == 0 else (ti <= si)
            lw = lw_scr[nb, d, rows, :]
            cum = _mm_exact_lhs(incl.astype(F32), lw)
            total = _mm_exact_lhs(ones_cc, lw)
            cum_ex = cum - lw
            mid = 0.5 * total
            rr = p_ref[nb, rows, 0:256]
            vv = p_ref[nb, rows, 512:768]
            kdc = kd_scr[nb, d, rows, :]
            bbc = bb_scr[nb, d, rows, :]
            kkc = kk_scr[nb, rows, :]
            e_inv = jnp.exp(mid - cum)
            At = -kkc * jnp.exp(cum_ex - mid)
            Rt = rr * jnp.exp(cum - mid)
            Bt = bbc * e_inv
            Kt = kdc * e_inv
            Ap = -kkc * jnp.exp(cum_ex)
            Rp = rr * jnp.exp(cum)
            e_out = jnp.exp(total - cum)
            Bh = bbc * e_out
            Kh = kdc * e_out
            e_tot = jnp.exp(total[0:1, :])
            for h in range(RW_HEADS):
                hs = slice(h * HEAD_DIM, (h + 1) * HEAD_DIM)
                ch.append(dict(nb=nb, d=d, h=h, rows=rows, hs=hs, strict=strict, incl=incl,
                               AR=jnp.concatenate([At[:, hs], Rt[:, hs]], axis=0),
                               BK=jnp.concatenate([Bt[:, hs], Kt[:, hs]], axis=0), V=vv[:, hs], X1=Ap[:, hs],
                               Rp=Rp[:, hs], BKh=jnp.concatenate([Bh[:, hs], Kh[:, hs]], axis=0),
                               e_tot=e_tot[:, hs]))
        for c in ch:
            c['G'] = _mm(c['AR'], c['BK'], _NT)
        for c in ch:
            fwd = c['d'] == 0
            c['P'] = jnp.where(c['strict'], c['G'][:C, :C], 0.0)
            c['A_ak0'] = jnp.where(((t2 > s2) if fwd else (t2 < s2)) & right, c['G'][:C], 0.0)
            c['A_r'] = jnp.where((t2 >= s2) if fwd else (t2 <= s2), c['G'][C:], 0.0)
            c['VV'] = jnp.concatenate([c['V'], c['V']], axis=0)
        for c in ch:
            c['X2'] = _mm(c['A_ak0'], c['VV'])
        for lvl in range(6):
            for c in ch:
                if lvl < 5:
                    c['PZ'] = _mm(c['P'], jnp.concatenate([c['P'], c['X1'], c['X2']], axis=1))
                else:
                    c['PZ'] = _mm(c['P'], jnp.concatenate([c['X1'], c['X2']], axis=1))
            for c in ch:
                PZ = c['PZ']
                if lvl < 5:
                    c['P'] = PZ[:, :C]
                    c['X1'] = c['X1'] + PZ[:, C:2 * C]
                    c['X2'] = c['X2'] + PZ[:, 2 * C:]
                else:
                    c['X1'] = c['X1'] + PZ[:, :C]
                    c['X2'] = c['X2'] + PZ[:, C:]
        for c in ch:
            c['S0'] = s_scr[c['nb'], c['d'], c['h']]
            c['UY'] = _mm(jnp.concatenate([c['X1'], c['Rp']], axis=0), c['S0'], _NT)
        for c in ch:
            c['U'] = c['UY'][:C] + c['X2']
        for c in ch:
            UV = jnp.concatenate([c['U'], c['V']], axis=0)
            c['Y'] = c['UY'][C:] + _mm(c['A_r'], UV)
            c['S1'] = c['S0'] * c['e_tot'] + _mm(UV, c['BKh'], _TN)
        for c in ch:
            s_scr[c['nb'], c['d'], c['h']] = c['S1']
            y_scr[c['nb'], c['d'], c['rows'], c['hs']] = c['Y']
        return carry

    lax.fori_loop(0, n_chunks, chunk_body, 0)

    for nb in range(NB):
        r = p_ref[nb, :, 0:256]
        v = p_ref[nb, :, 512:768]
        bonus = seg(r * (kd_scr[nb, 0] + kd_scr[nb, 1]) * rk_ref[...]) * v
        g = _mm(_sigmoid(p_ref[nb, :, 1024:1152]), g2_ref[...], passes=3)
        y = y_scr[nb, 0] + y_scr[nb, 1]
        mu = seg(y) * (1.0 / HEAD_DIM)
        yc = y - mu
        var = seg(yc * yc) * (1.0 / HEAD_DIM)
        yn = yc * lax.rsqrt(var + RW_GN_EPS)
        out_ref[nb] = (yn * gng_ref[...] + gnb_ref[...] + bonus) * g
    st_ref[:, 0] = s_scr[...]


RW_ROWS = 1024


def _state_spec(nb, layer):
    return pl.BlockSpec((nb, 1, 2, RW_HEADS, HEAD_DIM, HEAD_DIM), lambda b: (b, layer, 0, 0, 0, 0))


def rwkv7_mixer_pallas(p_rw, s0, l_in, prm, st_all=None, l_out=0):
    B, T, W = p_rw.shape
    NB = max(2, RW_ROWS // T)
    row = lambda a: a.reshape(1, RW_WIDTH)
    full = lambda shape: pl.BlockSpec(shape, lambda b: (0,) * len(shape))
    st_shape = (B, 1, 2, RW_HEADS, HEAD_DIM, HEAD_DIM) if st_all is None else st_all.shape
    return pl.pallas_call(
        partial(_rwkv_kernel, T=T, NB=NB),
        grid=(B // NB,),
        in_specs=[pl.BlockSpec((NB, T, W), lambda b: (b, 0, 0)), _state_spec(NB, l_in),
                  full((2, RW_WIDTH)), full((2, 64, RW_WIDTH)), full((2, RW_WIDTH)), full((2, 64, RW_WIDTH)),
                  full((128, RW_WIDTH)), full((1, RW_WIDTH)), full((1, RW_WIDTH)), full((1, RW_WIDTH)),
                  full((1, RW_WIDTH)), full((1, RW_WIDTH))]
        + ([] if st_all is None else [pl.BlockSpec(memory_space=pl.ANY)]),
        out_specs=[pl.BlockSpec((NB, T, RW_WIDTH), lambda b: (b, 0, 0)), _state_spec(NB, l_out)],
        out_shape=[jax.ShapeDtypeStruct((B, T, RW_WIDTH), F32), jax.ShapeDtypeStruct(st_shape, F32)],
        input_output_aliases={} if st_all is None else {12: 1},
        scratch_shapes=[pltpu.VMEM((NB, 2, T, RW_WIDTH), F32)] * 4
        + [pltpu.VMEM((NB, T, RW_WIDTH), F32), pltpu.VMEM((NB, 2, RW_HEADS, HEAD_DIM, HEAD_DIM), F32)],
        compiler_params=pltpu.CompilerParams(dimension_semantics=("arbitrary",), vmem_limit_bytes=56 * 1024 * 1024),
        name="rwkv7_mixer",
    )(p_rw, s0, prm['rw_w0'], prm['rw_w2'], prm['rw_a0'], prm['rw_a2'], prm['rw_g2'], row(prm['rw_kk']),
      row(prm['rw_ka']), row(prm['rw_rk']), row(prm['rw_gn_g']), row(prm['rw_gn_b']),
      *([] if st_all is None else [st_all]))


HG_SUB = 16
HG_ROWS = 256


def _hgrn_kernel(p_ref, s0_ref, lb_ref, ng_ref, *rest, T):
    out_ref, st_ref, lf_scr, kf_scr, o_scr, s_scr = rest[-6:]
    R, c = HG_ROWS, HG_SUB
    n_it = T // R
    x = p_ref[0]
    bd = _head_blockdiag(HG_WIDTH)
    seg = lambda t: _mm_exact_rhs(t, bd, n=2)
    for d in range(2):
        lbd = lb_ref[d:d + 1, :]
        f = lbd + (1.0 - lbd) * _sigmoid(x[:, 256 + 256 * d:512 + 256 * d])
        lf_scr[d] = jnp.log(jnp.maximum(f, HG_F_MIN))
        kf_scr[d] = 1.0 - f
        for h in range(HG_HEADS):
            s_scr[d, h] = s0_ref[0, 0, d, h].T

    ti = lax.broadcasted_iota(jnp.int32, (R, R), 0)
    si = lax.broadcasted_iota(jnp.int32, (R, R), 1)
    same_blk = (ti // c) == (si // c)
    t16 = lax.broadcasted_iota(jnp.int32, (c, 1), 0)

    def body(i, carry):
        for d in range(2):
            ci = i if d == 0 else n_it - 1 - i
            rows = pl.ds(pl.multiple_of(ci * R, R), R)
            incl = (ti >= si) if d == 0 else (ti <= si)
            lf = lf_scr[d, rows, :]
            cum = _mm_exact_lhs((incl & same_blk).astype(F32), lf)
            tot = _mm_exact_lhs(same_blk.astype(F32), lf)
            xq = p_ref[0, rows, 0:256]
            q = xq * _sigmoid(xq)
            v = p_ref[0, rows, 768:1024]
            kf = kf_scr[d, rows, :]
            Qp = q * jnp.exp(cum)
            Kh = kf * jnp.exp(tot - cum)
            e_tot = jnp.exp(tot)
            blocks = range(R // c) if d == 0 else range(R // c - 1, -1, -1)
            ST = [s_scr[d, h] for h in range(HG_HEADS)]
            o_parts = [None] * (R // c)
            for j in blocks:
                rs = slice(j * c, (j + 1) * c)
                cb, qb, kb, vb = cum[rs], q[rs], kf[rs], v[rs]
                half = c // 2
                spans = []
                for s in range(c):
                    if d == 0:
                        spans.append((half, c) if s >= half else (0, c))
                    else:
                        spans.append((0, half) if s < half else (0, c))
                prods = []
                for s, (lo, hi) in enumerate(spans):
                    e = jnp.exp(jnp.minimum(cb[lo:hi] - cb[s:s + 1, :], 0.0))
                    prods.append(qb[lo:hi] * (kb[s:s + 1, :] * e))
                att = _mm_exact_rhs(jnp.concatenate(prods, axis=0), bd, n=1)
                o_half = [jnp.zeros((half, HG_WIDTH), F32), jnp.zeros((half, HG_WIDTH), F32)]
                off = 0
                for s, (lo, hi) in enumerate(spans):
                    keep = (t16[lo:hi] >= s) if d == 0 else (t16[lo:hi] <= s)
                    term = jnp.where(keep, att[off:off + hi - lo], 0.0) * vb[s:s + 1, :]
                    off += hi - lo
                    for p in range(2):
                        a, b = max(lo, p * half), min(hi, (p + 1) * half)
                        if a < b:
                            o_half[p] = o_half[p] + term[a - lo:b - lo]
                o_blk = jnp.concatenate(o_half, axis=0)
                o_heads = []
                for h in range(HG_HEADS):
                    hs = slice(h * HEAD_DIM, (h + 1) * HEAD_DIM)
                    o_heads.append(_mm(Qp[rs, hs], ST[h], _NT))
                    ST[h] = ST[h] * e_tot[j * c:j * c + 1, hs] + _mm(vb[:, hs], Kh[rs, hs], _TN)
                o_parts[j] = o_blk + jnp.concatenate(o_heads, axis=1)
            for h in range(HG_HEADS):
                s_scr[d, h] = ST[h]
            o_scr[d, rows, :] = jnp.concatenate(o_parts, axis=0)
        return carry

    lax.fori_loop(0, n_it, body, 0)

    o = o_scr[0] + o_scr[1]
    o = o * lax.rsqrt(seg(o * o) * (1.0 / HEAD_DIM) + NORM_EPS) * ng_ref[...]
    gg = x[:, 1024:1280]
    out_ref[0] = o * (gg * _sigmoid(gg))
    for d in range(2):
        for h in range(HG_HEADS):
            st_ref[0, 0, d, h] = s_scr[d, h].T


def hgrn2_mixer_pallas(p_hg, s0, l_in, lb, norm_g, st_all=None, l_out=0):
    B, T, W = p_hg.shape
    full = lambda shape: pl.BlockSpec(shape, lambda b: (0,) * len(shape))
    st_shape = (B, 1, 2, HG_HEADS, HEAD_DIM, HEAD_DIM) if st_all is None else st_all.shape
    return pl.pallas_call(
        partial(_hgrn_kernel, T=T),
        grid=(B,),
        in_specs=[pl.BlockSpec((1, T, W), lambda b: (b, 0, 0)), _state_spec(1, l_in), full((2, HG_WIDTH)),
                  full((1, HG_WIDTH))] + ([] if st_all is None else [pl.BlockSpec(memory_space=pl.ANY)]),
        out_specs=[pl.BlockSpec((1, T, HG_WIDTH), lambda b: (b, 0, 0)), _state_spec(1, l_out)],
        out_shape=[jax.ShapeDtypeStruct((B, T, HG_WIDTH), F32), jax.ShapeDtypeStruct(st_shape, F32)],
        input_output_aliases={} if st_all is None else {4: 1},
        scratch_shapes=[pltpu.VMEM((2, T, HG_WIDTH), F32)] * 3
        + [pltpu.VMEM((2, HG_HEADS, HEAD_DIM, HEAD_DIM), F32)],
        compiler_params=pltpu.CompilerParams(dimension_semantics=("arbitrary",), vmem_limit_bytes=48 * 1024 * 1024),
        name="hgrn2_mixer",
    )(p_hg, s0, lb, jnp.tile(norm_g.reshape(1, HEAD_DIM), (1, HG_HEADS)), *([] if st_all is None else [st_all]))


ATT_REP = ATT_HEADS // ATT_KV_HEADS
ATT_QROWS = 128


def _swap_pairs(x):
    w = x.shape[-1]
    lane = lax.broadcasted_iota(jnp.int32, x.shape, x.ndim - 1)
    return jnp.where(lane % 2 == 0, pltpu.roll(x, w - 1, x.ndim - 1), pltpu.roll(x, 1, x.ndim - 1))


def _att_kernel(*refs, T, past, rope):
    if rope:
        p_ref, qg_ref, kg_ref, cos_ref, sin_ref, ck_ref, cv_ref, out_ref, k_scr, v_scr, q_scr = refs
    else:
        p_ref, qg_ref, kg_ref, _, _, out_ref, kh_ref, vh_ref, k_scr, v_scr, q_scr = refs
    x = p_ref[0]
    q = x[:, 0:ATT_WIDTH]
    k = x[:, ATT_WIDTH:ATT_WIDTH + KV_WIDTH]
    v = x[:, ATT_WIDTH + KV_WIDTH:ATT_WIDTH + 2 * KV_WIDTH]
    inv_d = 1.0 / HEAD_DIM
    q = q * lax.rsqrt(_mm_exact_rhs(q * q, _head_blockdiag(ATT_WIDTH), n=2) * inv_d + NORM_EPS) * qg_ref[...]
    k = k * lax.rsqrt(_mm_exact_rhs(k * k, _head_blockdiag(KV_WIDTH), n=2) * inv_d + NORM_EPS) * kg_ref[...]
    if rope:
        cos, sin = cos_ref[...], sin_ref[...]
        rep = ATT_WIDTH // KV_WIDTH
        q = q * jnp.concatenate([cos] * rep, axis=1) + _swap_pairs(q) * jnp.concatenate([sin] * rep, axis=1)
        k = k * cos + _swap_pairs(k) * sin
    q_scr[...] = (q * (1.0 / math.sqrt(HEAD_DIM))).astype(BF16)
    for g in range(ATT_KV_HEADS):
        gs = slice(g * HEAD_DIM, (g + 1) * HEAD_DIM)
        if rope:
            k_scr[g, 0:past, :] = ck_ref[0, 0, g].astype(BF16)
            v_scr[g, 0:past, :] = cv_ref[0, 0, g].astype(BF16)
        else:
            kh_ref[0, 0, g] = k[:, gs].T
            vh_ref[0, 0, g] = v[:, gs].T
        k_scr[g, past:past + T, :] = k[:, gs].astype(BF16)
        v_scr[g, past:past + T, :] = v[:, gs].astype(BF16)
    QR = ATT_QROWS

    def q_block(qb, carry):
        rows = pl.ds(pl.multiple_of(qb * QR, QR), QR)
        qblk = q_scr[rows, :]
        for g in range(ATT_KV_HEADS):
            qs = jnp.concatenate([qblk[:, (g * ATT_REP + r) * HEAD_DIM:(g * ATT_REP + r + 1) * HEAD_DIM]
                                  for r in range(ATT_REP)], axis=0)
            s = lax.dot_general(qs, k_scr[g], _NT, preferred_element_type=F32)
            e = jnp.exp(s - jnp.max(s, axis=-1, keepdims=True))
            l = jnp.sum(e, axis=-1, keepdims=True)
            o = lax.dot_general(e.astype(BF16), v_scr[g], _NN, preferred_element_type=F32) / l
            for r in range(ATT_REP):
                h = g * ATT_REP + r
                out_ref[0, rows, h * HEAD_DIM:(h + 1) * HEAD_DIM] = o[r * QR:(r + 1) * QR]
        return carry

    lax.fori_loop(0, T // QR, q_block, 0, unroll=4 if T // QR >= 4 else 2)


def rope_tables(T):
    rows = T // GRID_W
    row = jnp.repeat(jnp.arange(rows, dtype=F32), GRID_W)
    col = jnp.tile(jnp.arange(GRID_W, dtype=F32), rows)
    n_freq = HEAD_DIM // 4
    inv = ROPE_THETA ** (-jnp.arange(n_freq, dtype=F32) / n_freq)
    ang = jnp.concatenate([row[:, None] * inv, col[:, None] * inv], axis=-1)
    cos = jnp.repeat(jnp.cos(ang), 2, axis=-1)
    sin = jnp.stack([-jnp.sin(ang), jnp.sin(ang)], axis=-1).reshape(T, HEAD_DIM)
    return jnp.tile(cos, (1, ATT_KV_HEADS)), jnp.tile(sin, (1, ATT_KV_HEADS))


def attention_pallas(p_att, qnorm_g, knorm_g, l, cache=None, kv_all=None):
    B, T, W = p_att.shape
    rope = cache is not None
    past = cache[0].shape[3] if rope else 0
    full = lambda shape: pl.BlockSpec(shape, lambda b: (0,) * len(shape))
    qg = jnp.tile(qnorm_g.reshape(1, HEAD_DIM), (1, ATT_HEADS))
    kg = jnp.tile(knorm_g.reshape(1, HEAD_DIM), (1, ATT_KV_HEADS))
    in_specs = [pl.BlockSpec((1, T, W), lambda b: (b, 0, 0)), full((1, ATT_WIDTH)), full((1, KV_WIDTH))]
    args = [p_att, qg, kg]
    out_specs = [pl.BlockSpec((1, T, ATT_WIDTH), lambda b: (b, 0, 0))]
    out_shape = [jax.ShapeDtypeStruct((B, T, ATT_WIDTH), F32)]
    if rope:
        cos, sin = rope_tables(T)
        kv_spec = pl.BlockSpec((1, 1, ATT_KV_HEADS, past, HEAD_DIM), lambda b: (b, l, 0, 0, 0))
        in_specs += [full((T, KV_WIDTH)), full((T, KV_WIDTH)), kv_spec, kv_spec]
        args += [cos, sin, cache[0], cache[1]]
        aliases = {}
    else:
        kv_spec = pl.BlockSpec((1, 1, ATT_KV_HEADS, HEAD_DIM, T), lambda b: (b, l, 0, 0, 0))
        in_specs += [pl.BlockSpec(memory_space=pl.ANY)] * 2
        args += list(kv_all)
        aliases = {3: 1, 4: 2}
        out_specs += [kv_spec, kv_spec]
        out_shape += [jax.ShapeDtypeStruct(kv_all[0].shape, F32)] * 2
    res = pl.pallas_call(
        partial(_att_kernel, T=T, past=past, rope=rope),
        grid=(B,),
        in_specs=in_specs,
        out_specs=out_specs,
        out_shape=out_shape,
        input_output_aliases=aliases,
        scratch_shapes=[pltpu.VMEM((ATT_KV_HEADS, past + T, HEAD_DIM), BF16)] * 2
        + [pltpu.VMEM((T, ATT_WIDTH), BF16)],
        compiler_params=pltpu.CompilerParams(dimension_semantics=("arbitrary",), vmem_limit_bytes=48 * 1024 * 1024),
        name="attention_rope" if rope else "attention_ctx",
    )(*args)
    return res[0] if rope else tuple(res)


ROW_TILE = 256
MOD_TILE = 1536
ROUTE_TILE = 256
MOE_SLOTS = 256
MOE_WEIGHT_SPLIT = 4
SC_ROWS = 32
SC_LANES = 128
P_ATT, P_RW, P_HG = ATT_WIDTH + 2 * KV_WIDTH, 3 * RW_WIDTH + 384, 5 * HG_WIDTH


def _mod_kernel(c_ref, w_ref, b_ref, o_ref):
    c = c_ref[...]
    o_ref[0] = _mm(c * _sigmoid(c), w_ref[0], passes=3) + b_ref[0]


def adaln_mod_pallas(cvec, w_mod, b_mod):
    n = 6 * D_MODEL
    return pl.pallas_call(
        _mod_kernel,
        grid=(DEPTH, n // MOD_TILE),
        in_specs=[pl.BlockSpec((8, D_MODEL), lambda l, j: (0, 0)),
                  pl.BlockSpec((1, D_MODEL, MOD_TILE), lambda l, j: (l, 0, j)),
                  pl.BlockSpec((1, 1, MOD_TILE), lambda l, j: (l, 0, j))],
        out_specs=pl.BlockSpec((1, 8, MOD_TILE), lambda l, j: (l, 0, j)),
        out_shape=jax.ShapeDtypeStruct((DEPTH, 8, n), F32),
        compiler_params=pltpu.CompilerParams(dimension_semantics=("arbitrary", "arbitrary"),
                                             vmem_limit_bytes=48 * 1024 * 1024),
        name="adaln_mod",
    )(cvec, w_mod, b_mod.reshape(DEPTH, 1, n))


def _rms(x):
    return x * lax.rsqrt(jnp.mean(x * x, axis=-1, keepdims=True) + NORM_EPS)


HALF = D_MODEL // 2
U32 = jnp.uint32


def _pack_rows(x):
    bits = lax.bitcast_convert_type(x.astype(BF16).astype(F32), U32)
    return (bits[:, :HALF] >> 16) | bits[:, HALF:]


def _unpack_rows(w):
    lo = lax.bitcast_convert_type(w << 16, F32)
    hi = lax.bitcast_convert_type(w & jnp.uint32(0xFFFF0000), F32)
    return lo, hi


def _moe_residual(x_ref, g_ref, g4_ref, pm_ref):
    y = None
    for k in range(TOP_K):
        t = g4_ref[:, k:k + 1] * jnp.concatenate(_unpack_rows(g_ref[k]), axis=1)
        y = t if y is None else y + t
    return x_ref[...] + pm_ref[0, 5:6, :] * y


def _in_kernel(*refs, has_res):
    if has_res:
        x_ref, gth_ref, g4_ref, pm_ref, m_ref, g_ref, w_ref, xo_ref, pa_ref, pr_ref, ph_ref = refs
        x = _moe_residual(x_ref, gth_ref, g4_ref, pm_ref)
        xo_ref[...] = x
    else:
        x_ref, m_ref, g_ref, w_ref, pa_ref, pr_ref, ph_ref = refs
        x = x_ref[...]
    h = _rms(x) * g_ref[...] * (1.0 + m_ref[0, 1:2, :]) + m_ref[0, 0:1, :]
    proj = lax.dot_general(h, w_ref[0], _NN, precision=lax.Precision.DEFAULT, preferred_element_type=F32)
    pa_ref[...] = proj[:, 0:P_ATT]
    pr_ref[...] = proj[:, P_ATT:P_ATT + P_RW]
    ph_ref[...] = proj[:, P_ATT + P_RW:]


def _res_specs(row0):
    t0 = row0 // ROW_TILE
    return [pl.BlockSpec((TOP_K, ROW_TILE, HALF), lambda i: (0, i, 0)),
            pl.BlockSpec((ROW_TILE, TOP_K), lambda i: (i + t0, 0))]


def in_proj_pallas(x, mod, norm_g, w_in, l, rows_per_mod, res=None):
    R = x.shape[0]
    tpm = rows_per_mod // ROW_TILE
    rt = lambda w: pl.BlockSpec((ROW_TILE, w), lambda i: (i, 0))
    ms = pl.BlockSpec((1, 6, D_MODEL), lambda i: (i // tpm, 0, 0))
    full = lambda shape, **kw: pl.BlockSpec(shape, lambda i: (0,) * len(shape), **kw)
    in_specs = [rt(D_MODEL)] + (_res_specs(res[2]) + [ms] if res else []) + [
        ms, full((1, D_MODEL)),
        pl.BlockSpec((1, D_MODEL, w_in.shape[2]), lambda i: (l, 0, 0), pipeline_mode=pl.Buffered(1))]
    args = [x] + ([res[0], res[1], res[3]] if res else []) + [mod, norm_g.reshape(1, D_MODEL), w_in]
    widths = ([D_MODEL] if res else []) + [P_ATT, P_RW, P_HG]
    return pl.pallas_call(
        partial(_in_kernel, has_res=res is not None),
        grid=(R // ROW_TILE,),
        in_specs=in_specs,
        out_specs=[rt(w) for w in widths],
        out_shape=[jax.ShapeDtypeStruct((R, w), F32) for w in widths],
        compiler_params=pltpu.CompilerParams(dimension_semantics=("arbitrary",), vmem_limit_bytes=48 * 1024 * 1024),
        name="in_proj",
    )(*args)


def _out_kernel(att_ref, rw_ref, hg_ref, x_ref, m_ref, g_ref, w_ref, rw_w_ref, rb_ref, *rest):
    xo_ref, h_ref, lg_ref = rest[-3:]
    d = lambda a, lo, hi: lax.dot_general(a, w_ref[0, lo:hi, :], _NN, precision=lax.Precision.DEFAULT,
                                          preferred_element_type=F32)
    mixo = (d(att_ref[...], 0, ATT_WIDTH) + d(rw_ref[...], ATT_WIDTH, ATT_WIDTH + RW_WIDTH)
            + d(hg_ref[...], ATT_WIDTH + RW_WIDTH, ATT_WIDTH + RW_WIDTH + HG_WIDTH))
    x = x_ref[...] + m_ref[0, 2:3, :] * mixo
    xo_ref[...] = x
    h = _rms(x) * g_ref[...] * (1.0 + m_ref[0, 4:5, :]) + m_ref[0, 3:4, :]
    h_ref[...] = _pack_rows(h)
    lg_ref[...] = _mm(h, rw_w_ref[...], passes=3) + rb_ref[...]


def out_proj_pallas(att, rw, hg, x, mod, norm_g, w_out, l, router_w, router_b, rows_per_mod, n_all, row0, joint=None):
    R = x.shape[0]
    tpm = rows_per_mod // ROW_TILE
    t0 = row0 // ROW_TILE
    rt = lambda w: pl.BlockSpec((ROW_TILE, w), lambda i: (i, 0))
    jt = lambda w: pl.BlockSpec((ROW_TILE, w), lambda i: (i + t0, 0))
    full = lambda shape: pl.BlockSpec(shape, lambda i: (0,) * len(shape))
    in_specs = [rt(ATT_WIDTH), rt(RW_WIDTH), rt(HG_WIDTH), rt(D_MODEL),
                pl.BlockSpec((1, 6, D_MODEL), lambda i: (i // tpm, 0, 0)), full((1, D_MODEL)),
                pl.BlockSpec((1, D_MODEL, D_MODEL), lambda i: (l, 0, 0)), full((D_MODEL, N_EXPERTS)),
                full((1, N_EXPERTS))]
    args = [att, rw, hg, x, mod, norm_g.reshape(1, D_MODEL), w_out, router_w, router_b.reshape(1, N_EXPERTS)]
    aliases = {}
    if joint is not None:
        in_specs += [pl.BlockSpec(memory_space=pl.ANY)] * 2
        aliases = {len(args): 1, len(args) + 1: 2}
        args += list(joint)
    return pl.pallas_call(
        _out_kernel,
        grid=(R // ROW_TILE,),
        in_specs=in_specs,
        out_specs=[rt(D_MODEL), jt(HALF), jt(N_EXPERTS)],
        out_shape=[jax.ShapeDtypeStruct((R, D_MODEL), F32), jax.ShapeDtypeStruct((n_all, HALF), U32),
                   jax.ShapeDtypeStruct((n_all, N_EXPERTS), F32)],
        input_output_aliases=aliases,
        compiler_params=pltpu.CompilerParams(dimension_semantics=("arbitrary",), vmem_limit_bytes=48 * 1024 * 1024),
        name="out_proj",
    )(*args)


def _final_kernel(x_ref, gth_ref, g4_ref, m_ref, g_ref, o_ref):
    o_ref[...] = _rms(_moe_residual(x_ref, gth_ref, g4_ref, m_ref)) * g_ref[...]


def final_norm_pallas(x, gathered, gate4, row0, mod, norm_g, rows_per_mod):
    R = x.shape[0]
    tpm = rows_per_mod // ROW_TILE
    rt = pl.BlockSpec((ROW_TILE, D_MODEL), lambda i: (i, 0))
    return pl.pallas_call(
        _final_kernel,
        grid=(R // ROW_TILE,),
        in_specs=[rt] + _res_specs(row0) + [pl.BlockSpec((1, 6, D_MODEL), lambda i: (i // tpm, 0, 0)),
                                            pl.BlockSpec((1, D_MODEL), lambda i: (0, 0))],
        out_specs=rt,
        out_shape=jax.ShapeDtypeStruct((R, D_MODEL), F32),
        name="final_norm",
    )(x, gathered, gate4, mod, norm_g.reshape(1, D_MODEL))


def _moe_max_blocks(n_tok):
    return (n_tok * TOP_K + N_EXPERTS * (MOE_SLOTS - 1)) // MOE_SLOTS


def _route_kernel(lg_ref, dest_ref, gate4_ref, blk_ref, rank_scr, gate_scr, *, n_tok):
    Rt, E = ROUTE_TILE, N_EXPERTS
    n_tiles = n_tok // Rt
    ti = lax.broadcasted_iota(jnp.int32, (Rt, Rt), 0)
    si = lax.broadcasted_iota(jnp.int32, (Rt, Rt), 1)
    before_t = (ti < si).astype(BF16)
    eye_t = (ti == si).astype(BF16)
    ei = lax.broadcasted_iota(jnp.int32, (E, E), 0)
    ej = lax.broadcasted_iota(jnp.int32, (E, E), 1)
    before_e = (ei > ej).astype(BF16)
    sub = lax.broadcasted_iota(jnp.int32, (E, Rt), 0)
    d = lambda a, b, dims: lax.dot_general(a, b, dims, preferred_element_type=F32)

    def tile_members(it, off):
        rows = pl.ds(pl.multiple_of(it * Rt, Rt), Rt)
        l3, l2, l1 = reversed(_split(lg_ref[rows, :], 3))
        lgT = d(l3, eye_t, _TN) + d(l2, eye_t, _TN) + d(l1, eye_t, _TN)
        work = lgT
        member = jnp.zeros((E, Rt), jnp.bool_)
        top = None
        for k in range(TOP_K):
            m = jnp.max(work, axis=0, keepdims=True)
            if top is None:
                top = m
            first = jnp.min(jnp.where(work == m, sub, E), axis=0, keepdims=True)
            pick = sub == first
            member = member | pick
            work = jnp.where(pick, -jnp.inf, work)
        ex = jnp.where(member, jnp.exp(lgT - top), 0.0)
        gate_scr[:, rows] = ex / jnp.sum(ex, axis=0, keepdims=True)
        mem = member.astype(BF16)
        rank = d(mem, before_t, _NN) + off
        rank_scr[:, rows] = jnp.where(member, rank, -1.0)
        return off + jnp.sum(mem.astype(F32), axis=1, keepdims=True)

    count = lax.fori_loop(0, n_tiles, tile_members, jnp.zeros((E, 1), F32))
    nblk = jnp.floor((count + (MOE_SLOTS - 1)) * (1.0 / MOE_SLOTS))
    bstart = d(before_e, jnp.broadcast_to(nblk, (E, 128)).astype(BF16), _NN)[:, 0:1]
    bend = bstart + nblk
    pstart = bstart * MOE_SLOTS

    def tile_slots(it, carry):
        rows = pl.ds(pl.multiple_of(it * Rt, Rt), Rt)
        rank = rank_scr[:, rows]
        gate = gate_scr[:, rows]
        member = rank >= 0.0
        kidx = d(before_e, member.astype(BF16), _NN)
        slot = pstart + rank
        grows = []
        for k in range(TOP_K):
            sel = member & (kidx == k)
            dest_ref[k:k + 1, rows] = jnp.sum(jnp.where(sel, slot, 0.0), axis=0, keepdims=True).astype(jnp.int32)
            grows.append(jnp.sum(jnp.where(sel, gate, 0.0), axis=0, keepdims=True))
        g4t = jnp.concatenate(grows + [jnp.zeros((128 - TOP_K, Rt), F32)], axis=0)
        g3, g2, g1 = reversed(_split(g4t, 3))
        g4 = d(eye_t, g3, _NT) + d(eye_t, g2, _NT) + d(eye_t, g1, _NT)
        gate4_ref[rows, :] = g4[:, 0:TOP_K]
        return carry

    lax.fori_loop(0, n_tiles, tile_slots, 0)
    nb = blk_ref.shape[1]
    bi = lax.broadcasted_iota(jnp.int32, (E, nb), 1).astype(F32)
    owner = jnp.sum((bend <= bi).astype(F32), axis=0, keepdims=True)
    blk_ref[0:1, :] = jnp.minimum(owner, E - 1.0).astype(jnp.int32)
    blk_ref[1:2, :] = jnp.broadcast_to(jnp.sum(nblk, axis=0, keepdims=True), (1, nb)).astype(jnp.int32)
    blk_ref[2:8, :] = jnp.zeros((6, nb), jnp.int32)


def moe_route_pallas(logits):
    n_tok = logits.shape[0]
    nb = -(-_moe_max_blocks(n_tok) // 128) * 128
    return pl.pallas_call(
        partial(_route_kernel, n_tok=n_tok),
        out_shape=[jax.ShapeDtypeStruct((TOP_K, n_tok), jnp.int32),
                   jax.ShapeDtypeStruct((n_tok, TOP_K), F32),
                   jax.ShapeDtypeStruct((8, nb), jnp.int32)],
        scratch_shapes=[pltpu.VMEM((N_EXPERTS, n_tok), F32)] * 2,
        name="moe_route",
    )(logits)


def _moe_block_kernel(be_ref, nu_ref, first_ref, par_ref, nxt_ref, xb_ref, wgu_hbm, bgu_ref, wdn_hbm, bdn_ref, yb_ref,
                      wgu_buf, wdn_buf, sem, *, l):
    i = pl.program_id(0)

    def weight_copies(e, slot):
        cps = []
        for hbm, buf, s in ((wgu_hbm, wgu_buf, 0), (wdn_hbm, wdn_buf, 1)):
            rows = buf.shape[1] // MOE_WEIGHT_SPLIT
            for j in range(MOE_WEIGHT_SPLIT):
                rs = pl.ds(j * rows, rows)
                cps.append(pltpu.make_async_copy(hbm.at[l, e, rs], buf.at[slot, rs], sem.at[s, slot]))
        return cps

    @pl.when(i < nu_ref[0])
    def _():
        slot = par_ref[i]

        @pl.when(first_ref[i] == 1)
        def _():
            @pl.when(i == 0)
            def _():
                for cp in weight_copies(be_ref[0], slot):
                    cp.start()

            for cp in weight_copies(be_ref[i], slot):
                cp.wait()

            @pl.when(nxt_ref[i] >= 0)
            def _():
                for cp in weight_copies(nxt_ref[i], 1 - slot):
                    cp.start()

        dot = lambda a, w: lax.dot_general(a, w, _NN, precision=lax.Precision.DEFAULT, preferred_element_type=F32)
        x_lo, x_hi = _unpack_rows(xb_ref[...])
        gu = dot(x_lo, wgu_buf[slot, 0:HALF]) + dot(x_hi, wgu_buf[slot, HALF:D_MODEL]) + bgu_ref[0, 0]
        glu = jnp.minimum(gu[:, :EXPERT_FF], SWIGLU_LIMIT)
        lin = jnp.clip(gu[:, EXPERT_FF:], -SWIGLU_LIMIT, SWIGLU_LIMIT)
        act = glu * _sigmoid(SWIGLU_ALPHA * glu) * (lin + 1.0)
        yb_ref[...] = _pack_rows(dot(act, wdn_buf[slot]) + bdn_ref[0, 0])


def _expert_runs(block_e, n_used):
    n = block_e.shape[0]
    idx = jnp.arange(n, dtype=jnp.int32)
    valid = idx < n_used[0]
    first = valid & ((idx == 0) | (block_e != jnp.roll(block_e, 1)))
    par = (jnp.cumsum(first.astype(jnp.int32)) - 1) % 2
    start = jnp.where(first, idx, n)
    nxt_start = lax.cummin(jnp.concatenate([start[1:], jnp.full((1,), n, jnp.int32)]), reverse=True)
    nxt = jnp.where(nxt_start < n, block_e[jnp.minimum(nxt_start, n - 1)], -1)
    return first.astype(jnp.int32), par.astype(jnp.int32), nxt.astype(jnp.int32)


def moe_blocks_pallas(xb, block_e, n_used, l, w_gu, b_gu, w_down, b_down):
    n_blocks = xb.shape[0] // MOE_SLOTS
    first, par, nxt = _expert_runs(block_e, n_used)
    blk = lambda i, be, nu, *_: (jnp.minimum(i, nu[0] - 1), 0)
    bsel = lambda i, be, nu, *_: (l, be[jnp.minimum(i, nu[0] - 1)], 0, 0)
    grid_spec = pltpu.PrefetchScalarGridSpec(
        num_scalar_prefetch=5,
        grid=(n_blocks,),
        in_specs=[pl.BlockSpec((MOE_SLOTS, HALF), blk),
                  pl.BlockSpec(memory_space=pl.ANY),
                  pl.BlockSpec((1, 1, 1, 2 * EXPERT_FF), bsel),
                  pl.BlockSpec(memory_space=pl.ANY),
                  pl.BlockSpec((1, 1, 1, D_MODEL), bsel)],
        out_specs=pl.BlockSpec((MOE_SLOTS, HALF), blk),
        scratch_shapes=[pltpu.VMEM((2, D_MODEL, 2 * EXPERT_FF), F32), pltpu.VMEM((2, EXPERT_FF, D_MODEL), F32),
                        pltpu.SemaphoreType.DMA((2, 2))],
    )
    return pl.pallas_call(
        partial(_moe_block_kernel, l=l),
        grid_spec=grid_spec,
        out_shape=jax.ShapeDtypeStruct(xb.shape, U32),
        compiler_params=pltpu.CompilerParams(dimension_semantics=("arbitrary",), vmem_limit_bytes=48 * 1024 * 1024),
        name="moe_blocks",
    )(block_e, n_used, first, par, nxt, xb, w_gu, b_gu.reshape(DEPTH, N_EXPERTS, 1, 2 * EXPERT_FF), w_down,
      b_down.reshape(DEPTH, N_EXPERTS, 1, D_MODEL))


def _sc_mesh():
    return plsc.VectorSubcoreMesh(core_axis_name="c", subcore_axis_name="s")


def _sc_index_rows(idx):
    return jnp.pad(idx.reshape(-1, SC_ROWS), ((0, 0), (0, SC_LANES - SC_ROWS)))


def sc_dispatch(h, dest, n_rows):
    n_tok, d = h.shape
    idx = [_sc_index_rows(dest[k]) for k in range(TOP_K)]

    @pl.kernel(out_type=jax.ShapeDtypeStruct((n_rows, d), h.dtype), mesh=_sc_mesh(), scratch_types=[])
    def kern(h_hbm, i0, i1, i2, i3, o_hbm):
        def body(x_vmem, *i_vmem):
            for iv in i_vmem:
                pltpu.sync_copy(x_vmem, o_hbm.at[iv.at[0, pl.ds(0, SC_ROWS)]])

        pltpu.emit_pipeline(
            body,
            grid=(n_tok // SC_ROWS,),
            in_specs=[pl.BlockSpec((SC_ROWS, d), lambda i: (i, 0))]
            + [pl.BlockSpec((1, SC_LANES), lambda i: (i, 0))] * TOP_K,
            out_specs=[],
            core_axis_name=("c", "s"),
            dimension_semantics=(pltpu.PARALLEL,),
        )(h_hbm, i0, i1, i2, i3)

    return kern(h, *idx)


def sc_combine_gather(yb, dest):
    n_tok = dest.shape[1]
    d = yb.shape[1]
    idx = _sc_index_rows(dest.reshape(TOP_K * n_tok))

    @pl.kernel(out_type=jax.ShapeDtypeStruct((TOP_K * n_tok, d), yb.dtype), mesh=_sc_mesh(), scratch_types=[])
    def kern(y_hbm, i_hbm, o_hbm):
        def body(i_vmem, o_vmem):
            pltpu.sync_copy(y_hbm.at[i_vmem.at[0, pl.ds(0, SC_ROWS)]], o_vmem)

        pltpu.emit_pipeline(
            body,
            grid=(TOP_K * n_tok // SC_ROWS,),
            in_specs=[pl.BlockSpec((1, SC_LANES), lambda i: (i, 0))],
            out_specs=[pl.BlockSpec((SC_ROWS, d), lambda i: (i, 0))],
            core_axis_name=("c", "s"),
            dimension_semantics=(pltpu.PARALLEL,),
        )(i_hbm, o_hbm)

    return kern(yb, idx).reshape(TOP_K, n_tok, d)


def hgrn_lower_bounds(hg_lb):
    sm = jax.nn.softmax(hg_lb.astype(jnp.float32), axis=0)
    return jnp.cumsum(sm, axis=0) - sm[0:1]


def kernel(x_prompt, x_sample, cache_att_k, cache_att_v, state_rwkv, state_hgrn, c, c_ctx, w_mod, b_mod, norm_mix_g, norm_ffn_g, w_in, w_out, att_qnorm_g, att_knorm_g, rw_w0, rw_w2, rw_a0, rw_a2, rw_g2, rw_kk, rw_ka, rw_rk, rw_gn_g, rw_gn_b, hg_lb, hg_norm_g, router_w, router_b, moe_w_gu, moe_b_gu, moe_w_down, moe_b_down, final_norm_g):
    BP, TP, _ = x_prompt.shape
    BS, TS, _ = x_sample.shape
    n_p, n_s = BP * TP, BS * TS
    lb_all = hgrn_lower_bounds(hg_lb)
    cvec = jnp.concatenate([c_ctx[None, :], c, jnp.zeros((8 - 1 - BS, D_MODEL), F32)], axis=0)
    mod_all = adaln_mod_pallas(cvec, w_mod, b_mod).reshape(DEPTH, 8, 6, D_MODEL)
    zeros_state = jnp.zeros((BP, 1, 2, RW_HEADS, HEAD_DIM, HEAD_DIM), F32)
    kv_all = (jnp.zeros((BP, DEPTH, ATT_KV_HEADS, HEAD_DIM, TP), F32),) * 2
    rw_states = jnp.zeros((BP, DEPTH, 2, RW_HEADS, HEAD_DIM, HEAD_DIM), F32)
    hg_states = jnp.zeros((BP, DEPTH, 2, HG_HEADS, HEAD_DIM, HEAD_DIM), F32)
    x = {'p': x_prompt.reshape(n_p, D_MODEL), 's': x_sample.reshape(n_s, D_MODEL)}
    dims = {'p': (TP, BP, n_p, 0), 's': (TS, BS, TS, n_p)}
    moe_out, mod_prev = None, None
    for l in range(DEPTH):
        prm = dict(rw_w0=rw_w0[l], rw_w2=rw_w2[l], rw_a0=rw_a0[l], rw_a2=rw_a2[l], rw_g2=rw_g2[l],
                   rw_kk=rw_kk[l], rw_ka=rw_ka[l], rw_rk=rw_rk[l], rw_gn_g=rw_gn_g[l], rw_gn_b=rw_gn_b[l])
        mods = {'p': mod_all[l, 0:1], 's': mod_all[l, 1:1 + BS]}
        joint = (jnp.zeros((n_p + n_s, HALF), U32), jnp.zeros((n_p + n_s, N_EXPERTS), F32))
        for s in ('p', 's'):
            T, B, rpm, row0 = dims[s]
            if l == 0:
                p_att, p_rw, p_hg = in_proj_pallas(x[s], mods[s], norm_mix_g[l], w_in, l, rpm)
            else:
                x[s], p_att, p_rw, p_hg = in_proj_pallas(x[s], mods[s], norm_mix_g[l], w_in, l, rpm,
                                                         res=(*moe_out[s], row0, mod_prev[s]))
            p_att, p_rw, p_hg = (t.reshape(B, T, -1) for t in (p_att, p_rw, p_hg))
            if s == 'p':
                att, *kv_all = attention_pallas(p_att, att_qnorm_g[l], att_knorm_g[l], l, kv_all=kv_all)
                rw_out, rw_states = rwkv7_mixer_pallas(p_rw, zeros_state, 0, prm, rw_states, l)
                hg_out, hg_states = hgrn2_mixer_pallas(p_hg, zeros_state, 0, lb_all[l], hg_norm_g[l], hg_states, l)
            else:
                att = attention_pallas(p_att, att_qnorm_g[l], att_knorm_g[l], l, cache=(cache_att_k, cache_att_v))
                rw_out, _ = rwkv7_mixer_pallas(p_rw, state_rwkv, l, prm)
                hg_out, _ = hgrn2_mixer_pallas(p_hg, state_hgrn, l, lb_all[l], hg_norm_g[l])
            x[s], *joint = out_proj_pallas(att.reshape(B * T, -1), rw_out.reshape(B * T, -1),
                                           hg_out.reshape(B * T, -1), x[s], mods[s], norm_ffn_g[l], w_out, l,
                                           router_w[l], router_b[l], rpm, n_p + n_s, row0, joint)
        h_all, logits_all = joint
        dest, gate4, blk = moe_route_pallas(logits_all)
        xb = sc_dispatch(h_all, dest, _moe_max_blocks(n_p + n_s) * MOE_SLOTS)
        yb = moe_blocks_pallas(xb, blk[0], blk[1, :1], l, moe_w_gu, moe_b_gu, moe_w_down, moe_b_down)
        moe_out = {'p': (sc_combine_gather(yb, dest[:, :n_p]), gate4), 's': (sc_combine_gather(yb, dest[:, n_p:]), gate4)}
        mod_prev = mods
    y_prompt = final_norm_pallas(x['p'], *moe_out['p'], 0, mod_prev['p'], final_norm_g, n_p)
    y_sample = final_norm_pallas(x['s'], *moe_out['s'], n_p, mod_prev['s'], final_norm_g, TS)
    return (y_prompt.reshape(x_prompt.shape), y_sample.reshape(x_sample.shape),
            jnp.swapaxes(kv_all[0], 3, 4), jnp.swapaxes(kv_all[1], 3, 4),
            rw_states, hg_states)
```

```python
import math
from functools import partial

import jax
import jax.numpy as jnp
from jax import lax
from jax.experimental import pallas as pl
from jax.experimental.pallas import tpu as pltpu
from jax.experimental.pallas import tpu_sc as plsc

D_MODEL = 1024
DEPTH = 2
GRID_W = 64
HEAD_DIM = 64
ATT_HEADS = 8
ATT_KV_HEADS = 2
ATT_WIDTH = ATT_HEADS * HEAD_DIM
KV_WIDTH = ATT_KV_HEADS * HEAD_DIM
RW_HEADS = 4
RW_WIDTH = RW_HEADS * HEAD_DIM
RW_GN_EPS = 64e-5
HG_HEADS = 4
HG_WIDTH = HG_HEADS * HEAD_DIM
HG_F_MIN = 1e-6
N_EXPERTS = 32
TOP_K = 4
EXPERT_FF = D_MODEL
SWIGLU_LIMIT = 7.0
SWIGLU_ALPHA = 1.702
ROPE_THETA = 10000.0
NORM_EPS = 1e-6

RW_CHUNK = 64
BF16 = jnp.bfloat16
F32 = jnp.float32

V7X_VMEM_BYTES = 64 * 1024 * 1024
VMEM_LIMIT = V7X_VMEM_BYTES * 3 // 4
VMEM_LIMIT_RWKV = V7X_VMEM_BYTES * 7 // 8

_NN = (((1,), (0,)), ((), ()))
_NT = (((1,), (1,)), ((), ()))
_TN = (((0,), (0,)), ((), ()))


def _split(x, n):
    parts = []
    for _ in range(n - 1):
        hi = x.astype(BF16)
        parts.append(hi)
        x = x - hi.astype(F32)
    parts.append(x.astype(BF16))
    return parts


def _mm(a, b, dims=_NN, passes=1):
    d = lambda x, y: lax.dot_general(x, y, dims, preferred_element_type=F32)
    if passes == 1:
        return d(a.astype(BF16), b.astype(BF16))
    ah, al = _split(a, 2)
    bh, bl = _split(b, 2)
    return d(ah, bl) + d(al, bh) + d(ah, bh)


def _mm_exact_lhs(a01, b, n=3):
    a = a01.astype(BF16)
    out = None
    for t in reversed(_split(b, n)):
        y = lax.dot_general(a, t, _NN, preferred_element_type=F32)
        out = y if out is None else out + y
    return out


def _mm_exact_rhs(a, b01, n=3):
    b = b01.astype(BF16)
    out = None
    for t in reversed(_split(a, n)):
        y = lax.dot_general(t, b, _NN, preferred_element_type=F32)
        out = y if out is None else out + y
    return out


def _head_blockdiag(width):
    r = lax.broadcasted_iota(jnp.int32, (width, width), 0) // HEAD_DIM
    c = lax.broadcasted_iota(jnp.int32, (width, width), 1) // HEAD_DIM
    return (r == c).astype(F32)


def _sigmoid(x):
    return 1.0 / (1.0 + jnp.exp(-x))


def _softplus(x):
    return jnp.maximum(x, 0.0) + jnp.log(1.0 + jnp.exp(-jnp.abs(x)))


def _rwkv_kernel(p_ref, s0_ref, w0_ref, w2_ref, a0_ref, a2_ref, g2_ref, kk_ref, ka_ref, rk_ref, gng_ref, gnb_ref,
                 *rest, T, NB):
    out_ref, st_ref, lw_scr, kd_scr, bb_scr, y_scr, kk_scr, s_scr = rest[-8:]
    C = RW_CHUNK
    n_chunks = T // C
    bd = _head_blockdiag(RW_WIDTH)
    seg = lambda t: _mm_exact_rhs(t, bd, n=2)
    ka = ka_ref[...]
    for nb in range(NB):
        k = p_ref[nb, :, 256:512]
        kk = k * kk_ref[...]
        kk = kk * lax.rsqrt(seg(kk * kk) + 1e-12)
        kk_scr[nb] = kk
        for d in range(2):
            wd = p_ref[nb, :, 768 + 64 * d:832 + 64 * d]
            ad = p_ref[nb, :, 896 + 64 * d:960 + 64 * d]
            w_raw = w0_ref[d:d + 1, :] + _mm(jnp.tanh(wd), w2_ref[d])
            lw_scr[nb, d] = -jnp.exp(-_softplus(-w_raw) - 0.5)
            a = _sigmoid(a0_ref[d:d + 1, :] + _mm(ad, a2_ref[d]))
            kd_scr[nb, d] = k * (1.0 + (a - 1.0) * ka)
            bb_scr[nb, d] = kk * a
    s_scr[...] = s0_ref[:, 0]

    ti = lax.broadcasted_iota(jnp.int32, (C, C), 0)
    si = lax.broadcasted_iota(jnp.int32, (C, C), 1)
    ones_cc = jnp.ones((C, C), F32)
    t2 = lax.broadcasted_iota(jnp.int32, (C, 2 * C), 0)
    col2 = lax.broadcasted_iota(jnp.int32, (C, 2 * C), 1)
    right = col2 >= C
    s2 = jnp.where(right, col2 - C, col2)

    def chunk_body(i, carry):
        ch = []
        for nb, d in [(nb, d) for nb in range(NB) for d in range(2)]:
            ci = i if d == 0 else n_chunks - 1 - i
            rows = pl.ds(pl.multiple_of(ci * C, C), C)
            strict = (ti > si) if d == 0 else (ti < si)
            incl = (ti >= si) if d == 0 else (ti <= si)
            lw = lw_scr[nb, d, rows, :]
            cum = _mm_exact_lhs(incl.astype(F32), lw)
            total = _mm_exact_lhs(ones_cc, lw)
            cum_ex = cum - lw
            mid = 0.5 * total
            rr = p_ref[nb, rows, 0:256]
            vv = p_ref[nb, rows, 512:768]
            kdc = kd_scr[nb, d, rows, :]
            bbc = bb_scr[nb, d, rows, :]
            kkc = kk_scr[nb, rows, :]
            e_inv = jnp.exp(mid - cum)
            At = -kkc * jnp.exp(cum_ex - mid)
            Rt = rr * jnp.exp(cum - mid)
            Bt = bbc * e_inv
            Kt = kdc * e_inv
            Ap = -kkc * jnp.exp(cum_ex)
            Rp = rr * jnp.exp(cum)
            e_out = jnp.exp(total - cum)
            Bh = bbc * e_out
            Kh = kdc * e_out
            e_tot = jnp.exp(total[0:1, :])
            for h in range(RW_HEADS):
                hs = slice(h * HEAD_DIM, (h + 1) * HEAD_DIM)
                ch.append(dict(nb=nb, d=d, h=h, rows=rows, hs=hs, strict=strict, incl=incl,
                               AR=jnp.concatenate([At[:, hs], Rt[:, hs]], axis=0),
                               BK=jnp.concatenate([Bt[:, hs], Kt[:, hs]], axis=0), V=vv[:, hs], X1=Ap[:, hs],
                               Rp=Rp[:, hs], BKh=jnp.concatenate([Bh[:, hs], Kh[:, hs]], axis=0),
                               e_tot=e_tot[:, hs]))
        for c in ch:
            c['G'] = _mm(c['AR'], c['BK'], _NT)
        for c in ch:
            fwd = c['d'] == 0
            c['P'] = jnp.where(c['strict'], c['G'][:C, :C], 0.0)
            c['A_ak0'] = jnp.where(((t2 > s2) if fwd else (t2 < s2)) & right, c['G'][:C], 0.0)
            c['A_r'] = jnp.where((t2 >= s2) if fwd else (t2 <= s2), c['G'][C:], 0.0)
            c['VV'] = jnp.concatenate([c['V'], c['V']], axis=0)
        for c in ch:
            c['X2'] = _mm(c['A_ak0'], c['VV'])
        for lvl in range(6):
            for c in ch:
                if lvl < 5:
                    c['PZ'] = _mm(c['P'], jnp.concatenate([c['P'], c['X1'], c['X2']], axis=1))
                else:
                    c['PZ'] = _mm(c['P'], jnp.concatenate([c['X1'], c['X2']], axis=1))
            for c in ch:
                PZ = c['PZ']
                if lvl < 5:
                    c['P'] = PZ[:, :C]
                    c['X1'] = c['X1'] + PZ[:, C:2 * C]
                    c['X2'] = c['X2'] + PZ[:, 2 * C:]
                else:
                    c['X1'] = c['X1'] + PZ[:, :C]
                    c['X2'] = c['X2'] + PZ[:, C:]
        for c in ch:
            c['S0'] = s_scr[c['nb'], c['d'], c['h']]
            c['UY'] = _mm(jnp.concatenate([c['X1'], c['Rp']], axis=0), c['S0'], _NT)
        for c in ch:
            c['U'] = c['UY'][:C] + c['X2']
        for c in ch:
            UV = jnp.concatenate([c['U'], c['V']], axis=0)
            c['Y'] = c['UY'][C:] + _mm(c['A_r'], UV)
            c['S1'] = c['S0'] * c['e_tot'] + _mm(UV, c['BKh'], _TN)
        for c in ch:
            s_scr[c['nb'], c['d'], c['h']] = c['S1']
            y_scr[c['nb'], c['d'], c['rows'], c['hs']] = c['Y']
        return carry

    lax.fori_loop(0, n_chunks, chunk_body, 0)

    for nb in range(NB):
        r = p_ref[nb, :, 0:256]
        v = p_ref[nb, :, 512:768]
        bonus = seg(r * (kd_scr[nb, 0] + kd_scr[nb, 1]) * rk_ref[...]) * v
        g = _mm(_sigmoid(p_ref[nb, :, 1024:1152]), g2_ref[...])
        y = y_scr[nb, 0] + y_scr[nb, 1]
        mu = seg(y) * (1.0 / HEAD_DIM)
        yc = y - mu
        var = seg(yc * yc) * (1.0 / HEAD_DIM)
        yn = yc * lax.rsqrt(var + RW_GN_EPS)
        out_ref[nb] = (yn * gng_ref[...] + gnb_ref[...] + bonus) * g
    st_ref[:, 0] = s_scr[...]


RW_ROWS = 1024


def _state_spec(nb, layer):
    return pl.BlockSpec((nb, 1, 2, RW_HEADS, HEAD_DIM, HEAD_DIM), lambda b: (b, layer, 0, 0, 0, 0))


def rwkv7_mixer_pallas(p_rw, s0, l_in, prm, st_all=None, l_out=0):
    B, T, W = p_rw.shape
    NB = max(2, RW_ROWS // T)
    row = lambda a: a.reshape(1, RW_WIDTH)
    full = lambda shape: pl.BlockSpec(shape, lambda b: (0,) * len(shape))
    st_shape = (B, 1, 2, RW_HEADS, HEAD_DIM, HEAD_DIM) if st_all is None else st_all.shape
    return pl.pallas_call(
        partial(_rwkv_kernel, T=T, NB=NB),
        grid=(B // NB,),
        in_specs=[pl.BlockSpec((NB, T, W), lambda b: (b, 0, 0)), _state_spec(NB, l_in),
                  full((2, RW_WIDTH)), full((2, 64, RW_WIDTH)), full((2, RW_WIDTH)), full((2, 64, RW_WIDTH)),
                  full((128, RW_WIDTH)), full((1, RW_WIDTH)), full((1, RW_WIDTH)), full((1, RW_WIDTH)),
                  full((1, RW_WIDTH)), full((1, RW_WIDTH))]
        + ([] if st_all is None else [pl.BlockSpec(memory_space=pl.ANY)]),
        out_specs=[pl.BlockSpec((NB, T, RW_WIDTH), lambda b: (b, 0, 0)), _state_spec(NB, l_out)],
        out_shape=[jax.ShapeDtypeStruct((B, T, RW_WIDTH), F32), jax.ShapeDtypeStruct(st_shape, F32)],
        input_output_aliases={} if st_all is None else {12: 1},
        scratch_shapes=[pltpu.VMEM((NB, 2, T, RW_WIDTH), F32)] * 4
        + [pltpu.VMEM((NB, T, RW_WIDTH), F32), pltpu.VMEM((NB, 2, RW_HEADS, HEAD_DIM, HEAD_DIM), F32)],
        compiler_params=pltpu.CompilerParams(dimension_semantics=("arbitrary",), vmem_limit_bytes=VMEM_LIMIT_RWKV),
        name="rwkv7_mixer",
    )(p_rw, s0, prm['rw_w0'], prm['rw_w2'], prm['rw_a0'], prm['rw_a2'], prm['rw_g2'], row(prm['rw_kk']),
      row(prm['rw_ka']), row(prm['rw_rk']), row(prm['rw_gn_g']), row(prm['rw_gn_b']),
      *([] if st_all is None else [st_all]))


HG_SUB = 16
HG_ROWS = 256


def _hgrn_kernel(p_ref, s0_ref, lb_ref, ng_ref, *rest, T):
    out_ref, st_ref, lf_scr, kf_scr, o_scr, s_scr = rest[-6:]
    R, c = HG_ROWS, HG_SUB
    n_it = T // R
    x = p_ref[0]
    bd = _head_blockdiag(HG_WIDTH)
    seg = lambda t: _mm_exact_rhs(t, bd, n=2)
    for d in range(2):
        lbd = lb_ref[d:d + 1, :]
        f = lbd + (1.0 - lbd) * _sigmoid(x[:, 256 + 256 * d:512 + 256 * d])
        lf_scr[d] = jnp.log(jnp.maximum(f, HG_F_MIN))
        kf_scr[d] = 1.0 - f
        for h in range(HG_HEADS):
            s_scr[d, h] = s0_ref[0, 0, d, h].T

    ti = lax.broadcasted_iota(jnp.int32, (R, R), 0)
    si = lax.broadcasted_iota(jnp.int32, (R, R), 1)
    same_blk = (ti // c) == (si // c)
    t16 = lax.broadcasted_iota(jnp.int32, (c, 1), 0)

    def body(i, carry):
        for d in range(2):
            ci = i if d == 0 else n_it - 1 - i
            rows = pl.ds(pl.multiple_of(ci * R, R), R)
            incl = (ti >= si) if d == 0 else (ti <= si)
            lf = lf_scr[d, rows, :]
            cum = _mm_exact_lhs((incl & same_blk).astype(F32), lf)
            tot = _mm_exact_lhs(same_blk.astype(F32), lf)
            xq = p_ref[0, rows, 0:256]
            q = xq * _sigmoid(xq)
            v = p_ref[0, rows, 768:1024]
            kf = kf_scr[d, rows, :]
            Qp = q * jnp.exp(cum)
            Kh = kf * jnp.exp(tot - cum)
            e_tot = jnp.exp(tot)
            blocks = range(R // c) if d == 0 else range(R // c - 1, -1, -1)
            ST = [s_scr[d, h] for h in range(HG_HEADS)]
            o_parts = [None] * (R // c)
            for j in blocks:
                rs = slice(j * c, (j + 1) * c)
                cb, qb, kb, vb = cum[rs], q[rs], kf[rs], v[rs]
                half = c // 2
                spans = []
                for s in range(c):
                    if d == 0:
                        spans.append((half, c) if s >= half else (0, c))
                    else:
                        spans.append((0, half) if s < half else (0, c))
                prods = []
                for s, (lo, hi) in enumerate(spans):
                    e = jnp.exp(jnp.minimum(cb[lo:hi] - cb[s:s + 1, :], 0.0))
                    prods.append(qb[lo:hi] * (kb[s:s + 1, :] * e))
                att = _mm_exact_rhs(jnp.concatenate(prods, axis=0), bd, n=1)
                o_half = [jnp.zeros((half, HG_WIDTH), F32), jnp.zeros((half, HG_WIDTH), F32)]
                off = 0
                for s, (lo, hi) in enumerate(spans):
                    keep = (t16[lo:hi] >= s) if d == 0 else (t16[lo:hi] <= s)
                    term = jnp.where(keep, att[off:off + hi - lo], 0.0) * vb[s:s + 1, :]
                    off += hi - lo
                    for p in range(2):
                        a, b = max(lo, p * half), min(hi, (p + 1) * half)
                        if a < b:
                            o_half[p] = o_half[p] + term[a - lo:b - lo]
                o_blk = jnp.concatenate(o_half, axis=0)
                o_heads = []
                for h in range(HG_HEADS):
                    hs = slice(h * HEAD_DIM, (h + 1) * HEAD_DIM)
                    o_heads.append(_mm(Qp[rs, hs], ST[h], _NT))
                    ST[h] = ST[h] * e_tot[j * c:j * c + 1, hs] + _mm(vb[:, hs], Kh[rs, hs], _TN)
                o_parts[j] = o_blk + jnp.concatenate(o_heads, axis=1)
            for h in range(HG_HEADS):
                s_scr[d, h] = ST[h]
            o_scr[d, rows, :] = jnp.concatenate(o_parts, axis=0)
        return carry

    lax.fori_loop(0, n_it, body, 0)

    o = o_scr[0] + o_scr[1]
    o = o * lax.rsqrt(seg(o * o) * (1.0 / HEAD_DIM) + NORM_EPS) * ng_ref[...]
    gg = x[:, 1024:1280]
    out_ref[0] = o * (gg * _sigmoid(gg))
    for d in range(2):
        for h in range(HG_HEADS):
            st_ref[0, 0, d, h] = s_scr[d, h].T


def hgrn2_mixer_pallas(p_hg, s0, l_in, lb, norm_g, st_all=None, l_out=0):
    B, T, W = p_hg.shape
    full = lambda shape: pl.BlockSpec(shape, lambda b: (0,) * len(shape))
    st_shape = (B, 1, 2, HG_HEADS, HEAD_DIM, HEAD_DIM) if st_all is None else st_all.shape
    return pl.pallas_call(
        partial(_hgrn_kernel, T=T),
        grid=(B,),
        in_specs=[pl.BlockSpec((1, T, W), lambda b: (b, 0, 0)), _state_spec(1, l_in), full((2, HG_WIDTH)),
                  full((1, HG_WIDTH))] + ([] if st_all is None else [pl.BlockSpec(memory_space=pl.ANY)]),
        out_specs=[pl.BlockSpec((1, T, HG_WIDTH), lambda b: (b, 0, 0)), _state_spec(1, l_out)],
        out_shape=[jax.ShapeDtypeStruct((B, T, HG_WIDTH), F32), jax.ShapeDtypeStruct(st_shape, F32)],
        input_output_aliases={} if st_all is None else {4: 1},
        scratch_shapes=[pltpu.VMEM((2, T, HG_WIDTH), F32)] * 3
        + [pltpu.VMEM((2, HG_HEADS, HEAD_DIM, HEAD_DIM), F32)],
        compiler_params=pltpu.CompilerParams(dimension_semantics=("arbitrary",), vmem_limit_bytes=VMEM_LIMIT),
        name="hgrn2_mixer",
    )(p_hg, s0, lb, jnp.tile(norm_g.reshape(1, HEAD_DIM), (1, HG_HEADS)), *([] if st_all is None else [st_all]))


ATT_REP = ATT_HEADS // ATT_KV_HEADS
ATT_QROWS = 128


def _swap_pairs(x):
    w = x.shape[-1]
    lane = lax.broadcasted_iota(jnp.int32, x.shape, x.ndim - 1)
    return jnp.where(lane % 2 == 0, pltpu.roll(x, w - 1, x.ndim - 1), pltpu.roll(x, 1, x.ndim - 1))


def _att_kernel(*refs, T, past, rope):
    if rope:
        p_ref, qg_ref, kg_ref, cos_ref, sin_ref, ck_ref, cv_ref, out_ref, k_scr, v_scr, q_scr = refs
    else:
        p_ref, qg_ref, kg_ref, _, _, out_ref, kh_ref, vh_ref, k_scr, v_scr, q_scr = refs
    x = p_ref[0]
    q = x[:, 0:ATT_WIDTH]
    k = x[:, ATT_WIDTH:ATT_WIDTH + KV_WIDTH]
    v = x[:, ATT_WIDTH + KV_WIDTH:ATT_WIDTH + 2 * KV_WIDTH]
    inv_d = 1.0 / HEAD_DIM
    q = q * lax.rsqrt(_mm_exact_rhs(q * q, _head_blockdiag(ATT_WIDTH), n=2) * inv_d + NORM_EPS) * qg_ref[...]
    k = k * lax.rsqrt(_mm_exact_rhs(k * k, _head_blockdiag(KV_WIDTH), n=2) * inv_d + NORM_EPS) * kg_ref[...]
    if rope:
        cos, sin = cos_ref[...], sin_ref[...]
        rep = ATT_WIDTH // KV_WIDTH
        q = q * jnp.concatenate([cos] * rep, axis=1) + _swap_pairs(q) * jnp.concatenate([sin] * rep, axis=1)
        k = k * cos + _swap_pairs(k) * sin
    q_scr[...] = (q * (1.0 / math.sqrt(HEAD_DIM))).astype(BF16)
    for g in range(ATT_KV_HEADS):
        gs = slice(g * HEAD_DIM, (g + 1) * HEAD_DIM)
        if rope:
            k_scr[g, 0:past, :] = ck_ref[0, 0, g].astype(BF16)
            v_scr[g, 0:past, :] = cv_ref[0, 0, g].astype(BF16)
        else:
            kh_ref[0, 0, g] = k[:, gs].T
            vh_ref[0, 0, g] = v[:, gs].T
        k_scr[g, past:past + T, :] = k[:, gs].astype(BF16)
        v_scr[g, past:past + T, :] = v[:, gs].astype(BF16)
    QR = ATT_QROWS

    def q_block(qb, carry):
        rows = pl.ds(pl.multiple_of(qb * QR, QR), QR)
        qblk = q_scr[rows, :]
        for g in range(ATT_KV_HEADS):
            qs = jnp.concatenate([qblk[:, (g * ATT_REP + r) * HEAD_DIM:(g * ATT_REP + r + 1) * HEAD_DIM]
                                  for r in range(ATT_REP)], axis=0)
            s = lax.dot_general(qs, k_scr[g], _NT, preferred_element_type=F32)
            e = jnp.exp(s - jnp.max(s, axis=-1, keepdims=True))
            l = jnp.sum(e, axis=-1, keepdims=True)
            o = lax.dot_general(e.astype(BF16), v_scr[g], _NN, preferred_element_type=F32) / l
            for r in range(ATT_REP):
                h = g * ATT_REP + r
                out_ref[0, rows, h * HEAD_DIM:(h + 1) * HEAD_DIM] = o[r * QR:(r + 1) * QR]
        return carry

    lax.fori_loop(0, T // QR, q_block, 0, unroll=4 if T // QR >= 4 else 2)


def rope_tables(T):
    rows = T // GRID_W
    row = jnp.repeat(jnp.arange(rows, dtype=F32), GRID_W)
    col = jnp.tile(jnp.arange(GRID_W, dtype=F32), rows)
    n_freq = HEAD_DIM // 4
    inv = ROPE_THETA ** (-jnp.arange(n_freq, dtype=F32) / n_freq)
    ang = jnp.concatenate([row[:, None] * inv, col[:, None] * inv], axis=-1)
    cos = jnp.repeat(jnp.cos(ang), 2, axis=-1)
    sin = jnp.stack([-jnp.sin(ang), jnp.sin(ang)], axis=-1).reshape(T, HEAD_DIM)
    return jnp.tile(cos, (1, ATT_KV_HEADS)), jnp.tile(sin, (1, ATT_KV_HEADS))


def attention_pallas(p_att, qnorm_g, knorm_g, l, cache=None, kv_all=None):
    B, T, W = p_att.shape
    rope = cache is not None
    past = cache[0].shape[3] if rope else 0
    full = lambda shape: pl.BlockSpec(shape, lambda b: (0,) * len(shape))
    qg = jnp.tile(qnorm_g.reshape(1, HEAD_DIM), (1, ATT_HEADS))
    kg = jnp.tile(knorm_g.reshape(1, HEAD_DIM), (1, ATT_KV_HEADS))
    in_specs = [pl.BlockSpec((1, T, W), lambda b: (b, 0, 0)), full((1, ATT_WIDTH)), full((1, KV_WIDTH))]
    args = [p_att, qg, kg]
    out_specs = [pl.BlockSpec((1, T, ATT_WIDTH), lambda b: (b, 0, 0))]
    out_shape = [jax.ShapeDtypeStruct((B, T, ATT_WIDTH), F32)]
    if rope:
        cos, sin = rope_tables(T)
        kv_spec = pl.BlockSpec((1, 1, ATT_KV_HEADS, past, HEAD_DIM), lambda b: (b, l, 0, 0, 0))
        in_specs += [full((T, KV_WIDTH)), full((T, KV_WIDTH)), kv_spec, kv_spec]
        args += [cos, sin, cache[0], cache[1]]
        aliases = {}
    else:
        kv_spec = pl.BlockSpec((1, 1, ATT_KV_HEADS, HEAD_DIM, T), lambda b: (b, l, 0, 0, 0))
        in_specs += [pl.BlockSpec(memory_space=pl.ANY)] * 2
        args += list(kv_all)
        aliases = {3: 1, 4: 2}
        out_specs += [kv_spec, kv_spec]
        out_shape += [jax.ShapeDtypeStruct(kv_all[0].shape, F32)] * 2
    res = pl.pallas_call(
        partial(_att_kernel, T=T, past=past, rope=rope),
        grid=(B,),
        in_specs=in_specs,
        out_specs=out_specs,
        out_shape=out_shape,
        input_output_aliases=aliases,
        scratch_shapes=[pltpu.VMEM((ATT_KV_HEADS, past + T, HEAD_DIM), BF16)] * 2
        + [pltpu.VMEM((T, ATT_WIDTH), BF16)],
        compiler_params=pltpu.CompilerParams(dimension_semantics=("arbitrary",), vmem_limit_bytes=VMEM_LIMIT),
        name="attention_rope" if rope else "attention_ctx",
    )(*args)
    return res[0] if rope else tuple(res)


ROW_TILE = 256
MOD_TILE = 1536
ROUTE_TILE = 256
MOE_SLOTS = 256
SC_ROWS = 32
SC_LANES = 128
P_ATT, P_RW, P_HG = ATT_WIDTH + 2 * KV_WIDTH, 3 * RW_WIDTH + 384, 5 * HG_WIDTH


def _mod_kernel(c_ref, w_ref, b_ref, o_ref):
    c = c_ref[...]
    o_ref[0] = _mm(c * _sigmoid(c), w_ref[0], passes=3) + b_ref[0]


def adaln_mod_pallas(cvec, w_mod, b_mod):
    n = 6 * D_MODEL
    return pl.pallas_call(
        _mod_kernel,
        grid=(DEPTH, n // MOD_TILE),
        in_specs=[pl.BlockSpec((8, D_MODEL), lambda l, j: (0, 0)),
                  pl.BlockSpec((1, D_MODEL, MOD_TILE), lambda l, j: (l, 0, j)),
                  pl.BlockSpec((1, 1, MOD_TILE), lambda l, j: (l, 0, j))],
        out_specs=pl.BlockSpec((1, 8, MOD_TILE), lambda l, j: (l, 0, j)),
        out_shape=jax.ShapeDtypeStruct((DEPTH, 8, n), F32),
        compiler_params=pltpu.CompilerParams(dimension_semantics=("arbitrary", "arbitrary"),
                                             vmem_limit_bytes=VMEM_LIMIT),
        name="adaln_mod",
    )(cvec, w_mod, b_mod.reshape(DEPTH, 1, n))


def _rms(x):
    return x * lax.rsqrt(jnp.mean(x * x, axis=-1, keepdims=True) + NORM_EPS)


HALF = D_MODEL // 2
U32 = jnp.uint32


def _pack_rows(x):
    bits = lax.bitcast_convert_type(x.astype(BF16).astype(F32), U32)
    return (bits[:, :HALF] >> 16) | bits[:, HALF:]


def _unpack_rows(w):
    lo = lax.bitcast_convert_type(w << 16, F32)
    hi = lax.bitcast_convert_type(w & jnp.uint32(0xFFFF0000), F32)
    return lo, hi


def _moe_residual(x_ref, g_ref, g4_ref, pm_ref):
    y = None
    for k in range(TOP_K):
        t = g4_ref[:, k:k + 1] * jnp.concatenate(_unpack_rows(g_ref[k]), axis=1)
        y = t if y is None else y + t
    return x_ref[...] + pm_ref[0, 5:6, :] * y


def _in_kernel(*refs, has_res):
    if has_res:
        x_ref, gth_ref, g4_ref, pm_ref, m_ref, g_ref, w_ref, xo_ref, pa_ref, pr_ref, ph_ref = refs
        x = _moe_residual(x_ref, gth_ref, g4_ref, pm_ref)
        xo_ref[...] = x
    else:
        x_ref, m_ref, g_ref, w_ref, pa_ref, pr_ref, ph_ref = refs
        x = x_ref[...]
    h = _rms(x) * g_ref[...] * (1.0 + m_ref[0, 1:2, :]) + m_ref[0, 0:1, :]
    proj = lax.dot_general(h, w_ref[0], _NN, precision=lax.Precision.DEFAULT, preferred_element_type=F32)
    pa_ref[...] = proj[:, 0:P_ATT]
    pr_ref[...] = proj[:, P_ATT:P_ATT + P_RW]
    ph_ref[...] = proj[:, P_ATT + P_RW:]


def _res_specs(row0):
    t0 = row0 // ROW_TILE
    return [pl.BlockSpec((TOP_K, ROW_TILE, HALF), lambda i: (0, i, 0)),
            pl.BlockSpec((ROW_TILE, TOP_K), lambda i: (i + t0, 0))]


def in_proj_pallas(x, mod, norm_g, w_in, l, rows_per_mod, res=None):
    R = x.shape[0]
    tpm = rows_per_mod // ROW_TILE
    rt = lambda w: pl.BlockSpec((ROW_TILE, w), lambda i: (i, 0))
    ms = pl.BlockSpec((1, 6, D_MODEL), lambda i: (i // tpm, 0, 0))
    full = lambda shape, **kw: pl.BlockSpec(shape, lambda i: (0,) * len(shape), **kw)
    in_specs = [rt(D_MODEL)] + (_res_specs(res[2]) + [ms] if res else []) + [
        ms, full((1, D_MODEL)),
        pl.BlockSpec((1, D_MODEL, w_in.shape[2]), lambda i: (l, 0, 0), pipeline_mode=pl.Buffered(1))]
    args = [x] + ([res[0], res[1], res[3]] if res else []) + [mod, norm_g.reshape(1, D_MODEL), w_in]
    widths = ([D_MODEL] if res else []) + [P_ATT, P_RW, P_HG]
    return pl.pallas_call(
        partial(_in_kernel, has_res=res is not None),
        grid=(R // ROW_TILE,),
        in_specs=in_specs,
        out_specs=[rt(w) for w in widths],
        out_shape=[jax.ShapeDtypeStruct((R, w), F32) for w in widths],
        compiler_params=pltpu.CompilerParams(dimension_semantics=("arbitrary",), vmem_limit_bytes=VMEM_LIMIT),
        name="in_proj",
    )(*args)


def _out_kernel(att_ref, rw_ref, hg_ref, x_ref, m_ref, g_ref, w_ref, rw_w_ref, rb_ref, *rest):
    xo_ref, h_ref, lg_ref = rest[-3:]
    d = lambda a, lo, hi: lax.dot_general(a, w_ref[0, lo:hi, :], _NN, precision=lax.Precision.DEFAULT,
                                          preferred_element_type=F32)
    mixo = (d(att_ref[...], 0, ATT_WIDTH) + d(rw_ref[...], ATT_WIDTH, ATT_WIDTH + RW_WIDTH)
            + d(hg_ref[...], ATT_WIDTH + RW_WIDTH, ATT_WIDTH + RW_WIDTH + HG_WIDTH))
    x = x_ref[...] + m_ref[0, 2:3, :] * mixo
    xo_ref[...] = x
    h = _rms(x) * g_ref[...] * (1.0 + m_ref[0, 4:5, :]) + m_ref[0, 3:4, :]
    h_ref[...] = _pack_rows(h)
    lg_ref[...] = _mm(h, rw_w_ref[...], passes=3) + rb_ref[...]


def out_proj_pallas(att, rw, hg, x, mod, norm_g, w_out, l, router_w, router_b, rows_per_mod, n_all, row0, joint=None):
    R = x.shape[0]
    tpm = rows_per_mod // ROW_TILE
    t0 = row0 // ROW_TILE
    rt = lambda w: pl.BlockSpec((ROW_TILE, w), lambda i: (i, 0))
    jt = lambda w: pl.BlockSpec((ROW_TILE, w), lambda i: (i + t0, 0))
    full = lambda shape: pl.BlockSpec(shape, lambda i: (0,) * len(shape))
    in_specs = [rt(ATT_WIDTH), rt(RW_WIDTH), rt(HG_WIDTH), rt(D_MODEL),
                pl.BlockSpec((1, 6, D_MODEL), lambda i: (i // tpm, 0, 0)), full((1, D_MODEL)),
                pl.BlockSpec((1, D_MODEL, D_MODEL), lambda i: (l, 0, 0)), full((D_MODEL, N_EXPERTS)),
                full((1, N_EXPERTS))]
    args = [att, rw, hg, x, mod, norm_g.reshape(1, D_MODEL), w_out, router_w, router_b.reshape(1, N_EXPERTS)]
    aliases = {}
    if joint is not None:
        in_specs += [pl.BlockSpec(memory_space=pl.ANY)] * 2
        aliases = {len(args): 1, len(args) + 1: 2}
        args += list(joint)
    return pl.pallas_call(
        _out_kernel,
        grid=(R // ROW_TILE,),
        in_specs=in_specs,
        out_specs=[rt(D_MODEL), jt(HALF), jt(N_EXPERTS)],
        out_shape=[jax.ShapeDtypeStruct((R, D_MODEL), F32), jax.ShapeDtypeStruct((n_all, HALF), U32),
                   jax.ShapeDtypeStruct((n_all, N_EXPERTS), F32)],
        input_output_aliases=aliases,
        compiler_params=pltpu.CompilerParams(dimension_semantics=("arbitrary",), vmem_limit_bytes=VMEM_LIMIT),
        name="out_proj",
    )(*args)


def _final_kernel(x_ref, gth_ref, g4_ref, m_ref, g_ref, o_ref):
    o_ref[...] = _rms(_moe_residual(x_ref, gth_ref, g4_ref, m_ref)) * g_ref[...]


def final_norm_pallas(x, gathered, gate4, row0, mod, norm_g, rows_per_mod):
    R = x.shape[0]
    tpm = rows_per_mod // ROW_TILE
    rt = pl.BlockSpec((ROW_TILE, D_MODEL), lambda i: (i, 0))
    return pl.pallas_call(
        _final_kernel,
        grid=(R // ROW_TILE,),
        in_specs=[rt] + _res_specs(row0) + [pl.BlockSpec((1, 6, D_MODEL), lambda i: (i // tpm, 0, 0)),
                                            pl.BlockSpec((1, D_MODEL), lambda i: (0, 0))],
        out_specs=rt,
        out_shape=jax.ShapeDtypeStruct((R, D_MODEL), F32),
        name="final_norm",
    )(x, gathered, gate4, mod, norm_g.reshape(1, D_MODEL))


def _moe_max_blocks(n_tok):
    return (n_tok * TOP_K + N_EXPERTS * (MOE_SLOTS - 1)) // MOE_SLOTS


def _route_kernel(lg_ref, dest_ref, gate4_ref, blk_ref, rank_scr, gate_scr, *, n_tok):
    Rt, E = ROUTE_TILE, N_EXPERTS
    n_tiles = n_tok // Rt
    ti = lax.broadcasted_iota(jnp.int32, (Rt, Rt), 0)
    si = lax.broadcasted_iota(jnp.int32, (Rt, Rt), 1)
    before_t = (ti < si).astype(BF16)
    eye_t = (ti == si).astype(BF16)
    ei = lax.broadcasted_iota(jnp.int32, (E, E), 0)
    ej = lax.broadcasted_iota(jnp.int32, (E, E), 1)
    before_e = (ei > ej).astype(BF16)
    sub = lax.broadcasted_iota(jnp.int32, (E, Rt), 0)
    d = lambda a, b, dims: lax.dot_general(a, b, dims, preferred_element_type=F32)

    def tile_members(it, off):
        rows = pl.ds(pl.multiple_of(it * Rt, Rt), Rt)
        l3, l2, l1 = reversed(_split(lg_ref[rows, :], 3))
        lgT = d(l3, eye_t, _TN) + d(l2, eye_t, _TN) + d(l1, eye_t, _TN)
        work = lgT
        member = jnp.zeros((E, Rt), jnp.bool_)
        top = None
        for k in range(TOP_K):
            m = jnp.max(work, axis=0, keepdims=True)
            if top is None:
                top = m
            first = jnp.min(jnp.where(work == m, sub, E), axis=0, keepdims=True)
            pick = sub == first
            member = member | pick
            work = jnp.where(pick, -jnp.inf, work)
        ex = jnp.where(member, jnp.exp(lgT - top), 0.0)
        gate_scr[:, rows] = ex / jnp.sum(ex, axis=0, keepdims=True)
        mem = member.astype(BF16)
        rank = d(mem, before_t, _NN) + off
        rank_scr[:, rows] = jnp.where(member, rank, -1.0)
        return off + jnp.sum(mem.astype(F32), axis=1, keepdims=True)

    count = lax.fori_loop(0, n_tiles, tile_members, jnp.zeros((E, 1), F32))
    nblk = jnp.floor((count + (MOE_SLOTS - 1)) * (1.0 / MOE_SLOTS))
    bstart = d(before_e, jnp.broadcast_to(nblk, (E, 128)).astype(BF16), _NN)[:, 0:1]
    bend = bstart + nblk
    pstart = bstart * MOE_SLOTS

    def tile_slots(it, carry):
        rows = pl.ds(pl.multiple_of(it * Rt, Rt), Rt)
        rank = rank_scr[:, rows]
        gate = gate_scr[:, rows]
        member = rank >= 0.0
        kidx = d(before_e, member.astype(BF16), _NN)
        slot = pstart + rank
        grows = []
        for k in range(TOP_K):
            sel = member & (kidx == k)
            dest_ref[k:k + 1, rows] = jnp.sum(jnp.where(sel, slot, 0.0), axis=0, keepdims=True).astype(jnp.int32)
            grows.append(jnp.sum(jnp.where(sel, gate, 0.0), axis=0, keepdims=True))
        g4t = jnp.concatenate(grows + [jnp.zeros((128 - TOP_K, Rt), F32)], axis=0)
        g3, g2, g1 = reversed(_split(g4t, 3))
        g4 = d(eye_t, g3, _NT) + d(eye_t, g2, _NT) + d(eye_t, g1, _NT)
        gate4_ref[rows, :] = g4[:, 0:TOP_K]
        return carry

    lax.fori_loop(0, n_tiles, tile_slots, 0)
    nb = blk_ref.shape[1]
    bi = lax.broadcasted_iota(jnp.int32, (E, nb), 1).astype(F32)
    owner = jnp.sum((bend <= bi).astype(F32), axis=0, keepdims=True)
    blk_ref[0:1, :] = jnp.minimum(owner, E - 1.0).astype(jnp.int32)
    blk_ref[1:2, :] = jnp.broadcast_to(jnp.sum(nblk, axis=0, keepdims=True), (1, nb)).astype(jnp.int32)
    blk_ref[2:8, :] = jnp.zeros((6, nb), jnp.int32)


def moe_route_pallas(logits):
    n_tok = logits.shape[0]
    nb = -(-_moe_max_blocks(n_tok) // 128) * 128
    return pl.pallas_call(
        partial(_route_kernel, n_tok=n_tok),
        out_shape=[jax.ShapeDtypeStruct((TOP_K, n_tok), jnp.int32),
                   jax.ShapeDtypeStruct((n_tok, TOP_K), F32),
                   jax.ShapeDtypeStruct((8, nb), jnp.int32)],
        scratch_shapes=[pltpu.VMEM((N_EXPERTS, n_tok), F32)] * 2,
        name="moe_route",
    )(logits)


def _moe_block_kernel(be_ref, nu_ref, first_ref, par_ref, nxt_ref, xb_ref, wgu_hbm, bgu_ref, wdn_hbm, bdn_ref, yb_ref,
                      wgu_buf, wdn_buf, sem, *, l):
    i = pl.program_id(0)

    def weight_copies(e, slot):
        return (pltpu.make_async_copy(wgu_hbm.at[l, e], wgu_buf.at[slot], sem.at[0, slot]),
                pltpu.make_async_copy(wdn_hbm.at[l, e], wdn_buf.at[slot], sem.at[1, slot]))

    @pl.when(i < nu_ref[0])
    def _():
        slot = par_ref[i]

        @pl.when(first_ref[i] == 1)
        def _():
            @pl.when(i == 0)
            def _():
                for cp in weight_copies(be_ref[0], slot):
                    cp.start()

            for cp in weight_copies(be_ref[i], slot):
                cp.wait()

            @pl.when(nxt_ref[i] >= 0)
            def _():
                for cp in weight_copies(nxt_ref[i], 1 - slot):
                    cp.start()

        dot = lambda a, w: lax.dot_general(a, w, _NN, precision=lax.Precision.DEFAULT, preferred_element_type=F32)
        x_lo, x_hi = _unpack_rows(xb_ref[...])
        gu = dot(x_lo, wgu_buf[slot, 0:HALF]) + dot(x_hi, wgu_buf[slot, HALF:D_MODEL]) + bgu_ref[0, 0]
        glu = jnp.minimum(gu[:, :EXPERT_FF], SWIGLU_LIMIT)
        lin = jnp.clip(gu[:, EXPERT_FF:], -SWIGLU_LIMIT, SWIGLU_LIMIT)
        act = glu * _sigmoid(SWIGLU_ALPHA * glu) * (lin + 1.0)
        yb_ref[...] = _pack_rows(dot(act, wdn_buf[slot]) + bdn_ref[0, 0])


def _expert_runs(block_e, n_used):
    n = block_e.shape[0]
    idx = jnp.arange(n, dtype=jnp.int32)
    valid = idx < n_used[0]
    first = valid & ((idx == 0) | (block_e != jnp.roll(block_e, 1)))
    par = (jnp.cumsum(first.astype(jnp.int32)) - 1) % 2
    start = jnp.where(first, idx, n)
    nxt_start = lax.cummin(jnp.concatenate([start[1:], jnp.full((1,), n, jnp.int32)]), reverse=True)
    nxt = jnp.where(nxt_start < n, block_e[jnp.minimum(nxt_start, n - 1)], -1)
    return first.astype(jnp.int32), par.astype(jnp.int32), nxt.astype(jnp.int32)


def moe_blocks_pallas(xb, block_e, n_used, l, w_gu, b_gu, w_down, b_down):
    n_blocks = xb.shape[0] // MOE_SLOTS
    first, par, nxt = _expert_runs(block_e, n_used)
    blk = lambda i, be, nu, *_: (jnp.minimum(i, nu[0] - 1), 0)
    bsel = lambda i, be, nu, *_: (l, be[jnp.minimum(i, nu[0] - 1)], 0, 0)
    grid_spec = pltpu.PrefetchScalarGridSpec(
        num_scalar_prefetch=5,
        grid=(n_blocks,),
        in_specs=[pl.BlockSpec((MOE_SLOTS, HALF), blk),
                  pl.BlockSpec(memory_space=pl.ANY),
                  pl.BlockSpec((1, 1, 1, 2 * EXPERT_FF), bsel),
                  pl.BlockSpec(memory_space=pl.ANY),
                  pl.BlockSpec((1, 1, 1, D_MODEL), bsel)],
        out_specs=pl.BlockSpec((MOE_SLOTS, HALF), blk),
        scratch_shapes=[pltpu.VMEM((2, D_MODEL, 2 * EXPERT_FF), F32), pltpu.VMEM((2, EXPERT_FF, D_MODEL), F32),
                        pltpu.SemaphoreType.DMA((2, 2))],
    )
    return pl.pallas_call(
        partial(_moe_block_kernel, l=l),
        grid_spec=grid_spec,
        out_shape=jax.ShapeDtypeStruct(xb.shape, U32),
        compiler_params=pltpu.CompilerParams(dimension_semantics=("arbitrary",), vmem_limit_bytes=VMEM_LIMIT),
        name="moe_blocks",
    )(block_e, n_used, first, par, nxt, xb, w_gu, b_gu.reshape(DEPTH, N_EXPERTS, 1, 2 * EXPERT_FF), w_down,
      b_down.reshape(DEPTH, N_EXPERTS, 1, D_MODEL))


def _sc_mesh():
    return plsc.VectorSubcoreMesh(core_axis_name="c", subcore_axis_name="s")


def _sc_index_rows(idx):
    return jnp.pad(idx.reshape(-1, SC_ROWS), ((0, 0), (0, SC_LANES - SC_ROWS)))


def sc_dispatch(h, dest, n_rows):
    n_tok, d = h.shape
    idx = [_sc_index_rows(dest[k]) for k in range(TOP_K)]

    @pl.kernel(out_type=jax.ShapeDtypeStruct((n_rows, d), h.dtype), mesh=_sc_mesh(), scratch_types=[])
    def kern(h_hbm, i0, i1, i2, i3, o_hbm):
        def body(x_vmem, *i_vmem):
            for iv in i_vmem:
                pltpu.sync_copy(x_vmem, o_hbm.at[iv.at[0, pl.ds(0, SC_ROWS)]])

        pltpu.emit_pipeline(
            body,
            grid=(n_tok // SC_ROWS,),
            in_specs=[pl.BlockSpec((SC_ROWS, d), lambda i: (i, 0))]
            + [pl.BlockSpec((1, SC_LANES), lambda i: (i, 0))] * TOP_K,
            out_specs=[],
            core_axis_name=("c", "s"),
            dimension_semantics=(pltpu.PARALLEL,),
        )(h_hbm, i0, i1, i2, i3)

    return kern(h, *idx)


def sc_combine_gather(yb, dest):
    n_tok = dest.shape[1]
    d = yb.shape[1]
    idx = _sc_index_rows(dest.reshape(TOP_K * n_tok))

    @pl.kernel(out_type=jax.ShapeDtypeStruct((TOP_K * n_tok, d), yb.dtype), mesh=_sc_mesh(), scratch_types=[])
    def kern(y_hbm, i_hbm, o_hbm):
        def body(i_vmem, o_vmem):
            pltpu.sync_copy(y_hbm.at[i_vmem.at[0, pl.ds(0, SC_ROWS)]], o_vmem)

        pltpu.emit_pipeline(
            body,
            grid=(TOP_K * n_tok // SC_ROWS,),
            in_specs=[pl.BlockSpec((1, SC_LANES), lambda i: (i, 0))],
            out_specs=[pl.BlockSpec((SC_ROWS, d), lambda i: (i, 0))],
            core_axis_name=("c", "s"),
            dimension_semantics=(pltpu.PARALLEL,),
        )(i_hbm, o_hbm)

    return kern(yb, idx).reshape(TOP_K, n_tok, d)


def hgrn_lower_bounds(hg_lb):
    sm = jax.nn.softmax(hg_lb.astype(jnp.float32), axis=0)
    return jnp.cumsum(sm, axis=0) - sm[0:1]


def kernel(x_prompt, x_sample, cache_att_k, cache_att_v, state_rwkv, state_hgrn, c, c_ctx, w_mod, b_mod, norm_mix_g, norm_ffn_g, w_in, w_out, att_qnorm_g, att_knorm_g, rw_w0, rw_w2, rw_a0, rw_a2, rw_g2, rw_kk, rw_ka, rw_rk, rw_gn_g, rw_gn_b, hg_lb, hg_norm_g, router_w, router_b, moe_w_gu, moe_b_gu, moe_w_down, moe_b_down, final_norm_g):
    BP, TP, _ = x_prompt.shape
    BS, TS, _ = x_sample.shape
    n_p, n_s = BP * TP, BS * TS
    lb_all = hgrn_lower_bounds(hg_lb)
    cvec = jnp.concatenate([c_ctx[None, :], c, jnp.zeros((8 - 1 - BS, D_MODEL), F32)], axis=0)
    mod_all = adaln_mod_pallas(cvec, w_mod, b_mod).reshape(DEPTH, 8, 6, D_MODEL)
    zeros_state = jnp.zeros((BP, 1, 2, RW_HEADS, HEAD_DIM, HEAD_DIM), F32)
    kv_all = (jnp.zeros((BP, DEPTH, ATT_KV_HEADS, HEAD_DIM, TP), F32),) * 2
    rw_states = jnp.zeros((BP, DEPTH, 2, RW_HEADS, HEAD_DIM, HEAD_DIM), F32)
    hg_states = jnp.zeros((BP, DEPTH, 2, HG_HEADS, HEAD_DIM, HEAD_DIM), F32)
    x = {'p': x_prompt.reshape(n_p, D_MODEL), 's': x_sample.reshape(n_s, D_MODEL)}
    dims = {'p': (TP, BP, n_p, 0), 's': (TS, BS, TS, n_p)}
    moe_out, mod_prev = None, None
    for l in range(DEPTH):
        prm = dict(rw_w0=rw_w0[l], rw_w2=rw_w2[l], rw_a0=rw_a0[l], rw_a2=rw_a2[l], rw_g2=rw_g2[l],
                   rw_kk=rw_kk[l], rw_ka=rw_ka[l], rw_rk=rw_rk[l], rw_gn_g=rw_gn_g[l], rw_gn_b=rw_gn_b[l])
        mods = {'p': mod_all[l, 0:1], 's': mod_all[l, 1:1 + BS]}
        joint = (jnp.zeros((n_p + n_s, HALF), U32), jnp.zeros((n_p + n_s, N_EXPERTS), F32))
        for s in ('p', 's'):
            T, B, rpm, row0 = dims[s]
            if l == 0:
                p_att, p_rw, p_hg = in_proj_pallas(x[s], mods[s], norm_mix_g[l], w_in, l, rpm)
            else:
                x[s], p_att, p_rw, p_hg = in_proj_pallas(x[s], mods[s], norm_mix_g[l], w_in, l, rpm,
                                                         res=(*moe_out[s], row0, mod_prev[s]))
            p_att, p_rw, p_hg = (t.reshape(B, T, -1) for t in (p_att, p_rw, p_hg))
            if s == 'p':
                att, *kv_all = attention_pallas(p_att, att_qnorm_g[l], att_knorm_g[l], l, kv_all=kv_all)
                rw_out, rw_states = rwkv7_mixer_pallas(p_rw, zeros_state, 0, prm, rw_states, l)
                hg_out, hg_states = hgrn2_mixer_pallas(p_hg, zeros_state, 0, lb_all[l], hg_norm_g[l], hg_states, l)
            else:
                att = attention_pallas(p_att, att_qnorm_g[l], att_knorm_g[l], l, cache=(cache_att_k, cache_att_v))
                rw_out, _ = rwkv7_mixer_pallas(p_rw, state_rwkv, l, prm)
                hg_out, _ = hgrn2_mixer_pallas(p_hg, state_hgrn, l, lb_all[l], hg_norm_g[l])
            x[s], *joint = out_proj_pallas(att.reshape(B * T, -1), rw_out.reshape(B * T, -1),
                                           hg_out.reshape(B * T, -1), x[s], mods[s], norm_ffn_g[l], w_out, l,
                                           router_w[l], router_b[l], rpm, n_p + n_s, row0, joint)
        h_all, logits_all = joint
        dest, gate4, blk = moe_route_pallas(logits_all)
        xb = sc_dispatch(h_all, dest, _moe_max_blocks(n_p + n_s) * MOE_SLOTS)
        yb = moe_blocks_pallas(xb, blk[0], blk[1, :1], l, moe_w_gu, moe_b_gu, moe_w_down, moe_b_down)
        moe_out = {'p': (sc_combine_gather(yb, dest[:, :n_p]), gate4), 's': (sc_combine_gather(yb, dest[:, n_p:]), gate4)}
        mod_prev = mods
    y_prompt = final_norm_pallas(x['p'], *moe_out['p'], 0, mod_prev['p'], final_norm_g, n_p)
    y_sample = final_norm_pallas(x['s'], *moe_out['s'], n_p, mod_prev['s'], final_norm_g, TS)
    return (y_prompt.reshape(x_prompt.shape), y_sample.reshape(x_sample.shape),
            jnp.swapaxes(kv_all[0], 3, 4), jnp.swapaxes(kv_all[1], 3, 4),
            rw_states, hg_states)
```

```python
import math
from functools import partial

import jax
import jax.numpy as jnp
from jax import lax
from jax.experimental import pallas as pl
from jax.experimental.pallas import tpu as pltpu
from jax.experimental.pallas import tpu_sc as plsc

D_MODEL = 1024
DEPTH = 2
GRID_W = 64
HEAD_DIM = 64
ATT_HEADS = 8
ATT_KV_HEADS = 2
ATT_WIDTH = ATT_HEADS * HEAD_DIM
KV_WIDTH = ATT_KV_HEADS * HEAD_DIM
RW_HEADS = 4
RW_WIDTH = RW_HEADS * HEAD_DIM
RW_GN_EPS = 64e-5
HG_HEADS = 4
HG_WIDTH = HG_HEADS * HEAD_DIM
HG_F_MIN = 1e-6
N_EXPERTS = 32
TOP_K = 4
EXPERT_FF = D_MODEL
SWIGLU_LIMIT = 7.0
SWIGLU_ALPHA = 1.702
ROPE_THETA = 10000.0
NORM_EPS = 1e-6

RW_CHUNK = 64
BF16 = jnp.bfloat16
F32 = jnp.float32

V7X_VMEM_BYTES = 64 * 1024 * 1024
VMEM_LIMIT = V7X_VMEM_BYTES * 3 // 4
VMEM_LIMIT_RWKV = V7X_VMEM_BYTES * 7 // 8

_NN = (((1,), (0,)), ((), ()))
_NT = (((1,), (1,)), ((), ()))
_TN = (((0,), (0,)), ((), ()))


def _split(x, n):
    parts = []
    for _ in range(n - 1):
        hi = x.astype(BF16)
        parts.append(hi)
        x = x - hi.astype(F32)
    parts.append(x.astype(BF16))
    return parts


def _mm(a, b, dims=_NN, passes=1):
    d = lambda x, y: lax.dot_general(x, y, dims, preferred_element_type=F32)
    if passes == 1:
        return d(a.astype(BF16), b.astype(BF16))
    ah, al = _split(a, 2)
    bh, bl = _split(b, 2)
    return d(ah, bl) + d(al, bh) + d(ah, bh)


def _mm_exact_lhs(a01, b, n=3):
    a = a01.astype(BF16)
    out = None
    for t in reversed(_split(b, n)):
        y = lax.dot_general(a, t, _NN, preferred_element_type=F32)
        out = y if out is None else out + y
    return out


def _mm_exact_rhs(a, b01, n=3):
    b = b01.astype(BF16)
    out = None
    for t in reversed(_split(a, n)):
        y = lax.dot_general(t, b, _NN, preferred_element_type=F32)
        out = y if out is None else out + y
    return out


def _head_blockdiag(width):
    r = lax.broadcasted_iota(jnp.int32, (width, width), 0) // HEAD_DIM
    c = lax.broadcasted_iota(jnp.int32, (width, width), 1) // HEAD_DIM
    return (r == c).astype(F32)


def _sigmoid(x):
    return 1.0 / (1.0 + jnp.exp(-x))


def _softplus(x):
    return jnp.maximum(x, 0.0) + jnp.log(1.0 + jnp.exp(-jnp.abs(x)))


def _rwkv_kernel(p_ref, s0_ref, w0_ref, w2_ref, a0_ref, a2_ref, g2_ref, kk_ref, ka_ref, rk_ref, gng_ref, gnb_ref,
                 *rest, T, NB):
    out_ref, st_ref, lw_scr, kd_scr, bb_scr, y_scr, kk_scr, s_scr = rest[-8:]
    C = RW_CHUNK
    n_chunks = T // C
    bd = _head_blockdiag(RW_WIDTH)
    seg = lambda t: _mm_exact_rhs(t, bd, n=2)
    ka = ka_ref[...]
    for nb in range(NB):
        k = p_ref[nb, :, 256:512]
        kk = k * kk_ref[...]
        kk = kk * lax.rsqrt(seg(kk * kk) + 1e-12)
        kk_scr[nb] = kk
        for d in range(2):
            wd = p_ref[nb, :, 768 + 64 * d:832 + 64 * d]
            ad = p_ref[nb, :, 896 + 64 * d:960 + 64 * d]
            w_raw = w0_ref[d:d + 1, :] + _mm(jnp.tanh(wd), w2_ref[d])
            lw_scr[nb, d] = -jnp.exp(-_softplus(-w_raw) - 0.5)
            a = _sigmoid(a0_ref[d:d + 1, :] + _mm(ad, a2_ref[d]))
            kd_scr[nb, d] = k * (1.0 + (a - 1.0) * ka)
            bb_scr[nb, d] = kk * a
    s_scr[...] = s0_ref[:, 0]

    ti = lax.broadcasted_iota(jnp.int32, (C, C), 0)
    si = lax.broadcasted_iota(jnp.int32, (C, C), 1)
    ones_cc = jnp.ones((C, C), F32)
    t2 = lax.broadcasted_iota(jnp.int32, (C, 2 * C), 0)
    col2 = lax.broadcasted_iota(jnp.int32, (C, 2 * C), 1)
    right = col2 >= C
    s2 = jnp.where(right, col2 - C, col2)

    def chunk_body(i, carry):
        ch = []
        for nb, d in [(nb, d) for nb in range(NB) for d in range(2)]:
            ci = i if d == 0 else n_chunks - 1 - i
            rows = pl.ds(pl.multiple_of(ci * C, C), C)
            strict = (ti > si) if d == 0 else (ti < si)
            incl = (ti >= si) if d == 0 else (ti <= si)
            lw = lw_scr[nb, d, rows, :]
            cum = _mm_exact_lhs(incl.astype(F32), lw)
            total = _mm_exact_lhs(ones_cc, lw)
            cum_ex = cum - lw
            mid = 0.5 * total
            rr = p_ref[nb, rows, 0:256]
            vv = p_ref[nb, rows, 512:768]
            kdc = kd_scr[nb, d, rows, :]
            bbc = bb_scr[nb, d, rows, :]
            kkc = kk_scr[nb, rows, :]
            e_inv = jnp.exp(mid - cum)
            At = -kkc * jnp.exp(cum_ex - mid)
            Rt = rr * jnp.exp(cum - mid)
            Bt = bbc * e_inv
            Kt = kdc * e_inv
            Ap = -kkc * jnp.exp(cum_ex)
            Rp = rr * jnp.exp(cum)
            e_out = jnp.exp(total - cum)
            Bh = bbc * e_out
            Kh = kdc * e_out
            e_tot = jnp.exp(total[0:1, :])
            for h in range(RW_HEADS):
                hs = slice(h * HEAD_DIM, (h + 1) * HEAD_DIM)
                ch.append(dict(nb=nb, d=d, h=h, rows=rows, hs=hs, strict=strict, incl=incl,
                               AR=jnp.concatenate([At[:, hs], Rt[:, hs]], axis=0),
                               BK=jnp.concatenate([Bt[:, hs], Kt[:, hs]], axis=0), V=vv[:, hs], X1=Ap[:, hs],
                               Rp=Rp[:, hs], BKh=jnp.concatenate([Bh[:, hs], Kh[:, hs]], axis=0),
                               e_tot=e_tot[:, hs]))
        for c in ch:
            c['G'] = _mm(c['AR'], c['BK'], _NT)
        for c in ch:
            fwd = c['d'] == 0
            c['P'] = jnp.where(c['strict'], c['G'][:C, :C], 0.0)
            c['A_ak0'] = jnp.where(((t2 > s2) if fwd else (t2 < s2)) & right, c['G'][:C], 0.0)
            c['A_r'] = jnp.where((t2 >= s2) if fwd else (t2 <= s2), c['G'][C:], 0.0)
            c['VV'] = jnp.concatenate([c['V'], c['V']], axis=0)
        for c in ch:
            c['X2'] = _mm(c['A_ak0'], c['VV'])
        for lvl in range(6):
            for c in ch:
                if lvl < 5:
                    c['PZ'] = _mm(c['P'], jnp.concatenate([c['P'], c['X1'], c['X2']], axis=1))
                else:
                    c['PZ'] = _mm(c['P'], jnp.concatenate([c['X1'], c['X2']], axis=1))
            for c in ch:
                PZ = c['PZ']
                if lvl < 5:
                    c['P'] = PZ[:, :C]
                    c['X1'] = c['X1'] + PZ[:, C:2 * C]
                    c['X2'] = c['X2'] + PZ[:, 2 * C:]
                else:
                    c['X1'] = c['X1'] + PZ[:, :C]
                    c['X2'] = c['X2'] + PZ[:, C:]
        for c in ch:
            c['S0'] = s_scr[c['nb'], c['d'], c['h']]
            c['UY'] = _mm(jnp.concatenate([c['X1'], c['Rp']], axis=0), c['S0'], _NT)
        for c in ch:
            c['U'] = c['UY'][:C] + c['X2']
        for c in ch:
            UV = jnp.concatenate([c['U'], c['V']], axis=0)
            c['Y'] = c['UY'][C:] + _mm(c['A_r'], UV)
            c['S1'] = c['S0'] * c['e_tot'] + _mm(UV, c['BKh'], _TN)
        for c in ch:
            s_scr[c['nb'], c['d'], c['h']] = c['S1']
            y_scr[c['nb'], c['d'], c['rows'], c['hs']] = c['Y']
        return carry

    lax.fori_loop(0, n_chunks, chunk_body, 0)

    for nb in range(NB):
        r = p_ref[nb, :, 0:256]
        v = p_ref[nb, :, 512:768]
        bonus = seg(r * (kd_scr[nb, 0] + kd_scr[nb, 1]) * rk_ref[...]) * v
        g = _mm(_sigmoid(p_ref[nb, :, 1024:1152]), g2_ref[...])
        y = y_scr[nb, 0] + y_scr[nb, 1]
        mu = seg(y) * (1.0 / HEAD_DIM)
        yc = y - mu
        var = seg(yc * yc) * (1.0 / HEAD_DIM)
        yn = yc * lax.rsqrt(var + RW_GN_EPS)
        out_ref[nb] = (yn * gng_ref[...] + gnb_ref[...] + bonus) * g
    st_ref[:, 0] = s_scr[...]


RW_ROWS = 1024


def _state_spec(nb, layer):
    return pl.BlockSpec((nb, 1, 2, RW_HEADS, HEAD_DIM, HEAD_DIM), lambda b: (b, layer, 0, 0, 0, 0))


def rwkv7_mixer_pallas(p_rw, s0, l_in, prm, st_all=None, l_out=0):
    B, T, W = p_rw.shape
    NB = max(2, RW_ROWS // T)
    row = lambda a: a.reshape(1, RW_WIDTH)
    full = lambda shape: pl.BlockSpec(shape, lambda b: (0,) * len(shape))
    st_shape = (B, 1, 2, RW_HEADS, HEAD_DIM, HEAD_DIM) if st_all is None else st_all.shape
    return pl.pallas_call(
        partial(_rwkv_kernel, T=T, NB=NB),
        grid=(B // NB,),
        in_specs=[pl.BlockSpec((NB, T, W), lambda b: (b, 0, 0)), _state_spec(NB, l_in),
                  full((2, RW_WIDTH)), full((2, 64, RW_WIDTH)), full((2, RW_WIDTH)), full((2, 64, RW_WIDTH)),
                  full((128, RW_WIDTH)), full((1, RW_WIDTH)), full((1, RW_WIDTH)), full((1, RW_WIDTH)),
                  full((1, RW_WIDTH)), full((1, RW_WIDTH))]
        + ([] if st_all is None else [pl.BlockSpec(memory_space=pl.ANY)]),
        out_specs=[pl.BlockSpec((NB, T, RW_WIDTH), lambda b: (b, 0, 0)), _state_spec(NB, l_out)],
        out_shape=[jax.ShapeDtypeStruct((B, T, RW_WIDTH), F32), jax.ShapeDtypeStruct(st_shape, F32)],
        input_output_aliases={} if st_all is None else {12: 1},
        scratch_shapes=[pltpu.VMEM((NB, 2, T, RW_WIDTH), F32)] * 4
        + [pltpu.VMEM((NB, T, RW_WIDTH), F32), pltpu.VMEM((NB, 2, RW_HEADS, HEAD_DIM, HEAD_DIM), F32)],
        compiler_params=pltpu.CompilerParams(dimension_semantics=("arbitrary",), vmem_limit_bytes=VMEM_LIMIT_RWKV),
        name="rwkv7_mixer",
    )(p_rw, s0, prm['rw_w0'], prm['rw_w2'], prm['rw_a0'], prm['rw_a2'], prm['rw_g2'], row(prm['rw_kk']),
      row(prm['rw_ka']), row(prm['rw_rk']), row(prm['rw_gn_g']), row(prm['rw_gn_b']),
      *([] if st_all is None else [st_all]))


HG_SUB = 16
HG_ROWS = 256


def _hgrn_kernel(p_ref, s0_ref, lb_ref, ng_ref, *rest, T):
    out_ref, st_ref, lf_scr, kf_scr, o_scr, s_scr = rest[-6:]
    R, c = HG_ROWS, HG_SUB
    n_it = T // R
    x = p_ref[0]
    bd = _head_blockdiag(HG_WIDTH)
    seg = lambda t: _mm_exact_rhs(t, bd, n=2)
    for d in range(2):
        lbd = lb_ref[d:d + 1, :]
        f = lbd + (1.0 - lbd) * _sigmoid(x[:, 256 + 256 * d:512 + 256 * d])
        lf_scr[d] = jnp.log(jnp.maximum(f, HG_F_MIN))
        kf_scr[d] = 1.0 - f
        for h in range(HG_HEADS):
            s_scr[d, h] = s0_ref[0, 0, d, h].T

    ti = lax.broadcasted_iota(jnp.int32, (R, R), 0)
    si = lax.broadcasted_iota(jnp.int32, (R, R), 1)
    same_blk = (ti // c) == (si // c)
    t16 = lax.broadcasted_iota(jnp.int32, (c, 1), 0)

    def body(i, carry):
        for d in range(2):
            ci = i if d == 0 else n_it - 1 - i
            rows = pl.ds(pl.multiple_of(ci * R, R), R)
            incl = (ti >= si) if d == 0 else (ti <= si)
            lf = lf_scr[d, rows, :]
            cum = _mm_exact_lhs((incl & same_blk).astype(F32), lf)
            tot = _mm_exact_lhs(same_blk.astype(F32), lf)
            xq = p_ref[0, rows, 0:256]
            q = xq * _sigmoid(xq)
            v = p_ref[0, rows, 768:1024]
            kf = kf_scr[d, rows, :]
            Qp = q * jnp.exp(cum)
            Kh = kf * jnp.exp(tot - cum)
            e_tot = jnp.exp(tot)
            blocks = range(R // c) if d == 0 else range(R // c - 1, -1, -1)
            ST = [s_scr[d, h] for h in range(HG_HEADS)]
            o_parts = [None] * (R // c)
            for j in blocks:
                rs = slice(j * c, (j + 1) * c)
                cb, qb, kb, vb = cum[rs], q[rs], kf[rs], v[rs]
                half = c // 2
                spans = []
                for s in range(c):
                    if d == 0:
                        spans.append((half, c) if s >= half else (0, c))
                    else:
                        spans.append((0, half) if s < half else (0, c))
                prods = []
                for s, (lo, hi) in enumerate(spans):
                    e = jnp.exp(jnp.minimum(cb[lo:hi] - cb[s:s + 1, :], 0.0))
                    prods.append(qb[lo:hi] * (kb[s:s + 1, :] * e))
                att = _mm_exact_rhs(jnp.concatenate(prods, axis=0), bd, n=1)
                o_half = [jnp.zeros((half, HG_WIDTH), F32), jnp.zeros((half, HG_WIDTH), F32)]
                off = 0
                for s, (lo, hi) in enumerate(spans):
                    keep = (t16[lo:hi] >= s) if d == 0 else (t16[lo:hi] <= s)
                    term = jnp.where(keep, att[off:off + hi - lo], 0.0) * vb[s:s + 1, :]
                    off += hi - lo
                    for p in range(2):
                        a, b = max(lo, p * half), min(hi, (p + 1) * half)
                        if a < b:
                            o_half[p] = o_half[p] + term[a - lo:b - lo]
                o_blk = jnp.concatenate(o_half, axis=0)
                o_heads = []
                for h in range(HG_HEADS):
                    hs = slice(h * HEAD_DIM, (h + 1) * HEAD_DIM)
                    o_heads.append(_mm(Qp[rs, hs], ST[h], _NT))
                    ST[h] = ST[h] * e_tot[j * c:j * c + 1, hs] + _mm(vb[:, hs], Kh[rs, hs], _TN)
                o_parts[j] = o_blk + jnp.concatenate(o_heads, axis=1)
            for h in range(HG_HEADS):
                s_scr[d, h] = ST[h]
            o_scr[d, rows, :] = jnp.concatenate(o_parts, axis=0)
        return carry

    lax.fori_loop(0, n_it, body, 0)

    o = o_scr[0] + o_scr[1]
    o = o * lax.rsqrt(seg(o * o) * (1.0 / HEAD_DIM) + NORM_EPS) * ng_ref[...]
    gg = x[:, 1024:1280]
    out_ref[0] = o * (gg * _sigmoid(gg))
    for d in range(2):
        for h in range(HG_HEADS):
            st_ref[0, 0, d, h] = s_scr[d, h].T


def hgrn2_mixer_pallas(p_hg, s0, l_in, lb, norm_g, st_all=None, l_out=0):
    B, T, W = p_hg.shape
    full = lambda shape: pl.BlockSpec(shape, lambda b: (0,) * len(shape))
    st_shape = (B, 1, 2, HG_HEADS, HEAD_DIM, HEAD_DIM) if st_all is None else st_all.shape
    return pl.pallas_call(
        partial(_hgrn_kernel, T=T),
        grid=(B,),
        in_specs=[pl.BlockSpec((1, T, W), lambda b: (b, 0, 0)), _state_spec(1, l_in), full((2, HG_WIDTH)),
                  full((1, HG_WIDTH))] + ([] if st_all is None else [pl.BlockSpec(memory_space=pl.ANY)]),
        out_specs=[pl.BlockSpec((1, T, HG_WIDTH), lambda b: (b, 0, 0)), _state_spec(1, l_out)],
        out_shape=[jax.ShapeDtypeStruct((B, T, HG_WIDTH), F32), jax.ShapeDtypeStruct(st_shape, F32)],
        input_output_aliases={} if st_all is None else {4: 1},
        scratch_shapes=[pltpu.VMEM((2, T, HG_WIDTH), F32)] * 3
        + [pltpu.VMEM((2, HG_HEADS, HEAD_DIM, HEAD_DIM), F32)],
        compiler_params=pltpu.CompilerParams(dimension_semantics=("arbitrary",), vmem_limit_bytes=VMEM_LIMIT),
        name="hgrn2_mixer",
    )(p_hg, s0, lb, jnp.tile(norm_g.reshape(1, HEAD_DIM), (1, HG_HEADS)), *([] if st_all is None else [st_all]))


ATT_REP = ATT_HEADS // ATT_KV_HEADS
ATT_QROWS = 128


def _swap_pairs(x):
    w = x.shape[-1]
    lane = lax.broadcasted_iota(jnp.int32, x.shape, x.ndim - 1)
    return jnp.where(lane % 2 == 0, pltpu.roll(x, w - 1, x.ndim - 1), pltpu.roll(x, 1, x.ndim - 1))


def _att_kernel(*refs, T, past, rope):
    if rope:
        p_ref, qg_ref, kg_ref, cos_ref, sin_ref, ck_ref, cv_ref, out_ref, k_scr, v_scr, q_scr = refs
    else:
        p_ref, qg_ref, kg_ref, _, _, out_ref, kh_ref, vh_ref, k_scr, v_scr, q_scr = refs
    x = p_ref[0]
    q = x[:, 0:ATT_WIDTH]
    k = x[:, ATT_WIDTH:ATT_WIDTH + KV_WIDTH]
    v = x[:, ATT_WIDTH + KV_WIDTH:ATT_WIDTH + 2 * KV_WIDTH]
    inv_d = 1.0 / HEAD_DIM
    q = q * lax.rsqrt(_mm_exact_rhs(q * q, _head_blockdiag(ATT_WIDTH), n=2) * inv_d + NORM_EPS) * qg_ref[...]
    k = k * lax.rsqrt(_mm_exact_rhs(k * k, _head_blockdiag(KV_WIDTH), n=2) * inv_d + NORM_EPS) * kg_ref[...]
    if rope:
        cos, sin = cos_ref[...], sin_ref[...]
        rep = ATT_WIDTH // KV_WIDTH
        q = q * jnp.concatenate([cos] * rep, axis=1) + _swap_pairs(q) * jnp.concatenate([sin] * rep, axis=1)
        k = k * cos + _swap_pairs(k) * sin
    q_scr[...] = (q * (1.0 / math.sqrt(HEAD_DIM))).astype(BF16)
    for g in range(ATT_KV_HEADS):
        gs = slice(g * HEAD_DIM, (g + 1) * HEAD_DIM)
        if rope:
            k_scr[g, 0:past, :] = ck_ref[0, 0, g].astype(BF16)
            v_scr[g, 0:past, :] = cv_ref[0, 0, g].astype(BF16)
        else:
            kh_ref[0, 0, g] = k[:, gs].T
            vh_ref[0, 0, g] = v[:, gs].T
        k_scr[g, past:past + T, :] = k[:, gs].astype(BF16)
        v_scr[g, past:past + T, :] = v[:, gs].astype(BF16)
    QR = ATT_QROWS

    def q_block(qb, carry):
        rows = pl.ds(pl.multiple_of(qb * QR, QR), QR)
        qblk = q_scr[rows, :]
        for g in range(ATT_KV_HEADS):
            qs = jnp.concatenate([qblk[:, (g * ATT_REP + r) * HEAD_DIM:(g * ATT_REP + r + 1) * HEAD_DIM]
                                  for r in range(ATT_REP)], axis=0)
            s = lax.dot_general(qs, k_scr[g], _NT, preferred_element_type=F32)
            e = jnp.exp(s - jnp.max(s, axis=-1, keepdims=True))
            l = jnp.sum(e, axis=-1, keepdims=True)
            o = lax.dot_general(e.astype(BF16), v_scr[g], _NN, preferred_element_type=F32) / l
            for r in range(ATT_REP):
                h = g * ATT_REP + r
                out_ref[0, rows, h * HEAD_DIM:(h + 1) * HEAD_DIM] = o[r * QR:(r + 1) * QR]
        return carry

    lax.fori_loop(0, T // QR, q_block, 0, unroll=4 if T // QR >= 4 else 2)


def rope_tables(T):
    rows = T // GRID_W
    row = jnp.repeat(jnp.arange(rows, dtype=F32), GRID_W)
    col = jnp.tile(jnp.arange(GRID_W, dtype=F32), rows)
    n_freq = HEAD_DIM // 4
    inv = ROPE_THETA ** (-jnp.arange(n_freq, dtype=F32) / n_freq)
    ang = jnp.concatenate([row[:, None] * inv, col[:, None] * inv], axis=-1)
    cos = jnp.repeat(jnp.cos(ang), 2, axis=-1)
    sin = jnp.stack([-jnp.sin(ang), jnp.sin(ang)], axis=-1).reshape(T, HEAD_DIM)
    return jnp.tile(cos, (1, ATT_KV_HEADS)), jnp.tile(sin, (1, ATT_KV_HEADS))


def attention_pallas(p_att, qnorm_g, knorm_g, l, cache=None, kv_all=None):
    B, T, W = p_att.shape
    rope = cache is not None
    past = cache[0].shape[3] if rope else 0
    full = lambda shape: pl.BlockSpec(shape, lambda b: (0,) * len(shape))
    qg = jnp.tile(qnorm_g.reshape(1, HEAD_DIM), (1, ATT_HEADS))
    kg = jnp.tile(knorm_g.reshape(1, HEAD_DIM), (1, ATT_KV_HEADS))
    in_specs = [pl.BlockSpec((1, T, W), lambda b: (b, 0, 0)), full((1, ATT_WIDTH)), full((1, KV_WIDTH))]
    args = [p_att, qg, kg]
    out_specs = [pl.BlockSpec((1, T, ATT_WIDTH), lambda b: (b, 0, 0))]
    out_shape = [jax.ShapeDtypeStruct((B, T, ATT_WIDTH), F32)]
    if rope:
        cos, sin = rope_tables(T)
        kv_spec = pl.BlockSpec((1, 1, ATT_KV_HEADS, past, HEAD_DIM), lambda b: (b, l, 0, 0, 0))
        in_specs += [full((T, KV_WIDTH)), full((T, KV_WIDTH)), kv_spec, kv_spec]
        args += [cos, sin, cache[0], cache[1]]
        aliases = {}
    else:
        kv_spec = pl.BlockSpec((1, 1, ATT_KV_HEADS, HEAD_DIM, T), lambda b: (b, l, 0, 0, 0))
        in_specs += [pl.BlockSpec(memory_space=pl.ANY)] * 2
        args += list(kv_all)
        aliases = {3: 1, 4: 2}
        out_specs += [kv_spec, kv_spec]
        out_shape += [jax.ShapeDtypeStruct(kv_all[0].shape, F32)] * 2
    res = pl.pallas_call(
        partial(_att_kernel, T=T, past=past, rope=rope),
        grid=(B,),
        in_specs=in_specs,
        out_specs=out_specs,
        out_shape=out_shape,
        input_output_aliases=aliases,
        scratch_shapes=[pltpu.VMEM((ATT_KV_HEADS, past + T, HEAD_DIM), BF16)] * 2
        + [pltpu.VMEM((T, ATT_WIDTH), BF16)],
        compiler_params=pltpu.CompilerParams(dimension_semantics=("arbitrary",), vmem_limit_bytes=VMEM_LIMIT),
        name="attention_rope" if rope else "attention_ctx",
    )(*args)
    return res[0] if rope else tuple(res)


ROW_TILE = 256
MOD_TILE = 1536
ROUTE_TILE = 256
MOE_SLOTS = 256
SC_ROWS = 64
SC_LANES = 128
P_ATT, P_RW, P_HG = ATT_WIDTH + 2 * KV_WIDTH, 3 * RW_WIDTH + 384, 5 * HG_WIDTH


def _mod_kernel(c_ref, w_ref, b_ref, o_ref):
    c = c_ref[...]
    o_ref[0] = _mm(c * _sigmoid(c), w_ref[0], passes=3) + b_ref[0]


def adaln_mod_pallas(cvec, w_mod, b_mod):
    n = 6 * D_MODEL
    return pl.pallas_call(
        _mod_kernel,
        grid=(DEPTH, n // MOD_TILE),
        in_specs=[pl.BlockSpec((8, D_MODEL), lambda l, j: (0, 0)),
                  pl.BlockSpec((1, D_MODEL, MOD_TILE), lambda l, j: (l, 0, j)),
                  pl.BlockSpec((1, 1, MOD_TILE), lambda l, j: (l, 0, j))],
        out_specs=pl.BlockSpec((1, 8, MOD_TILE), lambda l, j: (l, 0, j)),
        out_shape=jax.ShapeDtypeStruct((DEPTH, 8, n), F32),
        compiler_params=pltpu.CompilerParams(dimension_semantics=("arbitrary", "arbitrary"),
                                             vmem_limit_bytes=VMEM_LIMIT),
        name="adaln_mod",
    )(cvec, w_mod, b_mod.reshape(DEPTH, 1, n))


def _rms(x):
    return x * lax.rsqrt(jnp.mean(x * x, axis=-1, keepdims=True) + NORM_EPS)


HALF = D_MODEL // 2
U32 = jnp.uint32


def _pack_rows(x):
    bits = lax.bitcast_convert_type(x.astype(BF16).astype(F32), U32)
    return (bits[:, :HALF] >> 16) | bits[:, HALF:]


def _unpack_rows(w):
    lo = lax.bitcast_convert_type(w << 16, F32)
    hi = lax.bitcast_convert_type(w & jnp.uint32(0xFFFF0000), F32)
    return lo, hi


def _moe_residual(x_ref, g_ref, g4_ref, pm_ref):
    y = None
    for k in range(TOP_K):
        t = g4_ref[:, k:k + 1] * jnp.concatenate(_unpack_rows(g_ref[k]), axis=1)
        y = t if y is None else y + t
    return x_ref[...] + pm_ref[0, 5:6, :] * y


def _in_kernel(*refs, has_res):
    if has_res:
        x_ref, gth_ref, g4_ref, pm_ref, m_ref, g_ref, w_ref, xo_ref, pa_ref, pr_ref, ph_ref = refs
        x = _moe_residual(x_ref, gth_ref, g4_ref, pm_ref)
        xo_ref[...] = x
    else:
        x_ref, m_ref, g_ref, w_ref, pa_ref, pr_ref, ph_ref = refs
        x = x_ref[...]
    h = _rms(x) * g_ref[...] * (1.0 + m_ref[0, 1:2, :]) + m_ref[0, 0:1, :]
    proj = lax.dot_general(h, w_ref[0], _NN, precision=lax.Precision.DEFAULT, preferred_element_type=F32)
    pa_ref[...] = proj[:, 0:P_ATT]
    pr_ref[...] = proj[:, P_ATT:P_ATT + P_RW]
    ph_ref[...] = proj[:, P_ATT + P_RW:]


def _res_specs(row0):
    t0 = row0 // ROW_TILE
    return [pl.BlockSpec((TOP_K, ROW_TILE, HALF), lambda i: (0, i, 0)),
            pl.BlockSpec((ROW_TILE, TOP_K), lambda i: (i + t0, 0))]


def in_proj_pallas(x, mod, norm_g, w_in, l, rows_per_mod, res=None):
    R = x.shape[0]
    tpm = rows_per_mod // ROW_TILE
    rt = lambda w: pl.BlockSpec((ROW_TILE, w), lambda i: (i, 0))
    ms = pl.BlockSpec((1, 6, D_MODEL), lambda i: (i // tpm, 0, 0))
    full = lambda shape, **kw: pl.BlockSpec(shape, lambda i: (0,) * len(shape), **kw)
    in_specs = [rt(D_MODEL)] + (_res_specs(res[2]) + [ms] if res else []) + [
        ms, full((1, D_MODEL)),
        pl.BlockSpec((1, D_MODEL, w_in.shape[2]), lambda i: (l, 0, 0), pipeline_mode=pl.Buffered(1))]
    args = [x] + ([res[0], res[1], res[3]] if res else []) + [mod, norm_g.reshape(1, D_MODEL), w_in]
    widths = ([D_MODEL] if res else []) + [P_ATT, P_RW, P_HG]
    return pl.pallas_call(
        partial(_in_kernel, has_res=res is not None),
        grid=(R // ROW_TILE,),
        in_specs=in_specs,
        out_specs=[rt(w) for w in widths],
        out_shape=[jax.ShapeDtypeStruct((R, w), F32) for w in widths],
        compiler_params=pltpu.CompilerParams(dimension_semantics=("arbitrary",), vmem_limit_bytes=VMEM_LIMIT),
        name="in_proj",
    )(*args)


def _out_kernel(att_ref, rw_ref, hg_ref, x_ref, m_ref, g_ref, w_ref, rw_w_ref, rb_ref, *rest):
    xo_ref, h_ref, lg_ref = rest[-3:]
    d = lambda a, lo, hi: lax.dot_general(a, w_ref[0, lo:hi, :], _NN, precision=lax.Precision.DEFAULT,
                                          preferred_element_type=F32)
    mixo = (d(att_ref[...], 0, ATT_WIDTH) + d(rw_ref[...], ATT_WIDTH, ATT_WIDTH + RW_WIDTH)
            + d(hg_ref[...], ATT_WIDTH + RW_WIDTH, ATT_WIDTH + RW_WIDTH + HG_WIDTH))
    x = x_ref[...] + m_ref[0, 2:3, :] * mixo
    xo_ref[...] = x
    h = _rms(x) * g_ref[...] * (1.0 + m_ref[0, 4:5, :]) + m_ref[0, 3:4, :]
    h_ref[...] = _pack_rows(h)
    lg_ref[...] = _mm(h, rw_w_ref[...], passes=3) + rb_ref[...]


def out_proj_pallas(att, rw, hg, x, mod, norm_g, w_out, l, router_w, router_b, rows_per_mod, n_all, row0, joint=None):
    R = x.shape[0]
    tpm = rows_per_mod // ROW_TILE
    t0 = row0 // ROW_TILE
    rt = lambda w: pl.BlockSpec((ROW_TILE, w), lambda i: (i, 0))
    jt = lambda w: pl.BlockSpec((ROW_TILE, w), lambda i: (i + t0, 0))
    full = lambda shape: pl.BlockSpec(shape, lambda i: (0,) * len(shape))
    in_specs = [rt(ATT_WIDTH), rt(RW_WIDTH), rt(HG_WIDTH), rt(D_MODEL),
                pl.BlockSpec((1, 6, D_MODEL), lambda i: (i // tpm, 0, 0)), full((1, D_MODEL)),
                pl.BlockSpec((1, D_MODEL, D_MODEL), lambda i: (l, 0, 0)), full((D_MODEL, N_EXPERTS)),
                full((1, N_EXPERTS))]
    args = [att, rw, hg, x, mod, norm_g.reshape(1, D_MODEL), w_out, router_w, router_b.reshape(1, N_EXPERTS)]
    aliases = {}
    if joint is not None:
        in_specs += [pl.BlockSpec(memory_space=pl.ANY)] * 2
        aliases = {len(args): 1, len(args) + 1: 2}
        args += list(joint)
    return pl.pallas_call(
        _out_kernel,
        grid=(R // ROW_TILE,),
        in_specs=in_specs,
        out_specs=[rt(D_MODEL), jt(HALF), jt(N_EXPERTS)],
        out_shape=[jax.ShapeDtypeStruct((R, D_MODEL), F32), jax.ShapeDtypeStruct((n_all, HALF), U32),
                   jax.ShapeDtypeStruct((n_all, N_EXPERTS), F32)],
        input_output_aliases=aliases,
        compiler_params=pltpu.CompilerParams(dimension_semantics=("arbitrary",), vmem_limit_bytes=VMEM_LIMIT),
        name="out_proj",
    )(*args)


def _final_kernel(x_ref, gth_ref, g4_ref, m_ref, g_ref, o_ref):
    o_ref[...] = _rms(_moe_residual(x_ref, gth_ref, g4_ref, m_ref)) * g_ref[...]


def final_norm_pallas(x, gathered, gate4, row0, mod, norm_g, rows_per_mod):
    R = x.shape[0]
    tpm = rows_per_mod // ROW_TILE
    rt = pl.BlockSpec((ROW_TILE, D_MODEL), lambda i: (i, 0))
    return pl.pallas_call(
        _final_kernel,
        grid=(R // ROW_TILE,),
        in_specs=[rt] + _res_specs(row0) + [pl.BlockSpec((1, 6, D_MODEL), lambda i: (i // tpm, 0, 0)),
                                            pl.BlockSpec((1, D_MODEL), lambda i: (0, 0))],
        out_specs=rt,
        out_shape=jax.ShapeDtypeStruct((R, D_MODEL), F32),
        name="final_norm",
    )(x, gathered, gate4, mod, norm_g.reshape(1, D_MODEL))


def _moe_max_blocks(n_tok):
    return (n_tok * TOP_K + N_EXPERTS * (MOE_SLOTS - 1)) // MOE_SLOTS


def _route_kernel(lg_ref, dest_ref, gate4_ref, blk_ref, rank_scr, gate_scr, *, n_tok):
    Rt, E = ROUTE_TILE, N_EXPERTS
    n_tiles = n_tok // Rt
    ti = lax.broadcasted_iota(jnp.int32, (Rt, Rt), 0)
    si = lax.broadcasted_iota(jnp.int32, (Rt, Rt), 1)
    before_t = (ti < si).astype(BF16)
    eye_t = (ti == si).astype(BF16)
    ei = lax.broadcasted_iota(jnp.int32, (E, E), 0)
    ej = lax.broadcasted_iota(jnp.int32, (E, E), 1)
    before_e = (ei > ej).astype(BF16)
    sub = lax.broadcasted_iota(jnp.int32, (E, Rt), 0)
    d = lambda a, b, dims: lax.dot_general(a, b, dims, preferred_element_type=F32)

    def tile_members(it, off):
        rows = pl.ds(pl.multiple_of(it * Rt, Rt), Rt)
        l3, l2, l1 = reversed(_split(lg_ref[rows, :], 3))
        lgT = d(l3, eye_t, _TN) + d(l2, eye_t, _TN) + d(l1, eye_t, _TN)
        work = lgT
        member = jnp.zeros((E, Rt), jnp.bool_)
        top = None
        for k in range(TOP_K):
            m = jnp.max(work, axis=0, keepdims=True)
            if top is None:
                top = m
            first = jnp.min(jnp.where(work == m, sub, E), axis=0, keepdims=True)
            pick = sub == first
            member = member | pick
            work = jnp.where(pick, -jnp.inf, work)
        ex = jnp.where(member, jnp.exp(lgT - top), 0.0)
        gate_scr[:, rows] = ex / jnp.sum(ex, axis=0, keepdims=True)
        mem = member.astype(BF16)
        rank = d(mem, before_t, _NN) + off
        rank_scr[:, rows] = jnp.where(member, rank, -1.0)
        return off + jnp.sum(mem.astype(F32), axis=1, keepdims=True)

    count = lax.fori_loop(0, n_tiles, tile_members, jnp.zeros((E, 1), F32))
    nblk = jnp.floor((count + (MOE_SLOTS - 1)) * (1.0 / MOE_SLOTS))
    bstart = d(before_e, jnp.broadcast_to(nblk, (E, 128)).astype(BF16), _NN)[:, 0:1]
    bend = bstart + nblk
    pstart = bstart * MOE_SLOTS

    def tile_slots(it, carry):
        rows = pl.ds(pl.multiple_of(it * Rt, Rt), Rt)
        rank = rank_scr[:, rows]
        gate = gate_scr[:, rows]
        member = rank >= 0.0
        kidx = d(before_e, member.astype(BF16), _NN)
        slot = pstart + rank
        grows = []
        for k in range(TOP_K):
            sel = member & (kidx == k)
            dest_ref[k:k + 1, rows] = jnp.sum(jnp.where(sel, slot, 0.0), axis=0, keepdims=True).astype(jnp.int32)
            grows.append(jnp.sum(jnp.where(sel, gate, 0.0), axis=0, keepdims=True))
        g4t = jnp.concatenate(grows + [jnp.zeros((128 - TOP_K, Rt), F32)], axis=0)
        g3, g2, g1 = reversed(_split(g4t, 3))
        g4 = d(eye_t, g3, _NT) + d(eye_t, g2, _NT) + d(eye_t, g1, _NT)
        gate4_ref[rows, :] = g4[:, 0:TOP_K]
        return carry

    lax.fori_loop(0, n_tiles, tile_slots, 0)
    nb = blk_ref.shape[1]
    bi = lax.broadcasted_iota(jnp.int32, (E, nb), 1).astype(F32)
    owner = jnp.sum((bend <= bi).astype(F32), axis=0, keepdims=True)
    blk_ref[0:1, :] = jnp.minimum(owner, E - 1.0).astype(jnp.int32)
    blk_ref[1:2, :] = jnp.broadcast_to(jnp.sum(nblk, axis=0, keepdims=True), (1, nb)).astype(jnp.int32)
    blk_ref[2:8, :] = jnp.zeros((6, nb), jnp.int32)


def moe_route_pallas(logits):
    n_tok = logits.shape[0]
    nb = -(-_moe_max_blocks(n_tok) // 128) * 128
    return pl.pallas_call(
        partial(_route_kernel, n_tok=n_tok),
        out_shape=[jax.ShapeDtypeStruct((TOP_K, n_tok), jnp.int32),
                   jax.ShapeDtypeStruct((n_tok, TOP_K), F32),
                   jax.ShapeDtypeStruct((8, nb), jnp.int32)],
        scratch_shapes=[pltpu.VMEM((N_EXPERTS, n_tok), F32)] * 2,
        name="moe_route",
    )(logits)


def _moe_block_kernel(be_ref, nu_ref, first_ref, par_ref, nxt_ref, xb_ref, wgu_hbm, bgu_ref, wdn_hbm, bdn_ref, yb_ref,
                      wgu_buf, wdn_buf, sem, *, l):
    i = pl.program_id(0)

    def weight_copies(e, slot):
        return (pltpu.make_async_copy(wgu_hbm.at[l, e], wgu_buf.at[slot], sem.at[0, slot]),
                pltpu.make_async_copy(wdn_hbm.at[l, e], wdn_buf.at[slot], sem.at[1, slot]))

    @pl.when(i < nu_ref[0])
    def _():
        slot = par_ref[i]

        @pl.when(first_ref[i] == 1)
        def _():
            @pl.when(i == 0)
            def _():
                for cp in weight_copies(be_ref[0], slot):
                    cp.start()

            for cp in weight_copies(be_ref[i], slot):
                cp.wait()

            @pl.when(nxt_ref[i] >= 0)
            def _():
                for cp in weight_copies(nxt_ref[i], 1 - slot):
                    cp.start()

        dot = lambda a, w: lax.dot_general(a, w, _NN, precision=lax.Precision.DEFAULT, preferred_element_type=F32)
        x_lo, x_hi = _unpack_rows(xb_ref[...])
        gu = dot(x_lo, wgu_buf[slot, 0:HALF]) + dot(x_hi, wgu_buf[slot, HALF:D_MODEL]) + bgu_ref[0, 0]
        glu = jnp.minimum(gu[:, :EXPERT_FF], SWIGLU_LIMIT)
        lin = jnp.clip(gu[:, EXPERT_FF:], -SWIGLU_LIMIT, SWIGLU_LIMIT)
        act = glu * _sigmoid(SWIGLU_ALPHA * glu) * (lin + 1.0)
        yb_ref[...] = _pack_rows(dot(act, wdn_buf[slot]) + bdn_ref[0, 0])


def _expert_runs(block_e, n_used):
    n = block_e.shape[0]
    idx = jnp.arange(n, dtype=jnp.int32)
    valid = idx < n_used[0]
    first = valid & ((idx == 0) | (block_e != jnp.roll(block_e, 1)))
    par = (jnp.cumsum(first.astype(jnp.int32)) - 1) % 2
    start = jnp.where(first, idx, n)
    nxt_start = lax.cummin(jnp.concatenate([start[1:], jnp.full((1,), n, jnp.int32)]), reverse=True)
    nxt = jnp.where(nxt_start < n, block_e[jnp.minimum(nxt_start, n - 1)], -1)
    return first.astype(jnp.int32), par.astype(jnp.int32), nxt.astype(jnp.int32)


def moe_blocks_pallas(xb, block_e, n_used, l, w_gu, b_gu, w_down, b_down):
    n_blocks = xb.shape[0] // MOE_SLOTS
    first, par, nxt = _expert_runs(block_e, n_used)
    blk = lambda i, be, nu, *_: (jnp.minimum(i, nu[0] - 1), 0)
    bsel = lambda i, be, nu, *_: (l, be[jnp.minimum(i, nu[0] - 1)], 0, 0)
    grid_spec = pltpu.PrefetchScalarGridSpec(
        num_scalar_prefetch=5,
        grid=(n_blocks,),
        in_specs=[pl.BlockSpec((MOE_SLOTS, HALF), blk),
                  pl.BlockSpec(memory_space=pl.ANY),
                  pl.BlockSpec((1, 1, 1, 2 * EXPERT_FF), bsel),
                  pl.BlockSpec(memory_space=pl.ANY),
                  pl.BlockSpec((1, 1, 1, D_MODEL), bsel)],
        out_specs=pl.BlockSpec((MOE_SLOTS, HALF), blk),
        scratch_shapes=[pltpu.VMEM((2, D_MODEL, 2 * EXPERT_FF), F32), pltpu.VMEM((2, EXPERT_FF, D_MODEL), F32),
                        pltpu.SemaphoreType.DMA((2, 2))],
    )
    return pl.pallas_call(
        partial(_moe_block_kernel, l=l),
        grid_spec=grid_spec,
        out_shape=jax.ShapeDtypeStruct(xb.shape, U32),
        compiler_params=pltpu.CompilerParams(dimension_semantics=("arbitrary",), vmem_limit_bytes=VMEM_LIMIT),
        name="moe_blocks",
    )(block_e, n_used, first, par, nxt, xb, w_gu, b_gu.reshape(DEPTH, N_EXPERTS, 1, 2 * EXPERT_FF), w_down,
      b_down.reshape(DEPTH, N_EXPERTS, 1, D_MODEL))


def _sc_mesh():
    return plsc.VectorSubcoreMesh(core_axis_name="c", subcore_axis_name="s")


def _sc_index_rows(idx):
    return jnp.pad(idx.reshape(-1, SC_ROWS), ((0, 0), (0, SC_LANES - SC_ROWS)))


def sc_dispatch(h, dest, n_rows):
    n_tok, d = h.shape
    idx = [_sc_index_rows(dest[k]) for k in range(TOP_K)]

    @pl.kernel(out_type=jax.ShapeDtypeStruct((n_rows, d), h.dtype), mesh=_sc_mesh(), scratch_types=[])
    def kern(h_hbm, i0, i1, i2, i3, o_hbm):
        def body(x_vmem, *i_vmem):
            for iv in i_vmem:
                pltpu.sync_copy(x_vmem, o_hbm.at[iv.at[0, pl.ds(0, SC_ROWS)]])

        pltpu.emit_pipeline(
            body,
            grid=(n_tok // SC_ROWS,),
            in_specs=[pl.BlockSpec((SC_ROWS, d), lambda i: (i, 0))]
            + [pl.BlockSpec((1, SC_LANES), lambda i: (i, 0))] * TOP_K,
            out_specs=[],
            core_axis_name=("c", "s"),
            dimension_semantics=(pltpu.PARALLEL,),
        )(h_hbm, i0, i1, i2, i3)

    return kern(h, *idx)


def sc_combine_gather(yb, dest):
    n_tok = dest.shape[1]
    d = yb.shape[1]
    idx = _sc_index_rows(dest.reshape(TOP_K * n_tok))

    @pl.kernel(out_type=jax.ShapeDtypeStruct((TOP_K * n_tok, d), yb.dtype), mesh=_sc_mesh(), scratch_types=[])
    def kern(y_hbm, i_hbm, o_hbm):
        def body(i_vmem, o_vmem):
            pltpu.sync_copy(y_hbm.at[i_vmem.at[0, pl.ds(0, SC_ROWS)]], o_vmem)

        pltpu.emit_pipeline(
            body,
            grid=(TOP_K * n_tok // SC_ROWS,),
            in_specs=[pl.BlockSpec((1, SC_LANES), lambda i: (i, 0))],
            out_specs=[pl.BlockSpec((SC_ROWS, d), lambda i: (i, 0))],
            core_axis_name=("c", "s"),
            dimension_semantics=(pltpu.PARALLEL,),
        )(i_hbm, o_hbm)

    return kern(yb, idx).reshape(TOP_K, n_tok, d)


def hgrn_lower_bounds(hg_lb):
    sm = jax.nn.softmax(hg_lb.astype(jnp.float32), axis=0)
    return jnp.cumsum(sm, axis=0) - sm[0:1]


def kernel(x_prompt, x_sample, cache_att_k, cache_att_v, state_rwkv, state_hgrn, c, c_ctx, w_mod, b_mod, norm_mix_g, norm_ffn_g, w_in, w_out, att_qnorm_g, att_knorm_g, rw_w0, rw_w2, rw_a0, rw_a2, rw_g2, rw_kk, rw_ka, rw_rk, rw_gn_g, rw_gn_b, hg_lb, hg_norm_g, router_w, router_b, moe_w_gu, moe_b_gu, moe_w_down, moe_b_down, final_norm_g):
    BP, TP, _ = x_prompt.shape
    BS, TS, _ = x_sample.shape
    n_p, n_s = BP * TP, BS * TS
    lb_all = hgrn_lower_bounds(hg_lb)
    cvec = jnp.concatenate([c_ctx[None, :], c, jnp.zeros((8 - 1 - BS, D_MODEL), F32)], axis=0)
    mod_all = adaln_mod_pallas(cvec, w_mod, b_mod).reshape(DEPTH, 8, 6, D_MODEL)
    zeros_state = jnp.zeros((BP, 1, 2, RW_HEADS, HEAD_DIM, HEAD_DIM), F32)
    kv_all = (jnp.zeros((BP, DEPTH, ATT_KV_HEADS, HEAD_DIM, TP), F32),) * 2
    rw_states = jnp.zeros((BP, DEPTH, 2, RW_HEADS, HEAD_DIM, HEAD_DIM), F32)
    hg_states = jnp.zeros((BP, DEPTH, 2, HG_HEADS, HEAD_DIM, HEAD_DIM), F32)
    x = {'p': x_prompt.reshape(n_p, D_MODEL), 's': x_sample.reshape(n_s, D_MODEL)}
    dims = {'p': (TP, BP, n_p, 0), 's': (TS, BS, TS, n_p)}
    moe_out, mod_prev = None, None
    for l in range(DEPTH):
        prm = dict(rw_w0=rw_w0[l], rw_w2=rw_w2[l], rw_a0=rw_a0[l], rw_a2=rw_a2[l], rw_g2=rw_g2[l],
                   rw_kk=rw_kk[l], rw_ka=rw_ka[l], rw_rk=rw_rk[l], rw_gn_g=rw_gn_g[l], rw_gn_b=rw_gn_b[l])
        mods = {'p': mod_all[l, 0:1], 's': mod_all[l, 1:1 + BS]}
        joint = (jnp.zeros((n_p + n_s, HALF), U32), jnp.zeros((n_p + n_s, N_EXPERTS), F32))
        for s in ('p', 's'):
            T, B, rpm, row0 = dims[s]
            if l == 0:
                p_att, p_rw, p_hg = in_proj_pallas(x[s], mods[s], norm_mix_g[l], w_in, l, rpm)
            else:
                x[s], p_att, p_rw, p_hg = in_proj_pallas(x[s], mods[s], norm_mix_g[l], w_in, l, rpm,
                                                         res=(*moe_out[s], row0, mod_prev[s]))
            p_att, p_rw, p_hg = (t.reshape(B, T, -1) for t in (p_att, p_rw, p_hg))
            if s == 'p':
                att, *kv_all = attention_pallas(p_att, att_qnorm_g[l], att_knorm_g[l], l, kv_all=kv_all)
                rw_out, rw_states = rwkv7_mixer_pallas(p_rw, zeros_state, 0, prm, rw_states, l)
                hg_out, hg_states = hgrn2_mixer_pallas(p_hg, zeros_state, 0, lb_all[l], hg_norm_g[l], hg_states, l)
            else:
                att = attention_pallas(p_att, att_qnorm_g[l], att_knorm_g[l], l, cache=(cache_att_k, cache_att_v))
                rw_out, _ = rwkv7_mixer_pallas(p_rw, state_rwkv, l, prm)
                hg_out, _ = hgrn2_mixer_pallas(p_hg, state_hgrn, l, lb_all[l], hg_norm_g[l])
            x[s], *joint = out_proj_pallas(att.reshape(B * T, -1), rw_out.reshape(B * T, -1),
                                           hg_out.reshape(B * T, -1), x[s], mods[s], norm_ffn_g[l], w_out, l,
                                           router_w[l], router_b[l], rpm, n_p + n_s, row0, joint)
        h_all, logits_all = joint
        dest, gate4, blk = moe_route_pallas(logits_all)
        xb = sc_dispatch(h_all, dest, _moe_max_blocks(n_p + n_s) * MOE_SLOTS)
        yb = moe_blocks_pallas(xb, blk[0], blk[1, :1], l, moe_w_gu, moe_b_gu, moe_w_down, moe_b_down)
        moe_out = {'p': (sc_combine_gather(yb, dest[:, :n_p]), gate4), 's': (sc_combine_gather(yb, dest[:, n_p:]), gate4)}
        mod_prev = mods
    y_prompt = final_norm_pallas(x['p'], *moe_out['p'], 0, mod_prev['p'], final_norm_g, n_p)
    y_sample = final_norm_pallas(x['s'], *moe_out['s'], n_p, mod_prev['s'], final_norm_g, TS)
    return (y_prompt.reshape(x_prompt.shape), y_sample.reshape(x_sample.shape),
            jnp.swapaxes(kv_all[0], 3, 4), jnp.swapaxes(kv_all[1], 3, 4),
            rw_states, hg_states)
```

```python
import math
from functools import partial

import jax
import jax.numpy as jnp
from jax import lax
from jax.experimental import pallas as pl
from jax.experimental.pallas import tpu as pltpu
from jax.experimental.pallas import tpu_sc as plsc

D_MODEL = 1024
DEPTH = 2
GRID_W = 64
HEAD_DIM = 64
ATT_HEADS = 8
ATT_KV_HEADS = 2
ATT_WIDTH = ATT_HEADS * HEAD_DIM
KV_WIDTH = ATT_KV_HEADS * HEAD_DIM
RW_HEADS = 4
RW_WIDTH = RW_HEADS * HEAD_DIM
RW_GN_EPS = 64e-5
HG_HEADS = 4
HG_WIDTH = HG_HEADS * HEAD_DIM
HG_F_MIN = 1e-6
N_EXPERTS = 32
TOP_K = 4
EXPERT_FF = D_MODEL
SWIGLU_LIMIT = 7.0
SWIGLU_ALPHA = 1.702
ROPE_THETA = 10000.0
NORM_EPS = 1e-6

RW_CHUNK = 64
BF16 = jnp.bfloat16
F32 = jnp.float32

V7X_VMEM_BYTES = 64 * 1024 * 1024
VMEM_LIMIT = V7X_VMEM_BYTES * 3 // 4
VMEM_LIMIT_RWKV = V7X_VMEM_BYTES * 7 // 8

_NN = (((1,), (0,)), ((), ()))
_NT = (((1,), (1,)), ((), ()))
_TN = (((0,), (0,)), ((), ()))


def _split(x, n):
    parts = []
    for _ in range(n - 1):
        hi = x.astype(BF16)
        parts.append(hi)
        x = x - hi.astype(F32)
    parts.append(x.astype(BF16))
    return parts


def _mm(a, b, dims=_NN, passes=1):
    d = lambda x, y: lax.dot_general(x, y, dims, preferred_element_type=F32)
    if passes == 1:
        return d(a.astype(BF16), b.astype(BF16))
    ah, al = _split(a, 2)
    bh, bl = _split(b, 2)
    return d(ah, bl) + d(al, bh) + d(ah, bh)


def _mm_exact_lhs(a01, b, n=3):
    a = a01.astype(BF16)
    out = None
    for t in reversed(_split(b, n)):
        y = lax.dot_general(a, t, _NN, preferred_element_type=F32)
        out = y if out is None else out + y
    return out


def _mm_exact_rhs(a, b01, n=3):
    b = b01.astype(BF16)
    out = None
    for t in reversed(_split(a, n)):
        y = lax.dot_general(t, b, _NN, preferred_element_type=F32)
        out = y if out is None else out + y
    return out


def _head_blockdiag(width):
    r = lax.broadcasted_iota(jnp.int32, (width, width), 0) // HEAD_DIM
    c = lax.broadcasted_iota(jnp.int32, (width, width), 1) // HEAD_DIM
    return (r == c).astype(F32)


def _sigmoid(x):
    return 1.0 / (1.0 + jnp.exp(-x))


def _softplus(x):
    return jnp.maximum(x, 0.0) + jnp.log(1.0 + jnp.exp(-jnp.abs(x)))


def _rwkv_kernel(p_ref, s0_ref, w0_ref, w2_ref, a0_ref, a2_ref, g2_ref, kk_ref, ka_ref, rk_ref, gng_ref, gnb_ref,
                 *rest, T, NB):
    out_ref, st_ref, lw_scr, kd_scr, bb_scr, y_scr, kk_scr, s_scr = rest[-8:]
    C = RW_CHUNK
    n_chunks = T // C
    bd = _head_blockdiag(RW_WIDTH)
    seg = lambda t: _mm_exact_rhs(t, bd, n=2)
    ka = ka_ref[...]
    for nb in range(NB):
        k = p_ref[nb, :, 256:512]
        kk = k * kk_ref[...]
        kk = kk * lax.rsqrt(seg(kk * kk) + 1e-12)
        kk_scr[nb] = kk
        for d in range(2):
            wd = p_ref[nb, :, 768 + 64 * d:832 + 64 * d]
            ad = p_ref[nb, :, 896 + 64 * d:960 + 64 * d]
            w_raw = w0_ref[d:d + 1, :] + _mm(jnp.tanh(wd), w2_ref[d])
            lw_scr[nb, d] = -jnp.exp(-_softplus(-w_raw) - 0.5)
            a = _sigmoid(a0_ref[d:d + 1, :] + _mm(ad, a2_ref[d]))
            kd_scr[nb, d] = k * (1.0 + (a - 1.0) * ka)
            bb_scr[nb, d] = kk * a
    s_scr[...] = s0_ref[:, 0]

    ti = lax.broadcasted_iota(jnp.int32, (C, C), 0)
    si = lax.broadcasted_iota(jnp.int32, (C, C), 1)
    ones_cc = jnp.ones((C, C), F32)
    t2 = lax.broadcasted_iota(jnp.int32, (C, 2 * C), 0)
    col2 = lax.broadcasted_iota(jnp.int32, (C, 2 * C), 1)
    right = col2 >= C
    s2 = jnp.where(right, col2 - C, col2)

    def chunk_body(i, carry):
        ch = []
        for nb, d in [(nb, d) for nb in range(NB) for d in range(2)]:
            ci = i if d == 0 else n_chunks - 1 - i
            rows = pl.ds(pl.multiple_of(ci * C, C), C)
            strict = (ti > si) if d == 0 else (ti < si)
            incl = (ti >= si) if d == 0 else (ti <= si)
            lw = lw_scr[nb, d, rows, :]
            cum = _mm_exact_lhs(incl.astype(F32), lw)
            total = _mm_exact_lhs(ones_cc, lw)
            cum_ex = cum - lw
            mid = 0.5 * total
            rr = p_ref[nb, rows, 0:256]
            vv = p_ref[nb, rows, 512:768]
            kdc = kd_scr[nb, d, rows, :]
            bbc = bb_scr[nb, d, rows, :]
            kkc = kk_scr[nb, rows, :]
            e_inv = jnp.exp(mid - cum)
            At = -kkc * jnp.exp(cum_ex - mid)
            Rt = rr * jnp.exp(cum - mid)
            Bt = bbc * e_inv
            Kt = kdc * e_inv
            Ap = -kkc * jnp.exp(cum_ex)
            Rp = rr * jnp.exp(cum)
            e_out = jnp.exp(total - cum)
            Bh = bbc * e_out
            Kh = kdc * e_out
            e_tot = jnp.exp(total[0:1, :])
            for h in range(RW_HEADS):
                hs = slice(h * HEAD_DIM, (h + 1) * HEAD_DIM)
                ch.append(dict(nb=nb, d=d, h=h, rows=rows, hs=hs, strict=strict, incl=incl,
                               AR=jnp.concatenate([At[:, hs], Rt[:, hs]], axis=0),
                               BK=jnp.concatenate([Bt[:, hs], Kt[:, hs]], axis=0), V=vv[:, hs], X1=Ap[:, hs],
                               Rp=Rp[:, hs], BKh=jnp.concatenate([Bh[:, hs], Kh[:, hs]], axis=0),
                               e_tot=e_tot[:, hs]))
        for c in ch:
            c['G'] = _mm(c['AR'], c['BK'], _NT)
        for c in ch:
            fwd = c['d'] == 0
            c['P'] = jnp.where(c['strict'], c['G'][:C, :C], 0.0)
            c['A_ak0'] = jnp.where(((t2 > s2) if fwd else (t2 < s2)) & right, c['G'][:C], 0.0)
            c['A_r'] = jnp.where((t2 >= s2) if fwd else (t2 <= s2), c['G'][C:], 0.0)
            c['VV'] = jnp.concatenate([c['V'], c['V']], axis=0)
        for c in ch:
            c['X2'] = _mm(c['A_ak0'], c['VV'])
        for lvl in range(6):
            for c in ch:
                if lvl < 5:
                    c['PZ'] = _mm(c['P'], jnp.concatenate([c['P'], c['X1'], c['X2']], axis=1))
                else:
                    c['PZ'] = _mm(c['P'], jnp.concatenate([c['X1'], c['X2']], axis=1))
            for c in ch:
                PZ = c['PZ']
                if lvl < 5:
                    c['P'] = PZ[:, :C]
                    c['X1'] = c['X1'] + PZ[:, C:2 * C]
                    c['X2'] = c['X2'] + PZ[:, 2 * C:]
                else:
                    c['X1'] = c['X1'] + PZ[:, :C]
                    c['X2'] = c['X2'] + PZ[:, C:]
        for c in ch:
            c['S0'] = s_scr[c['nb'], c['d'], c['h']]
            c['UY'] = _mm(jnp.concatenate([c['X1'], c['Rp']], axis=0), c['S0'], _NT)
        for c in ch:
            c['U'] = c['UY'][:C] + c['X2']
        for c in ch:
            UV = jnp.concatenate([c['U'], c['V']], axis=0)
            c['Y'] = c['UY'][C:] + _mm(c['A_r'], UV)
            c['S1'] = c['S0'] * c['e_tot'] + _mm(UV, c['BKh'], _TN)
        for c in ch:
            s_scr[c['nb'], c['d'], c['h']] = c['S1']
            y_scr[c['nb'], c['d'], c['rows'], c['hs']] = c['Y']
        return carry

    lax.fori_loop(0, n_chunks, chunk_body, 0)

    for nb in range(NB):
        r = p_ref[nb, :, 0:256]
        v = p_ref[nb, :, 512:768]
        bonus = seg(r * (kd_scr[nb, 0] + kd_scr[nb, 1]) * rk_ref[...]) * v
        g = _mm(_sigmoid(p_ref[nb, :, 1024:1152]), g2_ref[...])
        y = y_scr[nb, 0] + y_scr[nb, 1]
        mu = seg(y) * (1.0 / HEAD_DIM)
        yc = y - mu
        var = seg(yc * yc) * (1.0 / HEAD_DIM)
        yn = yc * lax.rsqrt(var + RW_GN_EPS)
        out_ref[nb] = (yn * gng_ref[...] + gnb_ref[...] + bonus) * g
    st_ref[:, 0] = s_scr[...]


RW_ROWS = 1024


def _state_spec(nb, layer):
    return pl.BlockSpec((nb, 1, 2, RW_HEADS, HEAD_DIM, HEAD_DIM), lambda b: (b, layer, 0, 0, 0, 0))


def rwkv7_mixer_pallas(p_rw, s0, l_in, prm, st_all=None, l_out=0):
    B, T, W = p_rw.shape
    NB = max(2, RW_ROWS // T)
    row = lambda a: a.reshape(1, RW_WIDTH)
    full = lambda shape: pl.BlockSpec(shape, lambda b: (0,) * len(shape))
    st_shape = (B, 1, 2, RW_HEADS, HEAD_DIM, HEAD_DIM) if st_all is None else st_all.shape
    return pl.pallas_call(
        partial(_rwkv_kernel, T=T, NB=NB),
        grid=(B // NB,),
        in_specs=[pl.BlockSpec((NB, T, W), lambda b: (b, 0, 0)), _state_spec(NB, l_in),
                  full((2, RW_WIDTH)), full((2, 64, RW_WIDTH)), full((2, RW_WIDTH)), full((2, 64, RW_WIDTH)),
                  full((128, RW_WIDTH)), full((1, RW_WIDTH)), full((1, RW_WIDTH)), full((1, RW_WIDTH)),
                  full((1, RW_WIDTH)), full((1, RW_WIDTH))]
        + ([] if st_all is None else [pl.BlockSpec(memory_space=pl.ANY)]),
        out_specs=[pl.BlockSpec((NB, T, RW_WIDTH), lambda b: (b, 0, 0)), _state_spec(NB, l_out)],
        out_shape=[jax.ShapeDtypeStruct((B, T, RW_WIDTH), F32), jax.ShapeDtypeStruct(st_shape, F32)],
        input_output_aliases={} if st_all is None else {12: 1},
        scratch_shapes=[pltpu.VMEM((NB, 2, T, RW_WIDTH), F32)] * 4
        + [pltpu.VMEM((NB, T, RW_WIDTH), F32), pltpu.VMEM((NB, 2, RW_HEADS, HEAD_DIM, HEAD_DIM), F32)],
        compiler_params=pltpu.CompilerParams(dimension_semantics=("arbitrary",), vmem_limit_bytes=VMEM_LIMIT_RWKV),
        name="rwkv7_mixer",
    )(p_rw, s0, prm['rw_w0'], prm['rw_w2'], prm['rw_a0'], prm['rw_a2'], prm['rw_g2'], row(prm['rw_kk']),
      row(prm['rw_ka']), row(prm['rw_rk']), row(prm['rw_gn_g']), row(prm['rw_gn_b']),
      *([] if st_all is None else [st_all]))


HG_SUB = 16
HG_ROWS = 256


def _hgrn_kernel(p_ref, s0_ref, lb_ref, ng_ref, *rest, T):
    out_ref, st_ref, lf_scr, kf_scr, o_scr, s_scr = rest[-6:]
    R, c = HG_ROWS, HG_SUB
    n_it = T // R
    x = p_ref[0]
    bd = _head_blockdiag(HG_WIDTH)
    seg = lambda t: _mm_exact_rhs(t, bd, n=2)
    for d in range(2):
        lbd = lb_ref[d:d + 1, :]
        f = lbd + (1.0 - lbd) * _sigmoid(x[:, 256 + 256 * d:512 + 256 * d])
        lf_scr[d] = jnp.log(jnp.maximum(f, HG_F_MIN))
        kf_scr[d] = 1.0 - f
        for h in range(HG_HEADS):
            s_scr[d, h] = s0_ref[0, 0, d, h].T

    ti = lax.broadcasted_iota(jnp.int32, (R, R), 0)
    si = lax.broadcasted_iota(jnp.int32, (R, R), 1)
    same_blk = (ti // c) == (si // c)
    t16 = lax.broadcasted_iota(jnp.int32, (c, 1), 0)

    def body(i, carry):
        for d in range(2):
            ci = i if d == 0 else n_it - 1 - i
            rows = pl.ds(pl.multiple_of(ci * R, R), R)
            incl = (ti >= si) if d == 0 else (ti <= si)
            lf = lf_scr[d, rows, :]
            cum = _mm_exact_lhs((incl & same_blk).astype(F32), lf)
            tot = _mm_exact_lhs(same_blk.astype(F32), lf)
            xq = p_ref[0, rows, 0:256]
            q = xq * _sigmoid(xq)
            v = p_ref[0, rows, 768:1024]
            kf = kf_scr[d, rows, :]
            Qp = q * jnp.exp(cum)
            Kh = kf * jnp.exp(tot - cum)
            e_tot = jnp.exp(tot)
            blocks = range(R // c) if d == 0 else range(R // c - 1, -1, -1)
            ST = [s_scr[d, h] for h in range(HG_HEADS)]
            o_parts = [None] * (R // c)
            for j in blocks:
                rs = slice(j * c, (j + 1) * c)
                cb, qb, kb, vb = cum[rs], q[rs], kf[rs], v[rs]
                half = c // 2
                spans = []
                for s in range(c):
                    if d == 0:
                        spans.append((half, c) if s >= half else (0, c))
                    else:
                        spans.append((0, half) if s < half else (0, c))
                prods = []
                for s, (lo, hi) in enumerate(spans):
                    e = jnp.exp(jnp.minimum(cb[lo:hi] - cb[s:s + 1, :], 0.0))
                    prods.append(qb[lo:hi] * (kb[s:s + 1, :] * e))
                att = _mm_exact_rhs(jnp.concatenate(prods, axis=0), bd, n=1)
                o_half = [jnp.zeros((half, HG_WIDTH), F32), jnp.zeros((half, HG_WIDTH), F32)]
                off = 0
                for s, (lo, hi) in enumerate(spans):
                    keep = (t16[lo:hi] >= s) if d == 0 else (t16[lo:hi] <= s)
                    term = jnp.where(keep, att[off:off + hi - lo], 0.0) * vb[s:s + 1, :]
                    off += hi - lo
                    for p in range(2):
                        a, b = max(lo, p * half), min(hi, (p + 1) * half)
                        if a < b:
                            o_half[p] = o_half[p] + term[a - lo:b - lo]
                o_blk = jnp.concatenate(o_half, axis=0)
                o_heads = []
                for h in range(HG_HEADS):
                    hs = slice(h * HEAD_DIM, (h + 1) * HEAD_DIM)
                    o_heads.append(_mm(Qp[rs, hs], ST[h], _NT))
                    ST[h] = ST[h] * e_tot[j * c:j * c + 1, hs] + _mm(vb[:, hs], Kh[rs, hs], _TN)
                o_parts[j] = o_blk + jnp.concatenate(o_heads, axis=1)
            for h in range(HG_HEADS):
                s_scr[d, h] = ST[h]
            o_scr[d, rows, :] = jnp.concatenate(o_parts, axis=0)
        return carry

    lax.fori_loop(0, n_it, body, 0)

    o = o_scr[0] + o_scr[1]
    o = o * lax.rsqrt(seg(o * o) * (1.0 / HEAD_DIM) + NORM_EPS) * ng_ref[...]
    gg = x[:, 1024:1280]
    out_ref[0] = o * (gg * _sigmoid(gg))
    for d in range(2):
        for h in range(HG_HEADS):
            st_ref[0, 0, d, h] = s_scr[d, h].T


def hgrn2_mixer_pallas(p_hg, s0, l_in, lb, norm_g, st_all=None, l_out=0):
    B, T, W = p_hg.shape
    full = lambda shape: pl.BlockSpec(shape, lambda b: (0,) * len(shape))
    st_shape = (B, 1, 2, HG_HEADS, HEAD_DIM, HEAD_DIM) if st_all is None else st_all.shape
    return pl.pallas_call(
        partial(_hgrn_kernel, T=T),
        grid=(B,),
        in_specs=[pl.BlockSpec((1, T, W), lambda b: (b, 0, 0)), _state_spec(1, l_in), full((2, HG_WIDTH)),
                  full((1, HG_WIDTH))] + ([] if st_all is None else [pl.BlockSpec(memory_space=pl.ANY)]),
        out_specs=[pl.BlockSpec((1, T, HG_WIDTH), lambda b: (b, 0, 0)), _state_spec(1, l_out)],
        out_shape=[jax.ShapeDtypeStruct((B, T, HG_WIDTH), F32), jax.ShapeDtypeStruct(st_shape, F32)],
        input_output_aliases={} if st_all is None else {4: 1},
        scratch_shapes=[pltpu.VMEM((2, T, HG_WIDTH), F32)] * 3
        + [pltpu.VMEM((2, HG_HEADS, HEAD_DIM, HEAD_DIM), F32)],
        compiler_params=pltpu.CompilerParams(dimension_semantics=("arbitrary",), vmem_limit_bytes=VMEM_LIMIT),
        name="hgrn2_mixer",
    )(p_hg, s0, lb, jnp.tile(norm_g.reshape(1, HEAD_DIM), (1, HG_HEADS)), *([] if st_all is None else [st_all]))


ATT_REP = ATT_HEADS // ATT_KV_HEADS
ATT_QROWS = 128


def _swap_pairs(x):
    w = x.shape[-1]
    lane = lax.broadcasted_iota(jnp.int32, x.shape, x.ndim - 1)
    return jnp.where(lane % 2 == 0, pltpu.roll(x, w - 1, x.ndim - 1), pltpu.roll(x, 1, x.ndim - 1))


def _att_kernel(*refs, T, past, rope):
    if rope:
        p_ref, qg_ref, kg_ref, cos_ref, sin_ref, ck_ref, cv_ref, out_ref, k_scr, v_scr, q_scr = refs
    else:
        p_ref, qg_ref, kg_ref, _, _, out_ref, kh_ref, vh_ref, k_scr, v_scr, q_scr = refs
    x = p_ref[0]
    q = x[:, 0:ATT_WIDTH]
    k = x[:, ATT_WIDTH:ATT_WIDTH + KV_WIDTH]
    v = x[:, ATT_WIDTH + KV_WIDTH:ATT_WIDTH + 2 * KV_WIDTH]
    inv_d = 1.0 / HEAD_DIM
    q = q * lax.rsqrt(_mm_exact_rhs(q * q, _head_blockdiag(ATT_WIDTH), n=2) * inv_d + NORM_EPS) * qg_ref[...]
    k = k * lax.rsqrt(_mm_exact_rhs(k * k, _head_blockdiag(KV_WIDTH), n=2) * inv_d + NORM_EPS) * kg_ref[...]
    if rope:
        cos, sin = cos_ref[...], sin_ref[...]
        rep = ATT_WIDTH // KV_WIDTH
        q = q * jnp.concatenate([cos] * rep, axis=1) + _swap_pairs(q) * jnp.concatenate([sin] * rep, axis=1)
        k = k * cos + _swap_pairs(k) * sin
    q_scr[...] = (q * (1.0 / math.sqrt(HEAD_DIM))).astype(BF16)
    for g in range(ATT_KV_HEADS):
        gs = slice(g * HEAD_DIM, (g + 1) * HEAD_DIM)
        if rope:
            k_scr[g, 0:past, :] = ck_ref[0, 0, g].astype(BF16)
            v_scr[g, 0:past, :] = cv_ref[0, 0, g].astype(BF16)
        else:
            kh_ref[0, 0, g] = k[:, gs].T
            vh_ref[0, 0, g] = v[:, gs].T
        k_scr[g, past:past + T, :] = k[:, gs].astype(BF16)
        v_scr[g, past:past + T, :] = v[:, gs].astype(BF16)
    QR = ATT_QROWS

    def q_block(qb, carry):
        rows = pl.ds(pl.multiple_of(qb * QR, QR), QR)
        qblk = q_scr[rows, :]
        for g in range(ATT_KV_HEADS):
            qs = jnp.concatenate([qblk[:, (g * ATT_REP + r) * HEAD_DIM:(g * ATT_REP + r + 1) * HEAD_DIM]
                                  for r in range(ATT_REP)], axis=0)
            s = lax.dot_general(qs, k_scr[g], _NT, preferred_element_type=F32)
            e = jnp.exp(s - jnp.max(s, axis=-1, keepdims=True))
            l = jnp.sum(e, axis=-1, keepdims=True)
            o = lax.dot_general(e.astype(BF16), v_scr[g], _NN, preferred_element_type=F32) / l
            for r in range(ATT_REP):
                h = g * ATT_REP + r
                out_ref[0, rows, h * HEAD_DIM:(h + 1) * HEAD_DIM] = o[r * QR:(r + 1) * QR]
        return carry

    lax.fori_loop(0, T // QR, q_block, 0, unroll=4 if T // QR >= 4 else 2)


def rope_tables(T):
    rows = T // GRID_W
    row = jnp.repeat(jnp.arange(rows, dtype=F32), GRID_W)
    col = jnp.tile(jnp.arange(GRID_W, dtype=F32), rows)
    n_freq = HEAD_DIM // 4
    inv = ROPE_THETA ** (-jnp.arange(n_freq, dtype=F32) / n_freq)
    ang = jnp.concatenate([row[:, None] * inv, col[:, None] * inv], axis=-1)
    cos = jnp.repeat(jnp.cos(ang), 2, axis=-1)
    sin = jnp.stack([-jnp.sin(ang), jnp.sin(ang)], axis=-1).reshape(T, HEAD_DIM)
    return jnp.tile(cos, (1, ATT_KV_HEADS)), jnp.tile(sin, (1, ATT_KV_HEADS))


def attention_pallas(p_att, qnorm_g, knorm_g, l, cache=None, kv_all=None):
    B, T, W = p_att.shape
    rope = cache is not None
    past = cache[0].shape[3] if rope else 0
    full = lambda shape: pl.BlockSpec(shape, lambda b: (0,) * len(shape))
    qg = jnp.tile(qnorm_g.reshape(1, HEAD_DIM), (1, ATT_HEADS))
    kg = jnp.tile(knorm_g.reshape(1, HEAD_DIM), (1, ATT_KV_HEADS))
    in_specs = [pl.BlockSpec((1, T, W), lambda b: (b, 0, 0)), full((1, ATT_WIDTH)), full((1, KV_WIDTH))]
    args = [p_att, qg, kg]
    out_specs = [pl.BlockSpec((1, T, ATT_WIDTH), lambda b: (b, 0, 0))]
    out_shape = [jax.ShapeDtypeStruct((B, T, ATT_WIDTH), F32)]
    if rope:
        cos, sin = rope_tables(T)
        kv_spec = pl.BlockSpec((1, 1, ATT_KV_HEADS, past, HEAD_DIM), lambda b: (b, l, 0, 0, 0))
        in_specs += [full((T, KV_WIDTH)), full((T, KV_WIDTH)), kv_spec, kv_spec]
        args += [cos, sin, cache[0], cache[1]]
        aliases = {}
    else:
        kv_spec = pl.BlockSpec((1, 1, ATT_KV_HEADS, HEAD_DIM, T), lambda b: (b, l, 0, 0, 0))
        in_specs += [pl.BlockSpec(memory_space=pl.ANY)] * 2
        args += list(kv_all)
        aliases = {3: 1, 4: 2}
        out_specs += [kv_spec, kv_spec]
        out_shape += [jax.ShapeDtypeStruct(kv_all[0].shape, F32)] * 2
    res = pl.pallas_call(
        partial(_att_kernel, T=T, past=past, rope=rope),
        grid=(B,),
        in_specs=in_specs,
        out_specs=out_specs,
        out_shape=out_shape,
        input_output_aliases=aliases,
        scratch_shapes=[pltpu.VMEM((ATT_KV_HEADS, past + T, HEAD_DIM), BF16)] * 2
        + [pltpu.VMEM((T, ATT_WIDTH), BF16)],
        compiler_params=pltpu.CompilerParams(dimension_semantics=("arbitrary",), vmem_limit_bytes=VMEM_LIMIT),
        name="attention_rope" if rope else "attention_ctx",
    )(*args)
    return res[0] if rope else tuple(res)


ROW_TILE = 512
ROW_SUB = 256
MOD_TILE = 1536
ROUTE_TILE = 256
MOE_SLOTS = 256
SC_ROWS = 64
SC_LANES = 128
P_ATT, P_RW, P_HG = ATT_WIDTH + 2 * KV_WIDTH, 3 * RW_WIDTH + 384, 5 * HG_WIDTH


def _mod_kernel(c_ref, w_ref, b_ref, o_ref):
    c = c_ref[...]
    o_ref[0] = _mm(c * _sigmoid(c), w_ref[0], passes=3) + b_ref[0]


def adaln_mod_pallas(cvec, w_mod, b_mod):
    n = 6 * D_MODEL
    return pl.pallas_call(
        _mod_kernel,
        grid=(DEPTH, n // MOD_TILE),
        in_specs=[pl.BlockSpec((8, D_MODEL), lambda l, j: (0, 0)),
                  pl.BlockSpec((1, D_MODEL, MOD_TILE), lambda l, j: (l, 0, j)),
                  pl.BlockSpec((1, 1, MOD_TILE), lambda l, j: (l, 0, j))],
        out_specs=pl.BlockSpec((1, 8, MOD_TILE), lambda l, j: (l, 0, j)),
        out_shape=jax.ShapeDtypeStruct((DEPTH, 8, n), F32),
        compiler_params=pltpu.CompilerParams(dimension_semantics=("arbitrary", "arbitrary"),
                                             vmem_limit_bytes=VMEM_LIMIT),
        name="adaln_mod",
    )(cvec, w_mod, b_mod.reshape(DEPTH, 1, n))


def _rms(x):
    return x * lax.rsqrt(jnp.mean(x * x, axis=-1, keepdims=True) + NORM_EPS)


HALF = D_MODEL // 2
U32 = jnp.uint32


def _pack_rows(x):
    bits = lax.bitcast_convert_type(x.astype(BF16).astype(F32), U32)
    return (bits[:, :HALF] >> 16) | bits[:, HALF:]


def _unpack_rows(w):
    lo = lax.bitcast_convert_type(w << 16, F32)
    hi = lax.bitcast_convert_type(w & jnp.uint32(0xFFFF0000), F32)
    return lo, hi


def _row_pieces():
    return [slice(r, r + ROW_SUB) for r in range(0, ROW_TILE, ROW_SUB)]


def _moe_residual(x_ref, g_ref, g4_ref, pm_ref, rows):
    y = None
    for k in range(TOP_K):
        t = g4_ref[rows, k:k + 1] * jnp.concatenate(_unpack_rows(g_ref[k, rows, :]), axis=1)
        y = t if y is None else y + t
    return x_ref[rows, :] + pm_ref[0, 5:6, :] * y


def _in_kernel(*refs, has_res):
    if has_res:
        x_ref, gth_ref, g4_ref, pm_ref, m_ref, g_ref, w_ref, xo_ref, pa_ref, pr_ref, ph_ref = refs
    else:
        x_ref, m_ref, g_ref, w_ref, pa_ref, pr_ref, ph_ref = refs
    for rows in _row_pieces():
        if has_res:
            x = _moe_residual(x_ref, gth_ref, g4_ref, pm_ref, rows)
            xo_ref[rows, :] = x
        else:
            x = x_ref[rows, :]
        h = _rms(x) * g_ref[...] * (1.0 + m_ref[0, 1:2, :]) + m_ref[0, 0:1, :]
        proj = lax.dot_general(h, w_ref[0], _NN, precision=lax.Precision.DEFAULT, preferred_element_type=F32)
        pa_ref[rows, :] = proj[:, 0:P_ATT]
        pr_ref[rows, :] = proj[:, P_ATT:P_ATT + P_RW]
        ph_ref[rows, :] = proj[:, P_ATT + P_RW:]


def _res_specs(row0):
    t0 = row0 // ROW_TILE
    return [pl.BlockSpec((TOP_K, ROW_TILE, HALF), lambda i: (0, i, 0)),
            pl.BlockSpec((ROW_TILE, TOP_K), lambda i: (i + t0, 0))]


def in_proj_pallas(x, mod, norm_g, w_in, l, rows_per_mod, res=None):
    R = x.shape[0]
    tpm = rows_per_mod // ROW_TILE
    rt = lambda w: pl.BlockSpec((ROW_TILE, w), lambda i: (i, 0))
    ms = pl.BlockSpec((1, 6, D_MODEL), lambda i: (i // tpm, 0, 0))
    full = lambda shape, **kw: pl.BlockSpec(shape, lambda i: (0,) * len(shape), **kw)
    in_specs = [rt(D_MODEL)] + (_res_specs(res[2]) + [ms] if res else []) + [
        ms, full((1, D_MODEL)),
        pl.BlockSpec((1, D_MODEL, w_in.shape[2]), lambda i: (l, 0, 0), pipeline_mode=pl.Buffered(1))]
    args = [x] + ([res[0], res[1], res[3]] if res else []) + [mod, norm_g.reshape(1, D_MODEL), w_in]
    widths = ([D_MODEL] if res else []) + [P_ATT, P_RW, P_HG]
    return pl.pallas_call(
        partial(_in_kernel, has_res=res is not None),
        grid=(R // ROW_TILE,),
        in_specs=in_specs,
        out_specs=[rt(w) for w in widths],
        out_shape=[jax.ShapeDtypeStruct((R, w), F32) for w in widths],
        compiler_params=pltpu.CompilerParams(dimension_semantics=("arbitrary",), vmem_limit_bytes=VMEM_LIMIT),
        name="in_proj",
    )(*args)


def _out_kernel(att_ref, rw_ref, hg_ref, x_ref, m_ref, g_ref, w_ref, rw_w_ref, rb_ref, *rest):
    xo_ref, h_ref, lg_ref = rest[-3:]
    d = lambda a, lo, hi: lax.dot_general(a, w_ref[0, lo:hi, :], _NN, precision=lax.Precision.DEFAULT,
                                          preferred_element_type=F32)
    for rows in _row_pieces():
        mixo = (d(att_ref[rows, :], 0, ATT_WIDTH) + d(rw_ref[rows, :], ATT_WIDTH, ATT_WIDTH + RW_WIDTH)
                + d(hg_ref[rows, :], ATT_WIDTH + RW_WIDTH, ATT_WIDTH + RW_WIDTH + HG_WIDTH))
        x = x_ref[rows, :] + m_ref[0, 2:3, :] * mixo
        xo_ref[rows, :] = x
        h = _rms(x) * g_ref[...] * (1.0 + m_ref[0, 4:5, :]) + m_ref[0, 3:4, :]
        h_ref[rows, :] = _pack_rows(h)
        lg_ref[rows, :] = _mm(h, rw_w_ref[...], passes=3) + rb_ref[...]


def out_proj_pallas(att, rw, hg, x, mod, norm_g, w_out, l, router_w, router_b, rows_per_mod, n_all, row0, joint=None):
    R = x.shape[0]
    tpm = rows_per_mod // ROW_TILE
    t0 = row0 // ROW_TILE
    rt = lambda w: pl.BlockSpec((ROW_TILE, w), lambda i: (i, 0))
    jt = lambda w: pl.BlockSpec((ROW_TILE, w), lambda i: (i + t0, 0))
    full = lambda shape: pl.BlockSpec(shape, lambda i: (0,) * len(shape))
    in_specs = [rt(ATT_WIDTH), rt(RW_WIDTH), rt(HG_WIDTH), rt(D_MODEL),
                pl.BlockSpec((1, 6, D_MODEL), lambda i: (i // tpm, 0, 0)), full((1, D_MODEL)),
                pl.BlockSpec((1, D_MODEL, D_MODEL), lambda i: (l, 0, 0)), full((D_MODEL, N_EXPERTS)),
                full((1, N_EXPERTS))]
    args = [att, rw, hg, x, mod, norm_g.reshape(1, D_MODEL), w_out, router_w, router_b.reshape(1, N_EXPERTS)]
    aliases = {}
    if joint is not None:
        in_specs += [pl.BlockSpec(memory_space=pl.ANY)] * 2
        aliases = {len(args): 1, len(args) + 1: 2}
        args += list(joint)
    return pl.pallas_call(
        _out_kernel,
        grid=(R // ROW_TILE,),
        in_specs=in_specs,
        out_specs=[rt(D_MODEL), jt(HALF), jt(N_EXPERTS)],
        out_shape=[jax.ShapeDtypeStruct((R, D_MODEL), F32), jax.ShapeDtypeStruct((n_all, HALF), U32),
                   jax.ShapeDtypeStruct((n_all, N_EXPERTS), F32)],
        input_output_aliases=aliases,
        compiler_params=pltpu.CompilerParams(dimension_semantics=("arbitrary",), vmem_limit_bytes=VMEM_LIMIT),
        name="out_proj",
    )(*args)


def _final_kernel(x_ref, gth_ref, g4_ref, m_ref, g_ref, o_ref):
    for rows in _row_pieces():
        o_ref[rows, :] = _rms(_moe_residual(x_ref, gth_ref, g4_ref, m_ref, rows)) * g_ref[...]


def final_norm_pallas(x, gathered, gate4, row0, mod, norm_g, rows_per_mod):
    R = x.shape[0]
    tpm = rows_per_mod // ROW_TILE
    rt = pl.BlockSpec((ROW_TILE, D_MODEL), lambda i: (i, 0))
    return pl.pallas_call(
        _final_kernel,
        grid=(R // ROW_TILE,),
        in_specs=[rt] + _res_specs(row0) + [pl.BlockSpec((1, 6, D_MODEL), lambda i: (i // tpm, 0, 0)),
                                            pl.BlockSpec((1, D_MODEL), lambda i: (0, 0))],
        out_specs=rt,
        out_shape=jax.ShapeDtypeStruct((R, D_MODEL), F32),
        name="final_norm",
    )(x, gathered, gate4, mod, norm_g.reshape(1, D_MODEL))


def _moe_max_blocks(n_tok):
    return (n_tok * TOP_K + N_EXPERTS * (MOE_SLOTS - 1)) // MOE_SLOTS


def _route_kernel(lg_ref, dest_ref, gate4_ref, blk_ref, rank_scr, gate_scr, *, n_tok):
    Rt, E = ROUTE_TILE, N_EXPERTS
    n_tiles = n_tok // Rt
    ti = lax.broadcasted_iota(jnp.int32, (Rt, Rt), 0)
    si = lax.broadcasted_iota(jnp.int32, (Rt, Rt), 1)
    before_t = (ti < si).astype(BF16)
    eye_t = (ti == si).astype(BF16)
    ei = lax.broadcasted_iota(jnp.int32, (E, E), 0)
    ej = lax.broadcasted_iota(jnp.int32, (E, E), 1)
    before_e = (ei > ej).astype(BF16)
    sub = lax.broadcasted_iota(jnp.int32, (E, Rt), 0)
    d = lambda a, b, dims: lax.dot_general(a, b, dims, preferred_element_type=F32)

    def tile_members(it, off):
        rows = pl.ds(pl.multiple_of(it * Rt, Rt), Rt)
        l3, l2, l1 = reversed(_split(lg_ref[rows, :], 3))
        lgT = d(l3, eye_t, _TN) + d(l2, eye_t, _TN) + d(l1, eye_t, _TN)
        work = lgT
        member = jnp.zeros((E, Rt), jnp.bool_)
        top = None
        for k in range(TOP_K):
            m = jnp.max(work, axis=0, keepdims=True)
            if top is None:
                top = m
            first = jnp.min(jnp.where(work == m, sub, E), axis=0, keepdims=True)
            pick = sub == first
            member = member | pick
            work = jnp.where(pick, -jnp.inf, work)
        ex = jnp.where(member, jnp.exp(lgT - top), 0.0)
        gate_scr[:, rows] = ex / jnp.sum(ex, axis=0, keepdims=True)
        mem = member.astype(BF16)
        rank = d(mem, before_t, _NN) + off
        rank_scr[:, rows] = jnp.where(member, rank, -1.0)
        return off + jnp.sum(mem.astype(F32), axis=1, keepdims=True)

    count = lax.fori_loop(0, n_tiles, tile_members, jnp.zeros((E, 1), F32))
    nblk = jnp.floor((count + (MOE_SLOTS - 1)) * (1.0 / MOE_SLOTS))
    bstart = d(before_e, jnp.broadcast_to(nblk, (E, 128)).astype(BF16), _NN)[:, 0:1]
    bend = bstart + nblk
    pstart = bstart * MOE_SLOTS

    def tile_slots(it, carry):
        rows = pl.ds(pl.multiple_of(it * Rt, Rt), Rt)
        rank = rank_scr[:, rows]
        gate = gate_scr[:, rows]
        member = rank >= 0.0
        kidx = d(before_e, member.astype(BF16), _NN)
        slot = pstart + rank
        grows = []
        for k in range(TOP_K):
            sel = member & (kidx == k)
            dest_ref[k:k + 1, rows] = jnp.sum(jnp.where(sel, slot, 0.0), axis=0, keepdims=True).astype(jnp.int32)
            grows.append(jnp.sum(jnp.where(sel, gate, 0.0), axis=0, keepdims=True))
        g4t = jnp.concatenate(grows + [jnp.zeros((128 - TOP_K, Rt), F32)], axis=0)
        g3, g2, g1 = reversed(_split(g4t, 3))
        g4 = d(eye_t, g3, _NT) + d(eye_t, g2, _NT) + d(eye_t, g1, _NT)
        gate4_ref[rows, :] = g4[:, 0:TOP_K]
        return carry

    lax.fori_loop(0, n_tiles, tile_slots, 0)
    nb = blk_ref.shape[1]
    bi = lax.broadcasted_iota(jnp.int32, (E, nb), 1).astype(F32)
    owner = jnp.sum((bend <= bi).astype(F32), axis=0, keepdims=True)
    blk_ref[0:1, :] = jnp.minimum(owner, E - 1.0).astype(jnp.int32)
    blk_ref[1:2, :] = jnp.broadcast_to(jnp.sum(nblk, axis=0, keepdims=True), (1, nb)).astype(jnp.int32)
    blk_ref[2:8, :] = jnp.zeros((6, nb), jnp.int32)


def moe_route_pallas(logits):
    n_tok = logits.shape[0]
    nb = -(-_moe_max_blocks(n_tok) // 128) * 128
    return pl.pallas_call(
        partial(_route_kernel, n_tok=n_tok),
        out_shape=[jax.ShapeDtypeStruct((TOP_K, n_tok), jnp.int32),
                   jax.ShapeDtypeStruct((n_tok, TOP_K), F32),
                   jax.ShapeDtypeStruct((8, nb), jnp.int32)],
        scratch_shapes=[pltpu.VMEM((N_EXPERTS, n_tok), F32)] * 2,
        name="moe_route",
    )(logits)


def _moe_block_kernel(be_ref, nu_ref, first_ref, par_ref, nxt_ref, xb_ref, wgu_hbm, bgu_ref, wdn_hbm, bdn_ref, yb_ref,
                      wgu_buf, wdn_buf, sem, *, l):
    i = pl.program_id(0)

    def weight_copies(e, slot):
        return (pltpu.make_async_copy(wgu_hbm.at[l, e], wgu_buf.at[slot], sem.at[0, slot]),
                pltpu.make_async_copy(wdn_hbm.at[l, e], wdn_buf.at[slot], sem.at[1, slot]))

    @pl.when(i < nu_ref[0])
    def _():
        slot = par_ref[i]

        @pl.when(first_ref[i] == 1)
        def _():
            @pl.when(i == 0)
            def _():
                for cp in weight_copies(be_ref[0], slot):
                    cp.start()

            for cp in weight_copies(be_ref[i], slot):
                cp.wait()

            @pl.when(nxt_ref[i] >= 0)
            def _():
                for cp in weight_copies(nxt_ref[i], 1 - slot):
                    cp.start()

        dot = lambda a, w: lax.dot_general(a, w, _NN, precision=lax.Precision.DEFAULT, preferred_element_type=F32)
        x_lo, x_hi = _unpack_rows(xb_ref[...])
        gu = dot(x_lo, wgu_buf[slot, 0:HALF]) + dot(x_hi, wgu_buf[slot, HALF:D_MODEL]) + bgu_ref[0, 0]
        glu = jnp.minimum(gu[:, :EXPERT_FF], SWIGLU_LIMIT)
        lin = jnp.clip(gu[:, EXPERT_FF:], -SWIGLU_LIMIT, SWIGLU_LIMIT)
        act = glu * _sigmoid(SWIGLU_ALPHA * glu) * (lin + 1.0)
        yb_ref[...] = _pack_rows(dot(act, wdn_buf[slot]) + bdn_ref[0, 0])


def _expert_runs(block_e, n_used):
    n = block_e.shape[0]
    idx = jnp.arange(n, dtype=jnp.int32)
    valid = idx < n_used[0]
    first = valid & ((idx == 0) | (block_e != jnp.roll(block_e, 1)))
    par = (jnp.cumsum(first.astype(jnp.int32)) - 1) % 2
    start = jnp.where(first, idx, n)
    nxt_start = lax.cummin(jnp.concatenate([start[1:], jnp.full((1,), n, jnp.int32)]), reverse=True)
    nxt = jnp.where(nxt_start < n, block_e[jnp.minimum(nxt_start, n - 1)], -1)
    return first.astype(jnp.int32), par.astype(jnp.int32), nxt.astype(jnp.int32)


def moe_blocks_pallas(xb, block_e, n_used, l, w_gu, b_gu, w_down, b_down):
    n_blocks = xb.shape[0] // MOE_SLOTS
    first, par, nxt = _expert_runs(block_e, n_used)
    blk = lambda i, be, nu, *_: (jnp.minimum(i, nu[0] - 1), 0)
    bsel = lambda i, be, nu, *_: (l, be[jnp.minimum(i, nu[0] - 1)], 0, 0)
    grid_spec = pltpu.PrefetchScalarGridSpec(
        num_scalar_prefetch=5,
        grid=(n_blocks,),
        in_specs=[pl.BlockSpec((MOE_SLOTS, HALF), blk),
                  pl.BlockSpec(memory_space=pl.ANY),
                  pl.BlockSpec((1, 1, 1, 2 * EXPERT_FF), bsel),
                  pl.BlockSpec(memory_space=pl.ANY),
                  pl.BlockSpec((1, 1, 1, D_MODEL), bsel)],
        out_specs=pl.BlockSpec((MOE_SLOTS, HALF), blk),
        scratch_shapes=[pltpu.VMEM((2, D_MODEL, 2 * EXPERT_FF), F32), pltpu.VMEM((2, EXPERT_FF, D_MODEL), F32),
                        pltpu.SemaphoreType.DMA((2, 2))],
    )
    return pl.pallas_call(
        partial(_moe_block_kernel, l=l),
        grid_spec=grid_spec,
        out_shape=jax.ShapeDtypeStruct(xb.shape, U32),
        compiler_params=pltpu.CompilerParams(dimension_semantics=("arbitrary",), vmem_limit_bytes=VMEM_LIMIT),
        name="moe_blocks",
    )(block_e, n_used, first, par, nxt, xb, w_gu, b_gu.reshape(DEPTH, N_EXPERTS, 1, 2 * EXPERT_FF), w_down,
      b_down.reshape(DEPTH, N_EXPERTS, 1, D_MODEL))


def _sc_mesh():
    return plsc.VectorSubcoreMesh(core_axis_name="c", subcore_axis_name="s")


def _sc_index_rows(idx):
    return jnp.pad(idx.reshape(-1, SC_ROWS), ((0, 0), (0, SC_LANES - SC_ROWS)))


def sc_dispatch(h, dest, n_rows):
    n_tok, d = h.shape
    idx = [_sc_index_rows(dest[k]) for k in range(TOP_K)]

    @pl.kernel(out_type=jax.ShapeDtypeStruct((n_rows, d), h.dtype), mesh=_sc_mesh(), scratch_types=[])
    def kern(h_hbm, i0, i1, i2, i3, o_hbm):
        def body(x_vmem, *i_vmem):
            for iv in i_vmem:
                pltpu.sync_copy(x_vmem, o_hbm.at[iv.at[0, pl.ds(0, SC_ROWS)]])

        pltpu.emit_pipeline(
            body,
            grid=(n_tok // SC_ROWS,),
            in_specs=[pl.BlockSpec((SC_ROWS, d), lambda i: (i, 0))]
            + [pl.BlockSpec((1, SC_LANES), lambda i: (i, 0))] * TOP_K,
            out_specs=[],
            core_axis_name=("c", "s"),
            dimension_semantics=(pltpu.PARALLEL,),
        )(h_hbm, i0, i1, i2, i3)

    return kern(h, *idx)


def sc_combine_gather(yb, dest):
    n_tok = dest.shape[1]
    d = yb.shape[1]
    idx = _sc_index_rows(dest.reshape(TOP_K * n_tok))

    @pl.kernel(out_type=jax.ShapeDtypeStruct((TOP_K * n_tok, d), yb.dtype), mesh=_sc_mesh(), scratch_types=[])
    def kern(y_hbm, i_hbm, o_hbm):
        def body(i_vmem, o_vmem):
            pltpu.sync_copy(y_hbm.at[i_vmem.at[0, pl.ds(0, SC_ROWS)]], o_vmem)

        pltpu.emit_pipeline(
            body,
            grid=(TOP_K * n_tok // SC_ROWS,),
            in_specs=[pl.BlockSpec((1, SC_LANES), lambda i: (i, 0))],
            out_specs=[pl.BlockSpec((SC_ROWS, d), lambda i: (i, 0))],
            core_axis_name=("c", "s"),
            dimension_semantics=(pltpu.PARALLEL,),
        )(i_hbm, o_hbm)

    return kern(yb, idx).reshape(TOP_K, n_tok, d)


def hgrn_lower_bounds(hg_lb):
    sm = jax.nn.softmax(hg_lb.astype(jnp.float32), axis=0)
    return jnp.cumsum(sm, axis=0) - sm[0:1]


def kernel(x_prompt, x_sample, cache_att_k, cache_att_v, state_rwkv, state_hgrn, c, c_ctx, w_mod, b_mod, norm_mix_g, norm_ffn_g, w_in, w_out, att_qnorm_g, att_knorm_g, rw_w0, rw_w2, rw_a0, rw_a2, rw_g2, rw_kk, rw_ka, rw_rk, rw_gn_g, rw_gn_b, hg_lb, hg_norm_g, router_w, router_b, moe_w_gu, moe_b_gu, moe_w_down, moe_b_down, final_norm_g):
    BP, TP, _ = x_prompt.shape
    BS, TS, _ = x_sample.shape
    n_p, n_s = BP * TP, BS * TS
    lb_all = hgrn_lower_bounds(hg_lb)
    cvec = jnp.concatenate([c_ctx[None, :], c, jnp.zeros((8 - 1 - BS, D_MODEL), F32)], axis=0)
    mod_all = adaln_mod_pallas(cvec, w_mod, b_mod).reshape(DEPTH, 8, 6, D_MODEL)
    zeros_state = jnp.zeros((BP, 1, 2, RW_HEADS, HEAD_DIM, HEAD_DIM), F32)
    kv_all = (jnp.zeros((BP, DEPTH, ATT_KV_HEADS, HEAD_DIM, TP), F32),) * 2
    rw_states = jnp.zeros((BP, DEPTH, 2, RW_HEADS, HEAD_DIM, HEAD_DIM), F32)
    hg_states = jnp.zeros((BP, DEPTH, 2, HG_HEADS, HEAD_DIM, HEAD_DIM), F32)
    x = {'p': x_prompt.reshape(n_p, D_MODEL), 's': x_sample.reshape(n_s, D_MODEL)}
    dims = {'p': (TP, BP, n_p, 0), 's': (TS, BS, TS, n_p)}
    moe_out, mod_prev = None, None
    for l in range(DEPTH):
        prm = dict(rw_w0=rw_w0[l], rw_w2=rw_w2[l], rw_a0=rw_a0[l], rw_a2=rw_a2[l], rw_g2=rw_g2[l],
                   rw_kk=rw_kk[l], rw_ka=rw_ka[l], rw_rk=rw_rk[l], rw_gn_g=rw_gn_g[l], rw_gn_b=rw_gn_b[l])
        mods = {'p': mod_all[l, 0:1], 's': mod_all[l, 1:1 + BS]}
        joint = (jnp.zeros((n_p + n_s, HALF), U32), jnp.zeros((n_p + n_s, N_EXPERTS), F32))
        for s in ('p', 's'):
            T, B, rpm, row0 = dims[s]
            if l == 0:
                p_att, p_rw, p_hg = in_proj_pallas(x[s], mods[s], norm_mix_g[l], w_in, l, rpm)
            else:
                x[s], p_att, p_rw, p_hg = in_proj_pallas(x[s], mods[s], norm_mix_g[l], w_in, l, rpm,
                                                         res=(*moe_out[s], row0, mod_prev[s]))
            p_att, p_rw, p_hg = (t.reshape(B, T, -1) for t in (p_att, p_rw, p_hg))
            if s == 'p':
                att, *kv_all = attention_pallas(p_att, att_qnorm_g[l], att_knorm_g[l], l, kv_all=kv_all)
                rw_out, rw_states = rwkv7_mixer_pallas(p_rw, zeros_state, 0, prm, rw_states, l)
                hg_out, hg_states = hgrn2_mixer_pallas(p_hg, zeros_state, 0, lb_all[l], hg_norm_g[l], hg_states, l)
            else:
                att = attention_pallas(p_att, att_qnorm_g[l], att_knorm_g[l], l, cache=(cache_att_k, cache_att_v))
                rw_out, _ = rwkv7_mixer_pallas(p_rw, state_rwkv, l, prm)
                hg_out, _ = hgrn2_mixer_pallas(p_hg, state_hgrn, l, lb_all[l], hg_norm_g[l])
            x[s], *joint = out_proj_pallas(att.reshape(B * T, -1), rw_out.reshape(B * T, -1),
                                           hg_out.reshape(B * T, -1), x[s], mods[s], norm_ffn_g[l], w_out, l,
                                           router_w[l], router_b[l], rpm, n_p + n_s, row0, joint)
        h_all, logits_all = joint
        dest, gate4, blk = moe_route_pallas(logits_all)
        xb = sc_dispatch(h_all, dest, _moe_max_blocks(n_p + n_s) * MOE_SLOTS)
        yb = moe_blocks_pallas(xb, blk[0], blk[1, :1], l, moe_w_gu, moe_b_gu, moe_w_down, moe_b_down)
        moe_out = {'p': (sc_combine_gather(yb, dest[:, :n_p]), gate4), 's': (sc_combine_gather(yb, dest[:, n_p:]), gate4)}
        mod_prev = mods
    y_prompt = final_norm_pallas(x['p'], *moe_out['p'], 0, mod_prev['p'], final_norm_g, n_p)
    y_sample = final_norm_pallas(x['s'], *moe_out['s'], n_p, mod_prev['s'], final_norm_g, TS)
    return (y_prompt.reshape(x_prompt.shape), y_sample.reshape(x_sample.shape),
            jnp.swapaxes(kv_all[0], 3, 4), jnp.swapaxes(kv_all[1], 3, 4),
            rw_states, hg_states)
```

```python
import math
from functools import partial

import jax
import jax.numpy as jnp
from jax import lax
from jax.experimental import pallas as pl
from jax.experimental.pallas import tpu as pltpu
from jax.experimental.pallas import tpu_sc as plsc

D_MODEL = 1024
DEPTH = 2
GRID_W = 64
HEAD_DIM = 64
ATT_HEADS = 8
ATT_KV_HEADS = 2
ATT_WIDTH = ATT_HEADS * HEAD_DIM
KV_WIDTH = ATT_KV_HEADS * HEAD_DIM
RW_HEADS = 4
RW_WIDTH = RW_HEADS * HEAD_DIM
RW_GN_EPS = 64e-5
HG_HEADS = 4
HG_WIDTH = HG_HEADS * HEAD_DIM
HG_F_MIN = 1e-6
N_EXPERTS = 32
TOP_K = 4
EXPERT_FF = D_MODEL
SWIGLU_LIMIT = 7.0
SWIGLU_ALPHA = 1.702
ROPE_THETA = 10000.0
NORM_EPS = 1e-6

RW_CHUNK = 64
BF16 = jnp.bfloat16
F32 = jnp.float32

V7X_VMEM_BYTES = 64 * 1024 * 1024
VMEM_LIMIT = V7X_VMEM_BYTES * 3 // 4
VMEM_LIMIT_RWKV = V7X_VMEM_BYTES * 7 // 8

_NN = (((1,), (0,)), ((), ()))
_NT = (((1,), (1,)), ((), ()))
_TN = (((0,), (0,)), ((), ()))


def _split(x, n):
    parts = []
    for _ in range(n - 1):
        hi = x.astype(BF16)
        parts.append(hi)
        x = x - hi.astype(F32)
    parts.append(x.astype(BF16))
    return parts


def _mm(a, b, dims=_NN, passes=1):
    d = lambda x, y: lax.dot_general(x, y, dims, preferred_element_type=F32)
    if passes == 1:
        return d(a.astype(BF16), b.astype(BF16))
    ah, al = _split(a, 2)
    bh, bl = _split(b, 2)
    return d(ah, bl) + d(al, bh) + d(ah, bh)


def _mm_exact_lhs(a01, b, n=3):
    a = a01.astype(BF16)
    out = None
    for t in reversed(_split(b, n)):
        y = lax.dot_general(a, t, _NN, preferred_element_type=F32)
        out = y if out is None else out + y
    return out


def _mm_exact_rhs(a, b01, n=3):
    b = b01.astype(BF16)
    out = None
    for t in reversed(_split(a, n)):
        y = lax.dot_general(t, b, _NN, preferred_element_type=F32)
        out = y if out is None else out + y
    return out


def _head_blockdiag(width):
    r = lax.broadcasted_iota(jnp.int32, (width, width), 0) // HEAD_DIM
    c = lax.broadcasted_iota(jnp.int32, (width, width), 1) // HEAD_DIM
    return (r == c).astype(F32)


def _sigmoid(x):
    return 1.0 / (1.0 + jnp.exp(-x))


def _softplus(x):
    return jnp.maximum(x, 0.0) + jnp.log(1.0 + jnp.exp(-jnp.abs(x)))


def _rwkv_kernel(p_ref, s0_ref, w0_ref, w2_ref, a0_ref, a2_ref, g2_ref, kk_ref, ka_ref, rk_ref, gng_ref, gnb_ref,
                 *rest, T, NB):
    out_ref, st_ref, lw_scr, kd_scr, bb_scr, y_scr, kk_scr, s_scr = rest[-8:]
    C = RW_CHUNK
    n_chunks = T // C
    bd = _head_blockdiag(RW_WIDTH)
    seg = lambda t: _mm_exact_rhs(t, bd, n=2)
    ka = ka_ref[...]
    for nb in range(NB):
        k = p_ref[nb, :, 256:512]
        kk = k * kk_ref[...]
        kk = kk * lax.rsqrt(seg(kk * kk) + 1e-12)
        kk_scr[nb] = kk
        for d in range(2):
            wd = p_ref[nb, :, 768 + 64 * d:832 + 64 * d]
            ad = p_ref[nb, :, 896 + 64 * d:960 + 64 * d]
            w_raw = w0_ref[d:d + 1, :] + _mm(jnp.tanh(wd), w2_ref[d])
            lw_scr[nb, d] = -jnp.exp(-_softplus(-w_raw) - 0.5)
            a = _sigmoid(a0_ref[d:d + 1, :] + _mm(ad, a2_ref[d]))
            kd_scr[nb, d] = k * (1.0 + (a - 1.0) * ka)
            bb_scr[nb, d] = kk * a
    s_scr[...] = s0_ref[:, 0]

    ti = lax.broadcasted_iota(jnp.int32, (C, C), 0)
    si = lax.broadcasted_iota(jnp.int32, (C, C), 1)
    ones_cc = jnp.ones((C, C), F32)
    t2 = lax.broadcasted_iota(jnp.int32, (C, 2 * C), 0)
    col2 = lax.broadcasted_iota(jnp.int32, (C, 2 * C), 1)
    right = col2 >= C
    s2 = jnp.where(right, col2 - C, col2)

    def chunk_body(i, carry):
        ch = []
        for nb, d in [(nb, d) for nb in range(NB) for d in range(2)]:
            ci = i if d == 0 else n_chunks - 1 - i
            rows = pl.ds(pl.multiple_of(ci * C, C), C)
            strict = (ti > si) if d == 0 else (ti < si)
            incl = (ti >= si) if d == 0 else (ti <= si)
            lw = lw_scr[nb, d, rows, :]
            cum = _mm_exact_lhs(incl.astype(F32), lw)
            total = _mm_exact_lhs(ones_cc, lw)
            cum_ex = cum - lw
            mid = 0.5 * total
            rr = p_ref[nb, rows, 0:256]
            vv = p_ref[nb, rows, 512:768]
            kdc = kd_scr[nb, d, rows, :]
            bbc = bb_scr[nb, d, rows, :]
            kkc = kk_scr[nb, rows, :]
            e_inv = jnp.exp(mid - cum)
            At = -kkc * jnp.exp(cum_ex - mid)
            Rt = rr * jnp.exp(cum - mid)
            Bt = bbc * e_inv
            Kt = kdc * e_inv
            Ap = -kkc * jnp.exp(cum_ex)
            Rp = rr * jnp.exp(cum)
            e_out = jnp.exp(total - cum)
            Bh = bbc * e_out
            Kh = kdc * e_out
            e_tot = jnp.exp(total[0:1, :])
            for h in range(RW_HEADS):
                hs = slice(h * HEAD_DIM, (h + 1) * HEAD_DIM)
                ch.append(dict(nb=nb, d=d, h=h, rows=rows, hs=hs, strict=strict, incl=incl,
                               AR=jnp.concatenate([At[:, hs], Rt[:, hs]], axis=0),
                               BK=jnp.concatenate([Bt[:, hs], Kt[:, hs]], axis=0), V=vv[:, hs], X1=Ap[:, hs],
                               Rp=Rp[:, hs], BKh=jnp.concatenate([Bh[:, hs], Kh[:, hs]], axis=0),
                               e_tot=e_tot[:, hs]))
        for c in ch:
            c['G'] = _mm(c['AR'], c['BK'], _NT)
        for c in ch:
            fwd = c['d'] == 0
            c['P'] = jnp.where(c['strict'], c['G'][:C, :C], 0.0)
            c['A_ak0'] = jnp.where(((t2 > s2) if fwd else (t2 < s2)) & right, c['G'][:C], 0.0)
            c['A_r'] = jnp.where((t2 >= s2) if fwd else (t2 <= s2), c['G'][C:], 0.0)
            c['VV'] = jnp.concatenate([c['V'], c['V']], axis=0)
        for c in ch:
            c['X2'] = _mm(c['A_ak0'], c['VV'])
        for lvl in range(6):
            for c in ch:
                if lvl < 5:
                    c['PZ'] = _mm(c['P'], jnp.concatenate([c['P'], c['X1'], c['X2']], axis=1))
                else:
                    c['PZ'] = _mm(c['P'], jnp.concatenate([c['X1'], c['X2']], axis=1))
            for c in ch:
                PZ = c['PZ']
                if lvl < 5:
                    c['P'] = PZ[:, :C]
                    c['X1'] = c['X1'] + PZ[:, C:2 * C]
                    c['X2'] = c['X2'] + PZ[:, 2 * C:]
                else:
                    c['X1'] = c['X1'] + PZ[:, :C]
                    c['X2'] = c['X2'] + PZ[:, C:]
        for c in ch:
            c['S0'] = s_scr[c['nb'], c['d'], c['h']]
            c['UY'] = _mm(jnp.concatenate([c['X1'], c['Rp']], axis=0), c['S0'], _NT)
        for c in ch:
            c['U'] = c['UY'][:C] + c['X2']
        for c in ch:
            UV = jnp.concatenate([c['U'], c['V']], axis=0)
            c['Y'] = c['UY'][C:] + _mm(c['A_r'], UV)
            c['S1'] = c['S0'] * c['e_tot'] + _mm(UV, c['BKh'], _TN)
        for c in ch:
            s_scr[c['nb'], c['d'], c['h']] = c['S1']
            y_scr[c['nb'], c['d'], c['rows'], c['hs']] = c['Y']
        return carry

    lax.fori_loop(0, n_chunks, chunk_body, 0)

    for nb in range(NB):
        r = p_ref[nb, :, 0:256]
        v = p_ref[nb, :, 512:768]
        bonus = seg(r * (kd_scr[nb, 0] + kd_scr[nb, 1]) * rk_ref[...]) * v
        g = _mm(_sigmoid(p_ref[nb, :, 1024:1152]), g2_ref[...])
        y = y_scr[nb, 0] + y_scr[nb, 1]
        mu = seg(y) * (1.0 / HEAD_DIM)
        yc = y - mu
        var = seg(yc * yc) * (1.0 / HEAD_DIM)
        yn = yc * lax.rsqrt(var + RW_GN_EPS)
        out_ref[nb] = (yn * gng_ref[...] + gnb_ref[...] + bonus) * g
    st_ref[:, 0] = s_scr[...]


RW_ROWS = 1024


def _state_spec(nb, layer):
    return pl.BlockSpec((nb, 1, 2, RW_HEADS, HEAD_DIM, HEAD_DIM), lambda b: (b, layer, 0, 0, 0, 0))


def rwkv7_mixer_pallas(p_rw, s0, l_in, prm, st_all=None, l_out=0):
    B, T, W = p_rw.shape
    NB = max(2, RW_ROWS // T)
    row = lambda a: a.reshape(1, RW_WIDTH)
    full = lambda shape: pl.BlockSpec(shape, lambda b: (0,) * len(shape))
    st_shape = (B, 1, 2, RW_HEADS, HEAD_DIM, HEAD_DIM) if st_all is None else st_all.shape
    return pl.pallas_call(
        partial(_rwkv_kernel, T=T, NB=NB),
        grid=(B // NB,),
        in_specs=[pl.BlockSpec((NB, T, W), lambda b: (b, 0, 0)), _state_spec(NB, l_in),
                  full((2, RW_WIDTH)), full((2, 64, RW_WIDTH)), full((2, RW_WIDTH)), full((2, 64, RW_WIDTH)),
                  full((128, RW_WIDTH)), full((1, RW_WIDTH)), full((1, RW_WIDTH)), full((1, RW_WIDTH)),
                  full((1, RW_WIDTH)), full((1, RW_WIDTH))]
        + ([] if st_all is None else [pl.BlockSpec(memory_space=pl.ANY)]),
        out_specs=[pl.BlockSpec((NB, T, RW_WIDTH), lambda b: (b, 0, 0)), _state_spec(NB, l_out)],
        out_shape=[jax.ShapeDtypeStruct((B, T, RW_WIDTH), F32), jax.ShapeDtypeStruct(st_shape, F32)],
        input_output_aliases={} if st_all is None else {12: 1},
        scratch_shapes=[pltpu.VMEM((NB, 2, T, RW_WIDTH), F32)] * 4
        + [pltpu.VMEM((NB, T, RW_WIDTH), F32), pltpu.VMEM((NB, 2, RW_HEADS, HEAD_DIM, HEAD_DIM), F32)],
        compiler_params=pltpu.CompilerParams(dimension_semantics=("arbitrary",), vmem_limit_bytes=VMEM_LIMIT_RWKV),
        name="rwkv7_mixer",
    )(p_rw, s0, prm['rw_w0'], prm['rw_w2'], prm['rw_a0'], prm['rw_a2'], prm['rw_g2'], row(prm['rw_kk']),
      row(prm['rw_ka']), row(prm['rw_rk']), row(prm['rw_gn_g']), row(prm['rw_gn_b']),
      *([] if st_all is None else [st_all]))


HG_SUB = 16
HG_ROWS = 256


def _hgrn_kernel(p_ref, s0_ref, lb_ref, ng_ref, *rest, T):
    out_ref, st_ref, lf_scr, kf_scr, o_scr, s_scr = rest[-6:]
    R, c = HG_ROWS, HG_SUB
    n_it = T // R
    x = p_ref[0]
    bd = _head_blockdiag(HG_WIDTH)
    seg = lambda t: _mm_exact_rhs(t, bd, n=2)
    for d in range(2):
        lbd = lb_ref[d:d + 1, :]
        f = lbd + (1.0 - lbd) * _sigmoid(x[:, 256 + 256 * d:512 + 256 * d])
        lf_scr[d] = jnp.log(jnp.maximum(f, HG_F_MIN))
        kf_scr[d] = 1.0 - f
        for h in range(HG_HEADS):
            s_scr[d, h] = s0_ref[0, 0, d, h].T

    ti = lax.broadcasted_iota(jnp.int32, (R, R), 0)
    si = lax.broadcasted_iota(jnp.int32, (R, R), 1)
    same_blk = (ti // c) == (si // c)
    t16 = lax.broadcasted_iota(jnp.int32, (c, 1), 0)

    def body(i, carry):
        for d in range(2):
            ci = i if d == 0 else n_it - 1 - i
            rows = pl.ds(pl.multiple_of(ci * R, R), R)
            incl = (ti >= si) if d == 0 else (ti <= si)
            lf = lf_scr[d, rows, :]
            cum = _mm_exact_lhs((incl & same_blk).astype(F32), lf)
            tot = _mm_exact_lhs(same_blk.astype(F32), lf)
            xq = p_ref[0, rows, 0:256]
            q = xq * _sigmoid(xq)
            v = p_ref[0, rows, 768:1024]
            kf = kf_scr[d, rows, :]
            Qp = q * jnp.exp(cum)
            Kh = kf * jnp.exp(tot - cum)
            e_tot = jnp.exp(tot)
            blocks = range(R // c) if d == 0 else range(R // c - 1, -1, -1)
            ST = [s_scr[d, h] for h in range(HG_HEADS)]
            o_parts = [None] * (R // c)
            for j in blocks:
                rs = slice(j * c, (j + 1) * c)
                cb, qb, kb, vb = cum[rs], q[rs], kf[rs], v[rs]
                half = c // 2
                spans = []
                for s in range(c):
                    if d == 0:
                        spans.append((half, c) if s >= half else (0, c))
                    else:
                        spans.append((0, half) if s < half else (0, c))
                prods = []
                for s, (lo, hi) in enumerate(spans):
                    e = jnp.exp(jnp.minimum(cb[lo:hi] - cb[s:s + 1, :], 0.0))
                    prods.append(qb[lo:hi] * (kb[s:s + 1, :] * e))
                att = _mm_exact_rhs(jnp.concatenate(prods, axis=0), bd, n=1)
                o_half = [jnp.zeros((half, HG_WIDTH), F32), jnp.zeros((half, HG_WIDTH), F32)]
                off = 0
                for s, (lo, hi) in enumerate(spans):
                    keep = (t16[lo:hi] >= s) if d == 0 else (t16[lo:hi] <= s)
                    term = jnp.where(keep, att[off:off + hi - lo], 0.0) * vb[s:s + 1, :]
                    off += hi - lo
                    for p in range(2):
                        a, b = max(lo, p * half), min(hi, (p + 1) * half)
                        if a < b:
                            o_half[p] = o_half[p] + term[a - lo:b - lo]
                o_blk = jnp.concatenate(o_half, axis=0)
                o_heads = []
                for h in range(HG_HEADS):
                    hs = slice(h * HEAD_DIM, (h + 1) * HEAD_DIM)
                    o_heads.append(_mm(Qp[rs, hs], ST[h], _NT))
                    ST[h] = ST[h] * e_tot[j * c:j * c + 1, hs] + _mm(vb[:, hs], Kh[rs, hs], _TN)
                o_parts[j] = o_blk + jnp.concatenate(o_heads, axis=1)
            for h in range(HG_HEADS):
                s_scr[d, h] = ST[h]
            o_scr[d, rows, :] = jnp.concatenate(o_parts, axis=0)
        return carry

    lax.fori_loop(0, n_it, body, 0)

    o = o_scr[0] + o_scr[1]
    o = o * lax.rsqrt(seg(o * o) * (1.0 / HEAD_DIM) + NORM_EPS) * ng_ref[...]
    gg = x[:, 1024:1280]
    out_ref[0] = o * (gg * _sigmoid(gg))
    for d in range(2):
        for h in range(HG_HEADS):
            st_ref[0, 0, d, h] = s_scr[d, h].T


def hgrn2_mixer_pallas(p_hg, s0, l_in, lb, norm_g, st_all=None, l_out=0):
    B, T, W = p_hg.shape
    full = lambda shape: pl.BlockSpec(shape, lambda b: (0,) * len(shape))
    st_shape = (B, 1, 2, HG_HEADS, HEAD_DIM, HEAD_DIM) if st_all is None else st_all.shape
    return pl.pallas_call(
        partial(_hgrn_kernel, T=T),
        grid=(B,),
        in_specs=[pl.BlockSpec((1, T, W), lambda b: (b, 0, 0)), _state_spec(1, l_in), full((2, HG_WIDTH)),
                  full((1, HG_WIDTH))] + ([] if st_all is None else [pl.BlockSpec(memory_space=pl.ANY)]),
        out_specs=[pl.BlockSpec((1, T, HG_WIDTH), lambda b: (b, 0, 0)), _state_spec(1, l_out)],
        out_shape=[jax.ShapeDtypeStruct((B, T, HG_WIDTH), F32), jax.ShapeDtypeStruct(st_shape, F32)],
        input_output_aliases={} if st_all is None else {4: 1},
        scratch_shapes=[pltpu.VMEM((2, T, HG_WIDTH), F32)] * 3
        + [pltpu.VMEM((2, HG_HEADS, HEAD_DIM, HEAD_DIM), F32)],
        compiler_params=pltpu.CompilerParams(dimension_semantics=("arbitrary",), vmem_limit_bytes=VMEM_LIMIT),
        name="hgrn2_mixer",
    )(p_hg, s0, lb, jnp.tile(norm_g.reshape(1, HEAD_DIM), (1, HG_HEADS)), *([] if st_all is None else [st_all]))


ATT_REP = ATT_HEADS // ATT_KV_HEADS
ATT_QROWS = 128


def _swap_pairs(x):
    w = x.shape[-1]
    lane = lax.broadcasted_iota(jnp.int32, x.shape, x.ndim - 1)
    return jnp.where(lane % 2 == 0, pltpu.roll(x, w - 1, x.ndim - 1), pltpu.roll(x, 1, x.ndim - 1))


def _att_kernel(*refs, T, past, rope):
    if rope:
        p_ref, qg_ref, kg_ref, cos_ref, sin_ref, ck_ref, cv_ref, out_ref, k_scr, v_scr, q_scr = refs
    else:
        p_ref, qg_ref, kg_ref, _, _, out_ref, kh_ref, vh_ref, k_scr, v_scr, q_scr = refs
    x = p_ref[0]
    q = x[:, 0:ATT_WIDTH]
    k = x[:, ATT_WIDTH:ATT_WIDTH + KV_WIDTH]
    v = x[:, ATT_WIDTH + KV_WIDTH:ATT_WIDTH + 2 * KV_WIDTH]
    inv_d = 1.0 / HEAD_DIM
    q = q * lax.rsqrt(_mm_exact_rhs(q * q, _head_blockdiag(ATT_WIDTH), n=2) * inv_d + NORM_EPS) * qg_ref[...]
    k = k * lax.rsqrt(_mm_exact_rhs(k * k, _head_blockdiag(KV_WIDTH), n=2) * inv_d + NORM_EPS) * kg_ref[...]
    if rope:
        cos, sin = cos_ref[...], sin_ref[...]
        rep = ATT_WIDTH // KV_WIDTH
        q = q * jnp.concatenate([cos] * rep, axis=1) + _swap_pairs(q) * jnp.concatenate([sin] * rep, axis=1)
        k = k * cos + _swap_pairs(k) * sin
    q_scr[...] = (q * (1.0 / math.sqrt(HEAD_DIM))).astype(BF16)
    for g in range(ATT_KV_HEADS):
        gs = slice(g * HEAD_DIM, (g + 1) * HEAD_DIM)
        if rope:
            k_scr[g, 0:past, :] = ck_ref[0, 0, g].astype(BF16)
            v_scr[g, 0:past, :] = cv_ref[0, 0, g].astype(BF16)
        else:
            kh_ref[0, 0, g] = k[:, gs].T
            vh_ref[0, 0, g] = v[:, gs].T
        k_scr[g, past:past + T, :] = k[:, gs].astype(BF16)
        v_scr[g, past:past + T, :] = v[:, gs].astype(BF16)
    QR = ATT_QROWS

    def q_block(qb, carry):
        rows = pl.ds(pl.multiple_of(qb * QR, QR), QR)
        qblk = q_scr[rows, :]
        for g in range(ATT_KV_HEADS):
            qs = jnp.concatenate([qblk[:, (g * ATT_REP + r) * HEAD_DIM:(g * ATT_REP + r + 1) * HEAD_DIM]
                                  for r in range(ATT_REP)], axis=0)
            s = lax.dot_general(qs, k_scr[g], _NT, preferred_element_type=F32)
            e = jnp.exp(s - jnp.max(s, axis=-1, keepdims=True))
            l = jnp.sum(e, axis=-1, keepdims=True)
            o = lax.dot_general(e.astype(BF16), v_scr[g], _NN, preferred_element_type=F32) / l
            for r in range(ATT_REP):
                h = g * ATT_REP + r
                out_ref[0, rows, h * HEAD_DIM:(h + 1) * HEAD_DIM] = o[r * QR:(r + 1) * QR]
        return carry

    lax.fori_loop(0, T // QR, q_block, 0, unroll=4 if T // QR >= 4 else 2)


def rope_tables(T):
    rows = T // GRID_W
    row = jnp.repeat(jnp.arange(rows, dtype=F32), GRID_W)
    col = jnp.tile(jnp.arange(GRID_W, dtype=F32), rows)
    n_freq = HEAD_DIM // 4
    inv = ROPE_THETA ** (-jnp.arange(n_freq, dtype=F32) / n_freq)
    ang = jnp.concatenate([row[:, None] * inv, col[:, None] * inv], axis=-1)
    cos = jnp.repeat(jnp.cos(ang), 2, axis=-1)
    sin = jnp.stack([-jnp.sin(ang), jnp.sin(ang)], axis=-1).reshape(T, HEAD_DIM)
    return jnp.tile(cos, (1, ATT_KV_HEADS)), jnp.tile(sin, (1, ATT_KV_HEADS))


def attention_pallas(p_att, qnorm_g, knorm_g, l, cache=None, kv_all=None):
    B, T, W = p_att.shape
    rope = cache is not None
    past = cache[0].shape[3] if rope else 0
    full = lambda shape: pl.BlockSpec(shape, lambda b: (0,) * len(shape))
    qg = jnp.tile(qnorm_g.reshape(1, HEAD_DIM), (1, ATT_HEADS))
    kg = jnp.tile(knorm_g.reshape(1, HEAD_DIM), (1, ATT_KV_HEADS))
    in_specs = [pl.BlockSpec((1, T, W), lambda b: (b, 0, 0)), full((1, ATT_WIDTH)), full((1, KV_WIDTH))]
    args = [p_att, qg, kg]
    out_specs = [pl.BlockSpec((1, T, ATT_WIDTH), lambda b: (b, 0, 0))]
    out_shape = [jax.ShapeDtypeStruct((B, T, ATT_WIDTH), F32)]
    if rope:
        cos, sin = rope_tables(T)
        kv_spec = pl.BlockSpec((1, 1, ATT_KV_HEADS, past, HEAD_DIM), lambda b: (b, l, 0, 0, 0))
        in_specs += [full((T, KV_WIDTH)), full((T, KV_WIDTH)), kv_spec, kv_spec]
        args += [cos, sin, cache[0], cache[1]]
        aliases = {}
    else:
        kv_spec = pl.BlockSpec((1, 1, ATT_KV_HEADS, HEAD_DIM, T), lambda b: (b, l, 0, 0, 0))
        in_specs += [pl.BlockSpec(memory_space=pl.ANY)] * 2
        args += list(kv_all)
        aliases = {3: 1, 4: 2}
        out_specs += [kv_spec, kv_spec]
        out_shape += [jax.ShapeDtypeStruct(kv_all[0].shape, F32)] * 2
    res = pl.pallas_call(
        partial(_att_kernel, T=T, past=past, rope=rope),
        grid=(B,),
        in_specs=in_specs,
        out_specs=out_specs,
        out_shape=out_shape,
        input_output_aliases=aliases,
        scratch_shapes=[pltpu.VMEM((ATT_KV_HEADS, past + T, HEAD_DIM), BF16)] * 2
        + [pltpu.VMEM((T, ATT_WIDTH), BF16)],
        compiler_params=pltpu.CompilerParams(dimension_semantics=("arbitrary",), vmem_limit_bytes=VMEM_LIMIT),
        name="attention_rope" if rope else "attention_ctx",
    )(*args)
    return res[0] if rope else tuple(res)


ROW_TILE = 512
ROW_SUB = 256
MOD_TILE = 1536
ROUTE_TILE = 256
MOE_SLOTS = 256
SC_ROWS = 64
SC_LANES = 128
P_ATT, P_RW, P_HG = ATT_WIDTH + 2 * KV_WIDTH, 3 * RW_WIDTH + 384, 5 * HG_WIDTH


def _mod_kernel(c_ref, w_ref, b_ref, o_ref):
    c = c_ref[...]
    o_ref[0] = _mm(c * _sigmoid(c), w_ref[0], passes=3) + b_ref[0]


def adaln_mod_pallas(cvec, w_mod, b_mod):
    n = 6 * D_MODEL
    return pl.pallas_call(
        _mod_kernel,
        grid=(DEPTH, n // MOD_TILE),
        in_specs=[pl.BlockSpec((8, D_MODEL), lambda l, j: (0, 0)),
                  pl.BlockSpec((1, D_MODEL, MOD_TILE), lambda l, j: (l, 0, j)),
                  pl.BlockSpec((1, 1, MOD_TILE), lambda l, j: (l, 0, j))],
        out_specs=pl.BlockSpec((1, 8, MOD_TILE), lambda l, j: (l, 0, j)),
        out_shape=jax.ShapeDtypeStruct((DEPTH, 8, n), F32),
        compiler_params=pltpu.CompilerParams(dimension_semantics=("arbitrary", "arbitrary"),
                                             vmem_limit_bytes=VMEM_LIMIT),
        name="adaln_mod",
    )(cvec, w_mod, b_mod.reshape(DEPTH, 1, n))


def _rms(x):
    return x * lax.rsqrt(jnp.mean(x * x, axis=-1, keepdims=True) + NORM_EPS)


HALF = D_MODEL // 2
U32 = jnp.uint32


def _pack_rows(x):
    bits = lax.bitcast_convert_type(x.astype(BF16).astype(F32), U32)
    return (bits[:, :HALF] >> 16) | bits[:, HALF:]


def _unpack_rows(w):
    lo = lax.bitcast_convert_type(w << 16, F32)
    hi = lax.bitcast_convert_type(w & jnp.uint32(0xFFFF0000), F32)
    return lo, hi


def _row_pieces():
    return [slice(r, r + ROW_SUB) for r in range(0, ROW_TILE, ROW_SUB)]


def _moe_residual(x_ref, g_ref, g4_ref, pm_ref, rows):
    y = None
    for k in range(TOP_K):
        t = g4_ref[rows, k:k + 1] * jnp.concatenate(_unpack_rows(g_ref[k, rows, :]), axis=1)
        y = t if y is None else y + t
    return x_ref[rows, :] + pm_ref[0, 5:6, :] * y


def _in_kernel(*refs, has_res):
    if has_res:
        x_ref, gth_ref, g4_ref, pm_ref, m_ref, g_ref, w_ref, xo_ref, pa_ref, pr_ref, ph_ref = refs
    else:
        x_ref, m_ref, g_ref, w_ref, pa_ref, pr_ref, ph_ref = refs
    for rows in _row_pieces():
        if has_res:
            x = _moe_residual(x_ref, gth_ref, g4_ref, pm_ref, rows)
            xo_ref[rows, :] = x
        else:
            x = x_ref[rows, :]
        h = _rms(x) * g_ref[...] * (1.0 + m_ref[0, 1:2, :]) + m_ref[0, 0:1, :]
        proj = lax.dot_general(h, w_ref[0], _NN, precision=lax.Precision.DEFAULT, preferred_element_type=F32)
        pa_ref[rows, :] = proj[:, 0:P_ATT]
        pr_ref[rows, :] = proj[:, P_ATT:P_ATT + P_RW]
        ph_ref[rows, :] = proj[:, P_ATT + P_RW:]


def _res_specs(row0):
    t0 = row0 // ROW_TILE
    return [pl.BlockSpec((TOP_K, ROW_TILE, HALF), lambda i: (0, i, 0)),
            pl.BlockSpec((ROW_TILE, TOP_K), lambda i: (i + t0, 0))]


def in_proj_pallas(x, mod, norm_g, w_in, l, rows_per_mod, res=None):
    R = x.shape[0]
    tpm = rows_per_mod // ROW_TILE
    rt = lambda w: pl.BlockSpec((ROW_TILE, w), lambda i: (i, 0))
    ms = pl.BlockSpec((1, 6, D_MODEL), lambda i: (i // tpm, 0, 0))
    full = lambda shape, **kw: pl.BlockSpec(shape, lambda i: (0,) * len(shape), **kw)
    in_specs = [rt(D_MODEL)] + (_res_specs(res[2]) + [ms] if res else []) + [
        ms, full((1, D_MODEL)),
        pl.BlockSpec((1, D_MODEL, w_in.shape[2]), lambda i: (l, 0, 0), pipeline_mode=pl.Buffered(1))]
    args = [x] + ([res[0], res[1], res[3]] if res else []) + [mod, norm_g.reshape(1, D_MODEL), w_in]
    widths = ([D_MODEL] if res else []) + [P_ATT, P_RW, P_HG]
    return pl.pallas_call(
        partial(_in_kernel, has_res=res is not None),
        grid=(R // ROW_TILE,),
        in_specs=in_specs,
        out_specs=[rt(w) for w in widths],
        out_shape=[jax.ShapeDtypeStruct((R, w), F32) for w in widths],
        compiler_params=pltpu.CompilerParams(dimension_semantics=("arbitrary",), vmem_limit_bytes=VMEM_LIMIT),
        name="in_proj",
    )(*args)


def _out_kernel(att_ref, rw_ref, hg_ref, x_ref, m_ref, g_ref, w_ref, rw_w_ref, rb_ref, *rest):
    xo_ref, h_ref, lg_ref = rest[-3:]
    d = lambda a, lo, hi: lax.dot_general(a, w_ref[0, lo:hi, :], _NN, precision=lax.Precision.DEFAULT,
                                          preferred_element_type=F32)
    for rows in _row_pieces():
        mixo = (d(att_ref[rows, :], 0, ATT_WIDTH) + d(rw_ref[rows, :], ATT_WIDTH, ATT_WIDTH + RW_WIDTH)
                + d(hg_ref[rows, :], ATT_WIDTH + RW_WIDTH, ATT_WIDTH + RW_WIDTH + HG_WIDTH))
        x = x_ref[rows, :] + m_ref[0, 2:3, :] * mixo
        xo_ref[rows, :] = x
        h = _rms(x) * g_ref[...] * (1.0 + m_ref[0, 4:5, :]) + m_ref[0, 3:4, :]
        h_ref[rows, :] = _pack_rows(h)
        lg_ref[rows, :] = _mm(h, rw_w_ref[...], passes=3) + rb_ref[...]


def out_proj_pallas(att, rw, hg, x, mod, norm_g, w_out, l, router_w, router_b, rows_per_mod, n_all, row0, joint=None):
    R = x.shape[0]
    tpm = rows_per_mod // ROW_TILE
    t0 = row0 // ROW_TILE
    rt = lambda w: pl.BlockSpec((ROW_TILE, w), lambda i: (i, 0))
    jt = lambda w: pl.BlockSpec((ROW_TILE, w), lambda i: (i + t0, 0))
    full = lambda shape: pl.BlockSpec(shape, lambda i: (0,) * len(shape))
    in_specs = [rt(ATT_WIDTH), rt(RW_WIDTH), rt(HG_WIDTH), rt(D_MODEL),
                pl.BlockSpec((1, 6, D_MODEL), lambda i: (i // tpm, 0, 0)), full((1, D_MODEL)),
                pl.BlockSpec((1, D_MODEL, D_MODEL), lambda i: (l, 0, 0)), full((D_MODEL, N_EXPERTS)),
                full((1, N_EXPERTS))]
    args = [att, rw, hg, x, mod, norm_g.reshape(1, D_MODEL), w_out, router_w, router_b.reshape(1, N_EXPERTS)]
    aliases = {}
    if joint is not None:
        in_specs += [pl.BlockSpec(memory_space=pl.ANY)] * 2
        aliases = {len(args): 1, len(args) + 1: 2}
        args += list(joint)
    return pl.pallas_call(
        _out_kernel,
        grid=(R // ROW_TILE,),
        in_specs=in_specs,
        out_specs=[rt(D_MODEL), jt(HALF), jt(N_EXPERTS)],
        out_shape=[jax.ShapeDtypeStruct((R, D_MODEL), F32), jax.ShapeDtypeStruct((n_all, HALF), U32),
                   jax.ShapeDtypeStruct((n_all, N_EXPERTS), F32)],
        input_output_aliases=aliases,
        compiler_params=pltpu.CompilerParams(dimension_semantics=("arbitrary",), vmem_limit_bytes=VMEM_LIMIT),
        name="out_proj",
    )(*args)


def _final_kernel(x_ref, gth_ref, g4_ref, m_ref, g_ref, o_ref):
    for rows in _row_pieces():
        o_ref[rows, :] = _rms(_moe_residual(x_ref, gth_ref, g4_ref, m_ref, rows)) * g_ref[...]


def final_norm_pallas(x, gathered, gate4, row0, mod, norm_g, rows_per_mod):
    R = x.shape[0]
    tpm = rows_per_mod // ROW_TILE
    rt = pl.BlockSpec((ROW_TILE, D_MODEL), lambda i: (i, 0))
    return pl.pallas_call(
        _final_kernel,
        grid=(R // ROW_TILE,),
        in_specs=[rt] + _res_specs(row0) + [pl.BlockSpec((1, 6, D_MODEL), lambda i: (i // tpm, 0, 0)),
                                            pl.BlockSpec((1, D_MODEL), lambda i: (0, 0))],
        out_specs=rt,
        out_shape=jax.ShapeDtypeStruct((R, D_MODEL), F32),
        name="final_norm",
    )(x, gathered, gate4, mod, norm_g.reshape(1, D_MODEL))


def _moe_max_blocks(n_tok):
    return (n_tok * TOP_K + N_EXPERTS * (MOE_SLOTS - 1)) // MOE_SLOTS


def _route_kernel(lg_ref, dest_ref, gate4_ref, blk_ref, rank_scr, gate_scr, *, n_tok):
    Rt, E = ROUTE_TILE, N_EXPERTS
    n_tiles = n_tok // Rt
    ti = lax.broadcasted_iota(jnp.int32, (Rt, Rt), 0)
    si = lax.broadcasted_iota(jnp.int32, (Rt, Rt), 1)
    before_t = (ti < si).astype(BF16)
    eye_t = (ti == si).astype(BF16)
    ei = lax.broadcasted_iota(jnp.int32, (E, E), 0)
    ej = lax.broadcasted_iota(jnp.int32, (E, E), 1)
    before_e = (ei > ej).astype(BF16)
    sub = lax.broadcasted_iota(jnp.int32, (E, Rt), 0)
    d = lambda a, b, dims: lax.dot_general(a, b, dims, preferred_element_type=F32)

    def tile_members(it, off):
        rows = pl.ds(pl.multiple_of(it * Rt, Rt), Rt)
        l3, l2, l1 = reversed(_split(lg_ref[rows, :], 3))
        lgT = d(l3, eye_t, _TN) + d(l2, eye_t, _TN) + d(l1, eye_t, _TN)
        work = lgT
        member = jnp.zeros((E, Rt), jnp.bool_)
        top = None
        for k in range(TOP_K):
            m = jnp.max(work, axis=0, keepdims=True)
            if top is None:
                top = m
            first = jnp.min(jnp.where(work == m, sub, E), axis=0, keepdims=True)
            pick = sub == first
            member = member | pick
            work = jnp.where(pick, -jnp.inf, work)
        ex = jnp.where(member, jnp.exp(lgT - top), 0.0)
        gate_scr[:, rows] = ex / jnp.sum(ex, axis=0, keepdims=True)
        mem = member.astype(BF16)
        rank = d(mem, before_t, _NN) + off
        rank_scr[:, rows] = jnp.where(member, rank, -1.0)
        return off + jnp.sum(mem.astype(F32), axis=1, keepdims=True)

    count = lax.fori_loop(0, n_tiles, tile_members, jnp.zeros((E, 1), F32), unroll=4)
    nblk = jnp.floor((count + (MOE_SLOTS - 1)) * (1.0 / MOE_SLOTS))
    bstart = d(before_e, jnp.broadcast_to(nblk, (E, 128)).astype(BF16), _NN)[:, 0:1]
    bend = bstart + nblk
    pstart = bstart * MOE_SLOTS

    def tile_slots(it, carry):
        rows = pl.ds(pl.multiple_of(it * Rt, Rt), Rt)
        rank = rank_scr[:, rows]
        gate = gate_scr[:, rows]
        member = rank >= 0.0
        kidx = d(before_e, member.astype(BF16), _NN)
        slot = pstart + rank
        grows = []
        for k in range(TOP_K):
            sel = member & (kidx == k)
            dest_ref[k:k + 1, rows] = jnp.sum(jnp.where(sel, slot, 0.0), axis=0, keepdims=True).astype(jnp.int32)
            grows.append(jnp.sum(jnp.where(sel, gate, 0.0), axis=0, keepdims=True))
        g4t = jnp.concatenate(grows + [jnp.zeros((128 - TOP_K, Rt), F32)], axis=0)
        g3, g2, g1 = reversed(_split(g4t, 3))
        g4 = d(eye_t, g3, _NT) + d(eye_t, g2, _NT) + d(eye_t, g1, _NT)
        gate4_ref[rows, :] = g4[:, 0:TOP_K]
        return carry

    lax.fori_loop(0, n_tiles, tile_slots, 0, unroll=4)
    nb = blk_ref.shape[1]
    bi = lax.broadcasted_iota(jnp.int32, (E, nb), 1).astype(F32)
    owner = jnp.sum((bend <= bi).astype(F32), axis=0, keepdims=True)
    blk_ref[0:1, :] = jnp.minimum(owner, E - 1.0).astype(jnp.int32)
    blk_ref[1:2, :] = jnp.broadcast_to(jnp.sum(nblk, axis=0, keepdims=True), (1, nb)).astype(jnp.int32)
    blk_ref[2:8, :] = jnp.zeros((6, nb), jnp.int32)


def moe_route_pallas(logits):
    n_tok = logits.shape[0]
    nb = -(-_moe_max_blocks(n_tok) // 128) * 128
    return pl.pallas_call(
        partial(_route_kernel, n_tok=n_tok),
        out_shape=[jax.ShapeDtypeStruct((TOP_K, n_tok), jnp.int32),
                   jax.ShapeDtypeStruct((n_tok, TOP_K), F32),
                   jax.ShapeDtypeStruct((8, nb), jnp.int32)],
        scratch_shapes=[pltpu.VMEM((N_EXPERTS, n_tok), F32)] * 2,
        name="moe_route",
    )(logits)


def _moe_block_kernel(be_ref, nu_ref, first_ref, par_ref, nxt_ref, xb_ref, wgu_hbm, bgu_ref, wdn_hbm, bdn_ref, yb_ref,
                      wgu_buf, wdn_buf, sem, *, l):
    i = pl.program_id(0)

    def weight_copies(e, slot):
        return (pltpu.make_async_copy(wgu_hbm.at[l, e], wgu_buf.at[slot], sem.at[0, slot]),
                pltpu.make_async_copy(wdn_hbm.at[l, e], wdn_buf.at[slot], sem.at[1, slot]))

    @pl.when(i < nu_ref[0])
    def _():
        slot = par_ref[i]

        @pl.when(first_ref[i] == 1)
        def _():
            @pl.when(i == 0)
            def _():
                for cp in weight_copies(be_ref[0], slot):
                    cp.start()

            for cp in weight_copies(be_ref[i], slot):
                cp.wait()

            @pl.when(nxt_ref[i] >= 0)
            def _():
                for cp in weight_copies(nxt_ref[i], 1 - slot):
                    cp.start()

        dot = lambda a, w: lax.dot_general(a, w, _NN, precision=lax.Precision.DEFAULT, preferred_element_type=F32)
        x_lo, x_hi = _unpack_rows(xb_ref[...])
        gu = dot(x_lo, wgu_buf[slot, 0:HALF]) + dot(x_hi, wgu_buf[slot, HALF:D_MODEL]) + bgu_ref[0, 0]
        glu = jnp.minimum(gu[:, :EXPERT_FF], SWIGLU_LIMIT)
        lin = jnp.clip(gu[:, EXPERT_FF:], -SWIGLU_LIMIT, SWIGLU_LIMIT)
        act = glu * _sigmoid(SWIGLU_ALPHA * glu) * (lin + 1.0)
        yb_ref[...] = _pack_rows(dot(act, wdn_buf[slot]) + bdn_ref[0, 0])


def _expert_runs(block_e, n_used):
    n = block_e.shape[0]
    idx = jnp.arange(n, dtype=jnp.int32)
    valid = idx < n_used[0]
    first = valid & ((idx == 0) | (block_e != jnp.roll(block_e, 1)))
    par = (jnp.cumsum(first.astype(jnp.int32)) - 1) % 2
    start = jnp.where(first, idx, n)
    nxt_start = lax.cummin(jnp.concatenate([start[1:], jnp.full((1,), n, jnp.int32)]), reverse=True)
    nxt = jnp.where(nxt_start < n, block_e[jnp.minimum(nxt_start, n - 1)], -1)
    return first.astype(jnp.int32), par.astype(jnp.int32), nxt.astype(jnp.int32)


def moe_blocks_pallas(xb, block_e, n_used, l, w_gu, b_gu, w_down, b_down):
    n_blocks = xb.shape[0] // MOE_SLOTS
    first, par, nxt = _expert_runs(block_e, n_used)
    blk = lambda i, be, nu, *_: (jnp.minimum(i, nu[0] - 1), 0)
    bsel = lambda i, be, nu, *_: (l, be[jnp.minimum(i, nu[0] - 1)], 0, 0)
    grid_spec = pltpu.PrefetchScalarGridSpec(
        num_scalar_prefetch=5,
        grid=(n_blocks,),
        in_specs=[pl.BlockSpec((MOE_SLOTS, HALF), blk),
                  pl.BlockSpec(memory_space=pl.ANY),
                  pl.BlockSpec((1, 1, 1, 2 * EXPERT_FF), bsel),
                  pl.BlockSpec(memory_space=pl.ANY),
                  pl.BlockSpec((1, 1, 1, D_MODEL), bsel)],
        out_specs=pl.BlockSpec((MOE_SLOTS, HALF), blk),
        scratch_shapes=[pltpu.VMEM((2, D_MODEL, 2 * EXPERT_FF), F32), pltpu.VMEM((2, EXPERT_FF, D_MODEL), F32),
                        pltpu.SemaphoreType.DMA((2, 2))],
    )
    return pl.pallas_call(
        partial(_moe_block_kernel, l=l),
        grid_spec=grid_spec,
        out_shape=jax.ShapeDtypeStruct(xb.shape, U32),
        compiler_params=pltpu.CompilerParams(dimension_semantics=("arbitrary",), vmem_limit_bytes=VMEM_LIMIT),
        name="moe_blocks",
    )(block_e, n_used, first, par, nxt, xb, w_gu, b_gu.reshape(DEPTH, N_EXPERTS, 1, 2 * EXPERT_FF), w_down,
      b_down.reshape(DEPTH, N_EXPERTS, 1, D_MODEL))


def _sc_mesh():
    return plsc.VectorSubcoreMesh(core_axis_name="c", subcore_axis_name="s")


def _sc_index_rows(idx):
    return jnp.pad(idx.reshape(-1, SC_ROWS), ((0, 0), (0, SC_LANES - SC_ROWS)))


def sc_dispatch(h, dest, n_rows):
    n_tok, d = h.shape
    idx = [_sc_index_rows(dest[k]) for k in range(TOP_K)]

    @pl.kernel(out_type=jax.ShapeDtypeStruct((n_rows, d), h.dtype), mesh=_sc_mesh(), scratch_types=[])
    def kern(h_hbm, i0, i1, i2, i3, o_hbm):
        def body(x_vmem, *i_vmem):
            for iv in i_vmem:
                pltpu.sync_copy(x_vmem, o_hbm.at[iv.at[0, pl.ds(0, SC_ROWS)]])

        pltpu.emit_pipeline(
            body,
            grid=(n_tok // SC_ROWS,),
            in_specs=[pl.BlockSpec((SC_ROWS, d), lambda i: (i, 0))]
            + [pl.BlockSpec((1, SC_LANES), lambda i: (i, 0))] * TOP_K,
            out_specs=[],
            core_axis_name=("c", "s"),
            dimension_semantics=(pltpu.PARALLEL,),
        )(h_hbm, i0, i1, i2, i3)

    return kern(h, *idx)


def sc_combine_gather(yb, dest):
    n_tok = dest.shape[1]
    d = yb.shape[1]
    idx = _sc_index_rows(dest.reshape(TOP_K * n_tok))

    @pl.kernel(out_type=jax.ShapeDtypeStruct((TOP_K * n_tok, d), yb.dtype), mesh=_sc_mesh(), scratch_types=[])
    def kern(y_hbm, i_hbm, o_hbm):
        def body(i_vmem, o_vmem):
            pltpu.sync_copy(y_hbm.at[i_vmem.at[0, pl.ds(0, SC_ROWS)]], o_vmem)

        pltpu.emit_pipeline(
            body,
            grid=(TOP_K * n_tok // SC_ROWS,),
            in_specs=[pl.BlockSpec((1, SC_LANES), lambda i: (i, 0))],
            out_specs=[pl.BlockSpec((SC_ROWS, d), lambda i: (i, 0))],
            core_axis_name=("c", "s"),
            dimension_semantics=(pltpu.PARALLEL,),
        )(i_hbm, o_hbm)

    return kern(yb, idx).reshape(TOP_K, n_tok, d)


def hgrn_lower_bounds(hg_lb):
    sm = jax.nn.softmax(hg_lb.astype(jnp.float32), axis=0)
    return jnp.cumsum(sm, axis=0) - sm[0:1]


def kernel(x_prompt, x_sample, cache_att_k, cache_att_v, state_rwkv, state_hgrn, c, c_ctx, w_mod, b_mod, norm_mix_g, norm_ffn_g, w_in, w_out, att_qnorm_g, att_knorm_g, rw_w0, rw_w2, rw_a0, rw_a2, rw_g2, rw_kk, rw_ka, rw_rk, rw_gn_g, rw_gn_b, hg_lb, hg_norm_g, router_w, router_b, moe_w_gu, moe_b_gu, moe_w_down, moe_b_down, final_norm_g):
    BP, TP, _ = x_prompt.shape
    BS, TS, _ = x_sample.shape
    n_p, n_s = BP * TP, BS * TS
    lb_all = hgrn_lower_bounds(hg_lb)
    cvec = jnp.concatenate([c_ctx[None, :], c, jnp.zeros((8 - 1 - BS, D_MODEL), F32)], axis=0)
    mod_all = adaln_mod_pallas(cvec, w_mod, b_mod).reshape(DEPTH, 8, 6, D_MODEL)
    zeros_state = jnp.zeros((BP, 1, 2, RW_HEADS, HEAD_DIM, HEAD_DIM), F32)
    kv_all = (jnp.zeros((BP, DEPTH, ATT_KV_HEADS, HEAD_DIM, TP), F32),) * 2
    rw_states = jnp.zeros((BP, DEPTH, 2, RW_HEADS, HEAD_DIM, HEAD_DIM), F32)
    hg_states = jnp.zeros((BP, DEPTH, 2, HG_HEADS, HEAD_DIM, HEAD_DIM), F32)
    x = {'p': x_prompt.reshape(n_p, D_MODEL), 's': x_sample.reshape(n_s, D_MODEL)}
    dims = {'p': (TP, BP, n_p, 0), 's': (TS, BS, TS, n_p)}
    moe_out, mod_prev = None, None
    for l in range(DEPTH):
        prm = dict(rw_w0=rw_w0[l], rw_w2=rw_w2[l], rw_a0=rw_a0[l], rw_a2=rw_a2[l], rw_g2=rw_g2[l],
                   rw_kk=rw_kk[l], rw_ka=rw_ka[l], rw_rk=rw_rk[l], rw_gn_g=rw_gn_g[l], rw_gn_b=rw_gn_b[l])
        mods = {'p': mod_all[l, 0:1], 's': mod_all[l, 1:1 + BS]}
        joint = (jnp.zeros((n_p + n_s, HALF), U32), jnp.zeros((n_p + n_s, N_EXPERTS), F32))
        for s in ('p', 's'):
            T, B, rpm, row0 = dims[s]
            if l == 0:
                p_att, p_rw, p_hg = in_proj_pallas(x[s], mods[s], norm_mix_g[l], w_in, l, rpm)
            else:
                x[s], p_att, p_rw, p_hg = in_proj_pallas(x[s], mods[s], norm_mix_g[l], w_in, l, rpm,
                                                         res=(*moe_out[s], row0, mod_prev[s]))
            p_att, p_rw, p_hg = (t.reshape(B, T, -1) for t in (p_att, p_rw, p_hg))
            if s == 'p':
                att, *kv_all = attention_pallas(p_att, att_qnorm_g[l], att_knorm_g[l], l, kv_all=kv_all)
                rw_out, rw_states = rwkv7_mixer_pallas(p_rw, zeros_state, 0, prm, rw_states, l)
                hg_out, hg_states = hgrn2_mixer_pallas(p_hg, zeros_state, 0, lb_all[l], hg_norm_g[l], hg_states, l)
            else:
                att = attention_pallas(p_att, att_qnorm_g[l], att_knorm_g[l], l, cache=(cache_att_k, cache_att_v))
                rw_out, _ = rwkv7_mixer_pallas(p_rw, state_rwkv, l, prm)
                hg_out, _ = hgrn2_mixer_pallas(p_hg, state_hgrn, l, lb_all[l], hg_norm_g[l])
            x[s], *joint = out_proj_pallas(att.reshape(B * T, -1), rw_out.reshape(B * T, -1),
                                           hg_out.reshape(B * T, -1), x[s], mods[s], norm_ffn_g[l], w_out, l,
                                           router_w[l], router_b[l], rpm, n_p + n_s, row0, joint)
        h_all, logits_all = joint
        dest, gate4, blk = moe_route_pallas(logits_all)
        xb = sc_dispatch(h_all, dest, _moe_max_blocks(n_p + n_s) * MOE_SLOTS)
        yb = moe_blocks_pallas(xb, blk[0], blk[1, :1], l, moe_w_gu, moe_b_gu, moe_w_down, moe_b_down)
        moe_out = {'p': (sc_combine_gather(yb, dest[:, :n_p]), gate4), 's': (sc_combine_gather(yb, dest[:, n_p:]), gate4)}
        mod_prev = mods
    y_prompt = final_norm_pallas(x['p'], *moe_out['p'], 0, mod_prev['p'], final_norm_g, n_p)
    y_sample = final_norm_pallas(x['s'], *moe_out['s'], n_p, mod_prev['s'], final_norm_g, TS)
    return (y_prompt.reshape(x_prompt.shape), y_sample.reshape(x_sample.shape),
            jnp.swapaxes(kv_all[0], 3, 4), jnp.swapaxes(kv_all[1], 3, 4),
            rw_states, hg_states)
```

```python
import math
from functools import partial

import jax
import jax.numpy as jnp
from jax import lax
from jax.experimental import pallas as pl
from jax.experimental.pallas import tpu as pltpu
from jax.experimental.pallas import tpu_sc as plsc

D_MODEL = 1024
DEPTH = 2
GRID_W = 64
HEAD_DIM = 64
ATT_HEADS = 8
ATT_KV_HEADS = 2
ATT_WIDTH = ATT_HEADS * HEAD_DIM
KV_WIDTH = ATT_KV_HEADS * HEAD_DIM
RW_HEADS = 4
RW_WIDTH = RW_HEADS * HEAD_DIM
RW_GN_EPS = 64e-5
HG_HEADS = 4
HG_WIDTH = HG_HEADS * HEAD_DIM
HG_F_MIN = 1e-6
N_EXPERTS = 32
TOP_K = 4
EXPERT_FF = D_MODEL
SWIGLU_LIMIT = 7.0
SWIGLU_ALPHA = 1.702
ROPE_THETA = 10000.0
NORM_EPS = 1e-6

RW_CHUNK = 64
BF16 = jnp.bfloat16
F32 = jnp.float32

V7X_VMEM_BYTES = 64 * 1024 * 1024
VMEM_LIMIT = V7X_VMEM_BYTES * 3 // 4
VMEM_LIMIT_RWKV = V7X_VMEM_BYTES * 7 // 8

_NN = (((1,), (0,)), ((), ()))
_NT = (((1,), (1,)), ((), ()))
_TN = (((0,), (0,)), ((), ()))


def _split(x, n):
    parts = []
    for _ in range(n - 1):
        hi = x.astype(BF16)
        parts.append(hi)
        x = x - hi.astype(F32)
    parts.append(x.astype(BF16))
    return parts


def _mm(a, b, dims=_NN, passes=1):
    d = lambda x, y: lax.dot_general(x, y, dims, preferred_element_type=F32)
    if passes == 1:
        return d(a.astype(BF16), b.astype(BF16))
    ah, al = _split(a, 2)
    bh, bl = _split(b, 2)
    return d(ah, bl) + d(al, bh) + d(ah, bh)


def _mm_exact_lhs(a01, b, n=3):
    a = a01.astype(BF16)
    out = None
    for t in reversed(_split(b, n)):
        y = lax.dot_general(a, t, _NN, preferred_element_type=F32)
        out = y if out is None else out + y
    return out


def _mm_exact_rhs(a, b01, n=3):
    b = b01.astype(BF16)
    out = None
    for t in reversed(_split(a, n)):
        y = lax.dot_general(t, b, _NN, preferred_element_type=F32)
        out = y if out is None else out + y
    return out


def _head_blockdiag(width):
    r = lax.broadcasted_iota(jnp.int32, (width, width), 0) // HEAD_DIM
    c = lax.broadcasted_iota(jnp.int32, (width, width), 1) // HEAD_DIM
    return (r == c).astype(F32)


def _sigmoid(x):
    return 1.0 / (1.0 + jnp.exp(-x))


def _softplus(x):
    return jnp.maximum(x, 0.0) + jnp.log(1.0 + jnp.exp(-jnp.abs(x)))


def _rwkv_kernel(p_ref, s0_ref, w0_ref, w2_ref, a0_ref, a2_ref, g2_ref, kk_ref, ka_ref, rk_ref, gng_ref, gnb_ref,
                 *rest, T, NB):
    out_ref, st_ref, lw_scr, kd_scr, bb_scr, y_scr, kk_scr, s_scr = rest[-8:]
    C = RW_CHUNK
    n_chunks = T // C
    bd = _head_blockdiag(RW_WIDTH)
    seg = lambda t: _mm_exact_rhs(t, bd, n=2)
    ka = ka_ref[...]
    for nb in range(NB):
        k = p_ref[nb, :, 256:512]
        kk = k * kk_ref[...]
        kk = kk * lax.rsqrt(seg(kk * kk) + 1e-12)
        kk_scr[nb] = kk
        for d in range(2):
            wd = p_ref[nb, :, 768 + 64 * d:832 + 64 * d]
            ad = p_ref[nb, :, 896 + 64 * d:960 + 64 * d]
            w_raw = w0_ref[d:d + 1, :] + _mm(jnp.tanh(wd), w2_ref[d])
            lw_scr[nb, d] = -jnp.exp(-_softplus(-w_raw) - 0.5)
            a = _sigmoid(a0_ref[d:d + 1, :] + _mm(ad, a2_ref[d]))
            kd_scr[nb, d] = k * (1.0 + (a - 1.0) * ka)
            bb_scr[nb, d] = kk * a
    s_scr[...] = s0_ref[:, 0]

    ti = lax.broadcasted_iota(jnp.int32, (C, C), 0)
    si = lax.broadcasted_iota(jnp.int32, (C, C), 1)
    ones_cc = jnp.ones((C, C), F32)
    t2 = lax.broadcasted_iota(jnp.int32, (C, 2 * C), 0)
    col2 = lax.broadcasted_iota(jnp.int32, (C, 2 * C), 1)
    right = col2 >= C
    s2 = jnp.where(right, col2 - C, col2)

    def chunk_body(i, carry):
        ch = []
        for nb, d in [(nb, d) for nb in range(NB) for d in range(2)]:
            ci = i if d == 0 else n_chunks - 1 - i
            rows = pl.ds(pl.multiple_of(ci * C, C), C)
            strict = (ti > si) if d == 0 else (ti < si)
            incl = (ti >= si) if d == 0 else (ti <= si)
            lw = lw_scr[nb, d, rows, :]
            cum = _mm_exact_lhs(incl.astype(F32), lw)
            total = _mm_exact_lhs(ones_cc, lw)
            cum_ex = cum - lw
            mid = 0.5 * total
            rr = p_ref[nb, rows, 0:256]
            vv = p_ref[nb, rows, 512:768]
            kdc = kd_scr[nb, d, rows, :]
            bbc = bb_scr[nb, d, rows, :]
            kkc = kk_scr[nb, rows, :]
            e_inv = jnp.exp(mid - cum)
            At = -kkc * jnp.exp(cum_ex - mid)
            Rt = rr * jnp.exp(cum - mid)
            Bt = bbc * e_inv
            Kt = kdc * e_inv
            Ap = -kkc * jnp.exp(cum_ex)
            Rp = rr * jnp.exp(cum)
            e_out = jnp.exp(total - cum)
            Bh = bbc * e_out
            Kh = kdc * e_out
            e_tot = jnp.exp(total[0:1, :])
            for h in range(RW_HEADS):
                hs = slice(h * HEAD_DIM, (h + 1) * HEAD_DIM)
                ch.append(dict(nb=nb, d=d, h=h, rows=rows, hs=hs, strict=strict, incl=incl,
                               AR=jnp.concatenate([At[:, hs], Rt[:, hs]], axis=0),
                               BK=jnp.concatenate([Bt[:, hs], Kt[:, hs]], axis=0), V=vv[:, hs], X1=Ap[:, hs],
                               Rp=Rp[:, hs], BKh=jnp.concatenate([Bh[:, hs], Kh[:, hs]], axis=0),
                               e_tot=e_tot[:, hs]))
        for c in ch:
            c['G'] = _mm(c['AR'], c['BK'], _NT)
        for c in ch:
            fwd = c['d'] == 0
            c['P'] = jnp.where(c['strict'], c['G'][:C, :C], 0.0)
            c['A_ak0'] = jnp.where(((t2 > s2) if fwd else (t2 < s2)) & right, c['G'][:C], 0.0)
            c['A_r'] = jnp.where((t2 >= s2) if fwd else (t2 <= s2), c['G'][C:], 0.0)
            c['VV'] = jnp.concatenate([c['V'], c['V']], axis=0)
        for c in ch:
            c['X2'] = _mm(c['A_ak0'], c['VV'])
        for lvl in range(6):
            for c in ch:
                if lvl < 5:
                    c['PZ'] = _mm(c['P'], jnp.concatenate([c['P'], c['X1'], c['X2']], axis=1))
                else:
                    c['PZ'] = _mm(c['P'], jnp.concatenate([c['X1'], c['X2']], axis=1))
            for c in ch:
                PZ = c['PZ']
                if lvl < 5:
                    c['P'] = PZ[:, :C]
                    c['X1'] = c['X1'] + PZ[:, C:2 * C]
                    c['X2'] = c['X2'] + PZ[:, 2 * C:]
                else:
                    c['X1'] = c['X1'] + PZ[:, :C]
                    c['X2'] = c['X2'] + PZ[:, C:]
        for c in ch:
            c['S0'] = s_scr[c['nb'], c['d'], c['h']]
            c['UY'] = _mm(jnp.concatenate([c['X1'], c['Rp']], axis=0), c['S0'], _NT)
        for c in ch:
            c['U'] = c['UY'][:C] + c['X2']
        for c in ch:
            UV = jnp.concatenate([c['U'], c['V']], axis=0)
            c['Y'] = c['UY'][C:] + _mm(c['A_r'], UV)
            c['S1'] = c['S0'] * c['e_tot'] + _mm(UV, c['BKh'], _TN)
        for c in ch:
            s_scr[c['nb'], c['d'], c['h']] = c['S1']
            y_scr[c['nb'], c['d'], c['rows'], c['hs']] = c['Y']
        return carry

    lax.fori_loop(0, n_chunks, chunk_body, 0)

    for nb in range(NB):
        r = p_ref[nb, :, 0:256]
        v = p_ref[nb, :, 512:768]
        bonus = seg(r * (kd_scr[nb, 0] + kd_scr[nb, 1]) * rk_ref[...]) * v
        g = _mm(_sigmoid(p_ref[nb, :, 1024:1152]), g2_ref[...])
        y = y_scr[nb, 0] + y_scr[nb, 1]
        mu = seg(y) * (1.0 / HEAD_DIM)
        yc = y - mu
        var = seg(yc * yc) * (1.0 / HEAD_DIM)
        yn = yc * lax.rsqrt(var + RW_GN_EPS)
        out_ref[nb] = (yn * gng_ref[...] + gnb_ref[...] + bonus) * g
    st_ref[:, 0] = s_scr[...]


RW_ROWS = 1024


def _state_spec(nb, layer):
    return pl.BlockSpec((nb, 1, 2, RW_HEADS, HEAD_DIM, HEAD_DIM), lambda b: (b, layer, 0, 0, 0, 0))


def rwkv7_mixer_pallas(p_rw, s0, l_in, prm, st_all=None, l_out=0):
    B, T, W = p_rw.shape
    NB = max(2, RW_ROWS // T)
    row = lambda a: a.reshape(1, RW_WIDTH)
    full = lambda shape: pl.BlockSpec(shape, lambda b: (0,) * len(shape))
    st_shape = (B, 1, 2, RW_HEADS, HEAD_DIM, HEAD_DIM) if st_all is None else st_all.shape
    return pl.pallas_call(
        partial(_rwkv_kernel, T=T, NB=NB),
        grid=(B // NB,),
        in_specs=[pl.BlockSpec((NB, T, W), lambda b: (b, 0, 0)), _state_spec(NB, l_in),
                  full((2, RW_WIDTH)), full((2, 64, RW_WIDTH)), full((2, RW_WIDTH)), full((2, 64, RW_WIDTH)),
                  full((128, RW_WIDTH)), full((1, RW_WIDTH)), full((1, RW_WIDTH)), full((1, RW_WIDTH)),
                  full((1, RW_WIDTH)), full((1, RW_WIDTH))]
        + ([] if st_all is None else [pl.BlockSpec(memory_space=pl.ANY)]),
        out_specs=[pl.BlockSpec((NB, T, RW_WIDTH), lambda b: (b, 0, 0)), _state_spec(NB, l_out)],
        out_shape=[jax.ShapeDtypeStruct((B, T, RW_WIDTH), F32), jax.ShapeDtypeStruct(st_shape, F32)],
        input_output_aliases={} if st_all is None else {12: 1},
        scratch_shapes=[pltpu.VMEM((NB, 2, T, RW_WIDTH), F32)] * 4
        + [pltpu.VMEM((NB, T, RW_WIDTH), F32), pltpu.VMEM((NB, 2, RW_HEADS, HEAD_DIM, HEAD_DIM), F32)],
        compiler_params=pltpu.CompilerParams(dimension_semantics=("arbitrary",), vmem_limit_bytes=VMEM_LIMIT_RWKV),
        name="rwkv7_mixer",
    )(p_rw, s0, prm['rw_w0'], prm['rw_w2'], prm['rw_a0'], prm['rw_a2'], prm['rw_g2'], row(prm['rw_kk']),
      row(prm['rw_ka']), row(prm['rw_rk']), row(prm['rw_gn_g']), row(prm['rw_gn_b']),
      *([] if st_all is None else [st_all]))


HG_SUB = 16
HG_ROWS = 256


def _hgrn_kernel(p_ref, s0_ref, lb_ref, ng_ref, *rest, T):
    out_ref, st_ref, lf_scr, kf_scr, o_scr, s_scr = rest[-6:]
    R, c = HG_ROWS, HG_SUB
    n_it = T // R
    x = p_ref[0]
    bd = _head_blockdiag(HG_WIDTH)
    seg = lambda t: _mm_exact_rhs(t, bd, n=2)
    for d in range(2):
        lbd = lb_ref[d:d + 1, :]
        f = lbd + (1.0 - lbd) * _sigmoid(x[:, 256 + 256 * d:512 + 256 * d])
        lf_scr[d] = jnp.log(jnp.maximum(f, HG_F_MIN))
        kf_scr[d] = 1.0 - f
        for h in range(HG_HEADS):
            s_scr[d, h] = s0_ref[0, 0, d, h].T

    ti = lax.broadcasted_iota(jnp.int32, (R, R), 0)
    si = lax.broadcasted_iota(jnp.int32, (R, R), 1)
    same_blk = (ti // c) == (si // c)
    t16 = lax.broadcasted_iota(jnp.int32, (c, 1), 0)

    def body(i, carry):
        for d in range(2):
            ci = i if d == 0 else n_it - 1 - i
            rows = pl.ds(pl.multiple_of(ci * R, R), R)
            incl = (ti >= si) if d == 0 else (ti <= si)
            lf = lf_scr[d, rows, :]
            cum = _mm_exact_lhs((incl & same_blk).astype(F32), lf)
            tot = _mm_exact_lhs(same_blk.astype(F32), lf)
            xq = p_ref[0, rows, 0:256]
            q = xq * _sigmoid(xq)
            v = p_ref[0, rows, 768:1024]
            kf = kf_scr[d, rows, :]
            Qp = q * jnp.exp(cum)
            Kh = kf * jnp.exp(tot - cum)
            e_tot = jnp.exp(tot)
            blocks = range(R // c) if d == 0 else range(R // c - 1, -1, -1)
            ST = [s_scr[d, h] for h in range(HG_HEADS)]
            o_parts = [None] * (R // c)
            for j in blocks:
                rs = slice(j * c, (j + 1) * c)
                cb, qb, kb, vb = cum[rs], q[rs], kf[rs], v[rs]
                half = c // 2
                spans = []
                for s in range(c):
                    if d == 0:
                        spans.append((half, c) if s >= half else (0, c))
                    else:
                        spans.append((0, half) if s < half else (0, c))
                prods = []
                for s, (lo, hi) in enumerate(spans):
                    e = jnp.exp(jnp.minimum(cb[lo:hi] - cb[s:s + 1, :], 0.0))
                    prods.append(qb[lo:hi] * (kb[s:s + 1, :] * e))
                att = _mm_exact_rhs(jnp.concatenate(prods, axis=0), bd, n=1)
                o_half = [jnp.zeros((half, HG_WIDTH), F32), jnp.zeros((half, HG_WIDTH), F32)]
                off = 0
                for s, (lo, hi) in enumerate(spans):
                    keep = (t16[lo:hi] >= s) if d == 0 else (t16[lo:hi] <= s)
                    term = jnp.where(keep, att[off:off + hi - lo], 0.0) * vb[s:s + 1, :]
                    off += hi - lo
                    for p in range(2):
                        a, b = max(lo, p * half), min(hi, (p + 1) * half)
                        if a < b:
                            o_half[p] = o_half[p] + term[a - lo:b - lo]
                o_blk = jnp.concatenate(o_half, axis=0)
                o_heads = []
                for h in range(HG_HEADS):
                    hs = slice(h * HEAD_DIM, (h + 1) * HEAD_DIM)
                    o_heads.append(_mm(Qp[rs, hs], ST[h], _NT))
                    ST[h] = ST[h] * e_tot[j * c:j * c + 1, hs] + _mm(vb[:, hs], Kh[rs, hs], _TN)
                o_parts[j] = o_blk + jnp.concatenate(o_heads, axis=1)
            for h in range(HG_HEADS):
                s_scr[d, h] = ST[h]
            o_scr[d, rows, :] = jnp.concatenate(o_parts, axis=0)
        return carry

    lax.fori_loop(0, n_it, body, 0)

    o = o_scr[0] + o_scr[1]
    o = o * lax.rsqrt(seg(o * o) * (1.0 / HEAD_DIM) + NORM_EPS) * ng_ref[...]
    gg = x[:, 1024:1280]
    out_ref[0] = o * (gg * _sigmoid(gg))
    for d in range(2):
        for h in range(HG_HEADS):
            st_ref[0, 0, d, h] = s_scr[d, h].T


def hgrn2_mixer_pallas(p_hg, s0, l_in, lb, norm_g, st_all=None, l_out=0):
    B, T, W = p_hg.shape
    full = lambda shape: pl.BlockSpec(shape, lambda b: (0,) * len(shape))
    st_shape = (B, 1, 2, HG_HEADS, HEAD_DIM, HEAD_DIM) if st_all is None else st_all.shape
    return pl.pallas_call(
        partial(_hgrn_kernel, T=T),
        grid=(B,),
        in_specs=[pl.BlockSpec((1, T, W), lambda b: (b, 0, 0)), _state_spec(1, l_in), full((2, HG_WIDTH)),
                  full((1, HG_WIDTH))] + ([] if st_all is None else [pl.BlockSpec(memory_space=pl.ANY)]),
        out_specs=[pl.BlockSpec((1, T, HG_WIDTH), lambda b: (b, 0, 0)), _state_spec(1, l_out)],
        out_shape=[jax.ShapeDtypeStruct((B, T, HG_WIDTH), F32), jax.ShapeDtypeStruct(st_shape, F32)],
        input_output_aliases={} if st_all is None else {4: 1},
        scratch_shapes=[pltpu.VMEM((2, T, HG_WIDTH), F32)] * 3
        + [pltpu.VMEM((2, HG_HEADS, HEAD_DIM, HEAD_DIM), F32)],
        compiler_params=pltpu.CompilerParams(dimension_semantics=("arbitrary",), vmem_limit_bytes=VMEM_LIMIT),
        name="hgrn2_mixer",
    )(p_hg, s0, lb, jnp.tile(norm_g.reshape(1, HEAD_DIM), (1, HG_HEADS)), *([] if st_all is None else [st_all]))


ATT_REP = ATT_HEADS // ATT_KV_HEADS
ATT_QROWS = 128


def _swap_pairs(x):
    w = x.shape[-1]
    lane = lax.broadcasted_iota(jnp.int32, x.shape, x.ndim - 1)
    return jnp.where(lane % 2 == 0, pltpu.roll(x, w - 1, x.ndim - 1), pltpu.roll(x, 1, x.ndim - 1))


def _att_kernel(*refs, T, past, rope):
    if rope:
        p_ref, qg_ref, kg_ref, cos_ref, sin_ref, ck_ref, cv_ref, out_ref, k_scr, v_scr, q_scr = refs
    else:
        p_ref, qg_ref, kg_ref, _, _, out_ref, kh_ref, vh_ref, k_scr, v_scr, q_scr = refs
    x = p_ref[0]
    q = x[:, 0:ATT_WIDTH]
    k = x[:, ATT_WIDTH:ATT_WIDTH + KV_WIDTH]
    v = x[:, ATT_WIDTH + KV_WIDTH:ATT_WIDTH + 2 * KV_WIDTH]
    inv_d = 1.0 / HEAD_DIM
    q = q * lax.rsqrt(_mm_exact_rhs(q * q, _head_blockdiag(ATT_WIDTH), n=2) * inv_d + NORM_EPS) * qg_ref[...]
    k = k * lax.rsqrt(_mm_exact_rhs(k * k, _head_blockdiag(KV_WIDTH), n=2) * inv_d + NORM_EPS) * kg_ref[...]
    if rope:
        cos, sin = cos_ref[...], sin_ref[...]
        rep = ATT_WIDTH // KV_WIDTH
        q = q * jnp.concatenate([cos] * rep, axis=1) + _swap_pairs(q) * jnp.concatenate([sin] * rep, axis=1)
        k = k * cos + _swap_pairs(k) * sin
    q_scr[...] = (q * (1.0 / math.sqrt(HEAD_DIM))).astype(BF16)
    for g in range(ATT_KV_HEADS):
        gs = slice(g * HEAD_DIM, (g + 1) * HEAD_DIM)
        if rope:
            k_scr[g, 0:past, :] = ck_ref[0, 0, g].astype(BF16)
            v_scr[g, 0:past, :] = cv_ref[0, 0, g].astype(BF16)
        else:
            kh_ref[0, 0, g] = k[:, gs].T
            vh_ref[0, 0, g] = v[:, gs].T
        k_scr[g, past:past + T, :] = k[:, gs].astype(BF16)
        v_scr[g, past:past + T, :] = v[:, gs].astype(BF16)
    QR = ATT_QROWS

    def q_block(qb, carry):
        rows = pl.ds(pl.multiple_of(qb * QR, QR), QR)
        qblk = q_scr[rows, :]
        for g in range(ATT_KV_HEADS):
            qs = jnp.concatenate([qblk[:, (g * ATT_REP + r) * HEAD_DIM:(g * ATT_REP + r + 1) * HEAD_DIM]
                                  for r in range(ATT_REP)], axis=0)
            s = lax.dot_general(qs, k_scr[g], _NT, preferred_element_type=F32)
            e = jnp.exp(s - jnp.max(s, axis=-1, keepdims=True))
            l = jnp.sum(e, axis=-1, keepdims=True)
            o = lax.dot_general(e.astype(BF16), v_scr[g], _NN, preferred_element_type=F32) / l
            for r in range(ATT_REP):
                h = g * ATT_REP + r
                out_ref[0, rows, h * HEAD_DIM:(h + 1) * HEAD_DIM] = o[r * QR:(r + 1) * QR]
        return carry

    lax.fori_loop(0, T // QR, q_block, 0, unroll=4 if T // QR >= 4 else 2)


def rope_tables(T):
    rows = T // GRID_W
    row = jnp.repeat(jnp.arange(rows, dtype=F32), GRID_W)
    col = jnp.tile(jnp.arange(GRID_W, dtype=F32), rows)
    n_freq = HEAD_DIM // 4
    inv = ROPE_THETA ** (-jnp.arange(n_freq, dtype=F32) / n_freq)
    ang = jnp.concatenate([row[:, None] * inv, col[:, None] * inv], axis=-1)
    cos = jnp.repeat(jnp.cos(ang), 2, axis=-1)
    sin = jnp.stack([-jnp.sin(ang), jnp.sin(ang)], axis=-1).reshape(T, HEAD_DIM)
    return jnp.tile(cos, (1, ATT_KV_HEADS)), jnp.tile(sin, (1, ATT_KV_HEADS))


def attention_pallas(p_att, qnorm_g, knorm_g, l, cache=None, kv_all=None):
    B, T, W = p_att.shape
    rope = cache is not None
    past = cache[0].shape[3] if rope else 0
    full = lambda shape: pl.BlockSpec(shape, lambda b: (0,) * len(shape))
    qg = jnp.tile(qnorm_g.reshape(1, HEAD_DIM), (1, ATT_HEADS))
    kg = jnp.tile(knorm_g.reshape(1, HEAD_DIM), (1, ATT_KV_HEADS))
    in_specs = [pl.BlockSpec((1, T, W), lambda b: (b, 0, 0)), full((1, ATT_WIDTH)), full((1, KV_WIDTH))]
    args = [p_att, qg, kg]
    out_specs = [pl.BlockSpec((1, T, ATT_WIDTH), lambda b: (b, 0, 0))]
    out_shape = [jax.ShapeDtypeStruct((B, T, ATT_WIDTH), F32)]
    if rope:
        cos, sin = rope_tables(T)
        kv_spec = pl.BlockSpec((1, 1, ATT_KV_HEADS, past, HEAD_DIM), lambda b: (b, l, 0, 0, 0))
        in_specs += [full((T, KV_WIDTH)), full((T, KV_WIDTH)), kv_spec, kv_spec]
        args += [cos, sin, cache[0], cache[1]]
        aliases = {}
    else:
        kv_spec = pl.BlockSpec((1, 1, ATT_KV_HEADS, HEAD_DIM, T), lambda b: (b, l, 0, 0, 0))
        in_specs += [pl.BlockSpec(memory_space=pl.ANY)] * 2
        args += list(kv_all)
        aliases = {3: 1, 4: 2}
        out_specs += [kv_spec, kv_spec]
        out_shape += [jax.ShapeDtypeStruct(kv_all[0].shape, F32)] * 2
    res = pl.pallas_call(
        partial(_att_kernel, T=T, past=past, rope=rope),
        grid=(B,),
        in_specs=in_specs,
        out_specs=out_specs,
        out_shape=out_shape,
        input_output_aliases=aliases,
        scratch_shapes=[pltpu.VMEM((ATT_KV_HEADS, past + T, HEAD_DIM), BF16)] * 2
        + [pltpu.VMEM((T, ATT_WIDTH), BF16)],
        compiler_params=pltpu.CompilerParams(dimension_semantics=("arbitrary",), vmem_limit_bytes=VMEM_LIMIT),
        name="attention_rope" if rope else "attention_ctx",
    )(*args)
    return res[0] if rope else tuple(res)


ROW_TILE = 512
ROW_SUB = 256
MOD_TILE = 1536
ROUTE_TILE = 256
MOE_SLOTS = 256
SC_ROWS = 64
SC_LANES = 128
P_ATT, P_RW, P_HG = ATT_WIDTH + 2 * KV_WIDTH, 3 * RW_WIDTH + 384, 5 * HG_WIDTH


def _mod_kernel(c_ref, w_ref, b_ref, o_ref):
    c = c_ref[...]
    o_ref[0] = _mm(c * _sigmoid(c), w_ref[0], passes=3) + b_ref[0]


def adaln_mod_pallas(cvec, w_mod, b_mod):
    n = 6 * D_MODEL
    return pl.pallas_call(
        _mod_kernel,
        grid=(DEPTH, n // MOD_TILE),
        in_specs=[pl.BlockSpec((8, D_MODEL), lambda l, j: (0, 0)),
                  pl.BlockSpec((1, D_MODEL, MOD_TILE), lambda l, j: (l, 0, j)),
                  pl.BlockSpec((1, 1, MOD_TILE), lambda l, j: (l, 0, j))],
        out_specs=pl.BlockSpec((1, 8, MOD_TILE), lambda l, j: (l, 0, j)),
        out_shape=jax.ShapeDtypeStruct((DEPTH, 8, n), F32),
        compiler_params=pltpu.CompilerParams(dimension_semantics=("arbitrary", "arbitrary"),
                                             vmem_limit_bytes=VMEM_LIMIT),
        name="adaln_mod",
    )(cvec, w_mod, b_mod.reshape(DEPTH, 1, n))


def _rms(x):
    return x * lax.rsqrt(jnp.mean(x * x, axis=-1, keepdims=True) + NORM_EPS)


HALF = D_MODEL // 2
U32 = jnp.uint32


def _pack_rows(x):
    bits = lax.bitcast_convert_type(x.astype(BF16).astype(F32), U32)
    return (bits[:, :HALF] >> 16) | bits[:, HALF:]


def _unpack_rows(w):
    lo = lax.bitcast_convert_type(w << 16, F32)
    hi = lax.bitcast_convert_type(w & jnp.uint32(0xFFFF0000), F32)
    return lo, hi


def _row_pieces():
    return [slice(r, r + ROW_SUB) for r in range(0, ROW_TILE, ROW_SUB)]


def _moe_residual(x_ref, g_ref, g4_ref, pm_ref, rows):
    y = None
    for k in range(TOP_K):
        t = g4_ref[rows, k:k + 1] * jnp.concatenate(_unpack_rows(g_ref[k, rows, :]), axis=1)
        y = t if y is None else y + t
    return x_ref[rows, :] + pm_ref[0, 5:6, :] * y


def _in_kernel(*refs, has_res):
    if has_res:
        x_ref, gth_ref, g4_ref, pm_ref, m_ref, g_ref, w_ref, xo_ref, pa_ref, pr_ref, ph_ref = refs
    else:
        x_ref, m_ref, g_ref, w_ref, pa_ref, pr_ref, ph_ref = refs
    for rows in _row_pieces():
        if has_res:
            x = _moe_residual(x_ref, gth_ref, g4_ref, pm_ref, rows)
            xo_ref[rows, :] = x
        else:
            x = x_ref[rows, :]
        h = _rms(x) * g_ref[...] * (1.0 + m_ref[0, 1:2, :]) + m_ref[0, 0:1, :]
        proj = lax.dot_general(h, w_ref[0], _NN, precision=lax.Precision.DEFAULT, preferred_element_type=F32)
        pa_ref[rows, :] = proj[:, 0:P_ATT]
        pr_ref[rows, :] = proj[:, P_ATT:P_ATT + P_RW]
        ph_ref[rows, :] = proj[:, P_ATT + P_RW:]


def _res_specs(row0):
    t0 = row0 // ROW_TILE
    return [pl.BlockSpec((TOP_K, ROW_TILE, HALF), lambda i: (0, i, 0)),
            pl.BlockSpec((ROW_TILE, TOP_K), lambda i: (i + t0, 0))]


def in_proj_pallas(x, mod, norm_g, w_in, l, rows_per_mod, res=None):
    R = x.shape[0]
    tpm = rows_per_mod // ROW_TILE
    rt = lambda w: pl.BlockSpec((ROW_TILE, w), lambda i: (i, 0))
    ms = pl.BlockSpec((1, 6, D_MODEL), lambda i: (i // tpm, 0, 0))
    full = lambda shape, **kw: pl.BlockSpec(shape, lambda i: (0,) * len(shape), **kw)
    in_specs = [rt(D_MODEL)] + (_res_specs(res[2]) + [ms] if res else []) + [
        ms, full((1, D_MODEL)),
        pl.BlockSpec((1, D_MODEL, w_in.shape[2]), lambda i: (l, 0, 0), pipeline_mode=pl.Buffered(1))]
    args = [x] + ([res[0], res[1], res[3]] if res else []) + [mod, norm_g.reshape(1, D_MODEL), w_in]
    widths = ([D_MODEL] if res else []) + [P_ATT, P_RW, P_HG]
    return pl.pallas_call(
        partial(_in_kernel, has_res=res is not None),
        grid=(R // ROW_TILE,),
        in_specs=in_specs,
        out_specs=[rt(w) for w in widths],
        out_shape=[jax.ShapeDtypeStruct((R, w), F32) for w in widths],
        compiler_params=pltpu.CompilerParams(dimension_semantics=("arbitrary",), vmem_limit_bytes=VMEM_LIMIT),
        name="in_proj",
    )(*args)


def _out_kernel(att_ref, rw_ref, hg_ref, x_ref, m_ref, g_ref, w_ref, rw_w_ref, rb_ref, *rest):
    xo_ref, h_ref, lg_ref = rest[-3:]
    d = lambda a, lo, hi: lax.dot_general(a, w_ref[0, lo:hi, :], _NN, precision=lax.Precision.DEFAULT,
                                          preferred_element_type=F32)
    for rows in _row_pieces():
        mixo = (d(att_ref[rows, :], 0, ATT_WIDTH) + d(rw_ref[rows, :], ATT_WIDTH, ATT_WIDTH + RW_WIDTH)
                + d(hg_ref[rows, :], ATT_WIDTH + RW_WIDTH, ATT_WIDTH + RW_WIDTH + HG_WIDTH))
        x = x_ref[rows, :] + m_ref[0, 2:3, :] * mixo
        xo_ref[rows, :] = x
        h = _rms(x) * g_ref[...] * (1.0 + m_ref[0, 4:5, :]) + m_ref[0, 3:4, :]
        h_ref[rows, :] = _pack_rows(h)
        lg_ref[rows, :] = _mm(h, rw_w_ref[...], passes=3) + rb_ref[...]


def out_proj_pallas(att, rw, hg, x, mod, norm_g, w_out, l, router_w, router_b, rows_per_mod, n_all, row0, joint=None):
    R = x.shape[0]
    tpm = rows_per_mod // ROW_TILE
    t0 = row0 // ROW_TILE
    rt = lambda w: pl.BlockSpec((ROW_TILE, w), lambda i: (i, 0))
    jt = lambda w: pl.BlockSpec((ROW_TILE, w), lambda i: (i + t0, 0))
    full = lambda shape: pl.BlockSpec(shape, lambda i: (0,) * len(shape))
    in_specs = [rt(ATT_WIDTH), rt(RW_WIDTH), rt(HG_WIDTH), rt(D_MODEL),
                pl.BlockSpec((1, 6, D_MODEL), lambda i: (i // tpm, 0, 0)), full((1, D_MODEL)),
                pl.BlockSpec((1, D_MODEL, D_MODEL), lambda i: (l, 0, 0)), full((D_MODEL, N_EXPERTS)),
                full((1, N_EXPERTS))]
    args = [att, rw, hg, x, mod, norm_g.reshape(1, D_MODEL), w_out, router_w, router_b.reshape(1, N_EXPERTS)]
    aliases = {}
    if joint is not None:
        in_specs += [pl.BlockSpec(memory_space=pl.ANY)] * 2
        aliases = {len(args): 1, len(args) + 1: 2}
        args += list(joint)
    return pl.pallas_call(
        _out_kernel,
        grid=(R // ROW_TILE,),
        in_specs=in_specs,
        out_specs=[rt(D_MODEL), jt(HALF), jt(N_EXPERTS)],
        out_shape=[jax.ShapeDtypeStruct((R, D_MODEL), F32), jax.ShapeDtypeStruct((n_all, HALF), U32),
                   jax.ShapeDtypeStruct((n_all, N_EXPERTS), F32)],
        input_output_aliases=aliases,
        compiler_params=pltpu.CompilerParams(dimension_semantics=("arbitrary",), vmem_limit_bytes=VMEM_LIMIT),
        name="out_proj",
    )(*args)


def _final_kernel(x_ref, gth_ref, g4_ref, m_ref, g_ref, o_ref):
    for rows in _row_pieces():
        o_ref[rows, :] = _rms(_moe_residual(x_ref, gth_ref, g4_ref, m_ref, rows)) * g_ref[...]


def final_norm_pallas(x, gathered, gate4, row0, mod, norm_g, rows_per_mod):
    R = x.shape[0]
    tpm = rows_per_mod // ROW_TILE
    rt = pl.BlockSpec((ROW_TILE, D_MODEL), lambda i: (i, 0))
    return pl.pallas_call(
        _final_kernel,
        grid=(R // ROW_TILE,),
        in_specs=[rt] + _res_specs(row0) + [pl.BlockSpec((1, 6, D_MODEL), lambda i: (i // tpm, 0, 0)),
                                            pl.BlockSpec((1, D_MODEL), lambda i: (0, 0))],
        out_specs=rt,
        out_shape=jax.ShapeDtypeStruct((R, D_MODEL), F32),
        name="final_norm",
    )(x, gathered, gate4, mod, norm_g.reshape(1, D_MODEL))


def _moe_max_blocks(n_tok):
    return (n_tok * TOP_K + N_EXPERTS * (MOE_SLOTS - 1)) // MOE_SLOTS


def _route_kernel(lg_ref, dest_ref, gate4_ref, blk_ref, rank_scr, gate_scr, *, n_tok):
    Rt, E = ROUTE_TILE, N_EXPERTS
    n_tiles = n_tok // Rt
    ti = lax.broadcasted_iota(jnp.int32, (Rt, Rt), 0)
    si = lax.broadcasted_iota(jnp.int32, (Rt, Rt), 1)
    before_t = (ti < si).astype(BF16)
    eye_t = (ti == si).astype(BF16)
    ei = lax.broadcasted_iota(jnp.int32, (E, E), 0)
    ej = lax.broadcasted_iota(jnp.int32, (E, E), 1)
    before_e = (ei > ej).astype(BF16)
    sub = lax.broadcasted_iota(jnp.int32, (E, Rt), 0)
    d = lambda a, b, dims: lax.dot_general(a, b, dims, preferred_element_type=F32)

    def tile_members(it, off):
        rows = pl.ds(pl.multiple_of(it * Rt, Rt), Rt)
        l3, l2, l1 = reversed(_split(lg_ref[rows, :], 3))
        lgT = d(l3, eye_t, _TN) + d(l2, eye_t, _TN) + d(l1, eye_t, _TN)
        work = lgT
        member = jnp.zeros((E, Rt), jnp.bool_)
        top = None
        for k in range(TOP_K):
            m = jnp.max(work, axis=0, keepdims=True)
            if top is None:
                top = m
            first = jnp.min(jnp.where(work == m, sub, E), axis=0, keepdims=True)
            pick = sub == first
            member = member | pick
            work = jnp.where(pick, -jnp.inf, work)
        ex = jnp.where(member, jnp.exp(lgT - top), 0.0)
        gate_scr[:, rows] = ex / jnp.sum(ex, axis=0, keepdims=True)
        mem = member.astype(BF16)
        rank = d(mem, before_t, _NN) + off
        rank_scr[:, rows] = jnp.where(member, rank, -1.0)
        return off + jnp.sum(mem.astype(F32), axis=1, keepdims=True)

    count = lax.fori_loop(0, n_tiles, tile_members, jnp.zeros((E, 1), F32), unroll=4)
    nblk = jnp.floor((count + (MOE_SLOTS - 1)) * (1.0 / MOE_SLOTS))
    bstart = d(before_e, jnp.broadcast_to(nblk, (E, 128)).astype(BF16), _NN)[:, 0:1]
    bend = bstart + nblk
    pstart = bstart * MOE_SLOTS

    def tile_slots(it, carry):
        rows = pl.ds(pl.multiple_of(it * Rt, Rt), Rt)
        rank = rank_scr[:, rows]
        gate = gate_scr[:, rows]
        member = rank >= 0.0
        kidx = d(before_e, member.astype(BF16), _NN)
        slot = pstart + rank
        grows = []
        for k in range(TOP_K):
            sel = member & (kidx == k)
            dest_ref[k:k + 1, rows] = jnp.sum(jnp.where(sel, slot, 0.0), axis=0, keepdims=True).astype(jnp.int32)
            grows.append(jnp.sum(jnp.where(sel, gate, 0.0), axis=0, keepdims=True))
        g4t = jnp.concatenate(grows + [jnp.zeros((128 - TOP_K, Rt), F32)], axis=0)
        g3, g2, g1 = reversed(_split(g4t, 3))
        g4 = d(eye_t, g3, _NT) + d(eye_t, g2, _NT) + d(eye_t, g1, _NT)
        gate4_ref[rows, :] = g4[:, 0:TOP_K]
        return carry

    lax.fori_loop(0, n_tiles, tile_slots, 0, unroll=4)
    nb = blk_ref.shape[1]
    bi = lax.broadcasted_iota(jnp.int32, (E, nb), 1).astype(F32)
    owner = jnp.sum((bend <= bi).astype(F32), axis=0, keepdims=True)
    blk_ref[0:1, :] = jnp.minimum(owner, E - 1.0).astype(jnp.int32)
    blk_ref[1:2, :] = jnp.broadcast_to(jnp.sum(nblk, axis=0, keepdims=True), (1, nb)).astype(jnp.int32)
    blk_ref[2:8, :] = jnp.zeros((6, nb), jnp.int32)


def moe_route_pallas(logits):
    n_tok = logits.shape[0]
    nb = -(-_moe_max_blocks(n_tok) // 128) * 128
    return pl.pallas_call(
        partial(_route_kernel, n_tok=n_tok),
        out_shape=[jax.ShapeDtypeStruct((TOP_K, n_tok), jnp.int32),
                   jax.ShapeDtypeStruct((n_tok, TOP_K), F32),
                   jax.ShapeDtypeStruct((8, nb), jnp.int32)],
        scratch_shapes=[pltpu.VMEM((N_EXPERTS, n_tok), F32)] * 2,
        name="moe_route",
    )(logits)


def _moe_block_kernel(be_ref, nu_ref, first_ref, par_ref, nxt_ref, xb_ref, wgu_hbm, bgu_ref, wdn_hbm, bdn_ref, yb_ref,
                      wgu_buf, wdn_buf, sem, *, l):
    i = pl.program_id(0)

    def weight_copies(e, slot):
        return (pltpu.make_async_copy(wgu_hbm.at[l, e], wgu_buf.at[slot], sem.at[0, slot]),
                pltpu.make_async_copy(wdn_hbm.at[l, e], wdn_buf.at[slot], sem.at[1, slot]))

    @pl.when(i < nu_ref[0])
    def _():
        slot = par_ref[i]

        @pl.when(first_ref[i] == 1)
        def _():
            @pl.when(i == 0)
            def _():
                for cp in weight_copies(be_ref[0], slot):
                    cp.start()

            for cp in weight_copies(be_ref[i], slot):
                cp.wait()

            @pl.when(nxt_ref[i] >= 0)
            def _():
                for cp in weight_copies(nxt_ref[i], 1 - slot):
                    cp.start()

        dot = lambda a, w: lax.dot_general(a, w, _NN, precision=lax.Precision.DEFAULT, preferred_element_type=F32)
        x_lo, x_hi = _unpack_rows(xb_ref[...])
        gu = dot(x_lo, wgu_buf[slot, 0:HALF]) + dot(x_hi, wgu_buf[slot, HALF:D_MODEL]) + bgu_ref[0, 0]
        glu = jnp.minimum(gu[:, :EXPERT_FF], SWIGLU_LIMIT)
        lin = jnp.clip(gu[:, EXPERT_FF:], -SWIGLU_LIMIT, SWIGLU_LIMIT)
        act = glu * _sigmoid(SWIGLU_ALPHA * glu) * (lin + 1.0)
        yb_ref[...] = _pack_rows(dot(act, wdn_buf[slot]) + bdn_ref[0, 0])


def _expert_runs(block_e, n_used):
    n = block_e.shape[0]
    idx = jnp.arange(n, dtype=jnp.int32)
    valid = idx < n_used[0]
    first = valid & ((idx == 0) | (block_e != jnp.roll(block_e, 1)))
    par = (jnp.cumsum(first.astype(jnp.int32)) - 1) % 2
    start = jnp.where(first, idx, n)
    nxt_start = lax.cummin(jnp.concatenate([start[1:], jnp.full((1,), n, jnp.int32)]), reverse=True)
    nxt = jnp.where(nxt_start < n, block_e[jnp.minimum(nxt_start, n - 1)], -1)
    return first.astype(jnp.int32), par.astype(jnp.int32), nxt.astype(jnp.int32)


def moe_blocks_pallas(xb, block_e, n_used, l, w_gu, b_gu, w_down, b_down):
    n_blocks = xb.shape[0] // MOE_SLOTS
    first, par, nxt = _expert_runs(block_e, n_used)
    blk = lambda i, be, nu, *_: (jnp.minimum(i, nu[0] - 1), 0)
    bsel = lambda i, be, nu, *_: (l, be[jnp.minimum(i, nu[0] - 1)], 0, 0)
    grid_spec = pltpu.PrefetchScalarGridSpec(
        num_scalar_prefetch=5,
        grid=(n_blocks,),
        in_specs=[pl.BlockSpec((MOE_SLOTS, HALF), blk),
                  pl.BlockSpec(memory_space=pl.ANY),
                  pl.BlockSpec((1, 1, 1, 2 * EXPERT_FF), bsel),
                  pl.BlockSpec(memory_space=pl.ANY),
                  pl.BlockSpec((1, 1, 1, D_MODEL), bsel)],
        out_specs=pl.BlockSpec((MOE_SLOTS, HALF), blk),
        scratch_shapes=[pltpu.VMEM((2, D_MODEL, 2 * EXPERT_FF), F32), pltpu.VMEM((2, EXPERT_FF, D_MODEL), F32),
                        pltpu.SemaphoreType.DMA((2, 2))],
    )
    return pl.pallas_call(
        partial(_moe_block_kernel, l=l),
        grid_spec=grid_spec,
        out_shape=jax.ShapeDtypeStruct(xb.shape, U32),
        compiler_params=pltpu.CompilerParams(dimension_semantics=("arbitrary",), vmem_limit_bytes=VMEM_LIMIT),
        name="moe_blocks",
    )(block_e, n_used, first, par, nxt, xb, w_gu, b_gu.reshape(DEPTH, N_EXPERTS, 1, 2 * EXPERT_FF), w_down,
      b_down.reshape(DEPTH, N_EXPERTS, 1, D_MODEL))


def _sc_mesh():
    return plsc.VectorSubcoreMesh(core_axis_name="c", subcore_axis_name="s")


def _sc_index_rows(idx):
    return jnp.pad(idx.reshape(-1, SC_ROWS), ((0, 0), (0, SC_LANES - SC_ROWS)))


def sc_dispatch(h, dest, n_rows):
    n_tok, d = h.shape
    idx = [_sc_index_rows(dest[k]) for k in range(TOP_K)]

    @pl.kernel(out_type=jax.ShapeDtypeStruct((n_rows, d), h.dtype), mesh=_sc_mesh(), scratch_types=[])
    def kern(h_hbm, i0, i1, i2, i3, o_hbm):
        def body(x_vmem, *i_vmem):
            for iv in i_vmem:
                pltpu.sync_copy(x_vmem, o_hbm.at[iv.at[0, pl.ds(0, SC_ROWS)]])

        pltpu.emit_pipeline(
            body,
            grid=(n_tok // SC_ROWS,),
            in_specs=[pl.BlockSpec((SC_ROWS, d), lambda i: (i, 0))]
            + [pl.BlockSpec((1, SC_LANES), lambda i: (i, 0))] * TOP_K,
            out_specs=[],
            core_axis_name=("c", "s"),
            dimension_semantics=(pltpu.PARALLEL,),
        )(h_hbm, i0, i1, i2, i3)

    return kern(h, *idx)


def sc_combine_gather(yb, dest):
    n_tok = dest.shape[1]
    d = yb.shape[1]
    idx = _sc_index_rows(dest.reshape(TOP_K * n_tok))

    @pl.kernel(out_type=jax.ShapeDtypeStruct((TOP_K * n_tok, d), yb.dtype), mesh=_sc_mesh(), scratch_types=[])
    def kern(y_hbm, i_hbm, o_hbm):
        def body(i_vmem, o_vmem):
            pltpu.sync_copy(y_hbm.at[i_vmem.at[0, pl.ds(0, SC_ROWS)]], o_vmem)

        pltpu.emit_pipeline(
            body,
            grid=(TOP_K * n_tok // SC_ROWS,),
            in_specs=[pl.BlockSpec((1, SC_LANES), lambda i: (i, 0))],
            out_specs=[pl.BlockSpec((SC_ROWS, d), lambda i: (i, 0))],
            core_axis_name=("c", "s"),
            dimension_semantics=(pltpu.PARALLEL,),
        )(i_hbm, o_hbm)

    return kern(yb, idx).reshape(TOP_K, n_tok, d)


def hgrn_lower_bounds(hg_lb):
    sm = jax.nn.softmax(hg_lb.astype(jnp.float32), axis=0)
    return jnp.cumsum(sm, axis=0) - sm[0:1]


def kernel(x_prompt, x_sample, cache_att_k, cache_att_v, state_rwkv, state_hgrn, c, c_ctx, w_mod, b_mod, norm_mix_g, norm_ffn_g, w_in, w_out, att_qnorm_g, att_knorm_g, rw_w0, rw_w2, rw_a0, rw_a2, rw_g2, rw_kk, rw_ka, rw_rk, rw_gn_g, rw_gn_b, hg_lb, hg_norm_g, router_w, router_b, moe_w_gu, moe_b_gu, moe_w_down, moe_b_down, final_norm_g):
    BP, TP, _ = x_prompt.shape
    BS, TS, _ = x_sample.shape
    n_p, n_s = BP * TP, BS * TS
    lb_all = hgrn_lower_bounds(hg_lb)
    cvec = jnp.concatenate([c_ctx[None, :], c, jnp.zeros((8 - 1 - BS, D_MODEL), F32)], axis=0)
    mod_all = adaln_mod_pallas(cvec, w_mod, b_mod).reshape(DEPTH, 8, 6, D_MODEL)
    zeros_state = jnp.zeros((BP, 1, 2, RW_HEADS, HEAD_DIM, HEAD_DIM), F32)
    kv_all = (jnp.zeros((BP, DEPTH, ATT_KV_HEADS, HEAD_DIM, TP), F32),) * 2
    rw_states = jnp.zeros((BP, DEPTH, 2, RW_HEADS, HEAD_DIM, HEAD_DIM), F32)
    hg_states = jnp.zeros((BP, DEPTH, 2, HG_HEADS, HEAD_DIM, HEAD_DIM), F32)
    x = {'p': x_prompt.reshape(n_p, D_MODEL), 's': x_sample.reshape(n_s, D_MODEL)}
    dims = {'p': (TP, BP, n_p, 0), 's': (TS, BS, TS, n_p)}
    moe_out, mod_prev = None, None
    joint = (jnp.zeros((n_p + n_s, HALF), U32), jnp.zeros((n_p + n_s, N_EXPERTS), F32))
    for l in range(DEPTH):
        prm = dict(rw_w0=rw_w0[l], rw_w2=rw_w2[l], rw_a0=rw_a0[l], rw_a2=rw_a2[l], rw_g2=rw_g2[l],
                   rw_kk=rw_kk[l], rw_ka=rw_ka[l], rw_rk=rw_rk[l], rw_gn_g=rw_gn_g[l], rw_gn_b=rw_gn_b[l])
        mods = {'p': mod_all[l, 0:1], 's': mod_all[l, 1:1 + BS]}
        for s in ('p', 's'):
            T, B, rpm, row0 = dims[s]
            if l == 0:
                p_att, p_rw, p_hg = in_proj_pallas(x[s], mods[s], norm_mix_g[l], w_in, l, rpm)
            else:
                x[s], p_att, p_rw, p_hg = in_proj_pallas(x[s], mods[s], norm_mix_g[l], w_in, l, rpm,
                                                         res=(*moe_out[s], row0, mod_prev[s]))
            p_att, p_rw, p_hg = (t.reshape(B, T, -1) for t in (p_att, p_rw, p_hg))
            if s == 'p':
                att, *kv_all = attention_pallas(p_att, att_qnorm_g[l], att_knorm_g[l], l, kv_all=kv_all)
                rw_out, rw_states = rwkv7_mixer_pallas(p_rw, zeros_state, 0, prm, rw_states, l)
                hg_out, hg_states = hgrn2_mixer_pallas(p_hg, zeros_state, 0, lb_all[l], hg_norm_g[l], hg_states, l)
            else:
                att = attention_pallas(p_att, att_qnorm_g[l], att_knorm_g[l], l, cache=(cache_att_k, cache_att_v))
                rw_out, _ = rwkv7_mixer_pallas(p_rw, state_rwkv, l, prm)
                hg_out, _ = hgrn2_mixer_pallas(p_hg, state_hgrn, l, lb_all[l], hg_norm_g[l])
            x[s], *joint = out_proj_pallas(att.reshape(B * T, -1), rw_out.reshape(B * T, -1),
                                           hg_out.reshape(B * T, -1), x[s], mods[s], norm_ffn_g[l], w_out, l,
                                           router_w[l], router_b[l], rpm, n_p + n_s, row0, joint)
        h_all, logits_all = joint
        dest, gate4, blk = moe_route_pallas(logits_all)
        xb = sc_dispatch(h_all, dest, _moe_max_blocks(n_p + n_s) * MOE_SLOTS)
        yb = moe_blocks_pallas(xb, blk[0], blk[1, :1], l, moe_w_gu, moe_b_gu, moe_w_down, moe_b_down)
        moe_out = {'p': (sc_combine_gather(yb, dest[:, :n_p]), gate4), 's': (sc_combine_gather(yb, dest[:, n_p:]), gate4)}
        mod_prev = mods
    y_prompt = final_norm_pallas(x['p'], *moe_out['p'], 0, mod_prev['p'], final_norm_g, n_p)
    y_sample = final_norm_pallas(x['s'], *moe_out['s'], n_p, mod_prev['s'], final_norm_g, TS)
    return (y_prompt.reshape(x_prompt.shape), y_sample.reshape(x_sample.shape),
            jnp.swapaxes(kv_all[0], 3, 4), jnp.swapaxes(kv_all[1], 3, 4),
            rw_states, hg_states)
```

```python
import math
from functools import partial

import jax
import jax.numpy as jnp
from jax import lax
from jax.experimental import pallas as pl
from jax.experimental.pallas import tpu as pltpu
from jax.experimental.pallas import tpu_sc as plsc

D_MODEL = 1024
DEPTH = 2
GRID_W = 64
HEAD_DIM = 64
ATT_HEADS = 8
ATT_KV_HEADS = 2
ATT_WIDTH = ATT_HEADS * HEAD_DIM
KV_WIDTH = ATT_KV_HEADS * HEAD_DIM
RW_HEADS = 4
RW_WIDTH = RW_HEADS * HEAD_DIM
RW_GN_EPS = 64e-5
HG_HEADS = 4
HG_WIDTH = HG_HEADS * HEAD_DIM
HG_F_MIN = 1e-6
N_EXPERTS = 32
TOP_K = 4
EXPERT_FF = D_MODEL
SWIGLU_LIMIT = 7.0
SWIGLU_ALPHA = 1.702
ROPE_THETA = 10000.0
NORM_EPS = 1e-6

RW_CHUNK = 64
BF16 = jnp.bfloat16
F32 = jnp.float32

V7X_VMEM_BYTES = 64 * 1024 * 1024
VMEM_LIMIT = V7X_VMEM_BYTES * 3 // 4
VMEM_LIMIT_RWKV = V7X_VMEM_BYTES * 7 // 8

_NN = (((1,), (0,)), ((), ()))
_NT = (((1,), (1,)), ((), ()))
_TN = (((0,), (0,)), ((), ()))


def _split(x, n):
    parts = []
    for _ in range(n - 1):
        hi = x.astype(BF16)
        parts.append(hi)
        x = x - hi.astype(F32)
    parts.append(x.astype(BF16))
    return parts


def _mm(a, b, dims=_NN, passes=1):
    d = lambda x, y: lax.dot_general(x, y, dims, preferred_element_type=F32)
    if passes == 1:
        return d(a.astype(BF16), b.astype(BF16))
    ah, al = _split(a, 2)
    bh, bl = _split(b, 2)
    return d(ah, bl) + d(al, bh) + d(ah, bh)


def _mm_exact_lhs(a01, b, n=3):
    a = a01.astype(BF16)
    out = None
    for t in reversed(_split(b, n)):
        y = lax.dot_general(a, t, _NN, preferred_element_type=F32)
        out = y if out is None else out + y
    return out


def _mm_exact_rhs(a, b01, n=3):
    b = b01.astype(BF16)
    out = None
    for t in reversed(_split(a, n)):
        y = lax.dot_general(t, b, _NN, preferred_element_type=F32)
        out = y if out is None else out + y
    return out


def _head_blockdiag(width):
    r = lax.broadcasted_iota(jnp.int32, (width, width), 0) // HEAD_DIM
    c = lax.broadcasted_iota(jnp.int32, (width, width), 1) // HEAD_DIM
    return (r == c).astype(F32)


def _sigmoid(x):
    return 1.0 / (1.0 + jnp.exp(-x))


def _softplus(x):
    return jnp.maximum(x, 0.0) + jnp.log(1.0 + jnp.exp(-jnp.abs(x)))


def _rwkv_kernel(p_ref, s0_ref, w0_ref, w2_ref, a0_ref, a2_ref, g2_ref, kk_ref, ka_ref, rk_ref, gng_ref, gnb_ref,
                 *rest, T, NB):
    out_ref, st_ref, lw_scr, kd_scr, bb_scr, y_scr, kk_scr, s_scr = rest[-8:]
    C = RW_CHUNK
    n_chunks = T // C
    bd = _head_blockdiag(RW_WIDTH)
    seg = lambda t: _mm_exact_rhs(t, bd, n=2)
    ka = ka_ref[...]
    for nb in range(NB):
        k = p_ref[nb, :, 256:512]
        kk = k * kk_ref[...]
        kk = kk * lax.rsqrt(seg(kk * kk) + 1e-12)
        kk_scr[nb] = kk
        for d in range(2):
            wd = p_ref[nb, :, 768 + 64 * d:832 + 64 * d]
            ad = p_ref[nb, :, 896 + 64 * d:960 + 64 * d]
            w_raw = w0_ref[d:d + 1, :] + _mm(jnp.tanh(wd), w2_ref[d])
            lw_scr[nb, d] = -jnp.exp(-_softplus(-w_raw) - 0.5)
            a = _sigmoid(a0_ref[d:d + 1, :] + _mm(ad, a2_ref[d]))
            kd_scr[nb, d] = k * (1.0 + (a - 1.0) * ka)
            bb_scr[nb, d] = kk * a
    s_scr[...] = s0_ref[:, 0]

    ti = lax.broadcasted_iota(jnp.int32, (C, C), 0)
    si = lax.broadcasted_iota(jnp.int32, (C, C), 1)
    ones_cc = jnp.ones((C, C), F32)
    t2 = lax.broadcasted_iota(jnp.int32, (C, 2 * C), 0)
    col2 = lax.broadcasted_iota(jnp.int32, (C, 2 * C), 1)
    right = col2 >= C
    s2 = jnp.where(right, col2 - C, col2)

    def chunk_body(i, carry):
        ch = []
        for nb, d in [(nb, d) for nb in range(NB) for d in range(2)]:
            ci = i if d == 0 else n_chunks - 1 - i
            rows = pl.ds(pl.multiple_of(ci * C, C), C)
            strict = (ti > si) if d == 0 else (ti < si)
            incl = (ti >= si) if d == 0 else (ti <= si)
            lw = lw_scr[nb, d, rows, :]
            cum = _mm_exact_lhs(incl.astype(F32), lw)
            total = _mm_exact_lhs(ones_cc, lw)
            cum_ex = cum - lw
            mid = 0.5 * total
            rr = p_ref[nb, rows, 0:256]
            vv = p_ref[nb, rows, 512:768]
            kdc = kd_scr[nb, d, rows, :]
            bbc = bb_scr[nb, d, rows, :]
            kkc = kk_scr[nb, rows, :]
            e_inv = jnp.exp(mid - cum)
            At = -kkc * jnp.exp(cum_ex - mid)
            Rt = rr * jnp.exp(cum - mid)
            Bt = bbc * e_inv
            Kt = kdc * e_inv
            Ap = -kkc * jnp.exp(cum_ex)
            Rp = rr * jnp.exp(cum)
            e_out = jnp.exp(total - cum)
            Bh = bbc * e_out
            Kh = kdc * e_out
            e_tot = jnp.exp(total[0:1, :])
            for h in range(RW_HEADS):
                hs = slice(h * HEAD_DIM, (h + 1) * HEAD_DIM)
                ch.append(dict(nb=nb, d=d, h=h, rows=rows, hs=hs, strict=strict, incl=incl,
                               AR=jnp.concatenate([At[:, hs], Rt[:, hs]], axis=0),
                               BK=jnp.concatenate([Bt[:, hs], Kt[:, hs]], axis=0), V=vv[:, hs], X1=Ap[:, hs],
                               Rp=Rp[:, hs], BKh=jnp.concatenate([Bh[:, hs], Kh[:, hs]], axis=0),
                               e_tot=e_tot[:, hs]))
        for c in ch:
            c['G'] = _mm(c['AR'], c['BK'], _NT)
        for c in ch:
            fwd = c['d'] == 0
            c['P'] = jnp.where(c['strict'], c['G'][:C, :C], 0.0)
            c['A_ak0'] = jnp.where(((t2 > s2) if fwd else (t2 < s2)) & right, c['G'][:C], 0.0)
            c['A_r'] = jnp.where((t2 >= s2) if fwd else (t2 <= s2), c['G'][C:], 0.0)
            c['VV'] = jnp.concatenate([c['V'], c['V']], axis=0)
        for c in ch:
            c['X2'] = _mm(c['A_ak0'], c['VV'])
        for lvl in range(6):
            for c in ch:
                if lvl < 5:
                    c['PZ'] = _mm(c['P'], jnp.concatenate([c['P'], c['X1'], c['X2']], axis=1))
                else:
                    c['PZ'] = _mm(c['P'], jnp.concatenate([c['X1'], c['X2']], axis=1))
            for c in ch:
                PZ = c['PZ']
                if lvl < 5:
                    c['P'] = PZ[:, :C]
                    c['X1'] = c['X1'] + PZ[:, C:2 * C]
                    c['X2'] = c['X2'] + PZ[:, 2 * C:]
                else:
                    c['X1'] = c['X1'] + PZ[:, :C]
                    c['X2'] = c['X2'] + PZ[:, C:]
        for c in ch:
            c['S0'] = s_scr[c['nb'], c['d'], c['h']]
            c['UY'] = _mm(jnp.concatenate([c['X1'], c['Rp']], axis=0), c['S0'], _NT)
        for c in ch:
            c['U'] = c['UY'][:C] + c['X2']
        for c in ch:
            UV = jnp.concatenate([c['U'], c['V']], axis=0)
            c['Y'] = c['UY'][C:] + _mm(c['A_r'], UV)
            c['S1'] = c['S0'] * c['e_tot'] + _mm(UV, c['BKh'], _TN)
        for c in ch:
            s_scr[c['nb'], c['d'], c['h']] = c['S1']
            y_scr[c['nb'], c['d'], c['rows'], c['hs']] = c['Y']
        return carry

    lax.fori_loop(0, n_chunks, chunk_body, 0)

    for nb in range(NB):
        r = p_ref[nb, :, 0:256]
        v = p_ref[nb, :, 512:768]
        bonus = seg(r * (kd_scr[nb, 0] + kd_scr[nb, 1]) * rk_ref[...]) * v
        g = _mm(_sigmoid(p_ref[nb, :, 1024:1152]), g2_ref[...])
        y = y_scr[nb, 0] + y_scr[nb, 1]
        mu = seg(y) * (1.0 / HEAD_DIM)
        yc = y - mu
        var = seg(yc * yc) * (1.0 / HEAD_DIM)
        yn = yc * lax.rsqrt(var + RW_GN_EPS)
        out_ref[nb] = (yn * gng_ref[...] + gnb_ref[...] + bonus) * g
    st_ref[:, 0] = s_scr[...]


RW_ROWS = 1024


def _state_spec(nb, layer):
    return pl.BlockSpec((nb, 1, 2, RW_HEADS, HEAD_DIM, HEAD_DIM), lambda b: (b, layer, 0, 0, 0, 0))


def rwkv7_mixer_pallas(p_rw, s0, l_in, prm, st_all=None, l_out=0):
    B, T, W = p_rw.shape
    NB = max(2, RW_ROWS // T)
    row = lambda a: a.reshape(1, RW_WIDTH)
    full = lambda shape: pl.BlockSpec(shape, lambda b: (0,) * len(shape))
    st_shape = (B, 1, 2, RW_HEADS, HEAD_DIM, HEAD_DIM) if st_all is None else st_all.shape
    return pl.pallas_call(
        partial(_rwkv_kernel, T=T, NB=NB),
        grid=(B // NB,),
        in_specs=[pl.BlockSpec((NB, T, W), lambda b: (b, 0, 0)), _state_spec(NB, l_in),
                  full((2, RW_WIDTH)), full((2, 64, RW_WIDTH)), full((2, RW_WIDTH)), full((2, 64, RW_WIDTH)),
                  full((128, RW_WIDTH)), full((1, RW_WIDTH)), full((1, RW_WIDTH)), full((1, RW_WIDTH)),
                  full((1, RW_WIDTH)), full((1, RW_WIDTH))]
        + ([] if st_all is None else [pl.BlockSpec(memory_space=pl.ANY)]),
        out_specs=[pl.BlockSpec((NB, T, RW_WIDTH), lambda b: (b, 0, 0)), _state_spec(NB, l_out)],
        out_shape=[jax.ShapeDtypeStruct((B, T, RW_WIDTH), F32), jax.ShapeDtypeStruct(st_shape, F32)],
        input_output_aliases={} if st_all is None else {12: 1},
        scratch_shapes=[pltpu.VMEM((NB, 2, T, RW_WIDTH), F32)] * 4
        + [pltpu.VMEM((NB, T, RW_WIDTH), F32), pltpu.VMEM((NB, 2, RW_HEADS, HEAD_DIM, HEAD_DIM), F32)],
        compiler_params=pltpu.CompilerParams(dimension_semantics=("arbitrary",), vmem_limit_bytes=VMEM_LIMIT_RWKV),
        name="rwkv7_mixer",
    )(p_rw, s0, prm['rw_w0'], prm['rw_w2'], prm['rw_a0'], prm['rw_a2'], prm['rw_g2'], row(prm['rw_kk']),
      row(prm['rw_ka']), row(prm['rw_rk']), row(prm['rw_gn_g']), row(prm['rw_gn_b']),
      *([] if st_all is None else [st_all]))


HG_SUB = 16
HG_ROWS = 256


def _hgrn_kernel(p_ref, s0_ref, lb_ref, ng_ref, *rest, T):
    out_ref, st_ref, lf_scr, kf_scr, o_scr, s_scr = rest[-6:]
    R, c = HG_ROWS, HG_SUB
    n_it = T // R
    x = p_ref[0]
    bd = _head_blockdiag(HG_WIDTH)
    seg = lambda t: _mm_exact_rhs(t, bd, n=2)
    for d in range(2):
        lbd = lb_ref[d:d + 1, :]
        f = lbd + (1.0 - lbd) * _sigmoid(x[:, 256 + 256 * d:512 + 256 * d])
        lf_scr[d] = jnp.log(jnp.maximum(f, HG_F_MIN))
        kf_scr[d] = 1.0 - f
        for h in range(HG_HEADS):
            s_scr[d, h] = s0_ref[0, 0, d, h].T

    ti = lax.broadcasted_iota(jnp.int32, (R, R), 0)
    si = lax.broadcasted_iota(jnp.int32, (R, R), 1)
    same_blk = (ti // c) == (si // c)
    t16 = lax.broadcasted_iota(jnp.int32, (c, 1), 0)

    def body(i, carry):
        for d in range(2):
            ci = i if d == 0 else n_it - 1 - i
            rows = pl.ds(pl.multiple_of(ci * R, R), R)
            incl = (ti >= si) if d == 0 else (ti <= si)
            lf = lf_scr[d, rows, :]
            cum = _mm_exact_lhs((incl & same_blk).astype(F32), lf)
            tot = _mm_exact_lhs(same_blk.astype(F32), lf)
            xq = p_ref[0, rows, 0:256]
            q = xq * _sigmoid(xq)
            v = p_ref[0, rows, 768:1024]
            kf = kf_scr[d, rows, :]
            Qp = q * jnp.exp(cum)
            Kh = kf * jnp.exp(tot - cum)
            e_tot = jnp.exp(tot)
            blocks = range(R // c) if d == 0 else range(R // c - 1, -1, -1)
            ST = [s_scr[d, h] for h in range(HG_HEADS)]
            o_parts = [None] * (R // c)
            for j in blocks:
                rs = slice(j * c, (j + 1) * c)
                cb, qb, kb, vb = cum[rs], q[rs], kf[rs], v[rs]
                half = c // 2
                spans = []
                for s in range(c):
                    if d == 0:
                        spans.append((half, c) if s >= half else (0, c))
                    else:
                        spans.append((0, half) if s < half else (0, c))
                prods = []
                for s, (lo, hi) in enumerate(spans):
                    e = jnp.exp(jnp.minimum(cb[lo:hi] - cb[s:s + 1, :], 0.0))
                    prods.append(qb[lo:hi] * (kb[s:s + 1, :] * e))
                att = _mm_exact_rhs(jnp.concatenate(prods, axis=0), bd, n=1)
                o_half = [jnp.zeros((half, HG_WIDTH), F32), jnp.zeros((half, HG_WIDTH), F32)]
                off = 0
                for s, (lo, hi) in enumerate(spans):
                    keep = (t16[lo:hi] >= s) if d == 0 else (t16[lo:hi] <= s)
                    term = jnp.where(keep, att[off:off + hi - lo], 0.0) * vb[s:s + 1, :]
                    off += hi - lo
                    for p in range(2):
                        a, b = max(lo, p * half), min(hi, (p + 1) * half)
                        if a < b:
                            o_half[p] = o_half[p] + term[a - lo:b - lo]
                o_blk = jnp.concatenate(o_half, axis=0)
                o_heads = []
                for h in range(HG_HEADS):
                    hs = slice(h * HEAD_DIM, (h + 1) * HEAD_DIM)
                    o_heads.append(_mm(Qp[rs, hs], ST[h], _NT))
                    ST[h] = ST[h] * e_tot[j * c:j * c + 1, hs] + _mm(vb[:, hs], Kh[rs, hs], _TN)
                o_parts[j] = o_blk + jnp.concatenate(o_heads, axis=1)
            for h in range(HG_HEADS):
                s_scr[d, h] = ST[h]
            o_scr[d, rows, :] = jnp.concatenate(o_parts, axis=0)
        return carry

    lax.fori_loop(0, n_it, body, 0)

    o = o_scr[0] + o_scr[1]
    o = o * lax.rsqrt(seg(o * o) * (1.0 / HEAD_DIM) + NORM_EPS) * ng_ref[...]
    gg = x[:, 1024:1280]
    out_ref[0] = o * (gg * _sigmoid(gg))
    for d in range(2):
        for h in range(HG_HEADS):
            st_ref[0, 0, d, h] = s_scr[d, h].T


def hgrn2_mixer_pallas(p_hg, s0, l_in, lb, norm_g, st_all=None, l_out=0):
    B, T, W = p_hg.shape
    full = lambda shape: pl.BlockSpec(shape, lambda b: (0,) * len(shape))
    st_shape = (B, 1, 2, HG_HEADS, HEAD_DIM, HEAD_DIM) if st_all is None else st_all.shape
    return pl.pallas_call(
        partial(_hgrn_kernel, T=T),
        grid=(B,),
        in_specs=[pl.BlockSpec((1, T, W), lambda b: (b, 0, 0)), _state_spec(1, l_in), full((2, HG_WIDTH)),
                  full((1, HG_WIDTH))] + ([] if st_all is None else [pl.BlockSpec(memory_space=pl.ANY)]),
        out_specs=[pl.BlockSpec((1, T, HG_WIDTH), lambda b: (b, 0, 0)), _state_spec(1, l_out)],
        out_shape=[jax.ShapeDtypeStruct((B, T, HG_WIDTH), F32), jax.ShapeDtypeStruct(st_shape, F32)],
        input_output_aliases={} if st_all is None else {4: 1},
        scratch_shapes=[pltpu.VMEM((2, T, HG_WIDTH), F32)] * 3
        + [pltpu.VMEM((2, HG_HEADS, HEAD_DIM, HEAD_DIM), F32)],
        compiler_params=pltpu.CompilerParams(dimension_semantics=("arbitrary",), vmem_limit_bytes=VMEM_LIMIT),
        name="hgrn2_mixer",
    )(p_hg, s0, lb, jnp.tile(norm_g.reshape(1, HEAD_DIM), (1, HG_HEADS)), *([] if st_all is None else [st_all]))


ATT_REP = ATT_HEADS // ATT_KV_HEADS
ATT_QROWS = 128


def _swap_pairs(x):
    w = x.shape[-1]
    lane = lax.broadcasted_iota(jnp.int32, x.shape, x.ndim - 1)
    return jnp.where(lane % 2 == 0, pltpu.roll(x, w - 1, x.ndim - 1), pltpu.roll(x, 1, x.ndim - 1))


def _att_kernel(*refs, T, past, rope):
    if rope:
        p_ref, qg_ref, kg_ref, cos_ref, sin_ref, ck_ref, cv_ref, out_ref, k_scr, v_scr, q_scr = refs
    else:
        p_ref, qg_ref, kg_ref, _, _, out_ref, kh_ref, vh_ref, k_scr, v_scr, q_scr = refs
    x = p_ref[0]
    q = x[:, 0:ATT_WIDTH]
    k = x[:, ATT_WIDTH:ATT_WIDTH + KV_WIDTH]
    v = x[:, ATT_WIDTH + KV_WIDTH:ATT_WIDTH + 2 * KV_WIDTH]
    inv_d = 1.0 / HEAD_DIM
    q = q * lax.rsqrt(_mm_exact_rhs(q * q, _head_blockdiag(ATT_WIDTH), n=2) * inv_d + NORM_EPS) * qg_ref[...]
    k = k * lax.rsqrt(_mm_exact_rhs(k * k, _head_blockdiag(KV_WIDTH), n=2) * inv_d + NORM_EPS) * kg_ref[...]
    if rope:
        cos, sin = cos_ref[...], sin_ref[...]
        rep = ATT_WIDTH // KV_WIDTH
        q = q * jnp.concatenate([cos] * rep, axis=1) + _swap_pairs(q) * jnp.concatenate([sin] * rep, axis=1)
        k = k * cos + _swap_pairs(k) * sin
    q_scr[...] = (q * (1.0 / math.sqrt(HEAD_DIM))).astype(BF16)
    for g in range(ATT_KV_HEADS):
        gs = slice(g * HEAD_DIM, (g + 1) * HEAD_DIM)
        if rope:
            k_scr[g, 0:past, :] = ck_ref[0, 0, g].astype(BF16)
            v_scr[g, 0:past, :] = cv_ref[0, 0, g].astype(BF16)
        else:
            kh_ref[0, 0, g] = k[:, gs].T
            vh_ref[0, 0, g] = v[:, gs].T
        k_scr[g, past:past + T, :] = k[:, gs].astype(BF16)
        v_scr[g, past:past + T, :] = v[:, gs].astype(BF16)
    QR = ATT_QROWS

    def q_block(qb, carry):
        rows = pl.ds(pl.multiple_of(qb * QR, QR), QR)
        qblk = q_scr[rows, :]
        for g in range(ATT_KV_HEADS):
            qs = jnp.concatenate([qblk[:, (g * ATT_REP + r) * HEAD_DIM:(g * ATT_REP + r + 1) * HEAD_DIM]
                                  for r in range(ATT_REP)], axis=0)
            s = lax.dot_general(qs, k_scr[g], _NT, preferred_element_type=F32)
            e = jnp.exp(s - jnp.max(s, axis=-1, keepdims=True))
            l = jnp.sum(e, axis=-1, keepdims=True)
            o = lax.dot_general(e.astype(BF16), v_scr[g], _NN, preferred_element_type=F32) / l
            for r in range(ATT_REP):
                h = g * ATT_REP + r
                out_ref[0, rows, h * HEAD_DIM:(h + 1) * HEAD_DIM] = o[r * QR:(r + 1) * QR]
        return carry

    lax.fori_loop(0, T // QR, q_block, 0, unroll=4 if T // QR >= 4 else 2)


def rope_tables(T):
    rows = T // GRID_W
    row = jnp.repeat(jnp.arange(rows, dtype=F32), GRID_W)
    col = jnp.tile(jnp.arange(GRID_W, dtype=F32), rows)
    n_freq = HEAD_DIM // 4
    inv = ROPE_THETA ** (-jnp.arange(n_freq, dtype=F32) / n_freq)
    ang = jnp.concatenate([row[:, None] * inv, col[:, None] * inv], axis=-1)
    cos = jnp.repeat(jnp.cos(ang), 2, axis=-1)
    sin = jnp.stack([-jnp.sin(ang), jnp.sin(ang)], axis=-1).reshape(T, HEAD_DIM)
    return jnp.tile(cos, (1, ATT_KV_HEADS)), jnp.tile(sin, (1, ATT_KV_HEADS))


def attention_pallas(p_att, qnorm_g, knorm_g, l, cache=None, kv_all=None):
    B, T, W = p_att.shape
    rope = cache is not None
    past = cache[0].shape[3] if rope else 0
    full = lambda shape: pl.BlockSpec(shape, lambda b: (0,) * len(shape))
    qg = jnp.tile(qnorm_g.reshape(1, HEAD_DIM), (1, ATT_HEADS))
    kg = jnp.tile(knorm_g.reshape(1, HEAD_DIM), (1, ATT_KV_HEADS))
    in_specs = [pl.BlockSpec((1, T, W), lambda b: (b, 0, 0)), full((1, ATT_WIDTH)), full((1, KV_WIDTH))]
    args = [p_att, qg, kg]
    out_specs = [pl.BlockSpec((1, T, ATT_WIDTH), lambda b: (b, 0, 0))]
    out_shape = [jax.ShapeDtypeStruct((B, T, ATT_WIDTH), F32)]
    if rope:
        cos, sin = rope_tables(T)
        kv_spec = pl.BlockSpec((1, 1, ATT_KV_HEADS, past, HEAD_DIM), lambda b: (b, l, 0, 0, 0))
        in_specs += [full((T, KV_WIDTH)), full((T, KV_WIDTH)), kv_spec, kv_spec]
        args += [cos, sin, cache[0], cache[1]]
        aliases = {}
    else:
        kv_spec = pl.BlockSpec((1, 1, ATT_KV_HEADS, HEAD_DIM, T), lambda b: (b, l, 0, 0, 0))
        in_specs += [pl.BlockSpec(memory_space=pl.ANY)] * 2
        args += list(kv_all)
        aliases = {3: 1, 4: 2}
        out_specs += [kv_spec, kv_spec]
        out_shape += [jax.ShapeDtypeStruct(kv_all[0].shape, F32)] * 2
    res = pl.pallas_call(
        partial(_att_kernel, T=T, past=past, rope=rope),
        grid=(B,),
        in_specs=in_specs,
        out_specs=out_specs,
        out_shape=out_shape,
        input_output_aliases=aliases,
        scratch_shapes=[pltpu.VMEM((ATT_KV_HEADS, past + T, HEAD_DIM), BF16)] * 2
        + [pltpu.VMEM((T, ATT_WIDTH), BF16)],
        compiler_params=pltpu.CompilerParams(dimension_semantics=("arbitrary",), vmem_limit_bytes=VMEM_LIMIT),
        name="attention_rope" if rope else "attention_ctx",
    )(*args)
    return res[0] if rope else tuple(res)


ROW_TILE = 512
ROW_SUB = 256
MOD_TILE = 1536
ROUTE_TILE = 256
MOE_SLOTS = 256
SC_ROWS = 64
SC_LANES = 128
P_ATT, P_RW, P_HG = ATT_WIDTH + 2 * KV_WIDTH, 3 * RW_WIDTH + 384, 5 * HG_WIDTH


def _mod_kernel(c_ref, w_ref, b_ref, o_ref):
    c = c_ref[...]
    o_ref[0] = _mm(c * _sigmoid(c), w_ref[0], passes=3) + b_ref[0]


def adaln_mod_pallas(cvec, w_mod, b_mod):
    n = 6 * D_MODEL
    return pl.pallas_call(
        _mod_kernel,
        grid=(DEPTH, n // MOD_TILE),
        in_specs=[pl.BlockSpec((8, D_MODEL), lambda l, j: (0, 0)),
                  pl.BlockSpec((1, D_MODEL, MOD_TILE), lambda l, j: (l, 0, j)),
                  pl.BlockSpec((1, 1, MOD_TILE), lambda l, j: (l, 0, j))],
        out_specs=pl.BlockSpec((1, 8, MOD_TILE), lambda l, j: (l, 0, j)),
        out_shape=jax.ShapeDtypeStruct((DEPTH, 8, n), F32),
        compiler_params=pltpu.CompilerParams(dimension_semantics=("arbitrary", "arbitrary"),
                                             vmem_limit_bytes=VMEM_LIMIT),
        name="adaln_mod",
    )(cvec, w_mod, b_mod.reshape(DEPTH, 1, n))


def _rms(x):
    return x * lax.rsqrt(jnp.mean(x * x, axis=-1, keepdims=True) + NORM_EPS)


HALF = D_MODEL // 2
U32 = jnp.uint32


def _pack_rows(x):
    bits = lax.bitcast_convert_type(x.astype(BF16).astype(F32), U32)
    return (bits[:, :HALF] >> 16) | bits[:, HALF:]


def _unpack_rows(w):
    lo = lax.bitcast_convert_type(w << 16, F32)
    hi = lax.bitcast_convert_type(w & jnp.uint32(0xFFFF0000), F32)
    return lo, hi


def _row_pieces():
    return [slice(r, r + ROW_SUB) for r in range(0, ROW_TILE, ROW_SUB)]


def _moe_residual(x_ref, g_ref, g4_ref, pm_ref, rows):
    y = None
    for k in range(TOP_K):
        t = g4_ref[rows, k:k + 1] * jnp.concatenate(_unpack_rows(g_ref[k, rows, :]), axis=1)
        y = t if y is None else y + t
    return x_ref[rows, :] + pm_ref[0, 5:6, :] * y


def _in_kernel(*refs, has_res):
    if has_res:
        x_ref, gth_ref, g4_ref, pm_ref, m_ref, g_ref, w_ref, xo_ref, pa_ref, pr_ref, ph_ref = refs
    else:
        x_ref, m_ref, g_ref, w_ref, pa_ref, pr_ref, ph_ref = refs
    for rows in _row_pieces():
        if has_res:
            x = _moe_residual(x_ref, gth_ref, g4_ref, pm_ref, rows)
            xo_ref[rows, :] = x
        else:
            x = x_ref[rows, :]
        h = _rms(x) * g_ref[...] * (1.0 + m_ref[0, 1:2, :]) + m_ref[0, 0:1, :]
        proj = lax.dot_general(h.astype(BF16), w_ref[0].astype(BF16), _NN, preferred_element_type=F32)
        pa_ref[rows, :] = proj[:, 0:P_ATT]
        pr_ref[rows, :] = proj[:, P_ATT:P_ATT + P_RW]
        ph_ref[rows, :] = proj[:, P_ATT + P_RW:]


def _res_specs(row0):
    t0 = row0 // ROW_TILE
    return [pl.BlockSpec((TOP_K, ROW_TILE, HALF), lambda i: (0, i, 0)),
            pl.BlockSpec((ROW_TILE, TOP_K), lambda i: (i + t0, 0))]


def in_proj_pallas(x, mod, norm_g, w_in, l, rows_per_mod, res=None):
    R = x.shape[0]
    tpm = rows_per_mod // ROW_TILE
    rt = lambda w: pl.BlockSpec((ROW_TILE, w), lambda i: (i, 0))
    ms = pl.BlockSpec((1, 6, D_MODEL), lambda i: (i // tpm, 0, 0))
    full = lambda shape, **kw: pl.BlockSpec(shape, lambda i: (0,) * len(shape), **kw)
    in_specs = [rt(D_MODEL)] + (_res_specs(res[2]) + [ms] if res else []) + [
        ms, full((1, D_MODEL)),
        pl.BlockSpec((1, D_MODEL, w_in.shape[2]), lambda i: (l, 0, 0), pipeline_mode=pl.Buffered(1))]
    args = [x] + ([res[0], res[1], res[3]] if res else []) + [mod, norm_g.reshape(1, D_MODEL), w_in]
    widths = ([D_MODEL] if res else []) + [P_ATT, P_RW, P_HG]
    return pl.pallas_call(
        partial(_in_kernel, has_res=res is not None),
        grid=(R // ROW_TILE,),
        in_specs=in_specs,
        out_specs=[rt(w) for w in widths],
        out_shape=[jax.ShapeDtypeStruct((R, w), F32) for w in widths],
        compiler_params=pltpu.CompilerParams(dimension_semantics=("arbitrary",), vmem_limit_bytes=VMEM_LIMIT),
        name="in_proj",
    )(*args)


def _out_kernel(att_ref, rw_ref, hg_ref, x_ref, m_ref, g_ref, w_ref, rw_w_ref, rb_ref, *rest):
    xo_ref, h_ref, lg_ref = rest[-3:]
    d = lambda a, lo, hi: lax.dot_general(a.astype(BF16), w_ref[0, lo:hi, :].astype(BF16), _NN,
                                          preferred_element_type=F32)
    for rows in _row_pieces():
        mixo = (d(att_ref[rows, :], 0, ATT_WIDTH) + d(rw_ref[rows, :], ATT_WIDTH, ATT_WIDTH + RW_WIDTH)
                + d(hg_ref[rows, :], ATT_WIDTH + RW_WIDTH, ATT_WIDTH + RW_WIDTH + HG_WIDTH))
        x = x_ref[rows, :] + m_ref[0, 2:3, :] * mixo
        xo_ref[rows, :] = x
        h = _rms(x) * g_ref[...] * (1.0 + m_ref[0, 4:5, :]) + m_ref[0, 3:4, :]
        h_ref[rows, :] = _pack_rows(h)
        lg_ref[rows, :] = _mm(h, rw_w_ref[...], passes=3) + rb_ref[...]


def out_proj_pallas(att, rw, hg, x, mod, norm_g, w_out, l, router_w, router_b, rows_per_mod, n_all, row0, joint=None):
    R = x.shape[0]
    tpm = rows_per_mod // ROW_TILE
    t0 = row0 // ROW_TILE
    rt = lambda w: pl.BlockSpec((ROW_TILE, w), lambda i: (i, 0))
    jt = lambda w: pl.BlockSpec((ROW_TILE, w), lambda i: (i + t0, 0))
    full = lambda shape: pl.BlockSpec(shape, lambda i: (0,) * len(shape))
    in_specs = [rt(ATT_WIDTH), rt(RW_WIDTH), rt(HG_WIDTH), rt(D_MODEL),
                pl.BlockSpec((1, 6, D_MODEL), lambda i: (i // tpm, 0, 0)), full((1, D_MODEL)),
                pl.BlockSpec((1, D_MODEL, D_MODEL), lambda i: (l, 0, 0)), full((D_MODEL, N_EXPERTS)),
                full((1, N_EXPERTS))]
    args = [att, rw, hg, x, mod, norm_g.reshape(1, D_MODEL), w_out, router_w, router_b.reshape(1, N_EXPERTS)]
    aliases = {}
    if joint is not None:
        in_specs += [pl.BlockSpec(memory_space=pl.ANY)] * 2
        aliases = {len(args): 1, len(args) + 1: 2}
        args += list(joint)
    return pl.pallas_call(
        _out_kernel,
        grid=(R // ROW_TILE,),
        in_specs=in_specs,
        out_specs=[rt(D_MODEL), jt(HALF), jt(N_EXPERTS)],
        out_shape=[jax.ShapeDtypeStruct((R, D_MODEL), F32), jax.ShapeDtypeStruct((n_all, HALF), U32),
                   jax.ShapeDtypeStruct((n_all, N_EXPERTS), F32)],
        input_output_aliases=aliases,
        compiler_params=pltpu.CompilerParams(dimension_semantics=("arbitrary",), vmem_limit_bytes=VMEM_LIMIT),
        name="out_proj",
    )(*args)


def _final_kernel(x_ref, gth_ref, g4_ref, m_ref, g_ref, o_ref):
    for rows in _row_pieces():
        o_ref[rows, :] = _rms(_moe_residual(x_ref, gth_ref, g4_ref, m_ref, rows)) * g_ref[...]


def final_norm_pallas(x, gathered, gate4, row0, mod, norm_g, rows_per_mod):
    R = x.shape[0]
    tpm = rows_per_mod // ROW_TILE
    rt = pl.BlockSpec((ROW_TILE, D_MODEL), lambda i: (i, 0))
    return pl.pallas_call(
        _final_kernel,
        grid=(R // ROW_TILE,),
        in_specs=[rt] + _res_specs(row0) + [pl.BlockSpec((1, 6, D_MODEL), lambda i: (i // tpm, 0, 0)),
                                            pl.BlockSpec((1, D_MODEL), lambda i: (0, 0))],
        out_specs=rt,
        out_shape=jax.ShapeDtypeStruct((R, D_MODEL), F32),
        name="final_norm",
    )(x, gathered, gate4, mod, norm_g.reshape(1, D_MODEL))


def _moe_max_blocks(n_tok):
    return (n_tok * TOP_K + N_EXPERTS * (MOE_SLOTS - 1)) // MOE_SLOTS


def _route_kernel(lg_ref, dest_ref, gate4_ref, blk_ref, rank_scr, gate_scr, *, n_tok):
    Rt, E = ROUTE_TILE, N_EXPERTS
    n_tiles = n_tok // Rt
    ti = lax.broadcasted_iota(jnp.int32, (Rt, Rt), 0)
    si = lax.broadcasted_iota(jnp.int32, (Rt, Rt), 1)
    before_t = (ti < si).astype(BF16)
    eye_t = (ti == si).astype(BF16)
    ei = lax.broadcasted_iota(jnp.int32, (E, E), 0)
    ej = lax.broadcasted_iota(jnp.int32, (E, E), 1)
    before_e = (ei > ej).astype(BF16)
    sub = lax.broadcasted_iota(jnp.int32, (E, Rt), 0)
    d = lambda a, b, dims: lax.dot_general(a, b, dims, preferred_element_type=F32)

    def tile_members(it, off):
        rows = pl.ds(pl.multiple_of(it * Rt, Rt), Rt)
        l3, l2, l1 = reversed(_split(lg_ref[rows, :], 3))
        lgT = d(l3, eye_t, _TN) + d(l2, eye_t, _TN) + d(l1, eye_t, _TN)
        work = lgT
        member = jnp.zeros((E, Rt), jnp.bool_)
        top = None
        for k in range(TOP_K):
            m = jnp.max(work, axis=0, keepdims=True)
            if top is None:
                top = m
            first = jnp.min(jnp.where(work == m, sub, E), axis=0, keepdims=True)
            pick = sub == first
            member = member | pick
            work = jnp.where(pick, -jnp.inf, work)
        ex = jnp.where(member, jnp.exp(lgT - top), 0.0)
        gate_scr[:, rows] = ex / jnp.sum(ex, axis=0, keepdims=True)
        mem = member.astype(BF16)
        rank = d(mem, before_t, _NN) + off
        rank_scr[:, rows] = jnp.where(member, rank, -1.0)
        return off + jnp.sum(mem.astype(F32), axis=1, keepdims=True)

    count = lax.fori_loop(0, n_tiles, tile_members, jnp.zeros((E, 1), F32), unroll=4)
    nblk = jnp.floor((count + (MOE_SLOTS - 1)) * (1.0 / MOE_SLOTS))
    bstart = d(before_e, jnp.broadcast_to(nblk, (E, 128)).astype(BF16), _NN)[:, 0:1]
    bend = bstart + nblk
    pstart = bstart * MOE_SLOTS

    def tile_slots(it, carry):
        rows = pl.ds(pl.multiple_of(it * Rt, Rt), Rt)
        rank = rank_scr[:, rows]
        gate = gate_scr[:, rows]
        member = rank >= 0.0
        kidx = d(before_e, member.astype(BF16), _NN)
        slot = pstart + rank
        grows = []
        for k in range(TOP_K):
            sel = member & (kidx == k)
            dest_ref[k:k + 1, rows] = jnp.sum(jnp.where(sel, slot, 0.0), axis=0, keepdims=True).astype(jnp.int32)
            grows.append(jnp.sum(jnp.where(sel, gate, 0.0), axis=0, keepdims=True))
        g4t = jnp.concatenate(grows + [jnp.zeros((128 - TOP_K, Rt), F32)], axis=0)
        g3, g2, g1 = reversed(_split(g4t, 3))
        g4 = d(eye_t, g3, _NT) + d(eye_t, g2, _NT) + d(eye_t, g1, _NT)
        gate4_ref[rows, :] = g4[:, 0:TOP_K]
        return carry

    lax.fori_loop(0, n_tiles, tile_slots, 0, unroll=4)
    nb = blk_ref.shape[1]
    bi = lax.broadcasted_iota(jnp.int32, (E, nb), 1).astype(F32)
    owner = jnp.sum((bend <= bi).astype(F32), axis=0, keepdims=True)
    blk_ref[0:1, :] = jnp.minimum(owner, E - 1.0).astype(jnp.int32)
    blk_ref[1:2, :] = jnp.broadcast_to(jnp.sum(nblk, axis=0, keepdims=True), (1, nb)).astype(jnp.int32)
    blk_ref[2:8, :] = jnp.zeros((6, nb), jnp.int32)


def moe_route_pallas(logits):
    n_tok = logits.shape[0]
    nb = -(-_moe_max_blocks(n_tok) // 128) * 128
    return pl.pallas_call(
        partial(_route_kernel, n_tok=n_tok),
        out_shape=[jax.ShapeDtypeStruct((TOP_K, n_tok), jnp.int32),
                   jax.ShapeDtypeStruct((n_tok, TOP_K), F32),
                   jax.ShapeDtypeStruct((8, nb), jnp.int32)],
        scratch_shapes=[pltpu.VMEM((N_EXPERTS, n_tok), F32)] * 2,
        name="moe_route",
    )(logits)


def _moe_block_kernel(be_ref, nu_ref, first_ref, par_ref, nxt_ref, xb_ref, wgu_hbm, bgu_ref, wdn_hbm, bdn_ref, yb_ref,
                      wgu_buf, wdn_buf, sem, *, l):
    i = pl.program_id(0)

    def weight_copies(e, slot):
        return (pltpu.make_async_copy(wgu_hbm.at[l, e], wgu_buf.at[slot], sem.at[0, slot]),
                pltpu.make_async_copy(wdn_hbm.at[l, e], wdn_buf.at[slot], sem.at[1, slot]))

    @pl.when(i < nu_ref[0])
    def _():
        slot = par_ref[i]

        @pl.when(first_ref[i] == 1)
        def _():
            @pl.when(i == 0)
            def _():
                for cp in weight_copies(be_ref[0], slot):
                    cp.start()

            for cp in weight_copies(be_ref[i], slot):
                cp.wait()

            @pl.when(nxt_ref[i] >= 0)
            def _():
                for cp in weight_copies(nxt_ref[i], 1 - slot):
                    cp.start()

        dot = lambda a, w: lax.dot_general(a.astype(BF16), w.astype(BF16), _NN, preferred_element_type=F32)
        x_lo, x_hi = _unpack_rows(xb_ref[...])
        gu = dot(x_lo, wgu_buf[slot, 0:HALF]) + dot(x_hi, wgu_buf[slot, HALF:D_MODEL]) + bgu_ref[0, 0]
        glu = jnp.minimum(gu[:, :EXPERT_FF], SWIGLU_LIMIT)
        lin = jnp.clip(gu[:, EXPERT_FF:], -SWIGLU_LIMIT, SWIGLU_LIMIT)
        act = glu * _sigmoid(SWIGLU_ALPHA * glu) * (lin + 1.0)
        yb_ref[...] = _pack_rows(dot(act, wdn_buf[slot]) + bdn_ref[0, 0])


def _expert_runs(block_e, n_used):
    n = block_e.shape[0]
    idx = jnp.arange(n, dtype=jnp.int32)
    valid = idx < n_used[0]
    first = valid & ((idx == 0) | (block_e != jnp.roll(block_e, 1)))
    par = (jnp.cumsum(first.astype(jnp.int32)) - 1) % 2
    start = jnp.where(first, idx, n)
    nxt_start = lax.cummin(jnp.concatenate([start[1:], jnp.full((1,), n, jnp.int32)]), reverse=True)
    nxt = jnp.where(nxt_start < n, block_e[jnp.minimum(nxt_start, n - 1)], -1)
    return first.astype(jnp.int32), par.astype(jnp.int32), nxt.astype(jnp.int32)


def moe_blocks_pallas(xb, block_e, n_used, l, w_gu, b_gu, w_down, b_down):
    n_blocks = xb.shape[0] // MOE_SLOTS
    first, par, nxt = _expert_runs(block_e, n_used)
    blk = lambda i, be, nu, *_: (jnp.minimum(i, nu[0] - 1), 0)
    bsel = lambda i, be, nu, *_: (l, be[jnp.minimum(i, nu[0] - 1)], 0, 0)
    grid_spec = pltpu.PrefetchScalarGridSpec(
        num_scalar_prefetch=5,
        grid=(n_blocks,),
        in_specs=[pl.BlockSpec((MOE_SLOTS, HALF), blk),
                  pl.BlockSpec(memory_space=pl.ANY),
                  pl.BlockSpec((1, 1, 1, 2 * EXPERT_FF), bsel),
                  pl.BlockSpec(memory_space=pl.ANY),
                  pl.BlockSpec((1, 1, 1, D_MODEL), bsel)],
        out_specs=pl.BlockSpec((MOE_SLOTS, HALF), blk),
        scratch_shapes=[pltpu.VMEM((2, D_MODEL, 2 * EXPERT_FF), F32), pltpu.VMEM((2, EXPERT_FF, D_MODEL), F32),
                        pltpu.SemaphoreType.DMA((2, 2))],
    )
    return pl.pallas_call(
        partial(_moe_block_kernel, l=l),
        grid_spec=grid_spec,
        out_shape=jax.ShapeDtypeStruct(xb.shape, U32),
        compiler_params=pltpu.CompilerParams(dimension_semantics=("arbitrary",), vmem_limit_bytes=VMEM_LIMIT),
        name="moe_blocks",
    )(block_e, n_used, first, par, nxt, xb, w_gu, b_gu.reshape(DEPTH, N_EXPERTS, 1, 2 * EXPERT_FF), w_down,
      b_down.reshape(DEPTH, N_EXPERTS, 1, D_MODEL))


def _sc_mesh():
    return plsc.VectorSubcoreMesh(core_axis_name="c", subcore_axis_name="s")


def _sc_index_rows(idx):
    return jnp.pad(idx.reshape(-1, SC_ROWS), ((0, 0), (0, SC_LANES - SC_ROWS)))


def sc_dispatch(h, dest, n_rows):
    n_tok, d = h.shape
    idx = [_sc_index_rows(dest[k]) for k in range(TOP_K)]

    @pl.kernel(out_type=jax.ShapeDtypeStruct((n_rows, d), h.dtype), mesh=_sc_mesh(), scratch_types=[])
    def kern(h_hbm, i0, i1, i2, i3, o_hbm):
        def body(x_vmem, *i_vmem):
            for iv in i_vmem:
                pltpu.sync_copy(x_vmem, o_hbm.at[iv.at[0, pl.ds(0, SC_ROWS)]])

        pltpu.emit_pipeline(
            body,
            grid=(n_tok // SC_ROWS,),
            in_specs=[pl.BlockSpec((SC_ROWS, d), lambda i: (i, 0))]
            + [pl.BlockSpec((1, SC_LANES), lambda i: (i, 0))] * TOP_K,
            out_specs=[],
            core_axis_name=("c", "s"),
            dimension_semantics=(pltpu.PARALLEL,),
        )(h_hbm, i0, i1, i2, i3)

    return kern(h, *idx)


def sc_combine_gather(yb, dest):
    n_tok = dest.shape[1]
    d = yb.shape[1]
    idx = _sc_index_rows(dest.reshape(TOP_K * n_tok))

    @pl.kernel(out_type=jax.ShapeDtypeStruct((TOP_K * n_tok, d), yb.dtype), mesh=_sc_mesh(), scratch_types=[])
    def kern(y_hbm, i_hbm, o_hbm):
        def body(i_vmem, o_vmem):
            pltpu.sync_copy(y_hbm.at[i_vmem.at[0, pl.ds(0, SC_ROWS)]], o_vmem)

        pltpu.emit_pipeline(
            body,
            grid=(TOP_K * n_tok // SC_ROWS,),
            in_specs=[pl.BlockSpec((1, SC_LANES), lambda i: (i, 0))],
            out_specs=[pl.BlockSpec((SC_ROWS, d), lambda i: (i, 0))],
            core_axis_name=("c", "s"),
            dimension_semantics=(pltpu.PARALLEL,),
        )(i_hbm, o_hbm)

    return kern(yb, idx).reshape(TOP_K, n_tok, d)


def hgrn_lower_bounds(hg_lb):
    sm = jax.nn.softmax(hg_lb.astype(jnp.float32), axis=0)
    return jnp.cumsum(sm, axis=0) - sm[0:1]


def kernel(x_prompt, x_sample, cache_att_k, cache_att_v, state_rwkv, state_hgrn, c, c_ctx, w_mod, b_mod, norm_mix_g, norm_ffn_g, w_in, w_out, att_qnorm_g, att_knorm_g, rw_w0, rw_w2, rw_a0, rw_a2, rw_g2, rw_kk, rw_ka, rw_rk, rw_gn_g, rw_gn_b, hg_lb, hg_norm_g, router_w, router_b, moe_w_gu, moe_b_gu, moe_w_down, moe_b_down, final_norm_g):
    BP, TP, _ = x_prompt.shape
    BS, TS, _ = x_sample.shape
    n_p, n_s = BP * TP, BS * TS
    lb_all = hgrn_lower_bounds(hg_lb)
    cvec = jnp.concatenate([c_ctx[None, :], c, jnp.zeros((8 - 1 - BS, D_MODEL), F32)], axis=0)
    mod_all = adaln_mod_pallas(cvec, w_mod, b_mod).reshape(DEPTH, 8, 6, D_MODEL)
    zeros_state = jnp.zeros((BP, 1, 2, RW_HEADS, HEAD_DIM, HEAD_DIM), F32)
    kv_all = (jnp.zeros((BP, DEPTH, ATT_KV_HEADS, HEAD_DIM, TP), F32),) * 2
    rw_states = jnp.zeros((BP, DEPTH, 2, RW_HEADS, HEAD_DIM, HEAD_DIM), F32)
    hg_states = jnp.zeros((BP, DEPTH, 2, HG_HEADS, HEAD_DIM, HEAD_DIM), F32)
    x = {'p': x_prompt.reshape(n_p, D_MODEL), 's': x_sample.reshape(n_s, D_MODEL)}
    dims = {'p': (TP, BP, n_p, 0), 's': (TS, BS, TS, n_p)}
    moe_out, mod_prev = None, None
    joint = (jnp.zeros((n_p + n_s, HALF), U32), jnp.zeros((n_p + n_s, N_EXPERTS), F32))
    for l in range(DEPTH):
        prm = dict(rw_w0=rw_w0[l], rw_w2=rw_w2[l], rw_a0=rw_a0[l], rw_a2=rw_a2[l], rw_g2=rw_g2[l],
                   rw_kk=rw_kk[l], rw_ka=rw_ka[l], rw_rk=rw_rk[l], rw_gn_g=rw_gn_g[l], rw_gn_b=rw_gn_b[l])
        mods = {'p': mod_all[l, 0:1], 's': mod_all[l, 1:1 + BS]}
        for s in ('p', 's'):
            T, B, rpm, row0 = dims[s]
            if l == 0:
                p_att, p_rw, p_hg = in_proj_pallas(x[s], mods[s], norm_mix_g[l], w_in, l, rpm)
            else:
                x[s], p_att, p_rw, p_hg = in_proj_pallas(x[s], mods[s], norm_mix_g[l], w_in, l, rpm,
                                                         res=(*moe_out[s], row0, mod_prev[s]))
            p_att, p_rw, p_hg = (t.reshape(B, T, -1) for t in (p_att, p_rw, p_hg))
            if s == 'p':
                att, *kv_all = attention_pallas(p_att, att_qnorm_g[l], att_knorm_g[l], l, kv_all=kv_all)
                rw_out, rw_states = rwkv7_mixer_pallas(p_rw, zeros_state, 0, prm, rw_states, l)
                hg_out, hg_states = hgrn2_mixer_pallas(p_hg, zeros_state, 0, lb_all[l], hg_norm_g[l], hg_states, l)
            else:
                att = attention_pallas(p_att, att_qnorm_g[l], att_knorm_g[l], l, cache=(cache_att_k, cache_att_v))
                rw_out, _ = rwkv7_mixer_pallas(p_rw, state_rwkv, l, prm)
                hg_out, _ = hgrn2_mixer_pallas(p_hg, state_hgrn, l, lb_all[l], hg_norm_g[l])
            x[s], *joint = out_proj_pallas(att.reshape(B * T, -1), rw_out.reshape(B * T, -1),
                                           hg_out.reshape(B * T, -1), x[s], mods[s], norm_ffn_g[l], w_out, l,
                                           router_w[l], router_b[l], rpm, n_p + n_s, row0, joint)
        h_all, logits_all = joint
        dest, gate4, blk = moe_route_pallas(logits_all)
        xb = sc_dispatch(h_all, dest, _moe_max_blocks(n_p + n_s) * MOE_SLOTS)
        yb = moe_blocks_pallas(xb, blk[0], blk[1, :1], l, moe_w_gu, moe_b_gu, moe_w_down, moe_b_down)
        moe_out = {'p': (sc_combine_gather(yb, dest[:, :n_p]), gate4), 's': (sc_combine_gather(yb, dest[:, n_p:]), gate4)}
        mod_prev = mods
    y_prompt = final_norm_pallas(x['p'], *moe_out['p'], 0, mod_prev['p'], final_norm_g, n_p)
    y_sample = final_norm_pallas(x['s'], *moe_out['s'], n_p, mod_prev['s'], final_norm_g, TS)
    return (y_prompt.reshape(x_prompt.shape), y_sample.reshape(x_sample.shape),
            jnp.swapaxes(kv_all[0], 3, 4), jnp.swapaxes(kv_all[1], 3, 4),
            rw_states, hg_states)
```
